```python
import jax, jax.numpy as jnp
from jax import lax
import numpy as np

D_MODEL = 1024
BATCH = 8
SEQ = 8192
DEPTH = 1

PLE_DIM = 256
HEAD_DIM = 64
FOX_HEADS = 8
SB_HEADS = 8
FOX_WIDTH = FOX_HEADS * HEAD_DIM
SB_WIDTH = SB_HEADS * HEAD_DIM
D_FF = 2816
Q_BLOCK = 128
EPS = 1e-6
FORGET_BIAS_INIT = 2.0
IN_SIZES = (FOX_WIDTH, FOX_WIDTH, FOX_WIDTH, FOX_HEADS, SB_WIDTH, SB_WIDTH, SB_WIDTH, D_MODEL, D_MODEL)
IN_WIDTH = 3 * FOX_WIDTH + FOX_HEADS + 3 * SB_WIDTH + 2 * D_MODEL

kernel_name = "hybrid_fox_stickbreak_macaron_ple"


def rms_norm(x, g):
    xf = x.astype(jnp.float32)
    y = xf * lax.rsqrt(jnp.mean(xf * xf, axis=-1, keepdims=True) + EPS)
    return (y * g.astype(jnp.float32)).astype(x.dtype)


def swiglu(h, w_gate, w_up, w_down):
    return (jax.nn.silu(h @ w_gate) * (h @ w_up)) @ w_down


def to_blocks(t):
    b, s = t.shape[0], t.shape[1]
    return jnp.moveaxis(t.reshape(b, s // Q_BLOCK, Q_BLOCK, *t.shape[2:]), 1, 0)


def from_blocks(t):
    nb, b = t.shape[0], t.shape[1]
    return jnp.moveaxis(t, 0, 1).reshape(b, nb * Q_BLOCK, -1)


def forgetting_attention(q, k, v, log_f):
    s_len = q.shape[1]
    scale = HEAD_DIM ** -0.5
    F = jnp.cumsum(log_f, axis=1)
    Fk = jnp.transpose(F, (0, 2, 1))[:, :, None, :]
    kpos = jnp.arange(s_len)
    qpos = kpos.reshape(s_len // Q_BLOCK, Q_BLOCK)

    def one_block(args):
        qi, Fi, pi = args
        logits = jnp.einsum('bqhd,bkhd->bhqk', qi, k).astype(jnp.float32) * scale
        logits = logits + jnp.transpose(Fi, (0, 2, 1))[..., None] - Fk
        mask = pi[:, None] >= kpos[None, :]
        logits = jnp.where(mask, logits, -jnp.inf)
        w = jax.nn.softmax(logits, axis=-1)
        return jnp.einsum('bhqk,bkhd->bqhd', w.astype(v.dtype), v)

    out = lax.map(one_block, (to_blocks(q), to_blocks(F), qpos))
    return from_blocks(out)


def stick_breaking_attention(q, k, v):
    s_len = q.shape[1]
    scale = HEAD_DIM ** -0.5
    kpos = jnp.arange(s_len)
    qpos = kpos.reshape(s_len // Q_BLOCK, Q_BLOCK)

    def one_block(args):
        qi, pi = args
        z = jnp.einsum('bqhd,bkhd->bhqk', qi, k).astype(jnp.float32) * scale
        mask = kpos[None, :] < pi[:, None]
        log_beta = jax.nn.log_sigmoid(z)
        log_1m = jnp.where(mask, jax.nn.log_sigmoid(-z), 0.0)
        after = lax.cumsum(log_1m, axis=3, reverse=True) - log_1m
        a = jnp.where(mask, jnp.exp(log_beta + after), 0.0)
        return jnp.einsum('bhqk,bkhd->bqhd', a.astype(v.dtype), v)

    out = lax.map(one_block, (to_blocks(q), qpos))
    return from_blocks(out)


def split_columns(t):
    outs, start = [], 0
    for size in IN_SIZES:
        outs.append(t[..., start:start + size])
        start += size
    return outs


def _fwd_setup_inputs(seed: int = 0) -> dict:
    key = jax.random.key(seed)
    ks = jax.random.split(key, 24)

    def w(k, shape, fan_in):
        return jax.random.normal(k, shape, jnp.float32) * (fan_in ** -0.5)

    def gain(k, shape):
        return 1.0 + 0.05 * jax.random.normal(k, shape, jnp.float32)

    L = DEPTH
    return {
        "x": jax.random.normal(ks[0], (BATCH, SEQ, D_MODEL), jnp.float32),
        "p": jax.random.normal(ks[1], (DEPTH, BATCH, SEQ, PLE_DIM), jnp.float32),
        "ffn1_norm": gain(ks[2], (L, D_MODEL)),
        "ffn1_w_gate": w(ks[3], (L, D_MODEL, D_FF), D_MODEL),
        "ffn1_w_up": w(ks[4], (L, D_MODEL, D_FF), D_MODEL),
        "ffn1_w_down": w(ks[5], (L, D_FF, D_MODEL), D_FF),
        "mix_norm": gain(ks[6], (L, D_MODEL)),
        "w_in": w(ks[7], (L, D_MODEL, IN_WIDTH), D_MODEL),
        "forget_bias": FORGET_BIAS_INIT + 0.1 * jax.random.normal(ks[8], (L, FOX_HEADS), jnp.float32),
        "q_norm": gain(ks[9], (L, HEAD_DIM)),
        "k_norm": gain(ks[10], (L, HEAD_DIM)),
        "w_branch_fox": w(ks[11], (L, FOX_WIDTH, D_MODEL), FOX_WIDTH),
        "w_branch_sb": w(ks[12], (L, SB_WIDTH, D_MODEL), SB_WIDTH),
        "w_out": w(ks[13], (L, D_MODEL, D_MODEL), D_MODEL),
        "ffn2_norm": gain(ks[14], (L, D_MODEL)),
        "ffn2_w_gate": w(ks[15], (L, D_MODEL, D_FF), D_MODEL),
        "ffn2_w_up": w(ks[16], (L, D_MODEL, D_FF), D_MODEL),
        "ffn2_w_down": w(ks[17], (L, D_FF, D_MODEL), D_FF),
        "ple_norm": gain(ks[18], (L, D_MODEL)),
        "w_ple_gate": w(ks[19], (L, D_MODEL, D_MODEL), D_MODEL),
        "w_ple_proj": w(ks[20], (L, PLE_DIM, D_MODEL), PLE_DIM),
    }


def _fwd_reference(x, p, ffn1_norm, ffn1_w_gate, ffn1_w_up, ffn1_w_down, mix_norm, w_in,
              forget_bias, q_norm, k_norm, w_branch_fox, w_branch_sb, w_out,
              ffn2_norm, ffn2_w_gate, ffn2_w_up, ffn2_w_down, ple_norm,
              w_ple_gate, w_ple_proj):
    b, s_len, _ = x.shape
    for i in range(DEPTH):
        x = x + 0.5 * swiglu(rms_norm(x, ffn1_norm[i]), ffn1_w_gate[i], ffn1_w_up[i], ffn1_w_down[i])

        h = rms_norm(x, mix_norm[i])
        fq, fk, fv, f_logit, sq, sk, sv, g_fox, g_sb = split_columns(h @ w_in[i])
        fq = rms_norm(fq.reshape(b, s_len, FOX_HEADS, HEAD_DIM), q_norm[i])
        fk = rms_norm(fk.reshape(b, s_len, FOX_HEADS, HEAD_DIM), k_norm[i])
        fv = fv.reshape(b, s_len, FOX_HEADS, HEAD_DIM)
        log_f = jax.nn.log_sigmoid((f_logit + forget_bias[i]).astype(jnp.float32))
        y_fox = forgetting_attention(fq, fk, fv, log_f)

        sq = sq.reshape(b, s_len, SB_HEADS, HEAD_DIM)
        sk = sk.reshape(b, s_len, SB_HEADS, HEAD_DIM)
        sv = sv.reshape(b, s_len, SB_HEADS, HEAD_DIM)
        y_sb = stick_breaking_attention(sq, sk, sv)

        merged = (jax.nn.sigmoid(g_fox) * (y_fox @ w_branch_fox[i])
                  + jax.nn.sigmoid(g_sb) * (y_sb @ w_branch_sb[i]))
        x = x + merged @ w_out[i]

        x = x + 0.5 * swiglu(rms_norm(x, ffn2_norm[i]), ffn2_w_gate[i], ffn2_w_up[i], ffn2_w_down[i])

        x = x + jax.nn.sigmoid(rms_norm(x, ple_norm[i]) @ w_ple_gate[i]) * (p[i] @ w_ple_proj[i])
    return x


import jax as _jax
import jax.numpy as _jnp

TWIN_FORMAT = 'train_step'
FWD_PARAMS = ['x', 'p', 'ffn1_norm', 'ffn1_w_gate', 'ffn1_w_up', 'ffn1_w_down', 'mix_norm', 'w_in', 'forget_bias', 'q_norm', 'k_norm', 'w_branch_fox', 'w_branch_sb', 'w_out', 'ffn2_norm', 'ffn2_w_gate', 'ffn2_w_up', 'ffn2_w_down', 'ple_norm', 'w_ple_gate', 'w_ple_proj']
TWIN_WEIGHTS = ['ffn1_norm', 'ffn1_w_gate', 'ffn1_w_up', 'ffn1_w_down', 'mix_norm', 'w_in', 'forget_bias', 'q_norm', 'k_norm', 'w_branch_fox', 'w_branch_sb', 'w_out', 'ffn2_norm', 'ffn2_w_gate', 'ffn2_w_up', 'ffn2_w_down', 'ple_norm', 'w_ple_gate', 'w_ple_proj']
TWIN_DIFF_INPUT = 'x'
TWIN_INPUTS = ['x', 'p', 'ffn1_norm', 'ffn1_w_gate', 'ffn1_w_up', 'ffn1_w_down', 'mix_norm', 'w_in', 'forget_bias', 'q_norm', 'k_norm', 'w_branch_fox', 'w_branch_sb', 'w_out', 'ffn2_norm', 'ffn2_w_gate', 'ffn2_w_up', 'ffn2_w_down', 'ple_norm', 'w_ple_gate', 'w_ple_proj', 'loss_target', 'm_ffn1_norm', 'm_ffn1_w_gate', 'm_ffn1_w_up', 'm_ffn1_w_down', 'm_mix_norm', 'm_w_in', 'm_forget_bias', 'm_q_norm', 'm_k_norm', 'm_w_branch_fox', 'm_w_branch_sb', 'm_w_out', 'm_ffn2_norm', 'm_ffn2_w_gate', 'm_ffn2_w_up', 'm_ffn2_w_down', 'm_ple_norm', 'm_w_ple_gate', 'm_w_ple_proj', 'v_ffn1_norm', 'v_ffn1_w_gate', 'v_ffn1_w_up', 'v_ffn1_w_down', 'v_mix_norm', 'v_w_in', 'v_forget_bias', 'v_q_norm', 'v_k_norm', 'v_w_branch_fox', 'v_w_branch_sb', 'v_w_out', 'v_ffn2_norm', 'v_ffn2_w_gate', 'v_ffn2_w_up', 'v_ffn2_w_down', 'v_ple_norm', 'v_w_ple_gate', 'v_w_ple_proj']
TWIN_OUTPUTS = ['loss', 'grad_x', 'grad_ffn1_norm', 'grad_ffn1_w_gate', 'grad_ffn1_w_up', 'grad_ffn1_w_down', 'grad_mix_norm', 'grad_w_in', 'grad_forget_bias', 'grad_q_norm', 'grad_k_norm', 'grad_w_branch_fox', 'grad_w_branch_sb', 'grad_w_out', 'grad_ffn2_norm', 'grad_ffn2_w_gate', 'grad_ffn2_w_up', 'grad_ffn2_w_down', 'grad_ple_norm', 'grad_w_ple_gate', 'grad_w_ple_proj', 'delta_ffn1_norm', 'delta_ffn1_w_gate', 'delta_ffn1_w_up', 'delta_ffn1_w_down', 'delta_mix_norm', 'delta_w_in', 'delta_forget_bias', 'delta_q_norm', 'delta_k_norm', 'delta_w_branch_fox', 'delta_w_branch_sb', 'delta_w_out', 'delta_ffn2_norm', 'delta_ffn2_w_gate', 'delta_ffn2_w_up', 'delta_ffn2_w_down', 'delta_ple_norm', 'delta_w_ple_gate', 'delta_w_ple_proj', 'new_m_ffn1_norm', 'new_m_ffn1_w_gate', 'new_m_ffn1_w_up', 'new_m_ffn1_w_down', 'new_m_mix_norm', 'new_m_w_in', 'new_m_forget_bias', 'new_m_q_norm', 'new_m_k_norm', 'new_m_w_branch_fox', 'new_m_w_branch_sb', 'new_m_w_out', 'new_m_ffn2_norm', 'new_m_ffn2_w_gate', 'new_m_ffn2_w_up', 'new_m_ffn2_w_down', 'new_m_ple_norm', 'new_m_w_ple_gate', 'new_m_w_ple_proj', 'new_v_ffn1_norm', 'new_v_ffn1_w_gate', 'new_v_ffn1_w_up', 'new_v_ffn1_w_down', 'new_v_mix_norm', 'new_v_w_in', 'new_v_forget_bias', 'new_v_q_norm', 'new_v_k_norm', 'new_v_w_branch_fox', 'new_v_w_branch_sb', 'new_v_w_out', 'new_v_ffn2_norm', 'new_v_ffn2_w_gate', 'new_v_ffn2_w_up', 'new_v_ffn2_w_down', 'new_v_ple_norm', 'new_v_w_ple_gate', 'new_v_w_ple_proj']
TWIN_LEAF_KINDS = {'loss': 'loss', 'grad_x': 'grad_x', 'grad_ffn1_norm': 'grad_w', 'grad_ffn1_w_gate': 'grad_w', 'grad_ffn1_w_up': 'grad_w', 'grad_ffn1_w_down': 'grad_w', 'grad_mix_norm': 'grad_w', 'grad_w_in': 'grad_w', 'grad_forget_bias': 'grad_w', 'grad_q_norm': 'grad_w', 'grad_k_norm': 'grad_w', 'grad_w_branch_fox': 'grad_w', 'grad_w_branch_sb': 'grad_w', 'grad_w_out': 'grad_w', 'grad_ffn2_norm': 'grad_w', 'grad_ffn2_w_gate': 'grad_w', 'grad_ffn2_w_up': 'grad_w', 'grad_ffn2_w_down': 'grad_w', 'grad_ple_norm': 'grad_w', 'grad_w_ple_gate': 'grad_w', 'grad_w_ple_proj': 'grad_w', 'delta_ffn1_norm': 'delta_w', 'delta_ffn1_w_gate': 'delta_w', 'delta_ffn1_w_up': 'delta_w', 'delta_ffn1_w_down': 'delta_w', 'delta_mix_norm': 'delta_w', 'delta_w_in': 'delta_w', 'delta_forget_bias': 'delta_w', 'delta_q_norm': 'delta_w', 'delta_k_norm': 'delta_w', 'delta_w_branch_fox': 'delta_w', 'delta_w_branch_sb': 'delta_w', 'delta_w_out': 'delta_w', 'delta_ffn2_norm': 'delta_w', 'delta_ffn2_w_gate': 'delta_w', 'delta_ffn2_w_up': 'delta_w', 'delta_ffn2_w_down': 'delta_w', 'delta_ple_norm': 'delta_w', 'delta_w_ple_gate': 'delta_w', 'delta_w_ple_proj': 'delta_w', 'new_m_ffn1_norm': 'new_m', 'new_m_ffn1_w_gate': 'new_m', 'new_m_ffn1_w_up': 'new_m', 'new_m_ffn1_w_down': 'new_m', 'new_m_mix_norm': 'new_m', 'new_m_w_in': 'new_m', 'new_m_forget_bias': 'new_m', 'new_m_q_norm': 'new_m', 'new_m_k_norm': 'new_m', 'new_m_w_branch_fox': 'new_m', 'new_m_w_branch_sb': 'new_m', 'new_m_w_out': 'new_m', 'new_m_ffn2_norm': 'new_m', 'new_m_ffn2_w_gate': 'new_m', 'new_m_ffn2_w_up': 'new_m', 'new_m_ffn2_w_down': 'new_m', 'new_m_ple_norm': 'new_m', 'new_m_w_ple_gate': 'new_m', 'new_m_w_ple_proj': 'new_m', 'new_v_ffn1_norm': 'new_v', 'new_v_ffn1_w_gate': 'new_v', 'new_v_ffn1_w_up': 'new_v', 'new_v_ffn1_w_down': 'new_v', 'new_v_mix_norm': 'new_v', 'new_v_w_in': 'new_v', 'new_v_forget_bias': 'new_v', 'new_v_q_norm': 'new_v', 'new_v_k_norm': 'new_v', 'new_v_w_branch_fox': 'new_v', 'new_v_w_branch_sb': 'new_v', 'new_v_w_out': 'new_v', 'new_v_ffn2_norm': 'new_v', 'new_v_ffn2_w_gate': 'new_v', 'new_v_ffn2_w_up': 'new_v', 'new_v_ffn2_w_down': 'new_v', 'new_v_ple_norm': 'new_v', 'new_v_w_ple_gate': 'new_v', 'new_v_w_ple_proj': 'new_v'}


def _forward(args):
    return _fwd_reference(*[args[k] for k in FWD_PARAMS])


def _output_shape():
    def fwd():
        inp = _fwd_setup_inputs(0)
        return _fwd_reference(*[inp[k] for k in FWD_PARAMS])
    out = _jax.eval_shape(fwd)
    return out.shape, out.dtype

N_MICROBATCH = 1
ADAM_LR = 0.001
ADAM_B1 = 0.9
ADAM_B2 = 0.999
ADAM_EPS = 1e-08
ADAM_WD = 0.01
ADAM_STEP = 10
PER_EXAMPLE_BATCH_AXIS = {'x': 0, 'p': 1, 'loss_target': 0}
SHARED_INPUTS = []
_WEIGHT_DTYPES = {'ffn1_norm': _jnp.float32, 'ffn1_w_gate': _jnp.float32, 'ffn1_w_up': _jnp.float32, 'ffn1_w_down': _jnp.float32, 'mix_norm': _jnp.float32, 'w_in': _jnp.float32, 'forget_bias': _jnp.float32, 'q_norm': _jnp.float32, 'k_norm': _jnp.float32, 'w_branch_fox': _jnp.float32, 'w_branch_sb': _jnp.float32, 'w_out': _jnp.float32, 'ffn2_norm': _jnp.float32, 'ffn2_w_gate': _jnp.float32, 'ffn2_w_up': _jnp.float32, 'ffn2_w_down': _jnp.float32, 'ple_norm': _jnp.float32, 'w_ple_gate': _jnp.float32, 'w_ple_proj': _jnp.float32}
MOMENT_SCALE = {'ffn1_norm': 1.223908e+01, 'ffn1_w_gate': 1.156305e-01, 'ffn1_w_up': 1.333181e-01, 'ffn1_w_down': 2.188472e-01, 'mix_norm': 1.515019e+01, 'w_in': 2.080343e-01, 'forget_bias': 1.246790e+02, 'q_norm': 1.929562e+01, 'k_norm': 1.931753e+01, 'w_branch_fox': 1.994304e-01, 'w_branch_sb': 3.804699e-01, 'w_out': 3.861243e-01, 'ffn2_norm': 1.250357e+01, 'ffn2_w_gate': 9.332643e-02, 'ffn2_w_up': 1.229579e-01, 'ffn2_w_down': 2.000476e-01, 'ple_norm': 1.925783e+00, 'w_ple_gate': 1.070075e-01, 'w_ple_proj': 9.021303e-01}


def _to_microbatches(a, axis):
    t = _jnp.moveaxis(a, axis, 0)
    t = t.reshape((N_MICROBATCH, t.shape[0] // N_MICROBATCH) + t.shape[1:])
    return _jnp.moveaxis(t, 1, axis + 1)


def setup_inputs(seed: int = 0) -> dict:
    inp = _fwd_setup_inputs(seed)
    key = _jax.random.fold_in(_jax.random.key(seed), 7919)
    shape, _ = _output_shape()
    out = dict(inp)
    out["loss_target"] = _jax.random.normal(_jax.random.fold_in(key, 0), shape, _jnp.float32)
    for i, name in enumerate(TWIN_WEIGHTS):
        w = inp[name].astype(_jnp.float32)
        if MOMENT_SCALE is None:
            s = _jnp.sqrt(_jnp.mean(_jnp.square(w)) + 1e-30)
        else:
            s = MOMENT_SCALE[name]
        km, kv = _jax.random.split(_jax.random.fold_in(key, i + 1))
        out[name] = w
        out["m_" + name] = s * _jax.random.normal(km, w.shape, _jnp.float32)
        out["v_" + name] = (s * s) * _jax.random.uniform(kv, w.shape, _jnp.float32, 0.5, 1.5)
    if N_MICROBATCH > 1:
        for name, axis in PER_EXAMPLE_BATCH_AXIS.items():
            out[name] = _to_microbatches(out[name], axis)
    return {'x': out['x'], 'p': out['p'], 'ffn1_norm': out['ffn1_norm'], 'ffn1_w_gate': out['ffn1_w_gate'], 'ffn1_w_up': out['ffn1_w_up'], 'ffn1_w_down': out['ffn1_w_down'], 'mix_norm': out['mix_norm'], 'w_in': out['w_in'], 'forget_bias': out['forget_bias'], 'q_norm': out['q_norm'], 'k_norm': out['k_norm'], 'w_branch_fox': out['w_branch_fox'], 'w_branch_sb': out['w_branch_sb'], 'w_out': out['w_out'], 'ffn2_norm': out['ffn2_norm'], 'ffn2_w_gate': out['ffn2_w_gate'], 'ffn2_w_up': out['ffn2_w_up'], 'ffn2_w_down': out['ffn2_w_down'], 'ple_norm': out['ple_norm'], 'w_ple_gate': out['w_ple_gate'], 'w_ple_proj': out['w_ple_proj'], 'loss_target': out['loss_target'], 'm_ffn1_norm': out['m_ffn1_norm'], 'm_ffn1_w_gate': out['m_ffn1_w_gate'], 'm_ffn1_w_up': out['m_ffn1_w_up'], 'm_ffn1_w_down': out['m_ffn1_w_down'], 'm_mix_norm': out['m_mix_norm'], 'm_w_in': out['m_w_in'], 'm_forget_bias': out['m_forget_bias'], 'm_q_norm': out['m_q_norm'], 'm_k_norm': out['m_k_norm'], 'm_w_branch_fox': out['m_w_branch_fox'], 'm_w_branch_sb': out['m_w_branch_sb'], 'm_w_out': out['m_w_out'], 'm_ffn2_norm': out['m_ffn2_norm'], 'm_ffn2_w_gate': out['m_ffn2_w_gate'], 'm_ffn2_w_up': out['m_ffn2_w_up'], 'm_ffn2_w_down': out['m_ffn2_w_down'], 'm_ple_norm': out['m_ple_norm'], 'm_w_ple_gate': out['m_w_ple_gate'], 'm_w_ple_proj': out['m_w_ple_proj'], 'v_ffn1_norm': out['v_ffn1_norm'], 'v_ffn1_w_gate': out['v_ffn1_w_gate'], 'v_ffn1_w_up': out['v_ffn1_w_up'], 'v_ffn1_w_down': out['v_ffn1_w_down'], 'v_mix_norm': out['v_mix_norm'], 'v_w_in': out['v_w_in'], 'v_forget_bias': out['v_forget_bias'], 'v_q_norm': out['v_q_norm'], 'v_k_norm': out['v_k_norm'], 'v_w_branch_fox': out['v_w_branch_fox'], 'v_w_branch_sb': out['v_w_branch_sb'], 'v_w_out': out['v_w_out'], 'v_ffn2_norm': out['v_ffn2_norm'], 'v_ffn2_w_gate': out['v_ffn2_w_gate'], 'v_ffn2_w_up': out['v_ffn2_w_up'], 'v_ffn2_w_down': out['v_ffn2_w_down'], 'v_ple_norm': out['v_ple_norm'], 'v_w_ple_gate': out['v_w_ple_gate'], 'v_w_ple_proj': out['v_w_ple_proj']}


def _loss(weights, diff, rest, loss_target):
    with _jax.named_scope("forward"):
        args = {**rest, TWIN_DIFF_INPUT: diff, **{k: w.astype(_WEIGHT_DTYPES[k]) for k, w in weights.items()}}
        y = _forward(args)
    with _jax.named_scope("loss_head"):
        err = _jnp.square(y.astype(_jnp.float32) - loss_target)
        return 0.5 * _jnp.sum(_jnp.mean(err, axis=-1)) if err.ndim else 0.5 * err


def _adamw(w, g, m, v):
    m = ADAM_B1 * m + (1.0 - ADAM_B1) * g
    v = ADAM_B2 * v + (1.0 - ADAM_B2) * _jnp.square(g)
    m_hat = m / (1.0 - ADAM_B1 ** ADAM_STEP)
    v_hat = v / (1.0 - ADAM_B2 ** ADAM_STEP)
    delta = -ADAM_LR * (m_hat / (_jnp.sqrt(v_hat) + ADAM_EPS) + ADAM_WD * w)
    return delta, m, v


def reference(x, p, ffn1_norm, ffn1_w_gate, ffn1_w_up, ffn1_w_down, mix_norm, w_in, forget_bias, q_norm, k_norm, w_branch_fox, w_branch_sb, w_out, ffn2_norm, ffn2_w_gate, ffn2_w_up, ffn2_w_down, ple_norm, w_ple_gate, w_ple_proj, loss_target, m_ffn1_norm, m_ffn1_w_gate, m_ffn1_w_up, m_ffn1_w_down, m_mix_norm, m_w_in, m_forget_bias, m_q_norm, m_k_norm, m_w_branch_fox, m_w_branch_sb, m_w_out, m_ffn2_norm, m_ffn2_w_gate, m_ffn2_w_up, m_ffn2_w_down, m_ple_norm, m_w_ple_gate, m_w_ple_proj, v_ffn1_norm, v_ffn1_w_gate, v_ffn1_w_up, v_ffn1_w_down, v_mix_norm, v_w_in, v_forget_bias, v_q_norm, v_k_norm, v_w_branch_fox, v_w_branch_sb, v_w_out, v_ffn2_norm, v_ffn2_w_gate, v_ffn2_w_up, v_ffn2_w_down, v_ple_norm, v_w_ple_gate, v_w_ple_proj):
    given = dict(x=x, p=p, ffn1_norm=ffn1_norm, ffn1_w_gate=ffn1_w_gate, ffn1_w_up=ffn1_w_up, ffn1_w_down=ffn1_w_down, mix_norm=mix_norm, w_in=w_in, forget_bias=forget_bias, q_norm=q_norm, k_norm=k_norm, w_branch_fox=w_branch_fox, w_branch_sb=w_branch_sb, w_out=w_out, ffn2_norm=ffn2_norm, ffn2_w_gate=ffn2_w_gate, ffn2_w_up=ffn2_w_up, ffn2_w_down=ffn2_w_down, ple_norm=ple_norm, w_ple_gate=w_ple_gate, w_ple_proj=w_ple_proj, loss_target=loss_target, m_ffn1_norm=m_ffn1_norm, m_ffn1_w_gate=m_ffn1_w_gate, m_ffn1_w_up=m_ffn1_w_up, m_ffn1_w_down=m_ffn1_w_down, m_mix_norm=m_mix_norm, m_w_in=m_w_in, m_forget_bias=m_forget_bias, m_q_norm=m_q_norm, m_k_norm=m_k_norm, m_w_branch_fox=m_w_branch_fox, m_w_branch_sb=m_w_branch_sb, m_w_out=m_w_out, m_ffn2_norm=m_ffn2_norm, m_ffn2_w_gate=m_ffn2_w_gate, m_ffn2_w_up=m_ffn2_w_up, m_ffn2_w_down=m_ffn2_w_down, m_ple_norm=m_ple_norm, m_w_ple_gate=m_w_ple_gate, m_w_ple_proj=m_w_ple_proj, v_ffn1_norm=v_ffn1_norm, v_ffn1_w_gate=v_ffn1_w_gate, v_ffn1_w_up=v_ffn1_w_up, v_ffn1_w_down=v_ffn1_w_down, v_mix_norm=v_mix_norm, v_w_in=v_w_in, v_forget_bias=v_forget_bias, v_q_norm=v_q_norm, v_k_norm=v_k_norm, v_w_branch_fox=v_w_branch_fox, v_w_branch_sb=v_w_branch_sb, v_w_out=v_w_out, v_ffn2_norm=v_ffn2_norm, v_ffn2_w_gate=v_ffn2_w_gate, v_ffn2_w_up=v_ffn2_w_up, v_ffn2_w_down=v_ffn2_w_down, v_ple_norm=v_ple_norm, v_w_ple_gate=v_w_ple_gate, v_w_ple_proj=v_w_ple_proj)
    weights = {n: given[n] for n in TWIN_WEIGHTS}
    shared = {n: given[n] for n in SHARED_INPUTS}
    per_example = {n: given[n] for n in ['x', 'p']}
    grad_fn = _jax.value_and_grad(_loss, argnums=(0, 1))

    def one_microbatch(ex, loss_target):
        ex = dict(ex)
        diff = ex.pop(TWIN_DIFF_INPUT)
        return grad_fn(weights, diff, {**shared, **ex}, loss_target)

    if N_MICROBATCH == 1:
        loss, (grad_w, grad_x) = one_microbatch(per_example, given["loss_target"])
    else:
        def body(carry, xs):
            loss_sum, grad_sum = carry
            l_k, (gw_k, gx_k) = one_microbatch(xs[0], xs[1])
            with _jax.named_scope("update"):
                return (loss_sum + l_k, _jax.tree.map(_jnp.add, grad_sum, gw_k)), gx_k

        init = (_jnp.zeros((), _jnp.float32), _jax.tree.map(_jnp.zeros_like, weights))
        (loss, grad_w), grad_x = _jax.lax.scan(body, init, (per_example, given["loss_target"]))
    with _jax.named_scope("update"):
        delta_w, new_m, new_v = {}, {}, {}
        for n in TWIN_WEIGHTS:
            delta_w[n], new_m[n], new_v[n] = _adamw(weights[n], grad_w[n], given["m_" + n], given["v_" + n])
    return (loss, grad_x, *[grad_w[n] for n in TWIN_WEIGHTS], *[delta_w[n] for n in TWIN_WEIGHTS],
            *[new_m[n] for n in TWIN_WEIGHTS], *[new_v[n] for n in TWIN_WEIGHTS])
```

```python
import jax
import jax.numpy as jnp
from jax import lax
from jax.experimental import pallas as pl
from jax.experimental.pallas import tpu as pltpu

F32 = jnp.float32
BF16 = jnp.bfloat16

D_MODEL = 1024
D_FF = 2816
N_HEADS = 8
HEAD_DIM = 64
ATT_W = N_HEADS * HEAD_DIM
PLE_DIM = 256
EPS = 1e-6
N_DEV = 8
MESH = pl.DeviceIdType.MESH

LANES = 128
V7X_SCOPED_VMEM_BYTES = 56 * 1024 * 1024

C_FQ, C_FK, C_FV, C_FL = 0, 512, 1024, 1536
C_SQ, C_SK, C_SV, C_GF, C_GS = 1792, 2304, 2816, 3328, 4352
IN_PAD = 5376
IN_REAL = 5128
FL_REAL_END = 1544

ADAM_LR = 0.001
ADAM_B1 = 0.9
ADAM_B2 = 0.999
ADAM_EPS = 1e-08
ADAM_WD = 0.01
ADAM_STEP = 10

PACK = (
    ("ffn1_w_gate", 352), ("ffn1_w_up", 352), ("ffn1_w_down", 352), ("w_in", 656),
    ("w_branch_fox", 64), ("w_branch_sb", 64), ("w_out", 128),
    ("ffn2_w_gate", 352), ("ffn2_w_up", 352), ("ffn2_w_down", 352),
    ("w_ple_gate", 128), ("w_ple_proj", 32), ("pad", 16),
)
PACK_ROWS = sum(n for _, n in PACK)
PACK_OFF = {}
_o = 0
for _n, _r in PACK:
    PACK_OFF[_n] = (_o, _r)
    _o += _r
W_IN_ROWS = 641

SMALL_ROWS = 8


def _cparams(*sem):
    return pltpu.CompilerParams(dimension_semantics=sem, vmem_limit_bytes=V7X_SCOPED_VMEM_BYTES)


def _dot(a, b):
    return jnp.dot(a, b, preferred_element_type=F32)


def _dot_nt(a, b):
    return lax.dot_general(a, b, (((1,), (1,)), ((), ())), preferred_element_type=F32)


def _dot_tn(a, b):
    return lax.dot_general(a, b, (((0,), (0,)), ((), ())), preferred_element_type=F32)


def _split(x, parts):
    out = []
    r = x
    for _ in range(parts):
        p = r.astype(BF16)
        out.append(p)
        r = r - p.astype(F32)
    return out


def _dot_split(x, m, parts):
    acc = None
    for p in _split(x, parts):
        t = _dot(p, m)
        acc = t if acc is None else acc + t
    return acc


def _dot_split_left(m, x, parts):
    acc = None
    for p in _split(x, parts):
        t = _dot(m, p)
        acc = t if acc is None else acc + t
    return acc


def _rms_rinv(xf):
    return lax.rsqrt(jnp.mean(xf * xf, axis=-1, keepdims=True) + EPS)


def _sigmoid(x):
    return 1.0 / (1.0 + jnp.exp(-x))


def _softplus_neg_abs(z):
    return jnp.log(1.0 + jnp.exp(-jnp.abs(z)))


def _ffn_fwd(x, gain, wg, wu, wd, name):
    s_len = x.shape[0]
    ts = min(512, s_len)
    fc = D_FF // 2
    nt, nc = s_len // ts, D_FF // fc

    def body(x_ref, gain_ref, wg_ref, wu_ref, wd_ref, y_ref, g_ref, u_ref, h_scr, acc_scr):
        j = pl.program_id(1)

        @pl.when(j == 0)
        def _():
            xf = x_ref[...]
            h_scr[...] = ((xf * _rms_rinv(xf)) * gain_ref[...]).astype(BF16)
            acc_scr[...] = jnp.zeros_like(acc_scr)

        h = h_scr[...]
        g = _dot(h, wg_ref[...])
        u = _dot(h, wu_ref[...])
        g_ref[...] = g.astype(BF16)
        u_ref[...] = u.astype(BF16)
        a = (g * _sigmoid(g) * u).astype(BF16)
        acc_scr[...] += _dot(a, wd_ref[...])

        @pl.when(j == nc - 1)
        def _():
            y_ref[...] = x_ref[...] + 0.5 * acc_scr[...]

    return pl.pallas_call(
        body, name=name, grid=(nt, nc),
        in_specs=[
            pl.BlockSpec((ts, D_MODEL), lambda i, j: (i, 0)),
            pl.BlockSpec((1, D_MODEL), lambda i, j: (0, 0)),
            pl.BlockSpec((D_MODEL, fc), lambda i, j: (0, j)),
            pl.BlockSpec((D_MODEL, fc), lambda i, j: (0, j)),
            pl.BlockSpec((fc, D_MODEL), lambda i, j: (j, 0)),
        ],
        out_specs=[
            pl.BlockSpec((ts, D_MODEL), lambda i, j: (i, 0)),
            pl.BlockSpec((ts, fc), lambda i, j: (i, j)),
            pl.BlockSpec((ts, fc), lambda i, j: (i, j)),
        ],
        out_shape=[
            jax.ShapeDtypeStruct((s_len, D_MODEL), F32),
            jax.ShapeDtypeStruct((s_len, D_FF), BF16),
            jax.ShapeDtypeStruct((s_len, D_FF), BF16),
        ],
        scratch_shapes=[pltpu.VMEM((ts, D_MODEL), BF16), pltpu.VMEM((ts, D_MODEL), F32)],
        compiler_params=_cparams("parallel", "arbitrary"),
    )(x, gain, wg, wu, wd)


def _ffn_bwd_hidden(dy, g, u, wd, name):
    s_len = dy.shape[0]
    ts = min(512, s_len)
    fc = D_FF // 2
    nt, nc = s_len // ts, D_FF // fc

    def body(dy_ref, g_ref, u_ref, wd_ref, dg_ref, du_ref, act_ref):
        da = 0.5 * _dot_nt(dy_ref[...].astype(BF16), wd_ref[...])
        gf = g_ref[...].astype(F32)
        uf = u_ref[...].astype(F32)
        sg = _sigmoid(gf)
        silu = gf * sg
        dg_ref[...] = (da * uf * (sg * (1.0 + gf * (1.0 - sg)))).astype(BF16)
        du_ref[...] = (da * silu).astype(BF16)
        act_ref[...] = (0.5 * silu * uf).astype(BF16)

    hid = pl.BlockSpec((ts, fc), lambda c, t: (t, c))
    return pl.pallas_call(
        body, name=name, grid=(nc, nt),
        in_specs=[
            pl.BlockSpec((ts, D_MODEL), lambda c, t: (t, 0)), hid, hid,
            pl.BlockSpec((fc, D_MODEL), lambda c, t: (c, 0)),
        ],
        out_specs=[hid, hid, hid],
        out_shape=[jax.ShapeDtypeStruct((s_len, D_FF), BF16)] * 3,
        compiler_params=_cparams("parallel", "parallel"),
    )(dy, g, u, wd)


def _ffn_bwd_input(x, dy, gain, dg, du, wg, wu, name):
    s_len = x.shape[0]
    ts = min(512, s_len)
    fc = D_FF // 2
    nt, nc = s_len // ts, D_FF // fc

    def body(x_ref, dy_ref, gain_ref, dg_ref, du_ref, wg_ref, wu_ref, dx_ref, h_ref, dgain_ref, acc):
        i = pl.program_id(0)
        j = pl.program_id(1)
        part = _dot_nt(dg_ref[...], wg_ref[...]) + _dot_nt(du_ref[...], wu_ref[...])

        @pl.when(j == 0)
        def _():
            acc[...] = part

        @pl.when(j > 0)
        def _():
            acc[...] += part

        @pl.when(j == nc - 1)
        def _():
            xf = x_ref[...]
            r = _rms_rinv(xf)
            xhat = xf * r
            dh = acc[...]
            h_ref[...] = (xhat * gain_ref[...]).astype(BF16)
            dgp = jnp.sum(dh * xhat, axis=0, keepdims=True)

            @pl.when(i == 0)
            def _():
                dgain_ref[...] = dgp

            @pl.when(i > 0)
            def _():
                dgain_ref[...] += dgp

            dn = dh * gain_ref[...]
            dx_ref[...] = dy_ref[...] + r * (dn - xhat * jnp.mean(dn * xhat, axis=-1, keepdims=True))

    tok = pl.BlockSpec((ts, D_MODEL), lambda i, j: (i, 0))
    row = pl.BlockSpec((1, D_MODEL), lambda i, j: (0, 0))
    hid = pl.BlockSpec((ts, fc), lambda i, j: (i, j))
    wsp = pl.BlockSpec((D_MODEL, fc), lambda i, j: (0, j))
    return pl.pallas_call(
        body, name=name, grid=(nt, nc),
        in_specs=[tok, tok, row, hid, hid, wsp, wsp],
        out_specs=[tok, tok, row],
        out_shape=[
            jax.ShapeDtypeStruct((s_len, D_MODEL), F32),
            jax.ShapeDtypeStruct((s_len, D_MODEL), BF16),
            jax.ShapeDtypeStruct((1, D_MODEL), F32),
        ],
        scratch_shapes=[pltpu.VMEM((ts, D_MODEL), F32)],
        compiler_params=_cparams("arbitrary", "arbitrary"),
    )(x, dy, gain, dg, du, wg, wu)


def _wgrad(a, b, name, tk, tn):
    s_len, k_dim = a.shape
    n_dim = b.shape[1]
    ts = min(512, s_len)

    def body(a_ref, b_ref, o_ref):
        s = pl.program_id(2)
        p = _dot_tn(a_ref[...].astype(BF16), b_ref[...].astype(BF16))

        @pl.when(s == 0)
        def _():
            o_ref[...] = p

        @pl.when(s > 0)
        def _():
            o_ref[...] += p

    return pl.pallas_call(
        body, name=name, grid=(k_dim // tk, n_dim // tn, s_len // ts),
        in_specs=[
            pl.BlockSpec((ts, tk), lambda k, n, s: (s, k)),
            pl.BlockSpec((ts, tn), lambda k, n, s: (s, n)),
        ],
        out_specs=pl.BlockSpec((tk, tn), lambda k, n, s: (k, n)),
        out_shape=jax.ShapeDtypeStruct((k_dim, n_dim), F32),
        compiler_params=_cparams("parallel", "parallel", "arbitrary"),
    )(a, b)


def _head_group_matrix():
    r = lax.broadcasted_iota(jnp.int32, (ATT_W, ATT_W), 0) // HEAD_DIM
    c = lax.broadcasted_iota(jnp.int32, (ATT_W, ATT_W), 1) // HEAD_DIM
    return (r == c).astype(BF16)


def _mix_fwd(x, gain, w_in, bias, qg, kg, name):
    s_len = x.shape[0]
    ts = min(256, s_len)
    nt = s_len // ts
    gmat = _head_group_matrix()

    def body(x_ref, gain_ref, w_ref, bias_ref, qg_ref, kg_ref, gm_ref,
             h_ref, fqr_ref, fkr_ref, fqn_ref, fkn_ref, fv_ref, logf_ref, f_ref, ft_ref,
             sq_ref, sk_ref, sv_ref, gf_ref, gs_ref, carry):
        i = pl.program_id(0)
        xf = x_ref[...]
        h = ((xf * _rms_rinv(xf)) * gain_ref[...]).astype(BF16)
        h_ref[...] = h
        gm = gm_ref[...]

        def proj(lo, n):
            return _dot(h, w_ref[:, lo:lo + n])

        def headnorm(raw, g):
            ms = _dot_split(raw * raw, gm, 3) * (1.0 / HEAD_DIM)
            return ((raw * lax.rsqrt(ms + EPS)) * g).astype(BF16)

        fq = proj(C_FQ, ATT_W)
        fqr_ref[...] = fq
        fqn_ref[...] = headnorm(fq, qg_ref[...])
        fk = proj(C_FK, ATT_W)
        fkr_ref[...] = fk
        fkn_ref[...] = headnorm(fk, kg_ref[...])
        fv_ref[...] = proj(C_FV, ATT_W).astype(BF16)
        sq_ref[...] = proj(C_SQ, ATT_W).astype(BF16)
        sk_ref[...] = proj(C_SK, ATT_W).astype(BF16)
        sv_ref[...] = proj(C_SV, ATT_W).astype(BF16)
        gf_ref[...] = proj(C_GF, D_MODEL)
        gs_ref[...] = proj(C_GS, D_MODEL)

        fl = proj(C_FL, LANES) + bias_ref[...]
        lane = lax.broadcasted_iota(jnp.int32, fl.shape, 1)
        logf = jnp.where(lane < N_HEADS, jnp.minimum(fl, 0.0) - _softplus_neg_abs(fl), 0.0)
        logf_ref[...] = logf

        @pl.when(i == 0)
        def _():
            carry[...] = jnp.zeros_like(carry)

        r = lax.broadcasted_iota(jnp.int32, (ts, ts), 0)
        c = lax.broadcasted_iota(jnp.int32, (ts, ts), 1)
        tri = (r >= c).astype(BF16)
        f_tile = _dot_split_left(tri, logf, 3) + carry[...]
        f_ref[...] = f_tile
        ft_ref[...] = f_tile.T[:N_HEADS, :]
        carry[...] = f_tile[ts - 1:ts, :]

    tok = lambda w: pl.BlockSpec((ts, w), lambda i: (i, 0))
    full = lambda a: pl.BlockSpec(a.shape, lambda i: (0, 0))
    f32o = lambda w: jax.ShapeDtypeStruct((s_len, w), F32)
    b16o = lambda w: jax.ShapeDtypeStruct((s_len, w), BF16)
    return pl.pallas_call(
        body, name=name, grid=(nt,),
        in_specs=[tok(D_MODEL), full(gain), full(w_in), full(bias), full(qg), full(kg), full(gmat)],
        out_specs=[
            tok(D_MODEL), tok(ATT_W), tok(ATT_W), tok(ATT_W), tok(ATT_W), tok(ATT_W), tok(LANES), tok(LANES),
            pl.BlockSpec((N_HEADS, ts), lambda i: (0, i)),
            tok(ATT_W), tok(ATT_W), tok(ATT_W), tok(D_MODEL), tok(D_MODEL),
        ],
        out_shape=[
            b16o(D_MODEL), f32o(ATT_W), f32o(ATT_W), b16o(ATT_W), b16o(ATT_W), b16o(ATT_W), f32o(LANES), f32o(LANES),
            jax.ShapeDtypeStruct((N_HEADS, s_len), F32),
            b16o(ATT_W), b16o(ATT_W), b16o(ATT_W), f32o(D_MODEL), f32o(D_MODEL),
        ],
        scratch_shapes=[pltpu.VMEM((1, LANES), F32)],
        compiler_params=_cparams("arbitrary"),
    )(x, gain, w_in, bias, qg, kg, gmat)


ATT_T = 256


def _pair_specs(s_len, t):
    qblk = pl.BlockSpec((t, LANES), lambda hp, i: (i, hp))
    kvfull = pl.BlockSpec((s_len, LANES), lambda hp, i: (0, hp))
    return qblk, kvfull


def _head_halves(x, lo):
    z = jnp.zeros_like(x)
    return jnp.where(lo, x, z), jnp.where(lo, z, x)


def _fox_fwd(q, k, v, f_col, f_row, name):
    s_len = q.shape[0]
    t = min(ATT_T, s_len)
    nq = s_len // t

    def body(q_ref, k_ref, v_ref, f_ref, ft_ref, y_ref, lse_ref):
        hp = pl.program_id(0)
        i = pl.program_id(1)
        lane = lax.broadcasted_iota(jnp.int32, (t, LANES), 1)
        lo = lane < HEAD_DIM
        causal = lax.broadcasted_iota(jnp.int32, (t, t), 0) >= lax.broadcasted_iota(jnp.int32, (t, t), 1)
        qh = _head_halves(q_ref[...] * jnp.asarray(HEAD_DIM ** -0.5, BF16), lo)
        fb = f_ref[...]
        fq = [jnp.sum(jnp.where(lane == 2 * hp + j, fb, 0.0), axis=1, keepdims=True) for j in (0, 1)]

        def tile(kb, carry, masked):
            m, l, acc = carry
            k0 = pl.multiple_of(kb * t, t)
            kblk = k_ref[pl.ds(k0, t), :]
            vh = _head_halves(v_ref[pl.ds(k0, t), :], lo)
            m_new, l_new, alpha, pv = [], [], [], None
            for j in (0, 1):
                s = _dot_nt(qh[j], kblk) + (fq[j] - ft_ref[pl.ds(2 * hp + j, 1), pl.ds(k0, t)])
                if masked:
                    s = jnp.where(causal, s, -1e30)
                mj = jnp.maximum(m[j], jnp.max(s, axis=1, keepdims=True))
                aj = jnp.exp(m[j] - mj)
                p = jnp.exp(s - mj)
                m_new.append(mj)
                alpha.append(aj)
                l_new.append(aj * l[j] + jnp.sum(p, axis=1, keepdims=True))
                d = _dot(p.astype(BF16), vh[j])
                pv = d if pv is None else pv + d
            acc = acc * jnp.where(lo, alpha[0], alpha[1]) + pv
            return tuple(m_new), tuple(l_new), acc

        neg = jnp.full((t, 1), -1e30, F32)
        zero = jnp.zeros((t, 1), F32)
        init = ((neg, neg), (zero, zero), jnp.zeros((t, LANES), F32))
        carry = lax.fori_loop(0, i, lambda kb, c: tile(kb, c, False), init)
        m, l, acc = tile(i, carry, True)
        y_ref[...] = (acc / jnp.where(lo, l[0], l[1])).astype(BF16)
        lse_ref[0] = jnp.where(lo, m[0] + jnp.log(l[0]), m[1] + jnp.log(l[1]))

    qblk, kvfull = _pair_specs(s_len, t)
    return pl.pallas_call(
        body, name=name, grid=(N_HEADS // 2, nq),
        in_specs=[qblk, kvfull, kvfull,
                  pl.BlockSpec((t, LANES), lambda hp, i: (i, 0)),
                  pl.BlockSpec((N_HEADS, s_len), lambda hp, i: (0, 0))],
        out_specs=[qblk, pl.BlockSpec((1, t, LANES), lambda hp, i: (hp, i, 0))],
        out_shape=[jax.ShapeDtypeStruct((s_len, ATT_W), BF16),
                   jax.ShapeDtypeStruct((N_HEADS // 2, s_len, LANES), F32)],
        compiler_params=_cparams("parallel", "parallel"),
    )(q, k, v, f_col, f_row)


def _fox_bwd(q, k, v, dy, y, lse, f_col, f_row, name):
    s_len = q.shape[0]
    t = min(ATT_T, s_len)
    nq = s_len // t

    def body(q_ref, k_ref, v_ref, dy_ref, y_ref, lse_ref, f_ref, ft_ref, dq_ref, dk_ref, dv_ref, dft_ref, dfq_ref):
        hp = pl.program_id(0)
        i = pl.program_id(1)

        @pl.when(i == 0)
        def _():
            dk_ref[...] = jnp.zeros_like(dk_ref)
            dv_ref[...] = jnp.zeros_like(dv_ref)
            dft_ref[...] = jnp.zeros_like(dft_ref)

        lane = lax.broadcasted_iota(jnp.int32, (t, LANES), 1)
        lo = lane < HEAD_DIM
        causal = lax.broadcasted_iota(jnp.int32, (t, t), 0) >= lax.broadcasted_iota(jnp.int32, (t, t), 1)
        qh = _head_halves(q_ref[...] * jnp.asarray(HEAD_DIM ** -0.5, BF16), lo)
        dyb = dy_ref[...]
        dyh = _head_halves(dyb, lo)
        prod = dyb.astype(F32) * y_ref[...].astype(F32)
        delta = [jnp.sum(jnp.where(lo, prod, 0.0), axis=1, keepdims=True),
                 jnp.sum(jnp.where(lo, 0.0, prod), axis=1, keepdims=True)]
        lse_b = lse_ref[0]
        lse = [lse_b[:, 0:1], lse_b[:, HEAD_DIM:HEAD_DIM + 1]]
        fb = f_ref[...]
        fq = [jnp.sum(jnp.where(lane == 2 * hp + j, fb, 0.0), axis=1, keepdims=True) for j in (0, 1)]

        def tile(kb, carry, masked):
            dq, rs = carry
            rs = list(rs)
            k0 = pl.multiple_of(kb * t, t)
            kblk = k_ref[pl.ds(k0, t), :]
            kh = _head_halves(kblk, lo)
            vblk = v_ref[pl.ds(k0, t), :]
            dk_t, dv_t = None, None
            for j in (0, 1):
                s = _dot_nt(qh[j], kblk) + (fq[j] - ft_ref[pl.ds(2 * hp + j, 1), pl.ds(k0, t)])
                p = jnp.exp(s - lse[j])
                if masked:
                    p = jnp.where(causal, p, 0.0)
                dp = _dot_nt(dyh[j], vblk)
                ds = p * (dp - delta[j])
                dsb = ds.astype(BF16)
                dft_ref[0, pl.ds(j, 1), pl.ds(k0, t)] -= jnp.sum(ds, axis=0, keepdims=True)
                rs[j] = rs[j] + jnp.sum(ds, axis=1, keepdims=True)
                a = _dot_tn(p.astype(BF16), dyh[j])
                b = _dot_tn(dsb, qh[j])
                dv_t = a if dv_t is None else dv_t + a
                dk_t = b if dk_t is None else dk_t + b
                dq = dq + _dot(dsb, kh[j])
            dk_ref[pl.ds(k0, t), :] += dk_t
            dv_ref[pl.ds(k0, t), :] += dv_t
            return dq, tuple(rs)

        zero = jnp.zeros((t, 1), F32)
        carry = lax.fori_loop(0, i, lambda kb, c: tile(kb, c, False), (jnp.zeros((t, LANES), F32), (zero, zero)))
        dq, rs = tile(i, carry, True)
        dq_ref[...] = dq * (HEAD_DIM ** -0.5)
        dfq_ref[0] = jnp.where(lo, rs[0], rs[1])

    qblk, kvfull = _pair_specs(s_len, t)
    return pl.pallas_call(
        body, name=name, grid=(N_HEADS // 2, nq),
        in_specs=[qblk, kvfull, kvfull, qblk, qblk,
                  pl.BlockSpec((1, t, LANES), lambda hp, i: (hp, i, 0)),
                  pl.BlockSpec((t, LANES), lambda hp, i: (i, 0)),
                  pl.BlockSpec((N_HEADS, s_len), lambda hp, i: (0, 0))],
        out_specs=[qblk, kvfull, kvfull, pl.BlockSpec((1, 8, s_len), lambda hp, i: (hp, 0, 0)),
                   pl.BlockSpec((1, t, LANES), lambda hp, i: (hp, i, 0))],
        out_shape=[jax.ShapeDtypeStruct((s_len, ATT_W), F32)] * 3
        + [jax.ShapeDtypeStruct((N_HEADS // 2, 8, s_len), F32),
           jax.ShapeDtypeStruct((N_HEADS // 2, s_len, LANES), F32)],
        compiler_params=_cparams("arbitrary", "arbitrary"),
    )(q, k, v, dy, y, lse, f_col, f_row)


def _sb_tile_fwd(qhj, kblk, cj, strict, upper):
    z = _dot_nt(qhj, kblk)
    sp = _softplus_neg_abs(z)
    logb = jnp.minimum(z, 0.0) - sp
    l1m = jnp.minimum(-z, 0.0) - sp
    if strict is not None:
        l1m = jnp.where(strict, l1m, 0.0)
    after = cj + _dot_split(l1m, upper, 2)
    a = jnp.exp(logb + after)
    if strict is not None:
        a = jnp.where(strict, a, 0.0)
    c_new = after[:, 0:1] + l1m[:, 0:1]
    return logb, a, c_new


def _sb_fwd(q, k, v, name):
    s_len = q.shape[0]
    t = min(ATT_T, s_len)
    nq = s_len // t

    def body(q_ref, k_ref, v_ref, y_ref, yf_ref):
        i = pl.program_id(1)
        lane = lax.broadcasted_iota(jnp.int32, (t, LANES), 1)
        lo = lane < HEAD_DIM
        rows = lax.broadcasted_iota(jnp.int32, (t, t), 0)
        cols = lax.broadcasted_iota(jnp.int32, (t, t), 1)
        strict = cols < rows
        upper = (rows > cols).astype(BF16)
        qh = _head_halves(q_ref[...] * jnp.asarray(HEAD_DIM ** -0.5, BF16), lo)

        def tile(kb, carry, masked):
            c, acc = carry
            k0 = pl.multiple_of(kb * t, t)
            kblk = k_ref[pl.ds(k0, t), :]
            vh = _head_halves(v_ref[pl.ds(k0, t), :], lo)
            c_new = []
            for j in (0, 1):
                _, a, cj = _sb_tile_fwd(qh[j], kblk, c[j], strict if masked else None, upper)
                c_new.append(cj)
                acc = acc + _dot(a.astype(BF16), vh[j])
            return tuple(c_new), acc

        zero = jnp.zeros((t, 1), F32)
        carry = tile(i, ((zero, zero), jnp.zeros((t, LANES), F32)), True)
        _, acc = lax.fori_loop(0, i, lambda n, c: tile(i - 1 - n, c, False), carry)
        y_ref[...] = acc.astype(BF16)
        yf_ref[...] = acc

    qblk, kvfull = _pair_specs(s_len, t)
    return pl.pallas_call(
        body, name=name, grid=(N_HEADS // 2, nq),
        in_specs=[qblk, kvfull, kvfull],
        out_specs=[qblk, qblk],
        out_shape=[jax.ShapeDtypeStruct((s_len, ATT_W), BF16), jax.ShapeDtypeStruct((s_len, ATT_W), F32)],
        compiler_params=_cparams("parallel", "parallel"),
    )(q, k, v)


def _sb_bwd(q, k, v, dy, yf, name):
    s_len = q.shape[0]
    t = min(ATT_T, s_len)
    nq = s_len // t

    def body(q_ref, k_ref, v_ref, dy_ref, yf_ref, dq_ref, dk_ref, dv_ref):
        i = pl.program_id(1)

        @pl.when(i == 0)
        def _():
            dk_ref[...] = jnp.zeros_like(dk_ref)
            dv_ref[...] = jnp.zeros_like(dv_ref)

        lane = lax.broadcasted_iota(jnp.int32, (t, LANES), 1)
        lo = lane < HEAD_DIM
        rows = lax.broadcasted_iota(jnp.int32, (t, t), 0)
        cols = lax.broadcasted_iota(jnp.int32, (t, t), 1)
        strict = cols < rows
        upper = (rows > cols).astype(BF16)
        upper_incl = (rows >= cols).astype(BF16)
        qh = _head_halves(q_ref[...] * jnp.asarray(HEAD_DIM ** -0.5, BF16), lo)
        dyb = dy_ref[...]
        dyh = _head_halves(dyb, lo)
        prod = dyb.astype(F32) * yf_ref[...]
        delta = [jnp.sum(jnp.where(lo, prod, 0.0), axis=1, keepdims=True),
                 jnp.sum(jnp.where(lo, 0.0, prod), axis=1, keepdims=True)]

        def tile(kb, carry, masked):
            c, e, dq = carry
            k0 = pl.multiple_of(kb * t, t)
            kblk = k_ref[pl.ds(k0, t), :]
            kh = _head_halves(kblk, lo)
            vblk = v_ref[pl.ds(k0, t), :]
            c_new, e_new, dk_t, dv_t = [], [], None, None
            for j in (0, 1):
                logb, a, cj = _sb_tile_fwd(qh[j], kblk, c[j], strict if masked else None, upper)
                c_new.append(cj)
                ab = a.astype(BF16)
                dl = ab.astype(F32) * _dot_nt(dyh[j], vblk)
                tail = _dot_split(dl, upper_incl, 2)
                dl1m = (delta[j] - e[j]) - tail
                e_new.append(e[j] + tail[:, 0:1])
                beta = jnp.exp(logb)
                dz = dl * (1.0 - beta) - dl1m * beta
                if masked:
                    dz = jnp.where(strict, dz, 0.0)
                dzb = dz.astype(BF16)
                av = _dot_tn(ab, dyh[j])
                bk = _dot_tn(dzb, qh[j])
                dv_t = av if dv_t is None else dv_t + av
                dk_t = bk if dk_t is None else dk_t + bk
                dq = dq + _dot(dzb, kh[j])
            dk_ref[pl.ds(k0, t), :] += dk_t
            dv_ref[pl.ds(k0, t), :] += dv_t
            return tuple(c_new), tuple(e_new), dq

        zero = jnp.zeros((t, 1), F32)
        carry = tile(i, ((zero, zero), (zero, zero), jnp.zeros((t, LANES), F32)), True)
        _, _, dq = lax.fori_loop(0, i, lambda n, c: tile(i - 1 - n, c, False), carry)
        dq_ref[...] = dq * (HEAD_DIM ** -0.5)

    qblk, kvfull = _pair_specs(s_len, t)
    return pl.pallas_call(
        body, name=name, grid=(N_HEADS // 2, nq),
        in_specs=[qblk, kvfull, kvfull, qblk, qblk],
        out_specs=[qblk, kvfull, kvfull],
        out_shape=[jax.ShapeDtypeStruct((s_len, ATT_W), F32)] * 3,
        compiler_params=_cparams("arbitrary", "arbitrary"),
    )(q, k, v, dy, yf)


def _merge_fwd(x, yf, ys, gf, gs, wbf, wbs, wo, name):
    s_len = x.shape[0]
    ts = min(512, s_len)

    def body(x_ref, yf_ref, ys_ref, gf_ref, gs_ref, wbf_ref, wbs_ref, wo_ref, o_ref):
        merged = (_sigmoid(gf_ref[...]) * _dot(yf_ref[...], wbf_ref[...])
                  + _sigmoid(gs_ref[...]) * _dot(ys_ref[...], wbs_ref[...]))
        o_ref[...] = x_ref[...] + _dot(merged.astype(BF16), wo_ref[...])

    tok = lambda w: pl.BlockSpec((ts, w), lambda i: (i, 0))
    full = lambda a: pl.BlockSpec(a.shape, lambda i: (0, 0))
    return pl.pallas_call(
        body, name=name, grid=(s_len // ts,),
        in_specs=[tok(D_MODEL), tok(ATT_W), tok(ATT_W), tok(D_MODEL), tok(D_MODEL), full(wbf), full(wbs), full(wo)],
        out_specs=tok(D_MODEL),
        out_shape=jax.ShapeDtypeStruct((s_len, D_MODEL), F32),
        compiler_params=_cparams("parallel"),
    )(x, yf, ys, gf, gs, wbf, wbs, wo)


def _merge_bwd(dx, yf, ys, gf, gs, wbf, wbs, wo, name):
    s_len = dx.shape[0]
    ts = min(512, s_len)

    def body(dx_ref, yf_ref, ys_ref, gf_ref, gs_ref, wbf_ref, wbs_ref, wo_ref,
             dyf_ref, dys_ref, dgf_ref, dgs_ref, dbf_ref, dbs_ref, mg_ref):
        bf = _dot(yf_ref[...], wbf_ref[...])
        bs = _dot(ys_ref[...], wbs_ref[...])
        sf = _sigmoid(gf_ref[...])
        ss = _sigmoid(gs_ref[...])
        mg_ref[...] = (sf * bf + ss * bs).astype(BF16)
        dm = _dot_nt(dx_ref[...].astype(BF16), wo_ref[...])
        dbf = (dm * sf).astype(BF16)
        dbs = (dm * ss).astype(BF16)
        dbf_ref[...] = dbf
        dbs_ref[...] = dbs
        dgf_ref[...] = (dm * bf * (sf * (1.0 - sf))).astype(BF16)
        dgs_ref[...] = (dm * bs * (ss * (1.0 - ss))).astype(BF16)
        dyf_ref[...] = _dot_nt(dbf, wbf_ref[...]).astype(BF16)
        dys_ref[...] = _dot_nt(dbs, wbs_ref[...]).astype(BF16)

    tok = lambda w: pl.BlockSpec((ts, w), lambda i: (i, 0))
    full = lambda a: pl.BlockSpec(a.shape, lambda i: (0, 0))
    b16o = lambda w: jax.ShapeDtypeStruct((s_len, w), BF16)
    return pl.pallas_call(
        body, name=name, grid=(s_len // ts,),
        in_specs=[tok(D_MODEL), tok(ATT_W), tok(ATT_W), tok(D_MODEL), tok(D_MODEL), full(wbf), full(wbs), full(wo)],
        out_specs=[tok(ATT_W), tok(ATT_W)] + [tok(D_MODEL)] * 5,
        out_shape=[b16o(ATT_W), b16o(ATT_W)] + [b16o(D_MODEL)] * 5,
        compiler_params=_cparams("parallel"),
    )(dx, yf, ys, gf, gs, wbf, wbs, wo)


def _mix_bwd(x, dx_in, gain, w_in, fqr, fkr, dfqn, dfkn, qg, kg, dfv, df_col, logf, dsq, dsk, dsv, dgf, dgs, name):
    s_len = x.shape[0]
    ts = min(256, s_len)
    nt = s_len // ts
    gmat = _head_group_matrix()

    def body(x_ref, dxi_ref, gain_ref, w_ref, fqr_ref, fkr_ref, dfqn_ref, dfkn_ref, qg_ref, kg_ref, gm_ref,
             dfv_ref, df_ref, logf_ref, dsq_ref, dsk_ref, dsv_ref, dgf_ref, dgs_ref,
             dp_ref, dx_ref, dgain_ref, dqg_ref, dkg_ref, dbias_ref, carry):
        i = pl.program_id(0)

        @pl.when(i == 0)
        def _():
            carry[...] = jnp.zeros_like(carry)
            dgain_ref[...] = jnp.zeros_like(dgain_ref)
            dqg_ref[...] = jnp.zeros_like(dqg_ref)
            dkg_ref[...] = jnp.zeros_like(dkg_ref)
            dbias_ref[...] = jnp.zeros_like(dbias_ref)

        gm = gm_ref[...]

        def headnorm_bwd(raw, dout, g, dg_ref):
            ms = _dot_split(raw * raw, gm, 3) * (1.0 / HEAD_DIM)
            r = lax.rsqrt(ms + EPS)
            nrm = raw * r
            dg_ref[...] += jnp.sum(dout * nrm, axis=0, keepdims=True)
            dn = dout * g
            mean_h = _dot_split(dn * nrm, gm, 3) * (1.0 / HEAD_DIM)
            return r * (dn - nrm * mean_h)

        dp_ref[:, C_FQ:C_FQ + ATT_W] = headnorm_bwd(fqr_ref[...], dfqn_ref[...], qg_ref[...], dqg_ref).astype(BF16)
        dp_ref[:, C_FK:C_FK + ATT_W] = headnorm_bwd(fkr_ref[...], dfkn_ref[...], kg_ref[...], dkg_ref).astype(BF16)
        dp_ref[:, C_FV:C_FV + ATT_W] = dfv_ref[...].astype(BF16)
        dp_ref[:, C_SQ:C_SQ + ATT_W] = dsq_ref[...].astype(BF16)
        dp_ref[:, C_SK:C_SK + ATT_W] = dsk_ref[...].astype(BF16)
        dp_ref[:, C_SV:C_SV + ATT_W] = dsv_ref[...].astype(BF16)
        dp_ref[:, C_GF:C_GF + D_MODEL] = dgf_ref[...]
        dp_ref[:, C_GS:C_GS + D_MODEL] = dgs_ref[...]

        r_ = lax.broadcasted_iota(jnp.int32, (ts, ts), 0)
        c_ = lax.broadcasted_iota(jnp.int32, (ts, ts), 1)
        rev = (c_ >= r_).astype(BF16)
        dlogf = _dot_split_left(rev, df_ref[...], 3) + carry[...]
        carry[...] = dlogf[0:1, :]
        lane = lax.broadcasted_iota(jnp.int32, (ts, LANES), 1)
        dfl = jnp.where(lane < N_HEADS, dlogf * (1.0 - jnp.exp(logf_ref[...])), 0.0)
        dbias_ref[...] += jnp.sum(dfl, axis=0, keepdims=True)
        dp_ref[:, C_FL:C_FL + LANES] = dfl.astype(BF16)
        dp_ref[:, C_FL + LANES:C_SQ] = jnp.zeros((ts, C_SQ - C_FL - LANES), BF16)

        dh = _dot_nt(dp_ref[...], w_ref[...])
        xf = x_ref[...]
        r = _rms_rinv(xf)
        xhat = xf * r
        dgain_ref[...] += jnp.sum(dh * xhat, axis=0, keepdims=True)
        dn = dh * gain_ref[...]
        dx_ref[...] = dxi_ref[...] + r * (dn - xhat * jnp.mean(dn * xhat, axis=-1, keepdims=True))

    tok = lambda w: pl.BlockSpec((ts, w), lambda i: (nt - 1 - i, 0))
    full = lambda a: pl.BlockSpec(a.shape, lambda i: (0, 0))
    row = lambda w: pl.BlockSpec((1, w), lambda i: (0, 0))
    return pl.pallas_call(
        body, name=name, grid=(nt,),
        in_specs=[tok(D_MODEL), tok(D_MODEL), full(gain), full(w_in), tok(ATT_W), tok(ATT_W), tok(ATT_W), tok(ATT_W),
                  full(qg), full(kg), full(gmat), tok(ATT_W), tok(LANES), tok(LANES), tok(ATT_W), tok(ATT_W), tok(ATT_W),
                  tok(D_MODEL), tok(D_MODEL)],
        out_specs=[tok(IN_PAD), tok(D_MODEL), row(D_MODEL), row(ATT_W), row(ATT_W), row(LANES)],
        out_shape=[jax.ShapeDtypeStruct((s_len, IN_PAD), BF16), jax.ShapeDtypeStruct((s_len, D_MODEL), F32),
                   jax.ShapeDtypeStruct((1, D_MODEL), F32), jax.ShapeDtypeStruct((1, ATT_W), F32),
                   jax.ShapeDtypeStruct((1, ATT_W), F32), jax.ShapeDtypeStruct((1, LANES), F32)],
        scratch_shapes=[pltpu.VMEM((1, LANES), F32)],
        compiler_params=_cparams("arbitrary"),
    )(x, dx_in, gain, w_in, fqr, fkr, dfqn, dfkn, qg, kg, gmat, dfv, df_col, logf, dsq, dsk, dsv, dgf, dgs)


def _ple_loss(x, p, tgt, gain, wpg, wpp, name):
    s_len = x.shape[0]
    ts = min(512, s_len)

    def body(x_ref, p_ref, t_ref, gain_ref, wpg_ref, wpp_ref, dx_ref, n_ref, ds_ref, dpp_ref, dgain_ref, loss_ref):
        i = pl.program_id(0)

        @pl.when(i == 0)
        def _():
            dgain_ref[...] = jnp.zeros_like(dgain_ref)
            loss_ref[...] = jnp.zeros_like(loss_ref)

        xf = x_ref[...]
        r = _rms_rinv(xf)
        n = xf * r
        hn = (n * gain_ref[...]).astype(BF16)
        n_ref[...] = hn
        sg = _sigmoid(_dot(hn, wpg_ref[...]))
        pp = _dot(p_ref[...].astype(BF16), wpp_ref[...])
        err = (xf + sg * pp) - t_ref[...]
        sq = jnp.sum(jnp.sum(err * err, axis=1, keepdims=True), axis=0, keepdims=True)
        loss_ref[...] += (0.5 / D_MODEL) * sq
        dout = err * (1.0 / D_MODEL)
        dpp_ref[...] = (dout * sg).astype(BF16)
        ds = (dout * pp * (sg * (1.0 - sg))).astype(BF16)
        ds_ref[...] = ds
        dhn = _dot_nt(ds, wpg_ref[...])
        dgain_ref[...] += jnp.sum(dhn * n, axis=0, keepdims=True)
        dn = dhn * gain_ref[...]
        dx_ref[...] = dout + r * (dn - n * jnp.mean(dn * n, axis=-1, keepdims=True))

    tok = lambda w: pl.BlockSpec((ts, w), lambda i: (i, 0))
    full = lambda a: pl.BlockSpec(a.shape, lambda i: (0, 0))
    return pl.pallas_call(
        body, name=name, grid=(s_len // ts,),
        in_specs=[tok(D_MODEL), tok(PLE_DIM), tok(D_MODEL), full(gain), full(wpg), full(wpp)],
        out_specs=[tok(D_MODEL), tok(D_MODEL), tok(D_MODEL), tok(D_MODEL),
                   pl.BlockSpec((1, D_MODEL), lambda i: (0, 0)), pl.BlockSpec((8, LANES), lambda i: (0, 0))],
        out_shape=[jax.ShapeDtypeStruct((s_len, D_MODEL), F32), jax.ShapeDtypeStruct((s_len, D_MODEL), BF16),
                   jax.ShapeDtypeStruct((s_len, D_MODEL), BF16), jax.ShapeDtypeStruct((s_len, D_MODEL), BF16),
                   jax.ShapeDtypeStruct((1, D_MODEL), F32), jax.ShapeDtypeStruct((8, LANES), F32)],
        compiler_params=_cparams("arbitrary"),
    )(x, p, tgt, gain, wpg, wpp)


def _exchange(x, name, broadcast):
    shape = x.shape if broadcast else x.shape[1:]

    def body(x_ref, out_ref, send_sems, recv_sems, local_sem):
        mx, my, mc = lax.axis_index("x"), lax.axis_index("y"), lax.axis_index("c")
        me = 4 * mx + 2 * my + mc

        def src(idx):
            return x_ref if broadcast else x_ref.at[idx]

        local = pltpu.make_async_copy(src(me), out_ref.at[me], local_sem)
        local.start()
        copies = []
        for k in range(1, N_DEV):
            px = (1 - mx) if k & 4 else mx
            py = (1 - my) if k & 2 else my
            pc = (1 - mc) if k & 1 else mc
            peer = 4 * px + 2 * py + pc
            cp = pltpu.make_async_remote_copy(
                src_ref=src(peer), dst_ref=out_ref.at[me], send_sem=send_sems.at[k - 1], recv_sem=recv_sems.at[k - 1],
                device_id=(px, py, pc), device_id_type=MESH)
            cp.start()
            copies.append((cp, peer, (px, py, pc)))
        for k, (cp, peer, pid) in enumerate(copies):
            pltpu.make_async_remote_copy(
                src_ref=src(peer), dst_ref=out_ref.at[peer], send_sem=send_sems.at[k], recv_sem=recv_sems.at[k],
                device_id=pid, device_id_type=MESH).wait_recv()
        for cp, _, _ in copies:
            cp.wait_send()
        local.wait()

    return pl.pallas_call(
        body, name=name,
        in_specs=[pl.BlockSpec(memory_space=pl.ANY)],
        out_specs=pl.BlockSpec(memory_space=pl.ANY),
        out_shape=jax.ShapeDtypeStruct((N_DEV,) + tuple(shape), x.dtype),
        scratch_shapes=[pltpu.SemaphoreType.DMA((N_DEV - 1,)), pltpu.SemaphoreType.DMA((N_DEV - 1,)),
                        pltpu.SemaphoreType.DMA],
        compiler_params=pltpu.CompilerParams(has_side_effects=True),
    )(x)


def _adamw_math(w, g, m, v):
    m2 = ADAM_B1 * m + (1.0 - ADAM_B1) * g
    v2 = ADAM_B2 * v + (1.0 - ADAM_B2) * (g * g)
    m_hat = m2 / (1.0 - ADAM_B1 ** ADAM_STEP)
    v_hat = v2 / (1.0 - ADAM_B2 ** ADAM_STEP)
    delta = -ADAM_LR * (m_hat / (jnp.sqrt(v_hat) + ADAM_EPS) + ADAM_WD * w)
    return delta, m2, v2


def _adamw(parts, w, m, v, name, tr):
    rows, cols = w.shape

    def body(p_ref, w_ref, m_ref, v_ref, g_ref, d_ref, m2_ref, v2_ref):
        g = p_ref[0].astype(F32)
        for s in range(1, N_DEV):
            g = g + p_ref[s].astype(F32)
        g_ref[...] = g
        d_ref[...], m2_ref[...], v2_ref[...] = _adamw_math(w_ref[...], g, m_ref[...], v_ref[...])

    blk = pl.BlockSpec((tr, cols), lambda i: (i, 0))
    return pl.pallas_call(
        body, name=name, grid=(rows // tr,),
        in_specs=[pl.BlockSpec((N_DEV, tr, cols), lambda i: (0, i, 0)), blk, blk, blk],
        out_specs=[blk] * 4,
        out_shape=[jax.ShapeDtypeStruct((rows, cols), F32)] * 4,
        compiler_params=_cparams("parallel"),
    )(parts, w, m, v)


def _to_rows(a):
    return a.reshape(-1, D_MODEL)


def _pack(pieces, dtype):
    out = []
    for name, rows in PACK:
        if name == "pad":
            out.append(jnp.zeros((rows, D_MODEL), dtype))
            continue
        r = _to_rows(pieces[name]).astype(dtype)
        if r.shape[0] != rows:
            r = jnp.pad(r, ((0, rows - r.shape[0]), (0, 0)))
        out.append(r)
    return jnp.concatenate(out, axis=0)


COL_SHARDED = {"ffn1_w_gate": (D_MODEL, D_FF), "ffn1_w_up": (D_MODEL, D_FF), "w_in": (D_MODEL, IN_REAL),
               "w_branch_fox": (ATT_W, D_MODEL), "w_branch_sb": (ATT_W, D_MODEL),
               "ffn2_w_gate": (D_MODEL, D_FF), "ffn2_w_up": (D_MODEL, D_FF), "w_ple_proj": (PLE_DIM, D_MODEL)}
ROW_SHARDED = {"ffn1_w_down": (D_FF, D_MODEL), "w_out": (D_MODEL, D_MODEL), "ffn2_w_down": (D_FF, D_MODEL),
               "w_ple_gate": (D_MODEL, D_MODEL)}


def _real_rows(name):
    return W_IN_ROWS if name == "w_in" else PACK_OFF[name][1]


def _unpack_full(gathered, name):
    off, _ = PACK_OFF[name]
    g = gathered[:, off:off + _real_rows(name), :]
    if name in ROW_SHARDED:
        return g.reshape(ROW_SHARDED[name])
    k_dim, n_dim = COL_SHARDED[name]
    return g.reshape(N_DEV, k_dim, n_dim // N_DEV).transpose(1, 0, 2).reshape(k_dim, n_dim)


def _to_chunks(full, name):
    if name in ROW_SHARDED:
        return full.reshape(N_DEV, -1, D_MODEL)
    k_dim, n_dim = COL_SHARDED[name]
    return full.reshape(k_dim, N_DEV, n_dim // N_DEV).transpose(1, 0, 2).reshape(N_DEV, -1, D_MODEL)


def _pack_chunks(grads):
    out = []
    for name, rows in PACK:
        if name == "pad":
            out.append(jnp.zeros((N_DEV, rows, D_MODEL), BF16))
            continue
        c = _to_chunks(grads[name], name).astype(BF16)
        if c.shape[1] != rows:
            c = jnp.pad(c, ((0, 0), (0, rows - c.shape[1]), (0, 0)))
        out.append(c)
    return jnp.concatenate(out, axis=1)


def _unpack_shard(packed, name, shape):
    off, _ = PACK_OFF[name]
    return packed[off:off + _real_rows(name), :].reshape(shape)


WEIGHT_NAMES = ['ffn1_norm', 'ffn1_w_gate', 'ffn1_w_up', 'ffn1_w_down', 'mix_norm', 'w_in', 'forget_bias', 'q_norm',
                'k_norm', 'w_branch_fox', 'w_branch_sb', 'w_out', 'ffn2_norm', 'ffn2_w_gate', 'ffn2_w_up',
                'ffn2_w_down', 'ple_norm', 'w_ple_gate', 'w_ple_proj']
SMALL_NAMES = ('ffn1_norm', 'mix_norm', 'ffn2_norm', 'ple_norm', 'q_norm', 'k_norm', 'forget_bias')
Q_OFF, K_OFF, B_OFF, LOSS_OFF = 0, HEAD_DIM, 2 * HEAD_DIM, 2 * HEAD_DIM + N_HEADS


def _pack_small(vals, loss=None):
    tail = [vals['q_norm'].reshape(1, -1), vals['k_norm'].reshape(1, -1), vals['forget_bias'].reshape(1, -1)]
    used = LOSS_OFF
    if loss is not None:
        tail.append(loss.reshape(1, 1))
        used += 1
    tail.append(jnp.zeros((1, D_MODEL - used), F32))
    rows = [vals[n].reshape(1, D_MODEL) for n in SMALL_NAMES[:4]] + [jnp.concatenate(tail, axis=1)]
    rows.append(jnp.zeros((SMALL_ROWS - len(rows), D_MODEL), F32))
    return jnp.concatenate(rows, axis=0)


def _unpack_small(packed, name, shape):
    if name in SMALL_NAMES[:4]:
        return packed[SMALL_NAMES.index(name)].reshape(shape)
    off, n = {'q_norm': (Q_OFF, HEAD_DIM), 'k_norm': (K_OFF, HEAD_DIM), 'forget_bias': (B_OFF, N_HEADS)}[name]
    return packed[4, off:off + n].reshape(shape)


def _pad_w_in(w):
    z = lambda n: jnp.zeros((D_MODEL, n), w.dtype)
    return jnp.concatenate([w[:, :FL_REAL_END], z(C_SQ - FL_REAL_END), w[:, FL_REAL_END:]], axis=1)


def _unpad_w_in(w):
    return jnp.concatenate([w[:, :FL_REAL_END], w[:, C_SQ:]], axis=1)


def _local_step(x, p, tgt, wts, small):
    row = lambda a: a.reshape(1, -1).astype(F32)
    g_ffn1, g_mix, g_ffn2, g_ple = (row(small[n]) for n in SMALL_NAMES[:4])
    qg = jnp.tile(row(small['q_norm']), (1, N_HEADS))
    kg = jnp.tile(row(small['k_norm']), (1, N_HEADS))
    bias = jnp.pad(row(small['forget_bias']), ((0, 0), (0, LANES - N_HEADS)))
    w_in = _pad_w_in(wts['w_in'])

    x1, g1, u1 = _ffn_fwd(x, g_ffn1, wts['ffn1_w_gate'], wts['ffn1_w_up'], wts['ffn1_w_down'], "ffn1_fwd")
    (hmix, fqr, fkr, fqn, fkn, fv, logf, f_col, f_row, sq, sk, sv, gf, gs) = _mix_fwd(
        x1, g_mix, w_in, bias, qg, kg, "mix_fwd")
    y_fox, lse = _fox_fwd(fqn, fkn, fv, f_col, f_row, "fox_fwd")
    y_sb, y_sb32 = _sb_fwd(sq, sk, sv, "sb_fwd")
    x2 = _merge_fwd(x1, y_fox, y_sb, gf, gs, wts['w_branch_fox'], wts['w_branch_sb'], wts['w_out'], "merge_fwd")
    x3, g2, u2 = _ffn_fwd(x2, g_ffn2, wts['ffn2_w_gate'], wts['ffn2_w_up'], wts['ffn2_w_down'], "ffn2_fwd")
    dx3, n_ple, ds_ple, dpp, dg_ple, loss = _ple_loss(x3, p, tgt, g_ple, wts['w_ple_gate'], wts['w_ple_proj'], "ple_loss")

    grads = {}
    dg2, du2, act2 = _ffn_bwd_hidden(dx3, g2, u2, wts['ffn2_w_down'], "ffn2_bwd_hidden")
    dx2, h2, dg_ffn2 = _ffn_bwd_input(x2, dx3, g_ffn2, dg2, du2, wts['ffn2_w_gate'], wts['ffn2_w_up'], "ffn2_bwd_input")
    dyf, dys, dgf, dgs, dbf, dbs, merged = _merge_bwd(
        dx2, y_fox, y_sb, gf, gs, wts['w_branch_fox'], wts['w_branch_sb'], wts['w_out'], "merge_bwd")
    dfqn, dfkn, dfv, dft, dfq = _fox_bwd(fqn, fkn, fv, dyf, y_fox, lse, f_col, f_row, "fox_bwd")
    dsq, dsk, dsv = _sb_bwd(sq, sk, sv, dys, y_sb32, "sb_bwd")
    s_len = x.shape[0]
    df_heads = dft[:, :2, :] + jnp.stack([dfq[:, :, 0], dfq[:, :, HEAD_DIM]], axis=1)
    df_col = jnp.pad(df_heads.reshape(N_HEADS, s_len).T, ((0, 0), (0, LANES - N_HEADS)))
    dproj, dx1, dg_mix, dqg, dkg, dbias = _mix_bwd(
        x1, dx2, g_mix, w_in, fqr, fkr, dfqn, dfkn, qg, kg, dfv, df_col, logf, dsq, dsk, dsv, dgf, dgs, "mix_bwd")
    dg1, du1, act1 = _ffn_bwd_hidden(dx1, g1, u1, wts['ffn1_w_down'], "ffn1_bwd_hidden")
    dx0, h1, dg_ffn1 = _ffn_bwd_input(x, dx1, g_ffn1, dg1, du1, wts['ffn1_w_gate'], wts['ffn1_w_up'], "ffn1_bwd_input")

    half = D_FF // 2
    grads['ffn1_w_gate'] = _wgrad(h1, dg1, "dw_ffn1_gate", D_MODEL, half)
    grads['ffn1_w_up'] = _wgrad(h1, du1, "dw_ffn1_up", D_MODEL, half)
    grads['ffn1_w_down'] = _wgrad(act1, dx1, "dw_ffn1_down", half, D_MODEL)
    grads['ffn2_w_gate'] = _wgrad(h2, dg2, "dw_ffn2_gate", D_MODEL, half)
    grads['ffn2_w_up'] = _wgrad(h2, du2, "dw_ffn2_up", D_MODEL, half)
    grads['ffn2_w_down'] = _wgrad(act2, dx3, "dw_ffn2_down", half, D_MODEL)
    grads['w_in'] = _unpad_w_in(_wgrad(hmix, dproj, "dw_in", D_MODEL, IN_PAD // 3))
    grads['w_branch_fox'] = _wgrad(y_fox, dbf, "dw_branch_fox", ATT_W, D_MODEL)
    grads['w_branch_sb'] = _wgrad(y_sb, dbs, "dw_branch_sb", ATT_W, D_MODEL)
    grads['w_out'] = _wgrad(merged, dx2, "dw_out", D_MODEL, D_MODEL)
    grads['w_ple_gate'] = _wgrad(n_ple, ds_ple, "dw_ple_gate", D_MODEL, D_MODEL)
    grads['w_ple_proj'] = _wgrad(p, dpp, "dw_ple_proj", PLE_DIM, D_MODEL)

    fold = lambda a: a.reshape(N_HEADS, HEAD_DIM).sum(axis=0).reshape(1, HEAD_DIM)
    small_g = {'ffn1_norm': dg_ffn1, 'mix_norm': dg_mix, 'ffn2_norm': dg_ffn2, 'ple_norm': dg_ple,
               'q_norm': fold(dqg), 'k_norm': fold(dkg), 'forget_bias': dbias[:, :N_HEADS]}
    return loss[0, 0], dx0, grads, small_g


def kernel(x, p, ffn1_norm, ffn1_w_gate, ffn1_w_up, ffn1_w_down, mix_norm, w_in, forget_bias, q_norm, k_norm, w_branch_fox, w_branch_sb, w_out, ffn2_norm, ffn2_w_gate, ffn2_w_up, ffn2_w_down, ple_norm, w_ple_gate, w_ple_proj, loss_target, m_ffn1_norm, m_ffn1_w_gate, m_ffn1_w_up, m_ffn1_w_down, m_mix_norm, m_w_in, m_forget_bias, m_q_norm, m_k_norm, m_w_branch_fox, m_w_branch_sb, m_w_out, m_ffn2_norm, m_ffn2_w_gate, m_ffn2_w_up, m_ffn2_w_down, m_ple_norm, m_w_ple_gate, m_w_ple_proj, v_ffn1_norm, v_ffn1_w_gate, v_ffn1_w_up, v_ffn1_w_down, v_mix_norm, v_w_in, v_forget_bias, v_q_norm, v_k_norm, v_w_branch_fox, v_w_branch_sb, v_w_out, v_ffn2_norm, v_ffn2_w_gate, v_ffn2_w_up, v_ffn2_w_down, v_ple_norm, v_w_ple_gate, v_w_ple_proj):
    args = dict(locals())
    w = {n: args[n][0] for n in WEIGHT_NAMES}
    m = {n: args["m_" + n][0] for n in WEIGHT_NAMES}
    v = {n: args["v_" + n][0] for n in WEIGHT_NAMES}
    big = [n for n, _ in PACK if n != "pad"]

    gathered = _exchange(_pack(w, BF16), "gather_weights", True)
    wts = {n: _unpack_full(gathered, n) for n in big}

    loss, dx, grads, small_g = _local_step(x[0], p[0, 0], loss_target[0], wts, w)

    parts = _exchange(_pack_chunks(grads), "scatter_grads", False)
    g_p, d_p, m_p, v_p = _adamw(parts, _pack(w, F32), _pack(m, F32), _pack(v, F32), "adamw", PACK_ROWS // 10)
    small_parts = _exchange(_pack_small(small_g, loss), "gather_small", True)
    sw, sm, sv = (_pack_small(t) for t in (w, m, v))
    g_s, d_s, m_s, v_s = _adamw(small_parts, sw, sm, sv, "adamw_small", SMALL_ROWS)

    outs = [g_s[4, LOSS_OFF], dx.reshape(x.shape)]
    for packed_big, packed_small in ((g_p, g_s), (d_p, d_s), (m_p, m_s), (v_p, v_s)):
        for n in WEIGHT_NAMES:
            shape = args[n].shape
            if n in SMALL_NAMES:
                outs.append(_unpack_small(packed_small, n, shape))
            else:
                outs.append(_unpack_shard(packed_big, n, shape))
    return tuple(outs)
```

```python
import jax
import jax.numpy as jnp
from jax import lax
from jax.experimental import pallas as pl
from jax.experimental.pallas import tpu as pltpu

F32 = jnp.float32
BF16 = jnp.bfloat16

D_MODEL = 1024
D_FF = 2816
N_HEADS = 8
HEAD_DIM = 64
ATT_W = N_HEADS * HEAD_DIM
PLE_DIM = 256
EPS = 1e-6
N_DEV = 8
MESH = pl.DeviceIdType.MESH

LANES = 128
V7X_SCOPED_VMEM_BYTES = 56 * 1024 * 1024

C_FQ, C_FK, C_FV, C_FL = 0, 512, 1024, 1536
C_SQ, C_SK, C_SV, C_GF, C_GS = 1792, 2304, 2816, 3328, 4352
IN_PAD = 5376
IN_REAL = 5128
FL_REAL_END = 1544

ADAM_LR = 0.001
ADAM_B1 = 0.9
ADAM_B2 = 0.999
ADAM_EPS = 1e-08
ADAM_WD = 0.01
ADAM_STEP = 10

PACK = (
    ("ffn1_w_gate", 352), ("ffn1_w_up", 352), ("ffn1_w_down", 352), ("w_in", 656),
    ("w_branch_fox", 64), ("w_branch_sb", 64), ("w_out", 128),
    ("ffn2_w_gate", 352), ("ffn2_w_up", 352), ("ffn2_w_down", 352),
    ("w_ple_gate", 128), ("w_ple_proj", 32), ("pad", 16),
)
PACK_ROWS = sum(n for _, n in PACK)
PACK_OFF = {}
_o = 0
for _n, _r in PACK:
    PACK_OFF[_n] = (_o, _r)
    _o += _r
W_IN_ROWS = 641

SMALL_ROWS = 8


def _cparams(*sem):
    return pltpu.CompilerParams(dimension_semantics=sem, vmem_limit_bytes=V7X_SCOPED_VMEM_BYTES)


def _dot(a, b):
    return jnp.dot(a, b, preferred_element_type=F32)


def _dot_nt(a, b):
    return lax.dot_general(a, b, (((1,), (1,)), ((), ())), preferred_element_type=F32)


def _dot_tn(a, b):
    return lax.dot_general(a, b, (((0,), (0,)), ((), ())), preferred_element_type=F32)


def _split(x, parts):
    out = []
    r = x
    for _ in range(parts):
        p = r.astype(BF16)
        out.append(p)
        r = r - p.astype(F32)
    return out


def _dot_split(x, m, parts):
    acc = None
    for p in _split(x, parts):
        t = _dot(p, m)
        acc = t if acc is None else acc + t
    return acc


def _dot_split_left(m, x, parts):
    acc = None
    for p in _split(x, parts):
        t = _dot(m, p)
        acc = t if acc is None else acc + t
    return acc


def _rms_rinv(xf):
    return lax.rsqrt(jnp.mean(xf * xf, axis=-1, keepdims=True) + EPS)


def _sigmoid(x):
    return 1.0 / (1.0 + jnp.exp(-x))


def _softplus_neg_abs(z):
    return jnp.log(1.0 + jnp.exp(-jnp.abs(z)))


def _ffn_fwd(x, gain, wg, wu, wd, name):
    s_len = x.shape[0]
    ts = min(512, s_len)
    fc = D_FF // 2
    nt, nc = s_len // ts, D_FF // fc

    def body(x_ref, gain_ref, wg_ref, wu_ref, wd_ref, y_ref, g_ref, u_ref, h_scr, acc_scr):
        j = pl.program_id(1)

        @pl.when(j == 0)
        def _():
            xf = x_ref[...]
            h_scr[...] = ((xf * _rms_rinv(xf)) * gain_ref[...]).astype(BF16)
            acc_scr[...] = jnp.zeros_like(acc_scr)

        h = h_scr[...]
        g = _dot(h, wg_ref[...])
        u = _dot(h, wu_ref[...])
        g_ref[...] = g.astype(BF16)
        u_ref[...] = u.astype(BF16)
        a = (g * _sigmoid(g) * u).astype(BF16)
        acc_scr[...] += _dot(a, wd_ref[...])

        @pl.when(j == nc - 1)
        def _():
            y_ref[...] = x_ref[...] + 0.5 * acc_scr[...]

    return pl.pallas_call(
        body, name=name, grid=(nt, nc),
        in_specs=[
            pl.BlockSpec((ts, D_MODEL), lambda i, j: (i, 0)),
            pl.BlockSpec((1, D_MODEL), lambda i, j: (0, 0)),
            pl.BlockSpec((D_MODEL, fc), lambda i, j: (0, j)),
            pl.BlockSpec((D_MODEL, fc), lambda i, j: (0, j)),
            pl.BlockSpec((fc, D_MODEL), lambda i, j: (j, 0)),
        ],
        out_specs=[
            pl.BlockSpec((ts, D_MODEL), lambda i, j: (i, 0)),
            pl.BlockSpec((ts, fc), lambda i, j: (i, j)),
            pl.BlockSpec((ts, fc), lambda i, j: (i, j)),
        ],
        out_shape=[
            jax.ShapeDtypeStruct((s_len, D_MODEL), F32),
            jax.ShapeDtypeStruct((s_len, D_FF), BF16),
            jax.ShapeDtypeStruct((s_len, D_FF), BF16),
        ],
        scratch_shapes=[pltpu.VMEM((ts, D_MODEL), BF16), pltpu.VMEM((ts, D_MODEL), F32)],
        compiler_params=_cparams("parallel", "arbitrary"),
    )(x, gain, wg, wu, wd)


def _ffn_bwd_hidden(dy, g, u, wd, name):
    s_len = dy.shape[0]
    ts = min(512, s_len)
    fc = D_FF // 2
    nt, nc = s_len // ts, D_FF // fc

    def body(dy_ref, g_ref, u_ref, wd_ref, dg_ref, du_ref, act_ref):
        da = 0.5 * _dot_nt(dy_ref[...].astype(BF16), wd_ref[...])
        gf = g_ref[...].astype(F32)
        uf = u_ref[...].astype(F32)
        sg = _sigmoid(gf)
        silu = gf * sg
        dg_ref[...] = (da * uf * (sg * (1.0 + gf * (1.0 - sg)))).astype(BF16)
        du_ref[...] = (da * silu).astype(BF16)
        act_ref[...] = (0.5 * silu * uf).astype(BF16)

    hid = pl.BlockSpec((ts, fc), lambda c, t: (t, c))
    return pl.pallas_call(
        body, name=name, grid=(nc, nt),
        in_specs=[
            pl.BlockSpec((ts, D_MODEL), lambda c, t: (t, 0)), hid, hid,
            pl.BlockSpec((fc, D_MODEL), lambda c, t: (c, 0)),
        ],
        out_specs=[hid, hid, hid],
        out_shape=[jax.ShapeDtypeStruct((s_len, D_FF), BF16)] * 3,
        compiler_params=_cparams("parallel", "parallel"),
    )(dy, g, u, wd)


def _ffn_bwd_input(x, dy, gain, dg, du, wg, wu, name):
    s_len = x.shape[0]
    ts = min(512, s_len)
    fc = D_FF // 2
    nt, nc = s_len // ts, D_FF // fc

    def body(x_ref, dy_ref, gain_ref, dg_ref, du_ref, wg_ref, wu_ref, dx_ref, h_ref, dgain_ref, acc):
        i = pl.program_id(0)
        j = pl.program_id(1)
        part = _dot_nt(dg_ref[...], wg_ref[...]) + _dot_nt(du_ref[...], wu_ref[...])

        @pl.when(j == 0)
        def _():
            acc[...] = part

        @pl.when(j > 0)
        def _():
            acc[...] += part

        @pl.when(j == nc - 1)
        def _():
            xf = x_ref[...]
            r = _rms_rinv(xf)
            xhat = xf * r
            dh = acc[...]
            h_ref[...] = (xhat * gain_ref[...]).astype(BF16)
            dgp = jnp.sum(dh * xhat, axis=0, keepdims=True)

            @pl.when(i == 0)
            def _():
                dgain_ref[...] = dgp

            @pl.when(i > 0)
            def _():
                dgain_ref[...] += dgp

            dn = dh * gain_ref[...]
            dx_ref[...] = dy_ref[...] + r * (dn - xhat * jnp.mean(dn * xhat, axis=-1, keepdims=True))

    tok = pl.BlockSpec((ts, D_MODEL), lambda i, j: (i, 0))
    row = pl.BlockSpec((1, D_MODEL), lambda i, j: (0, 0))
    hid = pl.BlockSpec((ts, fc), lambda i, j: (i, j))
    wsp = pl.BlockSpec((D_MODEL, fc), lambda i, j: (0, j))
    return pl.pallas_call(
        body, name=name, grid=(nt, nc),
        in_specs=[tok, tok, row, hid, hid, wsp, wsp],
        out_specs=[tok, tok, row],
        out_shape=[
            jax.ShapeDtypeStruct((s_len, D_MODEL), F32),
            jax.ShapeDtypeStruct((s_len, D_MODEL), BF16),
            jax.ShapeDtypeStruct((1, D_MODEL), F32),
        ],
        scratch_shapes=[pltpu.VMEM((ts, D_MODEL), F32)],
        compiler_params=_cparams("arbitrary", "arbitrary"),
    )(x, dy, gain, dg, du, wg, wu)


def _wgrad(a, b, name, tk, tn):
    s_len, k_dim = a.shape
    n_dim = b.shape[1]
    ts = min(512, s_len)

    def body(a_ref, b_ref, o_ref):
        s = pl.program_id(2)
        p = _dot_tn(a_ref[...].astype(BF16), b_ref[...].astype(BF16))

        @pl.when(s == 0)
        def _():
            o_ref[...] = p

        @pl.when(s > 0)
        def _():
            o_ref[...] += p

    return pl.pallas_call(
        body, name=name, grid=(k_dim // tk, n_dim // tn, s_len // ts),
        in_specs=[
            pl.BlockSpec((ts, tk), lambda k, n, s: (s, k)),
            pl.BlockSpec((ts, tn), lambda k, n, s: (s, n)),
        ],
        out_specs=pl.BlockSpec((tk, tn), lambda k, n, s: (k, n)),
        out_shape=jax.ShapeDtypeStruct((k_dim, n_dim), F32),
        compiler_params=_cparams("parallel", "parallel", "arbitrary"),
    )(a, b)


def _head_group_matrix():
    r = lax.broadcasted_iota(jnp.int32, (ATT_W, ATT_W), 0) // HEAD_DIM
    c = lax.broadcasted_iota(jnp.int32, (ATT_W, ATT_W), 1) // HEAD_DIM
    return (r == c).astype(BF16)


def _mix_fwd(x, gain, w_in, bias, qg, kg, name):
    s_len = x.shape[0]
    ts = min(256, s_len)
    nt = s_len // ts
    gmat = _head_group_matrix()

    def body(x_ref, gain_ref, w_ref, bias_ref, qg_ref, kg_ref, gm_ref,
             h_ref, fqr_ref, fkr_ref, fqn_ref, fkn_ref, fv_ref, logf_ref, f_ref, ft_ref,
             sq_ref, sk_ref, sv_ref, gf_ref, gs_ref, kmax_ref, carry):
        i = pl.program_id(0)
        xf = x_ref[...]
        h = ((xf * _rms_rinv(xf)) * gain_ref[...]).astype(BF16)
        h_ref[...] = h
        gm = gm_ref[...]

        def proj(lo, n):
            return _dot(h, w_ref[:, lo:lo + n])

        def headnorm(raw, g):
            ms = _dot_split(raw * raw, gm, 3) * (1.0 / HEAD_DIM)
            return ((raw * lax.rsqrt(ms + EPS)) * g).astype(BF16)

        fq = proj(C_FQ, ATT_W)
        fqr_ref[...] = fq
        fqn_ref[...] = headnorm(fq, qg_ref[...])
        fk = proj(C_FK, ATT_W)
        fkr_ref[...] = fk
        fkn = headnorm(fk, kg_ref[...])
        fkn_ref[...] = fkn
        kn2 = jnp.max(_dot_split(jnp.square(fkn.astype(F32)), gm, 3), axis=0, keepdims=True)

        @pl.when(i == 0)
        def _():
            kmax_ref[...] = kn2

        @pl.when(i > 0)
        def _():
            kmax_ref[...] = jnp.maximum(kmax_ref[...], kn2)
        fv_ref[...] = proj(C_FV, ATT_W).astype(BF16)
        sq_ref[...] = proj(C_SQ, ATT_W).astype(BF16)
        sk_ref[...] = proj(C_SK, ATT_W).astype(BF16)
        sv_ref[...] = proj(C_SV, ATT_W).astype(BF16)
        gf_ref[...] = proj(C_GF, D_MODEL)
        gs_ref[...] = proj(C_GS, D_MODEL)

        fl = proj(C_FL, LANES) + bias_ref[...]
        lane = lax.broadcasted_iota(jnp.int32, fl.shape, 1)
        logf = jnp.where(lane < N_HEADS, jnp.minimum(fl, 0.0) - _softplus_neg_abs(fl), 0.0)
        logf_ref[...] = logf

        @pl.when(i == 0)
        def _():
            carry[...] = jnp.zeros_like(carry)

        r = lax.broadcasted_iota(jnp.int32, (ts, ts), 0)
        c = lax.broadcasted_iota(jnp.int32, (ts, ts), 1)
        tri = (r >= c).astype(BF16)
        f_tile = _dot_split_left(tri, logf, 3) + carry[...]
        f_ref[...] = f_tile
        ft_ref[...] = f_tile.T[:N_HEADS, :]
        carry[...] = f_tile[ts - 1:ts, :]

    tok = lambda w: pl.BlockSpec((ts, w), lambda i: (i, 0))
    full = lambda a: pl.BlockSpec(a.shape, lambda i: (0, 0))
    f32o = lambda w: jax.ShapeDtypeStruct((s_len, w), F32)
    b16o = lambda w: jax.ShapeDtypeStruct((s_len, w), BF16)
    return pl.pallas_call(
        body, name=name, grid=(nt,),
        in_specs=[tok(D_MODEL), full(gain), full(w_in), full(bias), full(qg), full(kg), full(gmat)],
        out_specs=[
            tok(D_MODEL), tok(ATT_W), tok(ATT_W), tok(ATT_W), tok(ATT_W), tok(ATT_W), tok(LANES), tok(LANES),
            pl.BlockSpec((N_HEADS, ts), lambda i: (0, i)),
            tok(ATT_W), tok(ATT_W), tok(ATT_W), tok(D_MODEL), tok(D_MODEL),
            pl.BlockSpec((1, ATT_W), lambda i: (0, 0)),
        ],
        out_shape=[
            b16o(D_MODEL), f32o(ATT_W), f32o(ATT_W), b16o(ATT_W), b16o(ATT_W), b16o(ATT_W), f32o(LANES), f32o(LANES),
            jax.ShapeDtypeStruct((N_HEADS, s_len), F32),
            b16o(ATT_W), b16o(ATT_W), b16o(ATT_W), f32o(D_MODEL), f32o(D_MODEL),
            jax.ShapeDtypeStruct((1, ATT_W), F32),
        ],
        scratch_shapes=[pltpu.VMEM((1, LANES), F32)],
        compiler_params=_cparams("arbitrary"),
    )(x, gain, w_in, bias, qg, kg, gmat)


ATT_T = 256
ATT_ROWS = 2
EXP_ZERO = 104.0


def _att_tiling(s_len):
    t = min(ATT_T, s_len)
    nr = min(ATT_ROWS, s_len // t)
    return t, nr, s_len // (t * nr)


def _pair_specs(s_len, tq):
    qblk = pl.BlockSpec((tq, LANES), lambda hp, i: (i, hp))
    kvfull = pl.BlockSpec((s_len, LANES), lambda hp, i: (0, hp))
    return qblk, kvfull


def _walk_tiles(i, nr, load, sub, flush, more, init):
    base = i * nr
    carries = list(init)
    for r in range(nr):
        for kk in range(r, -1, -1):
            (carries[r],), side = sub([r], load(base + kk), [carries[r]], kk == r)
            flush(base + kk, side)

    def cond(state):
        return jnp.logical_and(state[0] < base, state[1] > 0)

    def step(state):
        n, _, cs = state
        kb = base - 1 - n
        cs, side = sub(list(range(nr)), load(kb), list(cs), False)
        flush(kb, side)
        return n + 1, more(cs, kb - 1), tuple(cs)

    return lax.while_loop(cond, step, (jnp.int32(0), more(carries, base - 1), tuple(carries)))[2]


def _stack(parts):
    return parts[0] if len(parts) == 1 else jnp.concatenate(parts, axis=0)


def _stacked_halves(x, lo):
    z = jnp.zeros_like(x)
    return jnp.concatenate([jnp.where(lo, x, z), jnp.where(lo, z, x)], axis=0)


def _fox_qk_bound(qst_r, km_ref, t):
    km = km_ref[...]
    out = []
    for j in (0, 1):
        qf = qst_r[j * t:(j + 1) * t, :].astype(F32)
        qn = jnp.sqrt(jnp.sum(qf * qf, axis=1, keepdims=True))
        out.append(qn * jnp.sqrt(km[:, j * HEAD_DIM:j * HEAD_DIM + 1]) * 1.001 + 1.0)
    return out


def _fox_more(nr, t, hp, ft_ref, qkb, fq, level):
    def more(carries, kb):
        k0 = pl.multiple_of(jnp.maximum(kb, 0) * t, t)
        worst = None
        for j in (0, 1):
            f_new = jnp.min(ft_ref[pl.ds(2 * hp + j, 1), pl.ds(k0, t)], axis=1, keepdims=True)
            for r in range(nr):
                gap = (qkb[r][j] + fq[r][j] - f_new) - level(carries, r, j)
                worst = gap if worst is None else jnp.maximum(worst, gap)
        return (jnp.max(worst) > -EXP_ZERO).astype(jnp.int32)
    return more


def _fox_fwd(q, k, v, f_col, f_row, kmax, name):
    s_len = q.shape[0]
    t, nr, nq = _att_tiling(s_len)

    def body(q_ref, k_ref, v_ref, f_ref, ft_ref, km_ref, y_ref, lse_ref):
        hp = pl.program_id(0)
        i = pl.program_id(1)
        lane = lax.broadcasted_iota(jnp.int32, (t, LANES), 1)
        lo = lane < HEAD_DIM
        causal = lax.broadcasted_iota(jnp.int32, (t, t), 0) >= lax.broadcasted_iota(jnp.int32, (t, t), 1)
        rows = [pl.ds(r * t, t) for r in range(nr)]
        qst = [_stacked_halves(q_ref[rw, :] * jnp.asarray(HEAD_DIM ** -0.5, BF16), lo) for rw in rows]
        q_all = _stack(qst)
        fq = [[jnp.sum(jnp.where(lane == 2 * hp + j, f_ref[rw, :], 0.0), axis=1, keepdims=True) for j in (0, 1)]
              for rw in rows]

        def load(kb):
            k0 = pl.multiple_of(kb * t, t)
            frow = [ft_ref[pl.ds(2 * hp + j, 1), pl.ds(k0, t)] for j in (0, 1)]
            return k_ref[pl.ds(k0, t), :], v_ref[pl.ds(k0, t), :], frow

        def sub(rs, tiles, carries, masked):
            kblk, vblk, frow = tiles
            z = _dot_nt(q_all if len(rs) == nr else qst[rs[0]], kblk)
            ps, stats = [], []
            for n, (r, j) in enumerate((r, j) for r in range(len(rs)) for j in (0, 1)):
                m, l, _ = carries[r]
                s = z[n * t:(n + 1) * t, :] + (fq[rs[r]][j] - frow[j])
                if masked:
                    s = jnp.where(causal, s, -1e30)
                mj = jnp.maximum(m[j], jnp.max(s, axis=1, keepdims=True))
                aj = jnp.exp(m[j] - mj)
                p = jnp.exp(s - mj)
                stats.append((mj, aj, aj * l[j] + jnp.sum(p, axis=1, keepdims=True)))
                ps.append(p.astype(BF16))
            pv = _dot(_stack(ps), vblk)
            out = []
            for r in range(len(rs)):
                (m0, a0, l0), (m1, a1, l1) = stats[2 * r], stats[2 * r + 1]
                acc = carries[r][2]
                acc = (acc[0] * a0 + pv[2 * r * t:(2 * r + 1) * t, :], acc[1] * a1 + pv[(2 * r + 1) * t:(2 * r + 2) * t, :])
                out.append(((m0, m1), (l0, l1), acc))
            return out, None

        neg = jnp.full((t, 1), -1e30, F32)
        zero = jnp.zeros((t, 1), F32)
        zacc = jnp.zeros((t, LANES), F32)
        init = [((neg, neg), (zero, zero), (zacc, zacc))] * nr
        qkb = [_fox_qk_bound(qs, km_ref, t) for qs in qst]
        more = _fox_more(nr, t, hp, ft_ref, qkb, fq, lambda carries, r, j: carries[r][0][j])
        out = _walk_tiles(i, nr, load, sub, lambda kb, side: None, more, init)
        for rw, (m, l, acc) in zip(rows, out):
            y_ref[rw, :] = jnp.where(lo, acc[0] / l[0], acc[1] / l[1]).astype(BF16)
            lse_ref[0, rw, :] = jnp.where(lo, m[0] + jnp.log(l[0]), m[1] + jnp.log(l[1]))

    qblk, kvfull = _pair_specs(s_len, t * nr)
    return pl.pallas_call(
        body, name=name, grid=(N_HEADS // 2, nq),
        in_specs=[qblk, kvfull, kvfull,
                  pl.BlockSpec((t * nr, LANES), lambda hp, i: (i, 0)),
                  pl.BlockSpec((N_HEADS, s_len), lambda hp, i: (0, 0)),
                  pl.BlockSpec((1, LANES), lambda hp, i: (0, hp))],
        out_specs=[qblk, pl.BlockSpec((1, t * nr, LANES), lambda hp, i: (hp, i, 0))],
        out_shape=[jax.ShapeDtypeStruct((s_len, ATT_W), BF16),
                   jax.ShapeDtypeStruct((N_HEADS // 2, s_len, LANES), F32)],
        compiler_params=_cparams("parallel", "parallel"),
    )(q, k, v, f_col, f_row, kmax)


def _fox_bwd(q, k, v, dy, y, lse, f_col, f_row, kmax, name):
    s_len = q.shape[0]
    t, nr, nq = _att_tiling(s_len)

    def body(q_ref, k_ref, v_ref, dy_ref, y_ref, lse_ref, f_ref, ft_ref, km_ref,
             dq_ref, dk_ref, dv_ref, dft_ref, dfq_ref):
        hp = pl.program_id(0)
        i = pl.program_id(1)

        @pl.when(i == 0)
        def _():
            dk_ref[...] = jnp.zeros_like(dk_ref)
            dv_ref[...] = jnp.zeros_like(dv_ref)
            dft_ref[...] = jnp.zeros_like(dft_ref)

        lane = lax.broadcasted_iota(jnp.int32, (t, LANES), 1)
        lo = lane < HEAD_DIM
        causal = lax.broadcasted_iota(jnp.int32, (t, t), 0) >= lax.broadcasted_iota(jnp.int32, (t, t), 1)
        rows = [pl.ds(r * t, t) for r in range(nr)]
        qst, dyst, delta, lse, fq = [], [], [], [], []
        for rw in rows:
            qst.append(_stacked_halves(q_ref[rw, :] * jnp.asarray(HEAD_DIM ** -0.5, BF16), lo))
            dyb = dy_ref[rw, :]
            dyst.append(_stacked_halves(dyb, lo))
            prod = dyb.astype(F32) * y_ref[rw, :].astype(F32)
            delta.append([jnp.sum(jnp.where(lo, prod, 0.0), axis=1, keepdims=True),
                          jnp.sum(jnp.where(lo, 0.0, prod), axis=1, keepdims=True)])
            lse_b = lse_ref[0, rw, :]
            lse.append([lse_b[:, 0:1], lse_b[:, HEAD_DIM:HEAD_DIM + 1]])
            fq.append([jnp.sum(jnp.where(lane == 2 * hp + j, f_ref[rw, :], 0.0), axis=1, keepdims=True)
                       for j in (0, 1)])

        q_all, dy_all = _stack(qst), _stack(dyst)

        def load(kb):
            k0 = pl.multiple_of(kb * t, t)
            frow = [ft_ref[pl.ds(2 * hp + j, 1), pl.ds(k0, t)] for j in (0, 1)]
            return k_ref[pl.ds(k0, t), :], v_ref[pl.ds(k0, t), :], frow

        def sub(rs, tiles, carries, masked):
            kblk, vblk, frow = tiles
            qs, dys = (q_all, dy_all) if len(rs) == nr else (qst[rs[0]], dyst[rs[0]])
            z = _dot_nt(qs, kblk)
            dp = _dot_nt(dys, vblk)
            pb, dsb, rsum, col = [], [], [], [None, None]
            for n, (r, j) in enumerate((r, j) for r in range(len(rs)) for j in (0, 1)):
                sl = slice(n * t, (n + 1) * t)
                s = z[sl, :] + (fq[rs[r]][j] - frow[j])
                p = jnp.exp(s - lse[rs[r]][j])
                if masked:
                    p = jnp.where(causal, p, 0.0)
                ds = p * (dp[sl, :] - delta[rs[r]][j])
                c = jnp.sum(ds, axis=0, keepdims=True)
                col[j] = c if col[j] is None else col[j] + c
                rsum.append(carries[r][1][j] + jnp.sum(ds, axis=1, keepdims=True))
                pb.append(p.astype(BF16))
                dsb.append(ds.astype(BF16))
            p_all, ds_all = _stack(pb), _stack(dsb)
            dqs = _dot(ds_all, kblk)
            out = []
            for r in range(len(rs)):
                dq = carries[r][0]
                dq = (dq[0] + dqs[2 * r * t:(2 * r + 1) * t, :], dq[1] + dqs[(2 * r + 1) * t:(2 * r + 2) * t, :])
                out.append((dq, (rsum[2 * r], rsum[2 * r + 1])))
            return out, (_dot_tn(ds_all, qs), _dot_tn(p_all, dys), col)

        def flush(kb, side):
            k0 = pl.multiple_of(kb * t, t)
            dk_ref[pl.ds(k0, t), :] += side[0]
            dv_ref[pl.ds(k0, t), :] += side[1]
            for j in (0, 1):
                dft_ref[0, pl.ds(j, 1), pl.ds(k0, t)] -= side[2][j]

        zero = jnp.zeros((t, 1), F32)
        zacc = jnp.zeros((t, LANES), F32)
        qkb = [_fox_qk_bound(qs, km_ref, t) for qs in qst]
        more = _fox_more(nr, t, hp, ft_ref, qkb, fq, lambda carries, r, j: lse[r][j])
        out = _walk_tiles(i, nr, load, sub, flush, more, [((zacc, zacc), (zero, zero))] * nr)
        for rw, (dq, rs) in zip(rows, out):
            dq_ref[rw, :] = jnp.where(lo, dq[0], dq[1]) * (HEAD_DIM ** -0.5)
            dfq_ref[0, rw, :] = jnp.where(lo, rs[0], rs[1])

    qblk, kvfull = _pair_specs(s_len, t * nr)
    return pl.pallas_call(
        body, name=name, grid=(N_HEADS // 2, nq),
        in_specs=[qblk, kvfull, kvfull, qblk, qblk,
                  pl.BlockSpec((1, t * nr, LANES), lambda hp, i: (hp, i, 0)),
                  pl.BlockSpec((t * nr, LANES), lambda hp, i: (i, 0)),
                  pl.BlockSpec((N_HEADS, s_len), lambda hp, i: (0, 0)),
                  pl.BlockSpec((1, LANES), lambda hp, i: (0, hp))],
        out_specs=[qblk, kvfull, kvfull, pl.BlockSpec((1, 8, s_len), lambda hp, i: (hp, 0, 0)),
                   pl.BlockSpec((1, t * nr, LANES), lambda hp, i: (hp, i, 0))],
        out_shape=[jax.ShapeDtypeStruct((s_len, ATT_W), F32)] * 3
        + [jax.ShapeDtypeStruct((N_HEADS // 2, 8, s_len), F32),
           jax.ShapeDtypeStruct((N_HEADS // 2, s_len, LANES), F32)],
        compiler_params=_cparams("arbitrary", "arbitrary"),
    )(q, k, v, dy, y, lse, f_col, f_row, kmax)


def _sb_more(carries, kb):
    worst = None
    for cr in carries:
        for cj in cr[0]:
            worst = cj if worst is None else jnp.maximum(worst, cj)
    return (jnp.max(worst) > -EXP_ZERO).astype(jnp.int32)


def _stacked_split_dot(slabs, m, parts):
    split = [_split(x, parts) for x in slabs]
    acc = None
    for p in range(parts):
        d = _dot(_stack([s[p] for s in split]), m)
        acc = d if acc is None else acc + d
    return acc


def _sb_weights(z, c, strict, upper, t):
    logs = []
    for n in range(z.shape[0] // t):
        zn = z[n * t:(n + 1) * t, :]
        sp = _softplus_neg_abs(zn)
        l1m = jnp.minimum(-zn, 0.0) - sp
        if strict is not None:
            l1m = jnp.where(strict, l1m, 0.0)
        logs.append((jnp.minimum(zn, 0.0) - sp, l1m))
    suf = _stacked_split_dot([l1m for _, l1m in logs], upper, 2)
    out = []
    for n, (logb, l1m) in enumerate(logs):
        after = c[n] + suf[n * t:(n + 1) * t, :]
        a = jnp.exp(logb + after)
        if strict is not None:
            a = jnp.where(strict, a, 0.0)
        out.append((logb, a, after[:, 0:1] + l1m[:, 0:1]))
    return out


def _sb_fwd(q, k, v, name):
    s_len = q.shape[0]
    t, nr, nq = _att_tiling(s_len)

    def body(q_ref, k_ref, v_ref, y_ref, yf_ref):
        i = pl.program_id(1)
        lane = lax.broadcasted_iota(jnp.int32, (t, LANES), 1)
        lo = lane < HEAD_DIM
        ri = lax.broadcasted_iota(jnp.int32, (t, t), 0)
        ci = lax.broadcasted_iota(jnp.int32, (t, t), 1)
        strict = ci < ri
        upper = (ri > ci).astype(BF16)
        rows = [pl.ds(r * t, t) for r in range(nr)]
        qst = [_stacked_halves(q_ref[rw, :] * jnp.asarray(HEAD_DIM ** -0.5, BF16), lo) for rw in rows]
        q_all = _stack(qst)

        def load(kb):
            k0 = pl.multiple_of(kb * t, t)
            return k_ref[pl.ds(k0, t), :], v_ref[pl.ds(k0, t), :]

        def sub(rs, tiles, carries, masked):
            kblk, vblk = tiles
            z = _dot_nt(q_all if len(rs) == nr else qst[rs[0]], kblk)
            c = [carries[r][0][j] for r in range(len(rs)) for j in (0, 1)]
            w = _sb_weights(z, c, strict if masked else None, upper, t)
            pv = _dot(_stack([a.astype(BF16) for _, a, _ in w]), vblk)
            out = []
            for r in range(len(rs)):
                acc = carries[r][1]
                acc = (acc[0] + pv[2 * r * t:(2 * r + 1) * t, :], acc[1] + pv[(2 * r + 1) * t:(2 * r + 2) * t, :])
                out.append(((w[2 * r][2], w[2 * r + 1][2]), acc))
            return out, None

        zero = jnp.zeros((t, 1), F32)
        zacc = jnp.zeros((t, LANES), F32)
        out = _walk_tiles(i, nr, load, sub, lambda kb, side: None, _sb_more, [((zero, zero), (zacc, zacc))] * nr)
        for rw, (_, acc) in zip(rows, out):
            y = jnp.where(lo, acc[0], acc[1])
            y_ref[rw, :] = y.astype(BF16)
            yf_ref[rw, :] = y

    qblk, kvfull = _pair_specs(s_len, t * nr)
    return pl.pallas_call(
        body, name=name, grid=(N_HEADS // 2, nq),
        in_specs=[qblk, kvfull, kvfull],
        out_specs=[qblk, qblk],
        out_shape=[jax.ShapeDtypeStruct((s_len, ATT_W), BF16), jax.ShapeDtypeStruct((s_len, ATT_W), F32)],
        compiler_params=_cparams("parallel", "parallel"),
    )(q, k, v)


def _sb_bwd(q, k, v, dy, yf, name):
    s_len = q.shape[0]
    t, nr, nq = _att_tiling(s_len)

    def body(q_ref, k_ref, v_ref, dy_ref, yf_ref, dq_ref, dk_ref, dv_ref):
        i = pl.program_id(1)

        @pl.when(i == 0)
        def _():
            dk_ref[...] = jnp.zeros_like(dk_ref)
            dv_ref[...] = jnp.zeros_like(dv_ref)

        lane = lax.broadcasted_iota(jnp.int32, (t, LANES), 1)
        lo = lane < HEAD_DIM
        ri = lax.broadcasted_iota(jnp.int32, (t, t), 0)
        ci = lax.broadcasted_iota(jnp.int32, (t, t), 1)
        strict = ci < ri
        upper = (ri > ci).astype(BF16)
        upper_incl = (ri >= ci).astype(BF16)
        rows = [pl.ds(r * t, t) for r in range(nr)]
        qst, dyst, delta = [], [], []
        for rw in rows:
            qst.append(_stacked_halves(q_ref[rw, :] * jnp.asarray(HEAD_DIM ** -0.5, BF16), lo))
            dyb = dy_ref[rw, :]
            dyst.append(_stacked_halves(dyb, lo))
            prod = dyb.astype(F32) * yf_ref[rw, :]
            delta.append([jnp.sum(jnp.where(lo, prod, 0.0), axis=1, keepdims=True),
                          jnp.sum(jnp.where(lo, 0.0, prod), axis=1, keepdims=True)])
        q_all, dy_all = _stack(qst), _stack(dyst)

        def load(kb):
            k0 = pl.multiple_of(kb * t, t)
            return k_ref[pl.ds(k0, t), :], v_ref[pl.ds(k0, t), :]

        def sub(rs, tiles, carries, masked):
            kblk, vblk = tiles
            qs, dys = (q_all, dy_all) if len(rs) == nr else (qst[rs[0]], dyst[rs[0]])
            slabs = [(r, j) for r in range(len(rs)) for j in (0, 1)]
            z = _dot_nt(qs, kblk)
            w = _sb_weights(z, [carries[r][0][j] for r, j in slabs], strict if masked else None, upper, t)
            da = _dot_nt(dys, vblk)
            ab = [a.astype(BF16) for _, a, _ in w]
            dl = [ab[n].astype(F32) * da[n * t:(n + 1) * t, :] for n in range(len(slabs))]
            tail = _stacked_split_dot(dl, upper_incl, 2)
            dzb, e_new = [], []
            for n, (r, j) in enumerate(slabs):
                tl = tail[n * t:(n + 1) * t, :]
                e = carries[r][1][j]
                dl1m = (delta[rs[r]][j] - e) - tl
                e_new.append(e + tl[:, 0:1])
                dz = dl[n] - jnp.exp(w[n][0]) * (dl[n] + dl1m)
                if masked:
                    dz = jnp.where(strict, dz, 0.0)
                dzb.append(dz.astype(BF16))
            a_all, dz_all = _stack(ab), _stack(dzb)
            dqs = _dot(dz_all, kblk)
            out = []
            for r in range(len(rs)):
                dq = carries[r][2]
                dq = (dq[0] + dqs[2 * r * t:(2 * r + 1) * t, :], dq[1] + dqs[(2 * r + 1) * t:(2 * r + 2) * t, :])
                out.append(((w[2 * r][2], w[2 * r + 1][2]), (e_new[2 * r], e_new[2 * r + 1]), dq))
            return out, (_dot_tn(dz_all, qs), _dot_tn(a_all, dys))

        def flush(kb, side):
            k0 = pl.multiple_of(kb * t, t)
            dk_ref[pl.ds(k0, t), :] += side[0]
            dv_ref[pl.ds(k0, t), :] += side[1]

        zero = jnp.zeros((t, 1), F32)
        zacc = jnp.zeros((t, LANES), F32)
        out = _walk_tiles(i, nr, load, sub, flush, _sb_more, [((zero, zero), (zero, zero), (zacc, zacc))] * nr)
        for rw, (_, _, dq) in zip(rows, out):
            dq_ref[rw, :] = jnp.where(lo, dq[0], dq[1]) * (HEAD_DIM ** -0.5)

    qblk, kvfull = _pair_specs(s_len, t * nr)
    return pl.pallas_call(
        body, name=name, grid=(N_HEADS // 2, nq),
        in_specs=[qblk, kvfull, kvfull, qblk, qblk],
        out_specs=[qblk, kvfull, kvfull],
        out_shape=[jax.ShapeDtypeStruct((s_len, ATT_W), F32)] * 3,
        compiler_params=_cparams("arbitrary", "arbitrary"),
    )(q, k, v, dy, yf)


def _merge_fwd(x, yf, ys, gf, gs, wbf, wbs, wo, name):
    s_len = x.shape[0]
    ts = min(512, s_len)

    def body(x_ref, yf_ref, ys_ref, gf_ref, gs_ref, wbf_ref, wbs_ref, wo_ref, o_ref):
        merged = (_sigmoid(gf_ref[...]) * _dot(yf_ref[...], wbf_ref[...])
                  + _sigmoid(gs_ref[...]) * _dot(ys_ref[...], wbs_ref[...]))
        o_ref[...] = x_ref[...] + _dot(merged.astype(BF16), wo_ref[...])

    tok = lambda w: pl.BlockSpec((ts, w), lambda i: (i, 0))
    full = lambda a: pl.BlockSpec(a.shape, lambda i: (0, 0))
    return pl.pallas_call(
        body, name=name, grid=(s_len // ts,),
        in_specs=[tok(D_MODEL), tok(ATT_W), tok(ATT_W), tok(D_MODEL), tok(D_MODEL), full(wbf), full(wbs), full(wo)],
        out_specs=tok(D_MODEL),
        out_shape=jax.ShapeDtypeStruct((s_len, D_MODEL), F32),
        compiler_params=_cparams("parallel"),
    )(x, yf, ys, gf, gs, wbf, wbs, wo)


def _merge_bwd(dx, yf, ys, gf, gs, wbf, wbs, wo, name):
    s_len = dx.shape[0]
    ts = min(512, s_len)

    def body(dx_ref, yf_ref, ys_ref, gf_ref, gs_ref, wbf_ref, wbs_ref, wo_ref,
             dyf_ref, dys_ref, dgf_ref, dgs_ref, dbf_ref, dbs_ref, mg_ref):
        bf = _dot(yf_ref[...], wbf_ref[...])
        bs = _dot(ys_ref[...], wbs_ref[...])
        sf = _sigmoid(gf_ref[...])
        ss = _sigmoid(gs_ref[...])
        mg_ref[...] = (sf * bf + ss * bs).astype(BF16)
        dm = _dot_nt(dx_ref[...].astype(BF16), wo_ref[...])
        dbf = (dm * sf).astype(BF16)
        dbs = (dm * ss).astype(BF16)
        dbf_ref[...] = dbf
        dbs_ref[...] = dbs
        dgf_ref[...] = (dm * bf * (sf * (1.0 - sf))).astype(BF16)
        dgs_ref[...] = (dm * bs * (ss * (1.0 - ss))).astype(BF16)
        dyf_ref[...] = _dot_nt(dbf, wbf_ref[...]).astype(BF16)
        dys_ref[...] = _dot_nt(dbs, wbs_ref[...]).astype(BF16)

    tok = lambda w: pl.BlockSpec((ts, w), lambda i: (i, 0))
    full = lambda a: pl.BlockSpec(a.shape, lambda i: (0, 0))
    b16o = lambda w: jax.ShapeDtypeStruct((s_len, w), BF16)
    return pl.pallas_call(
        body, name=name, grid=(s_len // ts,),
        in_specs=[tok(D_MODEL), tok(ATT_W), tok(ATT_W), tok(D_MODEL), tok(D_MODEL), full(wbf), full(wbs), full(wo)],
        out_specs=[tok(ATT_W), tok(ATT_W)] + [tok(D_MODEL)] * 5,
        out_shape=[b16o(ATT_W), b16o(ATT_W)] + [b16o(D_MODEL)] * 5,
        compiler_params=_cparams("parallel"),
    )(dx, yf, ys, gf, gs, wbf, wbs, wo)


def _mix_bwd(x, dx_in, gain, w_in, fqr, fkr, dfqn, dfkn, qg, kg, dfv, df_col, logf, dsq, dsk, dsv, dgf, dgs, name):
    s_len = x.shape[0]
    ts = min(256, s_len)
    nt = s_len // ts
    gmat = _head_group_matrix()

    def body(x_ref, dxi_ref, gain_ref, w_ref, fqr_ref, fkr_ref, dfqn_ref, dfkn_ref, qg_ref, kg_ref, gm_ref,
             dfv_ref, df_ref, logf_ref, dsq_ref, dsk_ref, dsv_ref, dgf_ref, dgs_ref,
             dp_ref, dx_ref, dgain_ref, dqg_ref, dkg_ref, dbias_ref, carry):
        i = pl.program_id(0)

        @pl.when(i == 0)
        def _():
            carry[...] = jnp.zeros_like(carry)
            dgain_ref[...] = jnp.zeros_like(dgain_ref)
            dqg_ref[...] = jnp.zeros_like(dqg_ref)
            dkg_ref[...] = jnp.zeros_like(dkg_ref)
            dbias_ref[...] = jnp.zeros_like(dbias_ref)

        gm = gm_ref[...]

        def headnorm_bwd(raw, dout, g, dg_ref):
            ms = _dot_split(raw * raw, gm, 3) * (1.0 / HEAD_DIM)
            r = lax.rsqrt(ms + EPS)
            nrm = raw * r
            dg_ref[...] += jnp.sum(dout * nrm, axis=0, keepdims=True)
            dn = dout * g
            mean_h = _dot_split(dn * nrm, gm, 3) * (1.0 / HEAD_DIM)
            return r * (dn - nrm * mean_h)

        dp_ref[:, C_FQ:C_FQ + ATT_W] = headnorm_bwd(fqr_ref[...], dfqn_ref[...], qg_ref[...], dqg_ref).astype(BF16)
        dp_ref[:, C_FK:C_FK + ATT_W] = headnorm_bwd(fkr_ref[...], dfkn_ref[...], kg_ref[...], dkg_ref).astype(BF16)
        dp_ref[:, C_FV:C_FV + ATT_W] = dfv_ref[...].astype(BF16)
        dp_ref[:, C_SQ:C_SQ + ATT_W] = dsq_ref[...].astype(BF16)
        dp_ref[:, C_SK:C_SK + ATT_W] = dsk_ref[...].astype(BF16)
        dp_ref[:, C_SV:C_SV + ATT_W] = dsv_ref[...].astype(BF16)
        dp_ref[:, C_GF:C_GF + D_MODEL] = dgf_ref[...]
        dp_ref[:, C_GS:C_GS + D_MODEL] = dgs_ref[...]

        r_ = lax.broadcasted_iota(jnp.int32, (ts, ts), 0)
        c_ = lax.broadcasted_iota(jnp.int32, (ts, ts), 1)
        rev = (c_ >= r_).astype(BF16)
        dlogf = _dot_split_left(rev, df_ref[...], 3) + carry[...]
        carry[...] = dlogf[0:1, :]
        lane = lax.broadcasted_iota(jnp.int32, (ts, LANES), 1)
        dfl = jnp.where(lane < N_HEADS, dlogf * (1.0 - jnp.exp(logf_ref[...])), 0.0)
        dbias_ref[...] += jnp.sum(dfl, axis=0, keepdims=True)
        dp_ref[:, C_FL:C_FL + LANES] = dfl.astype(BF16)
        dp_ref[:, C_FL + LANES:C_SQ] = jnp.zeros((ts, C_SQ - C_FL - LANES), BF16)

        dh = _dot_nt(dp_ref[...], w_ref[...])
        xf = x_ref[...]
        r = _rms_rinv(xf)
        xhat = xf * r
        dgain_ref[...] += jnp.sum(dh * xhat, axis=0, keepdims=True)
        dn = dh * gain_ref[...]
        dx_ref[...] = dxi_ref[...] + r * (dn - xhat * jnp.mean(dn * xhat, axis=-1, keepdims=True))

    tok = lambda w: pl.BlockSpec((ts, w), lambda i: (nt - 1 - i, 0))
    full = lambda a: pl.BlockSpec(a.shape, lambda i: (0, 0))
    row = lambda w: pl.BlockSpec((1, w), lambda i: (0, 0))
    return pl.pallas_call(
        body, name=name, grid=(nt,),
        in_specs=[tok(D_MODEL), tok(D_MODEL), full(gain), full(w_in), tok(ATT_W), tok(ATT_W), tok(ATT_W), tok(ATT_W),
                  full(qg), full(kg), full(gmat), tok(ATT_W), tok(LANES), tok(LANES), tok(ATT_W), tok(ATT_W), tok(ATT_W),
                  tok(D_MODEL), tok(D_MODEL)],
        out_specs=[tok(IN_PAD), tok(D_MODEL), row(D_MODEL), row(ATT_W), row(ATT_W), row(LANES)],
        out_shape=[jax.ShapeDtypeStruct((s_len, IN_PAD), BF16), jax.ShapeDtypeStruct((s_len, D_MODEL), F32),
                   jax.ShapeDtypeStruct((1, D_MODEL), F32), jax.ShapeDtypeStruct((1, ATT_W), F32),
                   jax.ShapeDtypeStruct((1, ATT_W), F32), jax.ShapeDtypeStruct((1, LANES), F32)],
        scratch_shapes=[pltpu.VMEM((1, LANES), F32)],
        compiler_params=_cparams("arbitrary"),
    )(x, dx_in, gain, w_in, fqr, fkr, dfqn, dfkn, qg, kg, gmat, dfv, df_col, logf, dsq, dsk, dsv, dgf, dgs)


def _ple_loss(x, p, tgt, gain, wpg, wpp, name):
    s_len = x.shape[0]
    ts = min(512, s_len)

    def body(x_ref, p_ref, t_ref, gain_ref, wpg_ref, wpp_ref, dx_ref, n_ref, ds_ref, dpp_ref, dgain_ref, loss_ref):
        i = pl.program_id(0)

        @pl.when(i == 0)
        def _():
            dgain_ref[...] = jnp.zeros_like(dgain_ref)
            loss_ref[...] = jnp.zeros_like(loss_ref)

        xf = x_ref[...]
        r = _rms_rinv(xf)
        n = xf * r
        hn = (n * gain_ref[...]).astype(BF16)
        n_ref[...] = hn
        sg = _sigmoid(_dot(hn, wpg_ref[...]))
        pp = _dot(p_ref[...].astype(BF16), wpp_ref[...])
        err = (xf + sg * pp) - t_ref[...]
        sq = jnp.sum(jnp.sum(err * err, axis=1, keepdims=True), axis=0, keepdims=True)
        loss_ref[...] += (0.5 / D_MODEL) * sq
        dout = err * (1.0 / D_MODEL)
        dpp_ref[...] = (dout * sg).astype(BF16)
        ds = (dout * pp * (sg * (1.0 - sg))).astype(BF16)
        ds_ref[...] = ds
        dhn = _dot_nt(ds, wpg_ref[...])
        dgain_ref[...] += jnp.sum(dhn * n, axis=0, keepdims=True)
        dn = dhn * gain_ref[...]
        dx_ref[...] = dout + r * (dn - n * jnp.mean(dn * n, axis=-1, keepdims=True))

    tok = lambda w: pl.BlockSpec((ts, w), lambda i: (i, 0))
    full = lambda a: pl.BlockSpec(a.shape, lambda i: (0, 0))
    return pl.pallas_call(
        body, name=name, grid=(s_len // ts,),
        in_specs=[tok(D_MODEL), tok(PLE_DIM), tok(D_MODEL), full(gain), full(wpg), full(wpp)],
        out_specs=[tok(D_MODEL), tok(D_MODEL), tok(D_MODEL), tok(D_MODEL),
                   pl.BlockSpec((1, D_MODEL), lambda i: (0, 0)), pl.BlockSpec((8, LANES), lambda i: (0, 0))],
        out_shape=[jax.ShapeDtypeStruct((s_len, D_MODEL), F32), jax.ShapeDtypeStruct((s_len, D_MODEL), BF16),
                   jax.ShapeDtypeStruct((s_len, D_MODEL), BF16), jax.ShapeDtypeStruct((s_len, D_MODEL), BF16),
                   jax.ShapeDtypeStruct((1, D_MODEL), F32), jax.ShapeDtypeStruct((8, LANES), F32)],
        compiler_params=_cparams("arbitrary"),
    )(x, p, tgt, gain, wpg, wpp)


def _exchange(x, name, broadcast):
    shape = x.shape if broadcast else x.shape[1:]

    def body(x_ref, out_ref, send_sems, recv_sems, local_sem):
        mx, my, mc = lax.axis_index("x"), lax.axis_index("y"), lax.axis_index("c")
        me = 4 * mx + 2 * my + mc

        def src(idx):
            return x_ref if broadcast else x_ref.at[idx]

        local = pltpu.make_async_copy(src(me), out_ref.at[me], local_sem)
        local.start()
        copies = []
        for k in range(1, N_DEV):
            px = (1 - mx) if k & 4 else mx
            py = (1 - my) if k & 2 else my
            pc = (1 - mc) if k & 1 else mc
            peer = 4 * px + 2 * py + pc
            cp = pltpu.make_async_remote_copy(
                src_ref=src(peer), dst_ref=out_ref.at[me], send_sem=send_sems.at[k - 1], recv_sem=recv_sems.at[k - 1],
                device_id=(px, py, pc), device_id_type=MESH)
            cp.start()
            copies.append((cp, peer, (px, py, pc)))
        for k, (cp, peer, pid) in enumerate(copies):
            pltpu.make_async_remote_copy(
                src_ref=src(peer), dst_ref=out_ref.at[peer], send_sem=send_sems.at[k], recv_sem=recv_sems.at[k],
                device_id=pid, device_id_type=MESH).wait_recv()
        for cp, _, _ in copies:
            cp.wait_send()
        local.wait()

    return pl.pallas_call(
        body, name=name,
        in_specs=[pl.BlockSpec(memory_space=pl.ANY)],
        out_specs=pl.BlockSpec(memory_space=pl.ANY),
        out_shape=jax.ShapeDtypeStruct((N_DEV,) + tuple(shape), x.dtype),
        scratch_shapes=[pltpu.SemaphoreType.DMA((N_DEV - 1,)), pltpu.SemaphoreType.DMA((N_DEV - 1,)),
                        pltpu.SemaphoreType.DMA],
        compiler_params=pltpu.CompilerParams(has_side_effects=True),
    )(x)


def _adamw_math(w, g, m, v):
    m2 = ADAM_B1 * m + (1.0 - ADAM_B1) * g
    v2 = ADAM_B2 * v + (1.0 - ADAM_B2) * (g * g)
    m_hat = m2 / (1.0 - ADAM_B1 ** ADAM_STEP)
    v_hat = v2 / (1.0 - ADAM_B2 ** ADAM_STEP)
    delta = -ADAM_LR * (m_hat / (jnp.sqrt(v_hat) + ADAM_EPS) + ADAM_WD * w)
    return delta, m2, v2


def _adamw(parts, w, m, v, name, tr):
    rows, cols = w.shape

    def body(p_ref, w_ref, m_ref, v_ref, g_ref, d_ref, m2_ref, v2_ref):
        g = p_ref[0].astype(F32)
        for s in range(1, N_DEV):
            g = g + p_ref[s].astype(F32)
        g_ref[...] = g
        d_ref[...], m2_ref[...], v2_ref[...] = _adamw_math(w_ref[...], g, m_ref[...], v_ref[...])

    blk = pl.BlockSpec((tr, cols), lambda i: (i, 0))
    return pl.pallas_call(
        body, name=name, grid=(rows // tr,),
        in_specs=[pl.BlockSpec((N_DEV, tr, cols), lambda i: (0, i, 0)), blk, blk, blk],
        out_specs=[blk] * 4,
        out_shape=[jax.ShapeDtypeStruct((rows, cols), F32)] * 4,
        compiler_params=_cparams("parallel"),
    )(parts, w, m, v)


def _to_rows(a):
    return a.reshape(-1, D_MODEL)


def _pack(pieces, dtype):
    out = []
    for name, rows in PACK:
        if name == "pad":
            out.append(jnp.zeros((rows, D_MODEL), dtype))
            continue
        r = _to_rows(pieces[name]).astype(dtype)
        if r.shape[0] != rows:
            r = jnp.pad(r, ((0, rows - r.shape[0]), (0, 0)))
        out.append(r)
    return jnp.concatenate(out, axis=0)


COL_SHARDED = {"ffn1_w_gate": (D_MODEL, D_FF), "ffn1_w_up": (D_MODEL, D_FF), "w_in": (D_MODEL, IN_REAL),
               "w_branch_fox": (ATT_W, D_MODEL), "w_branch_sb": (ATT_W, D_MODEL),
               "ffn2_w_gate": (D_MODEL, D_FF), "ffn2_w_up": (D_MODEL, D_FF), "w_ple_proj": (PLE_DIM, D_MODEL)}
ROW_SHARDED = {"ffn1_w_down": (D_FF, D_MODEL), "w_out": (D_MODEL, D_MODEL), "ffn2_w_down": (D_FF, D_MODEL),
               "w_ple_gate": (D_MODEL, D_MODEL)}


def _real_rows(name):
    return W_IN_ROWS if name == "w_in" else PACK_OFF[name][1]


def _unpack_full(gathered, name):
    off, _ = PACK_OFF[name]
    g = gathered[:, off:off + _real_rows(name), :]
    if name in ROW_SHARDED:
        return g.reshape(ROW_SHARDED[name])
    k_dim, n_dim = COL_SHARDED[name]
    return g.reshape(N_DEV, k_dim, n_dim // N_DEV).transpose(1, 0, 2).reshape(k_dim, n_dim)


def _to_chunks(full, name):
    if name in ROW_SHARDED:
        return full.reshape(N_DEV, -1, D_MODEL)
    k_dim, n_dim = COL_SHARDED[name]
    return full.reshape(k_dim, N_DEV, n_dim // N_DEV).transpose(1, 0, 2).reshape(N_DEV, -1, D_MODEL)


def _pack_chunks(grads):
    out = []
    for name, rows in PACK:
        if name == "pad":
            out.append(jnp.zeros((N_DEV, rows, D_MODEL), BF16))
            continue
        c = _to_chunks(grads[name], name).astype(BF16)
        if c.shape[1] != rows:
            c = jnp.pad(c, ((0, 0), (0, rows - c.shape[1]), (0, 0)))
        out.append(c)
    return jnp.concatenate(out, axis=1)


def _unpack_shard(packed, name, shape):
    off, _ = PACK_OFF[name]
    return packed[off:off + _real_rows(name), :].reshape(shape)


WEIGHT_NAMES = ['ffn1_norm', 'ffn1_w_gate', 'ffn1_w_up', 'ffn1_w_down', 'mix_norm', 'w_in', 'forget_bias', 'q_norm',
                'k_norm', 'w_branch_fox', 'w_branch_sb', 'w_out', 'ffn2_norm', 'ffn2_w_gate', 'ffn2_w_up',
                'ffn2_w_down', 'ple_norm', 'w_ple_gate', 'w_ple_proj']
SMALL_NAMES = ('ffn1_norm', 'mix_norm', 'ffn2_norm', 'ple_norm', 'q_norm', 'k_norm', 'forget_bias')
Q_OFF, K_OFF, B_OFF, LOSS_OFF = 0, HEAD_DIM, 2 * HEAD_DIM, 2 * HEAD_DIM + N_HEADS


def _pack_small(vals, loss=None):
    tail = [vals['q_norm'].reshape(1, -1), vals['k_norm'].reshape(1, -1), vals['forget_bias'].reshape(1, -1)]
    used = LOSS_OFF
    if loss is not None:
        tail.append(loss.reshape(1, 1))
        used += 1
    tail.append(jnp.zeros((1, D_MODEL - used), F32))
    rows = [vals[n].reshape(1, D_MODEL) for n in SMALL_NAMES[:4]] + [jnp.concatenate(tail, axis=1)]
    rows.append(jnp.zeros((SMALL_ROWS - len(rows), D_MODEL), F32))
    return jnp.concatenate(rows, axis=0)


def _unpack_small(packed, name, shape):
    if name in SMALL_NAMES[:4]:
        return packed[SMALL_NAMES.index(name)].reshape(shape)
    off, n = {'q_norm': (Q_OFF, HEAD_DIM), 'k_norm': (K_OFF, HEAD_DIM), 'forget_bias': (B_OFF, N_HEADS)}[name]
    return packed[4, off:off + n].reshape(shape)


def _pad_w_in(w):
    z = lambda n: jnp.zeros((D_MODEL, n), w.dtype)
    return jnp.concatenate([w[:, :FL_REAL_END], z(C_SQ - FL_REAL_END), w[:, FL_REAL_END:]], axis=1)


def _unpad_w_in(w):
    return jnp.concatenate([w[:, :FL_REAL_END], w[:, C_SQ:]], axis=1)


def _local_step(x, p, tgt, wts, small):
    row = lambda a: a.reshape(1, -1).astype(F32)
    g_ffn1, g_mix, g_ffn2, g_ple = (row(small[n]) for n in SMALL_NAMES[:4])
    qg = jnp.tile(row(small['q_norm']), (1, N_HEADS))
    kg = jnp.tile(row(small['k_norm']), (1, N_HEADS))
    bias = jnp.pad(row(small['forget_bias']), ((0, 0), (0, LANES - N_HEADS)))
    w_in = _pad_w_in(wts['w_in'])

    x1, g1, u1 = _ffn_fwd(x, g_ffn1, wts['ffn1_w_gate'], wts['ffn1_w_up'], wts['ffn1_w_down'], "ffn1_fwd")
    (hmix, fqr, fkr, fqn, fkn, fv, logf, f_col, f_row, sq, sk, sv, gf, gs, kmax) = _mix_fwd(
        x1, g_mix, w_in, bias, qg, kg, "mix_fwd")
    y_fox, lse = _fox_fwd(fqn, fkn, fv, f_col, f_row, kmax, "fox_fwd")
    y_sb, y_sb32 = _sb_fwd(sq, sk, sv, "sb_fwd")
    x2 = _merge_fwd(x1, y_fox, y_sb, gf, gs, wts['w_branch_fox'], wts['w_branch_sb'], wts['w_out'], "merge_fwd")
    x3, g2, u2 = _ffn_fwd(x2, g_ffn2, wts['ffn2_w_gate'], wts['ffn2_w_up'], wts['ffn2_w_down'], "ffn2_fwd")
    dx3, n_ple, ds_ple, dpp, dg_ple, loss = _ple_loss(x3, p, tgt, g_ple, wts['w_ple_gate'], wts['w_ple_proj'], "ple_loss")

    grads = {}
    dg2, du2, act2 = _ffn_bwd_hidden(dx3, g2, u2, wts['ffn2_w_down'], "ffn2_bwd_hidden")
    dx2, h2, dg_ffn2 = _ffn_bwd_input(x2, dx3, g_ffn2, dg2, du2, wts['ffn2_w_gate'], wts['ffn2_w_up'], "ffn2_bwd_input")
    dyf, dys, dgf, dgs, dbf, dbs, merged = _merge_bwd(
        dx2, y_fox, y_sb, gf, gs, wts['w_branch_fox'], wts['w_branch_sb'], wts['w_out'], "merge_bwd")
    dfqn, dfkn, dfv, dft, dfq = _fox_bwd(fqn, fkn, fv, dyf, y_fox, lse, f_col, f_row, kmax, "fox_bwd")
    dsq, dsk, dsv = _sb_bwd(sq, sk, sv, dys, y_sb32, "sb_bwd")
    s_len = x.shape[0]
    df_heads = dft[:, :2, :] + jnp.stack([dfq[:, :, 0], dfq[:, :, HEAD_DIM]], axis=1)
    df_col = jnp.pad(df_heads.reshape(N_HEADS, s_len).T, ((0, 0), (0, LANES - N_HEADS)))
    dproj, dx1, dg_mix, dqg, dkg, dbias = _mix_bwd(
        x1, dx2, g_mix, w_in, fqr, fkr, dfqn, dfkn, qg, kg, dfv, df_col, logf, dsq, dsk, dsv, dgf, dgs, "mix_bwd")
    dg1, du1, act1 = _ffn_bwd_hidden(dx1, g1, u1, wts['ffn1_w_down'], "ffn1_bwd_hidden")
    dx0, h1, dg_ffn1 = _ffn_bwd_input(x, dx1, g_ffn1, dg1, du1, wts['ffn1_w_gate'], wts['ffn1_w_up'], "ffn1_bwd_input")

    half = D_FF // 2
    grads['ffn1_w_gate'] = _wgrad(h1, dg1, "dw_ffn1_gate", D_MODEL, half)
    grads['ffn1_w_up'] = _wgrad(h1, du1, "dw_ffn1_up", D_MODEL, half)
    grads['ffn1_w_down'] = _wgrad(act1, dx1, "dw_ffn1_down", half, D_MODEL)
    grads['ffn2_w_gate'] = _wgrad(h2, dg2, "dw_ffn2_gate", D_MODEL, half)
    grads['ffn2_w_up'] = _wgrad(h2, du2, "dw_ffn2_up", D_MODEL, half)
    grads['ffn2_w_down'] = _wgrad(act2, dx3, "dw_ffn2_down", half, D_MODEL)
    grads['w_in'] = _unpad_w_in(_wgrad(hmix, dproj, "dw_in", D_MODEL, IN_PAD // 3))
    grads['w_branch_fox'] = _wgrad(y_fox, dbf, "dw_branch_fox", ATT_W, D_MODEL)
    grads['w_branch_sb'] = _wgrad(y_sb, dbs, "dw_branch_sb", ATT_W, D_MODEL)
    grads['w_out'] = _wgrad(merged, dx2, "dw_out", D_MODEL, D_MODEL)
    grads['w_ple_gate'] = _wgrad(n_ple, ds_ple, "dw_ple_gate", D_MODEL, D_MODEL)
    grads['w_ple_proj'] = _wgrad(p, dpp, "dw_ple_proj", PLE_DIM, D_MODEL)

    fold = lambda a: a.reshape(N_HEADS, HEAD_DIM).sum(axis=0).reshape(1, HEAD_DIM)
    small_g = {'ffn1_norm': dg_ffn1, 'mix_norm': dg_mix, 'ffn2_norm': dg_ffn2, 'ple_norm': dg_ple,
               'q_norm': fold(dqg), 'k_norm': fold(dkg), 'forget_bias': dbias[:, :N_HEADS]}
    return loss[0, 0], dx0, grads, small_g


def kernel(x, p, ffn1_norm, ffn1_w_gate, ffn1_w_up, ffn1_w_down, mix_norm, w_in, forget_bias, q_norm, k_norm, w_branch_fox, w_branch_sb, w_out, ffn2_norm, ffn2_w_gate, ffn2_w_up, ffn2_w_down, ple_norm, w_ple_gate, w_ple_proj, loss_target, m_ffn1_norm, m_ffn1_w_gate, m_ffn1_w_up, m_ffn1_w_down, m_mix_norm, m_w_in, m_forget_bias, m_q_norm, m_k_norm, m_w_branch_fox, m_w_branch_sb, m_w_out, m_ffn2_norm, m_ffn2_w_gate, m_ffn2_w_up, m_ffn2_w_down, m_ple_norm, m_w_ple_gate, m_w_ple_proj, v_ffn1_norm, v_ffn1_w_gate, v_ffn1_w_up, v_ffn1_w_down, v_mix_norm, v_w_in, v_forget_bias, v_q_norm, v_k_norm, v_w_branch_fox, v_w_branch_sb, v_w_out, v_ffn2_norm, v_ffn2_w_gate, v_ffn2_w_up, v_ffn2_w_down, v_ple_norm, v_w_ple_gate, v_w_ple_proj):
    args = dict(locals())
    w = {n: args[n][0] for n in WEIGHT_NAMES}
    m = {n: args["m_" + n][0] for n in WEIGHT_NAMES}
    v = {n: args["v_" + n][0] for n in WEIGHT_NAMES}
    big = [n for n, _ in PACK if n != "pad"]

    gathered = _exchange(_pack(w, BF16), "gather_weights", True)
    wts = {n: _unpack_full(gathered, n) for n in big}

    loss, dx, grads, small_g = _local_step(x[0], p[0, 0], loss_target[0], wts, w)

    parts = _exchange(_pack_chunks(grads), "scatter_grads", False)
    g_p, d_p, m_p, v_p = _adamw(parts, _pack(w, F32), _pack(m, F32), _pack(v, F32), "adamw", PACK_ROWS // 10)
    small_parts = _exchange(_pack_small(small_g, loss), "gather_small", True)
    sw, sm, sv = (_pack_small(t) for t in (w, m, v))
    g_s, d_s, m_s, v_s = _adamw(small_parts, sw, sm, sv, "adamw_small", SMALL_ROWS)

    outs = [g_s[4, LOSS_OFF], dx.reshape(x.shape)]
    for packed_big, packed_small in ((g_p, g_s), (d_p, d_s), (m_p, m_s), (v_p, v_s)):
        for n in WEIGHT_NAMES:
            shape = args[n].shape
            if n in SMALL_NAMES:
                outs.append(_unpack_small(packed_small, n, shape))
            else:
                outs.append(_unpack_shard(packed_big, n, shape))
    return tuple(outs)
```

```python
import jax
import jax.numpy as jnp
from jax import lax
from jax.experimental import pallas as pl
from jax.experimental.pallas import tpu as pltpu

F32 = jnp.float32
BF16 = jnp.bfloat16

D_MODEL = 1024
D_FF = 2816
N_HEADS = 8
HEAD_DIM = 64
ATT_W = N_HEADS * HEAD_DIM
PLE_DIM = 256
EPS = 1e-6
N_DEV = 8
MESH = pl.DeviceIdType.MESH

LANES = 128
V7X_SCOPED_VMEM_BYTES = 56 * 1024 * 1024

C_FQ, C_FK, C_FV, C_FL = 0, 512, 1024, 1536
C_SQ, C_SK, C_SV, C_GF, C_GS = 1792, 2304, 2816, 3328, 4352
IN_PAD = 5376
IN_REAL = 5128
FL_REAL_END = 1544

ADAM_LR = 0.001
ADAM_B1 = 0.9
ADAM_B2 = 0.999
ADAM_EPS = 1e-08
ADAM_WD = 0.01
ADAM_STEP = 10

PACK_GROUPS = (
    (("ffn1_w_gate", 352), ("ffn1_w_up", 352), ("ffn1_w_down", 352)),
    (("w_in", 656), ("w_branch_fox", 64), ("w_branch_sb", 64), ("w_out", 128)),
    (("ffn2_w_gate", 352), ("ffn2_w_up", 352), ("ffn2_w_down", 352), ("w_ple_gate", 128), ("w_ple_proj", 32)),
)
ADAM_TILE_ROWS = (528, 304, 304)
PACK_OFF = {}
for _grp in PACK_GROUPS:
    _o = 0
    for _n, _r in _grp:
        PACK_OFF[_n] = (_o, _r)
        _o += _r
W_IN_ROWS = 641

SMALL_ROWS = 8


def _cparams(*sem):
    return pltpu.CompilerParams(dimension_semantics=sem, vmem_limit_bytes=V7X_SCOPED_VMEM_BYTES)


def _dot(a, b):
    return jnp.dot(a, b, preferred_element_type=F32)


def _dot_nt(a, b):
    return lax.dot_general(a, b, (((1,), (1,)), ((), ())), preferred_element_type=F32)


def _dot_tn(a, b):
    return lax.dot_general(a, b, (((0,), (0,)), ((), ())), preferred_element_type=F32)


def _split(x, parts):
    out = []
    r = x
    for _ in range(parts):
        p = r.astype(BF16)
        out.append(p)
        r = r - p.astype(F32)
    return out


def _dot_split(x, m, parts):
    acc = None
    for p in _split(x, parts):
        t = _dot(p, m)
        acc = t if acc is None else acc + t
    return acc


def _dot_split_left(m, x, parts):
    acc = None
    for p in _split(x, parts):
        t = _dot(m, p)
        acc = t if acc is None else acc + t
    return acc


def _rms_rinv(xf):
    return lax.rsqrt(jnp.mean(xf * xf, axis=-1, keepdims=True) + EPS)


def _sigmoid(x):
    return 1.0 / (1.0 + jnp.exp(-x))


def _softplus_neg_abs(z):
    return jnp.log(1.0 + jnp.exp(-jnp.abs(z)))


def _ffn_fwd(x, gain, wg, wu, wd, name, ride=None):
    s_len = x.shape[0]
    ts = min(512, s_len)
    fc = D_FF // 2
    nt, nc = s_len // ts, D_FF // fc

    def body(x_ref, gain_ref, wg_ref, wu_ref, wd_ref, y_ref, g_ref, u_ref, h_ref, acc_scr):
        j = pl.program_id(1)

        @pl.when(j == 0)
        def _():
            xf = x_ref[...]
            h_ref[...] = ((xf * _rms_rinv(xf)) * gain_ref[...]).astype(BF16)
            acc_scr[...] = jnp.zeros_like(acc_scr)

        h = h_ref[...]
        g = _dot(h, wg_ref[...])
        u = _dot(h, wu_ref[...])
        g_ref[...] = g.astype(BF16)
        u_ref[...] = u.astype(BF16)
        a = (g * _sigmoid(g) * u).astype(BF16)
        acc_scr[...] += _dot(a, wd_ref[...])

        @pl.when(j == nc - 1)
        def _():
            y_ref[...] = x_ref[...] + 0.5 * acc_scr[...]

    tok = pl.BlockSpec((ts, D_MODEL), lambda i, j: (i, 0))
    hid = pl.BlockSpec((ts, fc), lambda i, j: (i, j))
    return _ride_call(
        body, name, (nt, nc),
        [tok, pl.BlockSpec((1, D_MODEL), lambda i, j: (0, 0)),
         pl.BlockSpec((D_MODEL, fc), lambda i, j: (0, j)),
         pl.BlockSpec((D_MODEL, fc), lambda i, j: (0, j)),
         pl.BlockSpec((fc, D_MODEL), lambda i, j: (j, 0))],
        [tok, hid, hid, tok],
        [jax.ShapeDtypeStruct((s_len, D_MODEL), F32), jax.ShapeDtypeStruct((s_len, D_FF), BF16),
         jax.ShapeDtypeStruct((s_len, D_FF), BF16), jax.ShapeDtypeStruct((s_len, D_MODEL), BF16)],
        [pltpu.VMEM((ts, D_MODEL), F32)], ("parallel", "arbitrary"), (x, gain, wg, wu, wd), ride)


def _ffn_bwd_hidden(dy, g, u, wd, name, ride=None):
    s_len = dy.shape[0]
    ts = min(512, s_len)
    fc = D_FF // 2
    nt, nc = s_len // ts, D_FF // fc

    def body(dy_ref, g_ref, u_ref, wd_ref, dg_ref, du_ref, act_ref):
        da = 0.5 * _dot_nt(dy_ref[...].astype(BF16), wd_ref[...])
        gf = g_ref[...].astype(F32)
        uf = u_ref[...].astype(F32)
        sg = _sigmoid(gf)
        silu = gf * sg
        dg_ref[...] = (da * uf * (sg * (1.0 + gf * (1.0 - sg)))).astype(BF16)
        du_ref[...] = (da * silu).astype(BF16)
        act_ref[...] = (0.5 * silu * uf).astype(BF16)

    hid = pl.BlockSpec((ts, fc), lambda c, t: (t, c))
    return _ride_call(
        body, name, (nc, nt),
        [pl.BlockSpec((ts, D_MODEL), lambda c, t: (t, 0)), hid, hid, pl.BlockSpec((fc, D_MODEL), lambda c, t: (c, 0))],
        [hid, hid, hid], [jax.ShapeDtypeStruct((s_len, D_FF), BF16)] * 3, [], ("parallel", "parallel"),
        (dy, g, u, wd), ride)


def _ffn_bwd_input(x, dy, gain, dg, du, wg, wu, name, ride=None):
    s_len = x.shape[0]
    ts = min(512, s_len)
    fc = D_FF // 2
    nt, nc = s_len // ts, D_FF // fc

    def body(x_ref, dy_ref, gain_ref, dg_ref, du_ref, wg_ref, wu_ref, dx_ref, dgain_ref, acc):
        i = pl.program_id(0)
        j = pl.program_id(1)
        part = _dot_nt(dg_ref[...], wg_ref[...]) + _dot_nt(du_ref[...], wu_ref[...])

        @pl.when(j == 0)
        def _():
            acc[...] = part

        @pl.when(j > 0)
        def _():
            acc[...] += part

        @pl.when(j == nc - 1)
        def _():
            xf = x_ref[...]
            r = _rms_rinv(xf)
            xhat = xf * r
            dh = acc[...]
            dgp = jnp.sum(dh * xhat, axis=0, keepdims=True)

            @pl.when(i == 0)
            def _():
                dgain_ref[...] = dgp

            @pl.when(i > 0)
            def _():
                dgain_ref[...] += dgp

            dn = dh * gain_ref[...]
            dx_ref[...] = dy_ref[...] + r * (dn - xhat * jnp.mean(dn * xhat, axis=-1, keepdims=True))

    tok = pl.BlockSpec((ts, D_MODEL), lambda i, j: (i, 0))
    row = pl.BlockSpec((1, D_MODEL), lambda i, j: (0, 0))
    hid = pl.BlockSpec((ts, fc), lambda i, j: (i, j))
    wsp = pl.BlockSpec((D_MODEL, fc), lambda i, j: (0, j))
    return _ride_call(
        body, name, (nt, nc), [tok, tok, row, hid, hid, wsp, wsp], [tok, row],
        [jax.ShapeDtypeStruct((s_len, D_MODEL), F32), jax.ShapeDtypeStruct((1, D_MODEL), F32)],
        [pltpu.VMEM((ts, D_MODEL), F32)], ("arbitrary", "arbitrary"), (x, dy, gain, dg, du, wg, wu), ride)


def _wgrad(a, b, name, tk, tn):
    s_len, k_dim = a.shape
    n_dim = b.shape[1]
    ts = min(512, s_len)
    ns = s_len // ts

    def body(a_ref, b_ref, o_ref, acc):
        s = pl.program_id(2)
        p = _dot_tn(a_ref[...].astype(BF16), b_ref[...].astype(BF16))

        @pl.when(s == 0)
        def _():
            acc[...] = p

        @pl.when(s > 0)
        def _():
            acc[...] += p

        @pl.when(s == ns - 1)
        def _():
            o_ref[...] = acc[...].astype(BF16)

    return pl.pallas_call(
        body, name=name, grid=(k_dim // tk, n_dim // tn, ns),
        in_specs=[
            pl.BlockSpec((ts, tk), lambda k, n, s: (s, k)),
            pl.BlockSpec((ts, tn), lambda k, n, s: (s, n)),
        ],
        out_specs=pl.BlockSpec((tk, tn), lambda k, n, s: (k, n)),
        out_shape=jax.ShapeDtypeStruct((k_dim, n_dim), BF16),
        scratch_shapes=[pltpu.VMEM((tk, tn), F32)],
        compiler_params=_cparams("parallel", "parallel", "arbitrary"),
    )(a, b)


def _head_group_matrix():
    r = lax.broadcasted_iota(jnp.int32, (ATT_W, ATT_W), 0) // HEAD_DIM
    c = lax.broadcasted_iota(jnp.int32, (ATT_W, ATT_W), 1) // HEAD_DIM
    return (r == c).astype(BF16)


def _mix_fwd(x, gain, w_in, bias, qg, kg, name, ride=None):
    s_len = x.shape[0]
    ts = min(256, s_len)
    nt = s_len // ts
    gmat = _head_group_matrix()

    def body(x_ref, gain_ref, w_ref, bias_ref, qg_ref, kg_ref, gm_ref,
             h_ref, fqr_ref, fkr_ref, fqn_ref, fkn_ref, fv_ref, logf_ref, f_ref, ft_ref,
             sq_ref, sk_ref, sv_ref, gf_ref, gs_ref, kmax_ref, carry):
        i = pl.program_id(0)
        xf = x_ref[...]
        h = ((xf * _rms_rinv(xf)) * gain_ref[...]).astype(BF16)
        h_ref[...] = h
        gm = gm_ref[...]

        def proj(lo, n):
            return _dot(h, w_ref[:, lo:lo + n])

        def headnorm(raw, g):
            ms = _dot_split(raw * raw, gm, 3) * (1.0 / HEAD_DIM)
            return ((raw * lax.rsqrt(ms + EPS)) * g).astype(BF16)

        fq = proj(C_FQ, ATT_W)
        fqr_ref[...] = fq
        fqn_ref[...] = headnorm(fq, qg_ref[...])
        fk = proj(C_FK, ATT_W)
        fkr_ref[...] = fk
        fkn = headnorm(fk, kg_ref[...])
        fkn_ref[...] = fkn
        kn2 = jnp.max(_dot_split(jnp.square(fkn.astype(F32)), gm, 3), axis=0, keepdims=True)

        @pl.when(i == 0)
        def _():
            kmax_ref[...] = kn2

        @pl.when(i > 0)
        def _():
            kmax_ref[...] = jnp.maximum(kmax_ref[...], kn2)
        fv_ref[...] = proj(C_FV, ATT_W).astype(BF16)
        sq_ref[...] = proj(C_SQ, ATT_W).astype(BF16)
        sk_ref[...] = proj(C_SK, ATT_W).astype(BF16)
        sv_ref[...] = proj(C_SV, ATT_W).astype(BF16)
        gf_ref[...] = proj(C_GF, D_MODEL)
        gs_ref[...] = proj(C_GS, D_MODEL)

        fl = proj(C_FL, LANES) + bias_ref[...]
        lane = lax.broadcasted_iota(jnp.int32, fl.shape, 1)
        logf = jnp.where(lane < N_HEADS, jnp.minimum(fl, 0.0) - _softplus_neg_abs(fl), 0.0)
        logf_ref[...] = logf

        @pl.when(i == 0)
        def _():
            carry[...] = jnp.zeros_like(carry)

        r = lax.broadcasted_iota(jnp.int32, (ts, ts), 0)
        c = lax.broadcasted_iota(jnp.int32, (ts, ts), 1)
        tri = (r >= c).astype(BF16)
        f_tile = _dot_split_left(tri, logf, 3) + carry[...]
        f_ref[...] = f_tile
        ft_ref[...] = f_tile.T[:N_HEADS, :]
        carry[...] = f_tile[ts - 1:ts, :]

    tok = lambda w: pl.BlockSpec((ts, w), lambda i: (i, 0))
    full = lambda a: pl.BlockSpec(a.shape, lambda i: (0, 0))
    f32o = lambda w: jax.ShapeDtypeStruct((s_len, w), F32)
    b16o = lambda w: jax.ShapeDtypeStruct((s_len, w), BF16)
    return _ride_call(
        body, name, (nt,),
        [tok(D_MODEL), full(gain), full(w_in), full(bias), full(qg), full(kg), full(gmat)],
        [
            tok(D_MODEL), tok(ATT_W), tok(ATT_W), tok(ATT_W), tok(ATT_W), tok(ATT_W), tok(LANES), tok(LANES),
            pl.BlockSpec((N_HEADS, ts), lambda i: (0, i)),
            tok(ATT_W), tok(ATT_W), tok(ATT_W), tok(D_MODEL), tok(D_MODEL),
            pl.BlockSpec((1, ATT_W), lambda i: (0, 0)),
        ],
        [
            b16o(D_MODEL), f32o(ATT_W), f32o(ATT_W), b16o(ATT_W), b16o(ATT_W), b16o(ATT_W), f32o(LANES), f32o(LANES),
            jax.ShapeDtypeStruct((N_HEADS, s_len), F32),
            b16o(ATT_W), b16o(ATT_W), b16o(ATT_W), f32o(D_MODEL), f32o(D_MODEL),
            jax.ShapeDtypeStruct((1, ATT_W), F32),
        ],
        [pltpu.VMEM((1, LANES), F32)], ("arbitrary",), (x, gain, w_in, bias, qg, kg, gmat), ride)


ATT_T = 256
ATT_ROWS = 2
EXP_ZERO = 104.0


def _att_tiling(s_len):
    t = min(ATT_T, s_len)
    nr = min(ATT_ROWS, s_len // t)
    return t, nr, s_len // (t * nr)


def _pair_specs(s_len, tq):
    qblk = pl.BlockSpec((tq, LANES), lambda hp, i: (i, hp))
    kvfull = pl.BlockSpec((s_len, LANES), lambda hp, i: (0, hp))
    return qblk, kvfull


def _walk_tiles(i, nr, load, sub, flush, more, init):
    base = i * nr
    carries = list(init)
    for r in range(nr):
        for kk in range(r, -1, -1):
            (carries[r],), side = sub([r], load(base + kk), [carries[r]], kk == r)
            flush(base + kk, side)

    def cond(state):
        return jnp.logical_and(state[0] < base, state[1] > 0)

    def step(state):
        n, _, cs = state
        kb = base - 1 - n
        cs, side = sub(list(range(nr)), load(kb), list(cs), False)
        flush(kb, side)
        return n + 1, more(cs, kb - 1), tuple(cs)

    return lax.while_loop(cond, step, (jnp.int32(0), more(carries, base - 1), tuple(carries)))[2]


def _stack(parts):
    return parts[0] if len(parts) == 1 else jnp.concatenate(parts, axis=0)


def _stacked_halves(x, lo):
    z = jnp.zeros_like(x)
    return jnp.concatenate([jnp.where(lo, x, z), jnp.where(lo, z, x)], axis=0)


def _fox_qk_bound(qst_r, km_ref, t):
    km = km_ref[...]
    out = []
    for j in (0, 1):
        qf = qst_r[j * t:(j + 1) * t, :].astype(F32)
        qn = jnp.sqrt(jnp.sum(qf * qf, axis=1, keepdims=True))
        out.append(qn * jnp.sqrt(km[:, j * HEAD_DIM:j * HEAD_DIM + 1]) * 1.001 + 1.0)
    return out


def _fox_more(nr, t, hp, ft_ref, qkb, fq, level):
    def more(carries, kb):
        k0 = pl.multiple_of(jnp.maximum(kb, 0) * t, t)
        worst = None
        for j in (0, 1):
            f_new = jnp.min(ft_ref[pl.ds(2 * hp + j, 1), pl.ds(k0, t)], axis=1, keepdims=True)
            for r in range(nr):
                gap = (qkb[r][j] + fq[r][j] - f_new) - level(carries, r, j)
                worst = gap if worst is None else jnp.maximum(worst, gap)
        return (jnp.max(worst) > -EXP_ZERO).astype(jnp.int32)
    return more


def _fox_fwd(q, k, v, f_col, f_row, kmax, name):
    s_len = q.shape[0]
    t, nr, nq = _att_tiling(s_len)

    def body(q_ref, k_ref, v_ref, f_ref, ft_ref, km_ref, y_ref, lse_ref):
        hp = pl.program_id(0)
        i = pl.program_id(1)
        lane = lax.broadcasted_iota(jnp.int32, (t, LANES), 1)
        lo = lane < HEAD_DIM
        causal = lax.broadcasted_iota(jnp.int32, (t, t), 0) >= lax.broadcasted_iota(jnp.int32, (t, t), 1)
        rows = [pl.ds(r * t, t) for r in range(nr)]
        qst = [_stacked_halves(q_ref[rw, :] * jnp.asarray(HEAD_DIM ** -0.5, BF16), lo) for rw in rows]
        q_all = _stack(qst)
        fq = [[jnp.sum(jnp.where(lane == 2 * hp + j, f_ref[rw, :], 0.0), axis=1, keepdims=True) for j in (0, 1)]
              for rw in rows]

        def load(kb):
            k0 = pl.multiple_of(kb * t, t)
            frow = [ft_ref[pl.ds(2 * hp + j, 1), pl.ds(k0, t)] for j in (0, 1)]
            return k_ref[pl.ds(k0, t), :], v_ref[pl.ds(k0, t), :], frow

        def sub(rs, tiles, carries, masked):
            kblk, vblk, frow = tiles
            z = _dot_nt(q_all if len(rs) == nr else qst[rs[0]], kblk)
            ps, stats = [], []
            for n, (r, j) in enumerate((r, j) for r in range(len(rs)) for j in (0, 1)):
                m, l, _ = carries[r]
                s = z[n * t:(n + 1) * t, :] + (fq[rs[r]][j] - frow[j])
                if masked:
                    s = jnp.where(causal, s, -1e30)
                mj = jnp.maximum(m[j], jnp.max(s, axis=1, keepdims=True))
                aj = jnp.exp(m[j] - mj)
                p = jnp.exp(s - mj)
                stats.append((mj, aj, aj * l[j] + jnp.sum(p, axis=1, keepdims=True)))
                ps.append(p.astype(BF16))
            pv = _dot(_stack(ps), vblk)
            out = []
            for r in range(len(rs)):
                (m0, a0, l0), (m1, a1, l1) = stats[2 * r], stats[2 * r + 1]
                acc = carries[r][2]
                acc = (acc[0] * a0 + pv[2 * r * t:(2 * r + 1) * t, :], acc[1] * a1 + pv[(2 * r + 1) * t:(2 * r + 2) * t, :])
                out.append(((m0, m1), (l0, l1), acc))
            return out, None

        neg = jnp.full((t, 1), -1e30, F32)
        zero = jnp.zeros((t, 1), F32)
        zacc = jnp.zeros((t, LANES), F32)
        init = [((neg, neg), (zero, zero), (zacc, zacc))] * nr
        qkb = [_fox_qk_bound(qs, km_ref, t) for qs in qst]
        more = _fox_more(nr, t, hp, ft_ref, qkb, fq, lambda carries, r, j: carries[r][0][j])
        out = _walk_tiles(i, nr, load, sub, lambda kb, side: None, more, init)
        for rw, (m, l, acc) in zip(rows, out):
            y_ref[rw, :] = jnp.where(lo, acc[0] / l[0], acc[1] / l[1]).astype(BF16)
            lse_ref[0, rw, :] = jnp.where(lo, m[0] + jnp.log(l[0]), m[1] + jnp.log(l[1]))

    qblk, kvfull = _pair_specs(s_len, t * nr)
    return pl.pallas_call(
        body, name=name, grid=(N_HEADS // 2, nq),
        in_specs=[qblk, kvfull, kvfull,
                  pl.BlockSpec((t * nr, LANES), lambda hp, i: (i, 0)),
                  pl.BlockSpec((N_HEADS, s_len), lambda hp, i: (0, 0)),
                  pl.BlockSpec((1, LANES), lambda hp, i: (0, hp))],
        out_specs=[qblk, pl.BlockSpec((1, t * nr, LANES), lambda hp, i: (hp, i, 0))],
        out_shape=[jax.ShapeDtypeStruct((s_len, ATT_W), BF16),
                   jax.ShapeDtypeStruct((N_HEADS // 2, s_len, LANES), F32)],
        compiler_params=_cparams("parallel", "parallel"),
    )(q, k, v, f_col, f_row, kmax)


def _fox_bwd(q, k, v, dy, y, lse, f_col, f_row, kmax, name):
    s_len = q.shape[0]
    t, nr, nq = _att_tiling(s_len)

    def body(q_ref, k_ref, v_ref, dy_ref, y_ref, lse_ref, f_ref, ft_ref, km_ref,
             dq_ref, dk_ref, dv_ref, dft_ref, dfq_ref):
        hp = pl.program_id(0)
        i = pl.program_id(1)

        @pl.when(i == 0)
        def _():
            dk_ref[...] = jnp.zeros_like(dk_ref)
            dv_ref[...] = jnp.zeros_like(dv_ref)
            dft_ref[...] = jnp.zeros_like(dft_ref)

        lane = lax.broadcasted_iota(jnp.int32, (t, LANES), 1)
        lo = lane < HEAD_DIM
        causal = lax.broadcasted_iota(jnp.int32, (t, t), 0) >= lax.broadcasted_iota(jnp.int32, (t, t), 1)
        rows = [pl.ds(r * t, t) for r in range(nr)]
        qst, dyst, delta, lse, fq = [], [], [], [], []
        for rw in rows:
            qst.append(_stacked_halves(q_ref[rw, :] * jnp.asarray(HEAD_DIM ** -0.5, BF16), lo))
            dyb = dy_ref[rw, :]
            dyst.append(_stacked_halves(dyb, lo))
            prod = dyb.astype(F32) * y_ref[rw, :].astype(F32)
            delta.append([jnp.sum(jnp.where(lo, prod, 0.0), axis=1, keepdims=True),
                          jnp.sum(jnp.where(lo, 0.0, prod), axis=1, keepdims=True)])
            lse_b = lse_ref[0, rw, :]
            lse.append([lse_b[:, 0:1], lse_b[:, HEAD_DIM:HEAD_DIM + 1]])
            fq.append([jnp.sum(jnp.where(lane == 2 * hp + j, f_ref[rw, :], 0.0), axis=1, keepdims=True)
                       for j in (0, 1)])

        q_all, dy_all = _stack(qst), _stack(dyst)

        def load(kb):
            k0 = pl.multiple_of(kb * t, t)
            frow = [ft_ref[pl.ds(2 * hp + j, 1), pl.ds(k0, t)] for j in (0, 1)]
            return k_ref[pl.ds(k0, t), :], v_ref[pl.ds(k0, t), :], frow

        def sub(rs, tiles, carries, masked):
            kblk, vblk, frow = tiles
            qs, dys = (q_all, dy_all) if len(rs) == nr else (qst[rs[0]], dyst[rs[0]])
            z = _dot_nt(qs, kblk)
            dp = _dot_nt(dys, vblk)
            pb, dsb, rsum, col = [], [], [], [None, None]
            for n, (r, j) in enumerate((r, j) for r in range(len(rs)) for j in (0, 1)):
                sl = slice(n * t, (n + 1) * t)
                s = z[sl, :] + (fq[rs[r]][j] - frow[j])
                p = jnp.exp(s - lse[rs[r]][j])
                if masked:
                    p = jnp.where(causal, p, 0.0)
                ds = p * (dp[sl, :] - delta[rs[r]][j])
                c = jnp.sum(ds, axis=0, keepdims=True)
                col[j] = c if col[j] is None else col[j] + c
                rsum.append(carries[r][1][j] + jnp.sum(ds, axis=1, keepdims=True))
                pb.append(p.astype(BF16))
                dsb.append(ds.astype(BF16))
            p_all, ds_all = _stack(pb), _stack(dsb)
            dqs = _dot(ds_all, kblk)
            out = []
            for r in range(len(rs)):
                dq = carries[r][0]
                dq = (dq[0] + dqs[2 * r * t:(2 * r + 1) * t, :], dq[1] + dqs[(2 * r + 1) * t:(2 * r + 2) * t, :])
                out.append((dq, (rsum[2 * r], rsum[2 * r + 1])))
            return out, (_dot_tn(ds_all, qs), _dot_tn(p_all, dys), col)

        def flush(kb, side):
            k0 = pl.multiple_of(kb * t, t)
            dk_ref[pl.ds(k0, t), :] += side[0]
            dv_ref[pl.ds(k0, t), :] += side[1]
            for j in (0, 1):
                dft_ref[0, pl.ds(j, 1), pl.ds(k0, t)] -= side[2][j]

        zero = jnp.zeros((t, 1), F32)
        zacc = jnp.zeros((t, LANES), F32)
        qkb = [_fox_qk_bound(qs, km_ref, t) for qs in qst]
        more = _fox_more(nr, t, hp, ft_ref, qkb, fq, lambda carries, r, j: lse[r][j])
        out = _walk_tiles(i, nr, load, sub, flush, more, [((zacc, zacc), (zero, zero))] * nr)
        for rw, (dq, rs) in zip(rows, out):
            dq_ref[rw, :] = jnp.where(lo, dq[0], dq[1]) * (HEAD_DIM ** -0.5)
            dfq_ref[0, rw, :] = jnp.where(lo, rs[0], rs[1])

    qblk, kvfull = _pair_specs(s_len, t * nr)
    return pl.pallas_call(
        body, name=name, grid=(N_HEADS // 2, nq),
        in_specs=[qblk, kvfull, kvfull, qblk, qblk,
                  pl.BlockSpec((1, t * nr, LANES), lambda hp, i: (hp, i, 0)),
                  pl.BlockSpec((t * nr, LANES), lambda hp, i: (i, 0)),
                  pl.BlockSpec((N_HEADS, s_len), lambda hp, i: (0, 0)),
                  pl.BlockSpec((1, LANES), lambda hp, i: (0, hp))],
        out_specs=[qblk, kvfull, kvfull, pl.BlockSpec((1, 8, s_len), lambda hp, i: (hp, 0, 0)),
                   pl.BlockSpec((1, t * nr, LANES), lambda hp, i: (hp, i, 0))],
        out_shape=[jax.ShapeDtypeStruct((s_len, ATT_W), F32)] * 3
        + [jax.ShapeDtypeStruct((N_HEADS // 2, 8, s_len), F32),
           jax.ShapeDtypeStruct((N_HEADS // 2, s_len, LANES), F32)],
        compiler_params=_cparams("arbitrary", "arbitrary"),
    )(q, k, v, dy, y, lse, f_col, f_row, kmax)


def _sb_more(carries, kb):
    worst = None
    for cr in carries:
        for cj in cr[0]:
            worst = cj if worst is None else jnp.maximum(worst, cj)
    return (jnp.max(worst) > -EXP_ZERO).astype(jnp.int32)


def _stacked_split_dot(slabs, m, parts):
    split = [_split(x, parts) for x in slabs]
    acc = None
    for p in range(parts):
        d = _dot(_stack([s[p] for s in split]), m)
        acc = d if acc is None else acc + d
    return acc


def _sb_weights(z, c, strict, upper, t):
    logs = []
    for n in range(z.shape[0] // t):
        zn = z[n * t:(n + 1) * t, :]
        sp = _softplus_neg_abs(zn)
        l1m = jnp.minimum(-zn, 0.0) - sp
        if strict is not None:
            l1m = jnp.where(strict, l1m, 0.0)
        logs.append((jnp.minimum(zn, 0.0) - sp, l1m))
    suf = _stacked_split_dot([l1m for _, l1m in logs], upper, 2)
    out = []
    for n, (logb, l1m) in enumerate(logs):
        after = c[n] + suf[n * t:(n + 1) * t, :]
        a = jnp.exp(logb + after)
        if strict is not None:
            a = jnp.where(strict, a, 0.0)
        out.append((logb, a, after[:, 0:1] + l1m[:, 0:1]))
    return out


def _sb_fwd(q, k, v, name):
    s_len = q.shape[0]
    t, nr, nq = _att_tiling(s_len)

    def body(q_ref, k_ref, v_ref, y_ref, yf_ref):
        i = pl.program_id(1)
        lane = lax.broadcasted_iota(jnp.int32, (t, LANES), 1)
        lo = lane < HEAD_DIM
        ri = lax.broadcasted_iota(jnp.int32, (t, t), 0)
        ci = lax.broadcasted_iota(jnp.int32, (t, t), 1)
        strict = ci < ri
        upper = (ri > ci).astype(BF16)
        rows = [pl.ds(r * t, t) for r in range(nr)]
        qst = [_stacked_halves(q_ref[rw, :] * jnp.asarray(HEAD_DIM ** -0.5, BF16), lo) for rw in rows]
        q_all = _stack(qst)

        def load(kb):
            k0 = pl.multiple_of(kb * t, t)
            return k_ref[pl.ds(k0, t), :], v_ref[pl.ds(k0, t), :]

        def sub(rs, tiles, carries, masked):
            kblk, vblk = tiles
            z = _dot_nt(q_all if len(rs) == nr else qst[rs[0]], kblk)
            c = [carries[r][0][j] for r in range(len(rs)) for j in (0, 1)]
            w = _sb_weights(z, c, strict if masked else None, upper, t)
            pv = _dot(_stack([a.astype(BF16) for _, a, _ in w]), vblk)
            out = []
            for r in range(len(rs)):
                acc = carries[r][1]
                acc = (acc[0] + pv[2 * r * t:(2 * r + 1) * t, :], acc[1] + pv[(2 * r + 1) * t:(2 * r + 2) * t, :])
                out.append(((w[2 * r][2], w[2 * r + 1][2]), acc))
            return out, None

        zero = jnp.zeros((t, 1), F32)
        zacc = jnp.zeros((t, LANES), F32)
        out = _walk_tiles(i, nr, load, sub, lambda kb, side: None, _sb_more, [((zero, zero), (zacc, zacc))] * nr)
        for rw, (_, acc) in zip(rows, out):
            y = jnp.where(lo, acc[0], acc[1])
            y_ref[rw, :] = y.astype(BF16)
            yf_ref[rw, :] = y

    qblk, kvfull = _pair_specs(s_len, t * nr)
    return pl.pallas_call(
        body, name=name, grid=(N_HEADS // 2, nq),
        in_specs=[qblk, kvfull, kvfull],
        out_specs=[qblk, qblk],
        out_shape=[jax.ShapeDtypeStruct((s_len, ATT_W), BF16), jax.ShapeDtypeStruct((s_len, ATT_W), F32)],
        compiler_params=_cparams("parallel", "parallel"),
    )(q, k, v)


def _sb_bwd(q, k, v, dy, yf, name):
    s_len = q.shape[0]
    t, nr, nq = _att_tiling(s_len)

    def body(q_ref, k_ref, v_ref, dy_ref, yf_ref, dq_ref, dk_ref, dv_ref):
        i = pl.program_id(1)

        @pl.when(i == 0)
        def _():
            dk_ref[...] = jnp.zeros_like(dk_ref)
            dv_ref[...] = jnp.zeros_like(dv_ref)

        lane = lax.broadcasted_iota(jnp.int32, (t, LANES), 1)
        lo = lane < HEAD_DIM
        ri = lax.broadcasted_iota(jnp.int32, (t, t), 0)
        ci = lax.broadcasted_iota(jnp.int32, (t, t), 1)
        strict = ci < ri
        upper = (ri > ci).astype(BF16)
        upper_incl = (ri >= ci).astype(BF16)
        rows = [pl.ds(r * t, t) for r in range(nr)]
        qst, dyst, delta = [], [], []
        for rw in rows:
            qst.append(_stacked_halves(q_ref[rw, :] * jnp.asarray(HEAD_DIM ** -0.5, BF16), lo))
            dyb = dy_ref[rw, :]
            dyst.append(_stacked_halves(dyb, lo))
            prod = dyb.astype(F32) * yf_ref[rw, :]
            delta.append([jnp.sum(jnp.where(lo, prod, 0.0), axis=1, keepdims=True),
                          jnp.sum(jnp.where(lo, 0.0, prod), axis=1, keepdims=True)])
        q_all, dy_all = _stack(qst), _stack(dyst)

        def load(kb):
            k0 = pl.multiple_of(kb * t, t)
            return k_ref[pl.ds(k0, t), :], v_ref[pl.ds(k0, t), :]

        def sub(rs, tiles, carries, masked):
            kblk, vblk = tiles
            qs, dys = (q_all, dy_all) if len(rs) == nr else (qst[rs[0]], dyst[rs[0]])
            slabs = [(r, j) for r in range(len(rs)) for j in (0, 1)]
            z = _dot_nt(qs, kblk)
            w = _sb_weights(z, [carries[r][0][j] for r, j in slabs], strict if masked else None, upper, t)
            da = _dot_nt(dys, vblk)
            ab = [a.astype(BF16) for _, a, _ in w]
            dl = [ab[n].astype(F32) * da[n * t:(n + 1) * t, :] for n in range(len(slabs))]
            tail = _stacked_split_dot(dl, upper_incl, 2)
            dzb, e_new = [], []
            for n, (r, j) in enumerate(slabs):
                tl = tail[n * t:(n + 1) * t, :]
                e = carries[r][1][j]
                dl1m = (delta[rs[r]][j] - e) - tl
                e_new.append(e + tl[:, 0:1])
                dz = dl[n] - jnp.exp(w[n][0]) * (dl[n] + dl1m)
                if masked:
                    dz = jnp.where(strict, dz, 0.0)
                dzb.append(dz.astype(BF16))
            a_all, dz_all = _stack(ab), _stack(dzb)
            dqs = _dot(dz_all, kblk)
            out = []
            for r in range(len(rs)):
                dq = carries[r][2]
                dq = (dq[0] + dqs[2 * r * t:(2 * r + 1) * t, :], dq[1] + dqs[(2 * r + 1) * t:(2 * r + 2) * t, :])
                out.append(((w[2 * r][2], w[2 * r + 1][2]), (e_new[2 * r], e_new[2 * r + 1]), dq))
            return out, (_dot_tn(dz_all, qs), _dot_tn(a_all, dys))

        def flush(kb, side):
            k0 = pl.multiple_of(kb * t, t)
            dk_ref[pl.ds(k0, t), :] += side[0]
            dv_ref[pl.ds(k0, t), :] += side[1]

        zero = jnp.zeros((t, 1), F32)
        zacc = jnp.zeros((t, LANES), F32)
        out = _walk_tiles(i, nr, load, sub, flush, _sb_more, [((zero, zero), (zero, zero), (zacc, zacc))] * nr)
        for rw, (_, _, dq) in zip(rows, out):
            dq_ref[rw, :] = jnp.where(lo, dq[0], dq[1]) * (HEAD_DIM ** -0.5)

    qblk, kvfull = _pair_specs(s_len, t * nr)
    return pl.pallas_call(
        body, name=name, grid=(N_HEADS // 2, nq),
        in_specs=[qblk, kvfull, kvfull, qblk, qblk],
        out_specs=[qblk, kvfull, kvfull],
        out_shape=[jax.ShapeDtypeStruct((s_len, ATT_W), F32)] * 3,
        compiler_params=_cparams("arbitrary", "arbitrary"),
    )(q, k, v, dy, yf)


def _merge_fwd(x, yf, ys, gf, gs, wbf, wbs, wo, name):
    s_len = x.shape[0]
    ts = min(512, s_len)

    def body(x_ref, yf_ref, ys_ref, gf_ref, gs_ref, wbf_ref, wbs_ref, wo_ref, o_ref):
        merged = (_sigmoid(gf_ref[...]) * _dot(yf_ref[...], wbf_ref[...])
                  + _sigmoid(gs_ref[...]) * _dot(ys_ref[...], wbs_ref[...]))
        o_ref[...] = x_ref[...] + _dot(merged.astype(BF16), wo_ref[...])

    tok = lambda w: pl.BlockSpec((ts, w), lambda i: (i, 0))
    full = lambda a: pl.BlockSpec(a.shape, lambda i: (0, 0))
    return pl.pallas_call(
        body, name=name, grid=(s_len // ts,),
        in_specs=[tok(D_MODEL), tok(ATT_W), tok(ATT_W), tok(D_MODEL), tok(D_MODEL), full(wbf), full(wbs), full(wo)],
        out_specs=tok(D_MODEL),
        out_shape=jax.ShapeDtypeStruct((s_len, D_MODEL), F32),
        compiler_params=_cparams("parallel"),
    )(x, yf, ys, gf, gs, wbf, wbs, wo)


def _merge_bwd(dx, yf, ys, gf, gs, wbf, wbs, wo, name):
    s_len = dx.shape[0]
    ts = min(512, s_len)

    def body(dx_ref, yf_ref, ys_ref, gf_ref, gs_ref, wbf_ref, wbs_ref, wo_ref,
             dyf_ref, dys_ref, dgf_ref, dgs_ref, dbf_ref, dbs_ref, mg_ref):
        bf = _dot(yf_ref[...], wbf_ref[...])
        bs = _dot(ys_ref[...], wbs_ref[...])
        sf = _sigmoid(gf_ref[...])
        ss = _sigmoid(gs_ref[...])
        mg_ref[...] = (sf * bf + ss * bs).astype(BF16)
        dm = _dot_nt(dx_ref[...].astype(BF16), wo_ref[...])
        dbf = (dm * sf).astype(BF16)
        dbs = (dm * ss).astype(BF16)
        dbf_ref[...] = dbf
        dbs_ref[...] = dbs
        dgf_ref[...] = (dm * bf * (sf * (1.0 - sf))).astype(BF16)
        dgs_ref[...] = (dm * bs * (ss * (1.0 - ss))).astype(BF16)
        dyf_ref[...] = _dot_nt(dbf, wbf_ref[...]).astype(BF16)
        dys_ref[...] = _dot_nt(dbs, wbs_ref[...]).astype(BF16)

    tok = lambda w: pl.BlockSpec((ts, w), lambda i: (i, 0))
    full = lambda a: pl.BlockSpec(a.shape, lambda i: (0, 0))
    b16o = lambda w: jax.ShapeDtypeStruct((s_len, w), BF16)
    return pl.pallas_call(
        body, name=name, grid=(s_len // ts,),
        in_specs=[tok(D_MODEL), tok(ATT_W), tok(ATT_W), tok(D_MODEL), tok(D_MODEL), full(wbf), full(wbs), full(wo)],
        out_specs=[tok(ATT_W), tok(ATT_W)] + [tok(D_MODEL)] * 5,
        out_shape=[b16o(ATT_W), b16o(ATT_W)] + [b16o(D_MODEL)] * 5,
        compiler_params=_cparams("parallel"),
    )(dx, yf, ys, gf, gs, wbf, wbs, wo)


def _mix_bwd(x, dx_in, gain, w_in, fqr, fkr, dfqn, dfkn, qg, kg, dfv, df_col, logf, dsq, dsk, dsv, dgf, dgs, name,
             ride=None):
    s_len = x.shape[0]
    ts = min(256, s_len)
    nt = s_len // ts
    gmat = _head_group_matrix()

    def body(x_ref, dxi_ref, gain_ref, w_ref, fqr_ref, fkr_ref, dfqn_ref, dfkn_ref, qg_ref, kg_ref, gm_ref,
             dfv_ref, df_ref, logf_ref, dsq_ref, dsk_ref, dsv_ref, dgf_ref, dgs_ref,
             dp_ref, dx_ref, dgain_ref, dqg_ref, dkg_ref, dbias_ref, carry):
        i = pl.program_id(0)

        @pl.when(i == 0)
        def _():
            carry[...] = jnp.zeros_like(carry)
            dgain_ref[...] = jnp.zeros_like(dgain_ref)
            dqg_ref[...] = jnp.zeros_like(dqg_ref)
            dkg_ref[...] = jnp.zeros_like(dkg_ref)
            dbias_ref[...] = jnp.zeros_like(dbias_ref)

        gm = gm_ref[...]

        def headnorm_bwd(raw, dout, g, dg_ref):
            ms = _dot_split(raw * raw, gm, 3) * (1.0 / HEAD_DIM)
            r = lax.rsqrt(ms + EPS)
            nrm = raw * r
            dg_ref[...] += jnp.sum(dout * nrm, axis=0, keepdims=True)
            dn = dout * g
            mean_h = _dot_split(dn * nrm, gm, 3) * (1.0 / HEAD_DIM)
            return r * (dn - nrm * mean_h)

        dp_ref[:, C_FQ:C_FQ + ATT_W] = headnorm_bwd(fqr_ref[...], dfqn_ref[...], qg_ref[...], dqg_ref).astype(BF16)
        dp_ref[:, C_FK:C_FK + ATT_W] = headnorm_bwd(fkr_ref[...], dfkn_ref[...], kg_ref[...], dkg_ref).astype(BF16)
        dp_ref[:, C_FV:C_FV + ATT_W] = dfv_ref[...].astype(BF16)
        dp_ref[:, C_SQ:C_SQ + ATT_W] = dsq_ref[...].astype(BF16)
        dp_ref[:, C_SK:C_SK + ATT_W] = dsk_ref[...].astype(BF16)
        dp_ref[:, C_SV:C_SV + ATT_W] = dsv_ref[...].astype(BF16)
        dp_ref[:, C_GF:C_GF + D_MODEL] = dgf_ref[...]
        dp_ref[:, C_GS:C_GS + D_MODEL] = dgs_ref[...]

        r_ = lax.broadcasted_iota(jnp.int32, (ts, ts), 0)
        c_ = lax.broadcasted_iota(jnp.int32, (ts, ts), 1)
        rev = (c_ >= r_).astype(BF16)
        dlogf = _dot_split_left(rev, df_ref[...], 3) + carry[...]
        carry[...] = dlogf[0:1, :]
        lane = lax.broadcasted_iota(jnp.int32, (ts, LANES), 1)
        dfl = jnp.where(lane < N_HEADS, dlogf * (1.0 - jnp.exp(logf_ref[...])), 0.0)
        dbias_ref[...] += jnp.sum(dfl, axis=0, keepdims=True)
        dp_ref[:, C_FL:C_FL + LANES] = dfl.astype(BF16)
        dp_ref[:, C_FL + LANES:C_SQ] = jnp.zeros((ts, C_SQ - C_FL - LANES), BF16)

        dh = _dot_nt(dp_ref[...], w_ref[...])
        xf = x_ref[...]
        r = _rms_rinv(xf)
        xhat = xf * r
        dgain_ref[...] += jnp.sum(dh * xhat, axis=0, keepdims=True)
        dn = dh * gain_ref[...]
        dx_ref[...] = dxi_ref[...] + r * (dn - xhat * jnp.mean(dn * xhat, axis=-1, keepdims=True))

    tok = lambda w: pl.BlockSpec((ts, w), lambda i: (nt - 1 - i, 0))
    full = lambda a: pl.BlockSpec(a.shape, lambda i: (0, 0))
    row = lambda w: pl.BlockSpec((1, w), lambda i: (0, 0))
    return _ride_call(
        body, name, (nt,),
        [tok(D_MODEL), tok(D_MODEL), full(gain), full(w_in), tok(ATT_W), tok(ATT_W), tok(ATT_W), tok(ATT_W),
         full(qg), full(kg), full(gmat), tok(ATT_W), tok(LANES), tok(LANES), tok(ATT_W), tok(ATT_W), tok(ATT_W),
         tok(D_MODEL), tok(D_MODEL)],
        [tok(IN_PAD), tok(D_MODEL), row(D_MODEL), row(ATT_W), row(ATT_W), row(LANES)],
        [jax.ShapeDtypeStruct((s_len, IN_PAD), BF16), jax.ShapeDtypeStruct((s_len, D_MODEL), F32),
         jax.ShapeDtypeStruct((1, D_MODEL), F32), jax.ShapeDtypeStruct((1, ATT_W), F32),
         jax.ShapeDtypeStruct((1, ATT_W), F32), jax.ShapeDtypeStruct((1, LANES), F32)],
        [pltpu.VMEM((1, LANES), F32)], ("arbitrary",),
        (x, dx_in, gain, w_in, fqr, fkr, dfqn, dfkn, qg, kg, gmat, dfv, df_col, logf, dsq, dsk, dsv, dgf, dgs), ride)


def _ple_loss(x, p, tgt, gain, wpg, wpp, name):
    s_len = x.shape[0]
    ts = min(512, s_len)

    def body(x_ref, p_ref, t_ref, gain_ref, wpg_ref, wpp_ref, dx_ref, n_ref, ds_ref, dpp_ref, dgain_ref, loss_ref):
        i = pl.program_id(0)

        @pl.when(i == 0)
        def _():
            dgain_ref[...] = jnp.zeros_like(dgain_ref)
            loss_ref[...] = jnp.zeros_like(loss_ref)

        xf = x_ref[...]
        r = _rms_rinv(xf)
        n = xf * r
        hn = (n * gain_ref[...]).astype(BF16)
        n_ref[...] = hn
        sg = _sigmoid(_dot(hn, wpg_ref[...]))
        pp = _dot(p_ref[...].astype(BF16), wpp_ref[...])
        err = (xf + sg * pp) - t_ref[...]
        sq = jnp.sum(jnp.sum(err * err, axis=1, keepdims=True), axis=0, keepdims=True)
        loss_ref[...] += (0.5 / D_MODEL) * sq
        dout = err * (1.0 / D_MODEL)
        dpp_ref[...] = (dout * sg).astype(BF16)
        ds = (dout * pp * (sg * (1.0 - sg))).astype(BF16)
        ds_ref[...] = ds
        dhn = _dot_nt(ds, wpg_ref[...])
        dgain_ref[...] += jnp.sum(dhn * n, axis=0, keepdims=True)
        dn = dhn * gain_ref[...]
        dx_ref[...] = dout + r * (dn - n * jnp.mean(dn * n, axis=-1, keepdims=True))

    tok = lambda w: pl.BlockSpec((ts, w), lambda i: (i, 0))
    full = lambda a: pl.BlockSpec(a.shape, lambda i: (0, 0))
    return pl.pallas_call(
        body, name=name, grid=(s_len // ts,),
        in_specs=[tok(D_MODEL), tok(PLE_DIM), tok(D_MODEL), full(gain), full(wpg), full(wpp)],
        out_specs=[tok(D_MODEL), tok(D_MODEL), tok(D_MODEL), tok(D_MODEL),
                   pl.BlockSpec((1, D_MODEL), lambda i: (0, 0)), pl.BlockSpec((8, LANES), lambda i: (0, 0))],
        out_shape=[jax.ShapeDtypeStruct((s_len, D_MODEL), F32), jax.ShapeDtypeStruct((s_len, D_MODEL), BF16),
                   jax.ShapeDtypeStruct((s_len, D_MODEL), BF16), jax.ShapeDtypeStruct((s_len, D_MODEL), BF16),
                   jax.ShapeDtypeStruct((1, D_MODEL), F32), jax.ShapeDtypeStruct((8, LANES), F32)],
        compiler_params=_cparams("arbitrary"),
    )(x, p, tgt, gain, wpg, wpp)


def _exchange(x, name, broadcast):
    def body(x_ref, out_ref, send_sems, recv_sems, local_sem):
        _exchange_start(x_ref, out_ref, send_sems, recv_sems, local_sem, broadcast)
        _exchange_wait(x_ref, out_ref, send_sems, recv_sems, local_sem, broadcast)

    return pl.pallas_call(
        body, name=name,
        in_specs=[EXCHANGE_SPEC],
        out_specs=EXCHANGE_SPEC,
        out_shape=_exchange_shape(x, broadcast),
        scratch_shapes=list(EXCHANGE_SEMS),
        compiler_params=pltpu.CompilerParams(has_side_effects=True),
    )(x)


EXCHANGE_SPEC = pl.BlockSpec(memory_space=pl.ANY)
EXCHANGE_SEMS = (pltpu.SemaphoreType.DMA((N_DEV - 1,)), pltpu.SemaphoreType.DMA((N_DEV - 1,)), pltpu.SemaphoreType.DMA)


def _exchange_shape(x, broadcast):
    return jax.ShapeDtypeStruct((N_DEV,) + tuple(x.shape if broadcast else x.shape[1:]), x.dtype)


def _exchange_copies(x_ref, out_ref, send_sems, recv_sems, local_sem, broadcast, with_recv=True):
    mx, my, mc = lax.axis_index("x"), lax.axis_index("y"), lax.axis_index("c")
    me = 4 * mx + 2 * my + mc

    def src(idx):
        return x_ref if broadcast else x_ref.at[idx]

    local = pltpu.make_async_copy(src(me), out_ref.at[me], local_sem)
    pairs = []
    for k in range(1, N_DEV):
        px = (1 - mx) if k & 4 else mx
        py = (1 - my) if k & 2 else my
        pc = (1 - mc) if k & 1 else mc
        peer = 4 * px + 2 * py + pc
        sems = dict(send_sem=send_sems.at[k - 1], recv_sem=recv_sems.at[k - 1], device_id=(px, py, pc), device_id_type=MESH)
        recv = pltpu.make_async_remote_copy(src_ref=src(peer), dst_ref=out_ref.at[peer], **sems) if with_recv else None
        pairs.append((pltpu.make_async_remote_copy(src_ref=src(peer), dst_ref=out_ref.at[me], **sems), recv))
    return local, pairs


def _exchange_start(*refs_and_mode):
    local, pairs = _exchange_copies(*refs_and_mode, with_recv=False)
    local.start()
    for send, _ in pairs:
        send.start()


def _exchange_wait(*refs_and_mode):
    local, pairs = _exchange_copies(*refs_and_mode)
    for _, recv in pairs:
        recv.wait_recv()
    for send, _ in pairs:
        send.wait_send()
    local.wait()


def _riding(body, grid, n_in, n_out, ride):
    if ride is None:
        return body
    broadcast = ride[1]

    def wrapped(*refs):
        ins, x_ref = refs[:n_in], refs[n_in]
        outs, out_ref = refs[n_in + 1:n_in + 1 + n_out], refs[n_in + 1 + n_out]
        scratch, sems = refs[n_in + 2 + n_out:-3], refs[-3:]
        step = pl.program_id(0)
        for d in range(1, len(grid)):
            step = step * grid[d] + pl.program_id(d)
        total = 1
        for g in grid:
            total *= g

        @pl.when(step == 0)
        def _():
            _exchange_start(x_ref, out_ref, *sems, broadcast)

        body(*ins, *outs, *scratch)

        @pl.when(step == total - 1)
        def _():
            _exchange_wait(x_ref, out_ref, *sems, broadcast)

    return wrapped


def _ride_call(body, name, grid, in_specs, out_specs, out_shape, scratch_shapes, sem, operands, ride):
    if ride is None:
        return pl.pallas_call(body, name=name, grid=grid, in_specs=in_specs, out_specs=out_specs, out_shape=out_shape,
                              scratch_shapes=scratch_shapes, compiler_params=_cparams(*sem))(*operands)
    return pl.pallas_call(
        _riding(body, grid, len(in_specs), len(out_specs), ride), name=name, grid=grid,
        in_specs=list(in_specs) + [EXCHANGE_SPEC], out_specs=list(out_specs) + [EXCHANGE_SPEC],
        out_shape=list(out_shape) + [_exchange_shape(*ride)],
        scratch_shapes=list(scratch_shapes) + list(EXCHANGE_SEMS),
        compiler_params=_cparams(*(["arbitrary"] * len(grid))),
    )(*operands, ride[0])


def _adamw_math(w, g, m, v):
    m2 = ADAM_B1 * m + (1.0 - ADAM_B1) * g
    v2 = ADAM_B2 * v + (1.0 - ADAM_B2) * (g * g)
    m_hat = m2 / (1.0 - ADAM_B1 ** ADAM_STEP)
    v_hat = v2 / (1.0 - ADAM_B2 ** ADAM_STEP)
    delta = -ADAM_LR * (m_hat / (jnp.sqrt(v_hat) + ADAM_EPS) + ADAM_WD * w)
    return delta, m2, v2


def _adamw(parts, w, m, v, name, tr):
    rows, cols = w.shape

    def body(p_ref, w_ref, m_ref, v_ref, g_ref, d_ref, m2_ref, v2_ref):
        g = p_ref[0].astype(F32)
        for s in range(1, N_DEV):
            g = g + p_ref[s].astype(F32)
        g_ref[...] = g
        d_ref[...], m2_ref[...], v2_ref[...] = _adamw_math(w_ref[...], g, m_ref[...], v_ref[...])

    blk = pl.BlockSpec((tr, cols), lambda i: (i, 0))
    return pl.pallas_call(
        body, name=name, grid=(rows // tr,),
        in_specs=[pl.BlockSpec((N_DEV, tr, cols), lambda i: (0, i, 0)), blk, blk, blk],
        out_specs=[blk] * 4,
        out_shape=[jax.ShapeDtypeStruct((rows, cols), F32)] * 4,
        compiler_params=_cparams("parallel"),
    )(parts, w, m, v)


def _to_rows(a):
    return a.reshape(-1, D_MODEL)


def _pack(pieces, group, dtype):
    out = []
    for name, rows in PACK_GROUPS[group]:
        r = _to_rows(pieces[name]).astype(dtype)
        if r.shape[0] != rows:
            r = jnp.pad(r, ((0, rows - r.shape[0]), (0, 0)))
        out.append(r)
    return jnp.concatenate(out, axis=0)


COL_SHARDED = {"ffn1_w_gate": (D_MODEL, D_FF), "ffn1_w_up": (D_MODEL, D_FF), "w_in": (D_MODEL, IN_REAL),
               "w_branch_fox": (ATT_W, D_MODEL), "w_branch_sb": (ATT_W, D_MODEL),
               "ffn2_w_gate": (D_MODEL, D_FF), "ffn2_w_up": (D_MODEL, D_FF), "w_ple_proj": (PLE_DIM, D_MODEL)}
ROW_SHARDED = {"ffn1_w_down": (D_FF, D_MODEL), "w_out": (D_MODEL, D_MODEL), "ffn2_w_down": (D_FF, D_MODEL),
               "w_ple_gate": (D_MODEL, D_MODEL)}


def _real_rows(name):
    return W_IN_ROWS if name == "w_in" else PACK_OFF[name][1]


def _unpack_full(gathered, name):
    off, _ = PACK_OFF[name]
    g = gathered[:, off:off + _real_rows(name), :]
    if name in ROW_SHARDED:
        return g.reshape(ROW_SHARDED[name])
    k_dim, n_dim = COL_SHARDED[name]
    return g.reshape(N_DEV, k_dim, n_dim // N_DEV).transpose(1, 0, 2).reshape(k_dim, n_dim)


def _to_chunks(full, name):
    if name in ROW_SHARDED:
        return full.reshape(N_DEV, -1, D_MODEL)
    k_dim, n_dim = COL_SHARDED[name]
    return full.reshape(k_dim, N_DEV, n_dim // N_DEV).transpose(1, 0, 2).reshape(N_DEV, -1, D_MODEL)


def _pack_chunks(grads, group):
    out = []
    for name, rows in PACK_GROUPS[group]:
        c = _to_chunks(grads[name], name).astype(BF16)
        if c.shape[1] != rows:
            c = jnp.pad(c, ((0, 0), (0, rows - c.shape[1]), (0, 0)))
        out.append(c)
    return jnp.concatenate(out, axis=1)


def _unpack_shard(packed, name, shape):
    off, _ = PACK_OFF[name]
    return packed[off:off + _real_rows(name), :].reshape(shape)


WEIGHT_NAMES = ['ffn1_norm', 'ffn1_w_gate', 'ffn1_w_up', 'ffn1_w_down', 'mix_norm', 'w_in', 'forget_bias', 'q_norm',
                'k_norm', 'w_branch_fox', 'w_branch_sb', 'w_out', 'ffn2_norm', 'ffn2_w_gate', 'ffn2_w_up',
                'ffn2_w_down', 'ple_norm', 'w_ple_gate', 'w_ple_proj']
SMALL_NAMES = ('ffn1_norm', 'mix_norm', 'ffn2_norm', 'ple_norm', 'q_norm', 'k_norm', 'forget_bias')
Q_OFF, K_OFF, B_OFF, LOSS_OFF = 0, HEAD_DIM, 2 * HEAD_DIM, 2 * HEAD_DIM + N_HEADS


def _pack_small(vals, loss=None):
    tail = [vals['q_norm'].reshape(1, -1), vals['k_norm'].reshape(1, -1), vals['forget_bias'].reshape(1, -1)]
    used = LOSS_OFF
    if loss is not None:
        tail.append(loss.reshape(1, 1))
        used += 1
    tail.append(jnp.zeros((1, D_MODEL - used), F32))
    rows = [vals[n].reshape(1, D_MODEL) for n in SMALL_NAMES[:4]] + [jnp.concatenate(tail, axis=1)]
    rows.append(jnp.zeros((SMALL_ROWS - len(rows), D_MODEL), F32))
    return jnp.concatenate(rows, axis=0)


def _unpack_small(packed, name, shape):
    if name in SMALL_NAMES[:4]:
        return packed[SMALL_NAMES.index(name)].reshape(shape)
    off, n = {'q_norm': (Q_OFF, HEAD_DIM), 'k_norm': (K_OFF, HEAD_DIM), 'forget_bias': (B_OFF, N_HEADS)}[name]
    return packed[4, off:off + n].reshape(shape)


def _pad_w_in(w):
    z = lambda n: jnp.zeros((D_MODEL, n), w.dtype)
    return jnp.concatenate([w[:, :FL_REAL_END], z(C_SQ - FL_REAL_END), w[:, FL_REAL_END:]], axis=1)


def _unpad_w_in(w):
    return jnp.concatenate([w[:, :FL_REAL_END], w[:, C_SQ:]], axis=1)


def _step(x, p, tgt, w):
    row = lambda a: a.reshape(1, -1).astype(F32)
    g_ffn1, g_mix, g_ffn2, g_ple = (row(w[n]) for n in SMALL_NAMES[:4])
    qg = jnp.tile(row(w['q_norm']), (1, N_HEADS))
    kg = jnp.tile(row(w['k_norm']), (1, N_HEADS))
    bias = jnp.pad(row(w['forget_bias']), ((0, 0), (0, LANES - N_HEADS)))
    half = D_FF // 2
    grads = {}

    got = _exchange(_pack(w, 0, BF16), "gather_ffn1", True)
    wg1, wu1, wd1 = (_unpack_full(got, n) for n in ("ffn1_w_gate", "ffn1_w_up", "ffn1_w_down"))
    x1, g1, u1, h1, got = _ffn_fwd(x, g_ffn1, wg1, wu1, wd1, "ffn1_fwd", ride=(_pack(w, 1, BF16), True))
    w_in = _pad_w_in(_unpack_full(got, "w_in"))
    wbf, wbs, wo = (_unpack_full(got, n) for n in ("w_branch_fox", "w_branch_sb", "w_out"))
    (hmix, fqr, fkr, fqn, fkn, fv, logf, f_col, f_row, sq, sk, sv, gf, gs, kmax, got) = _mix_fwd(
        x1, g_mix, w_in, bias, qg, kg, "mix_fwd", ride=(_pack(w, 2, BF16), True))
    wg2, wu2, wd2, wpg, wpp = (_unpack_full(got, n) for n, _ in PACK_GROUPS[2])
    y_fox, lse = _fox_fwd(fqn, fkn, fv, f_col, f_row, kmax, "fox_fwd")
    y_sb, y_sb32 = _sb_fwd(sq, sk, sv, "sb_fwd")
    x2 = _merge_fwd(x1, y_fox, y_sb, gf, gs, wbf, wbs, wo, "merge_fwd")
    x3, g2, u2, h2, = _ffn_fwd(x2, g_ffn2, wg2, wu2, wd2, "ffn2_fwd")
    dx3, n_ple, ds_ple, dpp, dg_ple, loss = _ple_loss(x3, p, tgt, g_ple, wpg, wpp, "ple_loss")

    grads['w_ple_gate'] = _wgrad(n_ple, ds_ple, "dw_ple_gate", D_MODEL, D_MODEL)
    grads['w_ple_proj'] = _wgrad(p, dpp, "dw_ple_proj", PLE_DIM, D_MODEL)
    dg2, du2, act2 = _ffn_bwd_hidden(dx3, g2, u2, wd2, "ffn2_bwd_hidden")
    grads['ffn2_w_gate'] = _wgrad(h2, dg2, "dw_ffn2_gate", D_MODEL, half)
    grads['ffn2_w_up'] = _wgrad(h2, du2, "dw_ffn2_up", D_MODEL, half)
    grads['ffn2_w_down'] = _wgrad(act2, dx3, "dw_ffn2_down", half, D_MODEL)
    dx2, dg_ffn2 = _ffn_bwd_input(x2, dx3, g_ffn2, dg2, du2, wg2, wu2, "ffn2_bwd_input")
    dyf, dys, dgf, dgs, dbf, dbs, merged = _merge_bwd(dx2, y_fox, y_sb, gf, gs, wbf, wbs, wo, "merge_bwd")
    dfqn, dfkn, dfv, dft, dfq = _fox_bwd(fqn, fkn, fv, dyf, y_fox, lse, f_col, f_row, kmax, "fox_bwd")
    dsq, dsk, dsv = _sb_bwd(sq, sk, sv, dys, y_sb32, "sb_bwd")
    s_len = x.shape[0]
    df_heads = dft[:, :2, :] + jnp.stack([dfq[:, :, 0], dfq[:, :, HEAD_DIM]], axis=1)
    df_col = jnp.pad(df_heads.reshape(N_HEADS, s_len).T, ((0, 0), (0, LANES - N_HEADS)))
    dproj, dx1, dg_mix, dqg, dkg, dbias, part2 = _mix_bwd(
        x1, dx2, g_mix, w_in, fqr, fkr, dfqn, dfkn, qg, kg, dfv, df_col, logf, dsq, dsk, dsv, dgf, dgs, "mix_bwd",
        ride=(_pack_chunks(grads, 2), False))
    grads['w_in'] = _unpad_w_in(_wgrad(hmix, dproj, "dw_in", D_MODEL, IN_PAD // 3))
    grads['w_branch_fox'] = _wgrad(y_fox, dbf, "dw_branch_fox", ATT_W, D_MODEL)
    grads['w_branch_sb'] = _wgrad(y_sb, dbs, "dw_branch_sb", ATT_W, D_MODEL)
    grads['w_out'] = _wgrad(merged, dx2, "dw_out", D_MODEL, D_MODEL)
    dg1, du1, act1, part1 = _ffn_bwd_hidden(dx1, g1, u1, wd1, "ffn1_bwd_hidden", ride=(_pack_chunks(grads, 1), False))
    grads['ffn1_w_gate'] = _wgrad(h1, dg1, "dw_ffn1_gate", D_MODEL, half)
    grads['ffn1_w_up'] = _wgrad(h1, du1, "dw_ffn1_up", D_MODEL, half)
    grads['ffn1_w_down'] = _wgrad(act1, dx1, "dw_ffn1_down", half, D_MODEL)
    dx0, dg_ffn1, part0 = _ffn_bwd_input(x, dx1, g_ffn1, dg1, du1, wg1, wu1, "ffn1_bwd_input",
                                         ride=(_pack_chunks(grads, 0), False))

    fold = lambda a: a.reshape(N_HEADS, HEAD_DIM).sum(axis=0).reshape(1, HEAD_DIM)
    small_g = {'ffn1_norm': dg_ffn1, 'mix_norm': dg_mix, 'ffn2_norm': dg_ffn2, 'ple_norm': dg_ple,
               'q_norm': fold(dqg), 'k_norm': fold(dkg), 'forget_bias': dbias[:, :N_HEADS]}
    return loss[0, 0], dx0, (part0, part1, part2), small_g


def kernel(x, p, ffn1_norm, ffn1_w_gate, ffn1_w_up, ffn1_w_down, mix_norm, w_in, forget_bias, q_norm, k_norm, w_branch_fox, w_branch_sb, w_out, ffn2_norm, ffn2_w_gate, ffn2_w_up, ffn2_w_down, ple_norm, w_ple_gate, w_ple_proj, loss_target, m_ffn1_norm, m_ffn1_w_gate, m_ffn1_w_up, m_ffn1_w_down, m_mix_norm, m_w_in, m_forget_bias, m_q_norm, m_k_norm, m_w_branch_fox, m_w_branch_sb, m_w_out, m_ffn2_norm, m_ffn2_w_gate, m_ffn2_w_up, m_ffn2_w_down, m_ple_norm, m_w_ple_gate, m_w_ple_proj, v_ffn1_norm, v_ffn1_w_gate, v_ffn1_w_up, v_ffn1_w_down, v_mix_norm, v_w_in, v_forget_bias, v_q_norm, v_k_norm, v_w_branch_fox, v_w_branch_sb, v_w_out, v_ffn2_norm, v_ffn2_w_gate, v_ffn2_w_up, v_ffn2_w_down, v_ple_norm, v_w_ple_gate, v_w_ple_proj):
    args = dict(locals())
    w = {n: args[n][0] for n in WEIGHT_NAMES}
    m = {n: args["m_" + n][0] for n in WEIGHT_NAMES}
    v = {n: args["v_" + n][0] for n in WEIGHT_NAMES}
    loss, dx, parts, small_g = _step(x[0], p[0, 0], loss_target[0], w)

    big = []
    for grp, part in enumerate(parts):
        big.append(_adamw(part, _pack(w, grp, F32), _pack(m, grp, F32), _pack(v, grp, F32), f"adamw_{grp}",
                          ADAM_TILE_ROWS[grp]))
    small_parts = _exchange(_pack_small(small_g, loss), "gather_small", True)
    sw, sm, sv = (_pack_small(t) for t in (w, m, v))
    small = _adamw(small_parts, sw, sm, sv, "adamw_small", SMALL_ROWS)

    group_of = {n: grp for grp, names in enumerate(PACK_GROUPS) for n, _ in names}
    outs = [small[0][4, LOSS_OFF], dx.reshape(x.shape)]
    for kind in range(4):
        for n in WEIGHT_NAMES:
            shape = args[n].shape
            if n in SMALL_NAMES:
                outs.append(_unpack_small(small[kind], n, shape))
            else:
                outs.append(_unpack_shard(big[group_of[n]][kind], n, shape))
    return tuple(outs)
```

```python
import jax
import jax.numpy as jnp
from jax import lax
from jax.experimental import pallas as pl
from jax.experimental.pallas import tpu as pltpu

F32 = jnp.float32
BF16 = jnp.bfloat16

D_MODEL = 1024
D_FF = 2816
N_HEADS = 8
HEAD_DIM = 64
ATT_W = N_HEADS * HEAD_DIM
PLE_DIM = 256
EPS = 1e-6
N_DEV = 8
MESH = pl.DeviceIdType.MESH

LANES = 128
V7X_SCOPED_VMEM_BYTES = 56 * 1024 * 1024

C_FQ, C_FK, C_FV, C_FL = 0, 512, 1024, 1536
C_SQ, C_SK, C_SV, C_GF, C_GS = 1792, 2304, 2816, 3328, 4352
IN_PAD = 5376
IN_REAL = 5128
FL_REAL_END = 1544

ADAM_LR = 0.001
ADAM_B1 = 0.9
ADAM_B2 = 0.999
ADAM_EPS = 1e-08
ADAM_WD = 0.01
ADAM_STEP = 10

PACK_GROUPS = (
    (("ffn1_w_gate", 352), ("ffn1_w_up", 352), ("ffn1_w_down", 352)),
    (("w_in", 656), ("w_branch_fox", 64), ("w_branch_sb", 64), ("w_out", 128)),
    (("ffn2_w_gate", 352), ("ffn2_w_up", 352), ("ffn2_w_down", 352), ("w_ple_gate", 128), ("w_ple_proj", 32)),
)
ADAM_TILE_ROWS = (528, 304, 304)
PACK_OFF = {}
for _grp in PACK_GROUPS:
    _o = 0
    for _n, _r in _grp:
        PACK_OFF[_n] = (_o, _r)
        _o += _r
W_IN_ROWS = 641

SMALL_ROWS = 8


def _cparams(*sem):
    return pltpu.CompilerParams(dimension_semantics=sem, vmem_limit_bytes=V7X_SCOPED_VMEM_BYTES)


def _dot(a, b):
    return jnp.dot(a, b, preferred_element_type=F32)


def _dot_nt(a, b):
    return lax.dot_general(a, b, (((1,), (1,)), ((), ())), preferred_element_type=F32)


def _dot_tn(a, b):
    return lax.dot_general(a, b, (((0,), (0,)), ((), ())), preferred_element_type=F32)


def _split(x, parts):
    out = []
    r = x
    for _ in range(parts):
        p = r.astype(BF16)
        out.append(p)
        r = r - p.astype(F32)
    return out


def _dot_split(x, m, parts):
    acc = None
    for p in _split(x, parts):
        t = _dot(p, m)
        acc = t if acc is None else acc + t
    return acc


def _dot_split_left(m, x, parts):
    acc = None
    for p in _split(x, parts):
        t = _dot(m, p)
        acc = t if acc is None else acc + t
    return acc


def _rms_rinv(xf):
    return lax.rsqrt(jnp.mean(xf * xf, axis=-1, keepdims=True) + EPS)


def _sigmoid(x):
    return 1.0 / (1.0 + jnp.exp(-x))


def _softplus_neg_abs(z):
    return jnp.log(1.0 + jnp.exp(-jnp.abs(z)))


FFN_SHARD = D_FF // N_DEV
FFN_CHUNK = 4


def _ffn_w_spec(blk, index_map):
    return pl.BlockSpec((FFN_CHUNK, FFN_SHARD, D_MODEL), lambda *g: (index_map(*g), blk, 0))


def _ffn_w(ref):
    return ref[...].reshape(FFN_CHUNK * FFN_SHARD, D_MODEL)


def _ffn_fwd(x, gain, wbuf, blks, name, ride=None):
    s_len = x.shape[0]
    ts = min(512, s_len)
    fc = FFN_CHUNK * FFN_SHARD
    nt, nc = s_len // ts, D_FF // fc

    def body(x_ref, gain_ref, wg_ref, wu_ref, wd_ref, y_ref, g_ref, u_ref, h_ref, acc_scr):
        j = pl.program_id(1)

        @pl.when(j == 0)
        def _():
            xf = x_ref[...]
            h_ref[...] = ((xf * _rms_rinv(xf)) * gain_ref[...]).astype(BF16)
            acc_scr[...] = jnp.zeros_like(acc_scr)

        h = h_ref[...]
        g = _dot_nt(h, _ffn_w(wg_ref))
        u = _dot_nt(h, _ffn_w(wu_ref))
        g_ref[...] = g.astype(BF16)
        u_ref[...] = u.astype(BF16)
        a = (g * _sigmoid(g) * u).astype(BF16)
        acc_scr[...] += _dot(a, _ffn_w(wd_ref))

        @pl.when(j == nc - 1)
        def _():
            y_ref[...] = x_ref[...] + 0.5 * acc_scr[...]

    tok = pl.BlockSpec((ts, D_MODEL), lambda i, j: (i, 0))
    hid = pl.BlockSpec((ts, fc), lambda i, j: (i, j))
    return _ride_call(
        body, name, (nt, nc),
        [tok, pl.BlockSpec((1, D_MODEL), lambda i, j: (0, 0))] + [_ffn_w_spec(b, lambda i, j: j) for b in blks],
        [tok, hid, hid, tok],
        [jax.ShapeDtypeStruct((s_len, D_MODEL), F32), jax.ShapeDtypeStruct((s_len, D_FF), BF16),
         jax.ShapeDtypeStruct((s_len, D_FF), BF16), jax.ShapeDtypeStruct((s_len, D_MODEL), BF16)],
        [pltpu.VMEM((ts, D_MODEL), F32)], ("parallel", "arbitrary"), (x, gain, wbuf, wbuf, wbuf), ride)


def _ffn_bwd_hidden(dy, g, u, wbuf, blk, name, ride=None):
    s_len = dy.shape[0]
    ts = min(512, s_len)
    fc = FFN_CHUNK * FFN_SHARD
    nt, nc = s_len // ts, D_FF // fc

    def body(dy_ref, g_ref, u_ref, wd_ref, dg_ref, du_ref, act_ref):
        da = 0.5 * _dot_nt(dy_ref[...].astype(BF16), _ffn_w(wd_ref))
        gf = g_ref[...].astype(F32)
        uf = u_ref[...].astype(F32)
        sg = _sigmoid(gf)
        silu = gf * sg
        dg_ref[...] = (da * uf * (sg * (1.0 + gf * (1.0 - sg)))).astype(BF16)
        du_ref[...] = (da * silu).astype(BF16)
        act_ref[...] = (0.5 * silu * uf).astype(BF16)

    hid = pl.BlockSpec((ts, fc), lambda c, t: (t, c))
    return _ride_call(
        body, name, (nc, nt),
        [pl.BlockSpec((ts, D_MODEL), lambda c, t: (t, 0)), hid, hid, _ffn_w_spec(blk, lambda c, t: c)],
        [hid, hid, hid], [jax.ShapeDtypeStruct((s_len, D_FF), BF16)] * 3, [], ("parallel", "parallel"),
        (dy, g, u, wbuf), ride)


def _ffn_bwd_input(x, dy, gain, dg, du, wbuf, blks, name, ride=None):
    s_len = x.shape[0]
    ts = min(512, s_len)
    fc = FFN_CHUNK * FFN_SHARD
    nt, nc = s_len // ts, D_FF // fc

    def body(x_ref, dy_ref, gain_ref, dg_ref, du_ref, wg_ref, wu_ref, dx_ref, dgain_ref, acc):
        i = pl.program_id(0)
        j = pl.program_id(1)
        part = _dot(dg_ref[...], _ffn_w(wg_ref)) + _dot(du_ref[...], _ffn_w(wu_ref))

        @pl.when(j == 0)
        def _():
            acc[...] = part

        @pl.when(j > 0)
        def _():
            acc[...] += part

        @pl.when(j == nc - 1)
        def _():
            xf = x_ref[...]
            r = _rms_rinv(xf)
            xhat = xf * r
            dh = acc[...]
            dgp = jnp.sum(dh * xhat, axis=0, keepdims=True)

            @pl.when(i == 0)
            def _():
                dgain_ref[...] = dgp

            @pl.when(i > 0)
            def _():
                dgain_ref[...] += dgp

            dn = dh * gain_ref[...]
            dx_ref[...] = dy_ref[...] + r * (dn - xhat * jnp.mean(dn * xhat, axis=-1, keepdims=True))

    tok = pl.BlockSpec((ts, D_MODEL), lambda i, j: (i, 0))
    row = pl.BlockSpec((1, D_MODEL), lambda i, j: (0, 0))
    hid = pl.BlockSpec((ts, fc), lambda i, j: (i, j))
    return _ride_call(
        body, name, (nt, nc), [tok, tok, row, hid, hid] + [_ffn_w_spec(b, lambda i, j: j) for b in blks], [tok, row],
        [jax.ShapeDtypeStruct((s_len, D_MODEL), F32), jax.ShapeDtypeStruct((1, D_MODEL), F32)],
        [pltpu.VMEM((ts, D_MODEL), F32)], ("arbitrary", "arbitrary"), (x, dy, gain, dg, du, wbuf, wbuf), ride)


def _wgrad(a, b, name, tk, tn):
    s_len, k_dim = a.shape
    n_dim = b.shape[1]
    ts = min(512, s_len)
    ns = s_len // ts

    def body(a_ref, b_ref, o_ref, acc):
        s = pl.program_id(2)
        p = _dot_tn(a_ref[...].astype(BF16), b_ref[...].astype(BF16))

        @pl.when(s == 0)
        def _():
            acc[...] = p

        @pl.when(s > 0)
        def _():
            acc[...] += p

        @pl.when(s == ns - 1)
        def _():
            o_ref[...] = acc[...].astype(BF16)

    return pl.pallas_call(
        body, name=name, grid=(k_dim // tk, n_dim // tn, ns),
        in_specs=[
            pl.BlockSpec((ts, tk), lambda k, n, s: (s, k)),
            pl.BlockSpec((ts, tn), lambda k, n, s: (s, n)),
        ],
        out_specs=pl.BlockSpec((tk, tn), lambda k, n, s: (k, n)),
        out_shape=jax.ShapeDtypeStruct((k_dim, n_dim), BF16),
        scratch_shapes=[pltpu.VMEM((tk, tn), F32)],
        compiler_params=_cparams("parallel", "parallel", "arbitrary"),
    )(a, b)


def _head_group_matrix():
    r = lax.broadcasted_iota(jnp.int32, (ATT_W, ATT_W), 0) // HEAD_DIM
    c = lax.broadcasted_iota(jnp.int32, (ATT_W, ATT_W), 1) // HEAD_DIM
    return (r == c).astype(BF16)


def _mix_fwd(x, gain, w_in, bias, qg, kg, name, ride=None):
    s_len = x.shape[0]
    ts = min(256, s_len)
    nt = s_len // ts
    gmat = _head_group_matrix()

    def body(x_ref, gain_ref, w_ref, bias_ref, qg_ref, kg_ref, gm_ref,
             h_ref, fqr_ref, fkr_ref, fqn_ref, fkn_ref, fv_ref, logf_ref, f_ref, ft_ref,
             sq_ref, sk_ref, sv_ref, gf_ref, gs_ref, kmax_ref, carry):
        i = pl.program_id(0)
        xf = x_ref[...]
        h = ((xf * _rms_rinv(xf)) * gain_ref[...]).astype(BF16)
        h_ref[...] = h
        gm = gm_ref[...]

        def proj(lo, n):
            return _dot_nt(h, w_ref[lo:lo + n, :])

        def headnorm(raw, g):
            ms = _dot_split(raw * raw, gm, 3) * (1.0 / HEAD_DIM)
            return ((raw * lax.rsqrt(ms + EPS)) * g).astype(BF16)

        fq = proj(C_FQ, ATT_W)
        fqr_ref[...] = fq
        fqn_ref[...] = headnorm(fq, qg_ref[...])
        fk = proj(C_FK, ATT_W)
        fkr_ref[...] = fk
        fkn = headnorm(fk, kg_ref[...])
        fkn_ref[...] = fkn
        kn2 = jnp.max(_dot_split(jnp.square(fkn.astype(F32)), gm, 3), axis=0, keepdims=True)

        @pl.when(i == 0)
        def _():
            kmax_ref[...] = kn2

        @pl.when(i > 0)
        def _():
            kmax_ref[...] = jnp.maximum(kmax_ref[...], kn2)
        fv_ref[...] = proj(C_FV, ATT_W).astype(BF16)
        sq_ref[...] = proj(C_SQ, ATT_W).astype(BF16)
        sk_ref[...] = proj(C_SK, ATT_W).astype(BF16)
        sv_ref[...] = proj(C_SV, ATT_W).astype(BF16)
        gf_ref[...] = proj(C_GF, D_MODEL)
        gs_ref[...] = proj(C_GS, D_MODEL)

        fl = proj(C_FL, LANES) + bias_ref[...]
        lane = lax.broadcasted_iota(jnp.int32, fl.shape, 1)
        logf = jnp.where(lane < N_HEADS, jnp.minimum(fl, 0.0) - _softplus_neg_abs(fl), 0.0)
        logf_ref[...] = logf

        @pl.when(i == 0)
        def _():
            carry[...] = jnp.zeros_like(carry)

        r = lax.broadcasted_iota(jnp.int32, (ts, ts), 0)
        c = lax.broadcasted_iota(jnp.int32, (ts, ts), 1)
        tri = (r >= c).astype(BF16)
        f_tile = _dot_split_left(tri, logf, 3) + carry[...]
        f_ref[...] = f_tile
        ft_ref[...] = f_tile.T[:N_HEADS, :]
        carry[...] = f_tile[ts - 1:ts, :]

    tok = lambda w: pl.BlockSpec((ts, w), lambda i: (i, 0))
    full = lambda a: pl.BlockSpec(a.shape, lambda i: (0, 0))
    f32o = lambda w: jax.ShapeDtypeStruct((s_len, w), F32)
    b16o = lambda w: jax.ShapeDtypeStruct((s_len, w), BF16)
    return _ride_call(
        body, name, (nt,),
        [tok(D_MODEL), full(gain), full(w_in), full(bias), full(qg), full(kg), full(gmat)],
        [
            tok(D_MODEL), tok(ATT_W), tok(ATT_W), tok(ATT_W), tok(ATT_W), tok(ATT_W), tok(LANES), tok(LANES),
            pl.BlockSpec((N_HEADS, ts), lambda i: (0, i)),
            tok(ATT_W), tok(ATT_W), tok(ATT_W), tok(D_MODEL), tok(D_MODEL),
            pl.BlockSpec((1, ATT_W), lambda i: (0, 0)),
        ],
        [
            b16o(D_MODEL), f32o(ATT_W), f32o(ATT_W), b16o(ATT_W), b16o(ATT_W), b16o(ATT_W), f32o(LANES), f32o(LANES),
            jax.ShapeDtypeStruct((N_HEADS, s_len), F32),
            b16o(ATT_W), b16o(ATT_W), b16o(ATT_W), f32o(D_MODEL), f32o(D_MODEL),
            jax.ShapeDtypeStruct((1, ATT_W), F32),
        ],
        [pltpu.VMEM((1, LANES), F32)], ("arbitrary",), (x, gain, w_in, bias, qg, kg, gmat), ride)


ATT_T = 256
ATT_ROWS = 2
EXP_ZERO = 104.0


def _att_tiling(s_len):
    t = min(ATT_T, s_len)
    nr = min(ATT_ROWS, s_len // t)
    return t, nr, s_len // (t * nr)


def _pair_specs(s_len, tq):
    qblk = pl.BlockSpec((tq, LANES), lambda hp, i: (i, hp))
    kvfull = pl.BlockSpec((s_len, LANES), lambda hp, i: (0, hp))
    return qblk, kvfull


def _walk_tiles(i, nr, load, sub, flush, more, init):
    base = i * nr
    carries = list(init)
    for r in range(nr):
        for kk in range(r, -1, -1):
            (carries[r],), side = sub([r], load(base + kk), [carries[r]], kk == r)
            flush(base + kk, side)

    def cond(state):
        return jnp.logical_and(state[0] < base, state[1] > 0)

    def step(state):
        n, _, cs = state
        kb = base - 1 - n
        cs, side = sub(list(range(nr)), load(kb), list(cs), False)
        flush(kb, side)
        return n + 1, more(cs, kb - 1), tuple(cs)

    return lax.while_loop(cond, step, (jnp.int32(0), more(carries, base - 1), tuple(carries)))[2]


def _stack(parts):
    return parts[0] if len(parts) == 1 else jnp.concatenate(parts, axis=0)


def _stacked_halves(x, lo):
    z = jnp.zeros_like(x)
    return jnp.concatenate([jnp.where(lo, x, z), jnp.where(lo, z, x)], axis=0)


def _fox_qk_bound(qst_r, km_ref, t):
    km = km_ref[...]
    out = []
    for j in (0, 1):
        qf = qst_r[j * t:(j + 1) * t, :].astype(F32)
        qn = jnp.sqrt(jnp.sum(qf * qf, axis=1, keepdims=True))
        out.append(qn * jnp.sqrt(km[:, j * HEAD_DIM:j * HEAD_DIM + 1]) * 1.001 + 1.0)
    return out


def _fox_more(nr, t, hp, ft_ref, qkb, fq, level):
    def more(carries, kb):
        k0 = pl.multiple_of(jnp.maximum(kb, 0) * t, t)
        worst = None
        for j in (0, 1):
            f_new = jnp.min(ft_ref[pl.ds(2 * hp + j, 1), pl.ds(k0, t)], axis=1, keepdims=True)
            for r in range(nr):
                gap = (qkb[r][j] + fq[r][j] - f_new) - level(carries, r, j)
                worst = gap if worst is None else jnp.maximum(worst, gap)
        return (jnp.max(worst) > -EXP_ZERO).astype(jnp.int32)
    return more


def _fox_fwd(q, k, v, f_col, f_row, kmax, name):
    s_len = q.shape[0]
    t, nr, nq = _att_tiling(s_len)

    def body(q_ref, k_ref, v_ref, f_ref, ft_ref, km_ref, y_ref, lse_ref):
        hp = pl.program_id(0)
        i = pl.program_id(1)
        lane = lax.broadcasted_iota(jnp.int32, (t, LANES), 1)
        lo = lane < HEAD_DIM
        causal = lax.broadcasted_iota(jnp.int32, (t, t), 0) >= lax.broadcasted_iota(jnp.int32, (t, t), 1)
        rows = [pl.ds(r * t, t) for r in range(nr)]
        qst = [_stacked_halves(q_ref[rw, :] * jnp.asarray(HEAD_DIM ** -0.5, BF16), lo) for rw in rows]
        q_all = _stack(qst)
        fq = [[jnp.sum(jnp.where(lane == 2 * hp + j, f_ref[rw, :], 0.0), axis=1, keepdims=True) for j in (0, 1)]
              for rw in rows]

        def load(kb):
            k0 = pl.multiple_of(kb * t, t)
            frow = [ft_ref[pl.ds(2 * hp + j, 1), pl.ds(k0, t)] for j in (0, 1)]
            return k_ref[pl.ds(k0, t), :], v_ref[pl.ds(k0, t), :], frow

        def sub(rs, tiles, carries, masked):
            kblk, vblk, frow = tiles
            z = _dot_nt(q_all if len(rs) == nr else qst[rs[0]], kblk)
            ps, stats = [], []
            for n, (r, j) in enumerate((r, j) for r in range(len(rs)) for j in (0, 1)):
                m, l, _ = carries[r]
                s = z[n * t:(n + 1) * t, :] + (fq[rs[r]][j] - frow[j])
                if masked:
                    s = jnp.where(causal, s, -1e30)
                mj = jnp.maximum(m[j], jnp.max(s, axis=1, keepdims=True))
                aj = jnp.exp(m[j] - mj)
                p = jnp.exp(s - mj)
                stats.append((mj, aj, aj * l[j] + jnp.sum(p, axis=1, keepdims=True)))
                ps.append(p.astype(BF16))
            pv = _dot(_stack(ps), vblk)
            out = []
            for r in range(len(rs)):
                (m0, a0, l0), (m1, a1, l1) = stats[2 * r], stats[2 * r + 1]
                acc = carries[r][2]
                acc = (acc[0] * a0 + pv[2 * r * t:(2 * r + 1) * t, :], acc[1] * a1 + pv[(2 * r + 1) * t:(2 * r + 2) * t, :])
                out.append(((m0, m1), (l0, l1), acc))
            return out, None

        neg = jnp.full((t, 1), -1e30, F32)
        zero = jnp.zeros((t, 1), F32)
        zacc = jnp.zeros((t, LANES), F32)
        init = [((neg, neg), (zero, zero), (zacc, zacc))] * nr
        qkb = [_fox_qk_bound(qs, km_ref, t) for qs in qst]
        more = _fox_more(nr, t, hp, ft_ref, qkb, fq, lambda carries, r, j: carries[r][0][j])
        out = _walk_tiles(i, nr, load, sub, lambda kb, side: None, more, init)
        for rw, (m, l, acc) in zip(rows, out):
            y_ref[rw, :] = jnp.where(lo, acc[0] / l[0], acc[1] / l[1]).astype(BF16)
            lse_ref[0, rw, :] = jnp.where(lo, m[0] + jnp.log(l[0]), m[1] + jnp.log(l[1]))

    qblk, kvfull = _pair_specs(s_len, t * nr)
    return pl.pallas_call(
        body, name=name, grid=(N_HEADS // 2, nq),
        in_specs=[qblk, kvfull, kvfull,
                  pl.BlockSpec((t * nr, LANES), lambda hp, i: (i, 0)),
                  pl.BlockSpec((N_HEADS, s_len), lambda hp, i: (0, 0)),
                  pl.BlockSpec((1, LANES), lambda hp, i: (0, hp))],
        out_specs=[qblk, pl.BlockSpec((1, t * nr, LANES), lambda hp, i: (hp, i, 0))],
        out_shape=[jax.ShapeDtypeStruct((s_len, ATT_W), BF16),
                   jax.ShapeDtypeStruct((N_HEADS // 2, s_len, LANES), F32)],
        compiler_params=_cparams("parallel", "parallel"),
    )(q, k, v, f_col, f_row, kmax)


def _fox_bwd(q, k, v, dy, y, lse, f_col, f_row, kmax, name):
    s_len = q.shape[0]
    t, nr, nq = _att_tiling(s_len)

    def body(q_ref, k_ref, v_ref, dy_ref, y_ref, lse_ref, f_ref, ft_ref, km_ref,
             dq_ref, dk_ref, dv_ref, dft_ref, dfq_ref):
        hp = pl.program_id(0)
        i = pl.program_id(1)

        @pl.when(i == 0)
        def _():
            dk_ref[...] = jnp.zeros_like(dk_ref)
            dv_ref[...] = jnp.zeros_like(dv_ref)
            dft_ref[...] = jnp.zeros_like(dft_ref)

        lane = lax.broadcasted_iota(jnp.int32, (t, LANES), 1)
        lo = lane < HEAD_DIM
        causal = lax.broadcasted_iota(jnp.int32, (t, t), 0) >= lax.broadcasted_iota(jnp.int32, (t, t), 1)
        rows = [pl.ds(r * t, t) for r in range(nr)]
        qst, dyst, delta, lse, fq = [], [], [], [], []
        for rw in rows:
            qst.append(_stacked_halves(q_ref[rw, :] * jnp.asarray(HEAD_DIM ** -0.5, BF16), lo))
            dyb = dy_ref[rw, :]
            dyst.append(_stacked_halves(dyb, lo))
            prod = dyb.astype(F32) * y_ref[rw, :].astype(F32)
            delta.append([jnp.sum(jnp.where(lo, prod, 0.0), axis=1, keepdims=True),
                          jnp.sum(jnp.where(lo, 0.0, prod), axis=1, keepdims=True)])
            lse_b = lse_ref[0, rw, :]
            lse.append([lse_b[:, 0:1], lse_b[:, HEAD_DIM:HEAD_DIM + 1]])
            fq.append([jnp.sum(jnp.where(lane == 2 * hp + j, f_ref[rw, :], 0.0), axis=1, keepdims=True)
                       for j in (0, 1)])

        q_all, dy_all = _stack(qst), _stack(dyst)

        def load(kb):
            k0 = pl.multiple_of(kb * t, t)
            frow = [ft_ref[pl.ds(2 * hp + j, 1), pl.ds(k0, t)] for j in (0, 1)]
            return k_ref[pl.ds(k0, t), :], v_ref[pl.ds(k0, t), :], frow

        def sub(rs, tiles, carries, masked):
            kblk, vblk, frow = tiles
            qs, dys = (q_all, dy_all) if len(rs) == nr else (qst[rs[0]], dyst[rs[0]])
            z = _dot_nt(qs, kblk)
            dp = _dot_nt(dys, vblk)
            pb, dsb, rsum, col = [], [], [], [None, None]
            for n, (r, j) in enumerate((r, j) for r in range(len(rs)) for j in (0, 1)):
                sl = slice(n * t, (n + 1) * t)
                s = z[sl, :] + (fq[rs[r]][j] - frow[j])
                p = jnp.exp(s - lse[rs[r]][j])
                if masked:
                    p = jnp.where(causal, p, 0.0)
                ds = p * (dp[sl, :] - delta[rs[r]][j])
                c = jnp.sum(ds, axis=0, keepdims=True)
                col[j] = c if col[j] is None else col[j] + c
                rsum.append(carries[r][1][j] + jnp.sum(ds, axis=1, keepdims=True))
                pb.append(p.astype(BF16))
                dsb.append(ds.astype(BF16))
            p_all, ds_all = _stack(pb), _stack(dsb)
            dqs = _dot(ds_all, kblk)
            out = []
            for r in range(len(rs)):
                dq = carries[r][0]
                dq = (dq[0] + dqs[2 * r * t:(2 * r + 1) * t, :], dq[1] + dqs[(2 * r + 1) * t:(2 * r + 2) * t, :])
                out.append((dq, (rsum[2 * r], rsum[2 * r + 1])))
            return out, (_dot_tn(ds_all, qs), _dot_tn(p_all, dys), col)

        def flush(kb, side):
            k0 = pl.multiple_of(kb * t, t)
            dk_ref[pl.ds(k0, t), :] += side[0]
            dv_ref[pl.ds(k0, t), :] += side[1]
            for j in (0, 1):
                dft_ref[0, pl.ds(j, 1), pl.ds(k0, t)] -= side[2][j]

        zero = jnp.zeros((t, 1), F32)
        zacc = jnp.zeros((t, LANES), F32)
        qkb = [_fox_qk_bound(qs, km_ref, t) for qs in qst]
        more = _fox_more(nr, t, hp, ft_ref, qkb, fq, lambda carries, r, j: lse[r][j])
        out = _walk_tiles(i, nr, load, sub, flush, more, [((zacc, zacc), (zero, zero))] * nr)
        for rw, (dq, rs) in zip(rows, out):
            dq_ref[rw, :] = jnp.where(lo, dq[0], dq[1]) * (HEAD_DIM ** -0.5)
            dfq_ref[0, rw, :] = jnp.where(lo, rs[0], rs[1])

    qblk, kvfull = _pair_specs(s_len, t * nr)
    return pl.pallas_call(
        body, name=name, grid=(N_HEADS // 2, nq),
        in_specs=[qblk, kvfull, kvfull, qblk, qblk,
                  pl.BlockSpec((1, t * nr, LANES), lambda hp, i: (hp, i, 0)),
                  pl.BlockSpec((t * nr, LANES), lambda hp, i: (i, 0)),
                  pl.BlockSpec((N_HEADS, s_len), lambda hp, i: (0, 0)),
                  pl.BlockSpec((1, LANES), lambda hp, i: (0, hp))],
        out_specs=[qblk, kvfull, kvfull, pl.BlockSpec((1, 8, s_len), lambda hp, i: (hp, 0, 0)),
                   pl.BlockSpec((1, t * nr, LANES), lambda hp, i: (hp, i, 0))],
        out_shape=[jax.ShapeDtypeStruct((s_len, ATT_W), F32)] * 3
        + [jax.ShapeDtypeStruct((N_HEADS // 2, 8, s_len), F32),
           jax.ShapeDtypeStruct((N_HEADS // 2, s_len, LANES), F32)],
        compiler_params=_cparams("arbitrary", "arbitrary"),
    )(q, k, v, dy, y, lse, f_col, f_row, kmax)


def _sb_more(carries, kb):
    worst = None
    for cr in carries:
        for cj in cr[0]:
            worst = cj if worst is None else jnp.maximum(worst, cj)
    return (jnp.max(worst) > -EXP_ZERO).astype(jnp.int32)


def _stacked_split_dot(slabs, m, parts):
    split = [_split(x, parts) for x in slabs]
    acc = None
    for p in range(parts):
        d = _dot(_stack([s[p] for s in split]), m)
        acc = d if acc is None else acc + d
    return acc


def _sb_weights(z, c, strict, upper, t):
    logs = []
    for n in range(z.shape[0] // t):
        zn = z[n * t:(n + 1) * t, :]
        sp = _softplus_neg_abs(zn)
        l1m = jnp.minimum(-zn, 0.0) - sp
        if strict is not None:
            l1m = jnp.where(strict, l1m, 0.0)
        logs.append((jnp.minimum(zn, 0.0) - sp, l1m))
    suf = _stacked_split_dot([l1m for _, l1m in logs], upper, 2)
    out = []
    for n, (logb, l1m) in enumerate(logs):
        after = c[n] + suf[n * t:(n + 1) * t, :]
        a = jnp.exp(logb + after)
        if strict is not None:
            a = jnp.where(strict, a, 0.0)
        out.append((logb, a, after[:, 0:1] + l1m[:, 0:1]))
    return out


def _sb_fwd(q, k, v, name):
    s_len = q.shape[0]
    t, nr, nq = _att_tiling(s_len)

    def body(q_ref, k_ref, v_ref, y_ref, yf_ref):
        i = pl.program_id(1)
        lane = lax.broadcasted_iota(jnp.int32, (t, LANES), 1)
        lo = lane < HEAD_DIM
        ri = lax.broadcasted_iota(jnp.int32, (t, t), 0)
        ci = lax.broadcasted_iota(jnp.int32, (t, t), 1)
        strict = ci < ri
        upper = (ri > ci).astype(BF16)
        rows = [pl.ds(r * t, t) for r in range(nr)]
        qst = [_stacked_halves(q_ref[rw, :] * jnp.asarray(HEAD_DIM ** -0.5, BF16), lo) for rw in rows]
        q_all = _stack(qst)

        def load(kb):
            k0 = pl.multiple_of(kb * t, t)
            return k_ref[pl.ds(k0, t), :], v_ref[pl.ds(k0, t), :]

        def sub(rs, tiles, carries, masked):
            kblk, vblk = tiles
            z = _dot_nt(q_all if len(rs) == nr else qst[rs[0]], kblk)
            c = [carries[r][0][j] for r in range(len(rs)) for j in (0, 1)]
            w = _sb_weights(z, c, strict if masked else None, upper, t)
            pv = _dot(_stack([a.astype(BF16) for _, a, _ in w]), vblk)
            out = []
            for r in range(len(rs)):
                acc = carries[r][1]
                acc = (acc[0] + pv[2 * r * t:(2 * r + 1) * t, :], acc[1] + pv[(2 * r + 1) * t:(2 * r + 2) * t, :])
                out.append(((w[2 * r][2], w[2 * r + 1][2]), acc))
            return out, None

        zero = jnp.zeros((t, 1), F32)
        zacc = jnp.zeros((t, LANES), F32)
        out = _walk_tiles(i, nr, load, sub, lambda kb, side: None, _sb_more, [((zero, zero), (zacc, zacc))] * nr)
        for rw, (_, acc) in zip(rows, out):
            y = jnp.where(lo, acc[0], acc[1])
            y_ref[rw, :] = y.astype(BF16)
            yf_ref[rw, :] = y

    qblk, kvfull = _pair_specs(s_len, t * nr)
    return pl.pallas_call(
        body, name=name, grid=(N_HEADS // 2, nq),
        in_specs=[qblk, kvfull, kvfull],
        out_specs=[qblk, qblk],
        out_shape=[jax.ShapeDtypeStruct((s_len, ATT_W), BF16), jax.ShapeDtypeStruct((s_len, ATT_W), F32)],
        compiler_params=_cparams("parallel", "parallel"),
    )(q, k, v)


def _sb_bwd(q, k, v, dy, yf, name):
    s_len = q.shape[0]
    t, nr, nq = _att_tiling(s_len)

    def body(q_ref, k_ref, v_ref, dy_ref, yf_ref, dq_ref, dk_ref, dv_ref):
        i = pl.program_id(1)

        @pl.when(i == 0)
        def _():
            dk_ref[...] = jnp.zeros_like(dk_ref)
            dv_ref[...] = jnp.zeros_like(dv_ref)

        lane = lax.broadcasted_iota(jnp.int32, (t, LANES), 1)
        lo = lane < HEAD_DIM
        ri = lax.broadcasted_iota(jnp.int32, (t, t), 0)
        ci = lax.broadcasted_iota(jnp.int32, (t, t), 1)
        strict = ci < ri
        upper = (ri > ci).astype(BF16)
        upper_incl = (ri >= ci).astype(BF16)
        rows = [pl.ds(r * t, t) for r in range(nr)]
        qst, dyst, delta = [], [], []
        for rw in rows:
            qst.append(_stacked_halves(q_ref[rw, :] * jnp.asarray(HEAD_DIM ** -0.5, BF16), lo))
            dyb = dy_ref[rw, :]
            dyst.append(_stacked_halves(dyb, lo))
            prod = dyb.astype(F32) * yf_ref[rw, :]
            delta.append([jnp.sum(jnp.where(lo, prod, 0.0), axis=1, keepdims=True),
                          jnp.sum(jnp.where(lo, 0.0, prod), axis=1, keepdims=True)])
        q_all, dy_all = _stack(qst), _stack(dyst)

        def load(kb):
            k0 = pl.multiple_of(kb * t, t)
            return k_ref[pl.ds(k0, t), :], v_ref[pl.ds(k0, t), :]

        def sub(rs, tiles, carries, masked):
            kblk, vblk = tiles
            qs, dys = (q_all, dy_all) if len(rs) == nr else (qst[rs[0]], dyst[rs[0]])
            slabs = [(r, j) for r in range(len(rs)) for j in (0, 1)]
            z = _dot_nt(qs, kblk)
            w = _sb_weights(z, [carries[r][0][j] for r, j in slabs], strict if masked else None, upper, t)
            da = _dot_nt(dys, vblk)
            ab = [a.astype(BF16) for _, a, _ in w]
            dl = [ab[n].astype(F32) * da[n * t:(n + 1) * t, :] for n in range(len(slabs))]
            tail = _stacked_split_dot(dl, upper_incl, 2)
            dzb, e_new = [], []
            for n, (r, j) in enumerate(slabs):
                tl = tail[n * t:(n + 1) * t, :]
                e = carries[r][1][j]
                dl1m = (delta[rs[r]][j] - e) - tl
                e_new.append(e + tl[:, 0:1])
                dz = dl[n] - jnp.exp(w[n][0]) * (dl[n] + dl1m)
                if masked:
                    dz = jnp.where(strict, dz, 0.0)
                dzb.append(dz.astype(BF16))
            a_all, dz_all = _stack(ab), _stack(dzb)
            dqs = _dot(dz_all, kblk)
            out = []
            for r in range(len(rs)):
                dq = carries[r][2]
                dq = (dq[0] + dqs[2 * r * t:(2 * r + 1) * t, :], dq[1] + dqs[(2 * r + 1) * t:(2 * r + 2) * t, :])
                out.append(((w[2 * r][2], w[2 * r + 1][2]), (e_new[2 * r], e_new[2 * r + 1]), dq))
            return out, (_dot_tn(dz_all, qs), _dot_tn(a_all, dys))

        def flush(kb, side):
            k0 = pl.multiple_of(kb * t, t)
            dk_ref[pl.ds(k0, t), :] += side[0]
            dv_ref[pl.ds(k0, t), :] += side[1]

        zero = jnp.zeros((t, 1), F32)
        zacc = jnp.zeros((t, LANES), F32)
        out = _walk_tiles(i, nr, load, sub, flush, _sb_more, [((zero, zero), (zero, zero), (zacc, zacc))] * nr)
        for rw, (_, _, dq) in zip(rows, out):
            dq_ref[rw, :] = jnp.where(lo, dq[0], dq[1]) * (HEAD_DIM ** -0.5)

    qblk, kvfull = _pair_specs(s_len, t * nr)
    return pl.pallas_call(
        body, name=name, grid=(N_HEADS // 2, nq),
        in_specs=[qblk, kvfull, kvfull, qblk, qblk],
        out_specs=[qblk, kvfull, kvfull],
        out_shape=[jax.ShapeDtypeStruct((s_len, ATT_W), F32)] * 3,
        compiler_params=_cparams("arbitrary", "arbitrary"),
    )(q, k, v, dy, yf)


def _merge_fwd(x, yf, ys, gf, gs, wbf, wbs, wo, name):
    s_len = x.shape[0]
    ts = min(512, s_len)

    def body(x_ref, yf_ref, ys_ref, gf_ref, gs_ref, wbf_ref, wbs_ref, wo_ref, o_ref):
        merged = (_sigmoid(gf_ref[...]) * _dot_nt(yf_ref[...], wbf_ref[...])
                  + _sigmoid(gs_ref[...]) * _dot_nt(ys_ref[...], wbs_ref[...]))
        o_ref[...] = x_ref[...] + _dot(merged.astype(BF16), wo_ref[...])

    tok = lambda w: pl.BlockSpec((ts, w), lambda i: (i, 0))
    full = lambda a: pl.BlockSpec(a.shape, lambda i: (0, 0))
    return pl.pallas_call(
        body, name=name, grid=(s_len // ts,),
        in_specs=[tok(D_MODEL), tok(ATT_W), tok(ATT_W), tok(D_MODEL), tok(D_MODEL), full(wbf), full(wbs), full(wo)],
        out_specs=tok(D_MODEL),
        out_shape=jax.ShapeDtypeStruct((s_len, D_MODEL), F32),
        compiler_params=_cparams("parallel"),
    )(x, yf, ys, gf, gs, wbf, wbs, wo)


def _merge_bwd(dx, yf, ys, gf, gs, wbf, wbs, wo, name):
    s_len = dx.shape[0]
    ts = min(512, s_len)

    def body(dx_ref, yf_ref, ys_ref, gf_ref, gs_ref, wbf_ref, wbs_ref, wo_ref,
             dyf_ref, dys_ref, dgf_ref, dgs_ref, dbf_ref, dbs_ref, mg_ref):
        bf = _dot_nt(yf_ref[...], wbf_ref[...])
        bs = _dot_nt(ys_ref[...], wbs_ref[...])
        sf = _sigmoid(gf_ref[...])
        ss = _sigmoid(gs_ref[...])
        mg_ref[...] = (sf * bf + ss * bs).astype(BF16)
        dm = _dot_nt(dx_ref[...].astype(BF16), wo_ref[...])
        dbf = (dm * sf).astype(BF16)
        dbs = (dm * ss).astype(BF16)
        dbf_ref[...] = dbf
        dbs_ref[...] = dbs
        dgf_ref[...] = (dm * bf * (sf * (1.0 - sf))).astype(BF16)
        dgs_ref[...] = (dm * bs * (ss * (1.0 - ss))).astype(BF16)
        dyf_ref[...] = _dot(dbf, wbf_ref[...]).astype(BF16)
        dys_ref[...] = _dot(dbs, wbs_ref[...]).astype(BF16)

    tok = lambda w: pl.BlockSpec((ts, w), lambda i: (i, 0))
    full = lambda a: pl.BlockSpec(a.shape, lambda i: (0, 0))
    b16o = lambda w: jax.ShapeDtypeStruct((s_len, w), BF16)
    return pl.pallas_call(
        body, name=name, grid=(s_len // ts,),
        in_specs=[tok(D_MODEL), tok(ATT_W), tok(ATT_W), tok(D_MODEL), tok(D_MODEL), full(wbf), full(wbs), full(wo)],
        out_specs=[tok(ATT_W), tok(ATT_W)] + [tok(D_MODEL)] * 5,
        out_shape=[b16o(ATT_W), b16o(ATT_W)] + [b16o(D_MODEL)] * 5,
        compiler_params=_cparams("parallel"),
    )(dx, yf, ys, gf, gs, wbf, wbs, wo)


def _mix_bwd(x, dx_in, gain, w_in, fqr, fkr, dfqn, dfkn, qg, kg, dfv, df_col, logf, dsq, dsk, dsv, dgf, dgs, name,
             ride=None):
    s_len = x.shape[0]
    ts = min(256, s_len)
    nt = s_len // ts
    gmat = _head_group_matrix()

    def body(x_ref, dxi_ref, gain_ref, w_ref, fqr_ref, fkr_ref, dfqn_ref, dfkn_ref, qg_ref, kg_ref, gm_ref,
             dfv_ref, df_ref, logf_ref, dsq_ref, dsk_ref, dsv_ref, dgf_ref, dgs_ref,
             dp_ref, dx_ref, dgain_ref, dqg_ref, dkg_ref, dbias_ref, carry):
        i = pl.program_id(0)

        @pl.when(i == 0)
        def _():
            carry[...] = jnp.zeros_like(carry)
            dgain_ref[...] = jnp.zeros_like(dgain_ref)
            dqg_ref[...] = jnp.zeros_like(dqg_ref)
            dkg_ref[...] = jnp.zeros_like(dkg_ref)
            dbias_ref[...] = jnp.zeros_like(dbias_ref)

        gm = gm_ref[...]

        def headnorm_bwd(raw, dout, g, dg_ref):
            ms = _dot_split(raw * raw, gm, 3) * (1.0 / HEAD_DIM)
            r = lax.rsqrt(ms + EPS)
            nrm = raw * r
            dg_ref[...] += jnp.sum(dout * nrm, axis=0, keepdims=True)
            dn = dout * g
            mean_h = _dot_split(dn * nrm, gm, 3) * (1.0 / HEAD_DIM)
            return r * (dn - nrm * mean_h)

        dp_ref[:, C_FQ:C_FQ + ATT_W] = headnorm_bwd(fqr_ref[...], dfqn_ref[...], qg_ref[...], dqg_ref).astype(BF16)
        dp_ref[:, C_FK:C_FK + ATT_W] = headnorm_bwd(fkr_ref[...], dfkn_ref[...], kg_ref[...], dkg_ref).astype(BF16)
        dp_ref[:, C_FV:C_FV + ATT_W] = dfv_ref[...].astype(BF16)
        dp_ref[:, C_SQ:C_SQ + ATT_W] = dsq_ref[...].astype(BF16)
        dp_ref[:, C_SK:C_SK + ATT_W] = dsk_ref[...].astype(BF16)
        dp_ref[:, C_SV:C_SV + ATT_W] = dsv_ref[...].astype(BF16)
        dp_ref[:, C_GF:C_GF + D_MODEL] = dgf_ref[...]
        dp_ref[:, C_GS:C_GS + D_MODEL] = dgs_ref[...]

        r_ = lax.broadcasted_iota(jnp.int32, (ts, ts), 0)
        c_ = lax.broadcasted_iota(jnp.int32, (ts, ts), 1)
        rev = (c_ >= r_).astype(BF16)
        dlogf = _dot_split_left(rev, df_ref[...], 3) + carry[...]
        carry[...] = dlogf[0:1, :]
        lane = lax.broadcasted_iota(jnp.int32, (ts, LANES), 1)
        dfl = jnp.where(lane < N_HEADS, dlogf * (1.0 - jnp.exp(logf_ref[...])), 0.0)
        dbias_ref[...] += jnp.sum(dfl, axis=0, keepdims=True)
        dp_ref[:, C_FL:C_FL + LANES] = dfl.astype(BF16)
        dp_ref[:, C_FL + LANES:C_SQ] = jnp.zeros((ts, C_SQ - C_FL - LANES), BF16)

        dh = _dot(dp_ref[...], w_ref[...])
        xf = x_ref[...]
        r = _rms_rinv(xf)
        xhat = xf * r
        dgain_ref[...] += jnp.sum(dh * xhat, axis=0, keepdims=True)
        dn = dh * gain_ref[...]
        dx_ref[...] = dxi_ref[...] + r * (dn - xhat * jnp.mean(dn * xhat, axis=-1, keepdims=True))

    tok = lambda w: pl.BlockSpec((ts, w), lambda i: (nt - 1 - i, 0))
    full = lambda a: pl.BlockSpec(a.shape, lambda i: (0, 0))
    row = lambda w: pl.BlockSpec((1, w), lambda i: (0, 0))
    return _ride_call(
        body, name, (nt,),
        [tok(D_MODEL), tok(D_MODEL), full(gain), full(w_in), tok(ATT_W), tok(ATT_W), tok(ATT_W), tok(ATT_W),
         full(qg), full(kg), full(gmat), tok(ATT_W), tok(LANES), tok(LANES), tok(ATT_W), tok(ATT_W), tok(ATT_W),
         tok(D_MODEL), tok(D_MODEL)],
        [tok(IN_PAD), tok(D_MODEL), row(D_MODEL), row(ATT_W), row(ATT_W), row(LANES)],
        [jax.ShapeDtypeStruct((s_len, IN_PAD), BF16), jax.ShapeDtypeStruct((s_len, D_MODEL), F32),
         jax.ShapeDtypeStruct((1, D_MODEL), F32), jax.ShapeDtypeStruct((1, ATT_W), F32),
         jax.ShapeDtypeStruct((1, ATT_W), F32), jax.ShapeDtypeStruct((1, LANES), F32)],
        [pltpu.VMEM((1, LANES), F32)], ("arbitrary",),
        (x, dx_in, gain, w_in, fqr, fkr, dfqn, dfkn, qg, kg, gmat, dfv, df_col, logf, dsq, dsk, dsv, dgf, dgs), ride)


def _ple_loss(x, p, tgt, gain, wpg, wpp, name):
    s_len = x.shape[0]
    ts = min(512, s_len)

    def body(x_ref, p_ref, t_ref, gain_ref, wpg_ref, wpp_ref, dx_ref, n_ref, ds_ref, dpp_ref, dgain_ref, loss_ref):
        i = pl.program_id(0)

        @pl.when(i == 0)
        def _():
            dgain_ref[...] = jnp.zeros_like(dgain_ref)
            loss_ref[...] = jnp.zeros_like(loss_ref)

        xf = x_ref[...]
        r = _rms_rinv(xf)
        n = xf * r
        hn = (n * gain_ref[...]).astype(BF16)
        n_ref[...] = hn
        sg = _sigmoid(_dot(hn, wpg_ref[...]))
        pp = _dot_nt(p_ref[...].astype(BF16), wpp_ref[...])
        err = (xf + sg * pp) - t_ref[...]
        sq = jnp.sum(jnp.sum(err * err, axis=1, keepdims=True), axis=0, keepdims=True)
        loss_ref[...] += (0.5 / D_MODEL) * sq
        dout = err * (1.0 / D_MODEL)
        dpp_ref[...] = (dout * sg).astype(BF16)
        ds = (dout * pp * (sg * (1.0 - sg))).astype(BF16)
        ds_ref[...] = ds
        dhn = _dot_nt(ds, wpg_ref[...])
        dgain_ref[...] += jnp.sum(dhn * n, axis=0, keepdims=True)
        dn = dhn * gain_ref[...]
        dx_ref[...] = dout + r * (dn - n * jnp.mean(dn * n, axis=-1, keepdims=True))

    tok = lambda w: pl.BlockSpec((ts, w), lambda i: (i, 0))
    full = lambda a: pl.BlockSpec(a.shape, lambda i: (0, 0))
    return pl.pallas_call(
        body, name=name, grid=(s_len // ts,),
        in_specs=[tok(D_MODEL), tok(PLE_DIM), tok(D_MODEL), full(gain), full(wpg), full(wpp)],
        out_specs=[tok(D_MODEL), tok(D_MODEL), tok(D_MODEL), tok(D_MODEL),
                   pl.BlockSpec((1, D_MODEL), lambda i: (0, 0)), pl.BlockSpec((8, LANES), lambda i: (0, 0))],
        out_shape=[jax.ShapeDtypeStruct((s_len, D_MODEL), F32), jax.ShapeDtypeStruct((s_len, D_MODEL), BF16),
                   jax.ShapeDtypeStruct((s_len, D_MODEL), BF16), jax.ShapeDtypeStruct((s_len, D_MODEL), BF16),
                   jax.ShapeDtypeStruct((1, D_MODEL), F32), jax.ShapeDtypeStruct((8, LANES), F32)],
        compiler_params=_cparams("arbitrary"),
    )(x, p, tgt, gain, wpg, wpp)


def _exchange(x, name, broadcast):
    def body(x_ref, out_ref, send_sems, recv_sems, local_sem):
        _exchange_start(x_ref, out_ref, send_sems, recv_sems, local_sem, broadcast)
        _exchange_wait(x_ref, out_ref, send_sems, recv_sems, local_sem, broadcast)

    return pl.pallas_call(
        body, name=name,
        in_specs=[EXCHANGE_SPEC],
        out_specs=EXCHANGE_SPEC,
        out_shape=_exchange_shape(x, broadcast),
        scratch_shapes=list(EXCHANGE_SEMS),
        compiler_params=pltpu.CompilerParams(has_side_effects=True),
    )(x)


EXCHANGE_SPEC = pl.BlockSpec(memory_space=pl.ANY)
EXCHANGE_SEMS = (pltpu.SemaphoreType.DMA((N_DEV - 1,)), pltpu.SemaphoreType.DMA((N_DEV - 1,)), pltpu.SemaphoreType.DMA)


def _exchange_shape(x, broadcast):
    return jax.ShapeDtypeStruct((N_DEV,) + tuple(x.shape if broadcast else x.shape[1:]), x.dtype)


def _exchange_copies(x_ref, out_ref, send_sems, recv_sems, local_sem, broadcast, with_recv=True):
    mx, my, mc = lax.axis_index("x"), lax.axis_index("y"), lax.axis_index("c")
    me = 4 * mx + 2 * my + mc

    def src(idx):
        return x_ref if broadcast else x_ref.at[idx]

    local = pltpu.make_async_copy(src(me), out_ref.at[me], local_sem)
    pairs = []
    for k in range(1, N_DEV):
        px = (1 - mx) if k & 4 else mx
        py = (1 - my) if k & 2 else my
        pc = (1 - mc) if k & 1 else mc
        peer = 4 * px + 2 * py + pc
        sems = dict(send_sem=send_sems.at[k - 1], recv_sem=recv_sems.at[k - 1], device_id=(px, py, pc), device_id_type=MESH)
        recv = pltpu.make_async_remote_copy(src_ref=src(peer), dst_ref=out_ref.at[peer], **sems) if with_recv else None
        pairs.append((pltpu.make_async_remote_copy(src_ref=src(peer), dst_ref=out_ref.at[me], **sems), recv))
    return local, pairs


def _exchange_start(*refs_and_mode):
    local, pairs = _exchange_copies(*refs_and_mode, with_recv=False)
    local.start()
    for send, _ in pairs:
        send.start()


def _exchange_wait(*refs_and_mode):
    local, pairs = _exchange_copies(*refs_and_mode)
    for _, recv in pairs:
        recv.wait_recv()
    for send, _ in pairs:
        send.wait_send()
    local.wait()


def _riding(body, grid, n_in, n_out, ride):
    if ride is None:
        return body
    broadcast = ride[1]

    def wrapped(*refs):
        ins, x_ref = refs[:n_in], refs[n_in]
        outs, out_ref = refs[n_in + 1:n_in + 1 + n_out], refs[n_in + 1 + n_out]
        scratch, sems = refs[n_in + 2 + n_out:-3], refs[-3:]
        step = pl.program_id(0)
        for d in range(1, len(grid)):
            step = step * grid[d] + pl.program_id(d)
        total = 1
        for g in grid:
            total *= g

        @pl.when(step == 0)
        def _():
            _exchange_start(x_ref, out_ref, *sems, broadcast)

        body(*ins, *outs, *scratch)

        @pl.when(step == total - 1)
        def _():
            _exchange_wait(x_ref, out_ref, *sems, broadcast)

    return wrapped


def _ride_call(body, name, grid, in_specs, out_specs, out_shape, scratch_shapes, sem, operands, ride):
    if ride is None:
        return pl.pallas_call(body, name=name, grid=grid, in_specs=in_specs, out_specs=out_specs, out_shape=out_shape,
                              scratch_shapes=scratch_shapes, compiler_params=_cparams(*sem))(*operands)
    return pl.pallas_call(
        _riding(body, grid, len(in_specs), len(out_specs), ride), name=name, grid=grid,
        in_specs=list(in_specs) + [EXCHANGE_SPEC], out_specs=list(out_specs) + [EXCHANGE_SPEC],
        out_shape=list(out_shape) + [_exchange_shape(*ride)],
        scratch_shapes=list(scratch_shapes) + list(EXCHANGE_SEMS),
        compiler_params=_cparams(*(["arbitrary"] * len(grid))),
    )(*operands, ride[0])


def _adamw_math(w, g, m, v):
    m2 = ADAM_B1 * m + (1.0 - ADAM_B1) * g
    v2 = ADAM_B2 * v + (1.0 - ADAM_B2) * (g * g)
    m_hat = m2 / (1.0 - ADAM_B1 ** ADAM_STEP)
    v_hat = v2 / (1.0 - ADAM_B2 ** ADAM_STEP)
    delta = -ADAM_LR * (m_hat / (jnp.sqrt(v_hat) + ADAM_EPS) + ADAM_WD * w)
    return delta, m2, v2


def _sum_parts(parts, name, tr):
    _, rows, cols = parts.shape

    def body(p_ref, g_ref):
        g = p_ref[0].astype(F32)
        for s in range(1, N_DEV):
            g = g + p_ref[s].astype(F32)
        g_ref[...] = g

    return pl.pallas_call(
        body, name=name, grid=(rows // tr,),
        in_specs=[pl.BlockSpec((N_DEV, tr, cols), lambda i: (0, i, 0))],
        out_specs=pl.BlockSpec((tr, cols), lambda i: (i, 0)),
        out_shape=jax.ShapeDtypeStruct((rows, cols), F32),
        compiler_params=_cparams("parallel"),
    )(parts)


ADAM_SPLIT_ELEMS = 400_000


def _adamw_shard(g, w, m, v, name):
    rows, cols = w.shape
    tr = rows // 2 if rows * cols > ADAM_SPLIT_ELEMS else rows

    def body(g_ref, w_ref, m_ref, v_ref, d_ref, m2_ref, v2_ref):
        d_ref[...], m2_ref[...], v2_ref[...] = _adamw_math(w_ref[...], g_ref[...], m_ref[...], v_ref[...])

    blk = pl.BlockSpec((tr, cols), lambda i: (i, 0))
    return pl.pallas_call(
        body, name=name, grid=(rows // tr,),
        in_specs=[blk] * 4, out_specs=[blk] * 3,
        out_shape=[jax.ShapeDtypeStruct((rows, cols), F32)] * 3,
        compiler_params=_cparams("parallel"),
    )(g, w, m, v)


def _adamw(parts, w, m, v, name, tr):
    rows, cols = w.shape

    def body(p_ref, w_ref, m_ref, v_ref, g_ref, d_ref, m2_ref, v2_ref):
        g = p_ref[0].astype(F32)
        for s in range(1, N_DEV):
            g = g + p_ref[s].astype(F32)
        g_ref[...] = g
        d_ref[...], m2_ref[...], v2_ref[...] = _adamw_math(w_ref[...], g, m_ref[...], v_ref[...])

    blk = pl.BlockSpec((tr, cols), lambda i: (i, 0))
    return pl.pallas_call(
        body, name=name, grid=(rows // tr,),
        in_specs=[pl.BlockSpec((N_DEV, tr, cols), lambda i: (0, i, 0)), blk, blk, blk],
        out_specs=[blk] * 4,
        out_shape=[jax.ShapeDtypeStruct((rows, cols), F32)] * 4,
        compiler_params=_cparams("parallel"),
    )(parts, w, m, v)


TRANSPOSED = frozenset(("ffn1_w_gate", "ffn1_w_up", "w_in", "w_branch_fox", "w_branch_sb", "ffn2_w_gate", "ffn2_w_up",
                        "w_ple_proj"))
F_PAD_ROWS = C_SQ - FL_REAL_END


def _pack(pieces, group, dtype):
    out = []
    for name, rows in PACK_GROUPS[group]:
        r = pieces[name].T if name in TRANSPOSED else pieces[name]
        r = r.reshape(-1, D_MODEL).astype(dtype)
        if r.shape[0] != rows:
            r = jnp.pad(r, ((0, rows - r.shape[0]), (0, 0)))
        out.append(r)
    return jnp.concatenate(out, axis=0)


def _real_rows(name):
    return W_IN_ROWS if name == "w_in" else PACK_OFF[name][1]


def _gathered(got, name, shape):
    off, _ = PACK_OFF[name]
    return got[:, off:off + _real_rows(name), :].reshape(shape)


def _w_in_t_padded(got):
    t = _gathered(got, "w_in", (IN_REAL, D_MODEL))
    return jnp.concatenate([t[:FL_REAL_END], jnp.zeros((F_PAD_ROWS, D_MODEL), t.dtype), t[FL_REAL_END:]], axis=0)


def _pack_chunks(grads, group):
    out = []
    for name, rows in PACK_GROUPS[group]:
        g = grads[name]
        if name == "w_in":
            g = jnp.concatenate([g[:FL_REAL_END], g[C_SQ:]], axis=0)
        c = g.reshape(N_DEV, -1, D_MODEL).astype(BF16)
        if c.shape[1] != rows:
            c = jnp.pad(c, ((0, 0), (0, rows - c.shape[1]), (0, 0)))
        out.append(c)
    return jnp.concatenate(out, axis=1)


def _shard_grad(packed, name, shape):
    off, _ = PACK_OFF[name]
    rows = packed[off:off + _real_rows(name), :]
    return rows.reshape(shape[1], shape[0]).T if name in TRANSPOSED else rows.reshape(shape)


WEIGHT_NAMES = ['ffn1_norm', 'ffn1_w_gate', 'ffn1_w_up', 'ffn1_w_down', 'mix_norm', 'w_in', 'forget_bias', 'q_norm',
                'k_norm', 'w_branch_fox', 'w_branch_sb', 'w_out', 'ffn2_norm', 'ffn2_w_gate', 'ffn2_w_up',
                'ffn2_w_down', 'ple_norm', 'w_ple_gate', 'w_ple_proj']
SMALL_NAMES = ('ffn1_norm', 'mix_norm', 'ffn2_norm', 'ple_norm', 'q_norm', 'k_norm', 'forget_bias')
Q_OFF, K_OFF, B_OFF, LOSS_OFF = 0, HEAD_DIM, 2 * HEAD_DIM, 2 * HEAD_DIM + N_HEADS


def _pack_small(vals, loss=None):
    tail = [vals['q_norm'].reshape(1, -1), vals['k_norm'].reshape(1, -1), vals['forget_bias'].reshape(1, -1)]
    used = LOSS_OFF
    if loss is not None:
        tail.append(loss.reshape(1, 1))
        used += 1
    tail.append(jnp.zeros((1, D_MODEL - used), F32))
    rows = [vals[n].reshape(1, D_MODEL) for n in SMALL_NAMES[:4]] + [jnp.concatenate(tail, axis=1)]
    rows.append(jnp.zeros((SMALL_ROWS - len(rows), D_MODEL), F32))
    return jnp.concatenate(rows, axis=0)


def _unpack_small(packed, name, shape):
    if name in SMALL_NAMES[:4]:
        return packed[SMALL_NAMES.index(name)].reshape(shape)
    off, n = {'q_norm': (Q_OFF, HEAD_DIM), 'k_norm': (K_OFF, HEAD_DIM), 'forget_bias': (B_OFF, N_HEADS)}[name]
    return packed[4, off:off + n].reshape(shape)


def _step(x, p, tgt, w):
    row = lambda a: a.reshape(1, -1).astype(F32)
    g_ffn1, g_mix, g_ffn2, g_ple = (row(w[n]) for n in SMALL_NAMES[:4])
    qg = jnp.tile(row(w['q_norm']), (1, N_HEADS))
    kg = jnp.tile(row(w['k_norm']), (1, N_HEADS))
    bias = jnp.pad(row(w['forget_bias']), ((0, 0), (0, LANES - N_HEADS)))
    half = D_FF // 2
    grads = {}

    blk = lambda n: PACK_OFF[n][0] // FFN_SHARD
    ffn1 = tuple(blk(n) for n in ("ffn1_w_gate", "ffn1_w_up", "ffn1_w_down"))
    ffn2 = tuple(blk(n) for n in ("ffn2_w_gate", "ffn2_w_up", "ffn2_w_down"))
    got0 = _exchange(_pack(w, 0, BF16), "gather_ffn1", True)
    x1, g1, u1, h1, got1 = _ffn_fwd(x, g_ffn1, got0, ffn1, "ffn1_fwd", ride=(_pack(w, 1, BF16), True))
    w_in = _w_in_t_padded(got1)
    wbf = _gathered(got1, "w_branch_fox", (D_MODEL, ATT_W))
    wbs = _gathered(got1, "w_branch_sb", (D_MODEL, ATT_W))
    wo = _gathered(got1, "w_out", (D_MODEL, D_MODEL))
    (hmix, fqr, fkr, fqn, fkn, fv, logf, f_col, f_row, sq, sk, sv, gf, gs, kmax, got2) = _mix_fwd(
        x1, g_mix, w_in, bias, qg, kg, "mix_fwd", ride=(_pack(w, 2, BF16), True))
    wpg = _gathered(got2, "w_ple_gate", (D_MODEL, D_MODEL))
    wpp = _gathered(got2, "w_ple_proj", (D_MODEL, PLE_DIM))
    y_fox, lse = _fox_fwd(fqn, fkn, fv, f_col, f_row, kmax, "fox_fwd")
    y_sb, y_sb32 = _sb_fwd(sq, sk, sv, "sb_fwd")
    x2 = _merge_fwd(x1, y_fox, y_sb, gf, gs, wbf, wbs, wo, "merge_fwd")
    x3, g2, u2, h2, = _ffn_fwd(x2, g_ffn2, got2, ffn2, "ffn2_fwd")
    dx3, n_ple, ds_ple, dpp, dg_ple, loss = _ple_loss(x3, p, tgt, g_ple, wpg, wpp, "ple_loss")

    grads['w_ple_gate'] = _wgrad(n_ple, ds_ple, "dw_ple_gate", D_MODEL, D_MODEL)
    grads['w_ple_proj'] = _wgrad(dpp, p, "dw_ple_proj", D_MODEL, PLE_DIM)
    dg2, du2, act2 = _ffn_bwd_hidden(dx3, g2, u2, got2, ffn2[2], "ffn2_bwd_hidden")
    grads['ffn2_w_gate'] = _wgrad(dg2, h2, "dw_ffn2_gate", half, D_MODEL)
    grads['ffn2_w_up'] = _wgrad(du2, h2, "dw_ffn2_up", half, D_MODEL)
    grads['ffn2_w_down'] = _wgrad(act2, dx3, "dw_ffn2_down", half, D_MODEL)
    dx2, dg_ffn2 = _ffn_bwd_input(x2, dx3, g_ffn2, dg2, du2, got2, ffn2[:2], "ffn2_bwd_input")
    dyf, dys, dgf, dgs, dbf, dbs, merged = _merge_bwd(dx2, y_fox, y_sb, gf, gs, wbf, wbs, wo, "merge_bwd")
    dfqn, dfkn, dfv, dft, dfq = _fox_bwd(fqn, fkn, fv, dyf, y_fox, lse, f_col, f_row, kmax, "fox_bwd")
    dsq, dsk, dsv = _sb_bwd(sq, sk, sv, dys, y_sb32, "sb_bwd")
    s_len = x.shape[0]
    df_heads = dft[:, :2, :] + jnp.stack([dfq[:, :, 0], dfq[:, :, HEAD_DIM]], axis=1)
    df_col = jnp.pad(df_heads.reshape(N_HEADS, s_len).T, ((0, 0), (0, LANES - N_HEADS)))
    dproj, dx1, dg_mix, dqg, dkg, dbias, part2 = _mix_bwd(
        x1, dx2, g_mix, w_in, fqr, fkr, dfqn, dfkn, qg, kg, dfv, df_col, logf, dsq, dsk, dsv, dgf, dgs, "mix_bwd",
        ride=(_pack_chunks(grads, 2), False))
    grads['w_in'] = _wgrad(dproj, hmix, "dw_in", IN_PAD // 3, D_MODEL)
    grads['w_branch_fox'] = _wgrad(dbf, y_fox, "dw_branch_fox", D_MODEL, ATT_W)
    grads['w_branch_sb'] = _wgrad(dbs, y_sb, "dw_branch_sb", D_MODEL, ATT_W)
    grads['w_out'] = _wgrad(merged, dx2, "dw_out", D_MODEL, D_MODEL)
    dg1, du1, act1, part1 = _ffn_bwd_hidden(dx1, g1, u1, got0, ffn1[2], "ffn1_bwd_hidden",
                                            ride=(_pack_chunks(grads, 1), False))
    grads['ffn1_w_gate'] = _wgrad(dg1, h1, "dw_ffn1_gate", half, D_MODEL)
    grads['ffn1_w_up'] = _wgrad(du1, h1, "dw_ffn1_up", half, D_MODEL)
    grads['ffn1_w_down'] = _wgrad(act1, dx1, "dw_ffn1_down", half, D_MODEL)
    dx0, dg_ffn1, part0 = _ffn_bwd_input(x, dx1, g_ffn1, dg1, du1, got0, ffn1[:2], "ffn1_bwd_input",
                                         ride=(_pack_chunks(grads, 0), False))

    fold = lambda a: a.reshape(N_HEADS, HEAD_DIM).sum(axis=0).reshape(1, HEAD_DIM)
    small_g = {'ffn1_norm': dg_ffn1, 'mix_norm': dg_mix, 'ffn2_norm': dg_ffn2, 'ple_norm': dg_ple,
               'q_norm': fold(dqg), 'k_norm': fold(dkg), 'forget_bias': dbias[:, :N_HEADS]}
    return loss[0, 0], dx0, (part0, part1, part2), small_g


def kernel(x, p, ffn1_norm, ffn1_w_gate, ffn1_w_up, ffn1_w_down, mix_norm, w_in, forget_bias, q_norm, k_norm, w_branch_fox, w_branch_sb, w_out, ffn2_norm, ffn2_w_gate, ffn2_w_up, ffn2_w_down, ple_norm, w_ple_gate, w_ple_proj, loss_target, m_ffn1_norm, m_ffn1_w_gate, m_ffn1_w_up, m_ffn1_w_down, m_mix_norm, m_w_in, m_forget_bias, m_q_norm, m_k_norm, m_w_branch_fox, m_w_branch_sb, m_w_out, m_ffn2_norm, m_ffn2_w_gate, m_ffn2_w_up, m_ffn2_w_down, m_ple_norm, m_w_ple_gate, m_w_ple_proj, v_ffn1_norm, v_ffn1_w_gate, v_ffn1_w_up, v_ffn1_w_down, v_mix_norm, v_w_in, v_forget_bias, v_q_norm, v_k_norm, v_w_branch_fox, v_w_branch_sb, v_w_out, v_ffn2_norm, v_ffn2_w_gate, v_ffn2_w_up, v_ffn2_w_down, v_ple_norm, v_w_ple_gate, v_w_ple_proj):
    args = dict(locals())
    w = {n: args[n][0] for n in WEIGHT_NAMES}
    m = {n: args["m_" + n][0] for n in WEIGHT_NAMES}
    v = {n: args["v_" + n][0] for n in WEIGHT_NAMES}
    loss, dx, parts, small_g = _step(x[0], p[0, 0], loss_target[0], w)

    big = {}
    for grp, part in enumerate(parts):
        summed = _sum_parts(part, f"sum_grads_{grp}", ADAM_TILE_ROWS[grp])
        for n, _ in PACK_GROUPS[grp]:
            g = _shard_grad(summed, n, w[n].shape)
            big[n] = (g,) + tuple(_adamw_shard(g, w[n], m[n], v[n], "adamw_" + n))
    small_parts = _exchange(_pack_small(small_g, loss), "gather_small", True)
    sw, sm, sv = (_pack_small(t) for t in (w, m, v))
    small = _adamw(small_parts, sw, sm, sv, "adamw_small", SMALL_ROWS)

    outs = [small[0][4, LOSS_OFF], dx.reshape(x.shape)]
    for kind in range(4):
        for n in WEIGHT_NAMES:
            shape = args[n].shape
            outs.append(_unpack_small(small[kind], n, shape) if n in SMALL_NAMES else big[n][kind].reshape(shape))
    return tuple(outs)
```

```python
import jax
import jax.numpy as jnp
from jax import lax
from jax.experimental import pallas as pl
from jax.experimental.pallas import tpu as pltpu

F32 = jnp.float32
BF16 = jnp.bfloat16

D_MODEL = 1024
D_FF = 2816
N_HEADS = 8
HEAD_DIM = 64
ATT_W = N_HEADS * HEAD_DIM
PLE_DIM = 256
EPS = 1e-6
N_DEV = 8
MESH = pl.DeviceIdType.MESH

LANES = 128
V7X_SCOPED_VMEM_BYTES = 56 * 1024 * 1024

C_FQ, C_FK, C_FV, C_FL = 0, 512, 1024, 1536
C_SQ, C_SK, C_SV, C_GF, C_GS = 1792, 2304, 2816, 3328, 4352
IN_PAD = 5376
IN_REAL = 5128
FL_REAL_END = 1544

ADAM_LR = 0.001
ADAM_B1 = 0.9
ADAM_B2 = 0.999
ADAM_EPS = 1e-08
ADAM_WD = 0.01
ADAM_STEP = 10

PACK_ROWS = {"ffn1_w_gate": 352, "ffn1_w_up": 352, "ffn1_w_down": 352, "w_in": 656, "w_branch_fox": 64,
             "w_branch_sb": 64, "w_out": 128, "ffn2_w_gate": 352, "ffn2_w_up": 352, "ffn2_w_down": 352,
             "w_ple_gate": 128, "w_ple_proj": 32}
GATHER_GROUPS = (
    ("ffn1_w_gate", "ffn1_w_up", "ffn1_w_down"),
    ("w_in", "w_branch_fox", "w_branch_sb", "w_out"),
    ("ffn2_w_gate", "ffn2_w_up", "ffn2_w_down", "w_ple_gate", "w_ple_proj"),
)
SCATTER_GROUPS = (
    ("ffn1_w_gate", "ffn1_w_up", "ffn1_w_down"),
    ("w_in",),
    ("ffn2_w_gate", "ffn2_w_up", "ffn2_w_down", "w_ple_gate", "w_ple_proj", "w_branch_fox", "w_branch_sb", "w_out"),
)
SUM_TILE_ROWS = (528, 656, 368)


def _offsets(groups):
    off = {}
    for grp in groups:
        o = 0
        for n in grp:
            off[n] = o
            o += PACK_ROWS[n]
    return off


GATHER_OFF = _offsets(GATHER_GROUPS)
SCATTER_OFF = _offsets(SCATTER_GROUPS)
W_IN_ROWS = 641

SMALL_ROWS = 8


def _cparams(*sem):
    return pltpu.CompilerParams(dimension_semantics=sem, vmem_limit_bytes=V7X_SCOPED_VMEM_BYTES)


def _dot(a, b):
    return jnp.dot(a, b, preferred_element_type=F32)


def _dot_nt(a, b):
    return lax.dot_general(a, b, (((1,), (1,)), ((), ())), preferred_element_type=F32)


def _dot_tn(a, b):
    return lax.dot_general(a, b, (((0,), (0,)), ((), ())), preferred_element_type=F32)


def _split(x, parts):
    out = []
    r = x
    for _ in range(parts):
        p = r.astype(BF16)
        out.append(p)
        r = r - p.astype(F32)
    return out


def _dot_split(x, m, parts):
    acc = None
    for p in _split(x, parts):
        t = _dot(p, m)
        acc = t if acc is None else acc + t
    return acc


def _dot_split_left(m, x, parts):
    acc = None
    for p in _split(x, parts):
        t = _dot(m, p)
        acc = t if acc is None else acc + t
    return acc


def _rms_rinv(xf):
    return lax.rsqrt(jnp.mean(xf * xf, axis=-1, keepdims=True) + EPS)


def _sigmoid(x):
    return 1.0 / (1.0 + jnp.exp(-x))


def _softplus_neg_abs(z):
    return jnp.log(1.0 + jnp.exp(-jnp.abs(z)))


FFN_SHARD = D_FF // N_DEV
FFN_CHUNK = 4


def _ffn_w_spec(blk, index_map):
    return pl.BlockSpec((FFN_CHUNK, FFN_SHARD, D_MODEL), lambda *g: (index_map(*g), blk, 0))


def _ffn_w(ref):
    return ref[...].reshape(FFN_CHUNK * FFN_SHARD, D_MODEL)


def _ffn_fwd(x, gain, wbuf, blks, name, ride=None):
    s_len = x.shape[0]
    ts = min(512, s_len)
    fc = FFN_CHUNK * FFN_SHARD
    nt, nc = s_len // ts, D_FF // fc

    def body(x_ref, gain_ref, wg_ref, wu_ref, wd_ref, y_ref, g_ref, u_ref, h_ref, acc_scr):
        j = pl.program_id(1)

        @pl.when(j == 0)
        def _():
            xf = x_ref[...]
            h_ref[...] = ((xf * _rms_rinv(xf)) * gain_ref[...]).astype(BF16)
            acc_scr[...] = jnp.zeros_like(acc_scr)

        h = h_ref[...]
        g = _dot_nt(h, _ffn_w(wg_ref))
        u = _dot_nt(h, _ffn_w(wu_ref))
        g_ref[...] = g.astype(BF16)
        u_ref[...] = u.astype(BF16)
        a = (g * _sigmoid(g) * u).astype(BF16)
        acc_scr[...] += _dot(a, _ffn_w(wd_ref))

        @pl.when(j == nc - 1)
        def _():
            y_ref[...] = x_ref[...] + 0.5 * acc_scr[...]

    tok = pl.BlockSpec((ts, D_MODEL), lambda i, j: (i, 0))
    hid = pl.BlockSpec((ts, fc), lambda i, j: (i, j))
    return _ride_call(
        body, name, (nt, nc),
        [tok, pl.BlockSpec((1, D_MODEL), lambda i, j: (0, 0))] + [_ffn_w_spec(b, lambda i, j: j) for b in blks],
        [tok, hid, hid, tok],
        [jax.ShapeDtypeStruct((s_len, D_MODEL), F32), jax.ShapeDtypeStruct((s_len, D_FF), BF16),
         jax.ShapeDtypeStruct((s_len, D_FF), BF16), jax.ShapeDtypeStruct((s_len, D_MODEL), BF16)],
        [pltpu.VMEM((ts, D_MODEL), F32)], ("parallel", "arbitrary"), (x, gain, wbuf, wbuf, wbuf), ride)


def _ffn_bwd_hidden(dy, g, u, wbuf, blk, name, ride=None):
    s_len = dy.shape[0]
    ts = min(512, s_len)
    fc = FFN_CHUNK * FFN_SHARD
    nt, nc = s_len // ts, D_FF // fc

    def body(dy_ref, g_ref, u_ref, wd_ref, dg_ref, du_ref, act_ref):
        da = 0.5 * _dot_nt(dy_ref[...].astype(BF16), _ffn_w(wd_ref))
        gf = g_ref[...].astype(F32)
        uf = u_ref[...].astype(F32)
        sg = _sigmoid(gf)
        silu = gf * sg
        dg_ref[...] = (da * uf * (sg * (1.0 + gf * (1.0 - sg)))).astype(BF16)
        du_ref[...] = (da * silu).astype(BF16)
        act_ref[...] = (0.5 * silu * uf).astype(BF16)

    hid = pl.BlockSpec((ts, fc), lambda c, t: (t, c))
    return _ride_call(
        body, name, (nc, nt),
        [pl.BlockSpec((ts, D_MODEL), lambda c, t: (t, 0)), hid, hid, _ffn_w_spec(blk, lambda c, t: c)],
        [hid, hid, hid], [jax.ShapeDtypeStruct((s_len, D_FF), BF16)] * 3, [], ("parallel", "parallel"),
        (dy, g, u, wbuf), ride)


def _ffn_bwd_input(x, dy, gain, dg, du, wbuf, blks, name, ride=None):
    s_len = x.shape[0]
    ts = min(512, s_len)
    fc = FFN_CHUNK * FFN_SHARD
    nt, nc = s_len // ts, D_FF // fc

    def body(x_ref, dy_ref, gain_ref, dg_ref, du_ref, wg_ref, wu_ref, dx_ref, dgain_ref, acc):
        i = pl.program_id(0)
        j = pl.program_id(1)
        part = _dot(dg_ref[...], _ffn_w(wg_ref)) + _dot(du_ref[...], _ffn_w(wu_ref))

        @pl.when(j == 0)
        def _():
            acc[...] = part

        @pl.when(j > 0)
        def _():
            acc[...] += part

        @pl.when(j == nc - 1)
        def _():
            xf = x_ref[...]
            r = _rms_rinv(xf)
            xhat = xf * r
            dh = acc[...]
            dgp = jnp.sum(dh * xhat, axis=0, keepdims=True)

            @pl.when(i == 0)
            def _():
                dgain_ref[...] = dgp

            @pl.when(i > 0)
            def _():
                dgain_ref[...] += dgp

            dn = dh * gain_ref[...]
            dx_ref[...] = dy_ref[...] + r * (dn - xhat * jnp.mean(dn * xhat, axis=-1, keepdims=True))

    tok = pl.BlockSpec((ts, D_MODEL), lambda i, j: (i, 0))
    row = pl.BlockSpec((1, D_MODEL), lambda i, j: (0, 0))
    hid = pl.BlockSpec((ts, fc), lambda i, j: (i, j))
    return _ride_call(
        body, name, (nt, nc), [tok, tok, row, hid, hid] + [_ffn_w_spec(b, lambda i, j: j) for b in blks], [tok, row],
        [jax.ShapeDtypeStruct((s_len, D_MODEL), F32), jax.ShapeDtypeStruct((1, D_MODEL), F32)],
        [pltpu.VMEM((ts, D_MODEL), F32)], ("arbitrary", "arbitrary"), (x, dy, gain, dg, du, wbuf, wbuf), ride)


def _wgrad(a, b, name, tk, tn):
    s_len, k_dim = a.shape
    n_dim = b.shape[1]
    ts = min(1024, s_len)
    ns = s_len // ts

    def body(a_ref, b_ref, o_ref, acc):
        s = pl.program_id(2)
        p = _dot_tn(a_ref[...].astype(BF16), b_ref[...].astype(BF16))

        @pl.when(s == 0)
        def _():
            acc[...] = p

        @pl.when(s > 0)
        def _():
            acc[...] += p

        @pl.when(s == ns - 1)
        def _():
            o_ref[...] = acc[...].astype(BF16)

    return pl.pallas_call(
        body, name=name, grid=(k_dim // tk, n_dim // tn, ns),
        in_specs=[
            pl.BlockSpec((ts, tk), lambda k, n, s: (s, k)),
            pl.BlockSpec((ts, tn), lambda k, n, s: (s, n)),
        ],
        out_specs=pl.BlockSpec((tk, tn), lambda k, n, s: (k, n)),
        out_shape=jax.ShapeDtypeStruct((k_dim, n_dim), BF16),
        scratch_shapes=[pltpu.VMEM((tk, tn), F32)],
        compiler_params=_cparams("parallel", "parallel", "arbitrary"),
    )(a, b)


HEAD_SUM_PARTS = 2


def _head_group_matrix():
    r = lax.broadcasted_iota(jnp.int32, (ATT_W, ATT_W), 0) // HEAD_DIM
    c = lax.broadcasted_iota(jnp.int32, (ATT_W, ATT_W), 1) // HEAD_DIM
    return (r == c).astype(BF16)


def _mix_fwd(x, gain, w_in, bias, qg, kg, name, ride=None):
    s_len = x.shape[0]
    ts = min(256, s_len)
    nt = s_len // ts
    gmat = _head_group_matrix()

    def body(x_ref, gain_ref, w_ref, bias_ref, qg_ref, kg_ref, gm_ref,
             h_ref, fqr_ref, fkr_ref, fqn_ref, fkn_ref, fv_ref, logf_ref, f_ref, ft_ref,
             sq_ref, sk_ref, sv_ref, gf_ref, gs_ref, kmax_ref, carry):
        i = pl.program_id(0)
        xf = x_ref[...]
        h = ((xf * _rms_rinv(xf)) * gain_ref[...]).astype(BF16)
        h_ref[...] = h
        gm = gm_ref[...]

        def proj(lo, n):
            return _dot_nt(h, w_ref[lo:lo + n, :])

        def headnorm(raw, g):
            ms = _dot_split(raw * raw, gm, HEAD_SUM_PARTS) * (1.0 / HEAD_DIM)
            return ((raw * lax.rsqrt(ms + EPS)) * g).astype(BF16)

        fq = proj(C_FQ, ATT_W)
        fqr_ref[...] = fq
        fqn_ref[...] = headnorm(fq, qg_ref[...])
        fk = proj(C_FK, ATT_W)
        fkr_ref[...] = fk
        fkn = headnorm(fk, kg_ref[...])
        fkn_ref[...] = fkn
        kn2 = jnp.max(_dot_split(jnp.square(fkn.astype(F32)), gm, HEAD_SUM_PARTS), axis=0, keepdims=True)

        @pl.when(i == 0)
        def _():
            kmax_ref[...] = kn2

        @pl.when(i > 0)
        def _():
            kmax_ref[...] = jnp.maximum(kmax_ref[...], kn2)
        fv_ref[...] = proj(C_FV, ATT_W).astype(BF16)
        sq_ref[...] = proj(C_SQ, ATT_W).astype(BF16)
        sk_ref[...] = proj(C_SK, ATT_W).astype(BF16)
        sv_ref[...] = proj(C_SV, ATT_W).astype(BF16)
        gf_ref[...] = proj(C_GF, D_MODEL)
        gs_ref[...] = proj(C_GS, D_MODEL)

        fl = proj(C_FL, LANES) + bias_ref[...]
        lane = lax.broadcasted_iota(jnp.int32, fl.shape, 1)
        logf = jnp.where(lane < N_HEADS, jnp.minimum(fl, 0.0) - _softplus_neg_abs(fl), 0.0)
        logf_ref[...] = logf

        @pl.when(i == 0)
        def _():
            carry[...] = jnp.zeros_like(carry)

        r = lax.broadcasted_iota(jnp.int32, (ts, ts), 0)
        c = lax.broadcasted_iota(jnp.int32, (ts, ts), 1)
        tri = (r >= c).astype(BF16)
        f_tile = _dot_split_left(tri, logf, 3) + carry[...]
        f_ref[...] = f_tile
        ft_ref[...] = f_tile.T[:N_HEADS, :]
        carry[...] = f_tile[ts - 1:ts, :]

    tok = lambda w: pl.BlockSpec((ts, w), lambda i: (i, 0))
    full = lambda a: pl.BlockSpec(a.shape, lambda i: (0, 0))
    f32o = lambda w: jax.ShapeDtypeStruct((s_len, w), F32)
    b16o = lambda w: jax.ShapeDtypeStruct((s_len, w), BF16)
    return _ride_call(
        body, name, (nt,),
        [tok(D_MODEL), full(gain), full(w_in), full(bias), full(qg), full(kg), full(gmat)],
        [
            tok(D_MODEL), tok(ATT_W), tok(ATT_W), tok(ATT_W), tok(ATT_W), tok(ATT_W), tok(LANES), tok(LANES),
            pl.BlockSpec((N_HEADS, ts), lambda i: (0, i)),
            tok(ATT_W), tok(ATT_W), tok(ATT_W), tok(D_MODEL), tok(D_MODEL),
            pl.BlockSpec((1, ATT_W), lambda i: (0, 0)),
        ],
        [
            b16o(D_MODEL), f32o(ATT_W), f32o(ATT_W), b16o(ATT_W), b16o(ATT_W), b16o(ATT_W), f32o(LANES), f32o(LANES),
            jax.ShapeDtypeStruct((N_HEADS, s_len), F32),
            b16o(ATT_W), b16o(ATT_W), b16o(ATT_W), f32o(D_MODEL), f32o(D_MODEL),
            jax.ShapeDtypeStruct((1, ATT_W), F32),
        ],
        [pltpu.VMEM((1, LANES), F32)], ("arbitrary",), (x, gain, w_in, bias, qg, kg, gmat), ride)


ATT_T = 256
ATT_ROWS = 2
EXP_ZERO = 104.0


def _att_tiling(s_len):
    t = min(ATT_T, s_len)
    nr = min(ATT_ROWS, s_len // t)
    return t, nr, s_len // (t * nr)


def _pair_specs(s_len, tq):
    qblk = pl.BlockSpec((tq, LANES), lambda hp, i: (i, hp))
    kvfull = pl.BlockSpec((s_len, LANES), lambda hp, i: (0, hp))
    return qblk, kvfull


def _walk_tiles(i, nr, load, sub, flush, more, init):
    base = i * nr
    carries = list(init)
    for r in range(nr):
        for kk in range(r, -1, -1):
            (carries[r],), side = sub([r], load(base + kk), [carries[r]], kk == r)
            flush(base + kk, side)

    def cond(state):
        return jnp.logical_and(state[0] < base, state[1] > 0)

    def step(state):
        n, _, cs = state
        kb = base - 1 - n
        cs, side = sub(list(range(nr)), load(kb), list(cs), False)
        flush(kb, side)
        return n + 1, more(cs, kb - 1), tuple(cs)

    return lax.while_loop(cond, step, (jnp.int32(0), more(carries, base - 1), tuple(carries)))[2]


def _stack(parts):
    return parts[0] if len(parts) == 1 else jnp.concatenate(parts, axis=0)


def _stacked_halves(x, lo):
    z = jnp.zeros_like(x)
    return jnp.concatenate([jnp.where(lo, x, z), jnp.where(lo, z, x)], axis=0)


def _fox_qk_bound(qst_r, km_ref, t):
    km = km_ref[...]
    out = []
    for j in (0, 1):
        qf = qst_r[j * t:(j + 1) * t, :].astype(F32)
        qn = jnp.sqrt(jnp.sum(qf * qf, axis=1, keepdims=True))
        out.append(qn * jnp.sqrt(km[:, j * HEAD_DIM:j * HEAD_DIM + 1]) * 1.001 + 1.0)
    return out


def _fox_more(nr, t, hp, ft_ref, qkb, fq, level):
    def more(carries, kb):
        k0 = pl.multiple_of(jnp.maximum(kb, 0) * t, t)
        worst = None
        for j in (0, 1):
            f_new = jnp.min(ft_ref[pl.ds(2 * hp + j, 1), pl.ds(k0, t)], axis=1, keepdims=True)
            for r in range(nr):
                gap = (qkb[r][j] + fq[r][j] - f_new) - level(carries, r, j)
                worst = gap if worst is None else jnp.maximum(worst, gap)
        return (jnp.max(worst) > -EXP_ZERO).astype(jnp.int32)
    return more


def _fox_fwd(q, k, v, f_col, f_row, kmax, name, ride=None):
    s_len = q.shape[0]
    t, nr, nq = _att_tiling(s_len)

    def body(q_ref, k_ref, v_ref, f_ref, ft_ref, km_ref, y_ref, lse_ref):
        hp = pl.program_id(0)
        i = pl.program_id(1)
        lane = lax.broadcasted_iota(jnp.int32, (t, LANES), 1)
        lo = lane < HEAD_DIM
        causal = lax.broadcasted_iota(jnp.int32, (t, t), 0) >= lax.broadcasted_iota(jnp.int32, (t, t), 1)
        rows = [pl.ds(r * t, t) for r in range(nr)]
        qst = [_stacked_halves(q_ref[rw, :] * jnp.asarray(HEAD_DIM ** -0.5, BF16), lo) for rw in rows]
        q_all = _stack(qst)
        fq = [[jnp.sum(jnp.where(lane == 2 * hp + j, f_ref[rw, :], 0.0), axis=1, keepdims=True) for j in (0, 1)]
              for rw in rows]

        def load(kb):
            k0 = pl.multiple_of(kb * t, t)
            frow = [ft_ref[pl.ds(2 * hp + j, 1), pl.ds(k0, t)] for j in (0, 1)]
            return k_ref[pl.ds(k0, t), :], v_ref[pl.ds(k0, t), :], frow

        def sub(rs, tiles, carries, masked):
            kblk, vblk, frow = tiles
            z = _dot_nt(q_all if len(rs) == nr else qst[rs[0]], kblk)
            ps, stats = [], []
            for n, (r, j) in enumerate((r, j) for r in range(len(rs)) for j in (0, 1)):
                m, l, _ = carries[r]
                s = z[n * t:(n + 1) * t, :] + (fq[rs[r]][j] - frow[j])
                if masked:
                    s = jnp.where(causal, s, -1e30)
                mj = jnp.maximum(m[j], jnp.max(s, axis=1, keepdims=True))
                aj = jnp.exp(m[j] - mj)
                p = jnp.exp(s - mj)
                stats.append((mj, aj, aj * l[j] + jnp.sum(p, axis=1, keepdims=True)))
                ps.append(p.astype(BF16))
            pv = _dot(_stack(ps), vblk)
            out = []
            for r in range(len(rs)):
                (m0, a0, l0), (m1, a1, l1) = stats[2 * r], stats[2 * r + 1]
                acc = carries[r][2]
                acc = (acc[0] * a0 + pv[2 * r * t:(2 * r + 1) * t, :], acc[1] * a1 + pv[(2 * r + 1) * t:(2 * r + 2) * t, :])
                out.append(((m0, m1), (l0, l1), acc))
            return out, None

        neg = jnp.full((t, 1), -1e30, F32)
        zero = jnp.zeros((t, 1), F32)
        zacc = jnp.zeros((t, LANES), F32)
        init = [((neg, neg), (zero, zero), (zacc, zacc))] * nr
        qkb = [_fox_qk_bound(qs, km_ref, t) for qs in qst]
        more = _fox_more(nr, t, hp, ft_ref, qkb, fq, lambda carries, r, j: carries[r][0][j])
        out = _walk_tiles(i, nr, load, sub, lambda kb, side: None, more, init)
        for rw, (m, l, acc) in zip(rows, out):
            y_ref[rw, :] = jnp.where(lo, acc[0] / l[0], acc[1] / l[1]).astype(BF16)
            lse_ref[0, rw, :] = jnp.where(lo, m[0] + jnp.log(l[0]), m[1] + jnp.log(l[1]))

    qblk, kvfull = _pair_specs(s_len, t * nr)
    return _ride_call(
        body, name, (N_HEADS // 2, nq),
        [qblk, kvfull, kvfull,
         pl.BlockSpec((t * nr, LANES), lambda hp, i: (i, 0)),
         pl.BlockSpec((N_HEADS, s_len), lambda hp, i: (0, 0)),
         pl.BlockSpec((1, LANES), lambda hp, i: (0, hp))],
        [qblk, pl.BlockSpec((1, t * nr, LANES), lambda hp, i: (hp, i, 0))],
        [jax.ShapeDtypeStruct((s_len, ATT_W), BF16), jax.ShapeDtypeStruct((N_HEADS // 2, s_len, LANES), F32)],
        [], ("parallel", "parallel"), (q, k, v, f_col, f_row, kmax), ride)


def _fox_bwd(q, k, v, dy, y, lse, f_col, f_row, kmax, name, ride=None):
    s_len = q.shape[0]
    t, nr, nq = _att_tiling(s_len)

    def body(q_ref, k_ref, v_ref, dy_ref, y_ref, lse_ref, f_ref, ft_ref, km_ref,
             dq_ref, dk_ref, dv_ref, dft_ref, dfq_ref):
        hp = pl.program_id(0)
        i = pl.program_id(1)

        @pl.when(i == 0)
        def _():
            dk_ref[...] = jnp.zeros_like(dk_ref)
            dv_ref[...] = jnp.zeros_like(dv_ref)
            dft_ref[...] = jnp.zeros_like(dft_ref)

        lane = lax.broadcasted_iota(jnp.int32, (t, LANES), 1)
        lo = lane < HEAD_DIM
        causal = lax.broadcasted_iota(jnp.int32, (t, t), 0) >= lax.broadcasted_iota(jnp.int32, (t, t), 1)
        rows = [pl.ds(r * t, t) for r in range(nr)]
        qst, dyst, delta, lse, fq = [], [], [], [], []
        for rw in rows:
            qst.append(_stacked_halves(q_ref[rw, :] * jnp.asarray(HEAD_DIM ** -0.5, BF16), lo))
            dyb = dy_ref[rw, :]
            dyst.append(_stacked_halves(dyb, lo))
            prod = dyb.astype(F32) * y_ref[rw, :].astype(F32)
            delta.append([jnp.sum(jnp.where(lo, prod, 0.0), axis=1, keepdims=True),
                          jnp.sum(jnp.where(lo, 0.0, prod), axis=1, keepdims=True)])
            lse_b = lse_ref[0, rw, :]
            lse.append([lse_b[:, 0:1], lse_b[:, HEAD_DIM:HEAD_DIM + 1]])
            fq.append([jnp.sum(jnp.where(lane == 2 * hp + j, f_ref[rw, :], 0.0), axis=1, keepdims=True)
                       for j in (0, 1)])

        q_all, dy_all = _stack(qst), _stack(dyst)

        def load(kb):
            k0 = pl.multiple_of(kb * t, t)
            frow = [ft_ref[pl.ds(2 * hp + j, 1), pl.ds(k0, t)] for j in (0, 1)]
            return k_ref[pl.ds(k0, t), :], v_ref[pl.ds(k0, t), :], frow

        def sub(rs, tiles, carries, masked):
            kblk, vblk, frow = tiles
            qs, dys = (q_all, dy_all) if len(rs) == nr else (qst[rs[0]], dyst[rs[0]])
            z = _dot_nt(qs, kblk)
            dp = _dot_nt(dys, vblk)
            pb, dsb, rsum, col = [], [], [], [None, None]
            for n, (r, j) in enumerate((r, j) for r in range(len(rs)) for j in (0, 1)):
                sl = slice(n * t, (n + 1) * t)
                s = z[sl, :] + (fq[rs[r]][j] - frow[j])
                p = jnp.exp(s - lse[rs[r]][j])
                if masked:
                    p = jnp.where(causal, p, 0.0)
                ds = p * (dp[sl, :] - delta[rs[r]][j])
                c = jnp.sum(ds, axis=0, keepdims=True)
                col[j] = c if col[j] is None else col[j] + c
                rsum.append(carries[r][1][j] + jnp.sum(ds, axis=1, keepdims=True))
                pb.append(p.astype(BF16))
                dsb.append(ds.astype(BF16))
            p_all, ds_all = _stack(pb), _stack(dsb)
            dqs = _dot(ds_all, kblk)
            out = []
            for r in range(len(rs)):
                dq = carries[r][0]
                dq = (dq[0] + dqs[2 * r * t:(2 * r + 1) * t, :], dq[1] + dqs[(2 * r + 1) * t:(2 * r + 2) * t, :])
                out.append((dq, (rsum[2 * r], rsum[2 * r + 1])))
            return out, (_dot_tn(ds_all, qs), _dot_tn(p_all, dys), col)

        def flush(kb, side):
            k0 = pl.multiple_of(kb * t, t)
            dk_ref[pl.ds(k0, t), :] += side[0]
            dv_ref[pl.ds(k0, t), :] += side[1]
            for j in (0, 1):
                dft_ref[0, pl.ds(j, 1), pl.ds(k0, t)] -= side[2][j]

        zero = jnp.zeros((t, 1), F32)
        zacc = jnp.zeros((t, LANES), F32)
        qkb = [_fox_qk_bound(qs, km_ref, t) for qs in qst]
        more = _fox_more(nr, t, hp, ft_ref, qkb, fq, lambda carries, r, j: lse[r][j])
        out = _walk_tiles(i, nr, load, sub, flush, more, [((zacc, zacc), (zero, zero))] * nr)
        for rw, (dq, rs) in zip(rows, out):
            dq_ref[rw, :] = jnp.where(lo, dq[0], dq[1]) * (HEAD_DIM ** -0.5)
            dfq_ref[0, rw, :] = jnp.where(lo, rs[0], rs[1])

    qblk, kvfull = _pair_specs(s_len, t * nr)
    return _ride_call(
        body, name, (N_HEADS // 2, nq),
        [qblk, kvfull, kvfull, qblk, qblk,
         pl.BlockSpec((1, t * nr, LANES), lambda hp, i: (hp, i, 0)),
         pl.BlockSpec((t * nr, LANES), lambda hp, i: (i, 0)),
         pl.BlockSpec((N_HEADS, s_len), lambda hp, i: (0, 0)),
         pl.BlockSpec((1, LANES), lambda hp, i: (0, hp))],
        [qblk, kvfull, kvfull, pl.BlockSpec((1, 8, s_len), lambda hp, i: (hp, 0, 0)),
         pl.BlockSpec((1, t * nr, LANES), lambda hp, i: (hp, i, 0))],
        [jax.ShapeDtypeStruct((s_len, ATT_W), F32)] * 3
        + [jax.ShapeDtypeStruct((N_HEADS // 2, 8, s_len), F32),
           jax.ShapeDtypeStruct((N_HEADS // 2, s_len, LANES), F32)],
        [], ("arbitrary", "arbitrary"), (q, k, v, dy, y, lse, f_col, f_row, kmax), ride)


def _sb_more(carries, kb):
    worst = None
    for cr in carries:
        for cj in cr[0]:
            worst = cj if worst is None else jnp.maximum(worst, cj)
    return (jnp.max(worst) > -EXP_ZERO).astype(jnp.int32)


def _stacked_split_dot(slabs, m, parts):
    split = [_split(x, parts) for x in slabs]
    acc = None
    for p in range(parts):
        d = _dot(_stack([s[p] for s in split]), m)
        acc = d if acc is None else acc + d
    return acc


def _sb_weights(z, c, strict, upper, t):
    logs = []
    for n in range(z.shape[0] // t):
        zn = z[n * t:(n + 1) * t, :]
        sp = _softplus_neg_abs(zn)
        l1m = jnp.minimum(-zn, 0.0) - sp
        if strict is not None:
            l1m = jnp.where(strict, l1m, 0.0)
        logs.append((jnp.minimum(zn, 0.0) - sp, l1m))
    suf = _stacked_split_dot([l1m for _, l1m in logs], upper, 2)
    out = []
    for n, (logb, l1m) in enumerate(logs):
        after = c[n] + suf[n * t:(n + 1) * t, :]
        a = jnp.exp(logb + after)
        if strict is not None:
            a = jnp.where(strict, a, 0.0)
        out.append((logb, a, after[:, 0:1] + l1m[:, 0:1]))
    return out


def _sb_fwd(q, k, v, name):
    s_len = q.shape[0]
    t, nr, nq = _att_tiling(s_len)

    def body(q_ref, k_ref, v_ref, y_ref, yf_ref):
        i = pl.program_id(1)
        lane = lax.broadcasted_iota(jnp.int32, (t, LANES), 1)
        lo = lane < HEAD_DIM
        ri = lax.broadcasted_iota(jnp.int32, (t, t), 0)
        ci = lax.broadcasted_iota(jnp.int32, (t, t), 1)
        strict = ci < ri
        upper = (ri > ci).astype(BF16)
        rows = [pl.ds(r * t, t) for r in range(nr)]
        qst = [_stacked_halves(q_ref[rw, :] * jnp.asarray(HEAD_DIM ** -0.5, BF16), lo) for rw in rows]
        q_all = _stack(qst)

        def load(kb):
            k0 = pl.multiple_of(kb * t, t)
            return k_ref[pl.ds(k0, t), :], v_ref[pl.ds(k0, t), :]

        def sub(rs, tiles, carries, masked):
            kblk, vblk = tiles
            z = _dot_nt(q_all if len(rs) == nr else qst[rs[0]], kblk)
            c = [carries[r][0][j] for r in range(len(rs)) for j in (0, 1)]
            w = _sb_weights(z, c, strict if masked else None, upper, t)
            pv = _dot(_stack([a.astype(BF16) for _, a, _ in w]), vblk)
            out = []
            for r in range(len(rs)):
                acc = carries[r][1]
                acc = (acc[0] + pv[2 * r * t:(2 * r + 1) * t, :], acc[1] + pv[(2 * r + 1) * t:(2 * r + 2) * t, :])
                out.append(((w[2 * r][2], w[2 * r + 1][2]), acc))
            return out, None

        zero = jnp.zeros((t, 1), F32)
        zacc = jnp.zeros((t, LANES), F32)
        out = _walk_tiles(i, nr, load, sub, lambda kb, side: None, _sb_more, [((zero, zero), (zacc, zacc))] * nr)
        for rw, (_, acc) in zip(rows, out):
            y = jnp.where(lo, acc[0], acc[1])
            y_ref[rw, :] = y.astype(BF16)
            yf_ref[rw, :] = y

    qblk, kvfull = _pair_specs(s_len, t * nr)
    return pl.pallas_call(
        body, name=name, grid=(N_HEADS // 2, nq),
        in_specs=[qblk, kvfull, kvfull],
        out_specs=[qblk, qblk],
        out_shape=[jax.ShapeDtypeStruct((s_len, ATT_W), BF16), jax.ShapeDtypeStruct((s_len, ATT_W), F32)],
        compiler_params=_cparams("parallel", "parallel"),
    )(q, k, v)


def _sb_bwd(q, k, v, dy, yf, name):
    s_len = q.shape[0]
    t, nr, nq = _att_tiling(s_len)

    def body(q_ref, k_ref, v_ref, dy_ref, yf_ref, dq_ref, dk_ref, dv_ref):
        i = pl.program_id(1)

        @pl.when(i == 0)
        def _():
            dk_ref[...] = jnp.zeros_like(dk_ref)
            dv_ref[...] = jnp.zeros_like(dv_ref)

        lane = lax.broadcasted_iota(jnp.int32, (t, LANES), 1)
        lo = lane < HEAD_DIM
        ri = lax.broadcasted_iota(jnp.int32, (t, t), 0)
        ci = lax.broadcasted_iota(jnp.int32, (t, t), 1)
        strict = ci < ri
        upper = (ri > ci).astype(BF16)
        upper_incl = (ri >= ci).astype(BF16)
        rows = [pl.ds(r * t, t) for r in range(nr)]
        qst, dyst, delta = [], [], []
        for rw in rows:
            qst.append(_stacked_halves(q_ref[rw, :] * jnp.asarray(HEAD_DIM ** -0.5, BF16), lo))
            dyb = dy_ref[rw, :]
            dyst.append(_stacked_halves(dyb, lo))
            prod = dyb.astype(F32) * yf_ref[rw, :]
            delta.append([jnp.sum(jnp.where(lo, prod, 0.0), axis=1, keepdims=True),
                          jnp.sum(jnp.where(lo, 0.0, prod), axis=1, keepdims=True)])
        q_all, dy_all = _stack(qst), _stack(dyst)

        def load(kb):
            k0 = pl.multiple_of(kb * t, t)
            return k_ref[pl.ds(k0, t), :], v_ref[pl.ds(k0, t), :]

        def sub(rs, tiles, carries, masked):
            kblk, vblk = tiles
            qs, dys = (q_all, dy_all) if len(rs) == nr else (qst[rs[0]], dyst[rs[0]])
            slabs = [(r, j) for r in range(len(rs)) for j in (0, 1)]
            z = _dot_nt(qs, kblk)
            w = _sb_weights(z, [carries[r][0][j] for r, j in slabs], strict if masked else None, upper, t)
            da = _dot_nt(dys, vblk)
            ab = [a.astype(BF16) for _, a, _ in w]
            dl = [ab[n].astype(F32) * da[n * t:(n + 1) * t, :] for n in range(len(slabs))]
            tail = _stacked_split_dot(dl, upper_incl, 2)
            dzb, e_new = [], []
            for n, (r, j) in enumerate(slabs):
                tl = tail[n * t:(n + 1) * t, :]
                e = carries[r][1][j]
                dl1m = (delta[rs[r]][j] - e) - tl
                e_new.append(e + tl[:, 0:1])
                dz = dl[n] - jnp.exp(w[n][0]) * (dl[n] + dl1m)
                if masked:
                    dz = jnp.where(strict, dz, 0.0)
                dzb.append(dz.astype(BF16))
            a_all, dz_all = _stack(ab), _stack(dzb)
            dqs = _dot(dz_all, kblk)
            out = []
            for r in range(len(rs)):
                dq = carries[r][2]
                dq = (dq[0] + dqs[2 * r * t:(2 * r + 1) * t, :], dq[1] + dqs[(2 * r + 1) * t:(2 * r + 2) * t, :])
                out.append(((w[2 * r][2], w[2 * r + 1][2]), (e_new[2 * r], e_new[2 * r + 1]), dq))
            return out, (_dot_tn(dz_all, qs), _dot_tn(a_all, dys))

        def flush(kb, side):
            k0 = pl.multiple_of(kb * t, t)
            dk_ref[pl.ds(k0, t), :] += side[0]
            dv_ref[pl.ds(k0, t), :] += side[1]

        zero = jnp.zeros((t, 1), F32)
        zacc = jnp.zeros((t, LANES), F32)
        out = _walk_tiles(i, nr, load, sub, flush, _sb_more, [((zero, zero), (zero, zero), (zacc, zacc))] * nr)
        for rw, (_, _, dq) in zip(rows, out):
            dq_ref[rw, :] = jnp.where(lo, dq[0], dq[1]) * (HEAD_DIM ** -0.5)

    qblk, kvfull = _pair_specs(s_len, t * nr)
    return pl.pallas_call(
        body, name=name, grid=(N_HEADS // 2, nq),
        in_specs=[qblk, kvfull, kvfull, qblk, qblk],
        out_specs=[qblk, kvfull, kvfull],
        out_shape=[jax.ShapeDtypeStruct((s_len, ATT_W), F32)] * 3,
        compiler_params=_cparams("arbitrary", "arbitrary"),
    )(q, k, v, dy, yf)


def _merge_fwd(x, yf, ys, gf, gs, wbf, wbs, wo, name):
    s_len = x.shape[0]
    ts = min(512, s_len)

    def body(x_ref, yf_ref, ys_ref, gf_ref, gs_ref, wbf_ref, wbs_ref, wo_ref, o_ref):
        merged = (_sigmoid(gf_ref[...]) * _dot_nt(yf_ref[...], wbf_ref[...])
                  + _sigmoid(gs_ref[...]) * _dot_nt(ys_ref[...], wbs_ref[...]))
        o_ref[...] = x_ref[...] + _dot(merged.astype(BF16), wo_ref[...])

    tok = lambda w: pl.BlockSpec((ts, w), lambda i: (i, 0))
    full = lambda a: pl.BlockSpec(a.shape, lambda i: (0, 0))
    return pl.pallas_call(
        body, name=name, grid=(s_len // ts,),
        in_specs=[tok(D_MODEL), tok(ATT_W), tok(ATT_W), tok(D_MODEL), tok(D_MODEL), full(wbf), full(wbs), full(wo)],
        out_specs=tok(D_MODEL),
        out_shape=jax.ShapeDtypeStruct((s_len, D_MODEL), F32),
        compiler_params=_cparams("parallel"),
    )(x, yf, ys, gf, gs, wbf, wbs, wo)


def _merge_bwd(dx, yf, ys, gf, gs, wbf, wbs, wo, name):
    s_len = dx.shape[0]
    ts = min(512, s_len)

    def body(dx_ref, yf_ref, ys_ref, gf_ref, gs_ref, wbf_ref, wbs_ref, wo_ref,
             dyf_ref, dys_ref, dgf_ref, dgs_ref, dbf_ref, dbs_ref, mg_ref):
        bf = _dot_nt(yf_ref[...], wbf_ref[...])
        bs = _dot_nt(ys_ref[...], wbs_ref[...])
        sf = _sigmoid(gf_ref[...])
        ss = _sigmoid(gs_ref[...])
        mg_ref[...] = (sf * bf + ss * bs).astype(BF16)
        dm = _dot_nt(dx_ref[...].astype(BF16), wo_ref[...])
        dbf = (dm * sf).astype(BF16)
        dbs = (dm * ss).astype(BF16)
        dbf_ref[...] = dbf
        dbs_ref[...] = dbs
        dgf_ref[...] = (dm * bf * (sf * (1.0 - sf))).astype(BF16)
        dgs_ref[...] = (dm * bs * (ss * (1.0 - ss))).astype(BF16)
        dyf_ref[...] = _dot(dbf, wbf_ref[...]).astype(BF16)
        dys_ref[...] = _dot(dbs, wbs_ref[...]).astype(BF16)

    tok = lambda w: pl.BlockSpec((ts, w), lambda i: (i, 0))
    full = lambda a: pl.BlockSpec(a.shape, lambda i: (0, 0))
    b16o = lambda w: jax.ShapeDtypeStruct((s_len, w), BF16)
    return pl.pallas_call(
        body, name=name, grid=(s_len // ts,),
        in_specs=[tok(D_MODEL), tok(ATT_W), tok(ATT_W), tok(D_MODEL), tok(D_MODEL), full(wbf), full(wbs), full(wo)],
        out_specs=[tok(ATT_W), tok(ATT_W)] + [tok(D_MODEL)] * 5,
        out_shape=[b16o(ATT_W), b16o(ATT_W)] + [b16o(D_MODEL)] * 5,
        compiler_params=_cparams("parallel"),
    )(dx, yf, ys, gf, gs, wbf, wbs, wo)


def _mix_bwd(x, dx_in, gain, w_in, fqr, fkr, dfqn, dfkn, qg, kg, dfv, df_col, logf, dsq, dsk, dsv, dgf, dgs, name,
             ride=None):
    s_len = x.shape[0]
    ts = min(256, s_len)
    nt = s_len // ts
    gmat = _head_group_matrix()

    def body(x_ref, dxi_ref, gain_ref, w_ref, fqr_ref, fkr_ref, dfqn_ref, dfkn_ref, qg_ref, kg_ref, gm_ref,
             dfv_ref, df_ref, logf_ref, dsq_ref, dsk_ref, dsv_ref, dgf_ref, dgs_ref,
             dp_ref, dx_ref, dgain_ref, dqg_ref, dkg_ref, dbias_ref, carry):
        i = pl.program_id(0)

        @pl.when(i == 0)
        def _():
            carry[...] = jnp.zeros_like(carry)
            dgain_ref[...] = jnp.zeros_like(dgain_ref)
            dqg_ref[...] = jnp.zeros_like(dqg_ref)
            dkg_ref[...] = jnp.zeros_like(dkg_ref)
            dbias_ref[...] = jnp.zeros_like(dbias_ref)

        gm = gm_ref[...]

        def headnorm_bwd(raw, dout, g, dg_ref):
            ms = _dot_split(raw * raw, gm, HEAD_SUM_PARTS) * (1.0 / HEAD_DIM)
            r = lax.rsqrt(ms + EPS)
            nrm = raw * r
            dg_ref[...] += jnp.sum(dout * nrm, axis=0, keepdims=True)
            dn = dout * g
            mean_h = _dot_split(dn * nrm, gm, HEAD_SUM_PARTS) * (1.0 / HEAD_DIM)
            return r * (dn - nrm * mean_h)

        dp_ref[:, C_FQ:C_FQ + ATT_W] = headnorm_bwd(fqr_ref[...], dfqn_ref[...], qg_ref[...], dqg_ref).astype(BF16)
        dp_ref[:, C_FK:C_FK + ATT_W] = headnorm_bwd(fkr_ref[...], dfkn_ref[...], kg_ref[...], dkg_ref).astype(BF16)
        dp_ref[:, C_FV:C_FV + ATT_W] = dfv_ref[...].astype(BF16)
        dp_ref[:, C_SQ:C_SQ + ATT_W] = dsq_ref[...].astype(BF16)
        dp_ref[:, C_SK:C_SK + ATT_W] = dsk_ref[...].astype(BF16)
        dp_ref[:, C_SV:C_SV + ATT_W] = dsv_ref[...].astype(BF16)
        dp_ref[:, C_GF:C_GF + D_MODEL] = dgf_ref[...]
        dp_ref[:, C_GS:C_GS + D_MODEL] = dgs_ref[...]

        r_ = lax.broadcasted_iota(jnp.int32, (ts, ts), 0)
        c_ = lax.broadcasted_iota(jnp.int32, (ts, ts), 1)
        rev = (c_ >= r_).astype(BF16)
        dlogf = _dot_split_left(rev, df_ref[...], 3) + carry[...]
        carry[...] = dlogf[0:1, :]
        lane = lax.broadcasted_iota(jnp.int32, (ts, LANES), 1)
        dfl = jnp.where(lane < N_HEADS, dlogf * (1.0 - jnp.exp(logf_ref[...])), 0.0)
        dbias_ref[...] += jnp.sum(dfl, axis=0, keepdims=True)
        dp_ref[:, C_FL:C_FL + LANES] = dfl.astype(BF16)
        dp_ref[:, C_FL + LANES:C_SQ] = jnp.zeros((ts, C_SQ - C_FL - LANES), BF16)

        dh = _dot(dp_ref[...], w_ref[...])
        xf = x_ref[...]
        r = _rms_rinv(xf)
        xhat = xf * r
        dgain_ref[...] += jnp.sum(dh * xhat, axis=0, keepdims=True)
        dn = dh * gain_ref[...]
        dx_ref[...] = dxi_ref[...] + r * (dn - xhat * jnp.mean(dn * xhat, axis=-1, keepdims=True))

    tok = lambda w: pl.BlockSpec((ts, w), lambda i: (nt - 1 - i, 0))
    full = lambda a: pl.BlockSpec(a.shape, lambda i: (0, 0))
    row = lambda w: pl.BlockSpec((1, w), lambda i: (0, 0))
    return _ride_call(
        body, name, (nt,),
        [tok(D_MODEL), tok(D_MODEL), full(gain), full(w_in), tok(ATT_W), tok(ATT_W), tok(ATT_W), tok(ATT_W),
         full(qg), full(kg), full(gmat), tok(ATT_W), tok(LANES), tok(LANES), tok(ATT_W), tok(ATT_W), tok(ATT_W),
         tok(D_MODEL), tok(D_MODEL)],
        [tok(IN_PAD), tok(D_MODEL), row(D_MODEL), row(ATT_W), row(ATT_W), row(LANES)],
        [jax.ShapeDtypeStruct((s_len, IN_PAD), BF16), jax.ShapeDtypeStruct((s_len, D_MODEL), F32),
         jax.ShapeDtypeStruct((1, D_MODEL), F32), jax.ShapeDtypeStruct((1, ATT_W), F32),
         jax.ShapeDtypeStruct((1, ATT_W), F32), jax.ShapeDtypeStruct((1, LANES), F32)],
        [pltpu.VMEM((1, LANES), F32)], ("arbitrary",),
        (x, dx_in, gain, w_in, fqr, fkr, dfqn, dfkn, qg, kg, gmat, dfv, df_col, logf, dsq, dsk, dsv, dgf, dgs), ride)


def _ple_loss(x, p, tgt, gain, wpg, wpp, name):
    s_len = x.shape[0]
    ts = min(512, s_len)

    def body(x_ref, p_ref, t_ref, gain_ref, wpg_ref, wpp_ref, dx_ref, n_ref, ds_ref, dpp_ref, dgain_ref, loss_ref):
        i = pl.program_id(0)

        @pl.when(i == 0)
        def _():
            dgain_ref[...] = jnp.zeros_like(dgain_ref)
            loss_ref[...] = jnp.zeros_like(loss_ref)

        xf = x_ref[...]
        r = _rms_rinv(xf)
        n = xf * r
        hn = (n * gain_ref[...]).astype(BF16)
        n_ref[...] = hn
        sg = _sigmoid(_dot(hn, wpg_ref[...]))
        pp = _dot_nt(p_ref[...].astype(BF16), wpp_ref[...])
        err = (xf + sg * pp) - t_ref[...]
        sq = jnp.sum(jnp.sum(err * err, axis=1, keepdims=True), axis=0, keepdims=True)
        loss_ref[...] += (0.5 / D_MODEL) * sq
        dout = err * (1.0 / D_MODEL)
        dpp_ref[...] = (dout * sg).astype(BF16)
        ds = (dout * pp * (sg * (1.0 - sg))).astype(BF16)
        ds_ref[...] = ds
        dhn = _dot_nt(ds, wpg_ref[...])
        dgain_ref[...] += jnp.sum(dhn * n, axis=0, keepdims=True)
        dn = dhn * gain_ref[...]
        dx_ref[...] = dout + r * (dn - n * jnp.mean(dn * n, axis=-1, keepdims=True))

    tok = lambda w: pl.BlockSpec((ts, w), lambda i: (i, 0))
    full = lambda a: pl.BlockSpec(a.shape, lambda i: (0, 0))
    return pl.pallas_call(
        body, name=name, grid=(s_len // ts,),
        in_specs=[tok(D_MODEL), tok(PLE_DIM), tok(D_MODEL), full(gain), full(wpg), full(wpp)],
        out_specs=[tok(D_MODEL), tok(D_MODEL), tok(D_MODEL), tok(D_MODEL),
                   pl.BlockSpec((1, D_MODEL), lambda i: (0, 0)), pl.BlockSpec((8, LANES), lambda i: (0, 0))],
        out_shape=[jax.ShapeDtypeStruct((s_len, D_MODEL), F32), jax.ShapeDtypeStruct((s_len, D_MODEL), BF16),
                   jax.ShapeDtypeStruct((s_len, D_MODEL), BF16), jax.ShapeDtypeStruct((s_len, D_MODEL), BF16),
                   jax.ShapeDtypeStruct((1, D_MODEL), F32), jax.ShapeDtypeStruct((8, LANES), F32)],
        compiler_params=_cparams("arbitrary"),
    )(x, p, tgt, gain, wpg, wpp)


def _exchange(x, name, broadcast):
    def body(x_ref, out_ref, send_sems, recv_sems, local_sem):
        _exchange_start(x_ref, out_ref, send_sems, recv_sems, local_sem, broadcast)
        _exchange_wait(x_ref, out_ref, send_sems, recv_sems, local_sem, broadcast)

    return pl.pallas_call(
        body, name=name,
        in_specs=[EXCHANGE_SPEC],
        out_specs=EXCHANGE_SPEC,
        out_shape=_exchange_shape(x, broadcast),
        scratch_shapes=list(EXCHANGE_SEMS),
        compiler_params=pltpu.CompilerParams(has_side_effects=True),
    )(x)


EXCHANGE_SPEC = pl.BlockSpec(memory_space=pl.ANY)
EXCHANGE_SEMS = (pltpu.SemaphoreType.DMA((N_DEV - 1,)), pltpu.SemaphoreType.DMA((N_DEV - 1,)), pltpu.SemaphoreType.DMA)


def _exchange_shape(x, broadcast):
    return jax.ShapeDtypeStruct((N_DEV,) + tuple(x.shape if broadcast else x.shape[1:]), x.dtype)


def _exchange_copies(x_ref, out_ref, send_sems, recv_sems, local_sem, broadcast, with_recv=True):
    mx, my, mc = lax.axis_index("x"), lax.axis_index("y"), lax.axis_index("c")
    me = 4 * mx + 2 * my + mc

    def src(idx):
        return x_ref if broadcast else x_ref.at[idx]

    local = pltpu.make_async_copy(src(me), out_ref.at[me], local_sem)
    pairs = []
    for k in range(1, N_DEV):
        px = (1 - mx) if k & 4 else mx
        py = (1 - my) if k & 2 else my
        pc = (1 - mc) if k & 1 else mc
        peer = 4 * px + 2 * py + pc
        sems = dict(send_sem=send_sems.at[k - 1], recv_sem=recv_sems.at[k - 1], device_id=(px, py, pc), device_id_type=MESH)
        recv = pltpu.make_async_remote_copy(src_ref=src(peer), dst_ref=out_ref.at[peer], **sems) if with_recv else None
        pairs.append((pltpu.make_async_remote_copy(src_ref=src(peer), dst_ref=out_ref.at[me], **sems), recv))
    return local, pairs


def _exchange_start(*refs_and_mode):
    local, pairs = _exchange_copies(*refs_and_mode, with_recv=False)
    local.start()
    for send, _ in pairs:
        send.start()


def _exchange_wait(*refs_and_mode):
    local, pairs = _exchange_copies(*refs_and_mode)
    for _, recv in pairs:
        recv.wait_recv()
    for send, _ in pairs:
        send.wait_send()
    local.wait()


def _riding(body, grid, n_in, n_out, ride):
    if ride is None:
        return body
    broadcast = ride[1]

    def wrapped(*refs):
        ins, x_ref = refs[:n_in], refs[n_in]
        outs, out_ref = refs[n_in + 1:n_in + 1 + n_out], refs[n_in + 1 + n_out]
        scratch, sems = refs[n_in + 2 + n_out:-3], refs[-3:]
        step = pl.program_id(0)
        for d in range(1, len(grid)):
            step = step * grid[d] + pl.program_id(d)
        total = 1
        for g in grid:
            total *= g

        @pl.when(step == 0)
        def _():
            _exchange_start(x_ref, out_ref, *sems, broadcast)

        body(*ins, *outs, *scratch)

        @pl.when(step == total - 1)
        def _():
            _exchange_wait(x_ref, out_ref, *sems, broadcast)

    return wrapped


def _ride_call(body, name, grid, in_specs, out_specs, out_shape, scratch_shapes, sem, operands, ride):
    if ride is None:
        return pl.pallas_call(body, name=name, grid=grid, in_specs=in_specs, out_specs=out_specs, out_shape=out_shape,
                              scratch_shapes=scratch_shapes, compiler_params=_cparams(*sem))(*operands)
    return pl.pallas_call(
        _riding(body, grid, len(in_specs), len(out_specs), ride), name=name, grid=grid,
        in_specs=list(in_specs) + [EXCHANGE_SPEC], out_specs=list(out_specs) + [EXCHANGE_SPEC],
        out_shape=list(out_shape) + [_exchange_shape(*ride)],
        scratch_shapes=list(scratch_shapes) + list(EXCHANGE_SEMS),
        compiler_params=_cparams(*(["arbitrary"] * len(grid))),
    )(*operands, ride[0])


def _adamw_math(w, g, m, v):
    m2 = ADAM_B1 * m + (1.0 - ADAM_B1) * g
    v2 = ADAM_B2 * v + (1.0 - ADAM_B2) * (g * g)
    m_hat = m2 / (1.0 - ADAM_B1 ** ADAM_STEP)
    v_hat = v2 / (1.0 - ADAM_B2 ** ADAM_STEP)
    delta = -ADAM_LR * (m_hat / (jnp.sqrt(v_hat) + ADAM_EPS) + ADAM_WD * w)
    return delta, m2, v2


def _sum_parts(parts, name, tr):
    _, rows, cols = parts.shape

    def body(p_ref, g_ref):
        g = p_ref[0].astype(F32)
        for s in range(1, N_DEV):
            g = g + p_ref[s].astype(F32)
        g_ref[...] = g

    return pl.pallas_call(
        body, name=name, grid=(rows // tr,),
        in_specs=[pl.BlockSpec((N_DEV, tr, cols), lambda i: (0, i, 0))],
        out_specs=pl.BlockSpec((tr, cols), lambda i: (i, 0)),
        out_shape=jax.ShapeDtypeStruct((rows, cols), F32),
        compiler_params=_cparams("parallel"),
    )(parts)


ADAM_SPLIT_ELEMS = 400_000


def _adamw_shard(g, w, m, v, name):
    rows, cols = w.shape
    tr = rows // 2 if rows * cols > ADAM_SPLIT_ELEMS else rows

    def body(g_ref, w_ref, m_ref, v_ref, d_ref, m2_ref, v2_ref):
        d_ref[...], m2_ref[...], v2_ref[...] = _adamw_math(w_ref[...], g_ref[...], m_ref[...], v_ref[...])

    blk = pl.BlockSpec((tr, cols), lambda i: (i, 0))
    return pl.pallas_call(
        body, name=name, grid=(rows // tr,),
        in_specs=[blk] * 4, out_specs=[blk] * 3,
        out_shape=[jax.ShapeDtypeStruct((rows, cols), F32)] * 3,
        compiler_params=_cparams("parallel"),
    )(g, w, m, v)


def _adamw(parts, w, m, v, name, tr):
    rows, cols = w.shape

    def body(p_ref, w_ref, m_ref, v_ref, g_ref, d_ref, m2_ref, v2_ref):
        g = p_ref[0].astype(F32)
        for s in range(1, N_DEV):
            g = g + p_ref[s].astype(F32)
        g_ref[...] = g
        d_ref[...], m2_ref[...], v2_ref[...] = _adamw_math(w_ref[...], g, m_ref[...], v_ref[...])

    blk = pl.BlockSpec((tr, cols), lambda i: (i, 0))
    return pl.pallas_call(
        body, name=name, grid=(rows // tr,),
        in_specs=[pl.BlockSpec((N_DEV, tr, cols), lambda i: (0, i, 0)), blk, blk, blk],
        out_specs=[blk] * 4,
        out_shape=[jax.ShapeDtypeStruct((rows, cols), F32)] * 4,
        compiler_params=_cparams("parallel"),
    )(parts, w, m, v)


TRANSPOSED = frozenset(("ffn1_w_gate", "ffn1_w_up", "w_in", "w_branch_fox", "w_branch_sb", "ffn2_w_gate", "ffn2_w_up",
                        "w_ple_proj"))
F_PAD_ROWS = C_SQ - FL_REAL_END


def _pack(pieces, group, dtype):
    out = []
    for name in GATHER_GROUPS[group]:
        r = pieces[name].T if name in TRANSPOSED else pieces[name]
        r = r.reshape(-1, D_MODEL).astype(dtype)
        if r.shape[0] != PACK_ROWS[name]:
            r = jnp.pad(r, ((0, PACK_ROWS[name] - r.shape[0]), (0, 0)))
        out.append(r)
    return jnp.concatenate(out, axis=0)


def _real_rows(name):
    return W_IN_ROWS if name == "w_in" else PACK_ROWS[name]


def _gathered(got, name, shape):
    off = GATHER_OFF[name]
    return got[:, off:off + _real_rows(name), :].reshape(shape)


def _w_in_t_padded(got):
    t = _gathered(got, "w_in", (IN_REAL, D_MODEL))
    return jnp.concatenate([t[:FL_REAL_END], jnp.zeros((F_PAD_ROWS, D_MODEL), t.dtype), t[FL_REAL_END:]], axis=0)


def _pack_chunks(grads, group):
    out = []
    for name in SCATTER_GROUPS[group]:
        g = grads[name]
        if name == "w_in":
            g = jnp.concatenate([g[:FL_REAL_END], g[C_SQ:]], axis=0)
        c = g.reshape(N_DEV, -1, D_MODEL).astype(BF16)
        if c.shape[1] != PACK_ROWS[name]:
            c = jnp.pad(c, ((0, 0), (0, PACK_ROWS[name] - c.shape[1]), (0, 0)))
        out.append(c)
    return out[0] if len(out) == 1 else jnp.concatenate(out, axis=1)


def _shard_grad(packed, name, shape):
    off = SCATTER_OFF[name]
    rows = packed[off:off + _real_rows(name), :]
    return rows.reshape(shape[1], shape[0]).T if name in TRANSPOSED else rows.reshape(shape)


WEIGHT_NAMES = ['ffn1_norm', 'ffn1_w_gate', 'ffn1_w_up', 'ffn1_w_down', 'mix_norm', 'w_in', 'forget_bias', 'q_norm',
                'k_norm', 'w_branch_fox', 'w_branch_sb', 'w_out', 'ffn2_norm', 'ffn2_w_gate', 'ffn2_w_up',
                'ffn2_w_down', 'ple_norm', 'w_ple_gate', 'w_ple_proj']
SMALL_NAMES = ('ffn1_norm', 'mix_norm', 'ffn2_norm', 'ple_norm', 'q_norm', 'k_norm', 'forget_bias')
Q_OFF, K_OFF, B_OFF, LOSS_OFF = 0, HEAD_DIM, 2 * HEAD_DIM, 2 * HEAD_DIM + N_HEADS


def _pack_small(vals, loss=None):
    tail = [vals['q_norm'].reshape(1, -1), vals['k_norm'].reshape(1, -1), vals['forget_bias'].reshape(1, -1)]
    used = LOSS_OFF
    if loss is not None:
        tail.append(loss.reshape(1, 1))
        used += 1
    tail.append(jnp.zeros((1, D_MODEL - used), F32))
    rows = [vals[n].reshape(1, D_MODEL) for n in SMALL_NAMES[:4]] + [jnp.concatenate(tail, axis=1)]
    rows.append(jnp.zeros((SMALL_ROWS - len(rows), D_MODEL), F32))
    return jnp.concatenate(rows, axis=0)


def _unpack_small(packed, name, shape):
    if name in SMALL_NAMES[:4]:
        return packed[SMALL_NAMES.index(name)].reshape(shape)
    off, n = {'q_norm': (Q_OFF, HEAD_DIM), 'k_norm': (K_OFF, HEAD_DIM), 'forget_bias': (B_OFF, N_HEADS)}[name]
    return packed[4, off:off + n].reshape(shape)


def _step(x, p, tgt, w):
    row = lambda a: a.reshape(1, -1).astype(F32)
    g_ffn1, g_mix, g_ffn2, g_ple = (row(w[n]) for n in SMALL_NAMES[:4])
    qg = jnp.tile(row(w['q_norm']), (1, N_HEADS))
    kg = jnp.tile(row(w['k_norm']), (1, N_HEADS))
    bias = jnp.pad(row(w['forget_bias']), ((0, 0), (0, LANES - N_HEADS)))
    half = D_FF // 2
    grads = {}

    blk = lambda n: GATHER_OFF[n] // FFN_SHARD
    ffn1 = tuple(blk(n) for n in ("ffn1_w_gate", "ffn1_w_up", "ffn1_w_down"))
    ffn2 = tuple(blk(n) for n in ("ffn2_w_gate", "ffn2_w_up", "ffn2_w_down"))
    got0 = _exchange(_pack(w, 0, BF16), "gather_ffn1", True)
    x1, g1, u1, h1, got1 = _ffn_fwd(x, g_ffn1, got0, ffn1, "ffn1_fwd", ride=(_pack(w, 1, BF16), True))
    w_in = _w_in_t_padded(got1)
    wbf = _gathered(got1, "w_branch_fox", (D_MODEL, ATT_W))
    wbs = _gathered(got1, "w_branch_sb", (D_MODEL, ATT_W))
    wo = _gathered(got1, "w_out", (D_MODEL, D_MODEL))
    (hmix, fqr, fkr, fqn, fkn, fv, logf, f_col, f_row, sq, sk, sv, gf, gs, kmax) = _mix_fwd(
        x1, g_mix, w_in, bias, qg, kg, "mix_fwd")
    y_fox, lse, got2 = _fox_fwd(fqn, fkn, fv, f_col, f_row, kmax, "fox_fwd", ride=(_pack(w, 2, BF16), True))
    wpg = _gathered(got2, "w_ple_gate", (D_MODEL, D_MODEL))
    wpp = _gathered(got2, "w_ple_proj", (D_MODEL, PLE_DIM))
    y_sb, y_sb32 = _sb_fwd(sq, sk, sv, "sb_fwd")
    x2 = _merge_fwd(x1, y_fox, y_sb, gf, gs, wbf, wbs, wo, "merge_fwd")
    x3, g2, u2, h2, = _ffn_fwd(x2, g_ffn2, got2, ffn2, "ffn2_fwd")
    dx3, n_ple, ds_ple, dpp, dg_ple, loss = _ple_loss(x3, p, tgt, g_ple, wpg, wpp, "ple_loss")

    grads['w_ple_gate'] = _wgrad(n_ple, ds_ple, "dw_ple_gate", D_MODEL, D_MODEL)
    grads['w_ple_proj'] = _wgrad(dpp, p, "dw_ple_proj", D_MODEL, PLE_DIM)
    dg2, du2, act2 = _ffn_bwd_hidden(dx3, g2, u2, got2, ffn2[2], "ffn2_bwd_hidden")
    grads['ffn2_w_gate'] = _wgrad(dg2, h2, "dw_ffn2_gate", half, D_MODEL)
    grads['ffn2_w_up'] = _wgrad(du2, h2, "dw_ffn2_up", half, D_MODEL)
    grads['ffn2_w_down'] = _wgrad(act2, dx3, "dw_ffn2_down", half, D_MODEL)
    dx2, dg_ffn2 = _ffn_bwd_input(x2, dx3, g_ffn2, dg2, du2, got2, ffn2[:2], "ffn2_bwd_input")
    dyf, dys, dgf, dgs, dbf, dbs, merged = _merge_bwd(dx2, y_fox, y_sb, gf, gs, wbf, wbs, wo, "merge_bwd")
    grads['w_branch_fox'] = _wgrad(dbf, y_fox, "dw_branch_fox", D_MODEL, ATT_W)
    grads['w_branch_sb'] = _wgrad(dbs, y_sb, "dw_branch_sb", D_MODEL, ATT_W)
    grads['w_out'] = _wgrad(merged, dx2, "dw_out", D_MODEL, D_MODEL)
    dfqn, dfkn, dfv, dft, dfq, part2 = _fox_bwd(fqn, fkn, fv, dyf, y_fox, lse, f_col, f_row, kmax, "fox_bwd",
                                                ride=(_pack_chunks(grads, 2), False))
    dsq, dsk, dsv = _sb_bwd(sq, sk, sv, dys, y_sb32, "sb_bwd")
    s_len = x.shape[0]
    df_heads = dft[:, :2, :] + jnp.stack([dfq[:, :, 0], dfq[:, :, HEAD_DIM]], axis=1)
    df_col = jnp.pad(df_heads.reshape(N_HEADS, s_len).T, ((0, 0), (0, LANES - N_HEADS)))
    dproj, dx1, dg_mix, dqg, dkg, dbias = _mix_bwd(
        x1, dx2, g_mix, w_in, fqr, fkr, dfqn, dfkn, qg, kg, dfv, df_col, logf, dsq, dsk, dsv, dgf, dgs, "mix_bwd")
    grads['w_in'] = _wgrad(dproj, hmix, "dw_in", IN_PAD // 3, D_MODEL)
    dg1, du1, act1, part1 = _ffn_bwd_hidden(dx1, g1, u1, got0, ffn1[2], "ffn1_bwd_hidden",
                                            ride=(_pack_chunks(grads, 1), False))
    grads['ffn1_w_gate'] = _wgrad(dg1, h1, "dw_ffn1_gate", half, D_MODEL)
    grads['ffn1_w_up'] = _wgrad(du1, h1, "dw_ffn1_up", half, D_MODEL)
    grads['ffn1_w_down'] = _wgrad(act1, dx1, "dw_ffn1_down", half, D_MODEL)
    dx0, dg_ffn1, part0 = _ffn_bwd_input(x, dx1, g_ffn1, dg1, du1, got0, ffn1[:2], "ffn1_bwd_input",
                                         ride=(_pack_chunks(grads, 0), False))

    fold = lambda a: a.reshape(N_HEADS, HEAD_DIM).sum(axis=0).reshape(1, HEAD_DIM)
    small_g = {'ffn1_norm': dg_ffn1, 'mix_norm': dg_mix, 'ffn2_norm': dg_ffn2, 'ple_norm': dg_ple,
               'q_norm': fold(dqg), 'k_norm': fold(dkg), 'forget_bias': dbias[:, :N_HEADS]}
    return loss[0, 0], dx0, (part0, part1, part2), small_g


def kernel(x, p, ffn1_norm, ffn1_w_gate, ffn1_w_up, ffn1_w_down, mix_norm, w_in, forget_bias, q_norm, k_norm, w_branch_fox, w_branch_sb, w_out, ffn2_norm, ffn2_w_gate, ffn2_w_up, ffn2_w_down, ple_norm, w_ple_gate, w_ple_proj, loss_target, m_ffn1_norm, m_ffn1_w_gate, m_ffn1_w_up, m_ffn1_w_down, m_mix_norm, m_w_in, m_forget_bias, m_q_norm, m_k_norm, m_w_branch_fox, m_w_branch_sb, m_w_out, m_ffn2_norm, m_ffn2_w_gate, m_ffn2_w_up, m_ffn2_w_down, m_ple_norm, m_w_ple_gate, m_w_ple_proj, v_ffn1_norm, v_ffn1_w_gate, v_ffn1_w_up, v_ffn1_w_down, v_mix_norm, v_w_in, v_forget_bias, v_q_norm, v_k_norm, v_w_branch_fox, v_w_branch_sb, v_w_out, v_ffn2_norm, v_ffn2_w_gate, v_ffn2_w_up, v_ffn2_w_down, v_ple_norm, v_w_ple_gate, v_w_ple_proj):
    args = dict(locals())
    w = {n: args[n][0] for n in WEIGHT_NAMES}
    m = {n: args["m_" + n][0] for n in WEIGHT_NAMES}
    v = {n: args["v_" + n][0] for n in WEIGHT_NAMES}
    loss, dx, parts, small_g = _step(x[0], p[0, 0], loss_target[0], w)

    big = {}
    for grp, part in enumerate(parts):
        summed = _sum_parts(part, f"sum_grads_{grp}", SUM_TILE_ROWS[grp])
        for n in SCATTER_GROUPS[grp]:
            g = _shard_grad(summed, n, w[n].shape)
            big[n] = (g,) + tuple(_adamw_shard(g, w[n], m[n], v[n], "adamw_" + n))
    small_parts = _exchange(_pack_small(small_g, loss), "gather_small", True)
    sw, sm, sv = (_pack_small(t) for t in (w, m, v))
    small = _adamw(small_parts, sw, sm, sv, "adamw_small", SMALL_ROWS)

    outs = [small[0][4, LOSS_OFF], dx.reshape(x.shape)]
    for kind in range(4):
        for n in WEIGHT_NAMES:
            shape = args[n].shape
            outs.append(_unpack_small(small[kind], n, shape) if n in SMALL_NAMES else big[n][kind].reshape(shape))
    return tuple(outs)
```

```python
import jax
import jax.numpy as jnp
from jax import lax
from jax.experimental import pallas as pl
from jax.experimental.pallas import tpu as pltpu

F32 = jnp.float32
BF16 = jnp.bfloat16

D_MODEL = 1024
D_FF = 2816
N_HEADS = 8
HEAD_DIM = 64
ATT_W = N_HEADS * HEAD_DIM
PLE_DIM = 256
EPS = 1e-6
N_DEV = 8
MESH = pl.DeviceIdType.MESH

LANES = 128
V7X_SCOPED_VMEM_BYTES = 56 * 1024 * 1024

C_FQ, C_FK, C_FV, C_FL = 0, 512, 1024, 1536
C_SQ, C_SK, C_SV, C_GF, C_GS = 1792, 2304, 2816, 3328, 4352
IN_PAD = 5376
IN_REAL = 5128
FL_REAL_END = 1544

ADAM_LR = 0.001
ADAM_B1 = 0.9
ADAM_B2 = 0.999
ADAM_EPS = 1e-08
ADAM_WD = 0.01
ADAM_STEP = 10

PACK_ROWS = {"ffn1_w_gate": 352, "ffn1_w_up": 352, "ffn1_w_down": 352, "w_in": 656, "w_branch_fox": 64,
             "w_branch_sb": 64, "w_out": 128, "ffn2_w_gate": 352, "ffn2_w_up": 352, "ffn2_w_down": 352,
             "w_ple_gate": 128, "w_ple_proj": 32}
GATHER_GROUPS = (
    ("ffn1_w_gate", "ffn1_w_up", "ffn1_w_down"),
    ("w_in", "w_branch_fox", "w_branch_sb", "w_out"),
    ("ffn2_w_gate", "ffn2_w_up", "ffn2_w_down", "w_ple_gate", "w_ple_proj"),
)
SCATTER_GROUPS = (
    ("ffn1_w_gate", "ffn1_w_up", "ffn1_w_down"),
    ("w_in",),
    ("ffn2_w_gate", "ffn2_w_up", "ffn2_w_down", "w_ple_gate", "w_ple_proj", "w_branch_fox", "w_branch_sb", "w_out"),
)
SUM_TILE_ROWS = (528, 656, 368)


def _offsets(groups):
    off = {}
    for grp in groups:
        o = 0
        for n in grp:
            off[n] = o
            o += PACK_ROWS[n]
    return off


GATHER_OFF = _offsets(GATHER_GROUPS)
SCATTER_OFF = _offsets(SCATTER_GROUPS)
W_IN_ROWS = 641

SMALL_ROWS = 8


def _cparams(*sem):
    return pltpu.CompilerParams(dimension_semantics=sem, vmem_limit_bytes=V7X_SCOPED_VMEM_BYTES)


def _dot(a, b):
    return jnp.dot(a, b, preferred_element_type=F32)


def _dot_nt(a, b):
    return lax.dot_general(a, b, (((1,), (1,)), ((), ())), preferred_element_type=F32)


def _dot_tn(a, b):
    return lax.dot_general(a, b, (((0,), (0,)), ((), ())), preferred_element_type=F32)


def _split(x, parts):
    out = []
    r = x
    for _ in range(parts):
        p = r.astype(BF16)
        out.append(p)
        r = r - p.astype(F32)
    return out


def _dot_split(x, m, parts):
    acc = None
    for p in _split(x, parts):
        t = _dot(p, m)
        acc = t if acc is None else acc + t
    return acc


def _dot_split_left(m, x, parts):
    acc = None
    for p in _split(x, parts):
        t = _dot(m, p)
        acc = t if acc is None else acc + t
    return acc


def _rms_rinv(xf):
    return lax.rsqrt(jnp.mean(xf * xf, axis=-1, keepdims=True) + EPS)


def _sigmoid(x):
    return 1.0 / (1.0 + jnp.exp(-x))


def _softplus_neg_abs(z):
    return jnp.log(1.0 + jnp.exp(-jnp.abs(z)))


FFN_SHARD = D_FF // N_DEV
FFN_CHUNK = 4


def _ffn_w_spec(blk, index_map):
    return pl.BlockSpec((FFN_CHUNK, FFN_SHARD, D_MODEL), lambda *g: (index_map(*g), blk, 0))


def _ffn_w(ref):
    return ref[...].reshape(FFN_CHUNK * FFN_SHARD, D_MODEL)


def _ffn_fwd(x, gain, wbuf, blks, name, ride=None):
    s_len = x.shape[0]
    ts = min(512, s_len)
    fc = FFN_CHUNK * FFN_SHARD
    nt, nc = s_len // ts, D_FF // fc

    def body(x_ref, gain_ref, wg_ref, wu_ref, wd_ref, y_ref, g_ref, u_ref, h_ref, acc_scr):
        j = pl.program_id(1)

        @pl.when(j == 0)
        def _():
            xf = x_ref[...]
            h_ref[...] = ((xf * _rms_rinv(xf)) * gain_ref[...]).astype(BF16)
            acc_scr[...] = jnp.zeros_like(acc_scr)

        h = h_ref[...]
        g = _dot_nt(h, _ffn_w(wg_ref))
        u = _dot_nt(h, _ffn_w(wu_ref))
        g_ref[...] = g.astype(BF16)
        u_ref[...] = u.astype(BF16)
        a = (g * _sigmoid(g) * u).astype(BF16)
        acc_scr[...] += _dot(a, _ffn_w(wd_ref))

        @pl.when(j == nc - 1)
        def _():
            y_ref[...] = x_ref[...] + 0.5 * acc_scr[...]

    tok = pl.BlockSpec((ts, D_MODEL), lambda i, j: (i, 0))
    hid = pl.BlockSpec((ts, fc), lambda i, j: (i, j))
    return _ride_call(
        body, name, (nt, nc),
        [tok, pl.BlockSpec((1, D_MODEL), lambda i, j: (0, 0))] + [_ffn_w_spec(b, lambda i, j: j) for b in blks],
        [tok, hid, hid, tok],
        [jax.ShapeDtypeStruct((s_len, D_MODEL), F32), jax.ShapeDtypeStruct((s_len, D_FF), BF16),
         jax.ShapeDtypeStruct((s_len, D_FF), BF16), jax.ShapeDtypeStruct((s_len, D_MODEL), BF16)],
        [pltpu.VMEM((ts, D_MODEL), F32)], ("parallel", "arbitrary"), (x, gain, wbuf, wbuf, wbuf), ride)


def _ffn_bwd_hidden(dy, g, u, wbuf, blk, name, ride=None):
    s_len = dy.shape[0]
    ts = min(512, s_len)
    fc = FFN_CHUNK * FFN_SHARD
    nt, nc = s_len // ts, D_FF // fc

    def body(dy_ref, g_ref, u_ref, wd_ref, dg_ref, du_ref, act_ref):
        da = 0.5 * _dot_nt(dy_ref[...].astype(BF16), _ffn_w(wd_ref))
        gf = g_ref[...].astype(F32)
        uf = u_ref[...].astype(F32)
        sg = _sigmoid(gf)
        silu = gf * sg
        dg_ref[...] = (da * uf * (sg * (1.0 + gf * (1.0 - sg)))).astype(BF16)
        du_ref[...] = (da * silu).astype(BF16)
        act_ref[...] = (0.5 * silu * uf).astype(BF16)

    hid = pl.BlockSpec((ts, fc), lambda c, t: (t, c))
    return _ride_call(
        body, name, (nc, nt),
        [pl.BlockSpec((ts, D_MODEL), lambda c, t: (t, 0)), hid, hid, _ffn_w_spec(blk, lambda c, t: c)],
        [hid, hid, hid], [jax.ShapeDtypeStruct((s_len, D_FF), BF16)] * 3, [], ("parallel", "parallel"),
        (dy, g, u, wbuf), ride)


def _ffn_bwd_input(x, dy, gain, dg, du, wbuf, blks, name, ride=None):
    s_len = x.shape[0]
    ts = min(512, s_len)
    fc = FFN_CHUNK * FFN_SHARD
    nt, nc = s_len // ts, D_FF // fc

    def body(x_ref, dy_ref, gain_ref, dg_ref, du_ref, wg_ref, wu_ref, dx_ref, dgain_ref, acc):
        i = pl.program_id(0)
        j = pl.program_id(1)
        part = _dot(dg_ref[...], _ffn_w(wg_ref)) + _dot(du_ref[...], _ffn_w(wu_ref))

        @pl.when(j == 0)
        def _():
            acc[...] = part

        @pl.when(j > 0)
        def _():
            acc[...] += part

        @pl.when(j == nc - 1)
        def _():
            xf = x_ref[...]
            r = _rms_rinv(xf)
            xhat = xf * r
            dh = acc[...]
            dgp = jnp.sum(dh * xhat, axis=0, keepdims=True)

            @pl.when(i == 0)
            def _():
                dgain_ref[...] = dgp

            @pl.when(i > 0)
            def _():
                dgain_ref[...] += dgp

            dn = dh * gain_ref[...]
            dx_ref[...] = dy_ref[...] + r * (dn - xhat * jnp.mean(dn * xhat, axis=-1, keepdims=True))

    tok = pl.BlockSpec((ts, D_MODEL), lambda i, j: (i, 0))
    row = pl.BlockSpec((1, D_MODEL), lambda i, j: (0, 0))
    hid = pl.BlockSpec((ts, fc), lambda i, j: (i, j))
    return _ride_call(
        body, name, (nt, nc), [tok, tok, row, hid, hid] + [_ffn_w_spec(b, lambda i, j: j) for b in blks], [tok, row],
        [jax.ShapeDtypeStruct((s_len, D_MODEL), F32), jax.ShapeDtypeStruct((1, D_MODEL), F32)],
        [pltpu.VMEM((ts, D_MODEL), F32)], ("arbitrary", "arbitrary"), (x, dy, gain, dg, du, wbuf, wbuf), ride)


def _wgrad(a, b, name, tk, tn):
    s_len, k_dim = a.shape
    n_dim = b.shape[1]
    ts = min(1024, s_len)
    ns = s_len // ts

    def body(a_ref, b_ref, o_ref, acc):
        s = pl.program_id(2)
        p = _dot_tn(a_ref[...].astype(BF16), b_ref[...].astype(BF16))

        @pl.when(s == 0)
        def _():
            acc[...] = p

        @pl.when(s > 0)
        def _():
            acc[...] += p

        @pl.when(s == ns - 1)
        def _():
            o_ref[...] = acc[...].astype(BF16)

    return pl.pallas_call(
        body, name=name, grid=(k_dim // tk, n_dim // tn, ns),
        in_specs=[
            pl.BlockSpec((ts, tk), lambda k, n, s: (s, k)),
            pl.BlockSpec((ts, tn), lambda k, n, s: (s, n)),
        ],
        out_specs=pl.BlockSpec((tk, tn), lambda k, n, s: (k, n)),
        out_shape=jax.ShapeDtypeStruct((k_dim, n_dim), BF16),
        scratch_shapes=[pltpu.VMEM((tk, tn), F32)],
        compiler_params=_cparams("parallel", "parallel", "arbitrary"),
    )(a, b)


HEAD_SUM_PARTS = 2


def _head_group_matrix():
    r = lax.broadcasted_iota(jnp.int32, (ATT_W, ATT_W), 0) // HEAD_DIM
    c = lax.broadcasted_iota(jnp.int32, (ATT_W, ATT_W), 1) // HEAD_DIM
    return (r == c).astype(BF16)


def _mix_fwd(x, gain, w_in, bias, qg, kg, name, ride=None):
    s_len = x.shape[0]
    ts = min(256, s_len)
    nt = s_len // ts
    gmat = _head_group_matrix()

    def body(x_ref, gain_ref, w_ref, bias_ref, qg_ref, kg_ref, gm_ref,
             h_ref, fqr_ref, fkr_ref, fqn_ref, fkn_ref, fv_ref, logf_ref, f_ref, ft_ref,
             sq_ref, sk_ref, sv_ref, gf_ref, gs_ref, kmax_ref, carry):
        i = pl.program_id(0)
        xf = x_ref[...]
        h = ((xf * _rms_rinv(xf)) * gain_ref[...]).astype(BF16)
        h_ref[...] = h
        gm = gm_ref[...]

        def proj(lo, n):
            return _dot_nt(h, w_ref[lo:lo + n, :])

        def headnorm(raw, g):
            ms = _dot_split(raw * raw, gm, HEAD_SUM_PARTS) * (1.0 / HEAD_DIM)
            return ((raw * lax.rsqrt(ms + EPS)) * g).astype(BF16)

        fq = proj(C_FQ, ATT_W)
        fqr_ref[...] = fq
        fqn_ref[...] = headnorm(fq, qg_ref[...])
        fk = proj(C_FK, ATT_W)
        fkr_ref[...] = fk
        fkn = headnorm(fk, kg_ref[...])
        fkn_ref[...] = fkn
        kn2 = jnp.max(_dot_split(jnp.square(fkn.astype(F32)), gm, HEAD_SUM_PARTS), axis=0, keepdims=True)

        @pl.when(i == 0)
        def _():
            kmax_ref[...] = kn2

        @pl.when(i > 0)
        def _():
            kmax_ref[...] = jnp.maximum(kmax_ref[...], kn2)
        fv_ref[...] = proj(C_FV, ATT_W).astype(BF16)
        sq_ref[...] = proj(C_SQ, ATT_W).astype(BF16)
        sk_ref[...] = proj(C_SK, ATT_W).astype(BF16)
        sv_ref[...] = proj(C_SV, ATT_W).astype(BF16)
        gf_ref[...] = proj(C_GF, D_MODEL)
        gs_ref[...] = proj(C_GS, D_MODEL)

        fl = proj(C_FL, LANES) + bias_ref[...]
        lane = lax.broadcasted_iota(jnp.int32, fl.shape, 1)
        logf = jnp.where(lane < N_HEADS, jnp.minimum(fl, 0.0) - _softplus_neg_abs(fl), 0.0)
        logf_ref[...] = logf

        @pl.when(i == 0)
        def _():
            carry[...] = jnp.zeros_like(carry)

        r = lax.broadcasted_iota(jnp.int32, (ts, ts), 0)
        c = lax.broadcasted_iota(jnp.int32, (ts, ts), 1)
        tri = (r >= c).astype(BF16)
        f_tile = _dot_split_left(tri, logf, 3) + carry[...]
        f_ref[...] = f_tile
        ft_ref[...] = f_tile.T[:N_HEADS, :]
        carry[...] = f_tile[ts - 1:ts, :]

    tok = lambda w: pl.BlockSpec((ts, w), lambda i: (i, 0))
    full = lambda a: pl.BlockSpec(a.shape, lambda i: (0, 0))
    f32o = lambda w: jax.ShapeDtypeStruct((s_len, w), F32)
    b16o = lambda w: jax.ShapeDtypeStruct((s_len, w), BF16)
    return _ride_call(
        body, name, (nt,),
        [tok(D_MODEL), full(gain), full(w_in), full(bias), full(qg), full(kg), full(gmat)],
        [
            tok(D_MODEL), tok(ATT_W), tok(ATT_W), tok(ATT_W), tok(ATT_W), tok(ATT_W), tok(LANES), tok(LANES),
            pl.BlockSpec((N_HEADS, ts), lambda i: (0, i)),
            tok(ATT_W), tok(ATT_W), tok(ATT_W), tok(D_MODEL), tok(D_MODEL),
            pl.BlockSpec((1, ATT_W), lambda i: (0, 0)),
        ],
        [
            b16o(D_MODEL), f32o(ATT_W), f32o(ATT_W), b16o(ATT_W), b16o(ATT_W), b16o(ATT_W), f32o(LANES), f32o(LANES),
            jax.ShapeDtypeStruct((N_HEADS, s_len), F32),
            b16o(ATT_W), b16o(ATT_W), b16o(ATT_W), f32o(D_MODEL), f32o(D_MODEL),
            jax.ShapeDtypeStruct((1, ATT_W), F32),
        ],
        [pltpu.VMEM((1, LANES), F32)], ("arbitrary",), (x, gain, w_in, bias, qg, kg, gmat), ride)


ATT_T = 256
ATT_ROWS = 2
EXP_ZERO = 88.0


def _att_tiling(s_len):
    t = min(ATT_T, s_len)
    nr = min(ATT_ROWS, s_len // t)
    return t, nr, s_len // (t * nr)


def _pair_specs(s_len, tq):
    qblk = pl.BlockSpec((tq, LANES), lambda hp, i: (i, hp))
    kvfull = pl.BlockSpec((s_len, LANES), lambda hp, i: (0, hp))
    return qblk, kvfull


def _walk_tiles(i, nr, load, sub, flush, init, more=None, trips=None):
    base = i * nr
    carries = list(init)
    for r in range(nr):
        for kk in range(r, -1, -1):
            (carries[r],), side = sub([r], load(base + kk), [carries[r]], kk == r)
            flush(base + kk, side)

    def visit(n, cs):
        kb = base - 1 - n
        cs, side = sub(list(range(nr)), load(kb), list(cs), False)
        flush(kb, side)
        return tuple(cs)

    if trips is not None:
        return lax.fori_loop(0, trips(carries, base), visit, tuple(carries))

    def cond(state):
        return jnp.logical_and(state[0] < base, state[1] > 0)

    def step(state):
        n, _, cs = state
        cs = visit(n, cs)
        return n + 1, more(cs, base - 2 - n), cs

    return lax.while_loop(cond, step, (jnp.int32(0), more(carries, base - 1), tuple(carries)))[2]


def _stack(parts):
    return parts[0] if len(parts) == 1 else jnp.concatenate(parts, axis=0)


def _stacked_halves(x, lo):
    z = jnp.zeros_like(x)
    return jnp.concatenate([jnp.where(lo, x, z), jnp.where(lo, z, x)], axis=0)


def _fox_qk_bound(qst_r, km_ref, t):
    km = km_ref[...]
    out = []
    for j in (0, 1):
        qf = qst_r[j * t:(j + 1) * t, :].astype(F32)
        qn = jnp.sqrt(jnp.sum(qf * qf, axis=1, keepdims=True))
        out.append(qn * jnp.sqrt(km[:, j * HEAD_DIM:j * HEAD_DIM + 1]) * 1.001 + 1.0)
    return out


def _fox_trips(nr, hp, flast_ref, qkb, fq, level):
    def trips(carries, base):
        gap = []
        for j in (0, 1):
            worst = None
            for r in range(nr):
                g = qkb[r][j] + fq[r][j] - level(carries, r, j)
                worst = g if worst is None else jnp.maximum(worst, g)
            gap.append(jnp.max(worst))

        def needed(n):
            kb = jnp.maximum(base - 1 - n, 0)
            return jnp.logical_or(gap[0] - flast_ref[2 * hp, kb] > -EXP_ZERO,
                                  gap[1] - flast_ref[2 * hp + 1, kb] > -EXP_ZERO)

        return lax.while_loop(lambda n: jnp.logical_and(n < base, needed(n)), lambda n: n + 1, jnp.int32(0))
    return trips


def _fox_fwd(q, k, v, f_col, f_row, kmax, name, ride=None):
    s_len = q.shape[0]
    t, nr, nq = _att_tiling(s_len)

    def body(q_ref, k_ref, v_ref, f_ref, ft_ref, km_ref, fl_ref, y_ref, lse_ref):
        hp = pl.program_id(0)
        i = pl.program_id(1)
        lane = lax.broadcasted_iota(jnp.int32, (t, LANES), 1)
        lo = lane < HEAD_DIM
        causal = lax.broadcasted_iota(jnp.int32, (t, t), 0) >= lax.broadcasted_iota(jnp.int32, (t, t), 1)
        rows = [pl.ds(r * t, t) for r in range(nr)]
        qst = [_stacked_halves(q_ref[rw, :] * jnp.asarray(HEAD_DIM ** -0.5, BF16), lo) for rw in rows]
        q_all = _stack(qst)
        fq = [[jnp.sum(jnp.where(lane == 2 * hp + j, f_ref[rw, :], 0.0), axis=1, keepdims=True) for j in (0, 1)]
              for rw in rows]

        def load(kb):
            k0 = pl.multiple_of(kb * t, t)
            frow = [ft_ref[pl.ds(2 * hp + j, 1), pl.ds(k0, t)] for j in (0, 1)]
            return k_ref[pl.ds(k0, t), :], v_ref[pl.ds(k0, t), :], frow

        def sub(rs, tiles, carries, masked):
            kblk, vblk, frow = tiles
            z = _dot_nt(q_all if len(rs) == nr else qst[rs[0]], kblk)
            ps, stats = [], []
            for n, (r, j) in enumerate((r, j) for r in range(len(rs)) for j in (0, 1)):
                m, l, _ = carries[r]
                s = z[n * t:(n + 1) * t, :] + (fq[rs[r]][j] - frow[j])
                if masked:
                    s = jnp.where(causal, s, -1e30)
                mj = jnp.maximum(m[j], jnp.max(s, axis=1, keepdims=True))
                aj = jnp.exp(m[j] - mj)
                p = jnp.exp(s - mj)
                stats.append((mj, aj, aj * l[j] + jnp.sum(p, axis=1, keepdims=True)))
                ps.append(p.astype(BF16))
            pv = _dot(_stack(ps), vblk)
            out = []
            for r in range(len(rs)):
                (m0, a0, l0), (m1, a1, l1) = stats[2 * r], stats[2 * r + 1]
                acc = carries[r][2]
                acc = (acc[0] * a0 + pv[2 * r * t:(2 * r + 1) * t, :], acc[1] * a1 + pv[(2 * r + 1) * t:(2 * r + 2) * t, :])
                out.append(((m0, m1), (l0, l1), acc))
            return out, None

        neg = jnp.full((t, 1), -1e30, F32)
        zero = jnp.zeros((t, 1), F32)
        zacc = jnp.zeros((t, LANES), F32)
        init = [((neg, neg), (zero, zero), (zacc, zacc))] * nr
        qkb = [_fox_qk_bound(qs, km_ref, t) for qs in qst]
        trips = _fox_trips(nr, hp, fl_ref, qkb, fq, lambda carries, r, j: carries[r][0][j])
        out = _walk_tiles(i, nr, load, sub, lambda kb, side: None, init, trips=trips)
        for rw, (m, l, acc) in zip(rows, out):
            y_ref[rw, :] = jnp.where(lo, acc[0] / l[0], acc[1] / l[1]).astype(BF16)
            lse_ref[0, rw, :] = jnp.where(lo, m[0] + jnp.log(l[0]), m[1] + jnp.log(l[1]))

    qblk, kvfull = _pair_specs(s_len, t * nr)
    return _ride_call(
        body, name, (N_HEADS // 2, nq),
        [qblk, kvfull, kvfull,
         pl.BlockSpec((t * nr, LANES), lambda hp, i: (i, 0)),
         pl.BlockSpec((N_HEADS, s_len), lambda hp, i: (0, 0)),
         pl.BlockSpec((1, LANES), lambda hp, i: (0, hp)),
         pl.BlockSpec(memory_space=pltpu.SMEM)],
        [qblk, pl.BlockSpec((1, t * nr, LANES), lambda hp, i: (hp, i, 0))],
        [jax.ShapeDtypeStruct((s_len, ATT_W), BF16), jax.ShapeDtypeStruct((N_HEADS // 2, s_len, LANES), F32)],
        [], ("parallel", "parallel"), (q, k, v, f_col, f_row, kmax, f_row[:, t - 1::t]), ride)


def _fox_bwd(q, k, v, dy, y, lse, f_col, f_row, kmax, name, ride=None):
    s_len = q.shape[0]
    t, nr, nq = _att_tiling(s_len)

    def body(q_ref, k_ref, v_ref, dy_ref, y_ref, lse_ref, f_ref, ft_ref, km_ref, fl_ref,
             dq_ref, dk_ref, dv_ref, dft_ref, dfq_ref):
        hp = pl.program_id(0)
        i = pl.program_id(1)

        @pl.when(i == 0)
        def _():
            dk_ref[...] = jnp.zeros_like(dk_ref)
            dv_ref[...] = jnp.zeros_like(dv_ref)
            dft_ref[...] = jnp.zeros_like(dft_ref)

        lane = lax.broadcasted_iota(jnp.int32, (t, LANES), 1)
        lo = lane < HEAD_DIM
        causal = lax.broadcasted_iota(jnp.int32, (t, t), 0) >= lax.broadcasted_iota(jnp.int32, (t, t), 1)
        rows = [pl.ds(r * t, t) for r in range(nr)]
        qst, dyst, delta, lse, fq = [], [], [], [], []
        for rw in rows:
            qst.append(_stacked_halves(q_ref[rw, :] * jnp.asarray(HEAD_DIM ** -0.5, BF16), lo))
            dyb = dy_ref[rw, :]
            dyst.append(_stacked_halves(dyb, lo))
            prod = dyb.astype(F32) * y_ref[rw, :].astype(F32)
            delta.append([jnp.sum(jnp.where(lo, prod, 0.0), axis=1, keepdims=True),
                          jnp.sum(jnp.where(lo, 0.0, prod), axis=1, keepdims=True)])
            lse_b = lse_ref[0, rw, :]
            lse.append([lse_b[:, 0:1], lse_b[:, HEAD_DIM:HEAD_DIM + 1]])
            fq.append([jnp.sum(jnp.where(lane == 2 * hp + j, f_ref[rw, :], 0.0), axis=1, keepdims=True)
                       for j in (0, 1)])

        q_all, dy_all = _stack(qst), _stack(dyst)

        def load(kb):
            k0 = pl.multiple_of(kb * t, t)
            frow = [ft_ref[pl.ds(2 * hp + j, 1), pl.ds(k0, t)] for j in (0, 1)]
            return k_ref[pl.ds(k0, t), :], v_ref[pl.ds(k0, t), :], frow

        def sub(rs, tiles, carries, masked):
            kblk, vblk, frow = tiles
            qs, dys = (q_all, dy_all) if len(rs) == nr else (qst[rs[0]], dyst[rs[0]])
            z = _dot_nt(qs, kblk)
            dp = _dot_nt(dys, vblk)
            pb, dsb, rsum, col = [], [], [], [None, None]
            for n, (r, j) in enumerate((r, j) for r in range(len(rs)) for j in (0, 1)):
                sl = slice(n * t, (n + 1) * t)
                s = z[sl, :] + (fq[rs[r]][j] - frow[j])
                p = jnp.exp(s - lse[rs[r]][j])
                if masked:
                    p = jnp.where(causal, p, 0.0)
                ds = p * (dp[sl, :] - delta[rs[r]][j])
                c = jnp.sum(ds, axis=0, keepdims=True)
                col[j] = c if col[j] is None else col[j] + c
                rsum.append(carries[r][1][j] + jnp.sum(ds, axis=1, keepdims=True))
                pb.append(p.astype(BF16))
                dsb.append(ds.astype(BF16))
            p_all, ds_all = _stack(pb), _stack(dsb)
            dqs = _dot(ds_all, kblk)
            out = []
            for r in range(len(rs)):
                dq = carries[r][0]
                dq = (dq[0] + dqs[2 * r * t:(2 * r + 1) * t, :], dq[1] + dqs[(2 * r + 1) * t:(2 * r + 2) * t, :])
                out.append((dq, (rsum[2 * r], rsum[2 * r + 1])))
            return out, (_dot_tn(ds_all, qs), _dot_tn(p_all, dys), col)

        def flush(kb, side):
            k0 = pl.multiple_of(kb * t, t)
            dk_ref[pl.ds(k0, t), :] += side[0]
            dv_ref[pl.ds(k0, t), :] += side[1]
            for j in (0, 1):
                dft_ref[0, pl.ds(j, 1), pl.ds(k0, t)] -= side[2][j]

        zero = jnp.zeros((t, 1), F32)
        zacc = jnp.zeros((t, LANES), F32)
        qkb = [_fox_qk_bound(qs, km_ref, t) for qs in qst]
        trips = _fox_trips(nr, hp, fl_ref, qkb, fq, lambda carries, r, j: lse[r][j])
        out = _walk_tiles(i, nr, load, sub, flush, [((zacc, zacc), (zero, zero))] * nr, trips=trips)
        for rw, (dq, rs) in zip(rows, out):
            dq_ref[rw, :] = jnp.where(lo, dq[0], dq[1]) * (HEAD_DIM ** -0.5)
            dfq_ref[0, rw, :] = jnp.where(lo, rs[0], rs[1])

    qblk, kvfull = _pair_specs(s_len, t * nr)
    return _ride_call(
        body, name, (N_HEADS // 2, nq),
        [qblk, kvfull, kvfull, qblk, qblk,
         pl.BlockSpec((1, t * nr, LANES), lambda hp, i: (hp, i, 0)),
         pl.BlockSpec((t * nr, LANES), lambda hp, i: (i, 0)),
         pl.BlockSpec((N_HEADS, s_len), lambda hp, i: (0, 0)),
         pl.BlockSpec((1, LANES), lambda hp, i: (0, hp)),
         pl.BlockSpec(memory_space=pltpu.SMEM)],
        [qblk, kvfull, kvfull, pl.BlockSpec((1, 8, s_len), lambda hp, i: (hp, 0, 0)),
         pl.BlockSpec((1, t * nr, LANES), lambda hp, i: (hp, i, 0))],
        [jax.ShapeDtypeStruct((s_len, ATT_W), F32)] * 3
        + [jax.ShapeDtypeStruct((N_HEADS // 2, 8, s_len), F32),
           jax.ShapeDtypeStruct((N_HEADS // 2, s_len, LANES), F32)],
        [], ("arbitrary", "arbitrary"), (q, k, v, dy, y, lse, f_col, f_row, kmax, f_row[:, t - 1::t]), ride)


def _sb_more(carries, kb):
    worst = None
    for cr in carries:
        for cj in cr[0]:
            worst = cj if worst is None else jnp.maximum(worst, cj)
    return (jnp.max(worst) > -EXP_ZERO).astype(jnp.int32)


def _stacked_split_dot(slabs, m, parts):
    split = [_split(x, parts) for x in slabs]
    acc = None
    for p in range(parts):
        d = _dot(_stack([s[p] for s in split]), m)
        acc = d if acc is None else acc + d
    return acc


def _sb_weights(z, c, strict, upper, t):
    logs = []
    for n in range(z.shape[0] // t):
        zn = z[n * t:(n + 1) * t, :]
        sp = _softplus_neg_abs(zn)
        l1m = jnp.minimum(-zn, 0.0) - sp
        if strict is not None:
            l1m = jnp.where(strict, l1m, 0.0)
        logs.append((jnp.minimum(zn, 0.0) - sp, l1m))
    suf = _stacked_split_dot([l1m for _, l1m in logs], upper, 2)
    out = []
    for n, (logb, l1m) in enumerate(logs):
        after = c[n] + suf[n * t:(n + 1) * t, :]
        a = jnp.exp(logb + after)
        if strict is not None:
            a = jnp.where(strict, a, 0.0)
        out.append((logb, a, after[:, 0:1] + l1m[:, 0:1]))
    return out


def _sb_fwd(q, k, v, name):
    s_len = q.shape[0]
    t, nr, nq = _att_tiling(s_len)

    def body(q_ref, k_ref, v_ref, y_ref, yf_ref):
        i = pl.program_id(1)
        lane = lax.broadcasted_iota(jnp.int32, (t, LANES), 1)
        lo = lane < HEAD_DIM
        ri = lax.broadcasted_iota(jnp.int32, (t, t), 0)
        ci = lax.broadcasted_iota(jnp.int32, (t, t), 1)
        strict = ci < ri
        upper = (ri > ci).astype(BF16)
        rows = [pl.ds(r * t, t) for r in range(nr)]
        qst = [_stacked_halves(q_ref[rw, :] * jnp.asarray(HEAD_DIM ** -0.5, BF16), lo) for rw in rows]
        q_all = _stack(qst)

        def load(kb):
            k0 = pl.multiple_of(kb * t, t)
            return k_ref[pl.ds(k0, t), :], v_ref[pl.ds(k0, t), :]

        def sub(rs, tiles, carries, masked):
            kblk, vblk = tiles
            z = _dot_nt(q_all if len(rs) == nr else qst[rs[0]], kblk)
            c = [carries[r][0][j] for r in range(len(rs)) for j in (0, 1)]
            w = _sb_weights(z, c, strict if masked else None, upper, t)
            pv = _dot(_stack([a.astype(BF16) for _, a, _ in w]), vblk)
            out = []
            for r in range(len(rs)):
                acc = carries[r][1]
                acc = (acc[0] + pv[2 * r * t:(2 * r + 1) * t, :], acc[1] + pv[(2 * r + 1) * t:(2 * r + 2) * t, :])
                out.append(((w[2 * r][2], w[2 * r + 1][2]), acc))
            return out, None

        zero = jnp.zeros((t, 1), F32)
        zacc = jnp.zeros((t, LANES), F32)
        out = _walk_tiles(i, nr, load, sub, lambda kb, side: None, [((zero, zero), (zacc, zacc))] * nr, more=_sb_more)
        for rw, (_, acc) in zip(rows, out):
            y = jnp.where(lo, acc[0], acc[1])
            y_ref[rw, :] = y.astype(BF16)
            yf_ref[rw, :] = y

    qblk, kvfull = _pair_specs(s_len, t * nr)
    return pl.pallas_call(
        body, name=name, grid=(N_HEADS // 2, nq),
        in_specs=[qblk, kvfull, kvfull],
        out_specs=[qblk, qblk],
        out_shape=[jax.ShapeDtypeStruct((s_len, ATT_W), BF16), jax.ShapeDtypeStruct((s_len, ATT_W), F32)],
        compiler_params=_cparams("parallel", "parallel"),
    )(q, k, v)


def _sb_bwd(q, k, v, dy, yf, name):
    s_len = q.shape[0]
    t, nr, nq = _att_tiling(s_len)

    def body(q_ref, k_ref, v_ref, dy_ref, yf_ref, dq_ref, dk_ref, dv_ref):
        i = pl.program_id(1)

        @pl.when(i == 0)
        def _():
            dk_ref[...] = jnp.zeros_like(dk_ref)
            dv_ref[...] = jnp.zeros_like(dv_ref)

        lane = lax.broadcasted_iota(jnp.int32, (t, LANES), 1)
        lo = lane < HEAD_DIM
        ri = lax.broadcasted_iota(jnp.int32, (t, t), 0)
        ci = lax.broadcasted_iota(jnp.int32, (t, t), 1)
        strict = ci < ri
        upper = (ri > ci).astype(BF16)
        upper_incl = (ri >= ci).astype(BF16)
        rows = [pl.ds(r * t, t) for r in range(nr)]
        qst, dyst, delta = [], [], []
        for rw in rows:
            qst.append(_stacked_halves(q_ref[rw, :] * jnp.asarray(HEAD_DIM ** -0.5, BF16), lo))
            dyb = dy_ref[rw, :]
            dyst.append(_stacked_halves(dyb, lo))
            prod = dyb.astype(F32) * yf_ref[rw, :]
            delta.append([jnp.sum(jnp.where(lo, prod, 0.0), axis=1, keepdims=True),
                          jnp.sum(jnp.where(lo, 0.0, prod), axis=1, keepdims=True)])
        q_all, dy_all = _stack(qst), _stack(dyst)

        def load(kb):
            k0 = pl.multiple_of(kb * t, t)
            return k_ref[pl.ds(k0, t), :], v_ref[pl.ds(k0, t), :]

        def sub(rs, tiles, carries, masked):
            kblk, vblk = tiles
            qs, dys = (q_all, dy_all) if len(rs) == nr else (qst[rs[0]], dyst[rs[0]])
            slabs = [(r, j) for r in range(len(rs)) for j in (0, 1)]
            z = _dot_nt(qs, kblk)
            w = _sb_weights(z, [carries[r][0][j] for r, j in slabs], strict if masked else None, upper, t)
            da = _dot_nt(dys, vblk)
            ab = [a.astype(BF16) for _, a, _ in w]
            dl = [ab[n].astype(F32) * da[n * t:(n + 1) * t, :] for n in range(len(slabs))]
            tail = _stacked_split_dot(dl, upper_incl, 2)
            dzb, e_new = [], []
            for n, (r, j) in enumerate(slabs):
                tl = tail[n * t:(n + 1) * t, :]
                e = carries[r][1][j]
                dl1m = (delta[rs[r]][j] - e) - tl
                e_new.append(e + tl[:, 0:1])
                dz = dl[n] - jnp.exp(w[n][0]) * (dl[n] + dl1m)
                if masked:
                    dz = jnp.where(strict, dz, 0.0)
                dzb.append(dz.astype(BF16))
            a_all, dz_all = _stack(ab), _stack(dzb)
            dqs = _dot(dz_all, kblk)
            out = []
            for r in range(len(rs)):
                dq = carries[r][2]
                dq = (dq[0] + dqs[2 * r * t:(2 * r + 1) * t, :], dq[1] + dqs[(2 * r + 1) * t:(2 * r + 2) * t, :])
                out.append(((w[2 * r][2], w[2 * r + 1][2]), (e_new[2 * r], e_new[2 * r + 1]), dq))
            return out, (_dot_tn(dz_all, qs), _dot_tn(a_all, dys))

        def flush(kb, side):
            k0 = pl.multiple_of(kb * t, t)
            dk_ref[pl.ds(k0, t), :] += side[0]
            dv_ref[pl.ds(k0, t), :] += side[1]

        zero = jnp.zeros((t, 1), F32)
        zacc = jnp.zeros((t, LANES), F32)
        out = _walk_tiles(i, nr, load, sub, flush, [((zero, zero), (zero, zero), (zacc, zacc))] * nr, more=_sb_more)
        for rw, (_, _, dq) in zip(rows, out):
            dq_ref[rw, :] = jnp.where(lo, dq[0], dq[1]) * (HEAD_DIM ** -0.5)

    qblk, kvfull = _pair_specs(s_len, t * nr)
    return pl.pallas_call(
        body, name=name, grid=(N_HEADS // 2, nq),
        in_specs=[qblk, kvfull, kvfull, qblk, qblk],
        out_specs=[qblk, kvfull, kvfull],
        out_shape=[jax.ShapeDtypeStruct((s_len, ATT_W), F32)] * 3,
        compiler_params=_cparams("arbitrary", "arbitrary"),
    )(q, k, v, dy, yf)


def _merge_fwd(x, yf, ys, gf, gs, wbf, wbs, wo, name):
    s_len = x.shape[0]
    ts = min(512, s_len)

    def body(x_ref, yf_ref, ys_ref, gf_ref, gs_ref, wbf_ref, wbs_ref, wo_ref, o_ref):
        merged = (_sigmoid(gf_ref[...]) * _dot_nt(yf_ref[...], wbf_ref[...])
                  + _sigmoid(gs_ref[...]) * _dot_nt(ys_ref[...], wbs_ref[...]))
        o_ref[...] = x_ref[...] + _dot(merged.astype(BF16), wo_ref[...])

    tok = lambda w: pl.BlockSpec((ts, w), lambda i: (i, 0))
    full = lambda a: pl.BlockSpec(a.shape, lambda i: (0, 0))
    return pl.pallas_call(
        body, name=name, grid=(s_len // ts,),
        in_specs=[tok(D_MODEL), tok(ATT_W), tok(ATT_W), tok(D_MODEL), tok(D_MODEL), full(wbf), full(wbs), full(wo)],
        out_specs=tok(D_MODEL),
        out_shape=jax.ShapeDtypeStruct((s_len, D_MODEL), F32),
        compiler_params=_cparams("parallel"),
    )(x, yf, ys, gf, gs, wbf, wbs, wo)


def _merge_bwd(dx, yf, ys, gf, gs, wbf, wbs, wo, name):
    s_len = dx.shape[0]
    ts = min(512, s_len)

    def body(dx_ref, yf_ref, ys_ref, gf_ref, gs_ref, wbf_ref, wbs_ref, wo_ref,
             dyf_ref, dys_ref, dgf_ref, dgs_ref, dbf_ref, dbs_ref, mg_ref):
        bf = _dot_nt(yf_ref[...], wbf_ref[...])
        bs = _dot_nt(ys_ref[...], wbs_ref[...])
        sf = _sigmoid(gf_ref[...])
        ss = _sigmoid(gs_ref[...])
        mg_ref[...] = (sf * bf + ss * bs).astype(BF16)
        dm = _dot_nt(dx_ref[...].astype(BF16), wo_ref[...])
        dbf = (dm * sf).astype(BF16)
        dbs = (dm * ss).astype(BF16)
        dbf_ref[...] = dbf
        dbs_ref[...] = dbs
        dgf_ref[...] = (dm * bf * (sf * (1.0 - sf))).astype(BF16)
        dgs_ref[...] = (dm * bs * (ss * (1.0 - ss))).astype(BF16)
        dyf_ref[...] = _dot(dbf, wbf_ref[...]).astype(BF16)
        dys_ref[...] = _dot(dbs, wbs_ref[...]).astype(BF16)

    tok = lambda w: pl.BlockSpec((ts, w), lambda i: (i, 0))
    full = lambda a: pl.BlockSpec(a.shape, lambda i: (0, 0))
    b16o = lambda w: jax.ShapeDtypeStruct((s_len, w), BF16)
    return pl.pallas_call(
        body, name=name, grid=(s_len // ts,),
        in_specs=[tok(D_MODEL), tok(ATT_W), tok(ATT_W), tok(D_MODEL), tok(D_MODEL), full(wbf), full(wbs), full(wo)],
        out_specs=[tok(ATT_W), tok(ATT_W)] + [tok(D_MODEL)] * 5,
        out_shape=[b16o(ATT_W), b16o(ATT_W)] + [b16o(D_MODEL)] * 5,
        compiler_params=_cparams("parallel"),
    )(dx, yf, ys, gf, gs, wbf, wbs, wo)


def _mix_bwd(x, dx_in, gain, w_in, fqr, fkr, dfqn, dfkn, qg, kg, dfv, df_col, logf, dsq, dsk, dsv, dgf, dgs, name,
             ride=None):
    s_len = x.shape[0]
    ts = min(256, s_len)
    nt = s_len // ts
    gmat = _head_group_matrix()

    def body(x_ref, dxi_ref, gain_ref, w_ref, fqr_ref, fkr_ref, dfqn_ref, dfkn_ref, qg_ref, kg_ref, gm_ref,
             dfv_ref, df_ref, logf_ref, dsq_ref, dsk_ref, dsv_ref, dgf_ref, dgs_ref,
             dp_ref, dx_ref, dgain_ref, dqg_ref, dkg_ref, dbias_ref, carry):
        i = pl.program_id(0)

        @pl.when(i == 0)
        def _():
            carry[...] = jnp.zeros_like(carry)
            dgain_ref[...] = jnp.zeros_like(dgain_ref)
            dqg_ref[...] = jnp.zeros_like(dqg_ref)
            dkg_ref[...] = jnp.zeros_like(dkg_ref)
            dbias_ref[...] = jnp.zeros_like(dbias_ref)

        gm = gm_ref[...]

        def headnorm_bwd(raw, dout, g, dg_ref):
            ms = _dot_split(raw * raw, gm, HEAD_SUM_PARTS) * (1.0 / HEAD_DIM)
            r = lax.rsqrt(ms + EPS)
            nrm = raw * r
            dg_ref[...] += jnp.sum(dout * nrm, axis=0, keepdims=True)
            dn = dout * g
            mean_h = _dot_split(dn * nrm, gm, HEAD_SUM_PARTS) * (1.0 / HEAD_DIM)
            return r * (dn - nrm * mean_h)

        dp_ref[:, C_FQ:C_FQ + ATT_W] = headnorm_bwd(fqr_ref[...], dfqn_ref[...], qg_ref[...], dqg_ref).astype(BF16)
        dp_ref[:, C_FK:C_FK + ATT_W] = headnorm_bwd(fkr_ref[...], dfkn_ref[...], kg_ref[...], dkg_ref).astype(BF16)
        dp_ref[:, C_FV:C_FV + ATT_W] = dfv_ref[...].astype(BF16)
        dp_ref[:, C_SQ:C_SQ + ATT_W] = dsq_ref[...].astype(BF16)
        dp_ref[:, C_SK:C_SK + ATT_W] = dsk_ref[...].astype(BF16)
        dp_ref[:, C_SV:C_SV + ATT_W] = dsv_ref[...].astype(BF16)
        dp_ref[:, C_GF:C_GF + D_MODEL] = dgf_ref[...]
        dp_ref[:, C_GS:C_GS + D_MODEL] = dgs_ref[...]

        r_ = lax.broadcasted_iota(jnp.int32, (ts, ts), 0)
        c_ = lax.broadcasted_iota(jnp.int32, (ts, ts), 1)
        rev = (c_ >= r_).astype(BF16)
        dlogf = _dot_split_left(rev, df_ref[...], 3) + carry[...]
        carry[...] = dlogf[0:1, :]
        lane = lax.broadcasted_iota(jnp.int32, (ts, LANES), 1)
        dfl = jnp.where(lane < N_HEADS, dlogf * (1.0 - jnp.exp(logf_ref[...])), 0.0)
        dbias_ref[...] += jnp.sum(dfl, axis=0, keepdims=True)
        dp_ref[:, C_FL:C_FL + LANES] = dfl.astype(BF16)
        dp_ref[:, C_FL + LANES:C_SQ] = jnp.zeros((ts, C_SQ - C_FL - LANES), BF16)

        dh = _dot(dp_ref[...], w_ref[...])
        xf = x_ref[...]
        r = _rms_rinv(xf)
        xhat = xf * r
        dgain_ref[...] += jnp.sum(dh * xhat, axis=0, keepdims=True)
        dn = dh * gain_ref[...]
        dx_ref[...] = dxi_ref[...] + r * (dn - xhat * jnp.mean(dn * xhat, axis=-1, keepdims=True))

    tok = lambda w: pl.BlockSpec((ts, w), lambda i: (nt - 1 - i, 0))
    full = lambda a: pl.BlockSpec(a.shape, lambda i: (0, 0))
    row = lambda w: pl.BlockSpec((1, w), lambda i: (0, 0))
    return _ride_call(
        body, name, (nt,),
        [tok(D_MODEL), tok(D_MODEL), full(gain), full(w_in), tok(ATT_W), tok(ATT_W), tok(ATT_W), tok(ATT_W),
         full(qg), full(kg), full(gmat), tok(ATT_W), tok(LANES), tok(LANES), tok(ATT_W), tok(ATT_W), tok(ATT_W),
         tok(D_MODEL), tok(D_MODEL)],
        [tok(IN_PAD), tok(D_MODEL), row(D_MODEL), row(ATT_W), row(ATT_W), row(LANES)],
        [jax.ShapeDtypeStruct((s_len, IN_PAD), BF16), jax.ShapeDtypeStruct((s_len, D_MODEL), F32),
         jax.ShapeDtypeStruct((1, D_MODEL), F32), jax.ShapeDtypeStruct((1, ATT_W), F32),
         jax.ShapeDtypeStruct((1, ATT_W), F32), jax.ShapeDtypeStruct((1, LANES), F32)],
        [pltpu.VMEM((1, LANES), F32)], ("arbitrary",),
        (x, dx_in, gain, w_in, fqr, fkr, dfqn, dfkn, qg, kg, gmat, dfv, df_col, logf, dsq, dsk, dsv, dgf, dgs), ride)


def _ple_loss(x, p, tgt, gain, wpg, wpp, name):
    s_len = x.shape[0]
    ts = min(512, s_len)

    def body(x_ref, p_ref, t_ref, gain_ref, wpg_ref, wpp_ref, dx_ref, n_ref, ds_ref, dpp_ref, dgain_ref, loss_ref):
        i = pl.program_id(0)

        @pl.when(i == 0)
        def _():
            dgain_ref[...] = jnp.zeros_like(dgain_ref)
            loss_ref[...] = jnp.zeros_like(loss_ref)

        xf = x_ref[...]
        r = _rms_rinv(xf)
        n = xf * r
        hn = (n * gain_ref[...]).astype(BF16)
        n_ref[...] = hn
        sg = _sigmoid(_dot(hn, wpg_ref[...]))
        pp = _dot_nt(p_ref[...].astype(BF16), wpp_ref[...])
        err = (xf + sg * pp) - t_ref[...]
        sq = jnp.sum(jnp.sum(err * err, axis=1, keepdims=True), axis=0, keepdims=True)
        loss_ref[...] += (0.5 / D_MODEL) * sq
        dout = err * (1.0 / D_MODEL)
        dpp_ref[...] = (dout * sg).astype(BF16)
        ds = (dout * pp * (sg * (1.0 - sg))).astype(BF16)
        ds_ref[...] = ds
        dhn = _dot_nt(ds, wpg_ref[...])
        dgain_ref[...] += jnp.sum(dhn * n, axis=0, keepdims=True)
        dn = dhn * gain_ref[...]
        dx_ref[...] = dout + r * (dn - n * jnp.mean(dn * n, axis=-1, keepdims=True))

    tok = lambda w: pl.BlockSpec((ts, w), lambda i: (i, 0))
    full = lambda a: pl.BlockSpec(a.shape, lambda i: (0, 0))
    return pl.pallas_call(
        body, name=name, grid=(s_len // ts,),
        in_specs=[tok(D_MODEL), tok(PLE_DIM), tok(D_MODEL), full(gain), full(wpg), full(wpp)],
        out_specs=[tok(D_MODEL), tok(D_MODEL), tok(D_MODEL), tok(D_MODEL),
                   pl.BlockSpec((1, D_MODEL), lambda i: (0, 0)), pl.BlockSpec((8, LANES), lambda i: (0, 0))],
        out_shape=[jax.ShapeDtypeStruct((s_len, D_MODEL), F32), jax.ShapeDtypeStruct((s_len, D_MODEL), BF16),
                   jax.ShapeDtypeStruct((s_len, D_MODEL), BF16), jax.ShapeDtypeStruct((s_len, D_MODEL), BF16),
                   jax.ShapeDtypeStruct((1, D_MODEL), F32), jax.ShapeDtypeStruct((8, LANES), F32)],
        compiler_params=_cparams("arbitrary"),
    )(x, p, tgt, gain, wpg, wpp)


def _exchange(x, name, broadcast):
    def body(x_ref, out_ref, send_sems, recv_sems, local_sem):
        _exchange_start(x_ref, out_ref, send_sems, recv_sems, local_sem, broadcast)
        _exchange_wait(x_ref, out_ref, send_sems, recv_sems, local_sem, broadcast)

    return pl.pallas_call(
        body, name=name,
        in_specs=[EXCHANGE_SPEC],
        out_specs=EXCHANGE_SPEC,
        out_shape=_exchange_shape(x, broadcast),
        scratch_shapes=list(EXCHANGE_SEMS),
        compiler_params=pltpu.CompilerParams(has_side_effects=True),
    )(x)


def _gather_two_level(x, name):
    def body(x_ref, out_ref, send_sems, recv_sems, local_sem):
        mx, my, mc = lax.axis_index("x"), lax.axis_index("y"), lax.axis_index("c")
        me, sibling = (mx, my, mc), (mx, my, 1 - mc)
        chips = [(1 - mx, my), (mx, 1 - my), (1 - mx, 1 - my)]

        def slot(px, py, pc):
            return out_ref.at[4 * px + 2 * py + pc]

        def copy(k, block, to, src=None):
            return pltpu.make_async_remote_copy(
                src_ref=slot(*block) if src is None else src, dst_ref=slot(*block),
                send_sem=send_sems.at[k], recv_sem=recv_sems.at[k], device_id=to, device_id_type=MESH)

        mine = pltpu.make_async_copy(x_ref, slot(*me), local_sem)
        mine.start()
        first = [copy(0, me, sibling, src=x_ref)]
        first += [copy(1 + j, me, (*chip, mc), src=x_ref) for j, chip in enumerate(chips)]
        for cp in first:
            cp.start()
        passed = [copy(4 + j, (*chip, mc), sibling) for j, chip in enumerate(chips)]
        for j, chip in enumerate(chips):
            copy(1 + j, (*chip, mc), me).wait_recv()
            passed[j].start()
        copy(0, sibling, me).wait_recv()
        for j, chip in enumerate(chips):
            copy(4 + j, (*chip, 1 - mc), me).wait_recv()
        for cp in first + passed:
            cp.wait_send()
        mine.wait()

    return pl.pallas_call(
        body, name=name,
        in_specs=[EXCHANGE_SPEC],
        out_specs=EXCHANGE_SPEC,
        out_shape=_exchange_shape(x, True),
        scratch_shapes=list(EXCHANGE_SEMS),
        compiler_params=pltpu.CompilerParams(has_side_effects=True),
    )(x)


EXCHANGE_SPEC = pl.BlockSpec(memory_space=pl.ANY)
EXCHANGE_SEMS = (pltpu.SemaphoreType.DMA((N_DEV - 1,)), pltpu.SemaphoreType.DMA((N_DEV - 1,)), pltpu.SemaphoreType.DMA)


def _exchange_shape(x, broadcast):
    return jax.ShapeDtypeStruct((N_DEV,) + tuple(x.shape if broadcast else x.shape[1:]), x.dtype)


def _exchange_copies(x_ref, out_ref, send_sems, recv_sems, local_sem, broadcast, with_recv=True):
    mx, my, mc = lax.axis_index("x"), lax.axis_index("y"), lax.axis_index("c")
    me = 4 * mx + 2 * my + mc

    def src(idx):
        return x_ref if broadcast else x_ref.at[idx]

    local = pltpu.make_async_copy(src(me), out_ref.at[me], local_sem)
    pairs = []
    for k in range(1, N_DEV):
        px = (1 - mx) if k & 4 else mx
        py = (1 - my) if k & 2 else my
        pc = (1 - mc) if k & 1 else mc
        peer = 4 * px + 2 * py + pc
        sems = dict(send_sem=send_sems.at[k - 1], recv_sem=recv_sems.at[k - 1], device_id=(px, py, pc), device_id_type=MESH)
        recv = pltpu.make_async_remote_copy(src_ref=src(peer), dst_ref=out_ref.at[peer], **sems) if with_recv else None
        pairs.append((pltpu.make_async_remote_copy(src_ref=src(peer), dst_ref=out_ref.at[me], **sems), recv))
    return local, pairs


def _exchange_start(*refs_and_mode):
    local, pairs = _exchange_copies(*refs_and_mode, with_recv=False)
    local.start()
    for send, _ in pairs:
        send.start()


def _exchange_wait(*refs_and_mode):
    local, pairs = _exchange_copies(*refs_and_mode)
    for _, recv in pairs:
        recv.wait_recv()
    for send, _ in pairs:
        send.wait_send()
    local.wait()


def _riding(body, grid, n_in, n_out, ride):
    if ride is None:
        return body
    broadcast = ride[1]

    def wrapped(*refs):
        ins, x_ref = refs[:n_in], refs[n_in]
        outs, out_ref = refs[n_in + 1:n_in + 1 + n_out], refs[n_in + 1 + n_out]
        scratch, sems = refs[n_in + 2 + n_out:-3], refs[-3:]
        step = pl.program_id(0)
        for d in range(1, len(grid)):
            step = step * grid[d] + pl.program_id(d)
        total = 1
        for g in grid:
            total *= g

        @pl.when(step == 0)
        def _():
            _exchange_start(x_ref, out_ref, *sems, broadcast)

        body(*ins, *outs, *scratch)

        @pl.when(step == total - 1)
        def _():
            _exchange_wait(x_ref, out_ref, *sems, broadcast)

    return wrapped


def _ride_call(body, name, grid, in_specs, out_specs, out_shape, scratch_shapes, sem, operands, ride):
    if ride is None:
        return pl.pallas_call(body, name=name, grid=grid, in_specs=in_specs, out_specs=out_specs, out_shape=out_shape,
                              scratch_shapes=scratch_shapes, compiler_params=_cparams(*sem))(*operands)
    return pl.pallas_call(
        _riding(body, grid, len(in_specs), len(out_specs), ride), name=name, grid=grid,
        in_specs=list(in_specs) + [EXCHANGE_SPEC], out_specs=list(out_specs) + [EXCHANGE_SPEC],
        out_shape=list(out_shape) + [_exchange_shape(*ride)],
        scratch_shapes=list(scratch_shapes) + list(EXCHANGE_SEMS),
        compiler_params=_cparams(*(["arbitrary"] * len(grid))),
    )(*operands, ride[0])


def _adamw_math(w, g, m, v):
    m2 = ADAM_B1 * m + (1.0 - ADAM_B1) * g
    v2 = ADAM_B2 * v + (1.0 - ADAM_B2) * (g * g)
    m_hat = m2 / (1.0 - ADAM_B1 ** ADAM_STEP)
    v_hat = v2 / (1.0 - ADAM_B2 ** ADAM_STEP)
    delta = -ADAM_LR * (m_hat / (jnp.sqrt(v_hat) + ADAM_EPS) + ADAM_WD * w)
    return delta, m2, v2


def _sum_parts(parts, name, tr):
    _, rows, cols = parts.shape

    def body(p_ref, g_ref):
        g = p_ref[0].astype(F32)
        for s in range(1, N_DEV):
            g = g + p_ref[s].astype(F32)
        g_ref[...] = g

    return pl.pallas_call(
        body, name=name, grid=(rows // tr,),
        in_specs=[pl.BlockSpec((N_DEV, tr, cols), lambda i: (0, i, 0))],
        out_specs=pl.BlockSpec((tr, cols), lambda i: (i, 0)),
        out_shape=jax.ShapeDtypeStruct((rows, cols), F32),
        compiler_params=_cparams("parallel"),
    )(parts)


ADAM_SPLIT_ELEMS = 400_000


def _adamw_shard(g, w, m, v, name):
    rows, cols = w.shape
    tr = rows // 2 if rows * cols > ADAM_SPLIT_ELEMS else rows

    def body(g_ref, w_ref, m_ref, v_ref, d_ref, m2_ref, v2_ref):
        d_ref[...], m2_ref[...], v2_ref[...] = _adamw_math(w_ref[...], g_ref[...], m_ref[...], v_ref[...])

    blk = pl.BlockSpec((tr, cols), lambda i: (i, 0))
    return pl.pallas_call(
        body, name=name, grid=(rows // tr,),
        in_specs=[blk] * 4, out_specs=[blk] * 3,
        out_shape=[jax.ShapeDtypeStruct((rows, cols), F32)] * 3,
        compiler_params=_cparams("parallel"),
    )(g, w, m, v)


def _adamw(parts, w, m, v, name, tr):
    rows, cols = w.shape

    def body(p_ref, w_ref, m_ref, v_ref, g_ref, d_ref, m2_ref, v2_ref):
        g = p_ref[0].astype(F32)
        for s in range(1, N_DEV):
            g = g + p_ref[s].astype(F32)
        g_ref[...] = g
        d_ref[...], m2_ref[...], v2_ref[...] = _adamw_math(w_ref[...], g, m_ref[...], v_ref[...])

    blk = pl.BlockSpec((tr, cols), lambda i: (i, 0))
    return pl.pallas_call(
        body, name=name, grid=(rows // tr,),
        in_specs=[pl.BlockSpec((N_DEV, tr, cols), lambda i: (0, i, 0)), blk, blk, blk],
        out_specs=[blk] * 4,
        out_shape=[jax.ShapeDtypeStruct((rows, cols), F32)] * 4,
        compiler_params=_cparams("parallel"),
    )(parts, w, m, v)


TRANSPOSED = frozenset(("ffn1_w_gate", "ffn1_w_up", "w_in", "w_branch_fox", "w_branch_sb", "ffn2_w_gate", "ffn2_w_up",
                        "w_ple_proj"))
F_PAD_ROWS = C_SQ - FL_REAL_END


def _pack(pieces, group, dtype):
    out = []
    for name in GATHER_GROUPS[group]:
        r = pieces[name].T if name in TRANSPOSED else pieces[name]
        r = r.reshape(-1, D_MODEL).astype(dtype)
        if r.shape[0] != PACK_ROWS[name]:
            r = jnp.pad(r, ((0, PACK_ROWS[name] - r.shape[0]), (0, 0)))
        out.append(r)
    return jnp.concatenate(out, axis=0)


def _real_rows(name):
    return W_IN_ROWS if name == "w_in" else PACK_ROWS[name]


def _gathered(got, name, shape):
    off = GATHER_OFF[name]
    return got[:, off:off + _real_rows(name), :].reshape(shape)


def _w_in_t_padded(got):
    t = _gathered(got, "w_in", (IN_REAL, D_MODEL))
    return jnp.concatenate([t[:FL_REAL_END], jnp.zeros((F_PAD_ROWS, D_MODEL), t.dtype), t[FL_REAL_END:]], axis=0)


def _pack_chunks(grads, group):
    out = []
    for name in SCATTER_GROUPS[group]:
        g = grads[name]
        if name == "w_in":
            g = jnp.concatenate([g[:FL_REAL_END], g[C_SQ:]], axis=0)
        c = g.reshape(N_DEV, -1, D_MODEL).astype(BF16)
        if c.shape[1] != PACK_ROWS[name]:
            c = jnp.pad(c, ((0, 0), (0, PACK_ROWS[name] - c.shape[1]), (0, 0)))
        out.append(c)
    return out[0] if len(out) == 1 else jnp.concatenate(out, axis=1)


def _shard_grad(packed, name, shape):
    off = SCATTER_OFF[name]
    rows = packed[off:off + _real_rows(name), :]
    return rows.reshape(shape[1], shape[0]).T if name in TRANSPOSED else rows.reshape(shape)


WEIGHT_NAMES = ['ffn1_norm', 'ffn1_w_gate', 'ffn1_w_up', 'ffn1_w_down', 'mix_norm', 'w_in', 'forget_bias', 'q_norm',
                'k_norm', 'w_branch_fox', 'w_branch_sb', 'w_out', 'ffn2_norm', 'ffn2_w_gate', 'ffn2_w_up',
                'ffn2_w_down', 'ple_norm', 'w_ple_gate', 'w_ple_proj']
SMALL_NAMES = ('ffn1_norm', 'mix_norm', 'ffn2_norm', 'ple_norm', 'q_norm', 'k_norm', 'forget_bias')
Q_OFF, K_OFF, B_OFF, LOSS_OFF = 0, HEAD_DIM, 2 * HEAD_DIM, 2 * HEAD_DIM + N_HEADS


def _pack_small(vals, loss=None):
    tail = [vals['q_norm'].reshape(1, -1), vals['k_norm'].reshape(1, -1), vals['forget_bias'].reshape(1, -1)]
    used = LOSS_OFF
    if loss is not None:
        tail.append(loss.reshape(1, 1))
        used += 1
    tail.append(jnp.zeros((1, D_MODEL - used), F32))
    rows = [vals[n].reshape(1, D_MODEL) for n in SMALL_NAMES[:4]] + [jnp.concatenate(tail, axis=1)]
    rows.append(jnp.zeros((SMALL_ROWS - len(rows), D_MODEL), F32))
    return jnp.concatenate(rows, axis=0)


def _unpack_small(packed, name, shape):
    if name in SMALL_NAMES[:4]:
        return packed[SMALL_NAMES.index(name)].reshape(shape)
    off, n = {'q_norm': (Q_OFF, HEAD_DIM), 'k_norm': (K_OFF, HEAD_DIM), 'forget_bias': (B_OFF, N_HEADS)}[name]
    return packed[4, off:off + n].reshape(shape)


def _step(x, p, tgt, w):
    row = lambda a: a.reshape(1, -1).astype(F32)
    g_ffn1, g_mix, g_ffn2, g_ple = (row(w[n]) for n in SMALL_NAMES[:4])
    qg = jnp.tile(row(w['q_norm']), (1, N_HEADS))
    kg = jnp.tile(row(w['k_norm']), (1, N_HEADS))
    bias = jnp.pad(row(w['forget_bias']), ((0, 0), (0, LANES - N_HEADS)))
    half = D_FF // 2
    grads = {}

    blk = lambda n: GATHER_OFF[n] // FFN_SHARD
    ffn1 = tuple(blk(n) for n in ("ffn1_w_gate", "ffn1_w_up", "ffn1_w_down"))
    ffn2 = tuple(blk(n) for n in ("ffn2_w_gate", "ffn2_w_up", "ffn2_w_down"))
    got0 = _gather_two_level(_pack(w, 0, BF16), "gather_ffn1")
    x1, g1, u1, h1, got1 = _ffn_fwd(x, g_ffn1, got0, ffn1, "ffn1_fwd", ride=(_pack(w, 1, BF16), True))
    w_in = _w_in_t_padded(got1)
    wbf = _gathered(got1, "w_branch_fox", (D_MODEL, ATT_W))
    wbs = _gathered(got1, "w_branch_sb", (D_MODEL, ATT_W))
    wo = _gathered(got1, "w_out", (D_MODEL, D_MODEL))
    (hmix, fqr, fkr, fqn, fkn, fv, logf, f_col, f_row, sq, sk, sv, gf, gs, kmax) = _mix_fwd(
        x1, g_mix, w_in, bias, qg, kg, "mix_fwd")
    y_fox, lse, got2 = _fox_fwd(fqn, fkn, fv, f_col, f_row, kmax, "fox_fwd", ride=(_pack(w, 2, BF16), True))
    wpg = _gathered(got2, "w_ple_gate", (D_MODEL, D_MODEL))
    wpp = _gathered(got2, "w_ple_proj", (D_MODEL, PLE_DIM))
    y_sb, y_sb32 = _sb_fwd(sq, sk, sv, "sb_fwd")
    x2 = _merge_fwd(x1, y_fox, y_sb, gf, gs, wbf, wbs, wo, "merge_fwd")
    x3, g2, u2, h2, = _ffn_fwd(x2, g_ffn2, got2, ffn2, "ffn2_fwd")
    dx3, n_ple, ds_ple, dpp, dg_ple, loss = _ple_loss(x3, p, tgt, g_ple, wpg, wpp, "ple_loss")

    grads['w_ple_gate'] = _wgrad(n_ple, ds_ple, "dw_ple_gate", D_MODEL, D_MODEL)
    grads['w_ple_proj'] = _wgrad(dpp, p, "dw_ple_proj", D_MODEL, PLE_DIM)
    dg2, du2, act2 = _ffn_bwd_hidden(dx3, g2, u2, got2, ffn2[2], "ffn2_bwd_hidden")
    grads['ffn2_w_gate'] = _wgrad(dg2, h2, "dw_ffn2_gate", half, D_MODEL)
    grads['ffn2_w_up'] = _wgrad(du2, h2, "dw_ffn2_up", half, D_MODEL)
    grads['ffn2_w_down'] = _wgrad(act2, dx3, "dw_ffn2_down", half, D_MODEL)
    dx2, dg_ffn2 = _ffn_bwd_input(x2, dx3, g_ffn2, dg2, du2, got2, ffn2[:2], "ffn2_bwd_input")
    dyf, dys, dgf, dgs, dbf, dbs, merged = _merge_bwd(dx2, y_fox, y_sb, gf, gs, wbf, wbs, wo, "merge_bwd")
    grads['w_branch_fox'] = _wgrad(dbf, y_fox, "dw_branch_fox", D_MODEL, ATT_W)
    grads['w_branch_sb'] = _wgrad(dbs, y_sb, "dw_branch_sb", D_MODEL, ATT_W)
    grads['w_out'] = _wgrad(merged, dx2, "dw_out", D_MODEL, D_MODEL)
    dfqn, dfkn, dfv, dft, dfq, part2 = _fox_bwd(fqn, fkn, fv, dyf, y_fox, lse, f_col, f_row, kmax, "fox_bwd",
                                                ride=(_pack_chunks(grads, 2), False))
    dsq, dsk, dsv = _sb_bwd(sq, sk, sv, dys, y_sb32, "sb_bwd")
    s_len = x.shape[0]
    df_heads = dft[:, :2, :] + jnp.stack([dfq[:, :, 0], dfq[:, :, HEAD_DIM]], axis=1)
    df_col = jnp.pad(df_heads.reshape(N_HEADS, s_len).T, ((0, 0), (0, LANES - N_HEADS)))
    dproj, dx1, dg_mix, dqg, dkg, dbias = _mix_bwd(
        x1, dx2, g_mix, w_in, fqr, fkr, dfqn, dfkn, qg, kg, dfv, df_col, logf, dsq, dsk, dsv, dgf, dgs, "mix_bwd")
    grads['w_in'] = _wgrad(dproj, hmix, "dw_in", IN_PAD // 3, D_MODEL)
    dg1, du1, act1, part1 = _ffn_bwd_hidden(dx1, g1, u1, got0, ffn1[2], "ffn1_bwd_hidden",
                                            ride=(_pack_chunks(grads, 1), False))
    grads['ffn1_w_gate'] = _wgrad(dg1, h1, "dw_ffn1_gate", half, D_MODEL)
    grads['ffn1_w_up'] = _wgrad(du1, h1, "dw_ffn1_up", half, D_MODEL)
    grads['ffn1_w_down'] = _wgrad(act1, dx1, "dw_ffn1_down", half, D_MODEL)
    dx0, dg_ffn1, part0 = _ffn_bwd_input(x, dx1, g_ffn1, dg1, du1, got0, ffn1[:2], "ffn1_bwd_input",
                                         ride=(_pack_chunks(grads, 0), False))

    fold = lambda a: a.reshape(N_HEADS, HEAD_DIM).sum(axis=0).reshape(1, HEAD_DIM)
    small_g = {'ffn1_norm': dg_ffn1, 'mix_norm': dg_mix, 'ffn2_norm': dg_ffn2, 'ple_norm': dg_ple,
               'q_norm': fold(dqg), 'k_norm': fold(dkg), 'forget_bias': dbias[:, :N_HEADS]}
    return loss[0, 0], dx0, (part0, part1, part2), small_g


def kernel(x, p, ffn1_norm, ffn1_w_gate, ffn1_w_up, ffn1_w_down, mix_norm, w_in, forget_bias, q_norm, k_norm, w_branch_fox, w_branch_sb, w_out, ffn2_norm, ffn2_w_gate, ffn2_w_up, ffn2_w_down, ple_norm, w_ple_gate, w_ple_proj, loss_target, m_ffn1_norm, m_ffn1_w_gate, m_ffn1_w_up, m_ffn1_w_down, m_mix_norm, m_w_in, m_forget_bias, m_q_norm, m_k_norm, m_w_branch_fox, m_w_branch_sb, m_w_out, m_ffn2_norm, m_ffn2_w_gate, m_ffn2_w_up, m_ffn2_w_down, m_ple_norm, m_w_ple_gate, m_w_ple_proj, v_ffn1_norm, v_ffn1_w_gate, v_ffn1_w_up, v_ffn1_w_down, v_mix_norm, v_w_in, v_forget_bias, v_q_norm, v_k_norm, v_w_branch_fox, v_w_branch_sb, v_w_out, v_ffn2_norm, v_ffn2_w_gate, v_ffn2_w_up, v_ffn2_w_down, v_ple_norm, v_w_ple_gate, v_w_ple_proj):
    args = dict(locals())
    w = {n: args[n][0] for n in WEIGHT_NAMES}
    m = {n: args["m_" + n][0] for n in WEIGHT_NAMES}
    v = {n: args["v_" + n][0] for n in WEIGHT_NAMES}
    loss, dx, parts, small_g = _step(x[0], p[0, 0], loss_target[0], w)

    big = {}
    for grp, part in enumerate(parts):
        summed = _sum_parts(part, f"sum_grads_{grp}", SUM_TILE_ROWS[grp])
        for n in SCATTER_GROUPS[grp]:
            g = _shard_grad(summed, n, w[n].shape)
            big[n] = (g,) + tuple(_adamw_shard(g, w[n], m[n], v[n], "adamw_" + n))
    small_parts = _exchange(_pack_small(small_g, loss), "gather_small", True)
    sw, sm, sv = (_pack_small(t) for t in (w, m, v))
    small = _adamw(small_parts, sw, sm, sv, "adamw_small", SMALL_ROWS)

    outs = [small[0][4, LOSS_OFF], dx.reshape(x.shape)]
    for kind in range(4):
        for n in WEIGHT_NAMES:
            shape = args[n].shape
            outs.append(_unpack_small(small[kind], n, shape) if n in SMALL_NAMES else big[n][kind].reshape(shape))
    return tuple(outs)
```

```python
import jax
import jax.numpy as jnp
from jax import lax
from jax.experimental import pallas as pl
from jax.experimental.pallas import tpu as pltpu

F32 = jnp.float32
BF16 = jnp.bfloat16

D_MODEL = 1024
D_FF = 2816
N_HEADS = 8
HEAD_DIM = 64
ATT_W = N_HEADS * HEAD_DIM
PLE_DIM = 256
EPS = 1e-6
N_DEV = 8
MESH = pl.DeviceIdType.MESH

LANES = 128
V7X_SCOPED_VMEM_BYTES = 56 * 1024 * 1024

C_FQ, C_FK, C_FV, C_FL = 0, 512, 1024, 1536
C_SQ, C_SK, C_SV, C_GF, C_GS = 1792, 2304, 2816, 3328, 4352
IN_PAD = 5376
IN_REAL = 5128
FL_REAL_END = 1544

ADAM_LR = 0.001
ADAM_B1 = 0.9
ADAM_B2 = 0.999
ADAM_EPS = 1e-08
ADAM_WD = 0.01
ADAM_STEP = 10

PACK_ROWS = {"ffn1_w_gate": 352, "ffn1_w_up": 352, "ffn1_w_down": 352, "w_in": 656, "w_branch_fox": 64,
             "w_branch_sb": 64, "w_out": 128, "ffn2_w_gate": 352, "ffn2_w_up": 352, "ffn2_w_down": 352,
             "w_ple_gate": 128, "w_ple_proj": 32}
GATHER_GROUPS = (
    ("ffn1_w_gate", "ffn1_w_up", "ffn1_w_down"),
    ("w_in", "w_branch_fox", "w_branch_sb", "w_out"),
    ("ffn2_w_gate", "ffn2_w_up", "ffn2_w_down", "w_ple_gate", "w_ple_proj"),
)
SCATTER_GROUPS = (
    ("ffn1_w_gate", "ffn1_w_up", "ffn1_w_down"),
    ("w_in",),
    ("ffn2_w_gate", "ffn2_w_up", "ffn2_w_down", "w_ple_gate", "w_ple_proj", "w_branch_fox", "w_branch_sb", "w_out"),
)
SUM_TILE_ROWS = (528, 656, 368)


def _offsets(groups):
    off = {}
    for grp in groups:
        o = 0
        for n in grp:
            off[n] = o
            o += PACK_ROWS[n]
    return off


GATHER_OFF = _offsets(GATHER_GROUPS)
SCATTER_OFF = _offsets(SCATTER_GROUPS)
W_IN_ROWS = 641

SMALL_ROWS = 8


def _cparams(*sem):
    return pltpu.CompilerParams(dimension_semantics=sem, vmem_limit_bytes=V7X_SCOPED_VMEM_BYTES)


def _dot(a, b):
    return jnp.dot(a, b, preferred_element_type=F32)


def _dot_nt(a, b):
    return lax.dot_general(a, b, (((1,), (1,)), ((), ())), preferred_element_type=F32)


def _dot_tn(a, b):
    return lax.dot_general(a, b, (((0,), (0,)), ((), ())), preferred_element_type=F32)


def _split(x, parts):
    out = []
    r = x
    for _ in range(parts):
        p = r.astype(BF16)
        out.append(p)
        r = r - p.astype(F32)
    return out


def _dot_split(x, m, parts):
    acc = None
    for p in _split(x, parts):
        t = _dot(p, m)
        acc = t if acc is None else acc + t
    return acc


def _dot_split_left(m, x, parts):
    acc = None
    for p in _split(x, parts):
        t = _dot(m, p)
        acc = t if acc is None else acc + t
    return acc


def _rms_rinv(xf):
    return lax.rsqrt(jnp.mean(xf * xf, axis=-1, keepdims=True) + EPS)


def _sigmoid(x):
    return 1.0 / (1.0 + jnp.exp(-x))


def _softplus_neg_abs(z):
    return jnp.log(1.0 + jnp.exp(-jnp.abs(z)))


FFN_SHARD = D_FF // N_DEV
FFN_CHUNK = 4


def _ffn_w_spec(blk, index_map):
    return pl.BlockSpec((FFN_CHUNK, FFN_SHARD, D_MODEL), lambda *g: (index_map(*g), blk, 0))


def _ffn_w(ref):
    return ref[...].reshape(FFN_CHUNK * FFN_SHARD, D_MODEL)


def _ffn_fwd(x, gain, wbuf, blks, name, ride=None):
    s_len = x.shape[0]
    ts = min(512, s_len)
    fc = FFN_CHUNK * FFN_SHARD
    nt, nc = s_len // ts, D_FF // fc

    def body(x_ref, gain_ref, wg_ref, wu_ref, wd_ref, y_ref, g_ref, u_ref, h_ref, acc_scr):
        j = pl.program_id(1)

        @pl.when(j == 0)
        def _():
            xf = x_ref[...]
            h_ref[...] = ((xf * _rms_rinv(xf)) * gain_ref[...]).astype(BF16)
            acc_scr[...] = jnp.zeros_like(acc_scr)

        h = h_ref[...]
        g = _dot_nt(h, _ffn_w(wg_ref))
        u = _dot_nt(h, _ffn_w(wu_ref))
        g_ref[...] = g.astype(BF16)
        u_ref[...] = u.astype(BF16)
        a = (g * _sigmoid(g) * u).astype(BF16)
        acc_scr[...] += _dot(a, _ffn_w(wd_ref))

        @pl.when(j == nc - 1)
        def _():
            y_ref[...] = x_ref[...] + 0.5 * acc_scr[...]

    tok = pl.BlockSpec((ts, D_MODEL), lambda i, j: (i, 0))
    hid = pl.BlockSpec((ts, fc), lambda i, j: (i, j))
    return _ride_call(
        body, name, (nt, nc),
        [tok, pl.BlockSpec((1, D_MODEL), lambda i, j: (0, 0))] + [_ffn_w_spec(b, lambda i, j: j) for b in blks],
        [tok, hid, hid, tok],
        [jax.ShapeDtypeStruct((s_len, D_MODEL), F32), jax.ShapeDtypeStruct((s_len, D_FF), BF16),
         jax.ShapeDtypeStruct((s_len, D_FF), BF16), jax.ShapeDtypeStruct((s_len, D_MODEL), BF16)],
        [pltpu.VMEM((ts, D_MODEL), F32)], ("parallel", "arbitrary"), (x, gain, wbuf, wbuf, wbuf), ride)


def _ffn_bwd_hidden(dy, g, u, wbuf, blk, name, ride=None):
    s_len = dy.shape[0]
    ts = min(512, s_len)
    fc = FFN_CHUNK * FFN_SHARD
    nt, nc = s_len // ts, D_FF // fc

    def body(dy_ref, g_ref, u_ref, wd_ref, dg_ref, du_ref, act_ref):
        da = 0.5 * _dot_nt(dy_ref[...].astype(BF16), _ffn_w(wd_ref))
        gf = g_ref[...].astype(F32)
        uf = u_ref[...].astype(F32)
        sg = _sigmoid(gf)
        silu = gf * sg
        dg_ref[...] = (da * uf * (sg * (1.0 + gf * (1.0 - sg)))).astype(BF16)
        du_ref[...] = (da * silu).astype(BF16)
        act_ref[...] = (0.5 * silu * uf).astype(BF16)

    hid = pl.BlockSpec((ts, fc), lambda c, t: (t, c))
    return _ride_call(
        body, name, (nc, nt),
        [pl.BlockSpec((ts, D_MODEL), lambda c, t: (t, 0)), hid, hid, _ffn_w_spec(blk, lambda c, t: c)],
        [hid, hid, hid], [jax.ShapeDtypeStruct((s_len, D_FF), BF16)] * 3, [], ("parallel", "parallel"),
        (dy, g, u, wbuf), ride)


def _ffn_bwd_input(x, dy, gain, dg, du, wbuf, blks, name, ride=None):
    s_len = x.shape[0]
    ts = min(512, s_len)
    fc = FFN_CHUNK * FFN_SHARD
    nt, nc = s_len // ts, D_FF // fc

    def body(x_ref, dy_ref, gain_ref, dg_ref, du_ref, wg_ref, wu_ref, dx_ref, dgain_ref, acc):
        i = pl.program_id(0)
        j = pl.program_id(1)
        part = _dot(dg_ref[...], _ffn_w(wg_ref)) + _dot(du_ref[...], _ffn_w(wu_ref))

        @pl.when(j == 0)
        def _():
            acc[...] = part

        @pl.when(j > 0)
        def _():
            acc[...] += part

        @pl.when(j == nc - 1)
        def _():
            xf = x_ref[...]
            r = _rms_rinv(xf)
            xhat = xf * r
            dh = acc[...]
            dgp = jnp.sum(dh * xhat, axis=0, keepdims=True)

            @pl.when(i == 0)
            def _():
                dgain_ref[...] = dgp

            @pl.when(i > 0)
            def _():
                dgain_ref[...] += dgp

            dn = dh * gain_ref[...]
            dx_ref[...] = dy_ref[...] + r * (dn - xhat * jnp.mean(dn * xhat, axis=-1, keepdims=True))

    tok = pl.BlockSpec((ts, D_MODEL), lambda i, j: (i, 0))
    row = pl.BlockSpec((1, D_MODEL), lambda i, j: (0, 0))
    hid = pl.BlockSpec((ts, fc), lambda i, j: (i, j))
    return _ride_call(
        body, name, (nt, nc), [tok, tok, row, hid, hid] + [_ffn_w_spec(b, lambda i, j: j) for b in blks], [tok, row],
        [jax.ShapeDtypeStruct((s_len, D_MODEL), F32), jax.ShapeDtypeStruct((1, D_MODEL), F32)],
        [pltpu.VMEM((ts, D_MODEL), F32)], ("arbitrary", "arbitrary"), (x, dy, gain, dg, du, wbuf, wbuf), ride)


def _ffn_bwd_fused(x, dy, gain, g, u, wbuf, blks, name):
    s_len = x.shape[0]
    ts = min(512, s_len)
    fc = FFN_CHUNK * FFN_SHARD
    nt = s_len // ts
    assert D_FF == 2 * fc

    def hidden(dy_ref, g_ref, u_ref, wg_ref, wu_ref, wd_ref, dg_ref, du_ref, act_ref):
        da = 0.5 * _dot_nt(dy_ref[...].astype(BF16), _ffn_w(wd_ref))
        gf = g_ref[...].astype(F32)
        uf = u_ref[...].astype(F32)
        sg = _sigmoid(gf)
        silu = gf * sg
        dg = (da * uf * (sg * (1.0 + gf * (1.0 - sg)))).astype(BF16)
        du = (da * silu).astype(BF16)
        dg_ref[...] = dg
        du_ref[...] = du
        act_ref[...] = (0.5 * silu * uf).astype(BF16)
        return _dot(dg, _ffn_w(wg_ref)) + _dot(du, _ffn_w(wu_ref))

    def first(dy_ref, g_ref, u_ref, wg_ref, wu_ref, wd_ref, dg_ref, du_ref, act_ref, dh_ref):
        dh_ref[...] = hidden(dy_ref, g_ref, u_ref, wg_ref, wu_ref, wd_ref, dg_ref, du_ref, act_ref)

    def second(x_ref, dy_ref, gain_ref, g_ref, u_ref, wg_ref, wu_ref, wd_ref, dh0_ref, dg_half, du_half, act_half,
               dg_ref, du_ref, act_ref, dx_ref, dgain_ref):
        i = pl.program_id(0)
        dh = dh0_ref[...] + hidden(dy_ref, g_ref, u_ref, wg_ref, wu_ref, wd_ref, dg_ref, du_ref, act_ref)
        xf = x_ref[...]
        r = _rms_rinv(xf)
        xhat = xf * r
        dgp = jnp.sum(dh * xhat, axis=0, keepdims=True)

        @pl.when(i == 0)
        def _():
            dgain_ref[...] = dgp

        @pl.when(i > 0)
        def _():
            dgain_ref[...] += dgp

        dn = dh * gain_ref[...]
        dx_ref[...] = dy_ref[...] + r * (dn - xhat * jnp.mean(dn * xhat, axis=-1, keepdims=True))

    tok = pl.BlockSpec((ts, D_MODEL), lambda i: (i, 0))
    row = pl.BlockSpec((1, D_MODEL), lambda i: (0, 0))
    hid = lambda c: pl.BlockSpec((ts, fc), lambda i: (i, c))
    wts = lambda c: [pl.BlockSpec((FFN_CHUNK, FFN_SHARD, D_MODEL), lambda i, b=b: (c, b, 0),
                                  pipeline_mode=pl.Buffered(1)) for b in blks]
    hidden_shapes = [jax.ShapeDtypeStruct((s_len, D_FF), BF16)] * 3
    dg, du, act, dh0 = pl.pallas_call(
        first, name=name + "_a", grid=(nt,),
        in_specs=[tok, hid(0), hid(0)] + wts(0),
        out_specs=[hid(0)] * 3 + [tok],
        out_shape=hidden_shapes + [jax.ShapeDtypeStruct((s_len, D_MODEL), F32)],
        compiler_params=_cparams("parallel"),
    )(dy, g, u, wbuf, wbuf, wbuf)
    filled = pl.BlockSpec(memory_space=pl.ANY)
    dg, du, act, dx, dgain = pl.pallas_call(
        second, name=name + "_b", grid=(nt,),
        in_specs=[tok, tok, row, hid(1), hid(1)] + wts(1) + [tok, filled, filled, filled],
        out_specs=[hid(1)] * 3 + [tok, row],
        out_shape=hidden_shapes + [jax.ShapeDtypeStruct((s_len, D_MODEL), F32), jax.ShapeDtypeStruct((1, D_MODEL), F32)],
        input_output_aliases={9: 0, 10: 1, 11: 2},
        compiler_params=_cparams("arbitrary"),
    )(x, dy, gain, g, u, wbuf, wbuf, wbuf, dh0, dg, du, act)
    return dg, du, act, dx, dgain


def _wgrad(a, b, name, tk, tn):
    s_len, k_dim = a.shape
    n_dim = b.shape[1]
    ts = min(1024, s_len)
    ns = s_len // ts

    def body(a_ref, b_ref, o_ref, acc):
        s = pl.program_id(2)
        p = _dot_tn(a_ref[...].astype(BF16), b_ref[...].astype(BF16))

        @pl.when(s == 0)
        def _():
            acc[...] = p

        @pl.when(s > 0)
        def _():
            acc[...] += p

        @pl.when(s == ns - 1)
        def _():
            o_ref[...] = acc[...].astype(BF16)

    return pl.pallas_call(
        body, name=name, grid=(k_dim // tk, n_dim // tn, ns),
        in_specs=[
            pl.BlockSpec((ts, tk), lambda k, n, s: (s, k)),
            pl.BlockSpec((ts, tn), lambda k, n, s: (s, n)),
        ],
        out_specs=pl.BlockSpec((tk, tn), lambda k, n, s: (k, n)),
        out_shape=jax.ShapeDtypeStruct((k_dim, n_dim), BF16),
        scratch_shapes=[pltpu.VMEM((tk, tn), F32)],
        compiler_params=_cparams("parallel", "parallel", "arbitrary"),
    )(a, b)


HEAD_SUM_PARTS = 2


def _head_group_matrix():
    r = lax.broadcasted_iota(jnp.int32, (ATT_W, ATT_W), 0) // HEAD_DIM
    c = lax.broadcasted_iota(jnp.int32, (ATT_W, ATT_W), 1) // HEAD_DIM
    return (r == c).astype(BF16)


def _mix_fwd(x, gain, w_in, bias, qg, kg, name, ride=None):
    s_len = x.shape[0]
    ts = min(256, s_len)
    nt = s_len // ts
    gmat = _head_group_matrix()

    def body(x_ref, gain_ref, w_ref, bias_ref, qg_ref, kg_ref, gm_ref,
             h_ref, fqr_ref, fkr_ref, fqn_ref, fkn_ref, fv_ref, logf_ref, f_ref, ft_ref,
             sq_ref, sk_ref, sv_ref, gf_ref, gs_ref, kmax_ref, carry):
        i = pl.program_id(0)
        xf = x_ref[...]
        h = ((xf * _rms_rinv(xf)) * gain_ref[...]).astype(BF16)
        h_ref[...] = h
        gm = gm_ref[...]

        def proj(lo, n):
            return _dot_nt(h, w_ref[lo:lo + n, :])

        def headnorm(raw, g):
            ms = _dot_split(raw * raw, gm, HEAD_SUM_PARTS) * (1.0 / HEAD_DIM)
            return ((raw * lax.rsqrt(ms + EPS)) * g).astype(BF16)

        fq = proj(C_FQ, ATT_W)
        fqr_ref[...] = fq
        fqn_ref[...] = headnorm(fq, qg_ref[...])
        fk = proj(C_FK, ATT_W)
        fkr_ref[...] = fk
        fkn = headnorm(fk, kg_ref[...])
        fkn_ref[...] = fkn
        kn2 = jnp.max(_dot_split(jnp.square(fkn.astype(F32)), gm, HEAD_SUM_PARTS), axis=0, keepdims=True)

        @pl.when(i == 0)
        def _():
            kmax_ref[...] = kn2

        @pl.when(i > 0)
        def _():
            kmax_ref[...] = jnp.maximum(kmax_ref[...], kn2)
        fv_ref[...] = proj(C_FV, ATT_W).astype(BF16)
        sq_ref[...] = proj(C_SQ, ATT_W).astype(BF16)
        sk_ref[...] = proj(C_SK, ATT_W).astype(BF16)
        sv_ref[...] = proj(C_SV, ATT_W).astype(BF16)
        gf_ref[...] = proj(C_GF, D_MODEL)
        gs_ref[...] = proj(C_GS, D_MODEL)

        fl = proj(C_FL, LANES) + bias_ref[...]
        lane = lax.broadcasted_iota(jnp.int32, fl.shape, 1)
        logf = jnp.where(lane < N_HEADS, jnp.minimum(fl, 0.0) - _softplus_neg_abs(fl), 0.0)
        logf_ref[...] = logf

        @pl.when(i == 0)
        def _():
            carry[...] = jnp.zeros_like(carry)

        r = lax.broadcasted_iota(jnp.int32, (ts, ts), 0)
        c = lax.broadcasted_iota(jnp.int32, (ts, ts), 1)
        tri = (r >= c).astype(BF16)
        f_tile = _dot_split_left(tri, logf, 3) + carry[...]
        f_ref[...] = f_tile
        ft_ref[...] = f_tile.T[:N_HEADS, :]
        carry[...] = f_tile[ts - 1:ts, :]

    tok = lambda w: pl.BlockSpec((ts, w), lambda i: (i, 0))
    full = lambda a: pl.BlockSpec(a.shape, lambda i: (0, 0))
    f32o = lambda w: jax.ShapeDtypeStruct((s_len, w), F32)
    b16o = lambda w: jax.ShapeDtypeStruct((s_len, w), BF16)
    return _ride_call(
        body, name, (nt,),
        [tok(D_MODEL), full(gain), full(w_in), full(bias), full(qg), full(kg), full(gmat)],
        [
            tok(D_MODEL), tok(ATT_W), tok(ATT_W), tok(ATT_W), tok(ATT_W), tok(ATT_W), tok(LANES), tok(LANES),
            pl.BlockSpec((N_HEADS, ts), lambda i: (0, i)),
            tok(ATT_W), tok(ATT_W), tok(ATT_W), tok(D_MODEL), tok(D_MODEL),
            pl.BlockSpec((1, ATT_W), lambda i: (0, 0)),
        ],
        [
            b16o(D_MODEL), f32o(ATT_W), f32o(ATT_W), b16o(ATT_W), b16o(ATT_W), b16o(ATT_W), f32o(LANES), f32o(LANES),
            jax.ShapeDtypeStruct((N_HEADS, s_len), F32),
            b16o(ATT_W), b16o(ATT_W), b16o(ATT_W), f32o(D_MODEL), f32o(D_MODEL),
            jax.ShapeDtypeStruct((1, ATT_W), F32),
        ],
        [pltpu.VMEM((1, LANES), F32)], ("arbitrary",), (x, gain, w_in, bias, qg, kg, gmat), ride)


ATT_T = 256
ATT_ROWS = 2
EXP_ZERO = 88.0


def _att_tiling(s_len):
    t = min(ATT_T, s_len)
    nr = min(ATT_ROWS, s_len // t)
    return t, nr, s_len // (t * nr)


def _pair_specs(s_len, tq):
    qblk = pl.BlockSpec((tq, LANES), lambda hp, i: (i, hp))
    kvfull = pl.BlockSpec((s_len, LANES), lambda hp, i: (0, hp))
    return qblk, kvfull


def _walk_tiles(i, nr, load, sub, flush, init, more=None, trips=None):
    base = i * nr
    carries = list(init)
    for r in range(nr):
        for kk in range(r, -1, -1):
            (carries[r],), side = sub([r], load(base + kk), [carries[r]], kk == r)
            flush(base + kk, side)

    def visit(n, cs):
        kb = base - 1 - n
        cs, side = sub(list(range(nr)), load(kb), list(cs), False)
        flush(kb, side)
        return tuple(cs)

    if trips is not None:
        return lax.fori_loop(0, trips(carries, base), visit, tuple(carries))

    def cond(state):
        return jnp.logical_and(state[0] < base, state[1] > 0)

    def step(state):
        n, _, cs = state
        cs = visit(n, cs)
        return n + 1, more(cs, base - 2 - n), cs

    return lax.while_loop(cond, step, (jnp.int32(0), more(carries, base - 1), tuple(carries)))[2]


def _stack(parts):
    return parts[0] if len(parts) == 1 else jnp.concatenate(parts, axis=0)


def _stacked_halves(x, lo):
    z = jnp.zeros_like(x)
    return jnp.concatenate([jnp.where(lo, x, z), jnp.where(lo, z, x)], axis=0)


def _fox_qk_bound(qst_r, km_ref, t):
    km = km_ref[...]
    out = []
    for j in (0, 1):
        qf = qst_r[j * t:(j + 1) * t, :].astype(F32)
        qn = jnp.sqrt(jnp.sum(qf * qf, axis=1, keepdims=True))
        out.append(qn * jnp.sqrt(km[:, j * HEAD_DIM:j * HEAD_DIM + 1]) * 1.001 + 1.0)
    return out


def _fox_trips(nr, hp, flast_ref, qkb, fq, level):
    def trips(carries, base):
        gap = []
        for j in (0, 1):
            worst = None
            for r in range(nr):
                g = qkb[r][j] + fq[r][j] - level(carries, r, j)
                worst = g if worst is None else jnp.maximum(worst, g)
            gap.append(jnp.max(worst))

        def needed(n):
            kb = jnp.maximum(base - 1 - n, 0)
            return jnp.logical_or(gap[0] - flast_ref[2 * hp, kb] > -EXP_ZERO,
                                  gap[1] - flast_ref[2 * hp + 1, kb] > -EXP_ZERO)

        return lax.while_loop(lambda n: jnp.logical_and(n < base, needed(n)), lambda n: n + 1, jnp.int32(0))
    return trips


def _fox_fwd(q, k, v, f_col, f_row, kmax, name, ride=None):
    s_len = q.shape[0]
    t, nr, nq = _att_tiling(s_len)

    def body(q_ref, k_ref, v_ref, f_ref, ft_ref, km_ref, fl_ref, y_ref, lse_ref):
        hp = pl.program_id(0)
        i = pl.program_id(1)
        lane = lax.broadcasted_iota(jnp.int32, (t, LANES), 1)
        lo = lane < HEAD_DIM
        causal = lax.broadcasted_iota(jnp.int32, (t, t), 0) >= lax.broadcasted_iota(jnp.int32, (t, t), 1)
        rows = [pl.ds(r * t, t) for r in range(nr)]
        qst = [_stacked_halves(q_ref[rw, :] * jnp.asarray(HEAD_DIM ** -0.5, BF16), lo) for rw in rows]
        q_all = _stack(qst)
        fq = [[jnp.sum(jnp.where(lane == 2 * hp + j, f_ref[rw, :], 0.0), axis=1, keepdims=True) for j in (0, 1)]
              for rw in rows]

        def load(kb):
            k0 = pl.multiple_of(kb * t, t)
            frow = [ft_ref[pl.ds(2 * hp + j, 1), pl.ds(k0, t)] for j in (0, 1)]
            return k_ref[pl.ds(k0, t), :], v_ref[pl.ds(k0, t), :], frow

        def sub(rs, tiles, carries, masked):
            kblk, vblk, frow = tiles
            z = _dot_nt(q_all if len(rs) == nr else qst[rs[0]], kblk)
            ps, stats = [], []
            for n, (r, j) in enumerate((r, j) for r in range(len(rs)) for j in (0, 1)):
                m, l, _ = carries[r]
                s = z[n * t:(n + 1) * t, :] + (fq[rs[r]][j] - frow[j])
                if masked:
                    s = jnp.where(causal, s, -1e30)
                mj = jnp.maximum(m[j], jnp.max(s, axis=1, keepdims=True))
                aj = jnp.exp(m[j] - mj)
                p = jnp.exp(s - mj)
                stats.append((mj, aj, aj * l[j] + jnp.sum(p, axis=1, keepdims=True)))
                ps.append(p.astype(BF16))
            pv = _dot(_stack(ps), vblk)
            out = []
            for r in range(len(rs)):
                (m0, a0, l0), (m1, a1, l1) = stats[2 * r], stats[2 * r + 1]
                acc = carries[r][2]
                acc = (acc[0] * a0 + pv[2 * r * t:(2 * r + 1) * t, :], acc[1] * a1 + pv[(2 * r + 1) * t:(2 * r + 2) * t, :])
                out.append(((m0, m1), (l0, l1), acc))
            return out, None

        neg = jnp.full((t, 1), -1e30, F32)
        zero = jnp.zeros((t, 1), F32)
        zacc = jnp.zeros((t, LANES), F32)
        init = [((neg, neg), (zero, zero), (zacc, zacc))] * nr
        qkb = [_fox_qk_bound(qs, km_ref, t) for qs in qst]
        trips = _fox_trips(nr, hp, fl_ref, qkb, fq, lambda carries, r, j: carries[r][0][j])
        out = _walk_tiles(i, nr, load, sub, lambda kb, side: None, init, trips=trips)
        for rw, (m, l, acc) in zip(rows, out):
            y_ref[rw, :] = jnp.where(lo, acc[0] / l[0], acc[1] / l[1]).astype(BF16)
            lse_ref[0, rw, :] = jnp.where(lo, m[0] + jnp.log(l[0]), m[1] + jnp.log(l[1]))

    qblk, kvfull = _pair_specs(s_len, t * nr)
    return _ride_call(
        body, name, (N_HEADS // 2, nq),
        [qblk, kvfull, kvfull,
         pl.BlockSpec((t * nr, LANES), lambda hp, i: (i, 0)),
         pl.BlockSpec((N_HEADS, s_len), lambda hp, i: (0, 0)),
         pl.BlockSpec((1, LANES), lambda hp, i: (0, hp)),
         pl.BlockSpec(memory_space=pltpu.SMEM)],
        [qblk, pl.BlockSpec((1, t * nr, LANES), lambda hp, i: (hp, i, 0))],
        [jax.ShapeDtypeStruct((s_len, ATT_W), BF16), jax.ShapeDtypeStruct((N_HEADS // 2, s_len, LANES), F32)],
        [], ("parallel", "parallel"), (q, k, v, f_col, f_row, kmax, f_row[:, t - 1::t]), ride)


def _fox_bwd(q, k, v, dy, y, lse, f_col, f_row, kmax, name, ride=None):
    s_len = q.shape[0]
    t, nr, nq = _att_tiling(s_len)

    def body(q_ref, k_ref, v_ref, dy_ref, y_ref, lse_ref, f_ref, ft_ref, km_ref, fl_ref,
             dq_ref, dk_ref, dv_ref, dft_ref):
        hp = pl.program_id(0)
        i = pl.program_id(1)

        @pl.when(i == 0)
        def _():
            dk_ref[...] = jnp.zeros_like(dk_ref)
            dv_ref[...] = jnp.zeros_like(dv_ref)
            dft_ref[...] = jnp.zeros_like(dft_ref)

        lane = lax.broadcasted_iota(jnp.int32, (t, LANES), 1)
        lo = lane < HEAD_DIM
        causal = lax.broadcasted_iota(jnp.int32, (t, t), 0) >= lax.broadcasted_iota(jnp.int32, (t, t), 1)
        rows = [pl.ds(r * t, t) for r in range(nr)]
        qst, dyst, delta, lse, fq = [], [], [], [], []
        for rw in rows:
            qst.append(_stacked_halves(q_ref[rw, :] * jnp.asarray(HEAD_DIM ** -0.5, BF16), lo))
            dyb = dy_ref[rw, :]
            dyst.append(_stacked_halves(dyb, lo))
            prod = dyb.astype(F32) * y_ref[rw, :].astype(F32)
            delta.append([jnp.sum(jnp.where(lo, prod, 0.0), axis=1, keepdims=True),
                          jnp.sum(jnp.where(lo, 0.0, prod), axis=1, keepdims=True)])
            lse_b = lse_ref[0, rw, :]
            lse.append([lse_b[:, 0:1], lse_b[:, HEAD_DIM:HEAD_DIM + 1]])
            fq.append([jnp.sum(jnp.where(lane == 2 * hp + j, f_ref[rw, :], 0.0), axis=1, keepdims=True)
                       for j in (0, 1)])

        q_all, dy_all = _stack(qst), _stack(dyst)

        def load(kb):
            k0 = pl.multiple_of(kb * t, t)
            frow = [ft_ref[pl.ds(2 * hp + j, 1), pl.ds(k0, t)] for j in (0, 1)]
            return k_ref[pl.ds(k0, t), :], v_ref[pl.ds(k0, t), :], frow

        def sub(rs, tiles, carries, masked):
            kblk, vblk, frow = tiles
            qs, dys = (q_all, dy_all) if len(rs) == nr else (qst[rs[0]], dyst[rs[0]])
            z = _dot_nt(qs, kblk)
            dp = _dot_nt(dys, vblk)
            pb, dsb, rsum, col = [], [], [], [None, None]
            for n, (r, j) in enumerate((r, j) for r in range(len(rs)) for j in (0, 1)):
                sl = slice(n * t, (n + 1) * t)
                s = z[sl, :] + (fq[rs[r]][j] - frow[j])
                p = jnp.exp(s - lse[rs[r]][j])
                if masked:
                    p = jnp.where(causal, p, 0.0)
                ds = p * (dp[sl, :] - delta[rs[r]][j])
                c = jnp.sum(ds, axis=0, keepdims=True)
                col[j] = c if col[j] is None else col[j] + c
                rsum.append(carries[r][1][j] + jnp.sum(ds, axis=1, keepdims=True))
                pb.append(p.astype(BF16))
                dsb.append(ds.astype(BF16))
            p_all, ds_all = _stack(pb), _stack(dsb)
            dqs = _dot(ds_all, kblk)
            out = []
            for r in range(len(rs)):
                dq = carries[r][0]
                dq = (dq[0] + dqs[2 * r * t:(2 * r + 1) * t, :], dq[1] + dqs[(2 * r + 1) * t:(2 * r + 2) * t, :])
                out.append((dq, (rsum[2 * r], rsum[2 * r + 1])))
            return out, (_dot_tn(ds_all, qs), _dot_tn(p_all, dys), col)

        def flush(kb, side):
            k0 = pl.multiple_of(kb * t, t)
            dk_ref[pl.ds(k0, t), :] += side[0]
            dv_ref[pl.ds(k0, t), :] += side[1]
            for j in (0, 1):
                dft_ref[0, pl.ds(j, 1), pl.ds(k0, t)] -= side[2][j]

        zero = jnp.zeros((t, 1), F32)
        zacc = jnp.zeros((t, LANES), F32)
        qkb = [_fox_qk_bound(qs, km_ref, t) for qs in qst]
        trips = _fox_trips(nr, hp, fl_ref, qkb, fq, lambda carries, r, j: lse[r][j])
        out = _walk_tiles(i, nr, load, sub, flush, [((zacc, zacc), (zero, zero))] * nr, trips=trips)
        for r, (rw, (dq, rs)) in enumerate(zip(rows, out)):
            dq_ref[rw, :] = jnp.where(lo, dq[0], dq[1]) * (HEAD_DIM ** -0.5)
            rs_t = jnp.where(lo, rs[0], rs[1]).T
            q0 = pl.multiple_of((i * nr + r) * t, t)
            for j in (0, 1):
                dft_ref[0, pl.ds(j, 1), pl.ds(q0, t)] += rs_t[j * HEAD_DIM:j * HEAD_DIM + 1, :]

    qblk, kvfull = _pair_specs(s_len, t * nr)
    return _ride_call(
        body, name, (N_HEADS // 2, nq),
        [qblk, kvfull, kvfull, qblk, qblk,
         pl.BlockSpec((1, t * nr, LANES), lambda hp, i: (hp, i, 0)),
         pl.BlockSpec((t * nr, LANES), lambda hp, i: (i, 0)),
         pl.BlockSpec((N_HEADS, s_len), lambda hp, i: (0, 0)),
         pl.BlockSpec((1, LANES), lambda hp, i: (0, hp)),
         pl.BlockSpec(memory_space=pltpu.SMEM)],
        [qblk, kvfull, kvfull, pl.BlockSpec((1, 8, s_len), lambda hp, i: (hp, 0, 0))],
        [jax.ShapeDtypeStruct((s_len, ATT_W), F32)] * 3 + [jax.ShapeDtypeStruct((N_HEADS // 2, 8, s_len), F32)],
        [], ("arbitrary", "arbitrary"), (q, k, v, dy, y, lse, f_col, f_row, kmax, f_row[:, t - 1::t]), ride)


def _sb_more(carries, kb):
    worst = None
    for cr in carries:
        for cj in cr[0]:
            worst = cj if worst is None else jnp.maximum(worst, cj)
    return (jnp.max(worst) > -EXP_ZERO).astype(jnp.int32)


def _stacked_split_dot(slabs, m, parts):
    split = [_split(x, parts) for x in slabs]
    acc = None
    for p in range(parts):
        d = _dot(_stack([s[p] for s in split]), m)
        acc = d if acc is None else acc + d
    return acc


def _sb_weights(z, c, strict, upper, t):
    logs = []
    for n in range(z.shape[0] // t):
        zn = z[n * t:(n + 1) * t, :]
        sp = _softplus_neg_abs(zn)
        l1m = jnp.minimum(-zn, 0.0) - sp
        if strict is not None:
            l1m = jnp.where(strict, l1m, 0.0)
        logs.append((jnp.minimum(zn, 0.0) - sp, l1m))
    suf = _stacked_split_dot([l1m for _, l1m in logs], upper, 2)
    out = []
    for n, (logb, l1m) in enumerate(logs):
        after = c[n] + suf[n * t:(n + 1) * t, :]
        a = jnp.exp(logb + after)
        if strict is not None:
            a = jnp.where(strict, a, 0.0)
        out.append((logb, a, after[:, 0:1] + l1m[:, 0:1]))
    return out


def _sb_fwd(q, k, v, name):
    s_len = q.shape[0]
    t, nr, nq = _att_tiling(s_len)

    def body(q_ref, k_ref, v_ref, y_ref, yf_ref):
        i = pl.program_id(1)
        lane = lax.broadcasted_iota(jnp.int32, (t, LANES), 1)
        lo = lane < HEAD_DIM
        ri = lax.broadcasted_iota(jnp.int32, (t, t), 0)
        ci = lax.broadcasted_iota(jnp.int32, (t, t), 1)
        strict = ci < ri
        upper = (ri > ci).astype(BF16)
        rows = [pl.ds(r * t, t) for r in range(nr)]
        qst = [_stacked_halves(q_ref[rw, :] * jnp.asarray(HEAD_DIM ** -0.5, BF16), lo) for rw in rows]
        q_all = _stack(qst)

        def load(kb):
            k0 = pl.multiple_of(kb * t, t)
            return k_ref[pl.ds(k0, t), :], v_ref[pl.ds(k0, t), :]

        def sub(rs, tiles, carries, masked):
            kblk, vblk = tiles
            z = _dot_nt(q_all if len(rs) == nr else qst[rs[0]], kblk)
            c = [carries[r][0][j] for r in range(len(rs)) for j in (0, 1)]
            w = _sb_weights(z, c, strict if masked else None, upper, t)
            pv = _dot(_stack([a.astype(BF16) for _, a, _ in w]), vblk)
            out = []
            for r in range(len(rs)):
                acc = carries[r][1]
                acc = (acc[0] + pv[2 * r * t:(2 * r + 1) * t, :], acc[1] + pv[(2 * r + 1) * t:(2 * r + 2) * t, :])
                out.append(((w[2 * r][2], w[2 * r + 1][2]), acc))
            return out, None

        zero = jnp.zeros((t, 1), F32)
        zacc = jnp.zeros((t, LANES), F32)
        out = _walk_tiles(i, nr, load, sub, lambda kb, side: None, [((zero, zero), (zacc, zacc))] * nr, more=_sb_more)
        for rw, (_, acc) in zip(rows, out):
            y = jnp.where(lo, acc[0], acc[1])
            y_ref[rw, :] = y.astype(BF16)
            yf_ref[rw, :] = y

    qblk, kvfull = _pair_specs(s_len, t * nr)
    return pl.pallas_call(
        body, name=name, grid=(N_HEADS // 2, nq),
        in_specs=[qblk, kvfull, kvfull],
        out_specs=[qblk, qblk],
        out_shape=[jax.ShapeDtypeStruct((s_len, ATT_W), BF16), jax.ShapeDtypeStruct((s_len, ATT_W), F32)],
        compiler_params=_cparams("parallel", "parallel"),
    )(q, k, v)


def _sb_bwd(q, k, v, dy, yf, name):
    s_len = q.shape[0]
    t, nr, nq = _att_tiling(s_len)

    def body(q_ref, k_ref, v_ref, dy_ref, yf_ref, dq_ref, dk_ref, dv_ref):
        i = pl.program_id(1)

        @pl.when(i == 0)
        def _():
            dk_ref[...] = jnp.zeros_like(dk_ref)
            dv_ref[...] = jnp.zeros_like(dv_ref)

        lane = lax.broadcasted_iota(jnp.int32, (t, LANES), 1)
        lo = lane < HEAD_DIM
        ri = lax.broadcasted_iota(jnp.int32, (t, t), 0)
        ci = lax.broadcasted_iota(jnp.int32, (t, t), 1)
        strict = ci < ri
        upper = (ri > ci).astype(BF16)
        upper_incl = (ri >= ci).astype(BF16)
        rows = [pl.ds(r * t, t) for r in range(nr)]
        qst, dyst, delta = [], [], []
        for rw in rows:
            qst.append(_stacked_halves(q_ref[rw, :] * jnp.asarray(HEAD_DIM ** -0.5, BF16), lo))
            dyb = dy_ref[rw, :]
            dyst.append(_stacked_halves(dyb, lo))
            prod = dyb.astype(F32) * yf_ref[rw, :]
            delta.append([jnp.sum(jnp.where(lo, prod, 0.0), axis=1, keepdims=True),
                          jnp.sum(jnp.where(lo, 0.0, prod), axis=1, keepdims=True)])
        q_all, dy_all = _stack(qst), _stack(dyst)

        def load(kb):
            k0 = pl.multiple_of(kb * t, t)
            return k_ref[pl.ds(k0, t), :], v_ref[pl.ds(k0, t), :]

        def sub(rs, tiles, carries, masked):
            kblk, vblk = tiles
            qs, dys = (q_all, dy_all) if len(rs) == nr else (qst[rs[0]], dyst[rs[0]])
            slabs = [(r, j) for r in range(len(rs)) for j in (0, 1)]
            z = _dot_nt(qs, kblk)
            w = _sb_weights(z, [carries[r][0][j] for r, j in slabs], strict if masked else None, upper, t)
            da = _dot_nt(dys, vblk)
            ab = [a.astype(BF16) for _, a, _ in w]
            dl = [ab[n].astype(F32) * da[n * t:(n + 1) * t, :] for n in range(len(slabs))]
            tail = _stacked_split_dot(dl, upper_incl, 2)
            dzb, e_new = [], []
            for n, (r, j) in enumerate(slabs):
                tl = tail[n * t:(n + 1) * t, :]
                e = carries[r][1][j]
                dl1m = (delta[rs[r]][j] - e) - tl
                e_new.append(e + tl[:, 0:1])
                dz = dl[n] - jnp.exp(w[n][0]) * (dl[n] + dl1m)
                if masked:
                    dz = jnp.where(strict, dz, 0.0)
                dzb.append(dz.astype(BF16))
            a_all, dz_all = _stack(ab), _stack(dzb)
            dqs = _dot(dz_all, kblk)
            out = []
            for r in range(len(rs)):
                dq = carries[r][2]
                dq = (dq[0] + dqs[2 * r * t:(2 * r + 1) * t, :], dq[1] + dqs[(2 * r + 1) * t:(2 * r + 2) * t, :])
                out.append(((w[2 * r][2], w[2 * r + 1][2]), (e_new[2 * r], e_new[2 * r + 1]), dq))
            return out, (_dot_tn(dz_all, qs), _dot_tn(a_all, dys))

        def flush(kb, side):
            k0 = pl.multiple_of(kb * t, t)
            dk_ref[pl.ds(k0, t), :] += side[0]
            dv_ref[pl.ds(k0, t), :] += side[1]

        zero = jnp.zeros((t, 1), F32)
        zacc = jnp.zeros((t, LANES), F32)
        out = _walk_tiles(i, nr, load, sub, flush, [((zero, zero), (zero, zero), (zacc, zacc))] * nr, more=_sb_more)
        for rw, (_, _, dq) in zip(rows, out):
            dq_ref[rw, :] = jnp.where(lo, dq[0], dq[1]) * (HEAD_DIM ** -0.5)

    qblk, kvfull = _pair_specs(s_len, t * nr)
    return pl.pallas_call(
        body, name=name, grid=(N_HEADS // 2, nq),
        in_specs=[qblk, kvfull, kvfull, qblk, qblk],
        out_specs=[qblk, kvfull, kvfull],
        out_shape=[jax.ShapeDtypeStruct((s_len, ATT_W), F32)] * 3,
        compiler_params=_cparams("arbitrary", "arbitrary"),
    )(q, k, v, dy, yf)


def _merge_fwd(x, yf, ys, gf, gs, wbf, wbs, wo, name):
    s_len = x.shape[0]
    ts = min(512, s_len)

    def body(x_ref, yf_ref, ys_ref, gf_ref, gs_ref, wbf_ref, wbs_ref, wo_ref, o_ref):
        merged = (_sigmoid(gf_ref[...]) * _dot_nt(yf_ref[...], wbf_ref[...])
                  + _sigmoid(gs_ref[...]) * _dot_nt(ys_ref[...], wbs_ref[...]))
        o_ref[...] = x_ref[...] + _dot(merged.astype(BF16), wo_ref[...])

    tok = lambda w: pl.BlockSpec((ts, w), lambda i: (i, 0))
    full = lambda a: pl.BlockSpec(a.shape, lambda i: (0, 0))
    return pl.pallas_call(
        body, name=name, grid=(s_len // ts,),
        in_specs=[tok(D_MODEL), tok(ATT_W), tok(ATT_W), tok(D_MODEL), tok(D_MODEL), full(wbf), full(wbs), full(wo)],
        out_specs=tok(D_MODEL),
        out_shape=jax.ShapeDtypeStruct((s_len, D_MODEL), F32),
        compiler_params=_cparams("parallel"),
    )(x, yf, ys, gf, gs, wbf, wbs, wo)


def _merge_bwd(dx, yf, ys, gf, gs, wbf, wbs, wo, name):
    s_len = dx.shape[0]
    ts = min(512, s_len)

    def body(dx_ref, yf_ref, ys_ref, gf_ref, gs_ref, wbf_ref, wbs_ref, wo_ref,
             dyf_ref, dys_ref, dgf_ref, dgs_ref, dbf_ref, dbs_ref, mg_ref):
        bf = _dot_nt(yf_ref[...], wbf_ref[...])
        bs = _dot_nt(ys_ref[...], wbs_ref[...])
        sf = _sigmoid(gf_ref[...])
        ss = _sigmoid(gs_ref[...])
        mg_ref[...] = (sf * bf + ss * bs).astype(BF16)
        dm = _dot_nt(dx_ref[...].astype(BF16), wo_ref[...])
        dbf = (dm * sf).astype(BF16)
        dbs = (dm * ss).astype(BF16)
        dbf_ref[...] = dbf
        dbs_ref[...] = dbs
        dgf_ref[...] = (dm * bf * (sf * (1.0 - sf))).astype(BF16)
        dgs_ref[...] = (dm * bs * (ss * (1.0 - ss))).astype(BF16)
        dyf_ref[...] = _dot(dbf, wbf_ref[...]).astype(BF16)
        dys_ref[...] = _dot(dbs, wbs_ref[...]).astype(BF16)

    tok = lambda w: pl.BlockSpec((ts, w), lambda i: (i, 0))
    full = lambda a: pl.BlockSpec(a.shape, lambda i: (0, 0))
    b16o = lambda w: jax.ShapeDtypeStruct((s_len, w), BF16)
    return pl.pallas_call(
        body, name=name, grid=(s_len // ts,),
        in_specs=[tok(D_MODEL), tok(ATT_W), tok(ATT_W), tok(D_MODEL), tok(D_MODEL), full(wbf), full(wbs), full(wo)],
        out_specs=[tok(ATT_W), tok(ATT_W)] + [tok(D_MODEL)] * 5,
        out_shape=[b16o(ATT_W), b16o(ATT_W)] + [b16o(D_MODEL)] * 5,
        compiler_params=_cparams("parallel"),
    )(dx, yf, ys, gf, gs, wbf, wbs, wo)


def _mix_bwd(x, dx_in, gain, w_in, fqr, fkr, dfqn, dfkn, qg, kg, dfv, df_col, logf, dsq, dsk, dsv, dgf, dgs, name,
             ride=None):
    s_len = x.shape[0]
    ts = min(256, s_len)
    nt = s_len // ts
    gmat = _head_group_matrix()

    def body(x_ref, dxi_ref, gain_ref, w_ref, fqr_ref, fkr_ref, dfqn_ref, dfkn_ref, qg_ref, kg_ref, gm_ref,
             dfv_ref, df_ref, logf_ref, dsq_ref, dsk_ref, dsv_ref, dgf_ref, dgs_ref,
             dp_ref, dx_ref, dgain_ref, dqg_ref, dkg_ref, dbias_ref, carry):
        i = pl.program_id(0)

        @pl.when(i == 0)
        def _():
            carry[...] = jnp.zeros_like(carry)
            dgain_ref[...] = jnp.zeros_like(dgain_ref)
            dqg_ref[...] = jnp.zeros_like(dqg_ref)
            dkg_ref[...] = jnp.zeros_like(dkg_ref)
            dbias_ref[...] = jnp.zeros_like(dbias_ref)

        gm = gm_ref[...]

        def headnorm_bwd(raw, dout, g, dg_ref):
            ms = _dot_split(raw * raw, gm, HEAD_SUM_PARTS) * (1.0 / HEAD_DIM)
            r = lax.rsqrt(ms + EPS)
            nrm = raw * r
            dg_ref[...] += jnp.sum(dout * nrm, axis=0, keepdims=True)
            dn = dout * g
            mean_h = _dot_split(dn * nrm, gm, HEAD_SUM_PARTS) * (1.0 / HEAD_DIM)
            return r * (dn - nrm * mean_h)

        dp_ref[:, C_FQ:C_FQ + ATT_W] = headnorm_bwd(fqr_ref[...], dfqn_ref[...], qg_ref[...], dqg_ref).astype(BF16)
        dp_ref[:, C_FK:C_FK + ATT_W] = headnorm_bwd(fkr_ref[...], dfkn_ref[...], kg_ref[...], dkg_ref).astype(BF16)
        dp_ref[:, C_FV:C_FV + ATT_W] = dfv_ref[...].astype(BF16)
        dp_ref[:, C_SQ:C_SQ + ATT_W] = dsq_ref[...].astype(BF16)
        dp_ref[:, C_SK:C_SK + ATT_W] = dsk_ref[...].astype(BF16)
        dp_ref[:, C_SV:C_SV + ATT_W] = dsv_ref[...].astype(BF16)
        dp_ref[:, C_GF:C_GF + D_MODEL] = dgf_ref[...]
        dp_ref[:, C_GS:C_GS + D_MODEL] = dgs_ref[...]

        r_ = lax.broadcasted_iota(jnp.int32, (ts, ts), 0)
        c_ = lax.broadcasted_iota(jnp.int32, (ts, ts), 1)
        rev = (c_ >= r_).astype(BF16)
        dlogf = _dot_split_left(rev, df_ref[...], 3) + carry[...]
        carry[...] = dlogf[0:1, :]
        lane = lax.broadcasted_iota(jnp.int32, (ts, LANES), 1)
        dfl = jnp.where(lane < N_HEADS, dlogf * (1.0 - jnp.exp(logf_ref[...])), 0.0)
        dbias_ref[...] += jnp.sum(dfl, axis=0, keepdims=True)
        dp_ref[:, C_FL:C_FL + LANES] = dfl.astype(BF16)
        dp_ref[:, C_FL + LANES:C_SQ] = jnp.zeros((ts, C_SQ - C_FL - LANES), BF16)

        dh = _dot(dp_ref[...], w_ref[...])
        xf = x_ref[...]
        r = _rms_rinv(xf)
        xhat = xf * r
        dgain_ref[...] += jnp.sum(dh * xhat, axis=0, keepdims=True)
        dn = dh * gain_ref[...]
        dx_ref[...] = dxi_ref[...] + r * (dn - xhat * jnp.mean(dn * xhat, axis=-1, keepdims=True))

    tok = lambda w: pl.BlockSpec((ts, w), lambda i: (nt - 1 - i, 0))
    full = lambda a: pl.BlockSpec(a.shape, lambda i: (0, 0))
    row = lambda w: pl.BlockSpec((1, w), lambda i: (0, 0))
    return _ride_call(
        body, name, (nt,),
        [tok(D_MODEL), tok(D_MODEL), full(gain), full(w_in), tok(ATT_W), tok(ATT_W), tok(ATT_W), tok(ATT_W),
         full(qg), full(kg), full(gmat), tok(ATT_W), tok(LANES), tok(LANES), tok(ATT_W), tok(ATT_W), tok(ATT_W),
         tok(D_MODEL), tok(D_MODEL)],
        [tok(IN_PAD), tok(D_MODEL), row(D_MODEL), row(ATT_W), row(ATT_W), row(LANES)],
        [jax.ShapeDtypeStruct((s_len, IN_PAD), BF16), jax.ShapeDtypeStruct((s_len, D_MODEL), F32),
         jax.ShapeDtypeStruct((1, D_MODEL), F32), jax.ShapeDtypeStruct((1, ATT_W), F32),
         jax.ShapeDtypeStruct((1, ATT_W), F32), jax.ShapeDtypeStruct((1, LANES), F32)],
        [pltpu.VMEM((1, LANES), F32)], ("arbitrary",),
        (x, dx_in, gain, w_in, fqr, fkr, dfqn, dfkn, qg, kg, gmat, dfv, df_col, logf, dsq, dsk, dsv, dgf, dgs), ride)


def _ple_loss(x, p, tgt, gain, wpg, wpp, name):
    s_len = x.shape[0]
    ts = min(512, s_len)

    def body(x_ref, p_ref, t_ref, gain_ref, wpg_ref, wpp_ref, dx_ref, n_ref, ds_ref, dpp_ref, dgain_ref, loss_ref):
        i = pl.program_id(0)

        @pl.when(i == 0)
        def _():
            dgain_ref[...] = jnp.zeros_like(dgain_ref)
            loss_ref[...] = jnp.zeros_like(loss_ref)

        xf = x_ref[...]
        r = _rms_rinv(xf)
        n = xf * r
        hn = (n * gain_ref[...]).astype(BF16)
        n_ref[...] = hn
        sg = _sigmoid(_dot(hn, wpg_ref[...]))
        pp = _dot_nt(p_ref[...].astype(BF16), wpp_ref[...])
        err = (xf + sg * pp) - t_ref[...]
        sq = jnp.sum(jnp.sum(err * err, axis=1, keepdims=True), axis=0, keepdims=True)
        loss_ref[...] += (0.5 / D_MODEL) * sq
        dout = err * (1.0 / D_MODEL)
        dpp_ref[...] = (dout * sg).astype(BF16)
        ds = (dout * pp * (sg * (1.0 - sg))).astype(BF16)
        ds_ref[...] = ds
        dhn = _dot_nt(ds, wpg_ref[...])
        dgain_ref[...] += jnp.sum(dhn * n, axis=0, keepdims=True)
        dn = dhn * gain_ref[...]
        dx_ref[...] = dout + r * (dn - n * jnp.mean(dn * n, axis=-1, keepdims=True))

    tok = lambda w: pl.BlockSpec((ts, w), lambda i: (i, 0))
    full = lambda a: pl.BlockSpec(a.shape, lambda i: (0, 0))
    return pl.pallas_call(
        body, name=name, grid=(s_len // ts,),
        in_specs=[tok(D_MODEL), tok(PLE_DIM), tok(D_MODEL), full(gain), full(wpg), full(wpp)],
        out_specs=[tok(D_MODEL), tok(D_MODEL), tok(D_MODEL), tok(D_MODEL),
                   pl.BlockSpec((1, D_MODEL), lambda i: (0, 0)), pl.BlockSpec((8, LANES), lambda i: (0, 0))],
        out_shape=[jax.ShapeDtypeStruct((s_len, D_MODEL), F32), jax.ShapeDtypeStruct((s_len, D_MODEL), BF16),
                   jax.ShapeDtypeStruct((s_len, D_MODEL), BF16), jax.ShapeDtypeStruct((s_len, D_MODEL), BF16),
                   jax.ShapeDtypeStruct((1, D_MODEL), F32), jax.ShapeDtypeStruct((8, LANES), F32)],
        compiler_params=_cparams("arbitrary"),
    )(x, p, tgt, gain, wpg, wpp)


def _exchange(x, name, broadcast):
    def body(x_ref, out_ref, send_sems, recv_sems, local_sem):
        _exchange_start(x_ref, out_ref, send_sems, recv_sems, local_sem, broadcast)
        _exchange_wait(x_ref, out_ref, send_sems, recv_sems, local_sem, broadcast)

    return pl.pallas_call(
        body, name=name,
        in_specs=[EXCHANGE_SPEC],
        out_specs=EXCHANGE_SPEC,
        out_shape=_exchange_shape(x, broadcast),
        scratch_shapes=list(EXCHANGE_SEMS),
        compiler_params=pltpu.CompilerParams(has_side_effects=True),
    )(x)


def _gather_two_level(x, name):
    def body(x_ref, out_ref, send_sems, recv_sems, local_sem):
        mx, my, mc = lax.axis_index("x"), lax.axis_index("y"), lax.axis_index("c")
        me, sibling = (mx, my, mc), (mx, my, 1 - mc)
        chips = [(1 - mx, my), (mx, 1 - my), (1 - mx, 1 - my)]

        def slot(px, py, pc):
            return out_ref.at[4 * px + 2 * py + pc]

        def copy(k, block, to, src=None):
            return pltpu.make_async_remote_copy(
                src_ref=slot(*block) if src is None else src, dst_ref=slot(*block),
                send_sem=send_sems.at[k], recv_sem=recv_sems.at[k], device_id=to, device_id_type=MESH)

        mine = pltpu.make_async_copy(x_ref, slot(*me), local_sem)
        mine.start()
        first = [copy(0, me, sibling, src=x_ref)]
        first += [copy(1 + j, me, (*chip, mc), src=x_ref) for j, chip in enumerate(chips)]
        for cp in first:
            cp.start()
        passed = [copy(4 + j, (*chip, mc), sibling) for j, chip in enumerate(chips)]
        for j, chip in enumerate(chips):
            copy(1 + j, (*chip, mc), me).wait_recv()
            passed[j].start()
        copy(0, sibling, me).wait_recv()
        for j, chip in enumerate(chips):
            copy(4 + j, (*chip, 1 - mc), me).wait_recv()
        for cp in first + passed:
            cp.wait_send()
        mine.wait()

    return pl.pallas_call(
        body, name=name,
        in_specs=[EXCHANGE_SPEC],
        out_specs=EXCHANGE_SPEC,
        out_shape=_exchange_shape(x, True),
        scratch_shapes=list(EXCHANGE_SEMS),
        compiler_params=pltpu.CompilerParams(has_side_effects=True),
    )(x)


EXCHANGE_SPEC = pl.BlockSpec(memory_space=pl.ANY)
EXCHANGE_SEMS = (pltpu.SemaphoreType.DMA((N_DEV - 1,)), pltpu.SemaphoreType.DMA((N_DEV - 1,)), pltpu.SemaphoreType.DMA)


def _exchange_shape(x, broadcast):
    return jax.ShapeDtypeStruct((N_DEV,) + tuple(x.shape if broadcast else x.shape[1:]), x.dtype)


def _exchange_copies(x_ref, out_ref, send_sems, recv_sems, local_sem, broadcast, with_recv=True):
    mx, my, mc = lax.axis_index("x"), lax.axis_index("y"), lax.axis_index("c")
    me = 4 * mx + 2 * my + mc

    def src(idx):
        return x_ref if broadcast else x_ref.at[idx]

    local = pltpu.make_async_copy(src(me), out_ref.at[me], local_sem)
    pairs = []
    for k in range(1, N_DEV):
        px = (1 - mx) if k & 4 else mx
        py = (1 - my) if k & 2 else my
        pc = (1 - mc) if k & 1 else mc
        peer = 4 * px + 2 * py + pc
        sems = dict(send_sem=send_sems.at[k - 1], recv_sem=recv_sems.at[k - 1], device_id=(px, py, pc), device_id_type=MESH)
        recv = pltpu.make_async_remote_copy(src_ref=src(peer), dst_ref=out_ref.at[peer], **sems) if with_recv else None
        pairs.append((pltpu.make_async_remote_copy(src_ref=src(peer), dst_ref=out_ref.at[me], **sems), recv))
    return local, pairs


def _exchange_start(*refs_and_mode):
    local, pairs = _exchange_copies(*refs_and_mode, with_recv=False)
    local.start()
    for send, _ in pairs:
        send.start()


def _exchange_wait(*refs_and_mode):
    local, pairs = _exchange_copies(*refs_and_mode)
    for _, recv in pairs:
        recv.wait_recv()
    for send, _ in pairs:
        send.wait_send()
    local.wait()


def _riding(body, grid, n_in, n_out, ride):
    if ride is None:
        return body
    broadcast = ride[1]

    def wrapped(*refs):
        ins, x_ref = refs[:n_in], refs[n_in]
        outs, out_ref = refs[n_in + 1:n_in + 1 + n_out], refs[n_in + 1 + n_out]
        scratch, sems = refs[n_in + 2 + n_out:-3], refs[-3:]
        step = pl.program_id(0)
        for d in range(1, len(grid)):
            step = step * grid[d] + pl.program_id(d)
        total = 1
        for g in grid:
            total *= g

        @pl.when(step == 0)
        def _():
            _exchange_start(x_ref, out_ref, *sems, broadcast)

        body(*ins, *outs, *scratch)

        @pl.when(step == total - 1)
        def _():
            _exchange_wait(x_ref, out_ref, *sems, broadcast)

    return wrapped


def _ride_call(body, name, grid, in_specs, out_specs, out_shape, scratch_shapes, sem, operands, ride):
    if ride is None:
        return pl.pallas_call(body, name=name, grid=grid, in_specs=in_specs, out_specs=out_specs, out_shape=out_shape,
                              scratch_shapes=scratch_shapes, compiler_params=_cparams(*sem))(*operands)
    return pl.pallas_call(
        _riding(body, grid, len(in_specs), len(out_specs), ride), name=name, grid=grid,
        in_specs=list(in_specs) + [EXCHANGE_SPEC], out_specs=list(out_specs) + [EXCHANGE_SPEC],
        out_shape=list(out_shape) + [_exchange_shape(*ride)],
        scratch_shapes=list(scratch_shapes) + list(EXCHANGE_SEMS),
        compiler_params=_cparams(*(["arbitrary"] * len(grid))),
    )(*operands, ride[0])


def _adamw_math(w, g, m, v):
    m2 = ADAM_B1 * m + (1.0 - ADAM_B1) * g
    v2 = ADAM_B2 * v + (1.0 - ADAM_B2) * (g * g)
    m_hat = m2 / (1.0 - ADAM_B1 ** ADAM_STEP)
    v_hat = v2 / (1.0 - ADAM_B2 ** ADAM_STEP)
    delta = -ADAM_LR * (m_hat / (jnp.sqrt(v_hat) + ADAM_EPS) + ADAM_WD * w)
    return delta, m2, v2


def _sum_parts(parts, name, tr):
    _, rows, cols = parts.shape

    def body(p_ref, g_ref):
        g = p_ref[0].astype(F32)
        for s in range(1, N_DEV):
            g = g + p_ref[s].astype(F32)
        g_ref[...] = g

    return pl.pallas_call(
        body, name=name, grid=(rows // tr,),
        in_specs=[pl.BlockSpec((N_DEV, tr, cols), lambda i: (0, i, 0))],
        out_specs=pl.BlockSpec((tr, cols), lambda i: (i, 0)),
        out_shape=jax.ShapeDtypeStruct((rows, cols), F32),
        compiler_params=_cparams("parallel"),
    )(parts)


ADAM_SPLIT_ELEMS = 400_000


def _adamw_shard(g, w, m, v, name):
    rows, cols = w.shape
    tr = rows // 2 if rows * cols > ADAM_SPLIT_ELEMS else rows

    def body(g_ref, w_ref, m_ref, v_ref, d_ref, m2_ref, v2_ref):
        d_ref[...], m2_ref[...], v2_ref[...] = _adamw_math(w_ref[...], g_ref[...], m_ref[...], v_ref[...])

    blk = pl.BlockSpec((tr, cols), lambda i: (i, 0))
    return pl.pallas_call(
        body, name=name, grid=(rows // tr,),
        in_specs=[blk] * 4, out_specs=[blk] * 3,
        out_shape=[jax.ShapeDtypeStruct((rows, cols), F32)] * 3,
        compiler_params=_cparams("parallel"),
    )(g, w, m, v)


def _adamw(parts, w, m, v, name, tr):
    rows, cols = w.shape

    def body(p_ref, w_ref, m_ref, v_ref, g_ref, d_ref, m2_ref, v2_ref):
        g = p_ref[0].astype(F32)
        for s in range(1, N_DEV):
            g = g + p_ref[s].astype(F32)
        g_ref[...] = g
        d_ref[...], m2_ref[...], v2_ref[...] = _adamw_math(w_ref[...], g, m_ref[...], v_ref[...])

    blk = pl.BlockSpec((tr, cols), lambda i: (i, 0))
    return pl.pallas_call(
        body, name=name, grid=(rows // tr,),
        in_specs=[pl.BlockSpec((N_DEV, tr, cols), lambda i: (0, i, 0)), blk, blk, blk],
        out_specs=[blk] * 4,
        out_shape=[jax.ShapeDtypeStruct((rows, cols), F32)] * 4,
        compiler_params=_cparams("parallel"),
    )(parts, w, m, v)


TRANSPOSED = frozenset(("ffn1_w_gate", "ffn1_w_up", "w_in", "w_branch_fox", "w_branch_sb", "ffn2_w_gate", "ffn2_w_up",
                        "w_ple_proj"))
F_PAD_ROWS = C_SQ - FL_REAL_END


def _pack(pieces, group, dtype):
    out = []
    for name in GATHER_GROUPS[group]:
        r = pieces[name].T if name in TRANSPOSED else pieces[name]
        r = r.reshape(-1, D_MODEL).astype(dtype)
        if r.shape[0] != PACK_ROWS[name]:
            r = jnp.pad(r, ((0, PACK_ROWS[name] - r.shape[0]), (0, 0)))
        out.append(r)
    return jnp.concatenate(out, axis=0)


def _real_rows(name):
    return W_IN_ROWS if name == "w_in" else PACK_ROWS[name]


def _gathered(got, name, shape):
    off = GATHER_OFF[name]
    return got[:, off:off + _real_rows(name), :].reshape(shape)


def _w_in_device_rows(d):
    lo, hi = d * W_IN_ROWS, (d + 1) * W_IN_ROWS
    if hi <= FL_REAL_END:
        return [(lo, hi)]
    if lo >= FL_REAL_END:
        return [(lo + F_PAD_ROWS, hi + F_PAD_ROWS)]
    return [(lo, FL_REAL_END), (C_SQ, hi + F_PAD_ROWS)]


def _w_in_t_padded(got):
    t = _gathered(got, "w_in", (IN_REAL, D_MODEL))
    return jnp.concatenate([t[:FL_REAL_END], jnp.zeros((F_PAD_ROWS, D_MODEL), t.dtype), t[FL_REAL_END:]], axis=0)


def _pack_chunks(grads, group):
    out = []
    for name in SCATTER_GROUPS[group]:
        g = grads[name].astype(BF16)
        if name == "w_in":
            tail = jnp.zeros((PACK_ROWS[name] - W_IN_ROWS, D_MODEL), BF16)
            c = jnp.stack([jnp.concatenate([g[lo:hi] for lo, hi in _w_in_device_rows(d)] + [tail], axis=0)
                           for d in range(N_DEV)])
        else:
            c = g.reshape(N_DEV, PACK_ROWS[name], D_MODEL)
        out.append(c)
    return out[0] if len(out) == 1 else jnp.concatenate(out, axis=1)


def _shard_grad(packed, name, shape):
    off = SCATTER_OFF[name]
    rows = packed[off:off + _real_rows(name), :]
    return rows.reshape(shape[1], shape[0]).T if name in TRANSPOSED else rows.reshape(shape)


WEIGHT_NAMES = ['ffn1_norm', 'ffn1_w_gate', 'ffn1_w_up', 'ffn1_w_down', 'mix_norm', 'w_in', 'forget_bias', 'q_norm',
                'k_norm', 'w_branch_fox', 'w_branch_sb', 'w_out', 'ffn2_norm', 'ffn2_w_gate', 'ffn2_w_up',
                'ffn2_w_down', 'ple_norm', 'w_ple_gate', 'w_ple_proj']
SMALL_NAMES = ('ffn1_norm', 'mix_norm', 'ffn2_norm', 'ple_norm', 'q_norm', 'k_norm', 'forget_bias')
Q_OFF, K_OFF, B_OFF, LOSS_OFF = 0, HEAD_DIM, 2 * HEAD_DIM, 2 * HEAD_DIM + N_HEADS


def _pack_small(vals, loss=None):
    tail = [vals['q_norm'].reshape(1, -1), vals['k_norm'].reshape(1, -1), vals['forget_bias'].reshape(1, -1)]
    used = LOSS_OFF
    if loss is not None:
        tail.append(loss.reshape(1, 1))
        used += 1
    tail.append(jnp.zeros((1, D_MODEL - used), F32))
    rows = [vals[n].reshape(1, D_MODEL) for n in SMALL_NAMES[:4]] + [jnp.concatenate(tail, axis=1)]
    rows.append(jnp.zeros((SMALL_ROWS - len(rows), D_MODEL), F32))
    return jnp.concatenate(rows, axis=0)


def _unpack_small(packed, name, shape):
    if name in SMALL_NAMES[:4]:
        return packed[SMALL_NAMES.index(name)].reshape(shape)
    off, n = {'q_norm': (Q_OFF, HEAD_DIM), 'k_norm': (K_OFF, HEAD_DIM), 'forget_bias': (B_OFF, N_HEADS)}[name]
    return packed[4, off:off + n].reshape(shape)


def _step(x, p, tgt, w):
    row = lambda a: a.reshape(1, -1).astype(F32)
    g_ffn1, g_mix, g_ffn2, g_ple = (row(w[n]) for n in SMALL_NAMES[:4])
    qg = jnp.tile(row(w['q_norm']), (1, N_HEADS))
    kg = jnp.tile(row(w['k_norm']), (1, N_HEADS))
    bias = jnp.pad(row(w['forget_bias']), ((0, 0), (0, LANES - N_HEADS)))
    half = D_FF // 2
    grads = {}

    blk = lambda n: GATHER_OFF[n] // FFN_SHARD
    ffn1 = tuple(blk(n) for n in ("ffn1_w_gate", "ffn1_w_up", "ffn1_w_down"))
    ffn2 = tuple(blk(n) for n in ("ffn2_w_gate", "ffn2_w_up", "ffn2_w_down"))
    got0 = _gather_two_level(_pack(w, 0, BF16), "gather_ffn1")
    x1, g1, u1, h1, got1 = _ffn_fwd(x, g_ffn1, got0, ffn1, "ffn1_fwd", ride=(_pack(w, 1, BF16), True))
    w_in = _w_in_t_padded(got1)
    wbf = _gathered(got1, "w_branch_fox", (D_MODEL, ATT_W))
    wbs = _gathered(got1, "w_branch_sb", (D_MODEL, ATT_W))
    wo = _gathered(got1, "w_out", (D_MODEL, D_MODEL))
    (hmix, fqr, fkr, fqn, fkn, fv, logf, f_col, f_row, sq, sk, sv, gf, gs, kmax) = _mix_fwd(
        x1, g_mix, w_in, bias, qg, kg, "mix_fwd")
    y_fox, lse, got2 = _fox_fwd(fqn, fkn, fv, f_col, f_row, kmax, "fox_fwd", ride=(_pack(w, 2, BF16), True))
    wpg = _gathered(got2, "w_ple_gate", (D_MODEL, D_MODEL))
    wpp = _gathered(got2, "w_ple_proj", (D_MODEL, PLE_DIM))
    y_sb, y_sb32 = _sb_fwd(sq, sk, sv, "sb_fwd")
    x2 = _merge_fwd(x1, y_fox, y_sb, gf, gs, wbf, wbs, wo, "merge_fwd")
    x3, g2, u2, h2, = _ffn_fwd(x2, g_ffn2, got2, ffn2, "ffn2_fwd")
    dx3, n_ple, ds_ple, dpp, dg_ple, loss = _ple_loss(x3, p, tgt, g_ple, wpg, wpp, "ple_loss")

    grads['w_ple_gate'] = _wgrad(n_ple, ds_ple, "dw_ple_gate", D_MODEL, D_MODEL)
    grads['w_ple_proj'] = _wgrad(dpp, p, "dw_ple_proj", D_MODEL, PLE_DIM)
    dg2, du2, act2, dx2, dg_ffn2 = _ffn_bwd_fused(x2, dx3, g_ffn2, g2, u2, got2, ffn2, "ffn2_bwd")
    grads['ffn2_w_gate'] = _wgrad(dg2, h2, "dw_ffn2_gate", half, D_MODEL)
    grads['ffn2_w_up'] = _wgrad(du2, h2, "dw_ffn2_up", half, D_MODEL)
    grads['ffn2_w_down'] = _wgrad(act2, dx3, "dw_ffn2_down", half, D_MODEL)
    dyf, dys, dgf, dgs, dbf, dbs, merged = _merge_bwd(dx2, y_fox, y_sb, gf, gs, wbf, wbs, wo, "merge_bwd")
    grads['w_branch_fox'] = _wgrad(dbf, y_fox, "dw_branch_fox", D_MODEL, ATT_W)
    grads['w_branch_sb'] = _wgrad(dbs, y_sb, "dw_branch_sb", D_MODEL, ATT_W)
    grads['w_out'] = _wgrad(merged, dx2, "dw_out", D_MODEL, D_MODEL)
    dfqn, dfkn, dfv, dft, part2 = _fox_bwd(fqn, fkn, fv, dyf, y_fox, lse, f_col, f_row, kmax, "fox_bwd",
                                                ride=(_pack_chunks(grads, 2), False))
    dsq, dsk, dsv = _sb_bwd(sq, sk, sv, dys, y_sb32, "sb_bwd")
    s_len = x.shape[0]
    df_col = jnp.pad(dft[:, :2, :].reshape(N_HEADS, s_len).T, ((0, 0), (0, LANES - N_HEADS)))
    dproj, dx1, dg_mix, dqg, dkg, dbias = _mix_bwd(
        x1, dx2, g_mix, w_in, fqr, fkr, dfqn, dfkn, qg, kg, dfv, df_col, logf, dsq, dsk, dsv, dgf, dgs, "mix_bwd")
    grads['w_in'] = _wgrad(dproj, hmix, "dw_in", IN_PAD // 3, D_MODEL)
    dg1, du1, act1, part1 = _ffn_bwd_hidden(dx1, g1, u1, got0, ffn1[2], "ffn1_bwd_hidden",
                                            ride=(_pack_chunks(grads, 1), False))
    grads['ffn1_w_gate'] = _wgrad(dg1, h1, "dw_ffn1_gate", half, D_MODEL)
    grads['ffn1_w_up'] = _wgrad(du1, h1, "dw_ffn1_up", half, D_MODEL)
    grads['ffn1_w_down'] = _wgrad(act1, dx1, "dw_ffn1_down", half, D_MODEL)
    dx0, dg_ffn1, part0 = _ffn_bwd_input(x, dx1, g_ffn1, dg1, du1, got0, ffn1[:2], "ffn1_bwd_input",
                                         ride=(_pack_chunks(grads, 0), False))

    fold = lambda a: a.reshape(N_HEADS, HEAD_DIM).sum(axis=0).reshape(1, HEAD_DIM)
    small_g = {'ffn1_norm': dg_ffn1, 'mix_norm': dg_mix, 'ffn2_norm': dg_ffn2, 'ple_norm': dg_ple,
               'q_norm': fold(dqg), 'k_norm': fold(dkg), 'forget_bias': dbias[:, :N_HEADS]}
    return loss[0, 0], dx0, (part0, part1, part2), small_g


def kernel(x, p, ffn1_norm, ffn1_w_gate, ffn1_w_up, ffn1_w_down, mix_norm, w_in, forget_bias, q_norm, k_norm, w_branch_fox, w_branch_sb, w_out, ffn2_norm, ffn2_w_gate, ffn2_w_up, ffn2_w_down, ple_norm, w_ple_gate, w_ple_proj, loss_target, m_ffn1_norm, m_ffn1_w_gate, m_ffn1_w_up, m_ffn1_w_down, m_mix_norm, m_w_in, m_forget_bias, m_q_norm, m_k_norm, m_w_branch_fox, m_w_branch_sb, m_w_out, m_ffn2_norm, m_ffn2_w_gate, m_ffn2_w_up, m_ffn2_w_down, m_ple_norm, m_w_ple_gate, m_w_ple_proj, v_ffn1_norm, v_ffn1_w_gate, v_ffn1_w_up, v_ffn1_w_down, v_mix_norm, v_w_in, v_forget_bias, v_q_norm, v_k_norm, v_w_branch_fox, v_w_branch_sb, v_w_out, v_ffn2_norm, v_ffn2_w_gate, v_ffn2_w_up, v_ffn2_w_down, v_ple_norm, v_w_ple_gate, v_w_ple_proj):
    args = dict(locals())
    w = {n: args[n][0] for n in WEIGHT_NAMES}
    m = {n: args["m_" + n][0] for n in WEIGHT_NAMES}
    v = {n: args["v_" + n][0] for n in WEIGHT_NAMES}
    loss, dx, parts, small_g = _step(x[0], p[0, 0], loss_target[0], w)

    big = {}
    for grp, part in enumerate(parts):
        summed = _sum_parts(part, f"sum_grads_{grp}", SUM_TILE_ROWS[grp])
        for n in SCATTER_GROUPS[grp]:
            g = _shard_grad(summed, n, w[n].shape)
            big[n] = (g,) + tuple(_adamw_shard(g, w[n], m[n], v[n], "adamw_" + n))
    small_parts = _exchange(_pack_small(small_g, loss), "gather_small", True)
    sw, sm, sv = (_pack_small(t) for t in (w, m, v))
    small = _adamw(small_parts, sw, sm, sv, "adamw_small", SMALL_ROWS)

    outs = [small[0][4, LOSS_OFF], dx.reshape(x.shape)]
    for kind in range(4):
        for n in WEIGHT_NAMES:
            shape = args[n].shape
            outs.append(_unpack_small(small[kind], n, shape) if n in SMALL_NAMES else big[n][kind].reshape(shape))
    return tuple(outs)
```

```python
import jax
import jax.numpy as jnp
from jax import lax
from jax.experimental import pallas as pl
from jax.experimental.pallas import tpu as pltpu

F32 = jnp.float32
BF16 = jnp.bfloat16

D_MODEL = 1024
D_FF = 2816
N_HEADS = 8
HEAD_DIM = 64
ATT_W = N_HEADS * HEAD_DIM
PLE_DIM = 256
EPS = 1e-6
N_DEV = 8
MESH = pl.DeviceIdType.MESH

LANES = 128
V7X_SCOPED_VMEM_BYTES = 56 * 1024 * 1024

C_FQ, C_FK, C_FV, C_FL = 0, 512, 1024, 1536
C_SQ, C_SK, C_SV, C_GF, C_GS = 1792, 2304, 2816, 3328, 4352
IN_PAD = 5376
IN_REAL = 5128
FL_REAL_END = 1544

ADAM_LR = 0.001
ADAM_B1 = 0.9
ADAM_B2 = 0.999
ADAM_EPS = 1e-08
ADAM_WD = 0.01
ADAM_STEP = 10

PACK_ROWS = {"ffn1_w_gate": 352, "ffn1_w_up": 352, "ffn1_w_down": 352, "w_in": 656, "w_branch_fox": 64,
             "w_branch_sb": 64, "w_out": 128, "ffn2_w_gate": 352, "ffn2_w_up": 352, "ffn2_w_down": 352,
             "w_ple_gate": 128, "w_ple_proj": 32}
GATHER_GROUPS = (
    ("ffn1_w_gate", "ffn1_w_up", "ffn1_w_down"),
    ("w_in", "w_branch_fox", "w_branch_sb", "w_out"),
    ("ffn2_w_gate", "ffn2_w_up", "ffn2_w_down", "w_ple_gate", "w_ple_proj"),
)
SCATTER_GROUPS = (
    ("ffn1_w_gate", "ffn1_w_up", "ffn1_w_down"),
    ("w_in",),
    ("ffn2_w_gate", "ffn2_w_up", "ffn2_w_down", "w_ple_gate", "w_ple_proj", "w_branch_fox", "w_branch_sb", "w_out"),
)
SUM_TILE_ROWS = (528, 656, 368)


def _offsets(groups):
    off = {}
    for grp in groups:
        o = 0
        for n in grp:
            off[n] = o
            o += PACK_ROWS[n]
    return off


GATHER_OFF = _offsets(GATHER_GROUPS)
SCATTER_OFF = _offsets(SCATTER_GROUPS)
W_IN_ROWS = 641

SMALL_ROWS = 8


def _cparams(*sem):
    return pltpu.CompilerParams(dimension_semantics=sem, vmem_limit_bytes=V7X_SCOPED_VMEM_BYTES)


def _dot(a, b):
    return jnp.dot(a, b, preferred_element_type=F32)


def _dot_nt(a, b):
    return lax.dot_general(a, b, (((1,), (1,)), ((), ())), preferred_element_type=F32)


def _dot_tn(a, b):
    return lax.dot_general(a, b, (((0,), (0,)), ((), ())), preferred_element_type=F32)


def _split(x, parts):
    out = []
    r = x
    for _ in range(parts):
        p = r.astype(BF16)
        out.append(p)
        r = r - p.astype(F32)
    return out


def _dot_split(x, m, parts):
    acc = None
    for p in _split(x, parts):
        t = _dot(p, m)
        acc = t if acc is None else acc + t
    return acc


def _dot_split_left(m, x, parts):
    acc = None
    for p in _split(x, parts):
        t = _dot(m, p)
        acc = t if acc is None else acc + t
    return acc


def _rms_rinv(xf):
    return lax.rsqrt(jnp.mean(xf * xf, axis=-1, keepdims=True) + EPS)


def _sigmoid(x):
    return 1.0 / (1.0 + jnp.exp(-x))


def _softplus_neg_abs(z):
    return jnp.log(1.0 + jnp.exp(-jnp.abs(z)))


FFN_SHARD = D_FF // N_DEV
FFN_CHUNK = 4


def _ffn_w_spec(blk, index_map):
    return pl.BlockSpec((FFN_CHUNK, FFN_SHARD, D_MODEL), lambda *g: (index_map(*g), blk, 0))


def _ffn_w(ref):
    return ref[...].reshape(FFN_CHUNK * FFN_SHARD, D_MODEL)


def _ffn_fwd(x, gain, wbuf, blks, name, ride=None):
    s_len = x.shape[0]
    ts = min(512, s_len)
    fc = FFN_CHUNK * FFN_SHARD
    nt, nc = s_len // ts, D_FF // fc

    def body(x_ref, gain_ref, wg_ref, wu_ref, wd_ref, y_ref, g_ref, u_ref, h_ref, acc_scr):
        j = pl.program_id(1)

        @pl.when(j == 0)
        def _():
            xf = x_ref[...]
            h_ref[...] = ((xf * _rms_rinv(xf)) * gain_ref[...]).astype(BF16)
            acc_scr[...] = jnp.zeros_like(acc_scr)

        h = h_ref[...]
        g = _dot_nt(h, _ffn_w(wg_ref))
        u = _dot_nt(h, _ffn_w(wu_ref))
        g_ref[...] = g.astype(BF16)
        u_ref[...] = u.astype(BF16)
        a = (g * _sigmoid(g) * u).astype(BF16)
        acc_scr[...] += _dot(a, _ffn_w(wd_ref))

        @pl.when(j == nc - 1)
        def _():
            y_ref[...] = x_ref[...] + 0.5 * acc_scr[...]

    tok = pl.BlockSpec((ts, D_MODEL), lambda i, j: (i, 0))
    hid = pl.BlockSpec((ts, fc), lambda i, j: (i, j))
    return _ride_call(
        body, name, (nt, nc),
        [tok, pl.BlockSpec((1, D_MODEL), lambda i, j: (0, 0))] + [_ffn_w_spec(b, lambda i, j: j) for b in blks],
        [tok, hid, hid, tok],
        [jax.ShapeDtypeStruct((s_len, D_MODEL), F32), jax.ShapeDtypeStruct((s_len, D_FF), BF16),
         jax.ShapeDtypeStruct((s_len, D_FF), BF16), jax.ShapeDtypeStruct((s_len, D_MODEL), BF16)],
        [pltpu.VMEM((ts, D_MODEL), F32)], ("parallel", "arbitrary"), (x, gain, wbuf, wbuf, wbuf), ride)


def _ffn_bwd_hidden(dy, g, u, wbuf, blk, name, ride=None):
    s_len = dy.shape[0]
    ts = min(512, s_len)
    fc = FFN_CHUNK * FFN_SHARD
    nt, nc = s_len // ts, D_FF // fc

    def body(dy_ref, g_ref, u_ref, wd_ref, dg_ref, du_ref, act_ref):
        da = 0.5 * _dot_nt(dy_ref[...].astype(BF16), _ffn_w(wd_ref))
        gf = g_ref[...].astype(F32)
        uf = u_ref[...].astype(F32)
        sg = _sigmoid(gf)
        silu = gf * sg
        dg_ref[...] = (da * uf * (sg * (1.0 + gf * (1.0 - sg)))).astype(BF16)
        du_ref[...] = (da * silu).astype(BF16)
        act_ref[...] = (0.5 * silu * uf).astype(BF16)

    hid = pl.BlockSpec((ts, fc), lambda c, t: (t, c))
    return _ride_call(
        body, name, (nc, nt),
        [pl.BlockSpec((ts, D_MODEL), lambda c, t: (t, 0)), hid, hid, _ffn_w_spec(blk, lambda c, t: c)],
        [hid, hid, hid], [jax.ShapeDtypeStruct((s_len, D_FF), BF16)] * 3, [], ("parallel", "parallel"),
        (dy, g, u, wbuf), ride)


def _ffn_bwd_input(x, dy, gain, dg, du, wbuf, blks, name, ride=None):
    s_len = x.shape[0]
    ts = min(512, s_len)
    fc = FFN_CHUNK * FFN_SHARD
    nt, nc = s_len // ts, D_FF // fc

    def body(x_ref, dy_ref, gain_ref, dg_ref, du_ref, wg_ref, wu_ref, dx_ref, dgain_ref, acc):
        i = pl.program_id(0)
        j = pl.program_id(1)
        part = _dot(dg_ref[...], _ffn_w(wg_ref)) + _dot(du_ref[...], _ffn_w(wu_ref))

        @pl.when(j == 0)
        def _():
            acc[...] = part

        @pl.when(j > 0)
        def _():
            acc[...] += part

        @pl.when(j == nc - 1)
        def _():
            xf = x_ref[...]
            r = _rms_rinv(xf)
            xhat = xf * r
            dh = acc[...]
            dgp = jnp.sum(dh * xhat, axis=0, keepdims=True)

            @pl.when(i == 0)
            def _():
                dgain_ref[...] = dgp

            @pl.when(i > 0)
            def _():
                dgain_ref[...] += dgp

            dn = dh * gain_ref[...]
            dx_ref[...] = dy_ref[...] + r * (dn - xhat * jnp.mean(dn * xhat, axis=-1, keepdims=True))

    tok = pl.BlockSpec((ts, D_MODEL), lambda i, j: (i, 0))
    row = pl.BlockSpec((1, D_MODEL), lambda i, j: (0, 0))
    hid = pl.BlockSpec((ts, fc), lambda i, j: (i, j))
    return _ride_call(
        body, name, (nt, nc), [tok, tok, row, hid, hid] + [_ffn_w_spec(b, lambda i, j: j) for b in blks], [tok, row],
        [jax.ShapeDtypeStruct((s_len, D_MODEL), F32), jax.ShapeDtypeStruct((1, D_MODEL), F32)],
        [pltpu.VMEM((ts, D_MODEL), F32)], ("arbitrary", "arbitrary"), (x, dy, gain, dg, du, wbuf, wbuf), ride)


def _ffn_bwd_fused(x, dy, gain, g, u, wbuf, blks, name):
    s_len = x.shape[0]
    ts = min(512, s_len)
    fc = FFN_CHUNK * FFN_SHARD
    nt = s_len // ts
    assert D_FF == 2 * fc

    def hidden(dy_ref, g_ref, u_ref, wg_ref, wu_ref, wd_ref, dg_ref, du_ref, act_ref):
        da = 0.5 * _dot_nt(dy_ref[...].astype(BF16), _ffn_w(wd_ref))
        gf = g_ref[...].astype(F32)
        uf = u_ref[...].astype(F32)
        sg = _sigmoid(gf)
        silu = gf * sg
        dg = (da * uf * (sg * (1.0 + gf * (1.0 - sg)))).astype(BF16)
        du = (da * silu).astype(BF16)
        dg_ref[...] = dg
        du_ref[...] = du
        act_ref[...] = (0.5 * silu * uf).astype(BF16)
        return _dot(dg, _ffn_w(wg_ref)) + _dot(du, _ffn_w(wu_ref))

    def first(dy_ref, g_ref, u_ref, wg_ref, wu_ref, wd_ref, dg_ref, du_ref, act_ref, dh_ref):
        dh_ref[...] = hidden(dy_ref, g_ref, u_ref, wg_ref, wu_ref, wd_ref, dg_ref, du_ref, act_ref)

    def second(x_ref, dy_ref, gain_ref, g_ref, u_ref, wg_ref, wu_ref, wd_ref, dh0_ref, dg_half, du_half, act_half,
               dg_ref, du_ref, act_ref, dx_ref, dgain_ref):
        i = pl.program_id(0)
        dh = dh0_ref[...] + hidden(dy_ref, g_ref, u_ref, wg_ref, wu_ref, wd_ref, dg_ref, du_ref, act_ref)
        xf = x_ref[...]
        r = _rms_rinv(xf)
        xhat = xf * r
        dgp = jnp.sum(dh * xhat, axis=0, keepdims=True)

        @pl.when(i == 0)
        def _():
            dgain_ref[...] = dgp

        @pl.when(i > 0)
        def _():
            dgain_ref[...] += dgp

        dn = dh * gain_ref[...]
        dx_ref[...] = dy_ref[...] + r * (dn - xhat * jnp.mean(dn * xhat, axis=-1, keepdims=True))

    tok = pl.BlockSpec((ts, D_MODEL), lambda i: (i, 0))
    row = pl.BlockSpec((1, D_MODEL), lambda i: (0, 0))
    hid = lambda c: pl.BlockSpec((ts, fc), lambda i: (i, c))
    wts = lambda c: [pl.BlockSpec((FFN_CHUNK, FFN_SHARD, D_MODEL), lambda i, b=b: (c, b, 0),
                                  pipeline_mode=pl.Buffered(1)) for b in blks]
    hidden_shapes = [jax.ShapeDtypeStruct((s_len, D_FF), BF16)] * 3
    dg, du, act, dh0 = pl.pallas_call(
        first, name=name + "_a", grid=(nt,),
        in_specs=[tok, hid(0), hid(0)] + wts(0),
        out_specs=[hid(0)] * 3 + [tok],
        out_shape=hidden_shapes + [jax.ShapeDtypeStruct((s_len, D_MODEL), F32)],
        compiler_params=_cparams("parallel"),
    )(dy, g, u, wbuf, wbuf, wbuf)
    filled = pl.BlockSpec(memory_space=pl.ANY)
    dg, du, act, dx, dgain = pl.pallas_call(
        second, name=name + "_b", grid=(nt,),
        in_specs=[tok, tok, row, hid(1), hid(1)] + wts(1) + [tok, filled, filled, filled],
        out_specs=[hid(1)] * 3 + [tok, row],
        out_shape=hidden_shapes + [jax.ShapeDtypeStruct((s_len, D_MODEL), F32), jax.ShapeDtypeStruct((1, D_MODEL), F32)],
        input_output_aliases={9: 0, 10: 1, 11: 2},
        compiler_params=_cparams("arbitrary"),
    )(x, dy, gain, g, u, wbuf, wbuf, wbuf, dh0, dg, du, act)
    return dg, du, act, dx, dgain


def _wgrad(a, b, name, tk, tn):
    s_len, k_dim = a.shape
    n_dim = b.shape[1]
    ts = min(1024, s_len)
    ns = s_len // ts

    def body(a_ref, b_ref, o_ref, acc):
        s = pl.program_id(2)
        p = _dot_tn(a_ref[...].astype(BF16), b_ref[...].astype(BF16))

        @pl.when(s == 0)
        def _():
            acc[...] = p

        @pl.when(s > 0)
        def _():
            acc[...] += p

        @pl.when(s == ns - 1)
        def _():
            o_ref[...] = acc[...].astype(BF16)

    return pl.pallas_call(
        body, name=name, grid=(k_dim // tk, n_dim // tn, ns),
        in_specs=[
            pl.BlockSpec((ts, tk), lambda k, n, s: (s, k)),
            pl.BlockSpec((ts, tn), lambda k, n, s: (s, n)),
        ],
        out_specs=pl.BlockSpec((tk, tn), lambda k, n, s: (k, n)),
        out_shape=jax.ShapeDtypeStruct((k_dim, n_dim), BF16),
        scratch_shapes=[pltpu.VMEM((tk, tn), F32)],
        compiler_params=_cparams("parallel", "parallel", "arbitrary"),
    )(a, b)


HEAD_SUM_PARTS = 2


def _head_group_matrix():
    r = lax.broadcasted_iota(jnp.int32, (ATT_W, ATT_W), 0) // HEAD_DIM
    c = lax.broadcasted_iota(jnp.int32, (ATT_W, ATT_W), 1) // HEAD_DIM
    return (r == c).astype(BF16)


def _mix_fwd(x, gain, w_in, bias, qg, kg, name, ride=None):
    s_len = x.shape[0]
    ts = min(256, s_len)
    nt = s_len // ts
    gmat = _head_group_matrix()

    def body(x_ref, gain_ref, w_ref, bias_ref, qg_ref, kg_ref, gm_ref,
             h_ref, fqr_ref, fkr_ref, fqn_ref, fkn_ref, fv_ref, logf_ref, f_ref, ft_ref,
             sq_ref, sk_ref, sv_ref, gf_ref, gs_ref, kmax_ref, carry):
        i = pl.program_id(0)
        xf = x_ref[...]
        h = ((xf * _rms_rinv(xf)) * gain_ref[...]).astype(BF16)
        h_ref[...] = h
        gm = gm_ref[...]

        def proj(lo, n):
            return _dot_nt(h, w_ref[lo:lo + n, :])

        def headnorm(raw, g):
            ms = _dot_split(raw * raw, gm, HEAD_SUM_PARTS) * (1.0 / HEAD_DIM)
            return ((raw * lax.rsqrt(ms + EPS)) * g).astype(BF16)

        fq = proj(C_FQ, ATT_W)
        fqr_ref[...] = fq
        fqn_ref[...] = headnorm(fq, qg_ref[...])
        fk = proj(C_FK, ATT_W)
        fkr_ref[...] = fk
        fkn = headnorm(fk, kg_ref[...])
        fkn_ref[...] = fkn
        kn2 = jnp.max(_dot_split(jnp.square(fkn.astype(F32)), gm, HEAD_SUM_PARTS), axis=0, keepdims=True)

        @pl.when(i == 0)
        def _():
            kmax_ref[...] = kn2

        @pl.when(i > 0)
        def _():
            kmax_ref[...] = jnp.maximum(kmax_ref[...], kn2)
        fv_ref[...] = proj(C_FV, ATT_W).astype(BF16)
        sq_ref[...] = proj(C_SQ, ATT_W).astype(BF16)
        sk_ref[...] = proj(C_SK, ATT_W).astype(BF16)
        sv_ref[...] = proj(C_SV, ATT_W).astype(BF16)
        gf_ref[...] = proj(C_GF, D_MODEL)
        gs_ref[...] = proj(C_GS, D_MODEL)

        fl = proj(C_FL, LANES) + bias_ref[...]
        lane = lax.broadcasted_iota(jnp.int32, fl.shape, 1)
        logf = jnp.where(lane < N_HEADS, jnp.minimum(fl, 0.0) - _softplus_neg_abs(fl), 0.0)
        logf_ref[...] = logf

        @pl.when(i == 0)
        def _():
            carry[...] = jnp.zeros_like(carry)

        r = lax.broadcasted_iota(jnp.int32, (ts, ts), 0)
        c = lax.broadcasted_iota(jnp.int32, (ts, ts), 1)
        tri = (r >= c).astype(BF16)
        f_tile = _dot_split_left(tri, logf, 3) + carry[...]
        f_ref[...] = f_tile
        ft_ref[...] = f_tile.T[:N_HEADS, :]
        carry[...] = f_tile[ts - 1:ts, :]

    tok = lambda w: pl.BlockSpec((ts, w), lambda i: (i, 0))
    full = lambda a: pl.BlockSpec(a.shape, lambda i: (0, 0))
    f32o = lambda w: jax.ShapeDtypeStruct((s_len, w), F32)
    b16o = lambda w: jax.ShapeDtypeStruct((s_len, w), BF16)
    return _ride_call(
        body, name, (nt,),
        [tok(D_MODEL), full(gain), full(w_in), full(bias), full(qg), full(kg), full(gmat)],
        [
            tok(D_MODEL), tok(ATT_W), tok(ATT_W), tok(ATT_W), tok(ATT_W), tok(ATT_W), tok(LANES), tok(LANES),
            pl.BlockSpec((N_HEADS, ts), lambda i: (0, i)),
            tok(ATT_W), tok(ATT_W), tok(ATT_W), tok(D_MODEL), tok(D_MODEL),
            pl.BlockSpec((1, ATT_W), lambda i: (0, 0)),
        ],
        [
            b16o(D_MODEL), f32o(ATT_W), f32o(ATT_W), b16o(ATT_W), b16o(ATT_W), b16o(ATT_W), f32o(LANES), f32o(LANES),
            jax.ShapeDtypeStruct((N_HEADS, s_len), F32),
            b16o(ATT_W), b16o(ATT_W), b16o(ATT_W), f32o(D_MODEL), f32o(D_MODEL),
            jax.ShapeDtypeStruct((1, ATT_W), F32),
        ],
        [pltpu.VMEM((1, LANES), F32)], ("arbitrary",), (x, gain, w_in, bias, qg, kg, gmat), ride)


ATT_T = 256
ATT_ROWS = 2
EXP_ZERO = 88.0


def _att_tiling(s_len):
    t = min(ATT_T, s_len)
    nr = min(ATT_ROWS, s_len // t)
    return t, nr, s_len // (t * nr)


def _pair_specs(s_len, tq):
    qblk = pl.BlockSpec((tq, LANES), lambda hp, i: (i, hp))
    kvfull = pl.BlockSpec((s_len, LANES), lambda hp, i: (0, hp))
    return qblk, kvfull


def _walk_tiles(i, nr, load, sub, flush, init, more=None, trips=None):
    base = i * nr
    carries = list(init)
    for kk in range(nr - 1, -1, -1):
        rs = list(range(kk, nr))
        new, side = sub(rs, load(base + kk), [carries[r] for r in rs], [r == kk for r in rs])
        carries[kk:] = new
        flush(base + kk, side)

    done = jnp.int32(0)
    for last in range(nr - 1, -1, -1):
        rs = list(range(last + 1))

        def visit(n, cs, rs=rs):
            kb = base - 1 - n
            cs, side = sub(rs, load(kb), list(cs), [False] * len(rs))
            flush(kb, side)
            return tuple(cs)

        if trips is not None:
            todo = jnp.maximum(trips(carries, base, last) - done, 0)
            new = lax.fori_loop(0, todo, lambda it, cs, start=done, visit=visit: visit(start + it, cs),
                                tuple(carries[:last + 1]))
            done = done + todo
        else:
            def step(state, visit=visit, last=last):
                n, _, cs = state
                cs = visit(n, cs)
                return n + 1, more(cs[last]), cs

            done, _, new = lax.while_loop(lambda state: jnp.logical_and(state[0] < base, state[1] > 0), step,
                                          (done, more(carries[last]), tuple(carries[:last + 1])))
        carries[:last + 1] = list(new)
    return carries


def _stack(parts):
    return parts[0] if len(parts) == 1 else jnp.concatenate(parts, axis=0)


def _stacked_halves(x, lo):
    z = jnp.zeros_like(x)
    return jnp.concatenate([jnp.where(lo, x, z), jnp.where(lo, z, x)], axis=0)


def _fox_qk_bound(qst_r, km_ref, t):
    km = km_ref[...]
    out = []
    for j in (0, 1):
        qf = qst_r[j * t:(j + 1) * t, :].astype(F32)
        qn = jnp.sqrt(jnp.sum(qf * qf, axis=1, keepdims=True))
        out.append(qn * jnp.sqrt(km[:, j * HEAD_DIM:j * HEAD_DIM + 1]) * 1.001 + 1.0)
    return out


def _fox_trips(hp, flast_ref, qkb, fq, level):
    def trips(carries, base, r):
        gap = [jnp.max(qkb[r][j] + fq[r][j] - level(carries, r, j)) for j in (0, 1)]

        def needed(n):
            kb = jnp.maximum(base - 1 - n, 0)
            return jnp.logical_or(gap[0] - flast_ref[2 * hp, kb] > -EXP_ZERO,
                                  gap[1] - flast_ref[2 * hp + 1, kb] > -EXP_ZERO)

        return lax.while_loop(lambda n: jnp.logical_and(n < base, needed(n)), lambda n: n + 1, jnp.int32(0))
    return trips


def _fox_fwd(q, k, v, f_col, f_row, kmax, name, ride=None):
    s_len = q.shape[0]
    t, nr, nq = _att_tiling(s_len)

    def body(q_ref, k_ref, v_ref, f_ref, ft_ref, km_ref, fl_ref, y_ref, lse_ref):
        hp = pl.program_id(0)
        i = pl.program_id(1)
        lane = lax.broadcasted_iota(jnp.int32, (t, LANES), 1)
        lo = lane < HEAD_DIM
        causal = lax.broadcasted_iota(jnp.int32, (t, t), 0) >= lax.broadcasted_iota(jnp.int32, (t, t), 1)
        rows = [pl.ds(r * t, t) for r in range(nr)]
        qst = [_stacked_halves(q_ref[rw, :] * jnp.asarray(HEAD_DIM ** -0.5, BF16), lo) for rw in rows]
        q_all = _stack(qst)
        fq = [[jnp.sum(jnp.where(lane == 2 * hp + j, f_ref[rw, :], 0.0), axis=1, keepdims=True) for j in (0, 1)]
              for rw in rows]

        def load(kb):
            k0 = pl.multiple_of(kb * t, t)
            frow = [ft_ref[pl.ds(2 * hp + j, 1), pl.ds(k0, t)] for j in (0, 1)]
            return k_ref[pl.ds(k0, t), :], v_ref[pl.ds(k0, t), :], frow

        def sub(rs, tiles, carries, masked):
            kblk, vblk, frow = tiles
            z = _dot_nt(q_all if len(rs) == nr else _stack([qst[r] for r in rs]), kblk)
            ps, stats = [], []
            for n, (r, j) in enumerate((r, j) for r in range(len(rs)) for j in (0, 1)):
                m, l, _ = carries[r]
                s = z[n * t:(n + 1) * t, :] + (fq[rs[r]][j] - frow[j])
                if masked[r]:
                    s = jnp.where(causal, s, -1e30)
                mj = jnp.maximum(m[j], jnp.max(s, axis=1, keepdims=True))
                aj = jnp.exp(m[j] - mj)
                p = jnp.exp(s - mj)
                stats.append((mj, aj, aj * l[j] + jnp.sum(p, axis=1, keepdims=True)))
                ps.append(p.astype(BF16))
            pv = _dot(_stack(ps), vblk)
            out = []
            for r in range(len(rs)):
                (m0, a0, l0), (m1, a1, l1) = stats[2 * r], stats[2 * r + 1]
                acc = carries[r][2]
                acc = (acc[0] * a0 + pv[2 * r * t:(2 * r + 1) * t, :], acc[1] * a1 + pv[(2 * r + 1) * t:(2 * r + 2) * t, :])
                out.append(((m0, m1), (l0, l1), acc))
            return out, None

        neg = jnp.full((t, 1), -1e30, F32)
        zero = jnp.zeros((t, 1), F32)
        zacc = jnp.zeros((t, LANES), F32)
        init = [((neg, neg), (zero, zero), (zacc, zacc))] * nr
        qkb = [_fox_qk_bound(qs, km_ref, t) for qs in qst]
        trips = _fox_trips(hp, fl_ref, qkb, fq, lambda carries, r, j: carries[r][0][j])
        out = _walk_tiles(i, nr, load, sub, lambda kb, side: None, init, trips=trips)
        for rw, (m, l, acc) in zip(rows, out):
            y_ref[rw, :] = jnp.where(lo, acc[0] / l[0], acc[1] / l[1]).astype(BF16)
            lse_ref[0, rw, :] = jnp.where(lo, m[0] + jnp.log(l[0]), m[1] + jnp.log(l[1]))

    qblk, kvfull = _pair_specs(s_len, t * nr)
    return _ride_call(
        body, name, (N_HEADS // 2, nq),
        [qblk, kvfull, kvfull,
         pl.BlockSpec((t * nr, LANES), lambda hp, i: (i, 0)),
         pl.BlockSpec((N_HEADS, s_len), lambda hp, i: (0, 0)),
         pl.BlockSpec((1, LANES), lambda hp, i: (0, hp)),
         pl.BlockSpec(memory_space=pltpu.SMEM)],
        [qblk, pl.BlockSpec((1, t * nr, LANES), lambda hp, i: (hp, i, 0))],
        [jax.ShapeDtypeStruct((s_len, ATT_W), BF16), jax.ShapeDtypeStruct((N_HEADS // 2, s_len, LANES), F32)],
        [], ("parallel", "parallel"), (q, k, v, f_col, f_row, kmax, f_row[:, t - 1::t]), ride)


def _fox_bwd(q, k, v, dy, y, lse, f_col, f_row, kmax, name, ride=None):
    s_len = q.shape[0]
    t, nr, nq = _att_tiling(s_len)

    def body(q_ref, k_ref, v_ref, dy_ref, y_ref, lse_ref, f_ref, ft_ref, km_ref, fl_ref,
             dq_ref, dk_ref, dv_ref, dft_ref):
        hp = pl.program_id(0)
        i = pl.program_id(1)

        @pl.when(i == 0)
        def _():
            dk_ref[...] = jnp.zeros_like(dk_ref)
            dv_ref[...] = jnp.zeros_like(dv_ref)
            dft_ref[...] = jnp.zeros_like(dft_ref)

        lane = lax.broadcasted_iota(jnp.int32, (t, LANES), 1)
        lo = lane < HEAD_DIM
        causal = lax.broadcasted_iota(jnp.int32, (t, t), 0) >= lax.broadcasted_iota(jnp.int32, (t, t), 1)
        rows = [pl.ds(r * t, t) for r in range(nr)]
        qst, dyst, delta, lse, fq = [], [], [], [], []
        for rw in rows:
            qst.append(_stacked_halves(q_ref[rw, :] * jnp.asarray(HEAD_DIM ** -0.5, BF16), lo))
            dyb = dy_ref[rw, :]
            dyst.append(_stacked_halves(dyb, lo))
            prod = dyb.astype(F32) * y_ref[rw, :].astype(F32)
            delta.append([jnp.sum(jnp.where(lo, prod, 0.0), axis=1, keepdims=True),
                          jnp.sum(jnp.where(lo, 0.0, prod), axis=1, keepdims=True)])
            lse_b = lse_ref[0, rw, :]
            lse.append([lse_b[:, 0:1], lse_b[:, HEAD_DIM:HEAD_DIM + 1]])
            fq.append([jnp.sum(jnp.where(lane == 2 * hp + j, f_ref[rw, :], 0.0), axis=1, keepdims=True)
                       for j in (0, 1)])

        q_all, dy_all = _stack(qst), _stack(dyst)

        def load(kb):
            k0 = pl.multiple_of(kb * t, t)
            frow = [ft_ref[pl.ds(2 * hp + j, 1), pl.ds(k0, t)] for j in (0, 1)]
            return k_ref[pl.ds(k0, t), :], v_ref[pl.ds(k0, t), :], frow

        def sub(rs, tiles, carries, masked):
            kblk, vblk, frow = tiles
            qs, dys = (q_all, dy_all) if len(rs) == nr else (_stack([qst[r] for r in rs]), _stack([dyst[r] for r in rs]))
            z = _dot_nt(qs, kblk)
            dp = _dot_nt(dys, vblk)
            pb, dsb, rsum, col = [], [], [], [None, None]
            for n, (r, j) in enumerate((r, j) for r in range(len(rs)) for j in (0, 1)):
                sl = slice(n * t, (n + 1) * t)
                s = z[sl, :] + (fq[rs[r]][j] - frow[j])
                p = jnp.exp(s - lse[rs[r]][j])
                if masked[r]:
                    p = jnp.where(causal, p, 0.0)
                ds = p * (dp[sl, :] - delta[rs[r]][j])
                c = jnp.sum(ds, axis=0, keepdims=True)
                col[j] = c if col[j] is None else col[j] + c
                rsum.append(carries[r][1][j] + jnp.sum(ds, axis=1, keepdims=True))
                pb.append(p.astype(BF16))
                dsb.append(ds.astype(BF16))
            p_all, ds_all = _stack(pb), _stack(dsb)
            dqs = _dot(ds_all, kblk)
            out = []
            for r in range(len(rs)):
                dq = carries[r][0]
                dq = (dq[0] + dqs[2 * r * t:(2 * r + 1) * t, :], dq[1] + dqs[(2 * r + 1) * t:(2 * r + 2) * t, :])
                out.append((dq, (rsum[2 * r], rsum[2 * r + 1])))
            return out, (_dot_tn(ds_all, qs), _dot_tn(p_all, dys), col)

        def flush(kb, side):
            k0 = pl.multiple_of(kb * t, t)
            dk_ref[pl.ds(k0, t), :] += side[0]
            dv_ref[pl.ds(k0, t), :] += side[1]
            for j in (0, 1):
                dft_ref[0, pl.ds(j, 1), pl.ds(k0, t)] -= side[2][j]

        zero = jnp.zeros((t, 1), F32)
        zacc = jnp.zeros((t, LANES), F32)
        qkb = [_fox_qk_bound(qs, km_ref, t) for qs in qst]
        trips = _fox_trips(hp, fl_ref, qkb, fq, lambda carries, r, j: lse[r][j])
        out = _walk_tiles(i, nr, load, sub, flush, [((zacc, zacc), (zero, zero))] * nr, trips=trips)
        for r, (rw, (dq, rs)) in enumerate(zip(rows, out)):
            dq_ref[rw, :] = jnp.where(lo, dq[0], dq[1]) * (HEAD_DIM ** -0.5)
            rs_t = jnp.where(lo, rs[0], rs[1]).T
            q0 = pl.multiple_of((i * nr + r) * t, t)
            for j in (0, 1):
                dft_ref[0, pl.ds(j, 1), pl.ds(q0, t)] += rs_t[j * HEAD_DIM:j * HEAD_DIM + 1, :]

    qblk, kvfull = _pair_specs(s_len, t * nr)
    return _ride_call(
        body, name, (N_HEADS // 2, nq),
        [qblk, kvfull, kvfull, qblk, qblk,
         pl.BlockSpec((1, t * nr, LANES), lambda hp, i: (hp, i, 0)),
         pl.BlockSpec((t * nr, LANES), lambda hp, i: (i, 0)),
         pl.BlockSpec((N_HEADS, s_len), lambda hp, i: (0, 0)),
         pl.BlockSpec((1, LANES), lambda hp, i: (0, hp)),
         pl.BlockSpec(memory_space=pltpu.SMEM)],
        [qblk, kvfull, kvfull, pl.BlockSpec((1, 8, s_len), lambda hp, i: (hp, 0, 0))],
        [jax.ShapeDtypeStruct((s_len, ATT_W), F32)] * 3 + [jax.ShapeDtypeStruct((N_HEADS // 2, 8, s_len), F32)],
        [], ("arbitrary", "arbitrary"), (q, k, v, dy, y, lse, f_col, f_row, kmax, f_row[:, t - 1::t]), ride)


def _sb_more(carry):
    return (jnp.max(jnp.maximum(carry[0][0], carry[0][1])) > -EXP_ZERO).astype(jnp.int32)


def _stacked_split_dot(slabs, m, parts):
    split = [_split(x, parts) for x in slabs]
    acc = None
    for p in range(parts):
        d = _dot(_stack([s[p] for s in split]), m)
        acc = d if acc is None else acc + d
    return acc


def _sb_weights(z, c, strict, upper, t):
    logs = []
    for n in range(z.shape[0] // t):
        zn = z[n * t:(n + 1) * t, :]
        sp = _softplus_neg_abs(zn)
        l1m = jnp.minimum(-zn, 0.0) - sp
        if strict[n] is not None:
            l1m = jnp.where(strict[n], l1m, 0.0)
        logs.append((jnp.minimum(zn, 0.0) - sp, l1m))
    suf = _stacked_split_dot([l1m for _, l1m in logs], upper, 2)
    out = []
    for n, (logb, l1m) in enumerate(logs):
        after = c[n] + suf[n * t:(n + 1) * t, :]
        a = jnp.exp(logb + after)
        if strict[n] is not None:
            a = jnp.where(strict[n], a, 0.0)
        out.append((logb, a, after[:, 0:1] + l1m[:, 0:1]))
    return out


def _sb_fwd(q, k, v, name):
    s_len = q.shape[0]
    t, nr, nq = _att_tiling(s_len)

    def body(q_ref, k_ref, v_ref, y_ref, yf_ref):
        i = pl.program_id(1)
        lane = lax.broadcasted_iota(jnp.int32, (t, LANES), 1)
        lo = lane < HEAD_DIM
        ri = lax.broadcasted_iota(jnp.int32, (t, t), 0)
        ci = lax.broadcasted_iota(jnp.int32, (t, t), 1)
        strict = ci < ri
        upper = (ri > ci).astype(BF16)
        rows = [pl.ds(r * t, t) for r in range(nr)]
        qst = [_stacked_halves(q_ref[rw, :] * jnp.asarray(HEAD_DIM ** -0.5, BF16), lo) for rw in rows]
        q_all = _stack(qst)

        def load(kb):
            k0 = pl.multiple_of(kb * t, t)
            return k_ref[pl.ds(k0, t), :], v_ref[pl.ds(k0, t), :]

        def sub(rs, tiles, carries, masked):
            kblk, vblk = tiles
            z = _dot_nt(q_all if len(rs) == nr else _stack([qst[r] for r in rs]), kblk)
            c = [carries[r][0][j] for r in range(len(rs)) for j in (0, 1)]
            w = _sb_weights(z, c, [strict if masked[r] else None for r in range(len(rs)) for j in (0, 1)], upper, t)
            pv = _dot(_stack([a.astype(BF16) for _, a, _ in w]), vblk)
            out = []
            for r in range(len(rs)):
                acc = carries[r][1]
                acc = (acc[0] + pv[2 * r * t:(2 * r + 1) * t, :], acc[1] + pv[(2 * r + 1) * t:(2 * r + 2) * t, :])
                out.append(((w[2 * r][2], w[2 * r + 1][2]), acc))
            return out, None

        zero = jnp.zeros((t, 1), F32)
        zacc = jnp.zeros((t, LANES), F32)
        out = _walk_tiles(i, nr, load, sub, lambda kb, side: None, [((zero, zero), (zacc, zacc))] * nr, more=_sb_more)
        for rw, (_, acc) in zip(rows, out):
            y = jnp.where(lo, acc[0], acc[1])
            y_ref[rw, :] = y.astype(BF16)
            yf_ref[rw, :] = y

    qblk, kvfull = _pair_specs(s_len, t * nr)
    return pl.pallas_call(
        body, name=name, grid=(N_HEADS // 2, nq),
        in_specs=[qblk, kvfull, kvfull],
        out_specs=[qblk, qblk],
        out_shape=[jax.ShapeDtypeStruct((s_len, ATT_W), BF16), jax.ShapeDtypeStruct((s_len, ATT_W), F32)],
        compiler_params=_cparams("parallel", "parallel"),
    )(q, k, v)


def _sb_bwd(q, k, v, dy, yf, name):
    s_len = q.shape[0]
    t, nr, nq = _att_tiling(s_len)

    def body(q_ref, k_ref, v_ref, dy_ref, yf_ref, dq_ref, dk_ref, dv_ref):
        i = pl.program_id(1)

        @pl.when(i == 0)
        def _():
            dk_ref[...] = jnp.zeros_like(dk_ref)
            dv_ref[...] = jnp.zeros_like(dv_ref)

        lane = lax.broadcasted_iota(jnp.int32, (t, LANES), 1)
        lo = lane < HEAD_DIM
        ri = lax.broadcasted_iota(jnp.int32, (t, t), 0)
        ci = lax.broadcasted_iota(jnp.int32, (t, t), 1)
        strict = ci < ri
        upper = (ri > ci).astype(BF16)
        upper_incl = (ri >= ci).astype(BF16)
        rows = [pl.ds(r * t, t) for r in range(nr)]
        qst, dyst, delta = [], [], []
        for rw in rows:
            qst.append(_stacked_halves(q_ref[rw, :] * jnp.asarray(HEAD_DIM ** -0.5, BF16), lo))
            dyb = dy_ref[rw, :]
            dyst.append(_stacked_halves(dyb, lo))
            prod = dyb.astype(F32) * yf_ref[rw, :]
            delta.append([jnp.sum(jnp.where(lo, prod, 0.0), axis=1, keepdims=True),
                          jnp.sum(jnp.where(lo, 0.0, prod), axis=1, keepdims=True)])
        q_all, dy_all = _stack(qst), _stack(dyst)

        def load(kb):
            k0 = pl.multiple_of(kb * t, t)
            return k_ref[pl.ds(k0, t), :], v_ref[pl.ds(k0, t), :]

        def sub(rs, tiles, carries, masked):
            kblk, vblk = tiles
            qs, dys = (q_all, dy_all) if len(rs) == nr else (_stack([qst[r] for r in rs]), _stack([dyst[r] for r in rs]))
            slabs = [(r, j) for r in range(len(rs)) for j in (0, 1)]
            z = _dot_nt(qs, kblk)
            w = _sb_weights(z, [carries[r][0][j] for r, j in slabs], [strict if masked[r] else None for r, j in slabs],
                            upper, t)
            da = _dot_nt(dys, vblk)
            ab = [a.astype(BF16) for _, a, _ in w]
            dl = [ab[n].astype(F32) * da[n * t:(n + 1) * t, :] for n in range(len(slabs))]
            tail = _stacked_split_dot(dl, upper_incl, 2)
            dzb, e_new = [], []
            for n, (r, j) in enumerate(slabs):
                tl = tail[n * t:(n + 1) * t, :]
                e = carries[r][1][j]
                dl1m = (delta[rs[r]][j] - e) - tl
                e_new.append(e + tl[:, 0:1])
                dz = dl[n] - jnp.exp(w[n][0]) * (dl[n] + dl1m)
                if masked[r]:
                    dz = jnp.where(strict, dz, 0.0)
                dzb.append(dz.astype(BF16))
            a_all, dz_all = _stack(ab), _stack(dzb)
            dqs = _dot(dz_all, kblk)
            out = []
            for r in range(len(rs)):
                dq = carries[r][2]
                dq = (dq[0] + dqs[2 * r * t:(2 * r + 1) * t, :], dq[1] + dqs[(2 * r + 1) * t:(2 * r + 2) * t, :])
                out.append(((w[2 * r][2], w[2 * r + 1][2]), (e_new[2 * r], e_new[2 * r + 1]), dq))
            return out, (_dot_tn(dz_all, qs), _dot_tn(a_all, dys))

        def flush(kb, side):
            k0 = pl.multiple_of(kb * t, t)
            dk_ref[pl.ds(k0, t), :] += side[0]
            dv_ref[pl.ds(k0, t), :] += side[1]

        zero = jnp.zeros((t, 1), F32)
        zacc = jnp.zeros((t, LANES), F32)
        out = _walk_tiles(i, nr, load, sub, flush, [((zero, zero), (zero, zero), (zacc, zacc))] * nr, more=_sb_more)
        for rw, (_, _, dq) in zip(rows, out):
            dq_ref[rw, :] = jnp.where(lo, dq[0], dq[1]) * (HEAD_DIM ** -0.5)

    qblk, kvfull = _pair_specs(s_len, t * nr)
    return pl.pallas_call(
        body, name=name, grid=(N_HEADS // 2, nq),
        in_specs=[qblk, kvfull, kvfull, qblk, qblk],
        out_specs=[qblk, kvfull, kvfull],
        out_shape=[jax.ShapeDtypeStruct((s_len, ATT_W), F32)] * 3,
        compiler_params=_cparams("arbitrary", "arbitrary"),
    )(q, k, v, dy, yf)


def _merge_fwd(x, yf, ys, gf, gs, wbf, wbs, wo, name):
    s_len = x.shape[0]
    ts = min(512, s_len)

    def body(x_ref, yf_ref, ys_ref, gf_ref, gs_ref, wbf_ref, wbs_ref, wo_ref, o_ref):
        merged = (_sigmoid(gf_ref[...]) * _dot_nt(yf_ref[...], wbf_ref[...])
                  + _sigmoid(gs_ref[...]) * _dot_nt(ys_ref[...], wbs_ref[...]))
        o_ref[...] = x_ref[...] + _dot(merged.astype(BF16), wo_ref[...])

    tok = lambda w: pl.BlockSpec((ts, w), lambda i: (i, 0))
    full = lambda a: pl.BlockSpec(a.shape, lambda i: (0, 0))
    return pl.pallas_call(
        body, name=name, grid=(s_len // ts,),
        in_specs=[tok(D_MODEL), tok(ATT_W), tok(ATT_W), tok(D_MODEL), tok(D_MODEL), full(wbf), full(wbs), full(wo)],
        out_specs=tok(D_MODEL),
        out_shape=jax.ShapeDtypeStruct((s_len, D_MODEL), F32),
        compiler_params=_cparams("parallel"),
    )(x, yf, ys, gf, gs, wbf, wbs, wo)


def _merge_bwd(dx, yf, ys, gf, gs, wbf, wbs, wo, name):
    s_len = dx.shape[0]
    ts = min(512, s_len)

    def body(dx_ref, yf_ref, ys_ref, gf_ref, gs_ref, wbf_ref, wbs_ref, wo_ref,
             dyf_ref, dys_ref, dgf_ref, dgs_ref, dbf_ref, dbs_ref, mg_ref):
        bf = _dot_nt(yf_ref[...], wbf_ref[...])
        bs = _dot_nt(ys_ref[...], wbs_ref[...])
        sf = _sigmoid(gf_ref[...])
        ss = _sigmoid(gs_ref[...])
        mg_ref[...] = (sf * bf + ss * bs).astype(BF16)
        dm = _dot_nt(dx_ref[...].astype(BF16), wo_ref[...])
        dbf = (dm * sf).astype(BF16)
        dbs = (dm * ss).astype(BF16)
        dbf_ref[...] = dbf
        dbs_ref[...] = dbs
        dgf_ref[...] = (dm * bf * (sf * (1.0 - sf))).astype(BF16)
        dgs_ref[...] = (dm * bs * (ss * (1.0 - ss))).astype(BF16)
        dyf_ref[...] = _dot(dbf, wbf_ref[...]).astype(BF16)
        dys_ref[...] = _dot(dbs, wbs_ref[...]).astype(BF16)

    tok = lambda w: pl.BlockSpec((ts, w), lambda i: (i, 0))
    full = lambda a: pl.BlockSpec(a.shape, lambda i: (0, 0))
    b16o = lambda w: jax.ShapeDtypeStruct((s_len, w), BF16)
    return pl.pallas_call(
        body, name=name, grid=(s_len // ts,),
        in_specs=[tok(D_MODEL), tok(ATT_W), tok(ATT_W), tok(D_MODEL), tok(D_MODEL), full(wbf), full(wbs), full(wo)],
        out_specs=[tok(ATT_W), tok(ATT_W)] + [tok(D_MODEL)] * 5,
        out_shape=[b16o(ATT_W), b16o(ATT_W)] + [b16o(D_MODEL)] * 5,
        compiler_params=_cparams("parallel"),
    )(dx, yf, ys, gf, gs, wbf, wbs, wo)


def _mix_bwd(x, dx_in, gain, w_in, fqr, fkr, dfqn, dfkn, qg, kg, dfv, df_col, logf, dsq, dsk, dsv, dgf, dgs, name,
             ride=None):
    s_len = x.shape[0]
    ts = min(256, s_len)
    nt = s_len // ts
    gmat = _head_group_matrix()

    def body(x_ref, dxi_ref, gain_ref, w_ref, fqr_ref, fkr_ref, dfqn_ref, dfkn_ref, qg_ref, kg_ref, gm_ref,
             dfv_ref, df_ref, logf_ref, dsq_ref, dsk_ref, dsv_ref, dgf_ref, dgs_ref,
             dp_ref, dx_ref, dgain_ref, dqg_ref, dkg_ref, dbias_ref, carry):
        i = pl.program_id(0)

        @pl.when(i == 0)
        def _():
            carry[...] = jnp.zeros_like(carry)
            dgain_ref[...] = jnp.zeros_like(dgain_ref)
            dqg_ref[...] = jnp.zeros_like(dqg_ref)
            dkg_ref[...] = jnp.zeros_like(dkg_ref)
            dbias_ref[...] = jnp.zeros_like(dbias_ref)

        gm = gm_ref[...]

        def headnorm_bwd(raw, dout, g, dg_ref):
            ms = _dot_split(raw * raw, gm, HEAD_SUM_PARTS) * (1.0 / HEAD_DIM)
            r = lax.rsqrt(ms + EPS)
            nrm = raw * r
            dg_ref[...] += jnp.sum(dout * nrm, axis=0, keepdims=True)
            dn = dout * g
            mean_h = _dot_split(dn * nrm, gm, HEAD_SUM_PARTS) * (1.0 / HEAD_DIM)
            return r * (dn - nrm * mean_h)

        dp_ref[:, C_FQ:C_FQ + ATT_W] = headnorm_bwd(fqr_ref[...], dfqn_ref[...], qg_ref[...], dqg_ref).astype(BF16)
        dp_ref[:, C_FK:C_FK + ATT_W] = headnorm_bwd(fkr_ref[...], dfkn_ref[...], kg_ref[...], dkg_ref).astype(BF16)
        dp_ref[:, C_FV:C_FV + ATT_W] = dfv_ref[...].astype(BF16)
        dp_ref[:, C_SQ:C_SQ + ATT_W] = dsq_ref[...].astype(BF16)
        dp_ref[:, C_SK:C_SK + ATT_W] = dsk_ref[...].astype(BF16)
        dp_ref[:, C_SV:C_SV + ATT_W] = dsv_ref[...].astype(BF16)
        dp_ref[:, C_GF:C_GF + D_MODEL] = dgf_ref[...]
        dp_ref[:, C_GS:C_GS + D_MODEL] = dgs_ref[...]

        r_ = lax.broadcasted_iota(jnp.int32, (ts, ts), 0)
        c_ = lax.broadcasted_iota(jnp.int32, (ts, ts), 1)
        rev = (c_ >= r_).astype(BF16)
        dlogf = _dot_split_left(rev, df_ref[...], 3) + carry[...]
        carry[...] = dlogf[0:1, :]
        lane = lax.broadcasted_iota(jnp.int32, (ts, LANES), 1)
        dfl = jnp.where(lane < N_HEADS, dlogf * (1.0 - jnp.exp(logf_ref[...])), 0.0)
        dbias_ref[...] += jnp.sum(dfl, axis=0, keepdims=True)
        dp_ref[:, C_FL:C_FL + LANES] = dfl.astype(BF16)
        dp_ref[:, C_FL + LANES:C_SQ] = jnp.zeros((ts, C_SQ - C_FL - LANES), BF16)

        dh = _dot(dp_ref[...], w_ref[...])
        xf = x_ref[...]
        r = _rms_rinv(xf)
        xhat = xf * r
        dgain_ref[...] += jnp.sum(dh * xhat, axis=0, keepdims=True)
        dn = dh * gain_ref[...]
        dx_ref[...] = dxi_ref[...] + r * (dn - xhat * jnp.mean(dn * xhat, axis=-1, keepdims=True))

    tok = lambda w: pl.BlockSpec((ts, w), lambda i: (nt - 1 - i, 0))
    full = lambda a: pl.BlockSpec(a.shape, lambda i: (0, 0))
    row = lambda w: pl.BlockSpec((1, w), lambda i: (0, 0))
    return _ride_call(
        body, name, (nt,),
        [tok(D_MODEL), tok(D_MODEL), full(gain), full(w_in), tok(ATT_W), tok(ATT_W), tok(ATT_W), tok(ATT_W),
         full(qg), full(kg), full(gmat), tok(ATT_W), tok(LANES), tok(LANES), tok(ATT_W), tok(ATT_W), tok(ATT_W),
         tok(D_MODEL), tok(D_MODEL)],
        [tok(IN_PAD), tok(D_MODEL), row(D_MODEL), row(ATT_W), row(ATT_W), row(LANES)],
        [jax.ShapeDtypeStruct((s_len, IN_PAD), BF16), jax.ShapeDtypeStruct((s_len, D_MODEL), F32),
         jax.ShapeDtypeStruct((1, D_MODEL), F32), jax.ShapeDtypeStruct((1, ATT_W), F32),
         jax.ShapeDtypeStruct((1, ATT_W), F32), jax.ShapeDtypeStruct((1, LANES), F32)],
        [pltpu.VMEM((1, LANES), F32)], ("arbitrary",),
        (x, dx_in, gain, w_in, fqr, fkr, dfqn, dfkn, qg, kg, gmat, dfv, df_col, logf, dsq, dsk, dsv, dgf, dgs), ride)


def _ple_loss(x, p, tgt, gain, wpg, wpp, name):
    s_len = x.shape[0]
    ts = min(512, s_len)

    def body(x_ref, p_ref, t_ref, gain_ref, wpg_ref, wpp_ref, dx_ref, n_ref, ds_ref, dpp_ref, dgain_ref, loss_ref):
        i = pl.program_id(0)

        @pl.when(i == 0)
        def _():
            dgain_ref[...] = jnp.zeros_like(dgain_ref)
            loss_ref[...] = jnp.zeros_like(loss_ref)

        xf = x_ref[...]
        r = _rms_rinv(xf)
        n = xf * r
        hn = (n * gain_ref[...]).astype(BF16)
        n_ref[...] = hn
        sg = _sigmoid(_dot(hn, wpg_ref[...]))
        pp = _dot_nt(p_ref[...].astype(BF16), wpp_ref[...])
        err = (xf + sg * pp) - t_ref[...]
        sq = jnp.sum(jnp.sum(err * err, axis=1, keepdims=True), axis=0, keepdims=True)
        loss_ref[...] += (0.5 / D_MODEL) * sq
        dout = err * (1.0 / D_MODEL)
        dpp_ref[...] = (dout * sg).astype(BF16)
        ds = (dout * pp * (sg * (1.0 - sg))).astype(BF16)
        ds_ref[...] = ds
        dhn = _dot_nt(ds, wpg_ref[...])
        dgain_ref[...] += jnp.sum(dhn * n, axis=0, keepdims=True)
        dn = dhn * gain_ref[...]
        dx_ref[...] = dout + r * (dn - n * jnp.mean(dn * n, axis=-1, keepdims=True))

    tok = lambda w: pl.BlockSpec((ts, w), lambda i: (i, 0))
    full = lambda a: pl.BlockSpec(a.shape, lambda i: (0, 0))
    return pl.pallas_call(
        body, name=name, grid=(s_len // ts,),
        in_specs=[tok(D_MODEL), tok(PLE_DIM), tok(D_MODEL), full(gain), full(wpg), full(wpp)],
        out_specs=[tok(D_MODEL), tok(D_MODEL), tok(D_MODEL), tok(D_MODEL),
                   pl.BlockSpec((1, D_MODEL), lambda i: (0, 0)), pl.BlockSpec((8, LANES), lambda i: (0, 0))],
        out_shape=[jax.ShapeDtypeStruct((s_len, D_MODEL), F32), jax.ShapeDtypeStruct((s_len, D_MODEL), BF16),
                   jax.ShapeDtypeStruct((s_len, D_MODEL), BF16), jax.ShapeDtypeStruct((s_len, D_MODEL), BF16),
                   jax.ShapeDtypeStruct((1, D_MODEL), F32), jax.ShapeDtypeStruct((8, LANES), F32)],
        compiler_params=_cparams("arbitrary"),
    )(x, p, tgt, gain, wpg, wpp)


def _exchange(x, name, broadcast):
    def body(x_ref, out_ref, send_sems, recv_sems, local_sem):
        _exchange_start(x_ref, out_ref, send_sems, recv_sems, local_sem, broadcast)
        _exchange_wait(x_ref, out_ref, send_sems, recv_sems, local_sem, broadcast)

    return pl.pallas_call(
        body, name=name,
        in_specs=[EXCHANGE_SPEC],
        out_specs=EXCHANGE_SPEC,
        out_shape=_exchange_shape(x, broadcast),
        scratch_shapes=list(EXCHANGE_SEMS),
        compiler_params=pltpu.CompilerParams(has_side_effects=True),
    )(x)


def _gather_two_level(x, name):
    def body(x_ref, out_ref, send_sems, recv_sems, local_sem):
        mx, my, mc = lax.axis_index("x"), lax.axis_index("y"), lax.axis_index("c")
        me, sibling = (mx, my, mc), (mx, my, 1 - mc)
        chips = [(1 - mx, my), (mx, 1 - my), (1 - mx, 1 - my)]

        def slot(px, py, pc):
            return out_ref.at[4 * px + 2 * py + pc]

        def copy(k, block, to, src=None):
            return pltpu.make_async_remote_copy(
                src_ref=slot(*block) if src is None else src, dst_ref=slot(*block),
                send_sem=send_sems.at[k], recv_sem=recv_sems.at[k], device_id=to, device_id_type=MESH)

        mine = pltpu.make_async_copy(x_ref, slot(*me), local_sem)
        mine.start()
        first = [copy(0, me, sibling, src=x_ref)]
        first += [copy(1 + j, me, (*chip, mc), src=x_ref) for j, chip in enumerate(chips)]
        for cp in first:
            cp.start()
        passed = [copy(4 + j, (*chip, mc), sibling) for j, chip in enumerate(chips)]
        for j, chip in enumerate(chips):
            copy(1 + j, (*chip, mc), me).wait_recv()
            passed[j].start()
        copy(0, sibling, me).wait_recv()
        for j, chip in enumerate(chips):
            copy(4 + j, (*chip, 1 - mc), me).wait_recv()
        for cp in first + passed:
            cp.wait_send()
        mine.wait()

    return pl.pallas_call(
        body, name=name,
        in_specs=[EXCHANGE_SPEC],
        out_specs=EXCHANGE_SPEC,
        out_shape=_exchange_shape(x, True),
        scratch_shapes=list(EXCHANGE_SEMS),
        compiler_params=pltpu.CompilerParams(has_side_effects=True),
    )(x)


EXCHANGE_SPEC = pl.BlockSpec(memory_space=pl.ANY)
EXCHANGE_SEMS = (pltpu.SemaphoreType.DMA((N_DEV - 1,)), pltpu.SemaphoreType.DMA((N_DEV - 1,)), pltpu.SemaphoreType.DMA)


def _exchange_shape(x, broadcast):
    return jax.ShapeDtypeStruct((N_DEV,) + tuple(x.shape if broadcast else x.shape[1:]), x.dtype)


def _exchange_copies(x_ref, out_ref, send_sems, recv_sems, local_sem, broadcast, with_recv=True):
    mx, my, mc = lax.axis_index("x"), lax.axis_index("y"), lax.axis_index("c")
    me = 4 * mx + 2 * my + mc

    def src(idx):
        return x_ref if broadcast else x_ref.at[idx]

    local = pltpu.make_async_copy(src(me), out_ref.at[me], local_sem)
    pairs = []
    for k in range(1, N_DEV):
        px = (1 - mx) if k & 4 else mx
        py = (1 - my) if k & 2 else my
        pc = (1 - mc) if k & 1 else mc
        peer = 4 * px + 2 * py + pc
        sems = dict(send_sem=send_sems.at[k - 1], recv_sem=recv_sems.at[k - 1], device_id=(px, py, pc), device_id_type=MESH)
        recv = pltpu.make_async_remote_copy(src_ref=src(peer), dst_ref=out_ref.at[peer], **sems) if with_recv else None
        pairs.append((pltpu.make_async_remote_copy(src_ref=src(peer), dst_ref=out_ref.at[me], **sems), recv))
    return local, pairs


def _exchange_start(*refs_and_mode):
    local, pairs = _exchange_copies(*refs_and_mode, with_recv=False)
    local.start()
    for send, _ in pairs:
        send.start()


def _exchange_wait(*refs_and_mode):
    local, pairs = _exchange_copies(*refs_and_mode)
    for _, recv in pairs:
        recv.wait_recv()
    for send, _ in pairs:
        send.wait_send()
    local.wait()


def _riding(body, grid, n_in, n_out, ride):
    if ride is None:
        return body
    broadcast = ride[1]

    def wrapped(*refs):
        ins, x_ref = refs[:n_in], refs[n_in]
        outs, out_ref = refs[n_in + 1:n_in + 1 + n_out], refs[n_in + 1 + n_out]
        scratch, sems = refs[n_in + 2 + n_out:-3], refs[-3:]
        step = pl.program_id(0)
        for d in range(1, len(grid)):
            step = step * grid[d] + pl.program_id(d)
        total = 1
        for g in grid:
            total *= g

        @pl.when(step == 0)
        def _():
            _exchange_start(x_ref, out_ref, *sems, broadcast)

        body(*ins, *outs, *scratch)

        @pl.when(step == total - 1)
        def _():
            _exchange_wait(x_ref, out_ref, *sems, broadcast)

    return wrapped


def _ride_call(body, name, grid, in_specs, out_specs, out_shape, scratch_shapes, sem, operands, ride):
    if ride is None:
        return pl.pallas_call(body, name=name, grid=grid, in_specs=in_specs, out_specs=out_specs, out_shape=out_shape,
                              scratch_shapes=scratch_shapes, compiler_params=_cparams(*sem))(*operands)
    return pl.pallas_call(
        _riding(body, grid, len(in_specs), len(out_specs), ride), name=name, grid=grid,
        in_specs=list(in_specs) + [EXCHANGE_SPEC], out_specs=list(out_specs) + [EXCHANGE_SPEC],
        out_shape=list(out_shape) + [_exchange_shape(*ride)],
        scratch_shapes=list(scratch_shapes) + list(EXCHANGE_SEMS),
        compiler_params=_cparams(*(["arbitrary"] * len(grid))),
    )(*operands, ride[0])


def _adamw_math(w, g, m, v):
    m2 = ADAM_B1 * m + (1.0 - ADAM_B1) * g
    v2 = ADAM_B2 * v + (1.0 - ADAM_B2) * (g * g)
    m_hat = m2 / (1.0 - ADAM_B1 ** ADAM_STEP)
    v_hat = v2 / (1.0 - ADAM_B2 ** ADAM_STEP)
    delta = -ADAM_LR * (m_hat / (jnp.sqrt(v_hat) + ADAM_EPS) + ADAM_WD * w)
    return delta, m2, v2


def _sum_parts(parts, name, tr):
    _, rows, cols = parts.shape

    def body(p_ref, g_ref):
        g = p_ref[0].astype(F32)
        for s in range(1, N_DEV):
            g = g + p_ref[s].astype(F32)
        g_ref[...] = g

    return pl.pallas_call(
        body, name=name, grid=(rows // tr,),
        in_specs=[pl.BlockSpec((N_DEV, tr, cols), lambda i: (0, i, 0))],
        out_specs=pl.BlockSpec((tr, cols), lambda i: (i, 0)),
        out_shape=jax.ShapeDtypeStruct((rows, cols), F32),
        compiler_params=_cparams("parallel"),
    )(parts)


ADAM_SPLIT_ELEMS = 400_000


def _adamw_shard(g, w, m, v, name):
    rows, cols = w.shape
    tr = rows // 2 if rows * cols > ADAM_SPLIT_ELEMS else rows

    def body(g_ref, w_ref, m_ref, v_ref, d_ref, m2_ref, v2_ref):
        d_ref[...], m2_ref[...], v2_ref[...] = _adamw_math(w_ref[...], g_ref[...], m_ref[...], v_ref[...])

    blk = pl.BlockSpec((tr, cols), lambda i: (i, 0))
    return pl.pallas_call(
        body, name=name, grid=(rows // tr,),
        in_specs=[blk] * 4, out_specs=[blk] * 3,
        out_shape=[jax.ShapeDtypeStruct((rows, cols), F32)] * 3,
        compiler_params=_cparams("parallel"),
    )(g, w, m, v)


def _adamw(parts, w, m, v, name, tr):
    rows, cols = w.shape

    def body(p_ref, w_ref, m_ref, v_ref, g_ref, d_ref, m2_ref, v2_ref):
        g = p_ref[0].astype(F32)
        for s in range(1, N_DEV):
            g = g + p_ref[s].astype(F32)
        g_ref[...] = g
        d_ref[...], m2_ref[...], v2_ref[...] = _adamw_math(w_ref[...], g, m_ref[...], v_ref[...])

    blk = pl.BlockSpec((tr, cols), lambda i: (i, 0))
    return pl.pallas_call(
        body, name=name, grid=(rows // tr,),
        in_specs=[pl.BlockSpec((N_DEV, tr, cols), lambda i: (0, i, 0)), blk, blk, blk],
        out_specs=[blk] * 4,
        out_shape=[jax.ShapeDtypeStruct((rows, cols), F32)] * 4,
        compiler_params=_cparams("parallel"),
    )(parts, w, m, v)


TRANSPOSED = frozenset(("ffn1_w_gate", "ffn1_w_up", "w_in", "w_branch_fox", "w_branch_sb", "ffn2_w_gate", "ffn2_w_up",
                        "w_ple_proj"))
F_PAD_ROWS = C_SQ - FL_REAL_END


def _pack(pieces, group, dtype):
    out = []
    for name in GATHER_GROUPS[group]:
        r = pieces[name].T if name in TRANSPOSED else pieces[name]
        r = r.reshape(-1, D_MODEL).astype(dtype)
        if r.shape[0] != PACK_ROWS[name]:
            r = jnp.pad(r, ((0, PACK_ROWS[name] - r.shape[0]), (0, 0)))
        out.append(r)
    return jnp.concatenate(out, axis=0)


def _real_rows(name):
    return W_IN_ROWS if name == "w_in" else PACK_ROWS[name]


def _gathered(got, name, shape):
    off = GATHER_OFF[name]
    return got[:, off:off + _real_rows(name), :].reshape(shape)


def _w_in_device_rows(d):
    lo, hi = d * W_IN_ROWS, (d + 1) * W_IN_ROWS
    if hi <= FL_REAL_END:
        return [(lo, hi)]
    if lo >= FL_REAL_END:
        return [(lo + F_PAD_ROWS, hi + F_PAD_ROWS)]
    return [(lo, FL_REAL_END), (C_SQ, hi + F_PAD_ROWS)]


def _w_in_t_padded(got):
    t = _gathered(got, "w_in", (IN_REAL, D_MODEL))
    return jnp.concatenate([t[:FL_REAL_END], jnp.zeros((F_PAD_ROWS, D_MODEL), t.dtype), t[FL_REAL_END:]], axis=0)


def _pack_chunks(grads, group):
    out = []
    for name in SCATTER_GROUPS[group]:
        g = grads[name].astype(BF16)
        if name == "w_in":
            tail = jnp.zeros((PACK_ROWS[name] - W_IN_ROWS, D_MODEL), BF16)
            c = jnp.stack([jnp.concatenate([g[lo:hi] for lo, hi in _w_in_device_rows(d)] + [tail], axis=0)
                           for d in range(N_DEV)])
        else:
            c = g.reshape(N_DEV, PACK_ROWS[name], D_MODEL)
        out.append(c)
    return out[0] if len(out) == 1 else jnp.concatenate(out, axis=1)


def _shard_grad(packed, name, shape):
    off = SCATTER_OFF[name]
    rows = packed[off:off + _real_rows(name), :]
    return rows.reshape(shape[1], shape[0]).T if name in TRANSPOSED else rows.reshape(shape)


WEIGHT_NAMES = ['ffn1_norm', 'ffn1_w_gate', 'ffn1_w_up', 'ffn1_w_down', 'mix_norm', 'w_in', 'forget_bias', 'q_norm',
                'k_norm', 'w_branch_fox', 'w_branch_sb', 'w_out', 'ffn2_norm', 'ffn2_w_gate', 'ffn2_w_up',
                'ffn2_w_down', 'ple_norm', 'w_ple_gate', 'w_ple_proj']
SMALL_NAMES = ('ffn1_norm', 'mix_norm', 'ffn2_norm', 'ple_norm', 'q_norm', 'k_norm', 'forget_bias')
Q_OFF, K_OFF, B_OFF, LOSS_OFF = 0, HEAD_DIM, 2 * HEAD_DIM, 2 * HEAD_DIM + N_HEADS


def _pack_small(vals, loss=None):
    tail = [vals['q_norm'].reshape(1, -1), vals['k_norm'].reshape(1, -1), vals['forget_bias'].reshape(1, -1)]
    used = LOSS_OFF
    if loss is not None:
        tail.append(loss.reshape(1, 1))
        used += 1
    tail.append(jnp.zeros((1, D_MODEL - used), F32))
    rows = [vals[n].reshape(1, D_MODEL) for n in SMALL_NAMES[:4]] + [jnp.concatenate(tail, axis=1)]
    rows.append(jnp.zeros((SMALL_ROWS - len(rows), D_MODEL), F32))
    return jnp.concatenate(rows, axis=0)


def _unpack_small(packed, name, shape):
    if name in SMALL_NAMES[:4]:
        return packed[SMALL_NAMES.index(name)].reshape(shape)
    off, n = {'q_norm': (Q_OFF, HEAD_DIM), 'k_norm': (K_OFF, HEAD_DIM), 'forget_bias': (B_OFF, N_HEADS)}[name]
    return packed[4, off:off + n].reshape(shape)


def _step(x, p, tgt, w):
    row = lambda a: a.reshape(1, -1).astype(F32)
    g_ffn1, g_mix, g_ffn2, g_ple = (row(w[n]) for n in SMALL_NAMES[:4])
    qg = jnp.tile(row(w['q_norm']), (1, N_HEADS))
    kg = jnp.tile(row(w['k_norm']), (1, N_HEADS))
    bias = jnp.pad(row(w['forget_bias']), ((0, 0), (0, LANES - N_HEADS)))
    half = D_FF // 2
    grads = {}

    blk = lambda n: GATHER_OFF[n] // FFN_SHARD
    ffn1 = tuple(blk(n) for n in ("ffn1_w_gate", "ffn1_w_up", "ffn1_w_down"))
    ffn2 = tuple(blk(n) for n in ("ffn2_w_gate", "ffn2_w_up", "ffn2_w_down"))
    got0 = _gather_two_level(_pack(w, 0, BF16), "gather_ffn1")
    x1, g1, u1, h1, got1 = _ffn_fwd(x, g_ffn1, got0, ffn1, "ffn1_fwd", ride=(_pack(w, 1, BF16), True))
    w_in = _w_in_t_padded(got1)
    wbf = _gathered(got1, "w_branch_fox", (D_MODEL, ATT_W))
    wbs = _gathered(got1, "w_branch_sb", (D_MODEL, ATT_W))
    wo = _gathered(got1, "w_out", (D_MODEL, D_MODEL))
    (hmix, fqr, fkr, fqn, fkn, fv, logf, f_col, f_row, sq, sk, sv, gf, gs, kmax) = _mix_fwd(
        x1, g_mix, w_in, bias, qg, kg, "mix_fwd")
    y_fox, lse, got2 = _fox_fwd(fqn, fkn, fv, f_col, f_row, kmax, "fox_fwd", ride=(_pack(w, 2, BF16), True))
    wpg = _gathered(got2, "w_ple_gate", (D_MODEL, D_MODEL))
    wpp = _gathered(got2, "w_ple_proj", (D_MODEL, PLE_DIM))
    y_sb, y_sb32 = _sb_fwd(sq, sk, sv, "sb_fwd")
    x2 = _merge_fwd(x1, y_fox, y_sb, gf, gs, wbf, wbs, wo, "merge_fwd")
    x3, g2, u2, h2, = _ffn_fwd(x2, g_ffn2, got2, ffn2, "ffn2_fwd")
    dx3, n_ple, ds_ple, dpp, dg_ple, loss = _ple_loss(x3, p, tgt, g_ple, wpg, wpp, "ple_loss")

    grads['w_ple_gate'] = _wgrad(n_ple, ds_ple, "dw_ple_gate", D_MODEL, D_MODEL)
    grads['w_ple_proj'] = _wgrad(dpp, p, "dw_ple_proj", D_MODEL, PLE_DIM)
    dg2, du2, act2, dx2, dg_ffn2 = _ffn_bwd_fused(x2, dx3, g_ffn2, g2, u2, got2, ffn2, "ffn2_bwd")
    grads['ffn2_w_gate'] = _wgrad(dg2, h2, "dw_ffn2_gate", half, D_MODEL)
    grads['ffn2_w_up'] = _wgrad(du2, h2, "dw_ffn2_up", half, D_MODEL)
    grads['ffn2_w_down'] = _wgrad(act2, dx3, "dw_ffn2_down", half, D_MODEL)
    dyf, dys, dgf, dgs, dbf, dbs, merged = _merge_bwd(dx2, y_fox, y_sb, gf, gs, wbf, wbs, wo, "merge_bwd")
    grads['w_branch_fox'] = _wgrad(dbf, y_fox, "dw_branch_fox", D_MODEL, ATT_W)
    grads['w_branch_sb'] = _wgrad(dbs, y_sb, "dw_branch_sb", D_MODEL, ATT_W)
    grads['w_out'] = _wgrad(merged, dx2, "dw_out", D_MODEL, D_MODEL)
    dfqn, dfkn, dfv, dft, part2 = _fox_bwd(fqn, fkn, fv, dyf, y_fox, lse, f_col, f_row, kmax, "fox_bwd",
                                                ride=(_pack_chunks(grads, 2), False))
    dsq, dsk, dsv = _sb_bwd(sq, sk, sv, dys, y_sb32, "sb_bwd")
    s_len = x.shape[0]
    df_col = jnp.pad(dft[:, :2, :].reshape(N_HEADS, s_len).T, ((0, 0), (0, LANES - N_HEADS)))
    dproj, dx1, dg_mix, dqg, dkg, dbias = _mix_bwd(
        x1, dx2, g_mix, w_in, fqr, fkr, dfqn, dfkn, qg, kg, dfv, df_col, logf, dsq, dsk, dsv, dgf, dgs, "mix_bwd")
    grads['w_in'] = _wgrad(dproj, hmix, "dw_in", IN_PAD // 3, D_MODEL)
    dg1, du1, act1, part1 = _ffn_bwd_hidden(dx1, g1, u1, got0, ffn1[2], "ffn1_bwd_hidden",
                                            ride=(_pack_chunks(grads, 1), False))
    grads['ffn1_w_gate'] = _wgrad(dg1, h1, "dw_ffn1_gate", half, D_MODEL)
    grads['ffn1_w_up'] = _wgrad(du1, h1, "dw_ffn1_up", half, D_MODEL)
    grads['ffn1_w_down'] = _wgrad(act1, dx1, "dw_ffn1_down", half, D_MODEL)
    dx0, dg_ffn1, part0 = _ffn_bwd_input(x, dx1, g_ffn1, dg1, du1, got0, ffn1[:2], "ffn1_bwd_input",
                                         ride=(_pack_chunks(grads, 0), False))

    fold = lambda a: a.reshape(N_HEADS, HEAD_DIM).sum(axis=0).reshape(1, HEAD_DIM)
    small_g = {'ffn1_norm': dg_ffn1, 'mix_norm': dg_mix, 'ffn2_norm': dg_ffn2, 'ple_norm': dg_ple,
               'q_norm': fold(dqg), 'k_norm': fold(dkg), 'forget_bias': dbias[:, :N_HEADS]}
    return loss[0, 0], dx0, (part0, part1, part2), small_g


def kernel(x, p, ffn1_norm, ffn1_w_gate, ffn1_w_up, ffn1_w_down, mix_norm, w_in, forget_bias, q_norm, k_norm, w_branch_fox, w_branch_sb, w_out, ffn2_norm, ffn2_w_gate, ffn2_w_up, ffn2_w_down, ple_norm, w_ple_gate, w_ple_proj, loss_target, m_ffn1_norm, m_ffn1_w_gate, m_ffn1_w_up, m_ffn1_w_down, m_mix_norm, m_w_in, m_forget_bias, m_q_norm, m_k_norm, m_w_branch_fox, m_w_branch_sb, m_w_out, m_ffn2_norm, m_ffn2_w_gate, m_ffn2_w_up, m_ffn2_w_down, m_ple_norm, m_w_ple_gate, m_w_ple_proj, v_ffn1_norm, v_ffn1_w_gate, v_ffn1_w_up, v_ffn1_w_down, v_mix_norm, v_w_in, v_forget_bias, v_q_norm, v_k_norm, v_w_branch_fox, v_w_branch_sb, v_w_out, v_ffn2_norm, v_ffn2_w_gate, v_ffn2_w_up, v_ffn2_w_down, v_ple_norm, v_w_ple_gate, v_w_ple_proj):
    args = dict(locals())
    w = {n: args[n][0] for n in WEIGHT_NAMES}
    m = {n: args["m_" + n][0] for n in WEIGHT_NAMES}
    v = {n: args["v_" + n][0] for n in WEIGHT_NAMES}
    loss, dx, parts, small_g = _step(x[0], p[0, 0], loss_target[0], w)

    big = {}
    for grp, part in enumerate(parts):
        summed = _sum_parts(part, f"sum_grads_{grp}", SUM_TILE_ROWS[grp])
        for n in SCATTER_GROUPS[grp]:
            g = _shard_grad(summed, n, w[n].shape)
            big[n] = (g,) + tuple(_adamw_shard(g, w[n], m[n], v[n], "adamw_" + n))
    small_parts = _exchange(_pack_small(small_g, loss), "gather_small", True)
    sw, sm, sv = (_pack_small(t) for t in (w, m, v))
    small = _adamw(small_parts, sw, sm, sv, "adamw_small", SMALL_ROWS)

    outs = [small[0][4, LOSS_OFF], dx.reshape(x.shape)]
    for kind in range(4):
        for n in WEIGHT_NAMES:
            shape = args[n].shape
            outs.append(_unpack_small(small[kind], n, shape) if n in SMALL_NAMES else big[n][kind].reshape(shape))
    return tuple(outs)
```

```python
import jax
import jax.numpy as jnp
from jax import lax
from jax.experimental import pallas as pl
from jax.experimental.pallas import tpu as pltpu

F32 = jnp.float32
BF16 = jnp.bfloat16

D_MODEL = 1024
D_FF = 2816
N_HEADS = 8
HEAD_DIM = 64
ATT_W = N_HEADS * HEAD_DIM
PLE_DIM = 256
EPS = 1e-6
N_DEV = 8
MESH = pl.DeviceIdType.MESH

LANES = 128
V7X_SCOPED_VMEM_BYTES = 56 * 1024 * 1024

C_FQ, C_FK, C_FV, C_FL = 0, 512, 1024, 1536
C_SQ, C_SK, C_SV, C_GF, C_GS = 1792, 2304, 2816, 3328, 4352
IN_PAD = 5376
IN_REAL = 5128
FL_REAL_END = 1544

ADAM_LR = 0.001
ADAM_B1 = 0.9
ADAM_B2 = 0.999
ADAM_EPS = 1e-08
ADAM_WD = 0.01
ADAM_STEP = 10

PACK_ROWS = {"ffn1_w_gate": 352, "ffn1_w_up": 352, "ffn1_w_down": 352, "w_in": 656, "w_branch_fox": 64,
             "w_branch_sb": 64, "w_out": 128, "ffn2_w_gate": 352, "ffn2_w_up": 352, "ffn2_w_down": 352,
             "w_ple_gate": 128, "w_ple_proj": 32}
GATHER_GROUPS = (
    ("ffn1_w_gate", "ffn1_w_up", "ffn1_w_down"),
    ("w_in", "w_branch_fox", "w_branch_sb", "w_out"),
    ("ffn2_w_gate", "ffn2_w_up", "ffn2_w_down", "w_ple_gate", "w_ple_proj"),
)
SCATTER_GROUPS = (
    ("ffn1_w_gate",), ("ffn1_w_up",), ("ffn1_w_down",),
    ("w_in",),
    ("ffn2_w_gate", "ffn2_w_up", "ffn2_w_down", "w_ple_gate", "w_ple_proj", "w_branch_fox", "w_branch_sb", "w_out"),
)
SUM_TILE_ROWS = (352, 352, 352, 656, 368)


def _offsets(groups):
    off = {}
    for grp in groups:
        o = 0
        for n in grp:
            off[n] = o
            o += PACK_ROWS[n]
    return off


GATHER_OFF = _offsets(GATHER_GROUPS)
SCATTER_OFF = _offsets(SCATTER_GROUPS)
W_IN_ROWS = 641

SMALL_ROWS = 8


def _cparams(*sem):
    return pltpu.CompilerParams(dimension_semantics=sem, vmem_limit_bytes=V7X_SCOPED_VMEM_BYTES)


def _dot(a, b):
    return jnp.dot(a, b, preferred_element_type=F32)


def _dot_nt(a, b):
    return lax.dot_general(a, b, (((1,), (1,)), ((), ())), preferred_element_type=F32)


def _dot_tn(a, b):
    return lax.dot_general(a, b, (((0,), (0,)), ((), ())), preferred_element_type=F32)


def _split(x, parts):
    out = []
    r = x
    for _ in range(parts):
        p = r.astype(BF16)
        out.append(p)
        r = r - p.astype(F32)
    return out


def _dot_split(x, m, parts):
    acc = None
    for p in _split(x, parts):
        t = _dot(p, m)
        acc = t if acc is None else acc + t
    return acc


def _dot_split_left(m, x, parts):
    acc = None
    for p in _split(x, parts):
        t = _dot(m, p)
        acc = t if acc is None else acc + t
    return acc


def _rms_rinv(xf):
    return lax.rsqrt(jnp.mean(xf * xf, axis=-1, keepdims=True) + EPS)


def _sigmoid(x):
    return 1.0 / (1.0 + jnp.exp(-x))


def _softplus_neg_abs(z):
    return jnp.log(1.0 + jnp.exp(-jnp.abs(z)))


FFN_SHARD = D_FF // N_DEV
FFN_CHUNK = 4


def _ffn_w_spec(blk, index_map):
    return pl.BlockSpec((FFN_CHUNK, FFN_SHARD, D_MODEL), lambda *g: (index_map(*g), blk, 0))


def _ffn_w(ref):
    return ref[...].reshape(FFN_CHUNK * FFN_SHARD, D_MODEL)


def _ffn_fwd(x, gain, wbuf, blks, name, ride=None):
    s_len = x.shape[0]
    ts = min(512, s_len)
    fc = FFN_CHUNK * FFN_SHARD
    nt, nc = s_len // ts, D_FF // fc

    def body(x_ref, gain_ref, wg_ref, wu_ref, wd_ref, y_ref, g_ref, u_ref, h_ref, acc_scr):
        j = pl.program_id(1)

        @pl.when(j == 0)
        def _():
            xf = x_ref[...]
            h_ref[...] = ((xf * _rms_rinv(xf)) * gain_ref[...]).astype(BF16)
            acc_scr[...] = jnp.zeros_like(acc_scr)

        h = h_ref[...]
        g = _dot_nt(h, _ffn_w(wg_ref))
        u = _dot_nt(h, _ffn_w(wu_ref))
        g_ref[...] = g.astype(BF16)
        u_ref[...] = u.astype(BF16)
        a = (g * _sigmoid(g) * u).astype(BF16)
        acc_scr[...] += _dot(a, _ffn_w(wd_ref))

        @pl.when(j == nc - 1)
        def _():
            y_ref[...] = x_ref[...] + 0.5 * acc_scr[...]

    tok = pl.BlockSpec((ts, D_MODEL), lambda i, j: (i, 0))
    hid = pl.BlockSpec((ts, fc), lambda i, j: (i, j))
    return _ride_call(
        body, name, (nt, nc),
        [tok, pl.BlockSpec((1, D_MODEL), lambda i, j: (0, 0))] + [_ffn_w_spec(b, lambda i, j: j) for b in blks],
        [tok, hid, hid, tok],
        [jax.ShapeDtypeStruct((s_len, D_MODEL), F32), jax.ShapeDtypeStruct((s_len, D_FF), BF16),
         jax.ShapeDtypeStruct((s_len, D_FF), BF16), jax.ShapeDtypeStruct((s_len, D_MODEL), BF16)],
        [pltpu.VMEM((ts, D_MODEL), F32)], ("parallel", "arbitrary"), (x, gain, wbuf, wbuf, wbuf), ride)


def _ffn_bwd_hidden(dy, g, u, wbuf, blk, name, ride=None):
    s_len = dy.shape[0]
    ts = min(512, s_len)
    fc = FFN_CHUNK * FFN_SHARD
    nt, nc = s_len // ts, D_FF // fc

    def body(dy_ref, g_ref, u_ref, wd_ref, dg_ref, du_ref, act_ref):
        da = 0.5 * _dot_nt(dy_ref[...].astype(BF16), _ffn_w(wd_ref))
        gf = g_ref[...].astype(F32)
        uf = u_ref[...].astype(F32)
        sg = _sigmoid(gf)
        silu = gf * sg
        dg_ref[...] = (da * uf * (sg * (1.0 + gf * (1.0 - sg)))).astype(BF16)
        du_ref[...] = (da * silu).astype(BF16)
        act_ref[...] = (0.5 * silu * uf).astype(BF16)

    hid = pl.BlockSpec((ts, fc), lambda c, t: (t, c))
    return _ride_call(
        body, name, (nc, nt),
        [pl.BlockSpec((ts, D_MODEL), lambda c, t: (t, 0)), hid, hid, _ffn_w_spec(blk, lambda c, t: c)],
        [hid, hid, hid], [jax.ShapeDtypeStruct((s_len, D_FF), BF16)] * 3, [], ("parallel", "parallel"),
        (dy, g, u, wbuf), ride)


def _ffn_bwd_input(x, dy, gain, dg, du, wbuf, blks, name, ride=None):
    s_len = x.shape[0]
    ts = min(512, s_len)
    fc = FFN_CHUNK * FFN_SHARD
    nt, nc = s_len // ts, D_FF // fc

    def body(x_ref, dy_ref, gain_ref, dg_ref, du_ref, wg_ref, wu_ref, dx_ref, dgain_ref, acc):
        i = pl.program_id(0)
        j = pl.program_id(1)
        part = _dot(dg_ref[...], _ffn_w(wg_ref)) + _dot(du_ref[...], _ffn_w(wu_ref))

        @pl.when(j == 0)
        def _():
            acc[...] = part

        @pl.when(j > 0)
        def _():
            acc[...] += part

        @pl.when(j == nc - 1)
        def _():
            xf = x_ref[...]
            r = _rms_rinv(xf)
            xhat = xf * r
            dh = acc[...]
            dgp = jnp.sum(dh * xhat, axis=0, keepdims=True)

            @pl.when(i == 0)
            def _():
                dgain_ref[...] = dgp

            @pl.when(i > 0)
            def _():
                dgain_ref[...] += dgp

            dn = dh * gain_ref[...]
            dx_ref[...] = dy_ref[...] + r * (dn - xhat * jnp.mean(dn * xhat, axis=-1, keepdims=True))

    tok = pl.BlockSpec((ts, D_MODEL), lambda i, j: (i, 0))
    row = pl.BlockSpec((1, D_MODEL), lambda i, j: (0, 0))
    hid = pl.BlockSpec((ts, fc), lambda i, j: (i, j))
    return _ride_call(
        body, name, (nt, nc), [tok, tok, row, hid, hid] + [_ffn_w_spec(b, lambda i, j: j) for b in blks], [tok, row],
        [jax.ShapeDtypeStruct((s_len, D_MODEL), F32), jax.ShapeDtypeStruct((1, D_MODEL), F32)],
        [pltpu.VMEM((ts, D_MODEL), F32)], ("arbitrary", "arbitrary"), (x, dy, gain, dg, du, wbuf, wbuf), ride)


def _ffn_bwd_fused(x, dy, gain, g, u, wbuf, blks, name, ride=None):
    s_len = x.shape[0]
    ts = min(512, s_len)
    fc = FFN_CHUNK * FFN_SHARD
    nt = s_len // ts
    assert D_FF == 2 * fc

    def hidden(dy_ref, g_ref, u_ref, wg_ref, wu_ref, wd_ref, dg_ref, du_ref, act_ref):
        da = 0.5 * _dot_nt(dy_ref[...].astype(BF16), _ffn_w(wd_ref))
        gf = g_ref[...].astype(F32)
        uf = u_ref[...].astype(F32)
        sg = _sigmoid(gf)
        silu = gf * sg
        dg = (da * uf * (sg * (1.0 + gf * (1.0 - sg)))).astype(BF16)
        du = (da * silu).astype(BF16)
        dg_ref[...] = dg
        du_ref[...] = du
        act_ref[...] = (0.5 * silu * uf).astype(BF16)
        return _dot(dg, _ffn_w(wg_ref)) + _dot(du, _ffn_w(wu_ref))

    def first(dy_ref, g_ref, u_ref, wg_ref, wu_ref, wd_ref, dg_ref, du_ref, act_ref, dh_ref):
        dh_ref[...] = hidden(dy_ref, g_ref, u_ref, wg_ref, wu_ref, wd_ref, dg_ref, du_ref, act_ref)

    def second(x_ref, dy_ref, gain_ref, g_ref, u_ref, wg_ref, wu_ref, wd_ref, dh0_ref, dg_half, du_half, act_half,
               dg_ref, du_ref, act_ref, dx_ref, dgain_ref):
        i = pl.program_id(0)
        dh = dh0_ref[...] + hidden(dy_ref, g_ref, u_ref, wg_ref, wu_ref, wd_ref, dg_ref, du_ref, act_ref)
        xf = x_ref[...]
        r = _rms_rinv(xf)
        xhat = xf * r
        dgp = jnp.sum(dh * xhat, axis=0, keepdims=True)

        @pl.when(i == 0)
        def _():
            dgain_ref[...] = dgp

        @pl.when(i > 0)
        def _():
            dgain_ref[...] += dgp

        dn = dh * gain_ref[...]
        dx_ref[...] = dy_ref[...] + r * (dn - xhat * jnp.mean(dn * xhat, axis=-1, keepdims=True))

    tok = pl.BlockSpec((ts, D_MODEL), lambda i: (i, 0))
    row = pl.BlockSpec((1, D_MODEL), lambda i: (0, 0))
    hid = lambda c: pl.BlockSpec((ts, fc), lambda i: (i, c))
    wts = lambda c: [pl.BlockSpec((FFN_CHUNK, FFN_SHARD, D_MODEL), lambda i, b=b: (c, b, 0),
                                  pipeline_mode=pl.Buffered(1)) for b in blks]
    hidden_shapes = [jax.ShapeDtypeStruct((s_len, D_FF), BF16)] * 3
    dg, du, act, dh0, *rode = _ride_call(
        first, name + "_a", (nt,), [tok, hid(0), hid(0)] + wts(0), [hid(0)] * 3 + [tok],
        hidden_shapes + [jax.ShapeDtypeStruct((s_len, D_MODEL), F32)], [], ("parallel",),
        (dy, g, u, wbuf, wbuf, wbuf), ride)
    filled = pl.BlockSpec(memory_space=pl.ANY)
    dg, du, act, dx, dgain = pl.pallas_call(
        second, name=name + "_b", grid=(nt,),
        in_specs=[tok, tok, row, hid(1), hid(1)] + wts(1) + [tok, filled, filled, filled],
        out_specs=[hid(1)] * 3 + [tok, row],
        out_shape=hidden_shapes + [jax.ShapeDtypeStruct((s_len, D_MODEL), F32), jax.ShapeDtypeStruct((1, D_MODEL), F32)],
        input_output_aliases={9: 0, 10: 1, 11: 2},
        compiler_params=_cparams("arbitrary"),
    )(x, dy, gain, g, u, wbuf, wbuf, wbuf, dh0, dg, du, act)
    return (dg, du, act, dx, dgain, *rode)


def _wgrad(a, b, name, tk, tn, ride=None):
    s_len, k_dim = a.shape
    n_dim = b.shape[1]
    ts = min(1024, s_len)
    ns = s_len // ts

    def body(a_ref, b_ref, o_ref, acc):
        s = pl.program_id(2)
        p = _dot_tn(a_ref[...].astype(BF16), b_ref[...].astype(BF16))

        @pl.when(s == 0)
        def _():
            acc[...] = p

        @pl.when(s > 0)
        def _():
            acc[...] += p

        @pl.when(s == ns - 1)
        def _():
            o_ref[...] = acc[...].astype(BF16)

    out = _ride_call(
        body, name, (k_dim // tk, n_dim // tn, ns),
        [pl.BlockSpec((ts, tk), lambda k, n, s: (s, k)), pl.BlockSpec((ts, tn), lambda k, n, s: (s, n))],
        [pl.BlockSpec((tk, tn), lambda k, n, s: (k, n))], [jax.ShapeDtypeStruct((k_dim, n_dim), BF16)],
        [pltpu.VMEM((tk, tn), F32)], ("parallel", "parallel", "arbitrary"), (a, b), ride)
    return out[0] if ride is None else tuple(out)


HEAD_SUM_PARTS = 2


def _head_group_matrix():
    r = lax.broadcasted_iota(jnp.int32, (ATT_W, ATT_W), 0) // HEAD_DIM
    c = lax.broadcasted_iota(jnp.int32, (ATT_W, ATT_W), 1) // HEAD_DIM
    return (r == c).astype(BF16)


def _mix_fwd(x, gain, w_in, bias, qg, kg, name, ride=None):
    s_len = x.shape[0]
    ts = min(256, s_len)
    nt = s_len // ts
    gmat = _head_group_matrix()

    def body(x_ref, gain_ref, w_ref, bias_ref, qg_ref, kg_ref, gm_ref,
             h_ref, fqr_ref, fkr_ref, fqn_ref, fkn_ref, fv_ref, logf_ref, f_ref, ft_ref,
             sq_ref, sk_ref, sv_ref, gf_ref, gs_ref, kmax_ref, carry):
        i = pl.program_id(0)
        xf = x_ref[...]
        h = ((xf * _rms_rinv(xf)) * gain_ref[...]).astype(BF16)
        h_ref[...] = h
        gm = gm_ref[...]

        def proj(lo, n):
            return _dot_nt(h, w_ref[lo:lo + n, :])

        def headnorm(raw, g):
            ms = _dot_split(raw * raw, gm, HEAD_SUM_PARTS) * (1.0 / HEAD_DIM)
            return ((raw * lax.rsqrt(ms + EPS)) * g).astype(BF16)

        fq = proj(C_FQ, ATT_W)
        fqr_ref[...] = fq
        fqn_ref[...] = headnorm(fq, qg_ref[...])
        fk = proj(C_FK, ATT_W)
        fkr_ref[...] = fk
        fkn = headnorm(fk, kg_ref[...])
        fkn_ref[...] = fkn
        kn2 = jnp.max(_dot_split(jnp.square(fkn.astype(F32)), gm, HEAD_SUM_PARTS), axis=0, keepdims=True)

        @pl.when(i == 0)
        def _():
            kmax_ref[...] = kn2

        @pl.when(i > 0)
        def _():
            kmax_ref[...] = jnp.maximum(kmax_ref[...], kn2)
        fv_ref[...] = proj(C_FV, ATT_W).astype(BF16)
        sq_ref[...] = proj(C_SQ, ATT_W).astype(BF16)
        sk_ref[...] = proj(C_SK, ATT_W).astype(BF16)
        sv_ref[...] = proj(C_SV, ATT_W).astype(BF16)
        gf_ref[...] = proj(C_GF, D_MODEL)
        gs_ref[...] = proj(C_GS, D_MODEL)

        fl = proj(C_FL, LANES) + bias_ref[...]
        lane = lax.broadcasted_iota(jnp.int32, fl.shape, 1)
        logf = jnp.where(lane < N_HEADS, jnp.minimum(fl, 0.0) - _softplus_neg_abs(fl), 0.0)
        logf_ref[...] = logf

        @pl.when(i == 0)
        def _():
            carry[...] = jnp.zeros_like(carry)

        r = lax.broadcasted_iota(jnp.int32, (ts, ts), 0)
        c = lax.broadcasted_iota(jnp.int32, (ts, ts), 1)
        tri = (r >= c).astype(BF16)
        f_tile = _dot_split_left(tri, logf, 3) + carry[...]
        f_ref[...] = f_tile
        ft_ref[...] = f_tile.T[:N_HEADS, :]
        carry[...] = f_tile[ts - 1:ts, :]

    tok = lambda w: pl.BlockSpec((ts, w), lambda i: (i, 0))
    full = lambda a: pl.BlockSpec(a.shape, lambda i: (0, 0))
    f32o = lambda w: jax.ShapeDtypeStruct((s_len, w), F32)
    b16o = lambda w: jax.ShapeDtypeStruct((s_len, w), BF16)
    return _ride_call(
        body, name, (nt,),
        [tok(D_MODEL), full(gain), full(w_in), full(bias), full(qg), full(kg), full(gmat)],
        [
            tok(D_MODEL), tok(ATT_W), tok(ATT_W), tok(ATT_W), tok(ATT_W), tok(ATT_W), tok(LANES), tok(LANES),
            pl.BlockSpec((N_HEADS, ts), lambda i: (0, i)),
            tok(ATT_W), tok(ATT_W), tok(ATT_W), tok(D_MODEL), tok(D_MODEL),
            pl.BlockSpec((1, ATT_W), lambda i: (0, 0)),
        ],
        [
            b16o(D_MODEL), f32o(ATT_W), f32o(ATT_W), b16o(ATT_W), b16o(ATT_W), b16o(ATT_W), f32o(LANES), f32o(LANES),
            jax.ShapeDtypeStruct((N_HEADS, s_len), F32),
            b16o(ATT_W), b16o(ATT_W), b16o(ATT_W), f32o(D_MODEL), f32o(D_MODEL),
            jax.ShapeDtypeStruct((1, ATT_W), F32),
        ],
        [pltpu.VMEM((1, LANES), F32)], ("arbitrary",), (x, gain, w_in, bias, qg, kg, gmat), ride)


ATT_T = 256
ATT_ROWS = 2
EXP_ZERO = 88.0


def _att_tiling(s_len):
    t = min(ATT_T, s_len)
    nr = min(ATT_ROWS, s_len // t)
    return t, nr, s_len // (t * nr)


def _pair_specs(s_len, tq):
    qblk = pl.BlockSpec((tq, LANES), lambda hp, i: (i, hp))
    kvfull = pl.BlockSpec((s_len, LANES), lambda hp, i: (0, hp))
    return qblk, kvfull


def _walk_tiles(i, nr, load, sub, flush, init, more=None, trips=None):
    base = i * nr
    carries = list(init)
    for kk in range(nr - 1, -1, -1):
        rs = list(range(kk, nr))
        new, side = sub(rs, load(base + kk), [carries[r] for r in rs], [r == kk for r in rs])
        carries[kk:] = new
        flush(base + kk, side)

    done = jnp.int32(0)
    for last in range(nr - 1, -1, -1):
        rs = list(range(last + 1))

        def visit(n, cs, rs=rs):
            kb = base - 1 - n
            cs, side = sub(rs, load(kb), list(cs), [False] * len(rs))
            flush(kb, side)
            return tuple(cs)

        if trips is not None:
            todo = jnp.maximum(trips(carries, base, last) - done, 0)
            new = lax.fori_loop(0, todo, lambda it, cs, start=done, visit=visit: visit(start + it, cs),
                                tuple(carries[:last + 1]))
            done = done + todo
        else:
            def step(state, visit=visit, last=last):
                n, _, cs = state
                cs = visit(n, cs)
                return n + 1, more(cs[last]), cs

            done, _, new = lax.while_loop(lambda state: jnp.logical_and(state[0] < base, state[1] > 0), step,
                                          (done, more(carries[last]), tuple(carries[:last + 1])))
        carries[:last + 1] = list(new)
    return carries


def _stack(parts):
    return parts[0] if len(parts) == 1 else jnp.concatenate(parts, axis=0)


def _stacked_halves(x, lo):
    z = jnp.zeros_like(x)
    return jnp.concatenate([jnp.where(lo, x, z), jnp.where(lo, z, x)], axis=0)


def _fox_qk_bound(qst_r, km_ref, t):
    km = km_ref[...]
    out = []
    for j in (0, 1):
        qf = qst_r[j * t:(j + 1) * t, :].astype(F32)
        qn = jnp.sqrt(jnp.sum(qf * qf, axis=1, keepdims=True))
        out.append(qn * jnp.sqrt(km[:, j * HEAD_DIM:j * HEAD_DIM + 1]) * 1.001 + 1.0)
    return out


def _fox_trips(hp, flast_ref, qkb, fq, level):
    def trips(carries, base, r):
        gap = [jnp.max(qkb[r][j] + fq[r][j] - level(carries, r, j)) for j in (0, 1)]

        def needed(n):
            kb = jnp.maximum(base - 1 - n, 0)
            return jnp.logical_or(gap[0] - flast_ref[2 * hp, kb] > -EXP_ZERO,
                                  gap[1] - flast_ref[2 * hp + 1, kb] > -EXP_ZERO)

        return lax.while_loop(lambda n: jnp.logical_and(n < base, needed(n)), lambda n: n + 1, jnp.int32(0))
    return trips


def _fox_fwd(q, k, v, f_col, f_row, kmax, name, ride=None):
    s_len = q.shape[0]
    t, nr, nq = _att_tiling(s_len)

    def body(q_ref, k_ref, v_ref, f_ref, ft_ref, km_ref, fl_ref, y_ref, lse_ref):
        hp = pl.program_id(0)
        i = pl.program_id(1)
        lane = lax.broadcasted_iota(jnp.int32, (t, LANES), 1)
        lo = lane < HEAD_DIM
        causal = lax.broadcasted_iota(jnp.int32, (t, t), 0) >= lax.broadcasted_iota(jnp.int32, (t, t), 1)
        rows = [pl.ds(r * t, t) for r in range(nr)]
        qst = [_stacked_halves(q_ref[rw, :] * jnp.asarray(HEAD_DIM ** -0.5, BF16), lo) for rw in rows]
        q_all = _stack(qst)
        fq = [[jnp.sum(jnp.where(lane == 2 * hp + j, f_ref[rw, :], 0.0), axis=1, keepdims=True) for j in (0, 1)]
              for rw in rows]

        def load(kb):
            k0 = pl.multiple_of(kb * t, t)
            frow = [ft_ref[pl.ds(2 * hp + j, 1), pl.ds(k0, t)] for j in (0, 1)]
            return k_ref[pl.ds(k0, t), :], v_ref[pl.ds(k0, t), :], frow

        def sub(rs, tiles, carries, masked):
            kblk, vblk, frow = tiles
            z = _dot_nt(q_all if len(rs) == nr else _stack([qst[r] for r in rs]), kblk)
            ps, stats = [], []
            for n, (r, j) in enumerate((r, j) for r in range(len(rs)) for j in (0, 1)):
                m, l, _ = carries[r]
                s = z[n * t:(n + 1) * t, :] + (fq[rs[r]][j] - frow[j])
                if masked[r]:
                    s = jnp.where(causal, s, -1e30)
                mj = jnp.maximum(m[j], jnp.max(s, axis=1, keepdims=True))
                aj = jnp.exp(m[j] - mj)
                p = jnp.exp(s - mj)
                stats.append((mj, aj, aj * l[j] + jnp.sum(p, axis=1, keepdims=True)))
                ps.append(p.astype(BF16))
            pv = _dot(_stack(ps), vblk)
            out = []
            for r in range(len(rs)):
                (m0, a0, l0), (m1, a1, l1) = stats[2 * r], stats[2 * r + 1]
                acc = carries[r][2]
                acc = (acc[0] * a0 + pv[2 * r * t:(2 * r + 1) * t, :], acc[1] * a1 + pv[(2 * r + 1) * t:(2 * r + 2) * t, :])
                out.append(((m0, m1), (l0, l1), acc))
            return out, None

        neg = jnp.full((t, 1), -1e30, F32)
        zero = jnp.zeros((t, 1), F32)
        zacc = jnp.zeros((t, LANES), F32)
        init = [((neg, neg), (zero, zero), (zacc, zacc))] * nr
        qkb = [_fox_qk_bound(qs, km_ref, t) for qs in qst]
        trips = _fox_trips(hp, fl_ref, qkb, fq, lambda carries, r, j: carries[r][0][j])
        out = _walk_tiles(i, nr, load, sub, lambda kb, side: None, init, trips=trips)
        for rw, (m, l, acc) in zip(rows, out):
            y_ref[rw, :] = jnp.where(lo, acc[0] / l[0], acc[1] / l[1]).astype(BF16)
            lse_ref[0, rw, :] = jnp.where(lo, m[0] + jnp.log(l[0]), m[1] + jnp.log(l[1]))

    qblk, kvfull = _pair_specs(s_len, t * nr)
    return _ride_call(
        body, name, (N_HEADS // 2, nq),
        [qblk, kvfull, kvfull,
         pl.BlockSpec((t * nr, LANES), lambda hp, i: (i, 0)),
         pl.BlockSpec((N_HEADS, s_len), lambda hp, i: (0, 0)),
         pl.BlockSpec((1, LANES), lambda hp, i: (0, hp)),
         pl.BlockSpec(memory_space=pltpu.SMEM)],
        [qblk, pl.BlockSpec((1, t * nr, LANES), lambda hp, i: (hp, i, 0))],
        [jax.ShapeDtypeStruct((s_len, ATT_W), BF16), jax.ShapeDtypeStruct((N_HEADS // 2, s_len, LANES), F32)],
        [], ("parallel", "parallel"), (q, k, v, f_col, f_row, kmax, f_row[:, t - 1::t]), ride)


def _fox_bwd(q, k, v, dy, y, lse, f_col, f_row, kmax, name, ride=None):
    s_len = q.shape[0]
    t, nr, nq = _att_tiling(s_len)

    def body(q_ref, k_ref, v_ref, dy_ref, y_ref, lse_ref, f_ref, ft_ref, km_ref, fl_ref,
             dq_ref, dk_ref, dv_ref, dft_ref):
        hp = pl.program_id(0)
        i = pl.program_id(1)

        @pl.when(i == 0)
        def _():
            dk_ref[...] = jnp.zeros_like(dk_ref)
            dv_ref[...] = jnp.zeros_like(dv_ref)
            dft_ref[...] = jnp.zeros_like(dft_ref)

        lane = lax.broadcasted_iota(jnp.int32, (t, LANES), 1)
        lo = lane < HEAD_DIM
        causal = lax.broadcasted_iota(jnp.int32, (t, t), 0) >= lax.broadcasted_iota(jnp.int32, (t, t), 1)
        rows = [pl.ds(r * t, t) for r in range(nr)]
        qst, dyst, delta, lse, fq = [], [], [], [], []
        for rw in rows:
            qst.append(_stacked_halves(q_ref[rw, :] * jnp.asarray(HEAD_DIM ** -0.5, BF16), lo))
            dyb = dy_ref[rw, :]
            dyst.append(_stacked_halves(dyb, lo))
            prod = dyb.astype(F32) * y_ref[rw, :].astype(F32)
            delta.append([jnp.sum(jnp.where(lo, prod, 0.0), axis=1, keepdims=True),
                          jnp.sum(jnp.where(lo, 0.0, prod), axis=1, keepdims=True)])
            lse_b = lse_ref[0, rw, :]
            lse.append([lse_b[:, 0:1], lse_b[:, HEAD_DIM:HEAD_DIM + 1]])
            fq.append([jnp.sum(jnp.where(lane == 2 * hp + j, f_ref[rw, :], 0.0), axis=1, keepdims=True)
                       for j in (0, 1)])

        q_all, dy_all = _stack(qst), _stack(dyst)

        def load(kb):
            k0 = pl.multiple_of(kb * t, t)
            frow = [ft_ref[pl.ds(2 * hp + j, 1), pl.ds(k0, t)] for j in (0, 1)]
            return k_ref[pl.ds(k0, t), :], v_ref[pl.ds(k0, t), :], frow

        def sub(rs, tiles, carries, masked):
            kblk, vblk, frow = tiles
            qs, dys = (q_all, dy_all) if len(rs) == nr else (_stack([qst[r] for r in rs]), _stack([dyst[r] for r in rs]))
            z = _dot_nt(qs, kblk)
            dp = _dot_nt(dys, vblk)
            pb, dsb, rsum, col = [], [], [], [None, None]
            for n, (r, j) in enumerate((r, j) for r in range(len(rs)) for j in (0, 1)):
                sl = slice(n * t, (n + 1) * t)
                s = z[sl, :] + (fq[rs[r]][j] - frow[j])
                p = jnp.exp(s - lse[rs[r]][j])
                if masked[r]:
                    p = jnp.where(causal, p, 0.0)
                ds = p * (dp[sl, :] - delta[rs[r]][j])
                c = jnp.sum(ds, axis=0, keepdims=True)
                col[j] = c if col[j] is None else col[j] + c
                rsum.append(carries[r][1][j] + jnp.sum(ds, axis=1, keepdims=True))
                pb.append(p.astype(BF16))
                dsb.append(ds.astype(BF16))
            p_all, ds_all = _stack(pb), _stack(dsb)
            dqs = _dot(ds_all, kblk)
            out = []
            for r in range(len(rs)):
                dq = carries[r][0]
                dq = (dq[0] + dqs[2 * r * t:(2 * r + 1) * t, :], dq[1] + dqs[(2 * r + 1) * t:(2 * r + 2) * t, :])
                out.append((dq, (rsum[2 * r], rsum[2 * r + 1])))
            return out, (_dot_tn(ds_all, qs), _dot_tn(p_all, dys), col)

        def flush(kb, side):
            k0 = pl.multiple_of(kb * t, t)
            dk_ref[pl.ds(k0, t), :] += side[0]
            dv_ref[pl.ds(k0, t), :] += side[1]
            for j in (0, 1):
                dft_ref[0, pl.ds(j, 1), pl.ds(k0, t)] -= side[2][j]

        zero = jnp.zeros((t, 1), F32)
        zacc = jnp.zeros((t, LANES), F32)
        qkb = [_fox_qk_bound(qs, km_ref, t) for qs in qst]
        trips = _fox_trips(hp, fl_ref, qkb, fq, lambda carries, r, j: lse[r][j])
        out = _walk_tiles(i, nr, load, sub, flush, [((zacc, zacc), (zero, zero))] * nr, trips=trips)
        for r, (rw, (dq, rs)) in enumerate(zip(rows, out)):
            dq_ref[rw, :] = jnp.where(lo, dq[0], dq[1]) * (HEAD_DIM ** -0.5)
            rs_t = jnp.where(lo, rs[0], rs[1]).T
            q0 = pl.multiple_of((i * nr + r) * t, t)
            for j in (0, 1):
                dft_ref[0, pl.ds(j, 1), pl.ds(q0, t)] += rs_t[j * HEAD_DIM:j * HEAD_DIM + 1, :]

    qblk, kvfull = _pair_specs(s_len, t * nr)
    return _ride_call(
        body, name, (N_HEADS // 2, nq),
        [qblk, kvfull, kvfull, qblk, qblk,
         pl.BlockSpec((1, t * nr, LANES), lambda hp, i: (hp, i, 0)),
         pl.BlockSpec((t * nr, LANES), lambda hp, i: (i, 0)),
         pl.BlockSpec((N_HEADS, s_len), lambda hp, i: (0, 0)),
         pl.BlockSpec((1, LANES), lambda hp, i: (0, hp)),
         pl.BlockSpec(memory_space=pltpu.SMEM)],
        [qblk, kvfull, kvfull, pl.BlockSpec((1, 8, s_len), lambda hp, i: (hp, 0, 0))],
        [jax.ShapeDtypeStruct((s_len, ATT_W), F32)] * 3 + [jax.ShapeDtypeStruct((N_HEADS // 2, 8, s_len), F32)],
        [], ("arbitrary", "arbitrary"), (q, k, v, dy, y, lse, f_col, f_row, kmax, f_row[:, t - 1::t]), ride)


def _sb_more(carry):
    return (jnp.max(jnp.maximum(carry[0][0], carry[0][1])) > -EXP_ZERO).astype(jnp.int32)


def _stacked_split_dot(slabs, m, parts):
    split = [_split(x, parts) for x in slabs]
    acc = None
    for p in range(parts):
        d = _dot(_stack([s[p] for s in split]), m)
        acc = d if acc is None else acc + d
    return acc


def _sb_weights(z, c, strict, upper, t):
    logs = []
    for n in range(z.shape[0] // t):
        zn = z[n * t:(n + 1) * t, :]
        sp = _softplus_neg_abs(zn)
        l1m = jnp.minimum(-zn, 0.0) - sp
        if strict[n] is not None:
            l1m = jnp.where(strict[n], l1m, 0.0)
        logs.append((jnp.minimum(zn, 0.0) - sp, l1m))
    suf = _stacked_split_dot([l1m for _, l1m in logs], upper, 2)
    out = []
    for n, (logb, l1m) in enumerate(logs):
        after = c[n] + suf[n * t:(n + 1) * t, :]
        a = jnp.exp(logb + after)
        if strict[n] is not None:
            a = jnp.where(strict[n], a, 0.0)
        out.append((logb, a, after[:, 0:1] + l1m[:, 0:1]))
    return out


def _sb_fwd(q, k, v, name):
    s_len = q.shape[0]
    t, nr, nq = _att_tiling(s_len)

    def body(q_ref, k_ref, v_ref, y_ref, yf_ref):
        i = pl.program_id(1)
        lane = lax.broadcasted_iota(jnp.int32, (t, LANES), 1)
        lo = lane < HEAD_DIM
        ri = lax.broadcasted_iota(jnp.int32, (t, t), 0)
        ci = lax.broadcasted_iota(jnp.int32, (t, t), 1)
        strict = ci < ri
        upper = (ri > ci).astype(BF16)
        rows = [pl.ds(r * t, t) for r in range(nr)]
        qst = [_stacked_halves(q_ref[rw, :] * jnp.asarray(HEAD_DIM ** -0.5, BF16), lo) for rw in rows]
        q_all = _stack(qst)

        def load(kb):
            k0 = pl.multiple_of(kb * t, t)
            return k_ref[pl.ds(k0, t), :], v_ref[pl.ds(k0, t), :]

        def sub(rs, tiles, carries, masked):
            kblk, vblk = tiles
            z = _dot_nt(q_all if len(rs) == nr else _stack([qst[r] for r in rs]), kblk)
            c = [carries[r][0][j] for r in range(len(rs)) for j in (0, 1)]
            w = _sb_weights(z, c, [strict if masked[r] else None for r in range(len(rs)) for j in (0, 1)], upper, t)
            pv = _dot(_stack([a.astype(BF16) for _, a, _ in w]), vblk)
            out = []
            for r in range(len(rs)):
                acc = carries[r][1]
                acc = (acc[0] + pv[2 * r * t:(2 * r + 1) * t, :], acc[1] + pv[(2 * r + 1) * t:(2 * r + 2) * t, :])
                out.append(((w[2 * r][2], w[2 * r + 1][2]), acc))
            return out, None

        zero = jnp.zeros((t, 1), F32)
        zacc = jnp.zeros((t, LANES), F32)
        out = _walk_tiles(i, nr, load, sub, lambda kb, side: None, [((zero, zero), (zacc, zacc))] * nr, more=_sb_more)
        for rw, (_, acc) in zip(rows, out):
            y = jnp.where(lo, acc[0], acc[1])
            y_ref[rw, :] = y.astype(BF16)
            yf_ref[rw, :] = y

    qblk, kvfull = _pair_specs(s_len, t * nr)
    return pl.pallas_call(
        body, name=name, grid=(N_HEADS // 2, nq),
        in_specs=[qblk, kvfull, kvfull],
        out_specs=[qblk, qblk],
        out_shape=[jax.ShapeDtypeStruct((s_len, ATT_W), BF16), jax.ShapeDtypeStruct((s_len, ATT_W), F32)],
        compiler_params=_cparams("parallel", "parallel"),
    )(q, k, v)


def _sb_bwd(q, k, v, dy, yf, name):
    s_len = q.shape[0]
    t, nr, nq = _att_tiling(s_len)

    def body(q_ref, k_ref, v_ref, dy_ref, yf_ref, dq_ref, dk_ref, dv_ref):
        i = pl.program_id(1)

        @pl.when(i == 0)
        def _():
            dk_ref[...] = jnp.zeros_like(dk_ref)
            dv_ref[...] = jnp.zeros_like(dv_ref)

        lane = lax.broadcasted_iota(jnp.int32, (t, LANES), 1)
        lo = lane < HEAD_DIM
        ri = lax.broadcasted_iota(jnp.int32, (t, t), 0)
        ci = lax.broadcasted_iota(jnp.int32, (t, t), 1)
        strict = ci < ri
        upper = (ri > ci).astype(BF16)
        upper_incl = (ri >= ci).astype(BF16)
        rows = [pl.ds(r * t, t) for r in range(nr)]
        qst, dyst, delta = [], [], []
        for rw in rows:
            qst.append(_stacked_halves(q_ref[rw, :] * jnp.asarray(HEAD_DIM ** -0.5, BF16), lo))
            dyb = dy_ref[rw, :]
            dyst.append(_stacked_halves(dyb, lo))
            prod = dyb.astype(F32) * yf_ref[rw, :]
            delta.append([jnp.sum(jnp.where(lo, prod, 0.0), axis=1, keepdims=True),
                          jnp.sum(jnp.where(lo, 0.0, prod), axis=1, keepdims=True)])
        q_all, dy_all = _stack(qst), _stack(dyst)

        def load(kb):
            k0 = pl.multiple_of(kb * t, t)
            return k_ref[pl.ds(k0, t), :], v_ref[pl.ds(k0, t), :]

        def sub(rs, tiles, carries, masked):
            kblk, vblk = tiles
            qs, dys = (q_all, dy_all) if len(rs) == nr else (_stack([qst[r] for r in rs]), _stack([dyst[r] for r in rs]))
            slabs = [(r, j) for r in range(len(rs)) for j in (0, 1)]
            z = _dot_nt(qs, kblk)
            w = _sb_weights(z, [carries[r][0][j] for r, j in slabs], [strict if masked[r] else None for r, j in slabs],
                            upper, t)
            da = _dot_nt(dys, vblk)
            ab = [a.astype(BF16) for _, a, _ in w]
            dl = [ab[n].astype(F32) * da[n * t:(n + 1) * t, :] for n in range(len(slabs))]
            tail = _stacked_split_dot(dl, upper_incl, 2)
            dzb, e_new = [], []
            for n, (r, j) in enumerate(slabs):
                tl = tail[n * t:(n + 1) * t, :]
                e = carries[r][1][j]
                dl1m = (delta[rs[r]][j] - e) - tl
                e_new.append(e + tl[:, 0:1])
                dz = dl[n] - jnp.exp(w[n][0]) * (dl[n] + dl1m)
                if masked[r]:
                    dz = jnp.where(strict, dz, 0.0)
                dzb.append(dz.astype(BF16))
            a_all, dz_all = _stack(ab), _stack(dzb)
            dqs = _dot(dz_all, kblk)
            out = []
            for r in range(len(rs)):
                dq = carries[r][2]
                dq = (dq[0] + dqs[2 * r * t:(2 * r + 1) * t, :], dq[1] + dqs[(2 * r + 1) * t:(2 * r + 2) * t, :])
                out.append(((w[2 * r][2], w[2 * r + 1][2]), (e_new[2 * r], e_new[2 * r + 1]), dq))
            return out, (_dot_tn(dz_all, qs), _dot_tn(a_all, dys))

        def flush(kb, side):
            k0 = pl.multiple_of(kb * t, t)
            dk_ref[pl.ds(k0, t), :] += side[0]
            dv_ref[pl.ds(k0, t), :] += side[1]

        zero = jnp.zeros((t, 1), F32)
        zacc = jnp.zeros((t, LANES), F32)
        out = _walk_tiles(i, nr, load, sub, flush, [((zero, zero), (zero, zero), (zacc, zacc))] * nr, more=_sb_more)
        for rw, (_, _, dq) in zip(rows, out):
            dq_ref[rw, :] = jnp.where(lo, dq[0], dq[1]) * (HEAD_DIM ** -0.5)

    qblk, kvfull = _pair_specs(s_len, t * nr)
    return pl.pallas_call(
        body, name=name, grid=(N_HEADS // 2, nq),
        in_specs=[qblk, kvfull, kvfull, qblk, qblk],
        out_specs=[qblk, kvfull, kvfull],
        out_shape=[jax.ShapeDtypeStruct((s_len, ATT_W), F32)] * 3,
        compiler_params=_cparams("arbitrary", "arbitrary"),
    )(q, k, v, dy, yf)


def _merge_fwd(x, yf, ys, gf, gs, wbf, wbs, wo, name):
    s_len = x.shape[0]
    ts = min(512, s_len)

    def body(x_ref, yf_ref, ys_ref, gf_ref, gs_ref, wbf_ref, wbs_ref, wo_ref, o_ref):
        merged = (_sigmoid(gf_ref[...]) * _dot_nt(yf_ref[...], wbf_ref[...])
                  + _sigmoid(gs_ref[...]) * _dot_nt(ys_ref[...], wbs_ref[...]))
        o_ref[...] = x_ref[...] + _dot(merged.astype(BF16), wo_ref[...])

    tok = lambda w: pl.BlockSpec((ts, w), lambda i: (i, 0))
    full = lambda a: pl.BlockSpec(a.shape, lambda i: (0, 0))
    return pl.pallas_call(
        body, name=name, grid=(s_len // ts,),
        in_specs=[tok(D_MODEL), tok(ATT_W), tok(ATT_W), tok(D_MODEL), tok(D_MODEL), full(wbf), full(wbs), full(wo)],
        out_specs=tok(D_MODEL),
        out_shape=jax.ShapeDtypeStruct((s_len, D_MODEL), F32),
        compiler_params=_cparams("parallel"),
    )(x, yf, ys, gf, gs, wbf, wbs, wo)


def _merge_bwd(dx, yf, ys, gf, gs, wbf, wbs, wo, name):
    s_len = dx.shape[0]
    ts = min(512, s_len)

    def body(dx_ref, yf_ref, ys_ref, gf_ref, gs_ref, wbf_ref, wbs_ref, wo_ref,
             dyf_ref, dys_ref, dgf_ref, dgs_ref, dbf_ref, dbs_ref, mg_ref):
        bf = _dot_nt(yf_ref[...], wbf_ref[...])
        bs = _dot_nt(ys_ref[...], wbs_ref[...])
        sf = _sigmoid(gf_ref[...])
        ss = _sigmoid(gs_ref[...])
        mg_ref[...] = (sf * bf + ss * bs).astype(BF16)
        dm = _dot_nt(dx_ref[...].astype(BF16), wo_ref[...])
        dbf = (dm * sf).astype(BF16)
        dbs = (dm * ss).astype(BF16)
        dbf_ref[...] = dbf
        dbs_ref[...] = dbs
        dgf_ref[...] = (dm * bf * (sf * (1.0 - sf))).astype(BF16)
        dgs_ref[...] = (dm * bs * (ss * (1.0 - ss))).astype(BF16)
        dyf_ref[...] = _dot(dbf, wbf_ref[...]).astype(BF16)
        dys_ref[...] = _dot(dbs, wbs_ref[...]).astype(BF16)

    tok = lambda w: pl.BlockSpec((ts, w), lambda i: (i, 0))
    full = lambda a: pl.BlockSpec(a.shape, lambda i: (0, 0))
    b16o = lambda w: jax.ShapeDtypeStruct((s_len, w), BF16)
    return pl.pallas_call(
        body, name=name, grid=(s_len // ts,),
        in_specs=[tok(D_MODEL), tok(ATT_W), tok(ATT_W), tok(D_MODEL), tok(D_MODEL), full(wbf), full(wbs), full(wo)],
        out_specs=[tok(ATT_W), tok(ATT_W)] + [tok(D_MODEL)] * 5,
        out_shape=[b16o(ATT_W), b16o(ATT_W)] + [b16o(D_MODEL)] * 5,
        compiler_params=_cparams("parallel"),
    )(dx, yf, ys, gf, gs, wbf, wbs, wo)


def _mix_bwd(x, dx_in, gain, w_in, fqr, fkr, dfqn, dfkn, qg, kg, dfv, df_col, logf, dsq, dsk, dsv, dgf, dgs, name,
             ride=None):
    s_len = x.shape[0]
    ts = min(256, s_len)
    nt = s_len // ts
    gmat = _head_group_matrix()

    def body(x_ref, dxi_ref, gain_ref, w_ref, fqr_ref, fkr_ref, dfqn_ref, dfkn_ref, qg_ref, kg_ref, gm_ref,
             dfv_ref, df_ref, logf_ref, dsq_ref, dsk_ref, dsv_ref, dgf_ref, dgs_ref,
             dp_ref, dx_ref, dgain_ref, dqg_ref, dkg_ref, dbias_ref, carry):
        i = pl.program_id(0)

        @pl.when(i == 0)
        def _():
            carry[...] = jnp.zeros_like(carry)
            dgain_ref[...] = jnp.zeros_like(dgain_ref)
            dqg_ref[...] = jnp.zeros_like(dqg_ref)
            dkg_ref[...] = jnp.zeros_like(dkg_ref)
            dbias_ref[...] = jnp.zeros_like(dbias_ref)

        gm = gm_ref[...]

        def headnorm_bwd(raw, dout, g, dg_ref):
            ms = _dot_split(raw * raw, gm, HEAD_SUM_PARTS) * (1.0 / HEAD_DIM)
            r = lax.rsqrt(ms + EPS)
            nrm = raw * r
            dg_ref[...] += jnp.sum(dout * nrm, axis=0, keepdims=True)
            dn = dout * g
            mean_h = _dot_split(dn * nrm, gm, HEAD_SUM_PARTS) * (1.0 / HEAD_DIM)
            return r * (dn - nrm * mean_h)

        dp_ref[:, C_FQ:C_FQ + ATT_W] = headnorm_bwd(fqr_ref[...], dfqn_ref[...], qg_ref[...], dqg_ref).astype(BF16)
        dp_ref[:, C_FK:C_FK + ATT_W] = headnorm_bwd(fkr_ref[...], dfkn_ref[...], kg_ref[...], dkg_ref).astype(BF16)
        dp_ref[:, C_FV:C_FV + ATT_W] = dfv_ref[...].astype(BF16)
        dp_ref[:, C_SQ:C_SQ + ATT_W] = dsq_ref[...].astype(BF16)
        dp_ref[:, C_SK:C_SK + ATT_W] = dsk_ref[...].astype(BF16)
        dp_ref[:, C_SV:C_SV + ATT_W] = dsv_ref[...].astype(BF16)
        dp_ref[:, C_GF:C_GF + D_MODEL] = dgf_ref[...]
        dp_ref[:, C_GS:C_GS + D_MODEL] = dgs_ref[...]

        r_ = lax.broadcasted_iota(jnp.int32, (ts, ts), 0)
        c_ = lax.broadcasted_iota(jnp.int32, (ts, ts), 1)
        rev = (c_ >= r_).astype(BF16)
        dlogf = _dot_split_left(rev, df_ref[...], 3) + carry[...]
        carry[...] = dlogf[0:1, :]
        lane = lax.broadcasted_iota(jnp.int32, (ts, LANES), 1)
        dfl = jnp.where(lane < N_HEADS, dlogf * (1.0 - jnp.exp(logf_ref[...])), 0.0)
        dbias_ref[...] += jnp.sum(dfl, axis=0, keepdims=True)
        dp_ref[:, C_FL:C_FL + LANES] = dfl.astype(BF16)
        dp_ref[:, C_FL + LANES:C_SQ] = jnp.zeros((ts, C_SQ - C_FL - LANES), BF16)

        dh = _dot(dp_ref[...], w_ref[...])
        xf = x_ref[...]
        r = _rms_rinv(xf)
        xhat = xf * r
        dgain_ref[...] += jnp.sum(dh * xhat, axis=0, keepdims=True)
        dn = dh * gain_ref[...]
        dx_ref[...] = dxi_ref[...] + r * (dn - xhat * jnp.mean(dn * xhat, axis=-1, keepdims=True))

    tok = lambda w: pl.BlockSpec((ts, w), lambda i: (nt - 1 - i, 0))
    full = lambda a: pl.BlockSpec(a.shape, lambda i: (0, 0))
    row = lambda w: pl.BlockSpec((1, w), lambda i: (0, 0))
    return _ride_call(
        body, name, (nt,),
        [tok(D_MODEL), tok(D_MODEL), full(gain), full(w_in), tok(ATT_W), tok(ATT_W), tok(ATT_W), tok(ATT_W),
         full(qg), full(kg), full(gmat), tok(ATT_W), tok(LANES), tok(LANES), tok(ATT_W), tok(ATT_W), tok(ATT_W),
         tok(D_MODEL), tok(D_MODEL)],
        [tok(IN_PAD), tok(D_MODEL), row(D_MODEL), row(ATT_W), row(ATT_W), row(LANES)],
        [jax.ShapeDtypeStruct((s_len, IN_PAD), BF16), jax.ShapeDtypeStruct((s_len, D_MODEL), F32),
         jax.ShapeDtypeStruct((1, D_MODEL), F32), jax.ShapeDtypeStruct((1, ATT_W), F32),
         jax.ShapeDtypeStruct((1, ATT_W), F32), jax.ShapeDtypeStruct((1, LANES), F32)],
        [pltpu.VMEM((1, LANES), F32)], ("arbitrary",),
        (x, dx_in, gain, w_in, fqr, fkr, dfqn, dfkn, qg, kg, gmat, dfv, df_col, logf, dsq, dsk, dsv, dgf, dgs), ride)


def _ple_loss(x, p, tgt, gain, wpg, wpp, name):
    s_len = x.shape[0]
    ts = min(512, s_len)

    def body(x_ref, p_ref, t_ref, gain_ref, wpg_ref, wpp_ref, dx_ref, n_ref, ds_ref, dpp_ref, dgain_ref, loss_ref):
        i = pl.program_id(0)

        @pl.when(i == 0)
        def _():
            dgain_ref[...] = jnp.zeros_like(dgain_ref)
            loss_ref[...] = jnp.zeros_like(loss_ref)

        xf = x_ref[...]
        r = _rms_rinv(xf)
        n = xf * r
        hn = (n * gain_ref[...]).astype(BF16)
        n_ref[...] = hn
        sg = _sigmoid(_dot(hn, wpg_ref[...]))
        pp = _dot_nt(p_ref[...].astype(BF16), wpp_ref[...])
        err = (xf + sg * pp) - t_ref[...]
        sq = jnp.sum(jnp.sum(err * err, axis=1, keepdims=True), axis=0, keepdims=True)
        loss_ref[...] += (0.5 / D_MODEL) * sq
        dout = err * (1.0 / D_MODEL)
        dpp_ref[...] = (dout * sg).astype(BF16)
        ds = (dout * pp * (sg * (1.0 - sg))).astype(BF16)
        ds_ref[...] = ds
        dhn = _dot_nt(ds, wpg_ref[...])
        dgain_ref[...] += jnp.sum(dhn * n, axis=0, keepdims=True)
        dn = dhn * gain_ref[...]
        dx_ref[...] = dout + r * (dn - n * jnp.mean(dn * n, axis=-1, keepdims=True))

    tok = lambda w: pl.BlockSpec((ts, w), lambda i: (i, 0))
    full = lambda a: pl.BlockSpec(a.shape, lambda i: (0, 0))
    return pl.pallas_call(
        body, name=name, grid=(s_len // ts,),
        in_specs=[tok(D_MODEL), tok(PLE_DIM), tok(D_MODEL), full(gain), full(wpg), full(wpp)],
        out_specs=[tok(D_MODEL), tok(D_MODEL), tok(D_MODEL), tok(D_MODEL),
                   pl.BlockSpec((1, D_MODEL), lambda i: (0, 0)), pl.BlockSpec((8, LANES), lambda i: (0, 0))],
        out_shape=[jax.ShapeDtypeStruct((s_len, D_MODEL), F32), jax.ShapeDtypeStruct((s_len, D_MODEL), BF16),
                   jax.ShapeDtypeStruct((s_len, D_MODEL), BF16), jax.ShapeDtypeStruct((s_len, D_MODEL), BF16),
                   jax.ShapeDtypeStruct((1, D_MODEL), F32), jax.ShapeDtypeStruct((8, LANES), F32)],
        compiler_params=_cparams("arbitrary"),
    )(x, p, tgt, gain, wpg, wpp)


def _exchange(x, name, broadcast):
    def body(x_ref, out_ref, send_sems, recv_sems, local_sem):
        _exchange_start(x_ref, out_ref, send_sems, recv_sems, local_sem, broadcast)
        _exchange_wait(x_ref, out_ref, send_sems, recv_sems, local_sem, broadcast)

    return pl.pallas_call(
        body, name=name,
        in_specs=[EXCHANGE_SPEC],
        out_specs=EXCHANGE_SPEC,
        out_shape=_exchange_shape(x, broadcast),
        scratch_shapes=list(EXCHANGE_SEMS),
        compiler_params=pltpu.CompilerParams(has_side_effects=True),
    )(x)


def _gather_two_level(x, name):
    def body(x_ref, out_ref, send_sems, recv_sems, local_sem):
        mx, my, mc = lax.axis_index("x"), lax.axis_index("y"), lax.axis_index("c")
        me, sibling = (mx, my, mc), (mx, my, 1 - mc)
        chips = [(1 - mx, my), (mx, 1 - my), (1 - mx, 1 - my)]

        def slot(px, py, pc):
            return out_ref.at[4 * px + 2 * py + pc]

        def copy(k, block, to, src=None):
            return pltpu.make_async_remote_copy(
                src_ref=slot(*block) if src is None else src, dst_ref=slot(*block),
                send_sem=send_sems.at[k], recv_sem=recv_sems.at[k], device_id=to, device_id_type=MESH)

        mine = pltpu.make_async_copy(x_ref, slot(*me), local_sem)
        mine.start()
        first = [copy(0, me, sibling, src=x_ref)]
        first += [copy(1 + j, me, (*chip, mc), src=x_ref) for j, chip in enumerate(chips)]
        for cp in first:
            cp.start()
        passed = [copy(4 + j, (*chip, mc), sibling) for j, chip in enumerate(chips)]
        for j, chip in enumerate(chips):
            copy(1 + j, (*chip, mc), me).wait_recv()
            passed[j].start()
        copy(0, sibling, me).wait_recv()
        for j, chip in enumerate(chips):
            copy(4 + j, (*chip, 1 - mc), me).wait_recv()
        for cp in first + passed:
            cp.wait_send()
        mine.wait()

    return pl.pallas_call(
        body, name=name,
        in_specs=[EXCHANGE_SPEC],
        out_specs=EXCHANGE_SPEC,
        out_shape=_exchange_shape(x, True),
        scratch_shapes=list(EXCHANGE_SEMS),
        compiler_params=pltpu.CompilerParams(has_side_effects=True),
    )(x)


EXCHANGE_SPEC = pl.BlockSpec(memory_space=pl.ANY)
EXCHANGE_SEMS = (pltpu.SemaphoreType.DMA((N_DEV - 1,)), pltpu.SemaphoreType.DMA((N_DEV - 1,)), pltpu.SemaphoreType.DMA)


def _exchange_shape(x, broadcast):
    return jax.ShapeDtypeStruct((N_DEV,) + tuple(x.shape if broadcast else x.shape[1:]), x.dtype)


def _exchange_copies(x_ref, out_ref, send_sems, recv_sems, local_sem, broadcast, with_recv=True):
    mx, my, mc = lax.axis_index("x"), lax.axis_index("y"), lax.axis_index("c")
    me = 4 * mx + 2 * my + mc

    def src(idx):
        return x_ref if broadcast else x_ref.at[idx]

    local = pltpu.make_async_copy(src(me), out_ref.at[me], local_sem)
    pairs = []
    for k in range(1, N_DEV):
        px = (1 - mx) if k & 4 else mx
        py = (1 - my) if k & 2 else my
        pc = (1 - mc) if k & 1 else mc
        peer = 4 * px + 2 * py + pc
        sems = dict(send_sem=send_sems.at[k - 1], recv_sem=recv_sems.at[k - 1], device_id=(px, py, pc), device_id_type=MESH)
        recv = pltpu.make_async_remote_copy(src_ref=src(peer), dst_ref=out_ref.at[peer], **sems) if with_recv else None
        pairs.append((pltpu.make_async_remote_copy(src_ref=src(peer), dst_ref=out_ref.at[me], **sems), recv))
    return local, pairs


def _exchange_start(*refs_and_mode):
    local, pairs = _exchange_copies(*refs_and_mode, with_recv=False)
    local.start()
    for send, _ in pairs:
        send.start()


def _exchange_wait(*refs_and_mode):
    local, pairs = _exchange_copies(*refs_and_mode)
    for _, recv in pairs:
        recv.wait_recv()
    for send, _ in pairs:
        send.wait_send()
    local.wait()


def _riding(body, grid, n_in, n_out, ride):
    if ride is None:
        return body
    broadcast = ride[1]

    def wrapped(*refs):
        ins, x_ref = refs[:n_in], refs[n_in]
        outs, out_ref = refs[n_in + 1:n_in + 1 + n_out], refs[n_in + 1 + n_out]
        scratch, sems = refs[n_in + 2 + n_out:-3], refs[-3:]
        step = pl.program_id(0)
        for d in range(1, len(grid)):
            step = step * grid[d] + pl.program_id(d)
        total = 1
        for g in grid:
            total *= g

        @pl.when(step == 0)
        def _():
            _exchange_start(x_ref, out_ref, *sems, broadcast)

        body(*ins, *outs, *scratch)

        @pl.when(step == total - 1)
        def _():
            _exchange_wait(x_ref, out_ref, *sems, broadcast)

    return wrapped


def _ride_call(body, name, grid, in_specs, out_specs, out_shape, scratch_shapes, sem, operands, ride):
    if ride is None:
        return pl.pallas_call(body, name=name, grid=grid, in_specs=in_specs, out_specs=out_specs, out_shape=out_shape,
                              scratch_shapes=scratch_shapes, compiler_params=_cparams(*sem))(*operands)
    return pl.pallas_call(
        _riding(body, grid, len(in_specs), len(out_specs), ride), name=name, grid=grid,
        in_specs=list(in_specs) + [EXCHANGE_SPEC], out_specs=list(out_specs) + [EXCHANGE_SPEC],
        out_shape=list(out_shape) + [_exchange_shape(*ride)],
        scratch_shapes=list(scratch_shapes) + list(EXCHANGE_SEMS),
        compiler_params=_cparams(*(["arbitrary"] * len(grid))),
    )(*operands, ride[0])


def _adamw_math(w, g, m, v):
    m2 = ADAM_B1 * m + (1.0 - ADAM_B1) * g
    v2 = ADAM_B2 * v + (1.0 - ADAM_B2) * (g * g)
    m_hat = m2 / (1.0 - ADAM_B1 ** ADAM_STEP)
    v_hat = v2 / (1.0 - ADAM_B2 ** ADAM_STEP)
    delta = -ADAM_LR * (m_hat / (jnp.sqrt(v_hat) + ADAM_EPS) + ADAM_WD * w)
    return delta, m2, v2


def _sum_parts(parts, name, tr):
    _, rows, cols = parts.shape

    def body(p_ref, g_ref):
        g = p_ref[0].astype(F32)
        for s in range(1, N_DEV):
            g = g + p_ref[s].astype(F32)
        g_ref[...] = g

    return pl.pallas_call(
        body, name=name, grid=(rows // tr,),
        in_specs=[pl.BlockSpec((N_DEV, tr, cols), lambda i: (0, i, 0))],
        out_specs=pl.BlockSpec((tr, cols), lambda i: (i, 0)),
        out_shape=jax.ShapeDtypeStruct((rows, cols), F32),
        compiler_params=_cparams("parallel"),
    )(parts)


ADAM_SPLIT_ELEMS = 400_000


def _adamw_shard(g, w, m, v, name):
    rows, cols = w.shape
    tr = rows // 2 if rows * cols > ADAM_SPLIT_ELEMS else rows

    def body(g_ref, w_ref, m_ref, v_ref, d_ref, m2_ref, v2_ref):
        d_ref[...], m2_ref[...], v2_ref[...] = _adamw_math(w_ref[...], g_ref[...], m_ref[...], v_ref[...])

    blk = pl.BlockSpec((tr, cols), lambda i: (i, 0))
    return pl.pallas_call(
        body, name=name, grid=(rows // tr,),
        in_specs=[blk] * 4, out_specs=[blk] * 3,
        out_shape=[jax.ShapeDtypeStruct((rows, cols), F32)] * 3,
        compiler_params=_cparams("parallel"),
    )(g, w, m, v)


def _adamw(parts, w, m, v, name, tr):
    rows, cols = w.shape

    def body(p_ref, w_ref, m_ref, v_ref, g_ref, d_ref, m2_ref, v2_ref):
        g = p_ref[0].astype(F32)
        for s in range(1, N_DEV):
            g = g + p_ref[s].astype(F32)
        g_ref[...] = g
        d_ref[...], m2_ref[...], v2_ref[...] = _adamw_math(w_ref[...], g, m_ref[...], v_ref[...])

    blk = pl.BlockSpec((tr, cols), lambda i: (i, 0))
    return pl.pallas_call(
        body, name=name, grid=(rows // tr,),
        in_specs=[pl.BlockSpec((N_DEV, tr, cols), lambda i: (0, i, 0)), blk, blk, blk],
        out_specs=[blk] * 4,
        out_shape=[jax.ShapeDtypeStruct((rows, cols), F32)] * 4,
        compiler_params=_cparams("parallel"),
    )(parts, w, m, v)


TRANSPOSED = frozenset(("ffn1_w_gate", "ffn1_w_up", "w_in", "w_branch_fox", "w_branch_sb", "ffn2_w_gate", "ffn2_w_up",
                        "w_ple_proj"))
F_PAD_ROWS = C_SQ - FL_REAL_END


def _pack(pieces, group, dtype):
    out = []
    for name in GATHER_GROUPS[group]:
        r = pieces[name].T if name in TRANSPOSED else pieces[name]
        r = r.reshape(-1, D_MODEL).astype(dtype)
        if r.shape[0] != PACK_ROWS[name]:
            r = jnp.pad(r, ((0, PACK_ROWS[name] - r.shape[0]), (0, 0)))
        out.append(r)
    return jnp.concatenate(out, axis=0)


def _real_rows(name):
    return W_IN_ROWS if name == "w_in" else PACK_ROWS[name]


def _gathered(got, name, shape):
    off = GATHER_OFF[name]
    return got[:, off:off + _real_rows(name), :].reshape(shape)


def _w_in_device_rows(d):
    lo, hi = d * W_IN_ROWS, (d + 1) * W_IN_ROWS
    if hi <= FL_REAL_END:
        return [(lo, hi)]
    if lo >= FL_REAL_END:
        return [(lo + F_PAD_ROWS, hi + F_PAD_ROWS)]
    return [(lo, FL_REAL_END), (C_SQ, hi + F_PAD_ROWS)]


def _w_in_t_padded(got):
    t = _gathered(got, "w_in", (IN_REAL, D_MODEL))
    return jnp.concatenate([t[:FL_REAL_END], jnp.zeros((F_PAD_ROWS, D_MODEL), t.dtype), t[FL_REAL_END:]], axis=0)


def _pack_chunks(grads, group):
    out = []
    for name in SCATTER_GROUPS[group]:
        g = grads[name].astype(BF16)
        if name == "w_in":
            tail = jnp.zeros((PACK_ROWS[name] - W_IN_ROWS, D_MODEL), BF16)
            c = jnp.stack([jnp.concatenate([g[lo:hi] for lo, hi in _w_in_device_rows(d)] + [tail], axis=0)
                           for d in range(N_DEV)])
        else:
            c = g.reshape(N_DEV, PACK_ROWS[name], D_MODEL)
        out.append(c)
    return out[0] if len(out) == 1 else jnp.concatenate(out, axis=1)


def _shard_grad(packed, name, shape):
    off = SCATTER_OFF[name]
    rows = packed[off:off + _real_rows(name), :]
    return rows.reshape(shape[1], shape[0]).T if name in TRANSPOSED else rows.reshape(shape)


WEIGHT_NAMES = ['ffn1_norm', 'ffn1_w_gate', 'ffn1_w_up', 'ffn1_w_down', 'mix_norm', 'w_in', 'forget_bias', 'q_norm',
                'k_norm', 'w_branch_fox', 'w_branch_sb', 'w_out', 'ffn2_norm', 'ffn2_w_gate', 'ffn2_w_up',
                'ffn2_w_down', 'ple_norm', 'w_ple_gate', 'w_ple_proj']
SMALL_NAMES = ('ffn1_norm', 'mix_norm', 'ffn2_norm', 'ple_norm', 'q_norm', 'k_norm', 'forget_bias')
Q_OFF, K_OFF, B_OFF, LOSS_OFF = 0, HEAD_DIM, 2 * HEAD_DIM, 2 * HEAD_DIM + N_HEADS


def _pack_small(vals, loss=None):
    tail = [vals['q_norm'].reshape(1, -1), vals['k_norm'].reshape(1, -1), vals['forget_bias'].reshape(1, -1)]
    used = LOSS_OFF
    if loss is not None:
        tail.append(loss.reshape(1, 1))
        used += 1
    tail.append(jnp.zeros((1, D_MODEL - used), F32))
    rows = [vals[n].reshape(1, D_MODEL) for n in SMALL_NAMES[:4]] + [jnp.concatenate(tail, axis=1)]
    rows.append(jnp.zeros((SMALL_ROWS - len(rows), D_MODEL), F32))
    return jnp.concatenate(rows, axis=0)


def _unpack_small(packed, name, shape):
    if name in SMALL_NAMES[:4]:
        return packed[SMALL_NAMES.index(name)].reshape(shape)
    off, n = {'q_norm': (Q_OFF, HEAD_DIM), 'k_norm': (K_OFF, HEAD_DIM), 'forget_bias': (B_OFF, N_HEADS)}[name]
    return packed[4, off:off + n].reshape(shape)


def _step(x, p, tgt, w):
    row = lambda a: a.reshape(1, -1).astype(F32)
    g_ffn1, g_mix, g_ffn2, g_ple = (row(w[n]) for n in SMALL_NAMES[:4])
    qg = jnp.tile(row(w['q_norm']), (1, N_HEADS))
    kg = jnp.tile(row(w['k_norm']), (1, N_HEADS))
    bias = jnp.pad(row(w['forget_bias']), ((0, 0), (0, LANES - N_HEADS)))
    half = D_FF // 2
    grads = {}

    blk = lambda n: GATHER_OFF[n] // FFN_SHARD
    ffn1 = tuple(blk(n) for n in ("ffn1_w_gate", "ffn1_w_up", "ffn1_w_down"))
    ffn2 = tuple(blk(n) for n in ("ffn2_w_gate", "ffn2_w_up", "ffn2_w_down"))
    got0 = _gather_two_level(_pack(w, 0, BF16), "gather_ffn1")
    x1, g1, u1, h1, got1 = _ffn_fwd(x, g_ffn1, got0, ffn1, "ffn1_fwd", ride=(_pack(w, 1, BF16), True))
    w_in = _w_in_t_padded(got1)
    wbf = _gathered(got1, "w_branch_fox", (D_MODEL, ATT_W))
    wbs = _gathered(got1, "w_branch_sb", (D_MODEL, ATT_W))
    wo = _gathered(got1, "w_out", (D_MODEL, D_MODEL))
    (hmix, fqr, fkr, fqn, fkn, fv, logf, f_col, f_row, sq, sk, sv, gf, gs, kmax) = _mix_fwd(
        x1, g_mix, w_in, bias, qg, kg, "mix_fwd")
    y_fox, lse, got2 = _fox_fwd(fqn, fkn, fv, f_col, f_row, kmax, "fox_fwd", ride=(_pack(w, 2, BF16), True))
    wpg = _gathered(got2, "w_ple_gate", (D_MODEL, D_MODEL))
    wpp = _gathered(got2, "w_ple_proj", (D_MODEL, PLE_DIM))
    y_sb, y_sb32 = _sb_fwd(sq, sk, sv, "sb_fwd")
    x2 = _merge_fwd(x1, y_fox, y_sb, gf, gs, wbf, wbs, wo, "merge_fwd")
    x3, g2, u2, h2, = _ffn_fwd(x2, g_ffn2, got2, ffn2, "ffn2_fwd")
    dx3, n_ple, ds_ple, dpp, dg_ple, loss = _ple_loss(x3, p, tgt, g_ple, wpg, wpp, "ple_loss")

    grads['w_ple_gate'] = _wgrad(n_ple, ds_ple, "dw_ple_gate", D_MODEL, D_MODEL)
    grads['w_ple_proj'] = _wgrad(dpp, p, "dw_ple_proj", D_MODEL, PLE_DIM)
    dg2, du2, act2, dx2, dg_ffn2 = _ffn_bwd_fused(x2, dx3, g_ffn2, g2, u2, got2, ffn2, "ffn2_bwd")
    grads['ffn2_w_gate'] = _wgrad(dg2, h2, "dw_ffn2_gate", half, D_MODEL)
    grads['ffn2_w_up'] = _wgrad(du2, h2, "dw_ffn2_up", half, D_MODEL)
    grads['ffn2_w_down'] = _wgrad(act2, dx3, "dw_ffn2_down", half, D_MODEL)
    dyf, dys, dgf, dgs, dbf, dbs, merged = _merge_bwd(dx2, y_fox, y_sb, gf, gs, wbf, wbs, wo, "merge_bwd")
    grads['w_branch_fox'] = _wgrad(dbf, y_fox, "dw_branch_fox", D_MODEL, ATT_W)
    grads['w_branch_sb'] = _wgrad(dbs, y_sb, "dw_branch_sb", D_MODEL, ATT_W)
    grads['w_out'] = _wgrad(merged, dx2, "dw_out", D_MODEL, D_MODEL)
    dfqn, dfkn, dfv, dft, part_rest = _fox_bwd(fqn, fkn, fv, dyf, y_fox, lse, f_col, f_row, kmax, "fox_bwd",
                                               ride=(_pack_chunks(grads, 4), False))
    dsq, dsk, dsv = _sb_bwd(sq, sk, sv, dys, y_sb32, "sb_bwd")
    s_len = x.shape[0]
    df_col = jnp.pad(dft[:, :2, :].reshape(N_HEADS, s_len).T, ((0, 0), (0, LANES - N_HEADS)))
    dproj, dx1, dg_mix, dqg, dkg, dbias = _mix_bwd(
        x1, dx2, g_mix, w_in, fqr, fkr, dfqn, dfkn, qg, kg, dfv, df_col, logf, dsq, dsk, dsv, dgf, dgs, "mix_bwd")
    grads['w_in'] = _wgrad(dproj, hmix, "dw_in", IN_PAD // 3, D_MODEL)
    dg1, du1, act1, dx0, dg_ffn1, part_in = _ffn_bwd_fused(x, dx1, g_ffn1, g1, u1, got0, ffn1, "ffn1_bwd",
                                                           ride=(_pack_chunks(grads, 3), False))
    grads['ffn1_w_gate'] = _wgrad(dg1, h1, "dw_ffn1_gate", half, D_MODEL)
    grads['ffn1_w_up'], part_gate = _wgrad(du1, h1, "dw_ffn1_up", half, D_MODEL, ride=(_pack_chunks(grads, 0), False))
    grads['ffn1_w_down'], part_up = _wgrad(act1, dx1, "dw_ffn1_down", half, D_MODEL,
                                           ride=(_pack_chunks(grads, 1), False))
    part_down = _exchange(_pack_chunks(grads, 2), "scatter_ffn1_down", False)

    fold = lambda a: a.reshape(N_HEADS, HEAD_DIM).sum(axis=0).reshape(1, HEAD_DIM)
    small_g = {'ffn1_norm': dg_ffn1, 'mix_norm': dg_mix, 'ffn2_norm': dg_ffn2, 'ple_norm': dg_ple,
               'q_norm': fold(dqg), 'k_norm': fold(dkg), 'forget_bias': dbias[:, :N_HEADS]}
    return loss[0, 0], dx0, (part_gate, part_up, part_down, part_in, part_rest), small_g


def kernel(x, p, ffn1_norm, ffn1_w_gate, ffn1_w_up, ffn1_w_down, mix_norm, w_in, forget_bias, q_norm, k_norm, w_branch_fox, w_branch_sb, w_out, ffn2_norm, ffn2_w_gate, ffn2_w_up, ffn2_w_down, ple_norm, w_ple_gate, w_ple_proj, loss_target, m_ffn1_norm, m_ffn1_w_gate, m_ffn1_w_up, m_ffn1_w_down, m_mix_norm, m_w_in, m_forget_bias, m_q_norm, m_k_norm, m_w_branch_fox, m_w_branch_sb, m_w_out, m_ffn2_norm, m_ffn2_w_gate, m_ffn2_w_up, m_ffn2_w_down, m_ple_norm, m_w_ple_gate, m_w_ple_proj, v_ffn1_norm, v_ffn1_w_gate, v_ffn1_w_up, v_ffn1_w_down, v_mix_norm, v_w_in, v_forget_bias, v_q_norm, v_k_norm, v_w_branch_fox, v_w_branch_sb, v_w_out, v_ffn2_norm, v_ffn2_w_gate, v_ffn2_w_up, v_ffn2_w_down, v_ple_norm, v_w_ple_gate, v_w_ple_proj):
    args = dict(locals())
    w = {n: args[n][0] for n in WEIGHT_NAMES}
    m = {n: args["m_" + n][0] for n in WEIGHT_NAMES}
    v = {n: args["v_" + n][0] for n in WEIGHT_NAMES}
    loss, dx, parts, small_g = _step(x[0], p[0, 0], loss_target[0], w)

    big = {}
    for grp, part in enumerate(parts):
        summed = _sum_parts(part, f"sum_grads_{grp}", SUM_TILE_ROWS[grp])
        for n in SCATTER_GROUPS[grp]:
            g = _shard_grad(summed, n, w[n].shape)
            big[n] = (g,) + tuple(_adamw_shard(g, w[n], m[n], v[n], "adamw_" + n))
    small_parts = _exchange(_pack_small(small_g, loss), "gather_small", True)
    sw, sm, sv = (_pack_small(t) for t in (w, m, v))
    small = _adamw(small_parts, sw, sm, sv, "adamw_small", SMALL_ROWS)

    outs = [small[0][4, LOSS_OFF], dx.reshape(x.shape)]
    for kind in range(4):
        for n in WEIGHT_NAMES:
            shape = args[n].shape
            outs.append(_unpack_small(small[kind], n, shape) if n in SMALL_NAMES else big[n][kind].reshape(shape))
    return tuple(outs)
```

```python
import jax
import jax.numpy as jnp
from jax import lax
from jax.experimental import pallas as pl
from jax.experimental.pallas import tpu as pltpu

F32 = jnp.float32
BF16 = jnp.bfloat16

D_MODEL = 1024
D_FF = 2816
N_HEADS = 8
HEAD_DIM = 64
ATT_W = N_HEADS * HEAD_DIM
PLE_DIM = 256
EPS = 1e-6
N_DEV = 8
MESH = pl.DeviceIdType.MESH

LANES = 128
V7X_SCOPED_VMEM_BYTES = 56 * 1024 * 1024

C_FQ, C_FK, C_FV, C_FL = 0, 512, 1024, 1536
C_SQ, C_SK, C_SV, C_GF, C_GS = 1792, 2304, 2816, 3328, 4352
IN_PAD = 5376
IN_REAL = 5128
FL_REAL_END = 1544

ADAM_LR = 0.001
ADAM_B1 = 0.9
ADAM_B2 = 0.999
ADAM_EPS = 1e-08
ADAM_WD = 0.01
ADAM_STEP = 10

PACK_ROWS = {"ffn1_w_gate": 352, "ffn1_w_up": 352, "ffn1_w_down": 352, "w_in": 656, "w_branch_fox": 64,
             "w_branch_sb": 64, "w_out": 128, "ffn2_w_gate": 352, "ffn2_w_up": 352, "ffn2_w_down": 352,
             "w_ple_gate": 128, "w_ple_proj": 32}
GATHER_GROUPS = (
    ("ffn1_w_gate", "ffn1_w_up", "ffn1_w_down"),
    ("w_in", "w_branch_fox", "w_branch_sb", "w_out"),
    ("ffn2_w_gate", "ffn2_w_up", "ffn2_w_down", "w_ple_gate", "w_ple_proj"),
)
W_IN_HALVES = ((0, 336), (336, 656))
PACK_ROWS.update({f"w_in#{i}": hi - lo for i, (lo, hi) in enumerate(W_IN_HALVES)})
SCATTER_GROUPS = (
    ("ffn1_w_gate",), ("ffn1_w_up",), ("ffn1_w_down",),
    ("w_in#0",), ("w_in#1",),
    ("ffn2_w_gate", "ffn2_w_up", "ffn2_w_down", "w_ple_gate", "w_ple_proj", "w_branch_fox", "w_branch_sb", "w_out"),
)
SUM_TILE_ROWS = (352, 352, 352, 336, 320, 368)


def _offsets(groups):
    off = {}
    for grp in groups:
        o = 0
        for n in grp:
            off[n] = o
            o += PACK_ROWS[n]
    return off


GATHER_OFF = _offsets(GATHER_GROUPS)
SCATTER_OFF = _offsets(SCATTER_GROUPS)
W_IN_ROWS = 641

SMALL_ROWS = 8


def _cparams(*sem):
    return pltpu.CompilerParams(dimension_semantics=sem, vmem_limit_bytes=V7X_SCOPED_VMEM_BYTES)


def _dot(a, b):
    return jnp.dot(a, b, preferred_element_type=F32)


def _dot_nt(a, b):
    return lax.dot_general(a, b, (((1,), (1,)), ((), ())), preferred_element_type=F32)


def _dot_tn(a, b):
    return lax.dot_general(a, b, (((0,), (0,)), ((), ())), preferred_element_type=F32)


def _split(x, parts):
    out = []
    r = x
    for _ in range(parts):
        p = r.astype(BF16)
        out.append(p)
        r = r - p.astype(F32)
    return out


def _dot_split(x, m, parts):
    acc = None
    for p in _split(x, parts):
        t = _dot(p, m)
        acc = t if acc is None else acc + t
    return acc


def _dot_split_left(m, x, parts):
    acc = None
    for p in _split(x, parts):
        t = _dot(m, p)
        acc = t if acc is None else acc + t
    return acc


def _rms_rinv(xf):
    return lax.rsqrt(jnp.mean(xf * xf, axis=-1, keepdims=True) + EPS)


def _sigmoid(x):
    return 1.0 / (1.0 + jnp.exp(-x))


def _softplus_neg_abs(z):
    return jnp.log(1.0 + jnp.exp(-jnp.abs(z)))


FFN_SHARD = D_FF // N_DEV
FFN_CHUNK = 4


def _ffn_w_spec(blk, index_map):
    return pl.BlockSpec((FFN_CHUNK, FFN_SHARD, D_MODEL), lambda *g: (index_map(*g), blk, 0))


def _ffn_w(ref):
    return ref[...].reshape(FFN_CHUNK * FFN_SHARD, D_MODEL)


def _ffn_fwd(x, gain, wbuf, blks, name, ride=None):
    s_len = x.shape[0]
    ts = min(512, s_len)
    fc = FFN_CHUNK * FFN_SHARD
    nt, nc = s_len // ts, D_FF // fc

    def body(x_ref, gain_ref, wg_ref, wu_ref, wd_ref, y_ref, g_ref, u_ref, h_ref, acc_scr):
        j = pl.program_id(1)

        @pl.when(j == 0)
        def _():
            xf = x_ref[...]
            h_ref[...] = ((xf * _rms_rinv(xf)) * gain_ref[...]).astype(BF16)
            acc_scr[...] = jnp.zeros_like(acc_scr)

        h = h_ref[...]
        g = _dot_nt(h, _ffn_w(wg_ref))
        u = _dot_nt(h, _ffn_w(wu_ref))
        g_ref[...] = g.astype(BF16)
        u_ref[...] = u.astype(BF16)
        a = (g * _sigmoid(g) * u).astype(BF16)
        acc_scr[...] += _dot(a, _ffn_w(wd_ref))

        @pl.when(j == nc - 1)
        def _():
            y_ref[...] = x_ref[...] + 0.5 * acc_scr[...]

    tok = pl.BlockSpec((ts, D_MODEL), lambda i, j: (i, 0))
    hid = pl.BlockSpec((ts, fc), lambda i, j: (i, j))
    return _ride_call(
        body, name, (nt, nc),
        [tok, pl.BlockSpec((1, D_MODEL), lambda i, j: (0, 0))] + [_ffn_w_spec(b, lambda i, j: j) for b in blks],
        [tok, hid, hid, tok],
        [jax.ShapeDtypeStruct((s_len, D_MODEL), F32), jax.ShapeDtypeStruct((s_len, D_FF), BF16),
         jax.ShapeDtypeStruct((s_len, D_FF), BF16), jax.ShapeDtypeStruct((s_len, D_MODEL), BF16)],
        [pltpu.VMEM((ts, D_MODEL), F32)], ("parallel", "arbitrary"), (x, gain, wbuf, wbuf, wbuf), ride)


def _ffn_bwd_fused(x, dy, gain, g, u, wbuf, blks, name, ride=None):
    s_len = x.shape[0]
    ts = min(512, s_len)
    fc = FFN_CHUNK * FFN_SHARD
    nt = s_len // ts
    assert D_FF == 2 * fc

    def hidden(dy_ref, g_ref, u_ref, wg_ref, wu_ref, wd_ref, dg_ref, du_ref, act_ref):
        da = 0.5 * _dot_nt(dy_ref[...].astype(BF16), _ffn_w(wd_ref))
        gf = g_ref[...].astype(F32)
        uf = u_ref[...].astype(F32)
        sg = _sigmoid(gf)
        silu = gf * sg
        dg = (da * uf * (sg * (1.0 + gf * (1.0 - sg)))).astype(BF16)
        du = (da * silu).astype(BF16)
        dg_ref[...] = dg
        du_ref[...] = du
        act_ref[...] = (0.5 * silu * uf).astype(BF16)
        return _dot(dg, _ffn_w(wg_ref)) + _dot(du, _ffn_w(wu_ref))

    def first(dy_ref, g_ref, u_ref, wg_ref, wu_ref, wd_ref, dg_ref, du_ref, act_ref, dh_ref):
        dh_ref[...] = hidden(dy_ref, g_ref, u_ref, wg_ref, wu_ref, wd_ref, dg_ref, du_ref, act_ref)

    def second(x_ref, dy_ref, gain_ref, g_ref, u_ref, wg_ref, wu_ref, wd_ref, dh0_ref, dg_half, du_half, act_half,
               dg_ref, du_ref, act_ref, dx_ref, dgain_ref):
        i = pl.program_id(0)
        dh = dh0_ref[...] + hidden(dy_ref, g_ref, u_ref, wg_ref, wu_ref, wd_ref, dg_ref, du_ref, act_ref)
        xf = x_ref[...]
        r = _rms_rinv(xf)
        xhat = xf * r
        dgp = jnp.sum(dh * xhat, axis=0, keepdims=True)

        @pl.when(i == 0)
        def _():
            dgain_ref[...] = dgp

        @pl.when(i > 0)
        def _():
            dgain_ref[...] += dgp

        dn = dh * gain_ref[...]
        dx_ref[...] = dy_ref[...] + r * (dn - xhat * jnp.mean(dn * xhat, axis=-1, keepdims=True))

    tok = pl.BlockSpec((ts, D_MODEL), lambda i: (i, 0))
    row = pl.BlockSpec((1, D_MODEL), lambda i: (0, 0))
    hid = lambda c: pl.BlockSpec((ts, fc), lambda i: (i, c))
    wts = lambda c: [pl.BlockSpec((FFN_CHUNK, FFN_SHARD, D_MODEL), lambda i, b=b: (c, b, 0),
                                  pipeline_mode=pl.Buffered(1)) for b in blks]
    hidden_shapes = [jax.ShapeDtypeStruct((s_len, D_FF), BF16)] * 3
    dg, du, act, dh0, *rode = _ride_call(
        first, name + "_a", (nt,), [tok, hid(0), hid(0)] + wts(0), [hid(0)] * 3 + [tok],
        hidden_shapes + [jax.ShapeDtypeStruct((s_len, D_MODEL), F32)], [], ("parallel",),
        (dy, g, u, wbuf, wbuf, wbuf), ride)
    filled = pl.BlockSpec(memory_space=pl.ANY)
    dg, du, act, dx, dgain = pl.pallas_call(
        second, name=name + "_b", grid=(nt,),
        in_specs=[tok, tok, row, hid(1), hid(1)] + wts(1) + [tok, filled, filled, filled],
        out_specs=[hid(1)] * 3 + [tok, row],
        out_shape=hidden_shapes + [jax.ShapeDtypeStruct((s_len, D_MODEL), F32), jax.ShapeDtypeStruct((1, D_MODEL), F32)],
        input_output_aliases={9: 0, 10: 1, 11: 2},
        compiler_params=_cparams("arbitrary"),
    )(x, dy, gain, g, u, wbuf, wbuf, wbuf, dh0, dg, du, act)
    return (dg, du, act, dx, dgain, *rode)


def _wgrad(a, b, name, tk, tn, ride=None):
    s_len, k_dim = a.shape
    n_dim = b.shape[1]
    ts = min(1024, s_len)
    ns = s_len // ts

    def body(a_ref, b_ref, o_ref, acc):
        s = pl.program_id(2)
        p = _dot_tn(a_ref[...].astype(BF16), b_ref[...].astype(BF16))

        @pl.when(s == 0)
        def _():
            acc[...] = p

        @pl.when(s > 0)
        def _():
            acc[...] += p

        @pl.when(s == ns - 1)
        def _():
            o_ref[...] = acc[...].astype(BF16)

    out = _ride_call(
        body, name, (k_dim // tk, n_dim // tn, ns),
        [pl.BlockSpec((ts, tk), lambda k, n, s: (s, k)), pl.BlockSpec((ts, tn), lambda k, n, s: (s, n))],
        [pl.BlockSpec((tk, tn), lambda k, n, s: (k, n))], [jax.ShapeDtypeStruct((k_dim, n_dim), BF16)],
        [pltpu.VMEM((tk, tn), F32)], ("parallel", "parallel", "arbitrary"), (a, b), ride)
    return out[0] if ride is None else tuple(out)


HEAD_SUM_PARTS = 2


def _head_group_matrix():
    r = lax.broadcasted_iota(jnp.int32, (ATT_W, ATT_W), 0) // HEAD_DIM
    c = lax.broadcasted_iota(jnp.int32, (ATT_W, ATT_W), 1) // HEAD_DIM
    return (r == c).astype(BF16)


def _mix_fwd(x, gain, w_in, bias, qg, kg, name):
    s_len = x.shape[0]
    ts = min(256, s_len)
    nt = s_len // ts
    gmat = _head_group_matrix()

    def body(x_ref, gain_ref, w_ref, bias_ref, qg_ref, kg_ref, gm_ref,
             h_ref, fqr_ref, fkr_ref, fqn_ref, fkn_ref, fv_ref, logf_ref, f_ref, ft_ref,
             sq_ref, sk_ref, sv_ref, gf_ref, gs_ref, kmax_ref, carry):
        i = pl.program_id(0)
        xf = x_ref[...]
        h = ((xf * _rms_rinv(xf)) * gain_ref[...]).astype(BF16)
        h_ref[...] = h
        gm = gm_ref[...]

        def proj(lo, n):
            return _dot_nt(h, w_ref[lo:lo + n, :])

        def headnorm(raw, g):
            ms = _dot_split(raw * raw, gm, HEAD_SUM_PARTS) * (1.0 / HEAD_DIM)
            return ((raw * lax.rsqrt(ms + EPS)) * g).astype(BF16)

        fq = proj(C_FQ, ATT_W)
        fqr_ref[...] = fq
        fqn_ref[...] = headnorm(fq, qg_ref[...])
        fk = proj(C_FK, ATT_W)
        fkr_ref[...] = fk
        fkn = headnorm(fk, kg_ref[...])
        fkn_ref[...] = fkn
        kn2 = jnp.max(_dot_split(jnp.square(fkn.astype(F32)), gm, HEAD_SUM_PARTS), axis=0, keepdims=True)

        @pl.when(i == 0)
        def _():
            kmax_ref[...] = kn2

        @pl.when(i > 0)
        def _():
            kmax_ref[...] = jnp.maximum(kmax_ref[...], kn2)
        fv_ref[...] = proj(C_FV, ATT_W).astype(BF16)
        sq_ref[...] = proj(C_SQ, ATT_W).astype(BF16)
        sk_ref[...] = proj(C_SK, ATT_W).astype(BF16)
        sv_ref[...] = proj(C_SV, ATT_W).astype(BF16)
        gf_ref[...] = proj(C_GF, D_MODEL)
        gs_ref[...] = proj(C_GS, D_MODEL)

        fl = proj(C_FL, LANES) + bias_ref[...]
        lane = lax.broadcasted_iota(jnp.int32, fl.shape, 1)
        logf = jnp.where(lane < N_HEADS, jnp.minimum(fl, 0.0) - _softplus_neg_abs(fl), 0.0)
        logf_ref[...] = logf

        @pl.when(i == 0)
        def _():
            carry[...] = jnp.zeros_like(carry)

        r = lax.broadcasted_iota(jnp.int32, (ts, ts), 0)
        c = lax.broadcasted_iota(jnp.int32, (ts, ts), 1)
        tri = (r >= c).astype(BF16)
        f_tile = _dot_split_left(tri, logf, 3) + carry[...]
        f_ref[...] = f_tile
        ft_ref[...] = f_tile.T[:N_HEADS, :]
        carry[...] = f_tile[ts - 1:ts, :]

    tok = lambda w: pl.BlockSpec((ts, w), lambda i: (i, 0))
    full = lambda a: pl.BlockSpec(a.shape, lambda i: (0, 0))
    f32o = lambda w: jax.ShapeDtypeStruct((s_len, w), F32)
    b16o = lambda w: jax.ShapeDtypeStruct((s_len, w), BF16)
    return _ride_call(
        body, name, (nt,),
        [tok(D_MODEL), full(gain), full(w_in), full(bias), full(qg), full(kg), full(gmat)],
        [
            tok(D_MODEL), tok(ATT_W), tok(ATT_W), tok(ATT_W), tok(ATT_W), tok(ATT_W), tok(LANES), tok(LANES),
            pl.BlockSpec((N_HEADS, ts), lambda i: (0, i)),
            tok(ATT_W), tok(ATT_W), tok(ATT_W), tok(D_MODEL), tok(D_MODEL),
            pl.BlockSpec((1, ATT_W), lambda i: (0, 0)),
        ],
        [
            b16o(D_MODEL), f32o(ATT_W), f32o(ATT_W), b16o(ATT_W), b16o(ATT_W), b16o(ATT_W), f32o(LANES), f32o(LANES),
            jax.ShapeDtypeStruct((N_HEADS, s_len), F32),
            b16o(ATT_W), b16o(ATT_W), b16o(ATT_W), f32o(D_MODEL), f32o(D_MODEL),
            jax.ShapeDtypeStruct((1, ATT_W), F32),
        ],
        [pltpu.VMEM((1, LANES), F32)], ("arbitrary",), (x, gain, w_in, bias, qg, kg, gmat), None)


ATT_T = 256
ATT_ROWS = 2
EXP_ZERO = 88.0


def _att_tiling(s_len):
    t = min(ATT_T, s_len)
    nr = min(ATT_ROWS, s_len // t)
    return t, nr, s_len // (t * nr)


def _pair_specs(s_len, tq):
    qblk = pl.BlockSpec((tq, LANES), lambda hp, i: (i, hp))
    kvfull = pl.BlockSpec((s_len, LANES), lambda hp, i: (0, hp))
    return qblk, kvfull


def _walk_tiles(i, nr, load, sub, flush, init, more=None, trips=None):
    base = i * nr
    carries = list(init)
    for kk in range(nr - 1, -1, -1):
        rs = list(range(kk, nr))
        new, side = sub(rs, load(base + kk), [carries[r] for r in rs], [r == kk for r in rs])
        carries[kk:] = new
        flush(base + kk, side)

    done = jnp.int32(0)
    for last in range(nr - 1, -1, -1):
        rs = list(range(last + 1))

        def visit(n, cs, rs=rs):
            kb = base - 1 - n
            cs, side = sub(rs, load(kb), list(cs), [False] * len(rs))
            flush(kb, side)
            return tuple(cs)

        if trips is not None:
            todo = jnp.maximum(trips(carries, base, last) - done, 0)
            new = lax.fori_loop(0, todo, lambda it, cs, start=done, visit=visit: visit(start + it, cs),
                                tuple(carries[:last + 1]))
            done = done + todo
        else:
            def step(state, visit=visit, last=last):
                n, _, cs = state
                cs = visit(n, cs)
                return n + 1, more(cs[last]), cs

            done, _, new = lax.while_loop(lambda state: jnp.logical_and(state[0] < base, state[1] > 0), step,
                                          (done, more(carries[last]), tuple(carries[:last + 1])))
        carries[:last + 1] = list(new)
    return carries


def _stack(parts):
    return parts[0] if len(parts) == 1 else jnp.concatenate(parts, axis=0)


def _stacked_halves(x, lo):
    z = jnp.zeros_like(x)
    return jnp.concatenate([jnp.where(lo, x, z), jnp.where(lo, z, x)], axis=0)


def _fox_qk_bound(qst_r, km_ref, t):
    km = km_ref[...]
    out = []
    for j in (0, 1):
        qf = qst_r[j * t:(j + 1) * t, :].astype(F32)
        qn = jnp.sqrt(jnp.sum(qf * qf, axis=1, keepdims=True))
        out.append(qn * jnp.sqrt(km[:, j * HEAD_DIM:j * HEAD_DIM + 1]) * 1.001 + 1.0)
    return out


def _fox_trips(hp, flast_ref, qkb, fq, level):
    def trips(carries, base, r):
        gap = [jnp.max(qkb[r][j] + fq[r][j] - level(carries, r, j)) for j in (0, 1)]

        def needed(n):
            kb = jnp.maximum(base - 1 - n, 0)
            return jnp.logical_or(gap[0] - flast_ref[2 * hp, kb] > -EXP_ZERO,
                                  gap[1] - flast_ref[2 * hp + 1, kb] > -EXP_ZERO)

        return lax.while_loop(lambda n: jnp.logical_and(n < base, needed(n)), lambda n: n + 1, jnp.int32(0))
    return trips


def _fox_fwd(q, k, v, f_col, f_row, kmax, name, ride=None):
    s_len = q.shape[0]
    t, nr, nq = _att_tiling(s_len)

    def body(q_ref, k_ref, v_ref, f_ref, ft_ref, km_ref, fl_ref, y_ref, lse_ref):
        hp = pl.program_id(0)
        i = pl.program_id(1)
        lane = lax.broadcasted_iota(jnp.int32, (t, LANES), 1)
        lo = lane < HEAD_DIM
        causal = lax.broadcasted_iota(jnp.int32, (t, t), 0) >= lax.broadcasted_iota(jnp.int32, (t, t), 1)
        rows = [pl.ds(r * t, t) for r in range(nr)]
        qst = [_stacked_halves(q_ref[rw, :] * jnp.asarray(HEAD_DIM ** -0.5, BF16), lo) for rw in rows]
        q_all = _stack(qst)
        fq = [[jnp.sum(jnp.where(lane == 2 * hp + j, f_ref[rw, :], 0.0), axis=1, keepdims=True) for j in (0, 1)]
              for rw in rows]

        def load(kb):
            k0 = pl.multiple_of(kb * t, t)
            frow = [ft_ref[pl.ds(2 * hp + j, 1), pl.ds(k0, t)] for j in (0, 1)]
            return k_ref[pl.ds(k0, t), :], v_ref[pl.ds(k0, t), :], frow

        def sub(rs, tiles, carries, masked):
            kblk, vblk, frow = tiles
            z = _dot_nt(q_all if len(rs) == nr else _stack([qst[r] for r in rs]), kblk)
            ps, stats = [], []
            for n, (r, j) in enumerate((r, j) for r in range(len(rs)) for j in (0, 1)):
                m, l, _ = carries[r]
                s = z[n * t:(n + 1) * t, :] + (fq[rs[r]][j] - frow[j])
                if masked[r]:
                    s = jnp.where(causal, s, -1e30)
                mj = jnp.maximum(m[j], jnp.max(s, axis=1, keepdims=True))
                aj = jnp.exp(m[j] - mj)
                p = jnp.exp(s - mj)
                stats.append((mj, aj, aj * l[j] + jnp.sum(p, axis=1, keepdims=True)))
                ps.append(p.astype(BF16))
            pv = _dot(_stack(ps), vblk)
            out = []
            for r in range(len(rs)):
                (m0, a0, l0), (m1, a1, l1) = stats[2 * r], stats[2 * r + 1]
                acc = carries[r][2]
                acc = (acc[0] * a0 + pv[2 * r * t:(2 * r + 1) * t, :], acc[1] * a1 + pv[(2 * r + 1) * t:(2 * r + 2) * t, :])
                out.append(((m0, m1), (l0, l1), acc))
            return out, None

        neg = jnp.full((t, 1), -1e30, F32)
        zero = jnp.zeros((t, 1), F32)
        zacc = jnp.zeros((t, LANES), F32)
        init = [((neg, neg), (zero, zero), (zacc, zacc))] * nr
        qkb = [_fox_qk_bound(qs, km_ref, t) for qs in qst]
        trips = _fox_trips(hp, fl_ref, qkb, fq, lambda carries, r, j: carries[r][0][j])
        out = _walk_tiles(i, nr, load, sub, lambda kb, side: None, init, trips=trips)
        for rw, (m, l, acc) in zip(rows, out):
            y_ref[rw, :] = jnp.where(lo, acc[0] / l[0], acc[1] / l[1]).astype(BF16)
            lse_ref[0, rw, :] = jnp.where(lo, m[0] + jnp.log(l[0]), m[1] + jnp.log(l[1]))

    qblk, kvfull = _pair_specs(s_len, t * nr)
    return _ride_call(
        body, name, (N_HEADS // 2, nq),
        [qblk, kvfull, kvfull,
         pl.BlockSpec((t * nr, LANES), lambda hp, i: (i, 0)),
         pl.BlockSpec((N_HEADS, s_len), lambda hp, i: (0, 0)),
         pl.BlockSpec((1, LANES), lambda hp, i: (0, hp)),
         pl.BlockSpec(memory_space=pltpu.SMEM)],
        [qblk, pl.BlockSpec((1, t * nr, LANES), lambda hp, i: (hp, i, 0))],
        [jax.ShapeDtypeStruct((s_len, ATT_W), BF16), jax.ShapeDtypeStruct((N_HEADS // 2, s_len, LANES), F32)],
        [], ("parallel", "parallel"), (q, k, v, f_col, f_row, kmax, f_row[:, t - 1::t]), ride)


def _fox_bwd(q, k, v, dy, y, lse, f_col, f_row, kmax, name, ride=None):
    s_len = q.shape[0]
    t, nr, nq = _att_tiling(s_len)

    def body(q_ref, k_ref, v_ref, dy_ref, y_ref, lse_ref, f_ref, ft_ref, km_ref, fl_ref,
             dq_ref, dk_ref, dv_ref, dft_ref):
        hp = pl.program_id(0)
        i = pl.program_id(1)

        @pl.when(i == 0)
        def _():
            dk_ref[...] = jnp.zeros_like(dk_ref)
            dv_ref[...] = jnp.zeros_like(dv_ref)
            dft_ref[...] = jnp.zeros_like(dft_ref)

        lane = lax.broadcasted_iota(jnp.int32, (t, LANES), 1)
        lo = lane < HEAD_DIM
        causal = lax.broadcasted_iota(jnp.int32, (t, t), 0) >= lax.broadcasted_iota(jnp.int32, (t, t), 1)
        rows = [pl.ds(r * t, t) for r in range(nr)]
        qst, dyst, delta, lse, fq = [], [], [], [], []
        for rw in rows:
            qst.append(_stacked_halves(q_ref[rw, :] * jnp.asarray(HEAD_DIM ** -0.5, BF16), lo))
            dyb = dy_ref[rw, :]
            dyst.append(_stacked_halves(dyb, lo))
            prod = dyb.astype(F32) * y_ref[rw, :].astype(F32)
            delta.append([jnp.sum(jnp.where(lo, prod, 0.0), axis=1, keepdims=True),
                          jnp.sum(jnp.where(lo, 0.0, prod), axis=1, keepdims=True)])
            lse_b = lse_ref[0, rw, :]
            lse.append([lse_b[:, 0:1], lse_b[:, HEAD_DIM:HEAD_DIM + 1]])
            fq.append([jnp.sum(jnp.where(lane == 2 * hp + j, f_ref[rw, :], 0.0), axis=1, keepdims=True)
                       for j in (0, 1)])

        q_all, dy_all = _stack(qst), _stack(dyst)

        def load(kb):
            k0 = pl.multiple_of(kb * t, t)
            frow = [ft_ref[pl.ds(2 * hp + j, 1), pl.ds(k0, t)] for j in (0, 1)]
            return k_ref[pl.ds(k0, t), :], v_ref[pl.ds(k0, t), :], frow

        def sub(rs, tiles, carries, masked):
            kblk, vblk, frow = tiles
            qs, dys = (q_all, dy_all) if len(rs) == nr else (_stack([qst[r] for r in rs]), _stack([dyst[r] for r in rs]))
            z = _dot_nt(qs, kblk)
            dp = _dot_nt(dys, vblk)
            pb, dsb, rsum, col = [], [], [], [None, None]
            for n, (r, j) in enumerate((r, j) for r in range(len(rs)) for j in (0, 1)):
                sl = slice(n * t, (n + 1) * t)
                s = z[sl, :] + (fq[rs[r]][j] - frow[j])
                p = jnp.exp(s - lse[rs[r]][j])
                if masked[r]:
                    p = jnp.where(causal, p, 0.0)
                ds = p * (dp[sl, :] - delta[rs[r]][j])
                c = jnp.sum(ds, axis=0, keepdims=True)
                col[j] = c if col[j] is None else col[j] + c
                rsum.append(carries[r][1][j] + jnp.sum(ds, axis=1, keepdims=True))
                pb.append(p.astype(BF16))
                dsb.append(ds.astype(BF16))
            p_all, ds_all = _stack(pb), _stack(dsb)
            dqs = _dot(ds_all, kblk)
            out = []
            for r in range(len(rs)):
                dq = carries[r][0]
                dq = (dq[0] + dqs[2 * r * t:(2 * r + 1) * t, :], dq[1] + dqs[(2 * r + 1) * t:(2 * r + 2) * t, :])
                out.append((dq, (rsum[2 * r], rsum[2 * r + 1])))
            return out, (_dot_tn(ds_all, qs), _dot_tn(p_all, dys), col)

        def flush(kb, side):
            k0 = pl.multiple_of(kb * t, t)
            dk_ref[pl.ds(k0, t), :] += side[0]
            dv_ref[pl.ds(k0, t), :] += side[1]
            for j in (0, 1):
                dft_ref[0, pl.ds(j, 1), pl.ds(k0, t)] -= side[2][j]

        zero = jnp.zeros((t, 1), F32)
        zacc = jnp.zeros((t, LANES), F32)
        qkb = [_fox_qk_bound(qs, km_ref, t) for qs in qst]
        trips = _fox_trips(hp, fl_ref, qkb, fq, lambda carries, r, j: lse[r][j])
        out = _walk_tiles(i, nr, load, sub, flush, [((zacc, zacc), (zero, zero))] * nr, trips=trips)
        for r, (rw, (dq, rs)) in enumerate(zip(rows, out)):
            dq_ref[rw, :] = jnp.where(lo, dq[0], dq[1]) * (HEAD_DIM ** -0.5)
            rs_t = jnp.where(lo, rs[0], rs[1]).T
            q0 = pl.multiple_of((i * nr + r) * t, t)
            for j in (0, 1):
                dft_ref[0, pl.ds(j, 1), pl.ds(q0, t)] += rs_t[j * HEAD_DIM:j * HEAD_DIM + 1, :]

    qblk, kvfull = _pair_specs(s_len, t * nr)
    return _ride_call(
        body, name, (N_HEADS // 2, nq),
        [qblk, kvfull, kvfull, qblk, qblk,
         pl.BlockSpec((1, t * nr, LANES), lambda hp, i: (hp, i, 0)),
         pl.BlockSpec((t * nr, LANES), lambda hp, i: (i, 0)),
         pl.BlockSpec((N_HEADS, s_len), lambda hp, i: (0, 0)),
         pl.BlockSpec((1, LANES), lambda hp, i: (0, hp)),
         pl.BlockSpec(memory_space=pltpu.SMEM)],
        [qblk, kvfull, kvfull, pl.BlockSpec((1, 8, s_len), lambda hp, i: (hp, 0, 0))],
        [jax.ShapeDtypeStruct((s_len, ATT_W), F32)] * 3 + [jax.ShapeDtypeStruct((N_HEADS // 2, 8, s_len), F32)],
        [], ("arbitrary", "arbitrary"), (q, k, v, dy, y, lse, f_col, f_row, kmax, f_row[:, t - 1::t]), ride)


def _sb_more(carry):
    return (jnp.max(jnp.maximum(carry[0][0], carry[0][1])) > -EXP_ZERO).astype(jnp.int32)


def _stacked_split_dot(slabs, m, parts):
    split = [_split(x, parts) for x in slabs]
    acc = None
    for p in range(parts):
        d = _dot(_stack([s[p] for s in split]), m)
        acc = d if acc is None else acc + d
    return acc


def _sb_weights(z, c, strict, upper, t):
    logs = []
    for n in range(z.shape[0] // t):
        zn = z[n * t:(n + 1) * t, :]
        sp = _softplus_neg_abs(zn)
        l1m = jnp.minimum(-zn, 0.0) - sp
        if strict[n] is not None:
            l1m = jnp.where(strict[n], l1m, 0.0)
        logs.append((jnp.minimum(zn, 0.0) - sp, l1m))
    suf = _stacked_split_dot([l1m for _, l1m in logs], upper, 2)
    out = []
    for n, (logb, l1m) in enumerate(logs):
        after = c[n] + suf[n * t:(n + 1) * t, :]
        a = jnp.exp(logb + after)
        if strict[n] is not None:
            a = jnp.where(strict[n], a, 0.0)
        out.append((logb, a, after[:, 0:1] + l1m[:, 0:1]))
    return out


def _sb_fwd(q, k, v, name):
    s_len = q.shape[0]
    t, nr, nq = _att_tiling(s_len)

    def body(q_ref, k_ref, v_ref, y_ref, yf_ref):
        i = pl.program_id(1)
        lane = lax.broadcasted_iota(jnp.int32, (t, LANES), 1)
        lo = lane < HEAD_DIM
        ri = lax.broadcasted_iota(jnp.int32, (t, t), 0)
        ci = lax.broadcasted_iota(jnp.int32, (t, t), 1)
        strict = ci < ri
        upper = (ri > ci).astype(BF16)
        rows = [pl.ds(r * t, t) for r in range(nr)]
        qst = [_stacked_halves(q_ref[rw, :] * jnp.asarray(HEAD_DIM ** -0.5, BF16), lo) for rw in rows]
        q_all = _stack(qst)

        def load(kb):
            k0 = pl.multiple_of(kb * t, t)
            return k_ref[pl.ds(k0, t), :], v_ref[pl.ds(k0, t), :]

        def sub(rs, tiles, carries, masked):
            kblk, vblk = tiles
            z = _dot_nt(q_all if len(rs) == nr else _stack([qst[r] for r in rs]), kblk)
            c = [carries[r][0][j] for r in range(len(rs)) for j in (0, 1)]
            w = _sb_weights(z, c, [strict if masked[r] else None for r in range(len(rs)) for j in (0, 1)], upper, t)
            pv = _dot(_stack([a.astype(BF16) for _, a, _ in w]), vblk)
            out = []
            for r in range(len(rs)):
                acc = carries[r][1]
                acc = (acc[0] + pv[2 * r * t:(2 * r + 1) * t, :], acc[1] + pv[(2 * r + 1) * t:(2 * r + 2) * t, :])
                out.append(((w[2 * r][2], w[2 * r + 1][2]), acc))
            return out, None

        zero = jnp.zeros((t, 1), F32)
        zacc = jnp.zeros((t, LANES), F32)
        out = _walk_tiles(i, nr, load, sub, lambda kb, side: None, [((zero, zero), (zacc, zacc))] * nr, more=_sb_more)
        for rw, (_, acc) in zip(rows, out):
            y = jnp.where(lo, acc[0], acc[1])
            y_ref[rw, :] = y.astype(BF16)
            yf_ref[rw, :] = y

    qblk, kvfull = _pair_specs(s_len, t * nr)
    return pl.pallas_call(
        body, name=name, grid=(N_HEADS // 2, nq),
        in_specs=[qblk, kvfull, kvfull],
        out_specs=[qblk, qblk],
        out_shape=[jax.ShapeDtypeStruct((s_len, ATT_W), BF16), jax.ShapeDtypeStruct((s_len, ATT_W), F32)],
        compiler_params=_cparams("parallel", "parallel"),
    )(q, k, v)


def _sb_bwd(q, k, v, dy, yf, name):
    s_len = q.shape[0]
    t, nr, nq = _att_tiling(s_len)

    def body(q_ref, k_ref, v_ref, dy_ref, yf_ref, dq_ref, dk_ref, dv_ref):
        i = pl.program_id(1)

        @pl.when(i == 0)
        def _():
            dk_ref[...] = jnp.zeros_like(dk_ref)
            dv_ref[...] = jnp.zeros_like(dv_ref)

        lane = lax.broadcasted_iota(jnp.int32, (t, LANES), 1)
        lo = lane < HEAD_DIM
        ri = lax.broadcasted_iota(jnp.int32, (t, t), 0)
        ci = lax.broadcasted_iota(jnp.int32, (t, t), 1)
        strict = ci < ri
        upper = (ri > ci).astype(BF16)
        upper_incl = (ri >= ci).astype(BF16)
        rows = [pl.ds(r * t, t) for r in range(nr)]
        qst, dyst, delta = [], [], []
        for rw in rows:
            qst.append(_stacked_halves(q_ref[rw, :] * jnp.asarray(HEAD_DIM ** -0.5, BF16), lo))
            dyb = dy_ref[rw, :]
            dyst.append(_stacked_halves(dyb, lo))
            prod = dyb.astype(F32) * yf_ref[rw, :]
            delta.append([jnp.sum(jnp.where(lo, prod, 0.0), axis=1, keepdims=True),
                          jnp.sum(jnp.where(lo, 0.0, prod), axis=1, keepdims=True)])
        q_all, dy_all = _stack(qst), _stack(dyst)

        def load(kb):
            k0 = pl.multiple_of(kb * t, t)
            return k_ref[pl.ds(k0, t), :], v_ref[pl.ds(k0, t), :]

        def sub(rs, tiles, carries, masked):
            kblk, vblk = tiles
            qs, dys = (q_all, dy_all) if len(rs) == nr else (_stack([qst[r] for r in rs]), _stack([dyst[r] for r in rs]))
            slabs = [(r, j) for r in range(len(rs)) for j in (0, 1)]
            z = _dot_nt(qs, kblk)
            w = _sb_weights(z, [carries[r][0][j] for r, j in slabs], [strict if masked[r] else None for r, j in slabs],
                            upper, t)
            da = _dot_nt(dys, vblk)
            ab = [a.astype(BF16) for _, a, _ in w]
            dl = [ab[n].astype(F32) * da[n * t:(n + 1) * t, :] for n in range(len(slabs))]
            tail = _stacked_split_dot(dl, upper_incl, 2)
            dzb, e_new = [], []
            for n, (r, j) in enumerate(slabs):
                tl = tail[n * t:(n + 1) * t, :]
                e = carries[r][1][j]
                dl1m = (delta[rs[r]][j] - e) - tl
                e_new.append(e + tl[:, 0:1])
                dz = dl[n] - jnp.exp(w[n][0]) * (dl[n] + dl1m)
                if masked[r]:
                    dz = jnp.where(strict, dz, 0.0)
                dzb.append(dz.astype(BF16))
            a_all, dz_all = _stack(ab), _stack(dzb)
            dqs = _dot(dz_all, kblk)
            out = []
            for r in range(len(rs)):
                dq = carries[r][2]
                dq = (dq[0] + dqs[2 * r * t:(2 * r + 1) * t, :], dq[1] + dqs[(2 * r + 1) * t:(2 * r + 2) * t, :])
                out.append(((w[2 * r][2], w[2 * r + 1][2]), (e_new[2 * r], e_new[2 * r + 1]), dq))
            return out, (_dot_tn(dz_all, qs), _dot_tn(a_all, dys))

        def flush(kb, side):
            k0 = pl.multiple_of(kb * t, t)
            dk_ref[pl.ds(k0, t), :] += side[0]
            dv_ref[pl.ds(k0, t), :] += side[1]

        zero = jnp.zeros((t, 1), F32)
        zacc = jnp.zeros((t, LANES), F32)
        out = _walk_tiles(i, nr, load, sub, flush, [((zero, zero), (zero, zero), (zacc, zacc))] * nr, more=_sb_more)
        for rw, (_, _, dq) in zip(rows, out):
            dq_ref[rw, :] = jnp.where(lo, dq[0], dq[1]) * (HEAD_DIM ** -0.5)

    qblk, kvfull = _pair_specs(s_len, t * nr)
    return pl.pallas_call(
        body, name=name, grid=(N_HEADS // 2, nq),
        in_specs=[qblk, kvfull, kvfull, qblk, qblk],
        out_specs=[qblk, kvfull, kvfull],
        out_shape=[jax.ShapeDtypeStruct((s_len, ATT_W), F32)] * 3,
        compiler_params=_cparams("arbitrary", "arbitrary"),
    )(q, k, v, dy, yf)


def _merge_fwd(x, yf, ys, gf, gs, wbf, wbs, wo, name):
    s_len = x.shape[0]
    ts = min(512, s_len)

    def body(x_ref, yf_ref, ys_ref, gf_ref, gs_ref, wbf_ref, wbs_ref, wo_ref, o_ref):
        merged = (_sigmoid(gf_ref[...]) * _dot_nt(yf_ref[...], wbf_ref[...])
                  + _sigmoid(gs_ref[...]) * _dot_nt(ys_ref[...], wbs_ref[...]))
        o_ref[...] = x_ref[...] + _dot(merged.astype(BF16), wo_ref[...])

    tok = lambda w: pl.BlockSpec((ts, w), lambda i: (i, 0))
    full = lambda a: pl.BlockSpec(a.shape, lambda i: (0, 0))
    return pl.pallas_call(
        body, name=name, grid=(s_len // ts,),
        in_specs=[tok(D_MODEL), tok(ATT_W), tok(ATT_W), tok(D_MODEL), tok(D_MODEL), full(wbf), full(wbs), full(wo)],
        out_specs=tok(D_MODEL),
        out_shape=jax.ShapeDtypeStruct((s_len, D_MODEL), F32),
        compiler_params=_cparams("parallel"),
    )(x, yf, ys, gf, gs, wbf, wbs, wo)


def _merge_bwd(dx, yf, ys, gf, gs, wbf, wbs, wo, name):
    s_len = dx.shape[0]
    ts = min(512, s_len)

    def body(dx_ref, yf_ref, ys_ref, gf_ref, gs_ref, wbf_ref, wbs_ref, wo_ref,
             dyf_ref, dys_ref, dgf_ref, dgs_ref, dbf_ref, dbs_ref, mg_ref):
        bf = _dot_nt(yf_ref[...], wbf_ref[...])
        bs = _dot_nt(ys_ref[...], wbs_ref[...])
        sf = _sigmoid(gf_ref[...])
        ss = _sigmoid(gs_ref[...])
        mg_ref[...] = (sf * bf + ss * bs).astype(BF16)
        dm = _dot_nt(dx_ref[...].astype(BF16), wo_ref[...])
        dbf = (dm * sf).astype(BF16)
        dbs = (dm * ss).astype(BF16)
        dbf_ref[...] = dbf
        dbs_ref[...] = dbs
        dgf_ref[...] = (dm * bf * (sf * (1.0 - sf))).astype(BF16)
        dgs_ref[...] = (dm * bs * (ss * (1.0 - ss))).astype(BF16)
        dyf_ref[...] = _dot(dbf, wbf_ref[...]).astype(BF16)
        dys_ref[...] = _dot(dbs, wbs_ref[...]).astype(BF16)

    tok = lambda w: pl.BlockSpec((ts, w), lambda i: (i, 0))
    full = lambda a: pl.BlockSpec(a.shape, lambda i: (0, 0))
    b16o = lambda w: jax.ShapeDtypeStruct((s_len, w), BF16)
    return pl.pallas_call(
        body, name=name, grid=(s_len // ts,),
        in_specs=[tok(D_MODEL), tok(ATT_W), tok(ATT_W), tok(D_MODEL), tok(D_MODEL), full(wbf), full(wbs), full(wo)],
        out_specs=[tok(ATT_W), tok(ATT_W)] + [tok(D_MODEL)] * 5,
        out_shape=[b16o(ATT_W), b16o(ATT_W)] + [b16o(D_MODEL)] * 5,
        compiler_params=_cparams("parallel"),
    )(dx, yf, ys, gf, gs, wbf, wbs, wo)


def _mix_bwd(x, dx_in, gain, w_in, fqr, fkr, dfqn, dfkn, qg, kg, dfv, df_col, logf, dsq, dsk, dsv, dgf, dgs, name):
    s_len = x.shape[0]
    ts = min(256, s_len)
    nt = s_len // ts
    gmat = _head_group_matrix()

    def body(x_ref, dxi_ref, gain_ref, w_ref, fqr_ref, fkr_ref, dfqn_ref, dfkn_ref, qg_ref, kg_ref, gm_ref,
             dfv_ref, df_ref, logf_ref, dsq_ref, dsk_ref, dsv_ref, dgf_ref, dgs_ref,
             dp_ref, dx_ref, dgain_ref, dqg_ref, dkg_ref, dbias_ref, carry):
        i = pl.program_id(0)

        @pl.when(i == 0)
        def _():
            carry[...] = jnp.zeros_like(carry)
            dgain_ref[...] = jnp.zeros_like(dgain_ref)
            dqg_ref[...] = jnp.zeros_like(dqg_ref)
            dkg_ref[...] = jnp.zeros_like(dkg_ref)
            dbias_ref[...] = jnp.zeros_like(dbias_ref)

        gm = gm_ref[...]

        def headnorm_bwd(raw, dout, g, dg_ref):
            ms = _dot_split(raw * raw, gm, HEAD_SUM_PARTS) * (1.0 / HEAD_DIM)
            r = lax.rsqrt(ms + EPS)
            nrm = raw * r
            dg_ref[...] += jnp.sum(dout * nrm, axis=0, keepdims=True)
            dn = dout * g
            mean_h = _dot_split(dn * nrm, gm, HEAD_SUM_PARTS) * (1.0 / HEAD_DIM)
            return r * (dn - nrm * mean_h)

        dp_ref[:, C_FQ:C_FQ + ATT_W] = headnorm_bwd(fqr_ref[...], dfqn_ref[...], qg_ref[...], dqg_ref).astype(BF16)
        dp_ref[:, C_FK:C_FK + ATT_W] = headnorm_bwd(fkr_ref[...], dfkn_ref[...], kg_ref[...], dkg_ref).astype(BF16)
        dp_ref[:, C_FV:C_FV + ATT_W] = dfv_ref[...].astype(BF16)
        dp_ref[:, C_SQ:C_SQ + ATT_W] = dsq_ref[...].astype(BF16)
        dp_ref[:, C_SK:C_SK + ATT_W] = dsk_ref[...].astype(BF16)
        dp_ref[:, C_SV:C_SV + ATT_W] = dsv_ref[...].astype(BF16)
        dp_ref[:, C_GF:C_GF + D_MODEL] = dgf_ref[...]
        dp_ref[:, C_GS:C_GS + D_MODEL] = dgs_ref[...]

        r_ = lax.broadcasted_iota(jnp.int32, (ts, ts), 0)
        c_ = lax.broadcasted_iota(jnp.int32, (ts, ts), 1)
        rev = (c_ >= r_).astype(BF16)
        dlogf = _dot_split_left(rev, df_ref[...], 3) + carry[...]
        carry[...] = dlogf[0:1, :]
        lane = lax.broadcasted_iota(jnp.int32, (ts, LANES), 1)
        dfl = jnp.where(lane < N_HEADS, dlogf * (1.0 - jnp.exp(logf_ref[...])), 0.0)
        dbias_ref[...] += jnp.sum(dfl, axis=0, keepdims=True)
        dp_ref[:, C_FL:C_FL + LANES] = dfl.astype(BF16)
        dp_ref[:, C_FL + LANES:C_SQ] = jnp.zeros((ts, C_SQ - C_FL - LANES), BF16)

        dh = _dot(dp_ref[...], w_ref[...])
        xf = x_ref[...]
        r = _rms_rinv(xf)
        xhat = xf * r
        dgain_ref[...] += jnp.sum(dh * xhat, axis=0, keepdims=True)
        dn = dh * gain_ref[...]
        dx_ref[...] = dxi_ref[...] + r * (dn - xhat * jnp.mean(dn * xhat, axis=-1, keepdims=True))

    tok = lambda w: pl.BlockSpec((ts, w), lambda i: (nt - 1 - i, 0))
    full = lambda a: pl.BlockSpec(a.shape, lambda i: (0, 0))
    row = lambda w: pl.BlockSpec((1, w), lambda i: (0, 0))
    return _ride_call(
        body, name, (nt,),
        [tok(D_MODEL), tok(D_MODEL), full(gain), full(w_in), tok(ATT_W), tok(ATT_W), tok(ATT_W), tok(ATT_W),
         full(qg), full(kg), full(gmat), tok(ATT_W), tok(LANES), tok(LANES), tok(ATT_W), tok(ATT_W), tok(ATT_W),
         tok(D_MODEL), tok(D_MODEL)],
        [tok(IN_PAD), tok(D_MODEL), row(D_MODEL), row(ATT_W), row(ATT_W), row(LANES)],
        [jax.ShapeDtypeStruct((s_len, IN_PAD), BF16), jax.ShapeDtypeStruct((s_len, D_MODEL), F32),
         jax.ShapeDtypeStruct((1, D_MODEL), F32), jax.ShapeDtypeStruct((1, ATT_W), F32),
         jax.ShapeDtypeStruct((1, ATT_W), F32), jax.ShapeDtypeStruct((1, LANES), F32)],
        [pltpu.VMEM((1, LANES), F32)], ("arbitrary",),
        (x, dx_in, gain, w_in, fqr, fkr, dfqn, dfkn, qg, kg, gmat, dfv, df_col, logf, dsq, dsk, dsv, dgf, dgs), None)


def _ple_loss(x, p, tgt, gain, wpg, wpp, name):
    s_len = x.shape[0]
    ts = min(512, s_len)

    def body(x_ref, p_ref, t_ref, gain_ref, wpg_ref, wpp_ref, dx_ref, n_ref, ds_ref, dpp_ref, dgain_ref, loss_ref):
        i = pl.program_id(0)

        @pl.when(i == 0)
        def _():
            dgain_ref[...] = jnp.zeros_like(dgain_ref)
            loss_ref[...] = jnp.zeros_like(loss_ref)

        xf = x_ref[...]
        r = _rms_rinv(xf)
        n = xf * r
        hn = (n * gain_ref[...]).astype(BF16)
        n_ref[...] = hn
        sg = _sigmoid(_dot(hn, wpg_ref[...]))
        pp = _dot_nt(p_ref[...].astype(BF16), wpp_ref[...])
        err = (xf + sg * pp) - t_ref[...]
        sq = jnp.sum(jnp.sum(err * err, axis=1, keepdims=True), axis=0, keepdims=True)
        loss_ref[...] += (0.5 / D_MODEL) * sq
        dout = err * (1.0 / D_MODEL)
        dpp_ref[...] = (dout * sg).astype(BF16)
        ds = (dout * pp * (sg * (1.0 - sg))).astype(BF16)
        ds_ref[...] = ds
        dhn = _dot_nt(ds, wpg_ref[...])
        dgain_ref[...] += jnp.sum(dhn * n, axis=0, keepdims=True)
        dn = dhn * gain_ref[...]
        dx_ref[...] = dout + r * (dn - n * jnp.mean(dn * n, axis=-1, keepdims=True))

    tok = lambda w: pl.BlockSpec((ts, w), lambda i: (i, 0))
    full = lambda a: pl.BlockSpec(a.shape, lambda i: (0, 0))
    return pl.pallas_call(
        body, name=name, grid=(s_len // ts,),
        in_specs=[tok(D_MODEL), tok(PLE_DIM), tok(D_MODEL), full(gain), full(wpg), full(wpp)],
        out_specs=[tok(D_MODEL), tok(D_MODEL), tok(D_MODEL), tok(D_MODEL),
                   pl.BlockSpec((1, D_MODEL), lambda i: (0, 0)), pl.BlockSpec((8, LANES), lambda i: (0, 0))],
        out_shape=[jax.ShapeDtypeStruct((s_len, D_MODEL), F32), jax.ShapeDtypeStruct((s_len, D_MODEL), BF16),
                   jax.ShapeDtypeStruct((s_len, D_MODEL), BF16), jax.ShapeDtypeStruct((s_len, D_MODEL), BF16),
                   jax.ShapeDtypeStruct((1, D_MODEL), F32), jax.ShapeDtypeStruct((8, LANES), F32)],
        compiler_params=_cparams("arbitrary"),
    )(x, p, tgt, gain, wpg, wpp)


def _exchange(x, name, broadcast):
    def body(x_ref, out_ref, send_sems, recv_sems, local_sem):
        _exchange_start(x_ref, out_ref, send_sems, recv_sems, local_sem, broadcast)
        _exchange_wait(x_ref, out_ref, send_sems, recv_sems, local_sem, broadcast)

    return pl.pallas_call(
        body, name=name,
        in_specs=[EXCHANGE_SPEC],
        out_specs=EXCHANGE_SPEC,
        out_shape=_exchange_shape(x, broadcast),
        scratch_shapes=list(EXCHANGE_SEMS),
        compiler_params=pltpu.CompilerParams(has_side_effects=True),
    )(x)


def _gather_two_level(x, name):
    def body(x_ref, out_ref, send_sems, recv_sems, local_sem):
        mx, my, mc = lax.axis_index("x"), lax.axis_index("y"), lax.axis_index("c")
        me, sibling = (mx, my, mc), (mx, my, 1 - mc)
        chips = [(1 - mx, my), (mx, 1 - my), (1 - mx, 1 - my)]

        def slot(px, py, pc):
            return out_ref.at[4 * px + 2 * py + pc]

        def copy(k, block, to, src=None):
            return pltpu.make_async_remote_copy(
                src_ref=slot(*block) if src is None else src, dst_ref=slot(*block),
                send_sem=send_sems.at[k], recv_sem=recv_sems.at[k], device_id=to, device_id_type=MESH)

        mine = pltpu.make_async_copy(x_ref, slot(*me), local_sem)
        mine.start()
        first = [copy(0, me, sibling, src=x_ref)]
        first += [copy(1 + j, me, (*chip, mc), src=x_ref) for j, chip in enumerate(chips)]
        for cp in first:
            cp.start()
        passed = [copy(4 + j, (*chip, mc), sibling) for j, chip in enumerate(chips)]
        for j, chip in enumerate(chips):
            copy(1 + j, (*chip, mc), me).wait_recv()
            passed[j].start()
        copy(0, sibling, me).wait_recv()
        for j, chip in enumerate(chips):
            copy(4 + j, (*chip, 1 - mc), me).wait_recv()
        for cp in first + passed:
            cp.wait_send()
        mine.wait()

    return pl.pallas_call(
        body, name=name,
        in_specs=[EXCHANGE_SPEC],
        out_specs=EXCHANGE_SPEC,
        out_shape=_exchange_shape(x, True),
        scratch_shapes=list(EXCHANGE_SEMS),
        compiler_params=pltpu.CompilerParams(has_side_effects=True),
    )(x)


EXCHANGE_SPEC = pl.BlockSpec(memory_space=pl.ANY)
EXCHANGE_SEMS = (pltpu.SemaphoreType.DMA((N_DEV - 1,)), pltpu.SemaphoreType.DMA((N_DEV - 1,)), pltpu.SemaphoreType.DMA)


def _exchange_shape(x, broadcast):
    return jax.ShapeDtypeStruct((N_DEV,) + tuple(x.shape if broadcast else x.shape[1:]), x.dtype)


def _exchange_copies(x_ref, out_ref, send_sems, recv_sems, local_sem, broadcast, with_recv=True):
    mx, my, mc = lax.axis_index("x"), lax.axis_index("y"), lax.axis_index("c")
    me = 4 * mx + 2 * my + mc

    def src(idx):
        return x_ref if broadcast else x_ref.at[idx]

    local = pltpu.make_async_copy(src(me), out_ref.at[me], local_sem)
    pairs = []
    for k in range(1, N_DEV):
        px = (1 - mx) if k & 4 else mx
        py = (1 - my) if k & 2 else my
        pc = (1 - mc) if k & 1 else mc
        peer = 4 * px + 2 * py + pc
        sems = dict(send_sem=send_sems.at[k - 1], recv_sem=recv_sems.at[k - 1], device_id=(px, py, pc), device_id_type=MESH)
        recv = pltpu.make_async_remote_copy(src_ref=src(peer), dst_ref=out_ref.at[peer], **sems) if with_recv else None
        pairs.append((pltpu.make_async_remote_copy(src_ref=src(peer), dst_ref=out_ref.at[me], **sems), recv))
    return local, pairs


def _exchange_start(*refs_and_mode):
    local, pairs = _exchange_copies(*refs_and_mode, with_recv=False)
    local.start()
    for send, _ in pairs:
        send.start()


def _exchange_wait(*refs_and_mode):
    local, pairs = _exchange_copies(*refs_and_mode)
    for _, recv in pairs:
        recv.wait_recv()
    for send, _ in pairs:
        send.wait_send()
    local.wait()


def _riding(body, grid, n_in, n_out, ride):
    if ride is None:
        return body
    broadcast = ride[1]

    def wrapped(*refs):
        ins, x_ref = refs[:n_in], refs[n_in]
        outs, out_ref = refs[n_in + 1:n_in + 1 + n_out], refs[n_in + 1 + n_out]
        scratch, sems = refs[n_in + 2 + n_out:-3], refs[-3:]
        step = pl.program_id(0)
        for d in range(1, len(grid)):
            step = step * grid[d] + pl.program_id(d)
        total = 1
        for g in grid:
            total *= g

        @pl.when(step == 0)
        def _():
            _exchange_start(x_ref, out_ref, *sems, broadcast)

        body(*ins, *outs, *scratch)

        @pl.when(step == total - 1)
        def _():
            _exchange_wait(x_ref, out_ref, *sems, broadcast)

    return wrapped


def _ride_call(body, name, grid, in_specs, out_specs, out_shape, scratch_shapes, sem, operands, ride):
    if ride is None:
        return pl.pallas_call(body, name=name, grid=grid, in_specs=in_specs, out_specs=out_specs, out_shape=out_shape,
                              scratch_shapes=scratch_shapes, compiler_params=_cparams(*sem))(*operands)
    return pl.pallas_call(
        _riding(body, grid, len(in_specs), len(out_specs), ride), name=name, grid=grid,
        in_specs=list(in_specs) + [EXCHANGE_SPEC], out_specs=list(out_specs) + [EXCHANGE_SPEC],
        out_shape=list(out_shape) + [_exchange_shape(*ride)],
        scratch_shapes=list(scratch_shapes) + list(EXCHANGE_SEMS),
        compiler_params=_cparams(*(["arbitrary"] * len(grid))),
    )(*operands, ride[0])


def _adamw_math(w, g, m, v):
    m2 = ADAM_B1 * m + (1.0 - ADAM_B1) * g
    v2 = ADAM_B2 * v + (1.0 - ADAM_B2) * (g * g)
    m_hat = m2 / (1.0 - ADAM_B1 ** ADAM_STEP)
    v_hat = v2 / (1.0 - ADAM_B2 ** ADAM_STEP)
    delta = -ADAM_LR * (m_hat / (jnp.sqrt(v_hat) + ADAM_EPS) + ADAM_WD * w)
    return delta, m2, v2


def _sum_parts(parts, name, tr):
    _, rows, cols = parts.shape

    def body(p_ref, g_ref):
        g = p_ref[0].astype(F32)
        for s in range(1, N_DEV):
            g = g + p_ref[s].astype(F32)
        g_ref[...] = g

    return pl.pallas_call(
        body, name=name, grid=(rows // tr,),
        in_specs=[pl.BlockSpec((N_DEV, tr, cols), lambda i: (0, i, 0))],
        out_specs=pl.BlockSpec((tr, cols), lambda i: (i, 0)),
        out_shape=jax.ShapeDtypeStruct((rows, cols), F32),
        compiler_params=_cparams("parallel"),
    )(parts)


ADAM_SPLIT_ELEMS = 400_000


def _adamw_shard(g, w, m, v, name):
    rows, cols = w.shape
    tr = rows // 2 if rows * cols > ADAM_SPLIT_ELEMS else rows

    def body(g_ref, w_ref, m_ref, v_ref, d_ref, m2_ref, v2_ref):
        d_ref[...], m2_ref[...], v2_ref[...] = _adamw_math(w_ref[...], g_ref[...], m_ref[...], v_ref[...])

    blk = pl.BlockSpec((tr, cols), lambda i: (i, 0))
    return pl.pallas_call(
        body, name=name, grid=(rows // tr,),
        in_specs=[blk] * 4, out_specs=[blk] * 3,
        out_shape=[jax.ShapeDtypeStruct((rows, cols), F32)] * 3,
        compiler_params=_cparams("parallel"),
    )(g, w, m, v)


def _adamw(parts, w, m, v, name, tr):
    rows, cols = w.shape

    def body(p_ref, w_ref, m_ref, v_ref, g_ref, d_ref, m2_ref, v2_ref):
        g = p_ref[0].astype(F32)
        for s in range(1, N_DEV):
            g = g + p_ref[s].astype(F32)
        g_ref[...] = g
        d_ref[...], m2_ref[...], v2_ref[...] = _adamw_math(w_ref[...], g, m_ref[...], v_ref[...])

    blk = pl.BlockSpec((tr, cols), lambda i: (i, 0))
    return pl.pallas_call(
        body, name=name, grid=(rows // tr,),
        in_specs=[pl.BlockSpec((N_DEV, tr, cols), lambda i: (0, i, 0)), blk, blk, blk],
        out_specs=[blk] * 4,
        out_shape=[jax.ShapeDtypeStruct((rows, cols), F32)] * 4,
        compiler_params=_cparams("parallel"),
    )(parts, w, m, v)


TRANSPOSED = frozenset(("ffn1_w_gate", "ffn1_w_up", "w_in", "w_branch_fox", "w_branch_sb", "ffn2_w_gate", "ffn2_w_up",
                        "w_ple_proj"))
F_PAD_ROWS = C_SQ - FL_REAL_END


def _pack(pieces, group, dtype):
    out = []
    for name in GATHER_GROUPS[group]:
        r = pieces[name].T if name in TRANSPOSED else pieces[name]
        r = r.reshape(-1, D_MODEL).astype(dtype)
        if r.shape[0] != PACK_ROWS[name]:
            r = jnp.pad(r, ((0, PACK_ROWS[name] - r.shape[0]), (0, 0)))
        out.append(r)
    return jnp.concatenate(out, axis=0)


def _real_rows(name):
    return W_IN_ROWS if name == "w_in" else PACK_ROWS[name]


def _gathered(got, name, shape):
    off = GATHER_OFF[name]
    return got[:, off:off + _real_rows(name), :].reshape(shape)


def _w_in_device_rows(d):
    lo, hi = d * W_IN_ROWS, (d + 1) * W_IN_ROWS
    if hi <= FL_REAL_END:
        return [(lo, hi)]
    if lo >= FL_REAL_END:
        return [(lo + F_PAD_ROWS, hi + F_PAD_ROWS)]
    return [(lo, FL_REAL_END), (C_SQ, hi + F_PAD_ROWS)]


def _w_in_t_padded(got):
    t = _gathered(got, "w_in", (IN_REAL, D_MODEL))
    return jnp.concatenate([t[:FL_REAL_END], jnp.zeros((F_PAD_ROWS, D_MODEL), t.dtype), t[FL_REAL_END:]], axis=0)


def _pack_chunks(grads, group):
    out = []
    for name in SCATTER_GROUPS[group]:
        base, _, half = name.partition("#")
        g = grads[base].astype(BF16)
        if base == "w_in":
            lo, hi = W_IN_HALVES[int(half)]
            tail = jnp.zeros((PACK_ROWS[base] - W_IN_ROWS, D_MODEL), BF16)
            c = jnp.stack([jnp.concatenate([g[a:b] for a, b in _w_in_device_rows(d)] + [tail], axis=0)[lo:hi]
                           for d in range(N_DEV)])
        else:
            c = g.reshape(N_DEV, PACK_ROWS[name], D_MODEL)
        out.append(c)
    return out[0] if len(out) == 1 else jnp.concatenate(out, axis=1)


def _shard_grad(summed, name, shape):
    if name == "w_in":
        rows = jnp.concatenate([summed[f"w_in#{i}"] for i in range(len(W_IN_HALVES))], axis=0)[:W_IN_ROWS]
    else:
        rows = summed[name][SCATTER_OFF[name]:SCATTER_OFF[name] + PACK_ROWS[name], :]
    return rows.reshape(shape[1], shape[0]).T if name in TRANSPOSED else rows.reshape(shape)


WEIGHT_NAMES = ['ffn1_norm', 'ffn1_w_gate', 'ffn1_w_up', 'ffn1_w_down', 'mix_norm', 'w_in', 'forget_bias', 'q_norm',
                'k_norm', 'w_branch_fox', 'w_branch_sb', 'w_out', 'ffn2_norm', 'ffn2_w_gate', 'ffn2_w_up',
                'ffn2_w_down', 'ple_norm', 'w_ple_gate', 'w_ple_proj']
SMALL_NAMES = ('ffn1_norm', 'mix_norm', 'ffn2_norm', 'ple_norm', 'q_norm', 'k_norm', 'forget_bias')
Q_OFF, K_OFF, B_OFF, LOSS_OFF = 0, HEAD_DIM, 2 * HEAD_DIM, 2 * HEAD_DIM + N_HEADS


def _pack_small(vals, loss=None):
    tail = [vals['q_norm'].reshape(1, -1), vals['k_norm'].reshape(1, -1), vals['forget_bias'].reshape(1, -1)]
    used = LOSS_OFF
    if loss is not None:
        tail.append(loss.reshape(1, 1))
        used += 1
    tail.append(jnp.zeros((1, D_MODEL - used), F32))
    rows = [vals[n].reshape(1, D_MODEL) for n in SMALL_NAMES[:4]] + [jnp.concatenate(tail, axis=1)]
    rows.append(jnp.zeros((SMALL_ROWS - len(rows), D_MODEL), F32))
    return jnp.concatenate(rows, axis=0)


def _unpack_small(packed, name, shape):
    if name in SMALL_NAMES[:4]:
        return packed[SMALL_NAMES.index(name)].reshape(shape)
    off, n = {'q_norm': (Q_OFF, HEAD_DIM), 'k_norm': (K_OFF, HEAD_DIM), 'forget_bias': (B_OFF, N_HEADS)}[name]
    return packed[4, off:off + n].reshape(shape)


def _step(x, p, tgt, w):
    row = lambda a: a.reshape(1, -1).astype(F32)
    g_ffn1, g_mix, g_ffn2, g_ple = (row(w[n]) for n in SMALL_NAMES[:4])
    qg = jnp.tile(row(w['q_norm']), (1, N_HEADS))
    kg = jnp.tile(row(w['k_norm']), (1, N_HEADS))
    bias = jnp.pad(row(w['forget_bias']), ((0, 0), (0, LANES - N_HEADS)))
    half = D_FF // 2
    grads = {}

    blk = lambda n: GATHER_OFF[n] // FFN_SHARD
    ffn1 = tuple(blk(n) for n in ("ffn1_w_gate", "ffn1_w_up", "ffn1_w_down"))
    ffn2 = tuple(blk(n) for n in ("ffn2_w_gate", "ffn2_w_up", "ffn2_w_down"))
    got0 = _gather_two_level(_pack(w, 0, BF16), "gather_ffn1")
    x1, g1, u1, h1, got1 = _ffn_fwd(x, g_ffn1, got0, ffn1, "ffn1_fwd", ride=(_pack(w, 1, BF16), True))
    w_in = _w_in_t_padded(got1)
    wbf = _gathered(got1, "w_branch_fox", (D_MODEL, ATT_W))
    wbs = _gathered(got1, "w_branch_sb", (D_MODEL, ATT_W))
    wo = _gathered(got1, "w_out", (D_MODEL, D_MODEL))
    (hmix, fqr, fkr, fqn, fkn, fv, logf, f_col, f_row, sq, sk, sv, gf, gs, kmax) = _mix_fwd(
        x1, g_mix, w_in, bias, qg, kg, "mix_fwd")
    y_fox, lse, got2 = _fox_fwd(fqn, fkn, fv, f_col, f_row, kmax, "fox_fwd", ride=(_pack(w, 2, BF16), True))
    wpg = _gathered(got2, "w_ple_gate", (D_MODEL, D_MODEL))
    wpp = _gathered(got2, "w_ple_proj", (D_MODEL, PLE_DIM))
    y_sb, y_sb32 = _sb_fwd(sq, sk, sv, "sb_fwd")
    x2 = _merge_fwd(x1, y_fox, y_sb, gf, gs, wbf, wbs, wo, "merge_fwd")
    x3, g2, u2, h2, = _ffn_fwd(x2, g_ffn2, got2, ffn2, "ffn2_fwd")
    dx3, n_ple, ds_ple, dpp, dg_ple, loss = _ple_loss(x3, p, tgt, g_ple, wpg, wpp, "ple_loss")

    grads['w_ple_gate'] = _wgrad(n_ple, ds_ple, "dw_ple_gate", D_MODEL, D_MODEL)
    grads['w_ple_proj'] = _wgrad(dpp, p, "dw_ple_proj", D_MODEL, PLE_DIM)
    dg2, du2, act2, dx2, dg_ffn2 = _ffn_bwd_fused(x2, dx3, g_ffn2, g2, u2, got2, ffn2, "ffn2_bwd")
    grads['ffn2_w_gate'] = _wgrad(dg2, h2, "dw_ffn2_gate", half, D_MODEL)
    grads['ffn2_w_up'] = _wgrad(du2, h2, "dw_ffn2_up", half, D_MODEL)
    grads['ffn2_w_down'] = _wgrad(act2, dx3, "dw_ffn2_down", half, D_MODEL)
    dyf, dys, dgf, dgs, dbf, dbs, merged = _merge_bwd(dx2, y_fox, y_sb, gf, gs, wbf, wbs, wo, "merge_bwd")
    grads['w_branch_fox'] = _wgrad(dbf, y_fox, "dw_branch_fox", D_MODEL, ATT_W)
    grads['w_branch_sb'] = _wgrad(dbs, y_sb, "dw_branch_sb", D_MODEL, ATT_W)
    grads['w_out'] = _wgrad(merged, dx2, "dw_out", D_MODEL, D_MODEL)
    dfqn, dfkn, dfv, dft, part_rest = _fox_bwd(fqn, fkn, fv, dyf, y_fox, lse, f_col, f_row, kmax, "fox_bwd",
                                               ride=(_pack_chunks(grads, 5), False))
    dsq, dsk, dsv = _sb_bwd(sq, sk, sv, dys, y_sb32, "sb_bwd")
    s_len = x.shape[0]
    df_col = jnp.pad(dft[:, :2, :].reshape(N_HEADS, s_len).T, ((0, 0), (0, LANES - N_HEADS)))
    dproj, dx1, dg_mix, dqg, dkg, dbias = _mix_bwd(
        x1, dx2, g_mix, w_in, fqr, fkr, dfqn, dfkn, qg, kg, dfv, df_col, logf, dsq, dsk, dsv, dgf, dgs, "mix_bwd")
    grads['w_in'] = _wgrad(dproj, hmix, "dw_in", IN_PAD // 3, D_MODEL)
    dg1, du1, act1, dx0, dg_ffn1, part_in0 = _ffn_bwd_fused(x, dx1, g_ffn1, g1, u1, got0, ffn1, "ffn1_bwd",
                                                            ride=(_pack_chunks(grads, 3), False))
    grads['ffn1_w_gate'], part_in1 = _wgrad(dg1, h1, "dw_ffn1_gate", half, D_MODEL,
                                            ride=(_pack_chunks(grads, 4), False))
    grads['ffn1_w_up'], part_gate = _wgrad(du1, h1, "dw_ffn1_up", half, D_MODEL, ride=(_pack_chunks(grads, 0), False))
    grads['ffn1_w_down'], part_up = _wgrad(act1, dx1, "dw_ffn1_down", half, D_MODEL,
                                           ride=(_pack_chunks(grads, 1), False))
    part_down = _exchange(_pack_chunks(grads, 2), "scatter_ffn1_down", False)

    fold = lambda a: a.reshape(N_HEADS, HEAD_DIM).sum(axis=0).reshape(1, HEAD_DIM)
    small_g = {'ffn1_norm': dg_ffn1, 'mix_norm': dg_mix, 'ffn2_norm': dg_ffn2, 'ple_norm': dg_ple,
               'q_norm': fold(dqg), 'k_norm': fold(dkg), 'forget_bias': dbias[:, :N_HEADS]}
    return loss[0, 0], dx0, (part_gate, part_up, part_down, part_in0, part_in1, part_rest), small_g


def kernel(x, p, ffn1_norm, ffn1_w_gate, ffn1_w_up, ffn1_w_down, mix_norm, w_in, forget_bias, q_norm, k_norm, w_branch_fox, w_branch_sb, w_out, ffn2_norm, ffn2_w_gate, ffn2_w_up, ffn2_w_down, ple_norm, w_ple_gate, w_ple_proj, loss_target, m_ffn1_norm, m_ffn1_w_gate, m_ffn1_w_up, m_ffn1_w_down, m_mix_norm, m_w_in, m_forget_bias, m_q_norm, m_k_norm, m_w_branch_fox, m_w_branch_sb, m_w_out, m_ffn2_norm, m_ffn2_w_gate, m_ffn2_w_up, m_ffn2_w_down, m_ple_norm, m_w_ple_gate, m_w_ple_proj, v_ffn1_norm, v_ffn1_w_gate, v_ffn1_w_up, v_ffn1_w_down, v_mix_norm, v_w_in, v_forget_bias, v_q_norm, v_k_norm, v_w_branch_fox, v_w_branch_sb, v_w_out, v_ffn2_norm, v_ffn2_w_gate, v_ffn2_w_up, v_ffn2_w_down, v_ple_norm, v_w_ple_gate, v_w_ple_proj):
    args = dict(locals())
    w = {n: args[n][0] for n in WEIGHT_NAMES}
    m = {n: args["m_" + n][0] for n in WEIGHT_NAMES}
    v = {n: args["v_" + n][0] for n in WEIGHT_NAMES}
    loss, dx, parts, small_g = _step(x[0], p[0, 0], loss_target[0], w)

    summed = {}
    for grp, part in enumerate(parts):
        s = _sum_parts(part, f"sum_grads_{grp}", SUM_TILE_ROWS[grp])
        summed.update({n: s for n in SCATTER_GROUPS[grp]})
    big = {}
    for n in WEIGHT_NAMES:
        if n not in SMALL_NAMES:
            g = _shard_grad(summed, n, w[n].shape)
            big[n] = (g,) + tuple(_adamw_shard(g, w[n], m[n], v[n], "adamw_" + n))
    small_parts = _exchange(_pack_small(small_g, loss), "gather_small", True)
    sw, sm, sv = (_pack_small(t) for t in (w, m, v))
    small = _adamw(small_parts, sw, sm, sv, "adamw_small", SMALL_ROWS)

    outs = [small[0][4, LOSS_OFF], dx.reshape(x.shape)]
    for kind in range(4):
        for n in WEIGHT_NAMES:
            shape = args[n].shape
            outs.append(_unpack_small(small[kind], n, shape) if n in SMALL_NAMES else big[n][kind].reshape(shape))
    return tuple(outs)
```

```python
import jax
import jax.numpy as jnp
from jax import lax
from jax.experimental import pallas as pl
from jax.experimental.pallas import tpu as pltpu

F32 = jnp.float32
BF16 = jnp.bfloat16

D_MODEL = 1024
D_FF = 2816
N_HEADS = 8
HEAD_DIM = 64
ATT_W = N_HEADS * HEAD_DIM
PLE_DIM = 256
EPS = 1e-6
N_DEV = 8
MESH = pl.DeviceIdType.MESH

LANES = 128
V7X_SCOPED_VMEM_BYTES = 56 * 1024 * 1024

C_FQ, C_FK, C_FV, C_FL = 0, 512, 1024, 1536
C_SQ, C_SK, C_SV, C_GF, C_GS = 1792, 2304, 2816, 3328, 4352
IN_PAD = 5376
IN_REAL = 5128
FL_REAL_END = 1544

ADAM_LR = 0.001
ADAM_B1 = 0.9
ADAM_B2 = 0.999
ADAM_EPS = 1e-08
ADAM_WD = 0.01
ADAM_STEP = 10

PACK_ROWS = {"ffn1_w_gate": 352, "ffn1_w_up": 352, "ffn1_w_down": 352, "w_in": 656, "w_branch_fox": 64,
             "w_branch_sb": 64, "w_out": 128, "ffn2_w_gate": 352, "ffn2_w_up": 352, "ffn2_w_down": 352,
             "w_ple_gate": 128, "w_ple_proj": 32}
GATHER_GROUPS = (
    ("ffn1_w_gate", "ffn1_w_up", "ffn1_w_down"),
    ("w_in", "w_branch_fox", "w_branch_sb", "w_out"),
    ("ffn2_w_gate", "ffn2_w_up", "ffn2_w_down", "w_ple_gate", "w_ple_proj"),
)
W_IN_HALVES = ((0, 336), (336, 656))
PACK_ROWS.update({f"w_in#{i}": hi - lo for i, (lo, hi) in enumerate(W_IN_HALVES)})
SCATTER_GROUPS = (
    ("ffn1_w_gate",), ("ffn1_w_up",), ("ffn1_w_down",),
    ("w_in#0",), ("w_in#1",),
    ("ffn2_w_gate", "ffn2_w_up", "ffn2_w_down", "w_ple_gate", "w_ple_proj", "w_branch_fox", "w_branch_sb", "w_out"),
)
SUM_TILE_ROWS = (352, 352, 352, 336, 320, 368)


def _offsets(groups):
    off = {}
    for grp in groups:
        o = 0
        for n in grp:
            off[n] = o
            o += PACK_ROWS[n]
    return off


GATHER_OFF = _offsets(GATHER_GROUPS)
SCATTER_OFF = _offsets(SCATTER_GROUPS)
W_IN_ROWS = 641

SMALL_ROWS = 8


def _cparams(*sem):
    return pltpu.CompilerParams(dimension_semantics=sem, vmem_limit_bytes=V7X_SCOPED_VMEM_BYTES)


def _dot(a, b):
    return jnp.dot(a, b, preferred_element_type=F32)


def _dot_nt(a, b):
    return lax.dot_general(a, b, (((1,), (1,)), ((), ())), preferred_element_type=F32)


def _dot_tn(a, b):
    return lax.dot_general(a, b, (((0,), (0,)), ((), ())), preferred_element_type=F32)


def _split(x, parts):
    out = []
    r = x
    for _ in range(parts):
        p = r.astype(BF16)
        out.append(p)
        r = r - p.astype(F32)
    return out


def _dot_split(x, m, parts):
    acc = None
    for p in _split(x, parts):
        t = _dot(p, m)
        acc = t if acc is None else acc + t
    return acc


def _dot_split_left(m, x, parts):
    acc = None
    for p in _split(x, parts):
        t = _dot(m, p)
        acc = t if acc is None else acc + t
    return acc


def _rms_rinv(xf):
    return lax.rsqrt(jnp.mean(xf * xf, axis=-1, keepdims=True) + EPS)


def _sigmoid(x):
    return 1.0 / (1.0 + jnp.exp(-x))


def _softplus_neg_abs(z):
    return jnp.log(1.0 + jnp.exp(-jnp.abs(z)))


FFN_SHARD = D_FF // N_DEV
FFN_CHUNK = 4


def _ffn_w_spec(blk, index_map):
    return pl.BlockSpec((FFN_CHUNK, FFN_SHARD, D_MODEL), lambda *g: (index_map(*g), blk, 0))


def _ffn_w(ref):
    return ref[...].reshape(FFN_CHUNK * FFN_SHARD, D_MODEL)


def _ffn_fwd(x, gain, wbuf, blks, name, ride=None):
    s_len = x.shape[0]
    ts = min(512, s_len)
    fc = FFN_CHUNK * FFN_SHARD
    nt, nc = s_len // ts, D_FF // fc

    def body(x_ref, gain_ref, wg_ref, wu_ref, wd_ref, y_ref, g_ref, u_ref, h_ref, acc_scr):
        j = pl.program_id(1)

        @pl.when(j == 0)
        def _():
            xf = x_ref[...]
            h_ref[...] = ((xf * _rms_rinv(xf)) * gain_ref[...]).astype(BF16)
            acc_scr[...] = jnp.zeros_like(acc_scr)

        h = h_ref[...]
        g = _dot_nt(h, _ffn_w(wg_ref))
        u = _dot_nt(h, _ffn_w(wu_ref))
        g_ref[...] = g.astype(BF16)
        u_ref[...] = u.astype(BF16)
        a = (g * _sigmoid(g) * u).astype(BF16)
        acc_scr[...] += _dot(a, _ffn_w(wd_ref))

        @pl.when(j == nc - 1)
        def _():
            y_ref[...] = x_ref[...] + 0.5 * acc_scr[...]

    tok = pl.BlockSpec((ts, D_MODEL), lambda i, j: (i, 0))
    hid = pl.BlockSpec((ts, fc), lambda i, j: (i, j))
    return _ride_call(
        body, name, (nt, nc),
        [tok, pl.BlockSpec((1, D_MODEL), lambda i, j: (0, 0))] + [_ffn_w_spec(b, lambda i, j: j) for b in blks],
        [tok, hid, hid, tok],
        [jax.ShapeDtypeStruct((s_len, D_MODEL), F32), jax.ShapeDtypeStruct((s_len, D_FF), BF16),
         jax.ShapeDtypeStruct((s_len, D_FF), BF16), jax.ShapeDtypeStruct((s_len, D_MODEL), BF16)],
        [pltpu.VMEM((ts, D_MODEL), F32)], ("parallel", "arbitrary"), (x, gain, wbuf, wbuf, wbuf), ride)


def _ffn_bwd_fused(x, dy, gain, g, u, wbuf, blks, name, ride=None):
    s_len = x.shape[0]
    ts = min(512, s_len)
    fc = FFN_CHUNK * FFN_SHARD
    nt = s_len // ts
    assert D_FF == 2 * fc

    def hidden(dy_ref, g_ref, u_ref, wg_ref, wu_ref, wd_ref, dg_ref, du_ref, act_ref):
        da = 0.5 * _dot_nt(dy_ref[...].astype(BF16), _ffn_w(wd_ref))
        gf = g_ref[...].astype(F32)
        uf = u_ref[...].astype(F32)
        sg = _sigmoid(gf)
        silu = gf * sg
        dg = (da * uf * (sg * (1.0 + gf * (1.0 - sg)))).astype(BF16)
        du = (da * silu).astype(BF16)
        dg_ref[...] = dg
        du_ref[...] = du
        act_ref[...] = (0.5 * silu * uf).astype(BF16)
        return _dot(dg, _ffn_w(wg_ref)) + _dot(du, _ffn_w(wu_ref))

    def first(dy_ref, g_ref, u_ref, wg_ref, wu_ref, wd_ref, dg_ref, du_ref, act_ref, dh_ref):
        dh_ref[...] = hidden(dy_ref, g_ref, u_ref, wg_ref, wu_ref, wd_ref, dg_ref, du_ref, act_ref)

    def second(x_ref, dy_ref, gain_ref, g_ref, u_ref, wg_ref, wu_ref, wd_ref, dh0_ref, dg_half, du_half, act_half,
               dg_ref, du_ref, act_ref, dx_ref, dgain_ref):
        i = pl.program_id(0)
        dh = dh0_ref[...] + hidden(dy_ref, g_ref, u_ref, wg_ref, wu_ref, wd_ref, dg_ref, du_ref, act_ref)
        xf = x_ref[...]
        r = _rms_rinv(xf)
        xhat = xf * r
        dgp = jnp.sum(dh * xhat, axis=0, keepdims=True)

        @pl.when(i == 0)
        def _():
            dgain_ref[...] = dgp

        @pl.when(i > 0)
        def _():
            dgain_ref[...] += dgp

        dn = dh * gain_ref[...]
        dx_ref[...] = dy_ref[...] + r * (dn - xhat * jnp.mean(dn * xhat, axis=-1, keepdims=True))

    tok = pl.BlockSpec((ts, D_MODEL), lambda i: (i, 0))
    row = pl.BlockSpec((1, D_MODEL), lambda i: (0, 0))
    hid = lambda c: pl.BlockSpec((ts, fc), lambda i: (i, c))
    wts = lambda c: [pl.BlockSpec((FFN_CHUNK, FFN_SHARD, D_MODEL), lambda i, b=b: (c, b, 0),
                                  pipeline_mode=pl.Buffered(1)) for b in blks]
    hidden_shapes = [jax.ShapeDtypeStruct((s_len, D_FF), BF16)] * 3
    dg, du, act, dh0, *rode = _ride_call(
        first, name + "_a", (nt,), [tok, hid(0), hid(0)] + wts(0), [hid(0)] * 3 + [tok],
        hidden_shapes + [jax.ShapeDtypeStruct((s_len, D_MODEL), F32)], [], ("parallel",),
        (dy, g, u, wbuf, wbuf, wbuf), ride)
    filled = pl.BlockSpec(memory_space=pl.ANY)
    dg, du, act, dx, dgain = pl.pallas_call(
        second, name=name + "_b", grid=(nt,),
        in_specs=[tok, tok, row, hid(1), hid(1)] + wts(1) + [tok, filled, filled, filled],
        out_specs=[hid(1)] * 3 + [tok, row],
        out_shape=hidden_shapes + [jax.ShapeDtypeStruct((s_len, D_MODEL), F32), jax.ShapeDtypeStruct((1, D_MODEL), F32)],
        input_output_aliases={9: 0, 10: 1, 11: 2},
        compiler_params=_cparams("arbitrary"),
    )(x, dy, gain, g, u, wbuf, wbuf, wbuf, dh0, dg, du, act)
    return (dg, du, act, dx, dgain, *rode)


def _wgrad(a, b, name, tk, tn, ride=None):
    s_len, k_dim = a.shape
    n_dim = b.shape[1]
    ts = min(1024, s_len)
    ns = s_len // ts

    def body(a_ref, b_ref, o_ref, acc):
        s = pl.program_id(2)
        p = _dot_tn(a_ref[...].astype(BF16), b_ref[...].astype(BF16))

        @pl.when(s == 0)
        def _():
            acc[...] = p

        @pl.when(s > 0)
        def _():
            acc[...] += p

        @pl.when(s == ns - 1)
        def _():
            o_ref[...] = acc[...].astype(BF16)

    out = _ride_call(
        body, name, (k_dim // tk, n_dim // tn, ns),
        [pl.BlockSpec((ts, tk), lambda k, n, s: (s, k)), pl.BlockSpec((ts, tn), lambda k, n, s: (s, n))],
        [pl.BlockSpec((tk, tn), lambda k, n, s: (k, n))], [jax.ShapeDtypeStruct((k_dim, n_dim), BF16)],
        [pltpu.VMEM((tk, tn), F32)], ("parallel", "parallel", "arbitrary"), (a, b), ride)
    return out[0] if ride is None else tuple(out)


HEAD_SUM_PARTS = 2


def _head_group_matrix():
    r = lax.broadcasted_iota(jnp.int32, (ATT_W, ATT_W), 0) // HEAD_DIM
    c = lax.broadcasted_iota(jnp.int32, (ATT_W, ATT_W), 1) // HEAD_DIM
    return (r == c).astype(BF16)


def _mix_fwd(x, gain, w_in, bias, qg, kg, name):
    s_len = x.shape[0]
    ts = min(512, s_len)
    nt = s_len // ts
    gmat = _head_group_matrix()

    def body(x_ref, gain_ref, w_ref, bias_ref, qg_ref, kg_ref, gm_ref,
             h_ref, fqr_ref, fkr_ref, fqn_ref, fkn_ref, fv_ref, logf_ref, f_ref, ft_ref,
             sq_ref, sk_ref, sv_ref, gf_ref, gs_ref, kmax_ref, carry):
        i = pl.program_id(0)
        xf = x_ref[...]
        h = ((xf * _rms_rinv(xf)) * gain_ref[...]).astype(BF16)
        h_ref[...] = h
        gm = gm_ref[...]

        def proj(lo, n):
            return _dot_nt(h, w_ref[lo:lo + n, :])

        def headnorm(raw, g):
            ms = _dot_split(raw * raw, gm, HEAD_SUM_PARTS) * (1.0 / HEAD_DIM)
            return ((raw * lax.rsqrt(ms + EPS)) * g).astype(BF16)

        fq = proj(C_FQ, ATT_W)
        fqr_ref[...] = fq
        fqn_ref[...] = headnorm(fq, qg_ref[...])
        fk = proj(C_FK, ATT_W)
        fkr_ref[...] = fk
        fkn = headnorm(fk, kg_ref[...])
        fkn_ref[...] = fkn
        kn2 = jnp.max(_dot_split(jnp.square(fkn.astype(F32)), gm, HEAD_SUM_PARTS), axis=0, keepdims=True)

        @pl.when(i == 0)
        def _():
            kmax_ref[...] = kn2

        @pl.when(i > 0)
        def _():
            kmax_ref[...] = jnp.maximum(kmax_ref[...], kn2)
        fv_ref[...] = proj(C_FV, ATT_W).astype(BF16)
        sq_ref[...] = proj(C_SQ, ATT_W).astype(BF16)
        sk_ref[...] = proj(C_SK, ATT_W).astype(BF16)
        sv_ref[...] = proj(C_SV, ATT_W).astype(BF16)
        gf_ref[...] = proj(C_GF, D_MODEL)
        gs_ref[...] = proj(C_GS, D_MODEL)

        fl = proj(C_FL, LANES) + bias_ref[...]
        lane = lax.broadcasted_iota(jnp.int32, fl.shape, 1)
        logf = jnp.where(lane < N_HEADS, jnp.minimum(fl, 0.0) - _softplus_neg_abs(fl), 0.0)
        logf_ref[...] = logf

        @pl.when(i == 0)
        def _():
            carry[...] = jnp.zeros_like(carry)

        r = lax.broadcasted_iota(jnp.int32, (ts, ts), 0)
        c = lax.broadcasted_iota(jnp.int32, (ts, ts), 1)
        tri = (r >= c).astype(BF16)
        f_tile = _dot_split_left(tri, logf, 3) + carry[...]
        f_ref[...] = f_tile
        ft_ref[...] = f_tile.T[:N_HEADS, :]
        carry[...] = f_tile[ts - 1:ts, :]

    tok = lambda w: pl.BlockSpec((ts, w), lambda i: (i, 0))
    full = lambda a: pl.BlockSpec(a.shape, lambda i: (0, 0), pipeline_mode=pl.Buffered(1))
    f32o = lambda w: jax.ShapeDtypeStruct((s_len, w), F32)
    b16o = lambda w: jax.ShapeDtypeStruct((s_len, w), BF16)
    return _ride_call(
        body, name, (nt,),
        [tok(D_MODEL), full(gain), full(w_in), full(bias), full(qg), full(kg), full(gmat)],
        [
            tok(D_MODEL), tok(ATT_W), tok(ATT_W), tok(ATT_W), tok(ATT_W), tok(ATT_W), tok(LANES), tok(LANES),
            pl.BlockSpec((N_HEADS, ts), lambda i: (0, i)),
            tok(ATT_W), tok(ATT_W), tok(ATT_W), tok(D_MODEL), tok(D_MODEL),
            pl.BlockSpec((1, ATT_W), lambda i: (0, 0)),
        ],
        [
            b16o(D_MODEL), f32o(ATT_W), f32o(ATT_W), b16o(ATT_W), b16o(ATT_W), b16o(ATT_W), f32o(LANES), f32o(LANES),
            jax.ShapeDtypeStruct((N_HEADS, s_len), F32),
            b16o(ATT_W), b16o(ATT_W), b16o(ATT_W), f32o(D_MODEL), f32o(D_MODEL),
            jax.ShapeDtypeStruct((1, ATT_W), F32),
        ],
        [pltpu.VMEM((1, LANES), F32)], ("arbitrary",), (x, gain, w_in, bias, qg, kg, gmat), None)


ATT_T = 256
ATT_ROWS = 2
EXP_ZERO = 88.0


def _att_tiling(s_len):
    t = min(ATT_T, s_len)
    nr = min(ATT_ROWS, s_len // t)
    return t, nr, s_len // (t * nr)


def _pair_specs(s_len, tq):
    qblk = pl.BlockSpec((tq, LANES), lambda hp, i: (i, hp))
    kvfull = pl.BlockSpec((s_len, LANES), lambda hp, i: (0, hp))
    return qblk, kvfull


def _walk_tiles(i, nr, load, sub, flush, init, more=None, trips=None):
    base = i * nr
    carries = list(init)
    for kk in range(nr - 1, -1, -1):
        rs = list(range(kk, nr))
        new, side = sub(rs, load(base + kk), [carries[r] for r in rs], [r == kk for r in rs])
        carries[kk:] = new
        flush(base + kk, side)

    done = jnp.int32(0)
    for last in range(nr - 1, -1, -1):
        rs = list(range(last + 1))

        def visit(n, cs, rs=rs):
            kb = base - 1 - n
            cs, side = sub(rs, load(kb), list(cs), [False] * len(rs))
            flush(kb, side)
            return tuple(cs)

        if trips is not None:
            todo = jnp.maximum(trips(carries, base, last) - done, 0)
            new = lax.fori_loop(0, todo, lambda it, cs, start=done, visit=visit: visit(start + it, cs),
                                tuple(carries[:last + 1]))
            done = done + todo
        else:
            def step(state, visit=visit, last=last):
                n, _, cs = state
                cs = visit(n, cs)
                return n + 1, more(cs[last]), cs

            done, _, new = lax.while_loop(lambda state: jnp.logical_and(state[0] < base, state[1] > 0), step,
                                          (done, more(carries[last]), tuple(carries[:last + 1])))
        carries[:last + 1] = list(new)
    return carries


def _stack(parts):
    return parts[0] if len(parts) == 1 else jnp.concatenate(parts, axis=0)


def _stacked_halves(x, lo):
    z = jnp.zeros_like(x)
    return jnp.concatenate([jnp.where(lo, x, z), jnp.where(lo, z, x)], axis=0)


def _fox_qk_bound(qst_r, km_ref, t):
    km = km_ref[...]
    out = []
    for j in (0, 1):
        qf = qst_r[j * t:(j + 1) * t, :].astype(F32)
        qn = jnp.sqrt(jnp.sum(qf * qf, axis=1, keepdims=True))
        out.append(qn * jnp.sqrt(km[:, j * HEAD_DIM:j * HEAD_DIM + 1]) * 1.001 + 1.0)
    return out


def _fox_trips(hp, flast_ref, qkb, fq, level):
    def trips(carries, base, r):
        gap = [jnp.max(qkb[r][j] + fq[r][j] - level(carries, r, j)) for j in (0, 1)]

        def needed(n):
            kb = jnp.maximum(base - 1 - n, 0)
            return jnp.logical_or(gap[0] - flast_ref[2 * hp, kb] > -EXP_ZERO,
                                  gap[1] - flast_ref[2 * hp + 1, kb] > -EXP_ZERO)

        return lax.while_loop(lambda n: jnp.logical_and(n < base, needed(n)), lambda n: n + 1, jnp.int32(0))
    return trips


def _fox_fwd(q, k, v, f_col, f_row, kmax, name, ride=None):
    s_len = q.shape[0]
    t, nr, nq = _att_tiling(s_len)

    def body(q_ref, k_ref, v_ref, f_ref, ft_ref, km_ref, fl_ref, y_ref, lse_ref):
        hp = pl.program_id(0)
        i = pl.program_id(1)
        lane = lax.broadcasted_iota(jnp.int32, (t, LANES), 1)
        lo = lane < HEAD_DIM
        causal = lax.broadcasted_iota(jnp.int32, (t, t), 0) >= lax.broadcasted_iota(jnp.int32, (t, t), 1)
        rows = [pl.ds(r * t, t) for r in range(nr)]
        qst = [_stacked_halves(q_ref[rw, :] * jnp.asarray(HEAD_DIM ** -0.5, BF16), lo) for rw in rows]
        q_all = _stack(qst)
        fq = [[jnp.sum(jnp.where(lane == 2 * hp + j, f_ref[rw, :], 0.0), axis=1, keepdims=True) for j in (0, 1)]
              for rw in rows]

        def load(kb):
            k0 = pl.multiple_of(kb * t, t)
            frow = [ft_ref[pl.ds(2 * hp + j, 1), pl.ds(k0, t)] for j in (0, 1)]
            return k_ref[pl.ds(k0, t), :], v_ref[pl.ds(k0, t), :], frow

        def sub(rs, tiles, carries, masked):
            kblk, vblk, frow = tiles
            z = _dot_nt(q_all if len(rs) == nr else _stack([qst[r] for r in rs]), kblk)
            ps, stats = [], []
            for n, (r, j) in enumerate((r, j) for r in range(len(rs)) for j in (0, 1)):
                m, l, _ = carries[r]
                s = z[n * t:(n + 1) * t, :] + (fq[rs[r]][j] - frow[j])
                if masked[r]:
                    s = jnp.where(causal, s, -1e30)
                mj = jnp.maximum(m[j], jnp.max(s, axis=1, keepdims=True))
                aj = jnp.exp(m[j] - mj)
                p = jnp.exp(s - mj)
                stats.append((mj, aj, aj * l[j] + jnp.sum(p, axis=1, keepdims=True)))
                ps.append(p.astype(BF16))
            pv = _dot(_stack(ps), vblk)
            out = []
            for r in range(len(rs)):
                (m0, a0, l0), (m1, a1, l1) = stats[2 * r], stats[2 * r + 1]
                acc = carries[r][2]
                acc = (acc[0] * a0 + pv[2 * r * t:(2 * r + 1) * t, :], acc[1] * a1 + pv[(2 * r + 1) * t:(2 * r + 2) * t, :])
                out.append(((m0, m1), (l0, l1), acc))
            return out, None

        neg = jnp.full((t, 1), -1e30, F32)
        zero = jnp.zeros((t, 1), F32)
        zacc = jnp.zeros((t, LANES), F32)
        init = [((neg, neg), (zero, zero), (zacc, zacc))] * nr
        qkb = [_fox_qk_bound(qs, km_ref, t) for qs in qst]
        trips = _fox_trips(hp, fl_ref, qkb, fq, lambda carries, r, j: carries[r][0][j])
        out = _walk_tiles(i, nr, load, sub, lambda kb, side: None, init, trips=trips)
        for rw, (m, l, acc) in zip(rows, out):
            y_ref[rw, :] = jnp.where(lo, acc[0] / l[0], acc[1] / l[1]).astype(BF16)
            lse_ref[0, rw, :] = jnp.where(lo, m[0] + jnp.log(l[0]), m[1] + jnp.log(l[1]))

    qblk, kvfull = _pair_specs(s_len, t * nr)
    return _ride_call(
        body, name, (N_HEADS // 2, nq),
        [qblk, kvfull, kvfull,
         pl.BlockSpec((t * nr, LANES), lambda hp, i: (i, 0)),
         pl.BlockSpec((N_HEADS, s_len), lambda hp, i: (0, 0)),
         pl.BlockSpec((1, LANES), lambda hp, i: (0, hp)),
         pl.BlockSpec(memory_space=pltpu.SMEM)],
        [qblk, pl.BlockSpec((1, t * nr, LANES), lambda hp, i: (hp, i, 0))],
        [jax.ShapeDtypeStruct((s_len, ATT_W), BF16), jax.ShapeDtypeStruct((N_HEADS // 2, s_len, LANES), F32)],
        [], ("parallel", "parallel"), (q, k, v, f_col, f_row, kmax, f_row[:, t - 1::t]), ride)


def _fox_bwd(q, k, v, dy, y, lse, f_col, f_row, kmax, name, ride=None):
    s_len = q.shape[0]
    t, nr, nq = _att_tiling(s_len)

    def body(q_ref, k_ref, v_ref, dy_ref, y_ref, lse_ref, f_ref, ft_ref, km_ref, fl_ref,
             dq_ref, dk_ref, dv_ref, dft_ref):
        hp = pl.program_id(0)
        i = pl.program_id(1)

        @pl.when(i == 0)
        def _():
            dk_ref[...] = jnp.zeros_like(dk_ref)
            dv_ref[...] = jnp.zeros_like(dv_ref)
            dft_ref[...] = jnp.zeros_like(dft_ref)

        lane = lax.broadcasted_iota(jnp.int32, (t, LANES), 1)
        lo = lane < HEAD_DIM
        causal = lax.broadcasted_iota(jnp.int32, (t, t), 0) >= lax.broadcasted_iota(jnp.int32, (t, t), 1)
        rows = [pl.ds(r * t, t) for r in range(nr)]
        qst, dyst, delta, lse, fq = [], [], [], [], []
        for rw in rows:
            qst.append(_stacked_halves(q_ref[rw, :] * jnp.asarray(HEAD_DIM ** -0.5, BF16), lo))
            dyb = dy_ref[rw, :]
            dyst.append(_stacked_halves(dyb, lo))
            prod = dyb.astype(F32) * y_ref[rw, :].astype(F32)
            delta.append([jnp.sum(jnp.where(lo, prod, 0.0), axis=1, keepdims=True),
                          jnp.sum(jnp.where(lo, 0.0, prod), axis=1, keepdims=True)])
            lse_b = lse_ref[0, rw, :]
            lse.append([lse_b[:, 0:1], lse_b[:, HEAD_DIM:HEAD_DIM + 1]])
            fq.append([jnp.sum(jnp.where(lane == 2 * hp + j, f_ref[rw, :], 0.0), axis=1, keepdims=True)
                       for j in (0, 1)])

        q_all, dy_all = _stack(qst), _stack(dyst)

        def load(kb):
            k0 = pl.multiple_of(kb * t, t)
            frow = [ft_ref[pl.ds(2 * hp + j, 1), pl.ds(k0, t)] for j in (0, 1)]
            return k_ref[pl.ds(k0, t), :], v_ref[pl.ds(k0, t), :], frow

        def sub(rs, tiles, carries, masked):
            kblk, vblk, frow = tiles
            qs, dys = (q_all, dy_all) if len(rs) == nr else (_stack([qst[r] for r in rs]), _stack([dyst[r] for r in rs]))
            z = _dot_nt(qs, kblk)
            dp = _dot_nt(dys, vblk)
            pb, dsb, rsum, col = [], [], [], [None, None]
            for n, (r, j) in enumerate((r, j) for r in range(len(rs)) for j in (0, 1)):
                sl = slice(n * t, (n + 1) * t)
                s = z[sl, :] + (fq[rs[r]][j] - frow[j])
                p = jnp.exp(s - lse[rs[r]][j])
                if masked[r]:
                    p = jnp.where(causal, p, 0.0)
                ds = p * (dp[sl, :] - delta[rs[r]][j])
                c = jnp.sum(ds, axis=0, keepdims=True)
                col[j] = c if col[j] is None else col[j] + c
                rsum.append(carries[r][1][j] + jnp.sum(ds, axis=1, keepdims=True))
                pb.append(p.astype(BF16))
                dsb.append(ds.astype(BF16))
            p_all, ds_all = _stack(pb), _stack(dsb)
            dqs = _dot(ds_all, kblk)
            out = []
            for r in range(len(rs)):
                dq = carries[r][0]
                dq = (dq[0] + dqs[2 * r * t:(2 * r + 1) * t, :], dq[1] + dqs[(2 * r + 1) * t:(2 * r + 2) * t, :])
                out.append((dq, (rsum[2 * r], rsum[2 * r + 1])))
            return out, (_dot_tn(ds_all, qs), _dot_tn(p_all, dys), col)

        def flush(kb, side):
            k0 = pl.multiple_of(kb * t, t)
            dk_ref[pl.ds(k0, t), :] += side[0]
            dv_ref[pl.ds(k0, t), :] += side[1]
            for j in (0, 1):
                dft_ref[0, pl.ds(j, 1), pl.ds(k0, t)] -= side[2][j]

        zero = jnp.zeros((t, 1), F32)
        zacc = jnp.zeros((t, LANES), F32)
        qkb = [_fox_qk_bound(qs, km_ref, t) for qs in qst]
        trips = _fox_trips(hp, fl_ref, qkb, fq, lambda carries, r, j: lse[r][j])
        out = _walk_tiles(i, nr, load, sub, flush, [((zacc, zacc), (zero, zero))] * nr, trips=trips)
        for r, (rw, (dq, rs)) in enumerate(zip(rows, out)):
            dq_ref[rw, :] = jnp.where(lo, dq[0], dq[1]) * (HEAD_DIM ** -0.5)
            rs_t = jnp.where(lo, rs[0], rs[1]).T
            q0 = pl.multiple_of((i * nr + r) * t, t)
            for j in (0, 1):
                dft_ref[0, pl.ds(j, 1), pl.ds(q0, t)] += rs_t[j * HEAD_DIM:j * HEAD_DIM + 1, :]

    qblk, kvfull = _pair_specs(s_len, t * nr)
    return _ride_call(
        body, name, (N_HEADS // 2, nq),
        [qblk, kvfull, kvfull, qblk, qblk,
         pl.BlockSpec((1, t * nr, LANES), lambda hp, i: (hp, i, 0)),
         pl.BlockSpec((t * nr, LANES), lambda hp, i: (i, 0)),
         pl.BlockSpec((N_HEADS, s_len), lambda hp, i: (0, 0)),
         pl.BlockSpec((1, LANES), lambda hp, i: (0, hp)),
         pl.BlockSpec(memory_space=pltpu.SMEM)],
        [qblk, kvfull, kvfull, pl.BlockSpec((1, 8, s_len), lambda hp, i: (hp, 0, 0))],
        [jax.ShapeDtypeStruct((s_len, ATT_W), F32)] * 3 + [jax.ShapeDtypeStruct((N_HEADS // 2, 8, s_len), F32)],
        [], ("arbitrary", "arbitrary"), (q, k, v, dy, y, lse, f_col, f_row, kmax, f_row[:, t - 1::t]), ride)


def _sb_more(carry):
    return (jnp.max(jnp.maximum(carry[0][0], carry[0][1])) > -EXP_ZERO).astype(jnp.int32)


def _stacked_split_dot(slabs, m, parts):
    split = [_split(x, parts) for x in slabs]
    acc = None
    for p in range(parts):
        d = _dot(_stack([s[p] for s in split]), m)
        acc = d if acc is None else acc + d
    return acc


def _sb_weights(z, c, strict, upper, t):
    logs = []
    for n in range(z.shape[0] // t):
        zn = z[n * t:(n + 1) * t, :]
        sp = _softplus_neg_abs(zn)
        l1m = jnp.minimum(-zn, 0.0) - sp
        if strict[n] is not None:
            l1m = jnp.where(strict[n], l1m, 0.0)
        logs.append((jnp.minimum(zn, 0.0) - sp, l1m))
    suf = _stacked_split_dot([l1m for _, l1m in logs], upper, 2)
    out = []
    for n, (logb, l1m) in enumerate(logs):
        after = c[n] + suf[n * t:(n + 1) * t, :]
        a = jnp.exp(logb + after)
        if strict[n] is not None:
            a = jnp.where(strict[n], a, 0.0)
        out.append((logb, a, after[:, 0:1] + l1m[:, 0:1]))
    return out


def _sb_fwd(q, k, v, name):
    s_len = q.shape[0]
    t, nr, nq = _att_tiling(s_len)

    def body(q_ref, k_ref, v_ref, y_ref, yf_ref):
        i = pl.program_id(1)
        lane = lax.broadcasted_iota(jnp.int32, (t, LANES), 1)
        lo = lane < HEAD_DIM
        ri = lax.broadcasted_iota(jnp.int32, (t, t), 0)
        ci = lax.broadcasted_iota(jnp.int32, (t, t), 1)
        strict = ci < ri
        upper = (ri > ci).astype(BF16)
        rows = [pl.ds(r * t, t) for r in range(nr)]
        qst = [_stacked_halves(q_ref[rw, :] * jnp.asarray(HEAD_DIM ** -0.5, BF16), lo) for rw in rows]
        q_all = _stack(qst)

        def load(kb):
            k0 = pl.multiple_of(kb * t, t)
            return k_ref[pl.ds(k0, t), :], v_ref[pl.ds(k0, t), :]

        def sub(rs, tiles, carries, masked):
            kblk, vblk = tiles
            z = _dot_nt(q_all if len(rs) == nr else _stack([qst[r] for r in rs]), kblk)
            c = [carries[r][0][j] for r in range(len(rs)) for j in (0, 1)]
            w = _sb_weights(z, c, [strict if masked[r] else None for r in range(len(rs)) for j in (0, 1)], upper, t)
            pv = _dot(_stack([a.astype(BF16) for _, a, _ in w]), vblk)
            out = []
            for r in range(len(rs)):
                acc = carries[r][1]
                acc = (acc[0] + pv[2 * r * t:(2 * r + 1) * t, :], acc[1] + pv[(2 * r + 1) * t:(2 * r + 2) * t, :])
                out.append(((w[2 * r][2], w[2 * r + 1][2]), acc))
            return out, None

        zero = jnp.zeros((t, 1), F32)
        zacc = jnp.zeros((t, LANES), F32)
        out = _walk_tiles(i, nr, load, sub, lambda kb, side: None, [((zero, zero), (zacc, zacc))] * nr, more=_sb_more)
        for rw, (_, acc) in zip(rows, out):
            y = jnp.where(lo, acc[0], acc[1])
            y_ref[rw, :] = y.astype(BF16)
            yf_ref[rw, :] = y

    qblk, kvfull = _pair_specs(s_len, t * nr)
    return pl.pallas_call(
        body, name=name, grid=(N_HEADS // 2, nq),
        in_specs=[qblk, kvfull, kvfull],
        out_specs=[qblk, qblk],
        out_shape=[jax.ShapeDtypeStruct((s_len, ATT_W), BF16), jax.ShapeDtypeStruct((s_len, ATT_W), F32)],
        compiler_params=_cparams("parallel", "parallel"),
    )(q, k, v)


def _sb_bwd(q, k, v, dy, yf, name):
    s_len = q.shape[0]
    t, nr, nq = _att_tiling(s_len)

    def body(q_ref, k_ref, v_ref, dy_ref, yf_ref, dq_ref, dk_ref, dv_ref):
        i = pl.program_id(1)

        @pl.when(i == 0)
        def _():
            dk_ref[...] = jnp.zeros_like(dk_ref)
            dv_ref[...] = jnp.zeros_like(dv_ref)

        lane = lax.broadcasted_iota(jnp.int32, (t, LANES), 1)
        lo = lane < HEAD_DIM
        ri = lax.broadcasted_iota(jnp.int32, (t, t), 0)
        ci = lax.broadcasted_iota(jnp.int32, (t, t), 1)
        strict = ci < ri
        upper = (ri > ci).astype(BF16)
        upper_incl = (ri >= ci).astype(BF16)
        rows = [pl.ds(r * t, t) for r in range(nr)]
        qst, dyst, delta = [], [], []
        for rw in rows:
            qst.append(_stacked_halves(q_ref[rw, :] * jnp.asarray(HEAD_DIM ** -0.5, BF16), lo))
            dyb = dy_ref[rw, :]
            dyst.append(_stacked_halves(dyb, lo))
            prod = dyb.astype(F32) * yf_ref[rw, :]
            delta.append([jnp.sum(jnp.where(lo, prod, 0.0), axis=1, keepdims=True),
                          jnp.sum(jnp.where(lo, 0.0, prod), axis=1, keepdims=True)])
        q_all, dy_all = _stack(qst), _stack(dyst)

        def load(kb):
            k0 = pl.multiple_of(kb * t, t)
            return k_ref[pl.ds(k0, t), :], v_ref[pl.ds(k0, t), :]

        def sub(rs, tiles, carries, masked):
            kblk, vblk = tiles
            qs, dys = (q_all, dy_all) if len(rs) == nr else (_stack([qst[r] for r in rs]), _stack([dyst[r] for r in rs]))
            slabs = [(r, j) for r in range(len(rs)) for j in (0, 1)]
            z = _dot_nt(qs, kblk)
            w = _sb_weights(z, [carries[r][0][j] for r, j in slabs], [strict if masked[r] else None for r, j in slabs],
                            upper, t)
            da = _dot_nt(dys, vblk)
            ab = [a.astype(BF16) for _, a, _ in w]
            dl = [ab[n].astype(F32) * da[n * t:(n + 1) * t, :] for n in range(len(slabs))]
            tail = _stacked_split_dot(dl, upper_incl, 2)
            dzb, e_new = [], []
            for n, (r, j) in enumerate(slabs):
                tl = tail[n * t:(n + 1) * t, :]
                e = carries[r][1][j]
                dl1m = (delta[rs[r]][j] - e) - tl
                e_new.append(e + tl[:, 0:1])
                dz = dl[n] - jnp.exp(w[n][0]) * (dl[n] + dl1m)
                if masked[r]:
                    dz = jnp.where(strict, dz, 0.0)
                dzb.append(dz.astype(BF16))
            a_all, dz_all = _stack(ab), _stack(dzb)
            dqs = _dot(dz_all, kblk)
            out = []
            for r in range(len(rs)):
                dq = carries[r][2]
                dq = (dq[0] + dqs[2 * r * t:(2 * r + 1) * t, :], dq[1] + dqs[(2 * r + 1) * t:(2 * r + 2) * t, :])
                out.append(((w[2 * r][2], w[2 * r + 1][2]), (e_new[2 * r], e_new[2 * r + 1]), dq))
            return out, (_dot_tn(dz_all, qs), _dot_tn(a_all, dys))

        def flush(kb, side):
            k0 = pl.multiple_of(kb * t, t)
            dk_ref[pl.ds(k0, t), :] += side[0]
            dv_ref[pl.ds(k0, t), :] += side[1]

        zero = jnp.zeros((t, 1), F32)
        zacc = jnp.zeros((t, LANES), F32)
        out = _walk_tiles(i, nr, load, sub, flush, [((zero, zero), (zero, zero), (zacc, zacc))] * nr, more=_sb_more)
        for rw, (_, _, dq) in zip(rows, out):
            dq_ref[rw, :] = jnp.where(lo, dq[0], dq[1]) * (HEAD_DIM ** -0.5)

    qblk, kvfull = _pair_specs(s_len, t * nr)
    return pl.pallas_call(
        body, name=name, grid=(N_HEADS // 2, nq),
        in_specs=[qblk, kvfull, kvfull, qblk, qblk],
        out_specs=[qblk, kvfull, kvfull],
        out_shape=[jax.ShapeDtypeStruct((s_len, ATT_W), F32)] * 3,
        compiler_params=_cparams("arbitrary", "arbitrary"),
    )(q, k, v, dy, yf)


def _merge_fwd(x, yf, ys, gf, gs, wbf, wbs, wo, name):
    s_len = x.shape[0]
    ts = min(512, s_len)

    def body(x_ref, yf_ref, ys_ref, gf_ref, gs_ref, wbf_ref, wbs_ref, wo_ref, o_ref):
        merged = (_sigmoid(gf_ref[...]) * _dot_nt(yf_ref[...], wbf_ref[...])
                  + _sigmoid(gs_ref[...]) * _dot_nt(ys_ref[...], wbs_ref[...]))
        o_ref[...] = x_ref[...] + _dot(merged.astype(BF16), wo_ref[...])

    tok = lambda w: pl.BlockSpec((ts, w), lambda i: (i, 0))
    full = lambda a: pl.BlockSpec(a.shape, lambda i: (0, 0))
    return pl.pallas_call(
        body, name=name, grid=(s_len // ts,),
        in_specs=[tok(D_MODEL), tok(ATT_W), tok(ATT_W), tok(D_MODEL), tok(D_MODEL), full(wbf), full(wbs), full(wo)],
        out_specs=tok(D_MODEL),
        out_shape=jax.ShapeDtypeStruct((s_len, D_MODEL), F32),
        compiler_params=_cparams("parallel"),
    )(x, yf, ys, gf, gs, wbf, wbs, wo)


def _merge_bwd(dx, yf, ys, gf, gs, wbf, wbs, wo, name):
    s_len = dx.shape[0]
    ts = min(512, s_len)

    def body(dx_ref, yf_ref, ys_ref, gf_ref, gs_ref, wbf_ref, wbs_ref, wo_ref,
             dyf_ref, dys_ref, dgf_ref, dgs_ref, dbf_ref, dbs_ref, mg_ref):
        bf = _dot_nt(yf_ref[...], wbf_ref[...])
        bs = _dot_nt(ys_ref[...], wbs_ref[...])
        sf = _sigmoid(gf_ref[...])
        ss = _sigmoid(gs_ref[...])
        mg_ref[...] = (sf * bf + ss * bs).astype(BF16)
        dm = _dot_nt(dx_ref[...].astype(BF16), wo_ref[...])
        dbf = (dm * sf).astype(BF16)
        dbs = (dm * ss).astype(BF16)
        dbf_ref[...] = dbf
        dbs_ref[...] = dbs
        dgf_ref[...] = (dm * bf * (sf * (1.0 - sf))).astype(BF16)
        dgs_ref[...] = (dm * bs * (ss * (1.0 - ss))).astype(BF16)
        dyf_ref[...] = _dot(dbf, wbf_ref[...]).astype(BF16)
        dys_ref[...] = _dot(dbs, wbs_ref[...]).astype(BF16)

    tok = lambda w: pl.BlockSpec((ts, w), lambda i: (i, 0))
    full = lambda a: pl.BlockSpec(a.shape, lambda i: (0, 0))
    b16o = lambda w: jax.ShapeDtypeStruct((s_len, w), BF16)
    return pl.pallas_call(
        body, name=name, grid=(s_len // ts,),
        in_specs=[tok(D_MODEL), tok(ATT_W), tok(ATT_W), tok(D_MODEL), tok(D_MODEL), full(wbf), full(wbs), full(wo)],
        out_specs=[tok(ATT_W), tok(ATT_W)] + [tok(D_MODEL)] * 5,
        out_shape=[b16o(ATT_W), b16o(ATT_W)] + [b16o(D_MODEL)] * 5,
        compiler_params=_cparams("parallel"),
    )(dx, yf, ys, gf, gs, wbf, wbs, wo)


def _mix_bwd(x, dx_in, gain, w_in, fqr, fkr, dfqn, dfkn, qg, kg, dfv, df_col, logf, dsq, dsk, dsv, dgf, dgs, name):
    s_len = x.shape[0]
    ts = min(256, s_len)
    nt = s_len // ts
    gmat = _head_group_matrix()

    def body(x_ref, dxi_ref, gain_ref, w_ref, fqr_ref, fkr_ref, dfqn_ref, dfkn_ref, qg_ref, kg_ref, gm_ref,
             dfv_ref, df_ref, logf_ref, dsq_ref, dsk_ref, dsv_ref, dgf_ref, dgs_ref,
             dp_ref, dx_ref, dgain_ref, dqg_ref, dkg_ref, dbias_ref, carry):
        i = pl.program_id(0)

        @pl.when(i == 0)
        def _():
            carry[...] = jnp.zeros_like(carry)
            dgain_ref[...] = jnp.zeros_like(dgain_ref)
            dqg_ref[...] = jnp.zeros_like(dqg_ref)
            dkg_ref[...] = jnp.zeros_like(dkg_ref)
            dbias_ref[...] = jnp.zeros_like(dbias_ref)

        gm = gm_ref[...]

        def headnorm_bwd(raw, dout, g, dg_ref):
            ms = _dot_split(raw * raw, gm, HEAD_SUM_PARTS) * (1.0 / HEAD_DIM)
            r = lax.rsqrt(ms + EPS)
            nrm = raw * r
            dg_ref[...] += jnp.sum(dout * nrm, axis=0, keepdims=True)
            dn = dout * g
            mean_h = _dot_split(dn * nrm, gm, HEAD_SUM_PARTS) * (1.0 / HEAD_DIM)
            return r * (dn - nrm * mean_h)

        dp_ref[:, C_FQ:C_FQ + ATT_W] = headnorm_bwd(fqr_ref[...], dfqn_ref[...], qg_ref[...], dqg_ref).astype(BF16)
        dp_ref[:, C_FK:C_FK + ATT_W] = headnorm_bwd(fkr_ref[...], dfkn_ref[...], kg_ref[...], dkg_ref).astype(BF16)
        dp_ref[:, C_FV:C_FV + ATT_W] = dfv_ref[...].astype(BF16)
        dp_ref[:, C_SQ:C_SQ + ATT_W] = dsq_ref[...].astype(BF16)
        dp_ref[:, C_SK:C_SK + ATT_W] = dsk_ref[...].astype(BF16)
        dp_ref[:, C_SV:C_SV + ATT_W] = dsv_ref[...].astype(BF16)
        dp_ref[:, C_GF:C_GF + D_MODEL] = dgf_ref[...]
        dp_ref[:, C_GS:C_GS + D_MODEL] = dgs_ref[...]

        r_ = lax.broadcasted_iota(jnp.int32, (ts, ts), 0)
        c_ = lax.broadcasted_iota(jnp.int32, (ts, ts), 1)
        rev = (c_ >= r_).astype(BF16)
        dlogf = _dot_split_left(rev, df_ref[...], 3) + carry[...]
        carry[...] = dlogf[0:1, :]
        lane = lax.broadcasted_iota(jnp.int32, (ts, LANES), 1)
        dfl = jnp.where(lane < N_HEADS, dlogf * (1.0 - jnp.exp(logf_ref[...])), 0.0)
        dbias_ref[...] += jnp.sum(dfl, axis=0, keepdims=True)
        dp_ref[:, C_FL:C_FL + LANES] = dfl.astype(BF16)
        dp_ref[:, C_FL + LANES:C_SQ] = jnp.zeros((ts, C_SQ - C_FL - LANES), BF16)

        dh = _dot(dp_ref[...], w_ref[...])
        xf = x_ref[...]
        r = _rms_rinv(xf)
        xhat = xf * r
        dgain_ref[...] += jnp.sum(dh * xhat, axis=0, keepdims=True)
        dn = dh * gain_ref[...]
        dx_ref[...] = dxi_ref[...] + r * (dn - xhat * jnp.mean(dn * xhat, axis=-1, keepdims=True))

    tok = lambda w: pl.BlockSpec((ts, w), lambda i: (nt - 1 - i, 0))
    full = lambda a: pl.BlockSpec(a.shape, lambda i: (0, 0))
    row = lambda w: pl.BlockSpec((1, w), lambda i: (0, 0))
    return _ride_call(
        body, name, (nt,),
        [tok(D_MODEL), tok(D_MODEL), full(gain), full(w_in), tok(ATT_W), tok(ATT_W), tok(ATT_W), tok(ATT_W),
         full(qg), full(kg), full(gmat), tok(ATT_W), tok(LANES), tok(LANES), tok(ATT_W), tok(ATT_W), tok(ATT_W),
         tok(D_MODEL), tok(D_MODEL)],
        [tok(IN_PAD), tok(D_MODEL), row(D_MODEL), row(ATT_W), row(ATT_W), row(LANES)],
        [jax.ShapeDtypeStruct((s_len, IN_PAD), BF16), jax.ShapeDtypeStruct((s_len, D_MODEL), F32),
         jax.ShapeDtypeStruct((1, D_MODEL), F32), jax.ShapeDtypeStruct((1, ATT_W), F32),
         jax.ShapeDtypeStruct((1, ATT_W), F32), jax.ShapeDtypeStruct((1, LANES), F32)],
        [pltpu.VMEM((1, LANES), F32)], ("arbitrary",),
        (x, dx_in, gain, w_in, fqr, fkr, dfqn, dfkn, qg, kg, gmat, dfv, df_col, logf, dsq, dsk, dsv, dgf, dgs), None)


def _ple_loss(x, p, tgt, gain, wpg, wpp, name):
    s_len = x.shape[0]
    ts = min(512, s_len)

    def body(x_ref, p_ref, t_ref, gain_ref, wpg_ref, wpp_ref, dx_ref, n_ref, ds_ref, dpp_ref, dgain_ref, loss_ref):
        i = pl.program_id(0)

        @pl.when(i == 0)
        def _():
            dgain_ref[...] = jnp.zeros_like(dgain_ref)
            loss_ref[...] = jnp.zeros_like(loss_ref)

        xf = x_ref[...]
        r = _rms_rinv(xf)
        n = xf * r
        hn = (n * gain_ref[...]).astype(BF16)
        n_ref[...] = hn
        sg = _sigmoid(_dot(hn, wpg_ref[...]))
        pp = _dot_nt(p_ref[...].astype(BF16), wpp_ref[...])
        err = (xf + sg * pp) - t_ref[...]
        sq = jnp.sum(jnp.sum(err * err, axis=1, keepdims=True), axis=0, keepdims=True)
        loss_ref[...] += (0.5 / D_MODEL) * sq
        dout = err * (1.0 / D_MODEL)
        dpp_ref[...] = (dout * sg).astype(BF16)
        ds = (dout * pp * (sg * (1.0 - sg))).astype(BF16)
        ds_ref[...] = ds
        dhn = _dot_nt(ds, wpg_ref[...])
        dgain_ref[...] += jnp.sum(dhn * n, axis=0, keepdims=True)
        dn = dhn * gain_ref[...]
        dx_ref[...] = dout + r * (dn - n * jnp.mean(dn * n, axis=-1, keepdims=True))

    tok = lambda w: pl.BlockSpec((ts, w), lambda i: (i, 0))
    full = lambda a: pl.BlockSpec(a.shape, lambda i: (0, 0))
    return pl.pallas_call(
        body, name=name, grid=(s_len // ts,),
        in_specs=[tok(D_MODEL), tok(PLE_DIM), tok(D_MODEL), full(gain), full(wpg), full(wpp)],
        out_specs=[tok(D_MODEL), tok(D_MODEL), tok(D_MODEL), tok(D_MODEL),
                   pl.BlockSpec((1, D_MODEL), lambda i: (0, 0)), pl.BlockSpec((8, LANES), lambda i: (0, 0))],
        out_shape=[jax.ShapeDtypeStruct((s_len, D_MODEL), F32), jax.ShapeDtypeStruct((s_len, D_MODEL), BF16),
                   jax.ShapeDtypeStruct((s_len, D_MODEL), BF16), jax.ShapeDtypeStruct((s_len, D_MODEL), BF16),
                   jax.ShapeDtypeStruct((1, D_MODEL), F32), jax.ShapeDtypeStruct((8, LANES), F32)],
        compiler_params=_cparams("arbitrary"),
    )(x, p, tgt, gain, wpg, wpp)


def _exchange(x, name, broadcast):
    def body(x_ref, out_ref, send_sems, recv_sems, local_sem):
        _exchange_start(x_ref, out_ref, send_sems, recv_sems, local_sem, broadcast)
        _exchange_wait(x_ref, out_ref, send_sems, recv_sems, local_sem, broadcast)

    return pl.pallas_call(
        body, name=name,
        in_specs=[EXCHANGE_SPEC],
        out_specs=EXCHANGE_SPEC,
        out_shape=_exchange_shape(x, broadcast),
        scratch_shapes=list(EXCHANGE_SEMS),
        compiler_params=pltpu.CompilerParams(has_side_effects=True),
    )(x)


def _gather_two_level(x, name):
    def body(x_ref, out_ref, send_sems, recv_sems, local_sem):
        mx, my, mc = lax.axis_index("x"), lax.axis_index("y"), lax.axis_index("c")
        me, sibling = (mx, my, mc), (mx, my, 1 - mc)
        chips = [(1 - mx, my), (mx, 1 - my), (1 - mx, 1 - my)]

        def slot(px, py, pc):
            return out_ref.at[4 * px + 2 * py + pc]

        def copy(k, block, to, src=None):
            return pltpu.make_async_remote_copy(
                src_ref=slot(*block) if src is None else src, dst_ref=slot(*block),
                send_sem=send_sems.at[k], recv_sem=recv_sems.at[k], device_id=to, device_id_type=MESH)

        mine = pltpu.make_async_copy(x_ref, slot(*me), local_sem)
        mine.start()
        first = [copy(0, me, sibling, src=x_ref)]
        first += [copy(1 + j, me, (*chip, mc), src=x_ref) for j, chip in enumerate(chips)]
        for cp in first:
            cp.start()
        passed = [copy(4 + j, (*chip, mc), sibling) for j, chip in enumerate(chips)]
        for j, chip in enumerate(chips):
            copy(1 + j, (*chip, mc), me).wait_recv()
            passed[j].start()
        copy(0, sibling, me).wait_recv()
        for j, chip in enumerate(chips):
            copy(4 + j, (*chip, 1 - mc), me).wait_recv()
        for cp in first + passed:
            cp.wait_send()
        mine.wait()

    return pl.pallas_call(
        body, name=name,
        in_specs=[EXCHANGE_SPEC],
        out_specs=EXCHANGE_SPEC,
        out_shape=_exchange_shape(x, True),
        scratch_shapes=list(EXCHANGE_SEMS),
        compiler_params=pltpu.CompilerParams(has_side_effects=True),
    )(x)


EXCHANGE_SPEC = pl.BlockSpec(memory_space=pl.ANY)
EXCHANGE_SEMS = (pltpu.SemaphoreType.DMA((N_DEV - 1,)), pltpu.SemaphoreType.DMA((N_DEV - 1,)), pltpu.SemaphoreType.DMA)


def _exchange_shape(x, broadcast):
    return jax.ShapeDtypeStruct((N_DEV,) + tuple(x.shape if broadcast else x.shape[1:]), x.dtype)


def _exchange_copies(x_ref, out_ref, send_sems, recv_sems, local_sem, broadcast, with_recv=True):
    mx, my, mc = lax.axis_index("x"), lax.axis_index("y"), lax.axis_index("c")
    me = 4 * mx + 2 * my + mc

    def src(idx):
        return x_ref if broadcast else x_ref.at[idx]

    local = pltpu.make_async_copy(src(me), out_ref.at[me], local_sem)
    pairs = []
    for k in range(1, N_DEV):
        px = (1 - mx) if k & 4 else mx
        py = (1 - my) if k & 2 else my
        pc = (1 - mc) if k & 1 else mc
        peer = 4 * px + 2 * py + pc
        sems = dict(send_sem=send_sems.at[k - 1], recv_sem=recv_sems.at[k - 1], device_id=(px, py, pc), device_id_type=MESH)
        recv = pltpu.make_async_remote_copy(src_ref=src(peer), dst_ref=out_ref.at[peer], **sems) if with_recv else None
        pairs.append((pltpu.make_async_remote_copy(src_ref=src(peer), dst_ref=out_ref.at[me], **sems), recv))
    return local, pairs


def _exchange_start(*refs_and_mode):
    local, pairs = _exchange_copies(*refs_and_mode, with_recv=False)
    local.start()
    for send, _ in pairs:
        send.start()


def _exchange_wait(*refs_and_mode):
    local, pairs = _exchange_copies(*refs_and_mode)
    for _, recv in pairs:
        recv.wait_recv()
    for send, _ in pairs:
        send.wait_send()
    local.wait()


def _riding(body, grid, n_in, n_out, ride):
    if ride is None:
        return body
    broadcast = ride[1]

    def wrapped(*refs):
        ins, x_ref = refs[:n_in], refs[n_in]
        outs, out_ref = refs[n_in + 1:n_in + 1 + n_out], refs[n_in + 1 + n_out]
        scratch, sems = refs[n_in + 2 + n_out:-3], refs[-3:]
        step = pl.program_id(0)
        for d in range(1, len(grid)):
            step = step * grid[d] + pl.program_id(d)
        total = 1
        for g in grid:
            total *= g

        @pl.when(step == 0)
        def _():
            _exchange_start(x_ref, out_ref, *sems, broadcast)

        body(*ins, *outs, *scratch)

        @pl.when(step == total - 1)
        def _():
            _exchange_wait(x_ref, out_ref, *sems, broadcast)

    return wrapped


def _ride_call(body, name, grid, in_specs, out_specs, out_shape, scratch_shapes, sem, operands, ride):
    if ride is None:
        return pl.pallas_call(body, name=name, grid=grid, in_specs=in_specs, out_specs=out_specs, out_shape=out_shape,
                              scratch_shapes=scratch_shapes, compiler_params=_cparams(*sem))(*operands)
    return pl.pallas_call(
        _riding(body, grid, len(in_specs), len(out_specs), ride), name=name, grid=grid,
        in_specs=list(in_specs) + [EXCHANGE_SPEC], out_specs=list(out_specs) + [EXCHANGE_SPEC],
        out_shape=list(out_shape) + [_exchange_shape(*ride)],
        scratch_shapes=list(scratch_shapes) + list(EXCHANGE_SEMS),
        compiler_params=_cparams(*(["arbitrary"] * len(grid))),
    )(*operands, ride[0])


def _adamw_math(w, g, m, v):
    m2 = ADAM_B1 * m + (1.0 - ADAM_B1) * g
    v2 = ADAM_B2 * v + (1.0 - ADAM_B2) * (g * g)
    m_hat = m2 / (1.0 - ADAM_B1 ** ADAM_STEP)
    v_hat = v2 / (1.0 - ADAM_B2 ** ADAM_STEP)
    delta = -ADAM_LR * (m_hat / (jnp.sqrt(v_hat) + ADAM_EPS) + ADAM_WD * w)
    return delta, m2, v2


def _sum_parts(parts, name, tr):
    _, rows, cols = parts.shape

    def body(p_ref, g_ref):
        g = p_ref[0].astype(F32)
        for s in range(1, N_DEV):
            g = g + p_ref[s].astype(F32)
        g_ref[...] = g

    return pl.pallas_call(
        body, name=name, grid=(rows // tr,),
        in_specs=[pl.BlockSpec((N_DEV, tr, cols), lambda i: (0, i, 0))],
        out_specs=pl.BlockSpec((tr, cols), lambda i: (i, 0)),
        out_shape=jax.ShapeDtypeStruct((rows, cols), F32),
        compiler_params=_cparams("parallel"),
    )(parts)


ADAM_SPLIT_ELEMS = 400_000


def _adamw_shard(g, w, m, v, name):
    rows, cols = w.shape
    tr = rows // 2 if rows * cols > ADAM_SPLIT_ELEMS else rows

    def body(g_ref, w_ref, m_ref, v_ref, d_ref, m2_ref, v2_ref):
        d_ref[...], m2_ref[...], v2_ref[...] = _adamw_math(w_ref[...], g_ref[...], m_ref[...], v_ref[...])

    blk = pl.BlockSpec((tr, cols), lambda i: (i, 0))
    return pl.pallas_call(
        body, name=name, grid=(rows // tr,),
        in_specs=[blk] * 4, out_specs=[blk] * 3,
        out_shape=[jax.ShapeDtypeStruct((rows, cols), F32)] * 3,
        compiler_params=_cparams("parallel"),
    )(g, w, m, v)


def _adamw_small(parts, w, m, v, name):
    names = list(SMALL_NAMES)

    def body(p_ref, *refs):
        ins, outs = refs[:3 * len(names)], refs[3 * len(names):]
        total = p_ref[0]
        for s in range(1, N_DEV):
            total = total + p_ref[s]
        for i, n in enumerate(names):
            row, off, width = SMALL_POS[n]
            g = total[row:row + 1, off:off + width]
            w_ref, m_ref, v_ref = ins[3 * i:3 * i + 3]
            g_ref, d_ref, m2_ref, v2_ref = outs[4 * i:4 * i + 4]
            g_ref[...] = g
            d_ref[...], m2_ref[...], v2_ref[...] = _adamw_math(w_ref[...], g, m_ref[...], v_ref[...])
        row, off, _ = SMALL_POS["loss"]
        outs[-1][...] = total[row:row + 1, off:off + 1]

    operands = [parts] + [t[n] for n in names for t in (w, m, v)]
    shapes = [jax.ShapeDtypeStruct(w[n].shape, F32) for n in names for _ in range(4)]
    shapes.append(jax.ShapeDtypeStruct((1, 1), F32))
    out = pl.pallas_call(body, name=name, out_shape=shapes)(*operands)
    return {n: tuple(out[4 * i:4 * i + 4]) for i, n in enumerate(names)}, out[-1]


TRANSPOSED = frozenset(("ffn1_w_gate", "ffn1_w_up", "w_in", "w_branch_fox", "w_branch_sb", "ffn2_w_gate", "ffn2_w_up",
                        "w_ple_proj"))
F_PAD_ROWS = C_SQ - FL_REAL_END


def _pack(pieces, group, dtype):
    out = []
    for name in GATHER_GROUPS[group]:
        r = pieces[name].T if name in TRANSPOSED else pieces[name]
        r = r.reshape(-1, D_MODEL).astype(dtype)
        if r.shape[0] != PACK_ROWS[name]:
            r = jnp.pad(r, ((0, PACK_ROWS[name] - r.shape[0]), (0, 0)))
        out.append(r)
    return jnp.concatenate(out, axis=0)


def _real_rows(name):
    return W_IN_ROWS if name == "w_in" else PACK_ROWS[name]


def _gathered(got, name, shape):
    off = GATHER_OFF[name]
    return got[:, off:off + _real_rows(name), :].reshape(shape)


def _w_in_device_rows(d):
    lo, hi = d * W_IN_ROWS, (d + 1) * W_IN_ROWS
    if hi <= FL_REAL_END:
        return [(lo, hi)]
    if lo >= FL_REAL_END:
        return [(lo + F_PAD_ROWS, hi + F_PAD_ROWS)]
    return [(lo, FL_REAL_END), (C_SQ, hi + F_PAD_ROWS)]


def _w_in_t_padded(got):
    t = _gathered(got, "w_in", (IN_REAL, D_MODEL))
    return jnp.concatenate([t[:FL_REAL_END], jnp.zeros((F_PAD_ROWS, D_MODEL), t.dtype), t[FL_REAL_END:]], axis=0)


def _pack_chunks(grads, group):
    out = []
    for name in SCATTER_GROUPS[group]:
        base, _, half = name.partition("#")
        g = grads[base].astype(BF16)
        if base == "w_in":
            lo, hi = W_IN_HALVES[int(half)]
            tail = jnp.zeros((PACK_ROWS[base] - W_IN_ROWS, D_MODEL), BF16)
            c = jnp.stack([jnp.concatenate([g[a:b] for a, b in _w_in_device_rows(d)] + [tail], axis=0)[lo:hi]
                           for d in range(N_DEV)])
        else:
            c = g.reshape(N_DEV, PACK_ROWS[name], D_MODEL)
        out.append(c)
    return out[0] if len(out) == 1 else jnp.concatenate(out, axis=1)


def _shard_grad(summed, name, shape):
    if name == "w_in":
        rows = jnp.concatenate([summed[f"w_in#{i}"] for i in range(len(W_IN_HALVES))], axis=0)[:W_IN_ROWS]
    else:
        rows = summed[name][SCATTER_OFF[name]:SCATTER_OFF[name] + PACK_ROWS[name], :]
    return rows.reshape(shape[1], shape[0]).T if name in TRANSPOSED else rows.reshape(shape)


WEIGHT_NAMES = ['ffn1_norm', 'ffn1_w_gate', 'ffn1_w_up', 'ffn1_w_down', 'mix_norm', 'w_in', 'forget_bias', 'q_norm',
                'k_norm', 'w_branch_fox', 'w_branch_sb', 'w_out', 'ffn2_norm', 'ffn2_w_gate', 'ffn2_w_up',
                'ffn2_w_down', 'ple_norm', 'w_ple_gate', 'w_ple_proj']
SMALL_NAMES = ('ffn1_norm', 'mix_norm', 'ffn2_norm', 'ple_norm', 'q_norm', 'k_norm', 'forget_bias')
SMALL_POS = {'ffn1_norm': (0, 0, D_MODEL), 'mix_norm': (1, 0, D_MODEL), 'ffn2_norm': (2, 0, D_MODEL),
             'ple_norm': (3, 0, D_MODEL), 'q_norm': (4, 0, HEAD_DIM), 'k_norm': (4, HEAD_DIM, HEAD_DIM),
             'forget_bias': (4, 2 * HEAD_DIM, N_HEADS), 'loss': (4, 2 * HEAD_DIM + N_HEADS, 1)}


def _pack_small(vals, loss):
    tail = [vals[n].reshape(1, -1) for n in ('q_norm', 'k_norm', 'forget_bias')] + [loss.reshape(1, 1)]
    tail.append(jnp.zeros((1, D_MODEL - sum(t.shape[1] for t in tail)), F32))
    rows = [vals[n].reshape(1, D_MODEL) for n in SMALL_NAMES[:4]] + [jnp.concatenate(tail, axis=1)]
    rows.append(jnp.zeros((SMALL_ROWS - len(rows), D_MODEL), F32))
    return jnp.concatenate(rows, axis=0)


def _step(x, p, tgt, w):
    row = lambda a: a.reshape(1, -1).astype(F32)
    g_ffn1, g_mix, g_ffn2, g_ple = (row(w[n]) for n in SMALL_NAMES[:4])
    qg = jnp.tile(row(w['q_norm']), (1, N_HEADS))
    kg = jnp.tile(row(w['k_norm']), (1, N_HEADS))
    bias = jnp.pad(row(w['forget_bias']), ((0, 0), (0, LANES - N_HEADS)))
    half = D_FF // 2
    grads = {}

    blk = lambda n: GATHER_OFF[n] // FFN_SHARD
    ffn1 = tuple(blk(n) for n in ("ffn1_w_gate", "ffn1_w_up", "ffn1_w_down"))
    ffn2 = tuple(blk(n) for n in ("ffn2_w_gate", "ffn2_w_up", "ffn2_w_down"))
    got0 = _gather_two_level(_pack(w, 0, BF16), "gather_ffn1")
    x1, g1, u1, h1, got1 = _ffn_fwd(x, g_ffn1, got0, ffn1, "ffn1_fwd", ride=(_pack(w, 1, BF16), True))
    w_in = _w_in_t_padded(got1)
    wbf = _gathered(got1, "w_branch_fox", (D_MODEL, ATT_W))
    wbs = _gathered(got1, "w_branch_sb", (D_MODEL, ATT_W))
    wo = _gathered(got1, "w_out", (D_MODEL, D_MODEL))
    (hmix, fqr, fkr, fqn, fkn, fv, logf, f_col, f_row, sq, sk, sv, gf, gs, kmax) = _mix_fwd(
        x1, g_mix, w_in, bias, qg, kg, "mix_fwd")
    y_fox, lse, got2 = _fox_fwd(fqn, fkn, fv, f_col, f_row, kmax, "fox_fwd", ride=(_pack(w, 2, BF16), True))
    wpg = _gathered(got2, "w_ple_gate", (D_MODEL, D_MODEL))
    wpp = _gathered(got2, "w_ple_proj", (D_MODEL, PLE_DIM))
    y_sb, y_sb32 = _sb_fwd(sq, sk, sv, "sb_fwd")
    x2 = _merge_fwd(x1, y_fox, y_sb, gf, gs, wbf, wbs, wo, "merge_fwd")
    x3, g2, u2, h2, = _ffn_fwd(x2, g_ffn2, got2, ffn2, "ffn2_fwd")
    dx3, n_ple, ds_ple, dpp, dg_ple, loss = _ple_loss(x3, p, tgt, g_ple, wpg, wpp, "ple_loss")

    grads['w_ple_gate'] = _wgrad(n_ple, ds_ple, "dw_ple_gate", D_MODEL, D_MODEL)
    grads['w_ple_proj'] = _wgrad(dpp, p, "dw_ple_proj", D_MODEL, PLE_DIM)
    dg2, du2, act2, dx2, dg_ffn2 = _ffn_bwd_fused(x2, dx3, g_ffn2, g2, u2, got2, ffn2, "ffn2_bwd")
    grads['ffn2_w_gate'] = _wgrad(dg2, h2, "dw_ffn2_gate", half, D_MODEL)
    grads['ffn2_w_up'] = _wgrad(du2, h2, "dw_ffn2_up", half, D_MODEL)
    grads['ffn2_w_down'] = _wgrad(act2, dx3, "dw_ffn2_down", half, D_MODEL)
    dyf, dys, dgf, dgs, dbf, dbs, merged = _merge_bwd(dx2, y_fox, y_sb, gf, gs, wbf, wbs, wo, "merge_bwd")
    grads['w_branch_fox'] = _wgrad(dbf, y_fox, "dw_branch_fox", D_MODEL, ATT_W)
    grads['w_branch_sb'] = _wgrad(dbs, y_sb, "dw_branch_sb", D_MODEL, ATT_W)
    grads['w_out'] = _wgrad(merged, dx2, "dw_out", D_MODEL, D_MODEL)
    dfqn, dfkn, dfv, dft, part_rest = _fox_bwd(fqn, fkn, fv, dyf, y_fox, lse, f_col, f_row, kmax, "fox_bwd",
                                               ride=(_pack_chunks(grads, 5), False))
    dsq, dsk, dsv = _sb_bwd(sq, sk, sv, dys, y_sb32, "sb_bwd")
    s_len = x.shape[0]
    df_col = jnp.pad(dft[:, :2, :].reshape(N_HEADS, s_len).T, ((0, 0), (0, LANES - N_HEADS)))
    dproj, dx1, dg_mix, dqg, dkg, dbias = _mix_bwd(
        x1, dx2, g_mix, w_in, fqr, fkr, dfqn, dfkn, qg, kg, dfv, df_col, logf, dsq, dsk, dsv, dgf, dgs, "mix_bwd")
    grads['w_in'] = _wgrad(dproj, hmix, "dw_in", IN_PAD // 3, D_MODEL)
    dg1, du1, act1, dx0, dg_ffn1, part_in0 = _ffn_bwd_fused(x, dx1, g_ffn1, g1, u1, got0, ffn1, "ffn1_bwd",
                                                            ride=(_pack_chunks(grads, 3), False))
    grads['ffn1_w_gate'], part_in1 = _wgrad(dg1, h1, "dw_ffn1_gate", half, D_MODEL,
                                            ride=(_pack_chunks(grads, 4), False))
    grads['ffn1_w_up'], part_gate = _wgrad(du1, h1, "dw_ffn1_up", half, D_MODEL, ride=(_pack_chunks(grads, 0), False))
    grads['ffn1_w_down'], part_up = _wgrad(act1, dx1, "dw_ffn1_down", half, D_MODEL,
                                           ride=(_pack_chunks(grads, 1), False))
    part_down = _exchange(_pack_chunks(grads, 2), "scatter_ffn1_down", False)

    fold = lambda a: a.reshape(N_HEADS, HEAD_DIM).sum(axis=0).reshape(1, HEAD_DIM)
    small_g = {'ffn1_norm': dg_ffn1, 'mix_norm': dg_mix, 'ffn2_norm': dg_ffn2, 'ple_norm': dg_ple,
               'q_norm': fold(dqg), 'k_norm': fold(dkg), 'forget_bias': dbias[:, :N_HEADS]}
    return loss[0, 0], dx0, (part_gate, part_up, part_down, part_in0, part_in1, part_rest), small_g


def kernel(x, p, ffn1_norm, ffn1_w_gate, ffn1_w_up, ffn1_w_down, mix_norm, w_in, forget_bias, q_norm, k_norm, w_branch_fox, w_branch_sb, w_out, ffn2_norm, ffn2_w_gate, ffn2_w_up, ffn2_w_down, ple_norm, w_ple_gate, w_ple_proj, loss_target, m_ffn1_norm, m_ffn1_w_gate, m_ffn1_w_up, m_ffn1_w_down, m_mix_norm, m_w_in, m_forget_bias, m_q_norm, m_k_norm, m_w_branch_fox, m_w_branch_sb, m_w_out, m_ffn2_norm, m_ffn2_w_gate, m_ffn2_w_up, m_ffn2_w_down, m_ple_norm, m_w_ple_gate, m_w_ple_proj, v_ffn1_norm, v_ffn1_w_gate, v_ffn1_w_up, v_ffn1_w_down, v_mix_norm, v_w_in, v_forget_bias, v_q_norm, v_k_norm, v_w_branch_fox, v_w_branch_sb, v_w_out, v_ffn2_norm, v_ffn2_w_gate, v_ffn2_w_up, v_ffn2_w_down, v_ple_norm, v_w_ple_gate, v_w_ple_proj):
    args = dict(locals())
    w = {n: args[n][0] for n in WEIGHT_NAMES}
    m = {n: args["m_" + n][0] for n in WEIGHT_NAMES}
    v = {n: args["v_" + n][0] for n in WEIGHT_NAMES}
    loss, dx, parts, small_g = _step(x[0], p[0, 0], loss_target[0], w)

    summed = {}
    for grp, part in enumerate(parts):
        s = _sum_parts(part, f"sum_grads_{grp}", SUM_TILE_ROWS[grp])
        summed.update({n: s for n in SCATTER_GROUPS[grp]})
    big = {}
    for n in WEIGHT_NAMES:
        if n not in SMALL_NAMES:
            g = _shard_grad(summed, n, w[n].shape)
            big[n] = (g,) + tuple(_adamw_shard(g, w[n], m[n], v[n], "adamw_" + n))
    small_parts = _exchange(_pack_small(small_g, loss), "gather_small", True)
    small, total_loss = _adamw_small(small_parts, *({n: args[pre + n] for n in SMALL_NAMES} for pre in ("", "m_", "v_")),
                                     "adamw_small")
    big.update(small)

    outs = [total_loss.reshape(()), dx.reshape(x.shape)]
    for kind in range(4):
        outs += [big[n][kind].reshape(args[n].shape) for n in WEIGHT_NAMES]
    return tuple(outs)
```

```python
import jax
import jax.numpy as jnp
from jax import lax
from jax.experimental import pallas as pl
from jax.experimental.pallas import tpu as pltpu

F32 = jnp.float32
BF16 = jnp.bfloat16

D_MODEL = 1024
D_FF = 2816
N_HEADS = 8
HEAD_DIM = 64
ATT_W = N_HEADS * HEAD_DIM
PLE_DIM = 256
EPS = 1e-6
N_DEV = 8
MESH = pl.DeviceIdType.MESH

LANES = 128
V7X_SCOPED_VMEM_BYTES = 56 * 1024 * 1024

C_FQ, C_FK, C_FV, C_FL = 0, 512, 1024, 1536
C_SQ, C_SK, C_SV, C_GF, C_GS = 1792, 2304, 2816, 3328, 4352
IN_PAD = 5376
IN_REAL = 5128
FL_REAL_END = 1544

ADAM_LR = 0.001
ADAM_B1 = 0.9
ADAM_B2 = 0.999
ADAM_EPS = 1e-08
ADAM_WD = 0.01
ADAM_STEP = 10

PACK_ROWS = {"ffn1_w_gate": 352, "ffn1_w_up": 352, "ffn1_w_down": 352, "w_in": 656, "w_branch_fox": 64,
             "w_branch_sb": 64, "w_out": 128, "ffn2_w_gate": 352, "ffn2_w_up": 352, "ffn2_w_down": 352,
             "w_ple_gate": 128, "w_ple_proj": 32}
GATHER_GROUPS = (
    ("ffn1_w_gate", "ffn1_w_up", "ffn1_w_down"),
    ("w_in", "w_branch_fox", "w_branch_sb", "w_out"),
    ("ffn2_w_gate", "ffn2_w_up", "ffn2_w_down", "w_ple_gate", "w_ple_proj"),
)
W_IN_HALVES = ((0, 336), (336, 656))
PACK_ROWS.update({f"w_in#{i}": hi - lo for i, (lo, hi) in enumerate(W_IN_HALVES)})
SCATTER_GROUPS = (
    ("ffn1_w_gate",), ("ffn1_w_up",), ("ffn1_w_down",),
    ("w_in#0",), ("w_in#1",),
    ("ffn2_w_gate", "ffn2_w_up", "ffn2_w_down", "w_ple_gate", "w_ple_proj", "w_branch_fox", "w_branch_sb", "w_out"),
)
SUM_TILE_ROWS = (352, 352, 352, 336, 320, 368)


def _offsets(groups):
    off = {}
    for grp in groups:
        o = 0
        for n in grp:
            off[n] = o
            o += PACK_ROWS[n]
    return off


GATHER_OFF = _offsets(GATHER_GROUPS)
SCATTER_OFF = _offsets(SCATTER_GROUPS)
W_IN_ROWS = 641

SMALL_ROWS = 8


def _cparams(*sem):
    return pltpu.CompilerParams(dimension_semantics=sem, vmem_limit_bytes=V7X_SCOPED_VMEM_BYTES)


def _dot(a, b):
    return jnp.dot(a, b, preferred_element_type=F32)


def _dot_nt(a, b):
    return lax.dot_general(a, b, (((1,), (1,)), ((), ())), preferred_element_type=F32)


def _dot_tn(a, b):
    return lax.dot_general(a, b, (((0,), (0,)), ((), ())), preferred_element_type=F32)


def _split(x, parts):
    out = []
    r = x
    for _ in range(parts):
        p = r.astype(BF16)
        out.append(p)
        r = r - p.astype(F32)
    return out


def _dot_split(x, m, parts):
    acc = None
    for p in _split(x, parts):
        t = _dot(p, m)
        acc = t if acc is None else acc + t
    return acc


def _dot_split_left(m, x, parts):
    acc = None
    for p in _split(x, parts):
        t = _dot(m, p)
        acc = t if acc is None else acc + t
    return acc


def _rms_rinv(xf):
    return lax.rsqrt(jnp.mean(xf * xf, axis=-1, keepdims=True) + EPS)


def _sigmoid(x):
    return 1.0 / (1.0 + jnp.exp(-x))


def _softplus_neg_abs(z):
    return jnp.log(1.0 + jnp.exp(-jnp.abs(z)))


FFN_SHARD = D_FF // N_DEV
FFN_CHUNK = 4


def _ffn_w_spec(blk, index_map):
    return pl.BlockSpec((FFN_CHUNK, FFN_SHARD, D_MODEL), lambda *g: (index_map(*g), blk, 0))


def _ffn_w(ref):
    return ref[...].reshape(FFN_CHUNK * FFN_SHARD, D_MODEL)


def _ffn_fwd(x, gain, wbuf, blks, name, ride=None):
    s_len = x.shape[0]
    ts = min(512, s_len)
    fc = FFN_CHUNK * FFN_SHARD
    nt, nc = s_len // ts, D_FF // fc

    def body(x_ref, gain_ref, wg_ref, wu_ref, wd_ref, y_ref, g_ref, u_ref, h_ref, acc_scr):
        j = pl.program_id(1)

        @pl.when(j == 0)
        def _():
            xf = x_ref[...]
            h_ref[...] = ((xf * _rms_rinv(xf)) * gain_ref[...]).astype(BF16)
            acc_scr[...] = jnp.zeros_like(acc_scr)

        h = h_ref[...]
        g = _dot_nt(h, _ffn_w(wg_ref))
        u = _dot_nt(h, _ffn_w(wu_ref))
        g_ref[...] = g.astype(BF16)
        u_ref[...] = u.astype(BF16)
        a = (g * _sigmoid(g) * u).astype(BF16)
        acc_scr[...] += _dot(a, _ffn_w(wd_ref))

        @pl.when(j == nc - 1)
        def _():
            y_ref[...] = x_ref[...] + 0.5 * acc_scr[...]

    tok = pl.BlockSpec((ts, D_MODEL), lambda i, j: (i, 0))
    hid = pl.BlockSpec((ts, fc), lambda i, j: (i, j))
    return _ride_call(
        body, name, (nt, nc),
        [tok, pl.BlockSpec((1, D_MODEL), lambda i, j: (0, 0))] + [_ffn_w_spec(b, lambda i, j: j) for b in blks],
        [tok, hid, hid, tok],
        [jax.ShapeDtypeStruct((s_len, D_MODEL), F32), jax.ShapeDtypeStruct((s_len, D_FF), BF16),
         jax.ShapeDtypeStruct((s_len, D_FF), BF16), jax.ShapeDtypeStruct((s_len, D_MODEL), BF16)],
        [pltpu.VMEM((ts, D_MODEL), F32)], ("parallel", "arbitrary"), (x, gain, wbuf, wbuf, wbuf), ride)


def _ffn_bwd_fused(x, dy, gain, g, u, wbuf, blks, name, ride=None):
    s_len = x.shape[0]
    ts = min(512, s_len)
    fc = FFN_CHUNK * FFN_SHARD
    nt = s_len // ts
    assert D_FF == 2 * fc

    def hidden(dy_ref, g_ref, u_ref, wg_ref, wu_ref, wd_ref, dg_ref, du_ref, act_ref):
        da = 0.5 * _dot_nt(dy_ref[...].astype(BF16), _ffn_w(wd_ref))
        gf = g_ref[...].astype(F32)
        uf = u_ref[...].astype(F32)
        sg = _sigmoid(gf)
        silu = gf * sg
        dg = (da * uf * (sg * (1.0 + gf * (1.0 - sg)))).astype(BF16)
        du = (da * silu).astype(BF16)
        dg_ref[...] = dg
        du_ref[...] = du
        act_ref[...] = (0.5 * silu * uf).astype(BF16)
        return _dot(dg, _ffn_w(wg_ref)) + _dot(du, _ffn_w(wu_ref))

    def first(dy_ref, g_ref, u_ref, wg_ref, wu_ref, wd_ref, dg_ref, du_ref, act_ref, dh_ref):
        dh_ref[...] = hidden(dy_ref, g_ref, u_ref, wg_ref, wu_ref, wd_ref, dg_ref, du_ref, act_ref)

    def second(x_ref, dy_ref, gain_ref, g_ref, u_ref, wg_ref, wu_ref, wd_ref, dh0_ref, dg_half, du_half, act_half,
               dg_ref, du_ref, act_ref, dx_ref, dgain_ref):
        i = pl.program_id(0)
        dh = dh0_ref[...] + hidden(dy_ref, g_ref, u_ref, wg_ref, wu_ref, wd_ref, dg_ref, du_ref, act_ref)
        xf = x_ref[...]
        r = _rms_rinv(xf)
        xhat = xf * r
        dgp = jnp.sum(dh * xhat, axis=0, keepdims=True)

        @pl.when(i == 0)
        def _():
            dgain_ref[...] = dgp

        @pl.when(i > 0)
        def _():
            dgain_ref[...] += dgp

        dn = dh * gain_ref[...]
        dx_ref[...] = dy_ref[...] + r * (dn - xhat * jnp.mean(dn * xhat, axis=-1, keepdims=True))

    tok = pl.BlockSpec((ts, D_MODEL), lambda i: (i, 0))
    row = pl.BlockSpec((1, D_MODEL), lambda i: (0, 0))
    hid = lambda c: pl.BlockSpec((ts, fc), lambda i: (i, c))
    wts = lambda c: [pl.BlockSpec((FFN_CHUNK, FFN_SHARD, D_MODEL), lambda i, b=b: (c, b, 0),
                                  pipeline_mode=pl.Buffered(1)) for b in blks]
    hidden_shapes = [jax.ShapeDtypeStruct((s_len, D_FF), BF16)] * 3
    dg, du, act, dh0, *rode = _ride_call(
        first, name + "_a", (nt,), [tok, hid(0), hid(0)] + wts(0), [hid(0)] * 3 + [tok],
        hidden_shapes + [jax.ShapeDtypeStruct((s_len, D_MODEL), F32)], [], ("parallel",),
        (dy, g, u, wbuf, wbuf, wbuf), ride)
    filled = pl.BlockSpec(memory_space=pl.ANY)
    dg, du, act, dx, dgain = pl.pallas_call(
        second, name=name + "_b", grid=(nt,),
        in_specs=[tok, tok, row, hid(1), hid(1)] + wts(1) + [tok, filled, filled, filled],
        out_specs=[hid(1)] * 3 + [tok, row],
        out_shape=hidden_shapes + [jax.ShapeDtypeStruct((s_len, D_MODEL), F32), jax.ShapeDtypeStruct((1, D_MODEL), F32)],
        input_output_aliases={9: 0, 10: 1, 11: 2},
        compiler_params=_cparams("arbitrary"),
    )(x, dy, gain, g, u, wbuf, wbuf, wbuf, dh0, dg, du, act)
    return (dg, du, act, dx, dgain, *rode)


def _wgrad(a, b, name, tk, tn, ride=None):
    s_len, k_dim = a.shape
    n_dim = b.shape[1]
    ts = min(2048, s_len)
    ns = s_len // ts

    def body(a_ref, b_ref, o_ref, acc):
        s = pl.program_id(2)
        p = _dot_tn(a_ref[...].astype(BF16), b_ref[...].astype(BF16))

        @pl.when(s == 0)
        def _():
            acc[...] = p

        @pl.when(s > 0)
        def _():
            acc[...] += p

        @pl.when(s == ns - 1)
        def _():
            o_ref[...] = acc[...].astype(BF16)

    out = _ride_call(
        body, name, (k_dim // tk, n_dim // tn, ns),
        [pl.BlockSpec((ts, tk), lambda k, n, s: (s, k)), pl.BlockSpec((ts, tn), lambda k, n, s: (s, n))],
        [pl.BlockSpec((tk, tn), lambda k, n, s: (k, n))], [jax.ShapeDtypeStruct((k_dim, n_dim), BF16)],
        [pltpu.VMEM((tk, tn), F32)], ("parallel", "parallel", "arbitrary"), (a, b), ride)
    return out[0] if ride is None else tuple(out)


HEAD_SUM_PARTS = 2


def _head_group_matrix():
    r = lax.broadcasted_iota(jnp.int32, (ATT_W, ATT_W), 0) // HEAD_DIM
    c = lax.broadcasted_iota(jnp.int32, (ATT_W, ATT_W), 1) // HEAD_DIM
    return (r == c).astype(BF16)


def _mix_fwd(x, gain, w_in, bias, qg, kg, name):
    s_len = x.shape[0]
    ts = min(512, s_len)
    nt = s_len // ts
    gmat = _head_group_matrix()

    def body(x_ref, gain_ref, w_ref, bias_ref, qg_ref, kg_ref, gm_ref,
             h_ref, fqr_ref, fkr_ref, fqn_ref, fkn_ref, fv_ref, logf_ref, f_ref, ft_ref,
             sq_ref, sk_ref, sv_ref, gf_ref, gs_ref, kmax_ref, carry):
        i = pl.program_id(0)
        xf = x_ref[...]
        h = ((xf * _rms_rinv(xf)) * gain_ref[...]).astype(BF16)
        h_ref[...] = h
        gm = gm_ref[...]

        def proj(lo, n):
            return _dot_nt(h, w_ref[lo:lo + n, :])

        def headnorm(raw, g):
            ms = _dot_split(raw * raw, gm, HEAD_SUM_PARTS) * (1.0 / HEAD_DIM)
            return ((raw * lax.rsqrt(ms + EPS)) * g).astype(BF16)

        fq = proj(C_FQ, ATT_W)
        fqr_ref[...] = fq
        fqn_ref[...] = headnorm(fq, qg_ref[...])
        fk = proj(C_FK, ATT_W)
        fkr_ref[...] = fk
        fkn = headnorm(fk, kg_ref[...])
        fkn_ref[...] = fkn
        kn2 = jnp.max(_dot_split(jnp.square(fkn.astype(F32)), gm, HEAD_SUM_PARTS), axis=0, keepdims=True)

        @pl.when(i == 0)
        def _():
            kmax_ref[...] = kn2

        @pl.when(i > 0)
        def _():
            kmax_ref[...] = jnp.maximum(kmax_ref[...], kn2)
        fv_ref[...] = proj(C_FV, ATT_W).astype(BF16)
        sq_ref[...] = proj(C_SQ, ATT_W).astype(BF16)
        sk_ref[...] = proj(C_SK, ATT_W).astype(BF16)
        sv_ref[...] = proj(C_SV, ATT_W).astype(BF16)
        gf_ref[...] = proj(C_GF, D_MODEL)
        gs_ref[...] = proj(C_GS, D_MODEL)

        fl = proj(C_FL, LANES) + bias_ref[...]
        lane = lax.broadcasted_iota(jnp.int32, fl.shape, 1)
        logf = jnp.where(lane < N_HEADS, jnp.minimum(fl, 0.0) - _softplus_neg_abs(fl), 0.0)
        logf_ref[...] = logf

        @pl.when(i == 0)
        def _():
            carry[...] = jnp.zeros_like(carry)

        r = lax.broadcasted_iota(jnp.int32, (ts, ts), 0)
        c = lax.broadcasted_iota(jnp.int32, (ts, ts), 1)
        tri = (r >= c).astype(BF16)
        f_tile = _dot_split_left(tri, logf, 3) + carry[...]
        f_ref[...] = f_tile
        ft_ref[...] = f_tile.T[:N_HEADS, :]
        carry[...] = f_tile[ts - 1:ts, :]

    tok = lambda w: pl.BlockSpec((ts, w), lambda i: (i, 0))
    full = lambda a: pl.BlockSpec(a.shape, lambda i: (0, 0), pipeline_mode=pl.Buffered(1))
    f32o = lambda w: jax.ShapeDtypeStruct((s_len, w), F32)
    b16o = lambda w: jax.ShapeDtypeStruct((s_len, w), BF16)
    return _ride_call(
        body, name, (nt,),
        [tok(D_MODEL), full(gain), full(w_in), full(bias), full(qg), full(kg), full(gmat)],
        [
            tok(D_MODEL), tok(ATT_W), tok(ATT_W), tok(ATT_W), tok(ATT_W), tok(ATT_W), tok(LANES), tok(LANES),
            pl.BlockSpec((N_HEADS, ts), lambda i: (0, i)),
            tok(ATT_W), tok(ATT_W), tok(ATT_W), tok(D_MODEL), tok(D_MODEL),
            pl.BlockSpec((1, ATT_W), lambda i: (0, 0)),
        ],
        [
            b16o(D_MODEL), f32o(ATT_W), f32o(ATT_W), b16o(ATT_W), b16o(ATT_W), b16o(ATT_W), f32o(LANES), f32o(LANES),
            jax.ShapeDtypeStruct((N_HEADS, s_len), F32),
            b16o(ATT_W), b16o(ATT_W), b16o(ATT_W), f32o(D_MODEL), f32o(D_MODEL),
            jax.ShapeDtypeStruct((1, ATT_W), F32),
        ],
        [pltpu.VMEM((1, LANES), F32)], ("arbitrary",), (x, gain, w_in, bias, qg, kg, gmat), None)


ATT_T = 256
SB_ROWS = 2
FOX_ROWS = 2
EXP_ZERO = 88.0


def _att_tiling(s_len, rows):
    t = min(ATT_T, s_len)
    nr = min(rows, s_len // t)
    return t, nr, s_len // (t * nr)


def _pair_specs(s_len, tq):
    qblk = pl.BlockSpec((tq, LANES), lambda hp, i: (i, hp))
    kvfull = pl.BlockSpec((s_len, LANES), lambda hp, i: (0, hp))
    return qblk, kvfull


def _walk_tiles(i, nr, load, sub, flush, init, more=None, trips=None):
    base = i * nr
    carries = list(init)
    for kk in range(nr - 1, -1, -1):
        rs = list(range(kk, nr))
        new, side = sub(rs, load(base + kk), [carries[r] for r in rs], [r == kk for r in rs])
        carries[kk:] = new
        flush(base + kk, side)

    done = jnp.int32(0)
    for last in range(nr - 1, -1, -1):
        rs = list(range(last + 1))

        def visit(n, cs, rs=rs):
            kb = base - 1 - n
            cs, side = sub(rs, load(kb), list(cs), [False] * len(rs))
            flush(kb, side)
            return tuple(cs)

        if trips is not None:
            todo = jnp.maximum(trips(carries, base, last) - done, 0)
            new = lax.fori_loop(0, todo, lambda it, cs, start=done, visit=visit: visit(start + it, cs),
                                tuple(carries[:last + 1]))
            done = done + todo
        else:
            def step(state, visit=visit, last=last):
                n, _, cs = state
                cs = visit(n, cs)
                return n + 1, more(cs[last]), cs

            done, _, new = lax.while_loop(lambda state: jnp.logical_and(state[0] < base, state[1] > 0), step,
                                          (done, more(carries[last]), tuple(carries[:last + 1])))
        carries[:last + 1] = list(new)
    return carries


def _stack(parts):
    return parts[0] if len(parts) == 1 else jnp.concatenate(parts, axis=0)


def _stacked_halves(x, lo):
    z = jnp.zeros_like(x)
    return jnp.concatenate([jnp.where(lo, x, z), jnp.where(lo, z, x)], axis=0)


def _fox_qk_bound(qst_r, km_ref, t):
    km = km_ref[...]
    out = []
    for j in (0, 1):
        qf = qst_r[j * t:(j + 1) * t, :].astype(F32)
        qn = jnp.sqrt(jnp.sum(qf * qf, axis=1, keepdims=True))
        out.append(qn * jnp.sqrt(km[:, j * HEAD_DIM:j * HEAD_DIM + 1]) * 1.001 + 1.0)
    return out


def _fox_trips(hp, flast_ref, qkb, fq, level):
    def trips(carries, base, r):
        gap = [jnp.max(qkb[r][j] + fq[r][j] - level(carries, r, j)) for j in (0, 1)]

        def needed(n):
            kb = jnp.maximum(base - 1 - n, 0)
            return jnp.logical_or(gap[0] - flast_ref[2 * hp, kb] > -EXP_ZERO,
                                  gap[1] - flast_ref[2 * hp + 1, kb] > -EXP_ZERO)

        return lax.while_loop(lambda n: jnp.logical_and(n < base, needed(n)), lambda n: n + 1, jnp.int32(0))
    return trips


def _fox_fwd(q, k, v, f_col, f_row, kmax, name, ride=None):
    s_len = q.shape[0]
    t, nr, nq = _att_tiling(s_len, FOX_ROWS)

    def body(q_ref, k_ref, v_ref, f_ref, ft_ref, km_ref, fl_ref, y_ref, lse_ref):
        hp = pl.program_id(0)
        i = pl.program_id(1)
        lane = lax.broadcasted_iota(jnp.int32, (t, LANES), 1)
        lo = lane < HEAD_DIM
        causal = lax.broadcasted_iota(jnp.int32, (t, t), 0) >= lax.broadcasted_iota(jnp.int32, (t, t), 1)
        rows = [pl.ds(r * t, t) for r in range(nr)]
        qst = [_stacked_halves(q_ref[rw, :] * jnp.asarray(HEAD_DIM ** -0.5, BF16), lo) for rw in rows]
        q_all = _stack(qst)
        fq = [[jnp.sum(jnp.where(lane == 2 * hp + j, f_ref[rw, :], 0.0), axis=1, keepdims=True) for j in (0, 1)]
              for rw in rows]

        def load(kb):
            k0 = pl.multiple_of(kb * t, t)
            frow = [ft_ref[pl.ds(2 * hp + j, 1), pl.ds(k0, t)] for j in (0, 1)]
            return k_ref[pl.ds(k0, t), :], v_ref[pl.ds(k0, t), :], frow

        def sub(rs, tiles, carries, masked):
            kblk, vblk, frow = tiles
            z = _dot_nt(q_all if len(rs) == nr else _stack([qst[r] for r in rs]), kblk)
            ps, stats = [], []
            for n, (r, j) in enumerate((r, j) for r in range(len(rs)) for j in (0, 1)):
                m, l, _ = carries[r]
                s = z[n * t:(n + 1) * t, :] + (fq[rs[r]][j] - frow[j])
                if masked[r]:
                    s = jnp.where(causal, s, -1e30)
                mj = jnp.maximum(m[j], jnp.max(s, axis=1, keepdims=True))
                aj = jnp.exp(m[j] - mj)
                p = jnp.exp(s - mj)
                stats.append((mj, aj, aj * l[j] + jnp.sum(p, axis=1, keepdims=True)))
                ps.append(p.astype(BF16))
            pv = _dot(_stack(ps), vblk)
            out = []
            for r in range(len(rs)):
                (m0, a0, l0), (m1, a1, l1) = stats[2 * r], stats[2 * r + 1]
                acc = carries[r][2]
                acc = (acc[0] * a0 + pv[2 * r * t:(2 * r + 1) * t, :], acc[1] * a1 + pv[(2 * r + 1) * t:(2 * r + 2) * t, :])
                out.append(((m0, m1), (l0, l1), acc))
            return out, None

        neg = jnp.full((t, 1), -1e30, F32)
        zero = jnp.zeros((t, 1), F32)
        zacc = jnp.zeros((t, LANES), F32)
        init = [((neg, neg), (zero, zero), (zacc, zacc))] * nr
        qkb = [_fox_qk_bound(qs, km_ref, t) for qs in qst]
        trips = _fox_trips(hp, fl_ref, qkb, fq, lambda carries, r, j: carries[r][0][j])
        out = _walk_tiles(i, nr, load, sub, lambda kb, side: None, init, trips=trips)
        for rw, (m, l, acc) in zip(rows, out):
            y_ref[rw, :] = jnp.where(lo, acc[0] / l[0], acc[1] / l[1]).astype(BF16)
            lse_ref[0, rw, :] = jnp.where(lo, m[0] + jnp.log(l[0]), m[1] + jnp.log(l[1]))

    qblk, kvfull = _pair_specs(s_len, t * nr)
    return _ride_call(
        body, name, (N_HEADS // 2, nq),
        [qblk, kvfull, kvfull,
         pl.BlockSpec((t * nr, LANES), lambda hp, i: (i, 0)),
         pl.BlockSpec((N_HEADS, s_len), lambda hp, i: (0, 0)),
         pl.BlockSpec((1, LANES), lambda hp, i: (0, hp)),
         pl.BlockSpec(memory_space=pltpu.SMEM)],
        [qblk, pl.BlockSpec((1, t * nr, LANES), lambda hp, i: (hp, i, 0))],
        [jax.ShapeDtypeStruct((s_len, ATT_W), BF16), jax.ShapeDtypeStruct((N_HEADS // 2, s_len, LANES), F32)],
        [], ("parallel", "parallel"), (q, k, v, f_col, f_row, kmax, f_row[:, t - 1::t]), ride)


def _fox_bwd(q, k, v, dy, y, lse, f_col, f_row, kmax, name, ride=None):
    s_len = q.shape[0]
    t, nr, nq = _att_tiling(s_len, FOX_ROWS)

    def body(q_ref, k_ref, v_ref, dy_ref, y_ref, lse_ref, f_ref, ft_ref, km_ref, fl_ref,
             dq_ref, dk_ref, dv_ref, dft_ref):
        hp = pl.program_id(0)
        i = pl.program_id(1)

        @pl.when(i == 0)
        def _():
            dk_ref[...] = jnp.zeros_like(dk_ref)
            dv_ref[...] = jnp.zeros_like(dv_ref)
            dft_ref[...] = jnp.zeros_like(dft_ref)

        lane = lax.broadcasted_iota(jnp.int32, (t, LANES), 1)
        lo = lane < HEAD_DIM
        causal = lax.broadcasted_iota(jnp.int32, (t, t), 0) >= lax.broadcasted_iota(jnp.int32, (t, t), 1)
        rows = [pl.ds(r * t, t) for r in range(nr)]
        qst, dyst, delta, lse, fq = [], [], [], [], []
        for rw in rows:
            qst.append(_stacked_halves(q_ref[rw, :] * jnp.asarray(HEAD_DIM ** -0.5, BF16), lo))
            dyb = dy_ref[rw, :]
            dyst.append(_stacked_halves(dyb, lo))
            prod = dyb.astype(F32) * y_ref[rw, :].astype(F32)
            delta.append([jnp.sum(jnp.where(lo, prod, 0.0), axis=1, keepdims=True),
                          jnp.sum(jnp.where(lo, 0.0, prod), axis=1, keepdims=True)])
            lse_b = lse_ref[0, rw, :]
            lse.append([lse_b[:, 0:1], lse_b[:, HEAD_DIM:HEAD_DIM + 1]])
            fq.append([jnp.sum(jnp.where(lane == 2 * hp + j, f_ref[rw, :], 0.0), axis=1, keepdims=True)
                       for j in (0, 1)])

        q_all, dy_all = _stack(qst), _stack(dyst)

        def load(kb):
            k0 = pl.multiple_of(kb * t, t)
            frow = [ft_ref[pl.ds(2 * hp + j, 1), pl.ds(k0, t)] for j in (0, 1)]
            return k_ref[pl.ds(k0, t), :], v_ref[pl.ds(k0, t), :], frow

        def sub(rs, tiles, carries, masked):
            kblk, vblk, frow = tiles
            qs, dys = (q_all, dy_all) if len(rs) == nr else (_stack([qst[r] for r in rs]), _stack([dyst[r] for r in rs]))
            z = _dot_nt(qs, kblk)
            dp = _dot_nt(dys, vblk)
            pb, dsb, rsum, col = [], [], [], [None, None]
            for n, (r, j) in enumerate((r, j) for r in range(len(rs)) for j in (0, 1)):
                sl = slice(n * t, (n + 1) * t)
                s = z[sl, :] + (fq[rs[r]][j] - frow[j])
                p = jnp.exp(s - lse[rs[r]][j])
                if masked[r]:
                    p = jnp.where(causal, p, 0.0)
                ds = p * (dp[sl, :] - delta[rs[r]][j])
                c = jnp.sum(ds, axis=0, keepdims=True)
                col[j] = c if col[j] is None else col[j] + c
                rsum.append(carries[r][1][j] + jnp.sum(ds, axis=1, keepdims=True))
                pb.append(p.astype(BF16))
                dsb.append(ds.astype(BF16))
            p_all, ds_all = _stack(pb), _stack(dsb)
            dqs = _dot(ds_all, kblk)
            out = []
            for r in range(len(rs)):
                dq = carries[r][0]
                dq = (dq[0] + dqs[2 * r * t:(2 * r + 1) * t, :], dq[1] + dqs[(2 * r + 1) * t:(2 * r + 2) * t, :])
                out.append((dq, (rsum[2 * r], rsum[2 * r + 1])))
            return out, (_dot_tn(ds_all, qs), _dot_tn(p_all, dys), col)

        def flush(kb, side):
            k0 = pl.multiple_of(kb * t, t)
            dk_ref[pl.ds(k0, t), :] += side[0]
            dv_ref[pl.ds(k0, t), :] += side[1]
            for j in (0, 1):
                dft_ref[0, pl.ds(j, 1), pl.ds(k0, t)] -= side[2][j]

        zero = jnp.zeros((t, 1), F32)
        zacc = jnp.zeros((t, LANES), F32)
        qkb = [_fox_qk_bound(qs, km_ref, t) for qs in qst]
        trips = _fox_trips(hp, fl_ref, qkb, fq, lambda carries, r, j: lse[r][j])
        out = _walk_tiles(i, nr, load, sub, flush, [((zacc, zacc), (zero, zero))] * nr, trips=trips)
        for r, (rw, (dq, rs)) in enumerate(zip(rows, out)):
            dq_ref[rw, :] = jnp.where(lo, dq[0], dq[1]) * (HEAD_DIM ** -0.5)
            rs_t = jnp.where(lo, rs[0], rs[1]).T
            q0 = pl.multiple_of((i * nr + r) * t, t)
            for j in (0, 1):
                dft_ref[0, pl.ds(j, 1), pl.ds(q0, t)] += rs_t[j * HEAD_DIM:j * HEAD_DIM + 1, :]

    qblk, kvfull = _pair_specs(s_len, t * nr)
    return _ride_call(
        body, name, (N_HEADS // 2, nq),
        [qblk, kvfull, kvfull, qblk, qblk,
         pl.BlockSpec((1, t * nr, LANES), lambda hp, i: (hp, i, 0)),
         pl.BlockSpec((t * nr, LANES), lambda hp, i: (i, 0)),
         pl.BlockSpec((N_HEADS, s_len), lambda hp, i: (0, 0)),
         pl.BlockSpec((1, LANES), lambda hp, i: (0, hp)),
         pl.BlockSpec(memory_space=pltpu.SMEM)],
        [qblk, kvfull, kvfull, pl.BlockSpec((1, 8, s_len), lambda hp, i: (hp, 0, 0))],
        [jax.ShapeDtypeStruct((s_len, ATT_W), F32)] * 3 + [jax.ShapeDtypeStruct((N_HEADS // 2, 8, s_len), F32)],
        [], ("arbitrary", "arbitrary"), (q, k, v, dy, y, lse, f_col, f_row, kmax, f_row[:, t - 1::t]), ride)


def _sb_more(carry):
    return (jnp.max(jnp.maximum(carry[0][0], carry[0][1])) > -EXP_ZERO).astype(jnp.int32)


def _stacked_split_dot(slabs, m, parts):
    split = [_split(x, parts) for x in slabs]
    acc = None
    for p in range(parts):
        d = _dot(_stack([s[p] for s in split]), m)
        acc = d if acc is None else acc + d
    return acc


def _sb_weights(z, c, strict, upper, t):
    logs = []
    for n in range(z.shape[0] // t):
        zn = z[n * t:(n + 1) * t, :]
        sp = _softplus_neg_abs(zn)
        l1m = jnp.minimum(-zn, 0.0) - sp
        if strict[n] is not None:
            l1m = jnp.where(strict[n], l1m, 0.0)
        logs.append((jnp.minimum(zn, 0.0) - sp, l1m))
    suf = _stacked_split_dot([l1m for _, l1m in logs], upper, 2)
    out = []
    for n, (logb, l1m) in enumerate(logs):
        after = c[n] + suf[n * t:(n + 1) * t, :]
        a = jnp.exp(logb + after)
        if strict[n] is not None:
            a = jnp.where(strict[n], a, 0.0)
        out.append((logb, a, after[:, 0:1] + l1m[:, 0:1]))
    return out


def _sb_fwd(q, k, v, name):
    s_len = q.shape[0]
    t, nr, nq = _att_tiling(s_len, SB_ROWS)

    def body(q_ref, k_ref, v_ref, y_ref, yf_ref):
        i = pl.program_id(1)
        lane = lax.broadcasted_iota(jnp.int32, (t, LANES), 1)
        lo = lane < HEAD_DIM
        ri = lax.broadcasted_iota(jnp.int32, (t, t), 0)
        ci = lax.broadcasted_iota(jnp.int32, (t, t), 1)
        strict = ci < ri
        upper = (ri > ci).astype(BF16)
        rows = [pl.ds(r * t, t) for r in range(nr)]
        qst = [_stacked_halves(q_ref[rw, :] * jnp.asarray(HEAD_DIM ** -0.5, BF16), lo) for rw in rows]
        q_all = _stack(qst)

        def load(kb):
            k0 = pl.multiple_of(kb * t, t)
            return k_ref[pl.ds(k0, t), :], v_ref[pl.ds(k0, t), :]

        def sub(rs, tiles, carries, masked):
            kblk, vblk = tiles
            z = _dot_nt(q_all if len(rs) == nr else _stack([qst[r] for r in rs]), kblk)
            c = [carries[r][0][j] for r in range(len(rs)) for j in (0, 1)]
            w = _sb_weights(z, c, [strict if masked[r] else None for r in range(len(rs)) for j in (0, 1)], upper, t)
            pv = _dot(_stack([a.astype(BF16) for _, a, _ in w]), vblk)
            out = []
            for r in range(len(rs)):
                acc = carries[r][1]
                acc = (acc[0] + pv[2 * r * t:(2 * r + 1) * t, :], acc[1] + pv[(2 * r + 1) * t:(2 * r + 2) * t, :])
                out.append(((w[2 * r][2], w[2 * r + 1][2]), acc))
            return out, None

        zero = jnp.zeros((t, 1), F32)
        zacc = jnp.zeros((t, LANES), F32)
        out = _walk_tiles(i, nr, load, sub, lambda kb, side: None, [((zero, zero), (zacc, zacc))] * nr, more=_sb_more)
        for rw, (_, acc) in zip(rows, out):
            y = jnp.where(lo, acc[0], acc[1])
            y_ref[rw, :] = y.astype(BF16)
            yf_ref[rw, :] = y

    qblk, kvfull = _pair_specs(s_len, t * nr)
    return pl.pallas_call(
        body, name=name, grid=(N_HEADS // 2, nq),
        in_specs=[qblk, kvfull, kvfull],
        out_specs=[qblk, qblk],
        out_shape=[jax.ShapeDtypeStruct((s_len, ATT_W), BF16), jax.ShapeDtypeStruct((s_len, ATT_W), F32)],
        compiler_params=_cparams("parallel", "parallel"),
    )(q, k, v)


def _sb_bwd(q, k, v, dy, yf, name):
    s_len = q.shape[0]
    t, nr, nq = _att_tiling(s_len, SB_ROWS)

    def body(q_ref, k_ref, v_ref, dy_ref, yf_ref, dq_ref, dk_ref, dv_ref):
        i = pl.program_id(1)

        @pl.when(i == 0)
        def _():
            dk_ref[...] = jnp.zeros_like(dk_ref)
            dv_ref[...] = jnp.zeros_like(dv_ref)

        lane = lax.broadcasted_iota(jnp.int32, (t, LANES), 1)
        lo = lane < HEAD_DIM
        ri = lax.broadcasted_iota(jnp.int32, (t, t), 0)
        ci = lax.broadcasted_iota(jnp.int32, (t, t), 1)
        strict = ci < ri
        upper = (ri > ci).astype(BF16)
        upper_incl = (ri >= ci).astype(BF16)
        rows = [pl.ds(r * t, t) for r in range(nr)]
        qst, dyst, delta = [], [], []
        for rw in rows:
            qst.append(_stacked_halves(q_ref[rw, :] * jnp.asarray(HEAD_DIM ** -0.5, BF16), lo))
            dyb = dy_ref[rw, :]
            dyst.append(_stacked_halves(dyb, lo))
            prod = dyb.astype(F32) * yf_ref[rw, :]
            delta.append([jnp.sum(jnp.where(lo, prod, 0.0), axis=1, keepdims=True),
                          jnp.sum(jnp.where(lo, 0.0, prod), axis=1, keepdims=True)])
        q_all, dy_all = _stack(qst), _stack(dyst)

        def load(kb):
            k0 = pl.multiple_of(kb * t, t)
            return k_ref[pl.ds(k0, t), :], v_ref[pl.ds(k0, t), :]

        def sub(rs, tiles, carries, masked):
            kblk, vblk = tiles
            qs, dys = (q_all, dy_all) if len(rs) == nr else (_stack([qst[r] for r in rs]), _stack([dyst[r] for r in rs]))
            slabs = [(r, j) for r in range(len(rs)) for j in (0, 1)]
            z = _dot_nt(qs, kblk)
            w = _sb_weights(z, [carries[r][0][j] for r, j in slabs], [strict if masked[r] else None for r, j in slabs],
                            upper, t)
            da = _dot_nt(dys, vblk)
            ab = [a.astype(BF16) for _, a, _ in w]
            dl = [ab[n].astype(F32) * da[n * t:(n + 1) * t, :] for n in range(len(slabs))]
            tail = _stacked_split_dot(dl, upper_incl, 2)
            dzb, e_new = [], []
            for n, (r, j) in enumerate(slabs):
                tl = tail[n * t:(n + 1) * t, :]
                e = carries[r][1][j]
                dl1m = (delta[rs[r]][j] - e) - tl
                e_new.append(e + tl[:, 0:1])
                dz = dl[n] - jnp.exp(w[n][0]) * (dl[n] + dl1m)
                if masked[r]:
                    dz = jnp.where(strict, dz, 0.0)
                dzb.append(dz.astype(BF16))
            a_all, dz_all = _stack(ab), _stack(dzb)
            dqs = _dot(dz_all, kblk)
            out = []
            for r in range(len(rs)):
                dq = carries[r][2]
                dq = (dq[0] + dqs[2 * r * t:(2 * r + 1) * t, :], dq[1] + dqs[(2 * r + 1) * t:(2 * r + 2) * t, :])
                out.append(((w[2 * r][2], w[2 * r + 1][2]), (e_new[2 * r], e_new[2 * r + 1]), dq))
            return out, (_dot_tn(dz_all, qs), _dot_tn(a_all, dys))

        def flush(kb, side):
            k0 = pl.multiple_of(kb * t, t)
            dk_ref[pl.ds(k0, t), :] += side[0]
            dv_ref[pl.ds(k0, t), :] += side[1]

        zero = jnp.zeros((t, 1), F32)
        zacc = jnp.zeros((t, LANES), F32)
        out = _walk_tiles(i, nr, load, sub, flush, [((zero, zero), (zero, zero), (zacc, zacc))] * nr, more=_sb_more)
        for rw, (_, _, dq) in zip(rows, out):
            dq_ref[rw, :] = jnp.where(lo, dq[0], dq[1]) * (HEAD_DIM ** -0.5)

    qblk, kvfull = _pair_specs(s_len, t * nr)
    return pl.pallas_call(
        body, name=name, grid=(N_HEADS // 2, nq),
        in_specs=[qblk, kvfull, kvfull, qblk, qblk],
        out_specs=[qblk, kvfull, kvfull],
        out_shape=[jax.ShapeDtypeStruct((s_len, ATT_W), F32)] * 3,
        compiler_params=_cparams("arbitrary", "arbitrary"),
    )(q, k, v, dy, yf)


def _merge_fwd(x, yf, ys, gf, gs, wbf, wbs, wo, name):
    s_len = x.shape[0]
    ts = min(512, s_len)

    def body(x_ref, yf_ref, ys_ref, gf_ref, gs_ref, wbf_ref, wbs_ref, wo_ref, o_ref):
        merged = (_sigmoid(gf_ref[...]) * _dot_nt(yf_ref[...], wbf_ref[...])
                  + _sigmoid(gs_ref[...]) * _dot_nt(ys_ref[...], wbs_ref[...]))
        o_ref[...] = x_ref[...] + _dot(merged.astype(BF16), wo_ref[...])

    tok = lambda w: pl.BlockSpec((ts, w), lambda i: (i, 0))
    full = lambda a: pl.BlockSpec(a.shape, lambda i: (0, 0))
    return pl.pallas_call(
        body, name=name, grid=(s_len // ts,),
        in_specs=[tok(D_MODEL), tok(ATT_W), tok(ATT_W), tok(D_MODEL), tok(D_MODEL), full(wbf), full(wbs), full(wo)],
        out_specs=tok(D_MODEL),
        out_shape=jax.ShapeDtypeStruct((s_len, D_MODEL), F32),
        compiler_params=_cparams("parallel"),
    )(x, yf, ys, gf, gs, wbf, wbs, wo)


def _merge_bwd(dx, yf, ys, gf, gs, wbf, wbs, wo, name):
    s_len = dx.shape[0]
    ts = min(512, s_len)

    def body(dx_ref, yf_ref, ys_ref, gf_ref, gs_ref, wbf_ref, wbs_ref, wo_ref,
             dyf_ref, dys_ref, dgf_ref, dgs_ref, dbf_ref, dbs_ref, mg_ref):
        bf = _dot_nt(yf_ref[...], wbf_ref[...])
        bs = _dot_nt(ys_ref[...], wbs_ref[...])
        sf = _sigmoid(gf_ref[...])
        ss = _sigmoid(gs_ref[...])
        mg_ref[...] = (sf * bf + ss * bs).astype(BF16)
        dm = _dot_nt(dx_ref[...].astype(BF16), wo_ref[...])
        dbf = (dm * sf).astype(BF16)
        dbs = (dm * ss).astype(BF16)
        dbf_ref[...] = dbf
        dbs_ref[...] = dbs
        dgf_ref[...] = (dm * bf * (sf * (1.0 - sf))).astype(BF16)
        dgs_ref[...] = (dm * bs * (ss * (1.0 - ss))).astype(BF16)
        dyf_ref[...] = _dot(dbf, wbf_ref[...]).astype(BF16)
        dys_ref[...] = _dot(dbs, wbs_ref[...]).astype(BF16)

    tok = lambda w: pl.BlockSpec((ts, w), lambda i: (i, 0))
    full = lambda a: pl.BlockSpec(a.shape, lambda i: (0, 0))
    b16o = lambda w: jax.ShapeDtypeStruct((s_len, w), BF16)
    return pl.pallas_call(
        body, name=name, grid=(s_len // ts,),
        in_specs=[tok(D_MODEL), tok(ATT_W), tok(ATT_W), tok(D_MODEL), tok(D_MODEL), full(wbf), full(wbs), full(wo)],
        out_specs=[tok(ATT_W), tok(ATT_W)] + [tok(D_MODEL)] * 5,
        out_shape=[b16o(ATT_W), b16o(ATT_W)] + [b16o(D_MODEL)] * 5,
        compiler_params=_cparams("parallel"),
    )(dx, yf, ys, gf, gs, wbf, wbs, wo)


def _mix_bwd(x, dx_in, gain, w_in, fqr, fkr, dfqn, dfkn, qg, kg, dfv, df_col, logf, dsq, dsk, dsv, dgf, dgs, name):
    s_len = x.shape[0]
    ts = min(256, s_len)
    nt = s_len // ts
    gmat = _head_group_matrix()

    def body(x_ref, dxi_ref, gain_ref, w_ref, fqr_ref, fkr_ref, dfqn_ref, dfkn_ref, qg_ref, kg_ref, gm_ref,
             dfv_ref, df_ref, logf_ref, dsq_ref, dsk_ref, dsv_ref, dgf_ref, dgs_ref,
             dp_ref, dx_ref, dgain_ref, dqg_ref, dkg_ref, dbias_ref, carry):
        i = pl.program_id(0)

        @pl.when(i == 0)
        def _():
            carry[...] = jnp.zeros_like(carry)
            dgain_ref[...] = jnp.zeros_like(dgain_ref)
            dqg_ref[...] = jnp.zeros_like(dqg_ref)
            dkg_ref[...] = jnp.zeros_like(dkg_ref)
            dbias_ref[...] = jnp.zeros_like(dbias_ref)

        gm = gm_ref[...]

        def headnorm_bwd(raw, dout, g, dg_ref):
            ms = _dot_split(raw * raw, gm, HEAD_SUM_PARTS) * (1.0 / HEAD_DIM)
            r = lax.rsqrt(ms + EPS)
            nrm = raw * r
            dg_ref[...] += jnp.sum(dout * nrm, axis=0, keepdims=True)
            dn = dout * g
            mean_h = _dot_split(dn * nrm, gm, HEAD_SUM_PARTS) * (1.0 / HEAD_DIM)
            return r * (dn - nrm * mean_h)

        dp_ref[:, C_FQ:C_FQ + ATT_W] = headnorm_bwd(fqr_ref[...], dfqn_ref[...], qg_ref[...], dqg_ref).astype(BF16)
        dp_ref[:, C_FK:C_FK + ATT_W] = headnorm_bwd(fkr_ref[...], dfkn_ref[...], kg_ref[...], dkg_ref).astype(BF16)
        dp_ref[:, C_FV:C_FV + ATT_W] = dfv_ref[...].astype(BF16)
        dp_ref[:, C_SQ:C_SQ + ATT_W] = dsq_ref[...].astype(BF16)
        dp_ref[:, C_SK:C_SK + ATT_W] = dsk_ref[...].astype(BF16)
        dp_ref[:, C_SV:C_SV + ATT_W] = dsv_ref[...].astype(BF16)
        dp_ref[:, C_GF:C_GF + D_MODEL] = dgf_ref[...]
        dp_ref[:, C_GS:C_GS + D_MODEL] = dgs_ref[...]

        r_ = lax.broadcasted_iota(jnp.int32, (ts, ts), 0)
        c_ = lax.broadcasted_iota(jnp.int32, (ts, ts), 1)
        rev = (c_ >= r_).astype(BF16)
        dlogf = _dot_split_left(rev, df_ref[...], 3) + carry[...]
        carry[...] = dlogf[0:1, :]
        lane = lax.broadcasted_iota(jnp.int32, (ts, LANES), 1)
        dfl = jnp.where(lane < N_HEADS, dlogf * (1.0 - jnp.exp(logf_ref[...])), 0.0)
        dbias_ref[...] += jnp.sum(dfl, axis=0, keepdims=True)
        dp_ref[:, C_FL:C_FL + LANES] = dfl.astype(BF16)
        dp_ref[:, C_FL + LANES:C_SQ] = jnp.zeros((ts, C_SQ - C_FL - LANES), BF16)

        dh = _dot(dp_ref[...], w_ref[...])
        xf = x_ref[...]
        r = _rms_rinv(xf)
        xhat = xf * r
        dgain_ref[...] += jnp.sum(dh * xhat, axis=0, keepdims=True)
        dn = dh * gain_ref[...]
        dx_ref[...] = dxi_ref[...] + r * (dn - xhat * jnp.mean(dn * xhat, axis=-1, keepdims=True))

    tok = lambda w: pl.BlockSpec((ts, w), lambda i: (nt - 1 - i, 0))
    full = lambda a: pl.BlockSpec(a.shape, lambda i: (0, 0))
    row = lambda w: pl.BlockSpec((1, w), lambda i: (0, 0))
    return _ride_call(
        body, name, (nt,),
        [tok(D_MODEL), tok(D_MODEL), full(gain), full(w_in), tok(ATT_W), tok(ATT_W), tok(ATT_W), tok(ATT_W),
         full(qg), full(kg), full(gmat), tok(ATT_W), tok(LANES), tok(LANES), tok(ATT_W), tok(ATT_W), tok(ATT_W),
         tok(D_MODEL), tok(D_MODEL)],
        [tok(IN_PAD), tok(D_MODEL), row(D_MODEL), row(ATT_W), row(ATT_W), row(LANES)],
        [jax.ShapeDtypeStruct((s_len, IN_PAD), BF16), jax.ShapeDtypeStruct((s_len, D_MODEL), F32),
         jax.ShapeDtypeStruct((1, D_MODEL), F32), jax.ShapeDtypeStruct((1, ATT_W), F32),
         jax.ShapeDtypeStruct((1, ATT_W), F32), jax.ShapeDtypeStruct((1, LANES), F32)],
        [pltpu.VMEM((1, LANES), F32)], ("arbitrary",),
        (x, dx_in, gain, w_in, fqr, fkr, dfqn, dfkn, qg, kg, gmat, dfv, df_col, logf, dsq, dsk, dsv, dgf, dgs), None)


def _ple_loss(x, p, tgt, gain, wpg, wpp, name):
    s_len = x.shape[0]
    ts = min(512, s_len)

    def body(x_ref, p_ref, t_ref, gain_ref, wpg_ref, wpp_ref, dx_ref, n_ref, ds_ref, dpp_ref, dgain_ref, loss_ref):
        i = pl.program_id(0)

        @pl.when(i == 0)
        def _():
            dgain_ref[...] = jnp.zeros_like(dgain_ref)
            loss_ref[...] = jnp.zeros_like(loss_ref)

        xf = x_ref[...]
        r = _rms_rinv(xf)
        n = xf * r
        hn = (n * gain_ref[...]).astype(BF16)
        n_ref[...] = hn
        sg = _sigmoid(_dot(hn, wpg_ref[...]))
        pp = _dot_nt(p_ref[...].astype(BF16), wpp_ref[...])
        err = (xf + sg * pp) - t_ref[...]
        sq = jnp.sum(jnp.sum(err * err, axis=1, keepdims=True), axis=0, keepdims=True)
        loss_ref[...] += (0.5 / D_MODEL) * sq
        dout = err * (1.0 / D_MODEL)
        dpp_ref[...] = (dout * sg).astype(BF16)
        ds = (dout * pp * (sg * (1.0 - sg))).astype(BF16)
        ds_ref[...] = ds
        dhn = _dot_nt(ds, wpg_ref[...])
        dgain_ref[...] += jnp.sum(dhn * n, axis=0, keepdims=True)
        dn = dhn * gain_ref[...]
        dx_ref[...] = dout + r * (dn - n * jnp.mean(dn * n, axis=-1, keepdims=True))

    tok = lambda w: pl.BlockSpec((ts, w), lambda i: (i, 0))
    full = lambda a: pl.BlockSpec(a.shape, lambda i: (0, 0))
    return pl.pallas_call(
        body, name=name, grid=(s_len // ts,),
        in_specs=[tok(D_MODEL), tok(PLE_DIM), tok(D_MODEL), full(gain), full(wpg), full(wpp)],
        out_specs=[tok(D_MODEL), tok(D_MODEL), tok(D_MODEL), tok(D_MODEL),
                   pl.BlockSpec((1, D_MODEL), lambda i: (0, 0)), pl.BlockSpec((8, LANES), lambda i: (0, 0))],
        out_shape=[jax.ShapeDtypeStruct((s_len, D_MODEL), F32), jax.ShapeDtypeStruct((s_len, D_MODEL), BF16),
                   jax.ShapeDtypeStruct((s_len, D_MODEL), BF16), jax.ShapeDtypeStruct((s_len, D_MODEL), BF16),
                   jax.ShapeDtypeStruct((1, D_MODEL), F32), jax.ShapeDtypeStruct((8, LANES), F32)],
        compiler_params=_cparams("arbitrary"),
    )(x, p, tgt, gain, wpg, wpp)


def _exchange(x, name, broadcast):
    def body(x_ref, out_ref, send_sems, recv_sems, local_sem):
        _exchange_start(x_ref, out_ref, send_sems, recv_sems, local_sem, broadcast)
        _exchange_wait(x_ref, out_ref, send_sems, recv_sems, local_sem, broadcast)

    return pl.pallas_call(
        body, name=name,
        in_specs=[EXCHANGE_SPEC],
        out_specs=EXCHANGE_SPEC,
        out_shape=_exchange_shape(x, broadcast),
        scratch_shapes=list(EXCHANGE_SEMS),
        compiler_params=pltpu.CompilerParams(has_side_effects=True),
    )(x)


def _gather_two_level(x, name):
    def body(x_ref, out_ref, send_sems, recv_sems, local_sem):
        mx, my, mc = lax.axis_index("x"), lax.axis_index("y"), lax.axis_index("c")
        me, sibling = (mx, my, mc), (mx, my, 1 - mc)
        chips = [(1 - mx, my), (mx, 1 - my), (1 - mx, 1 - my)]

        def slot(px, py, pc):
            return out_ref.at[4 * px + 2 * py + pc]

        def copy(k, block, to, src=None):
            return pltpu.make_async_remote_copy(
                src_ref=slot(*block) if src is None else src, dst_ref=slot(*block),
                send_sem=send_sems.at[k], recv_sem=recv_sems.at[k], device_id=to, device_id_type=MESH)

        mine = pltpu.make_async_copy(x_ref, slot(*me), local_sem)
        mine.start()
        first = [copy(0, me, sibling, src=x_ref)]
        first += [copy(1 + j, me, (*chip, mc), src=x_ref) for j, chip in enumerate(chips)]
        for cp in first:
            cp.start()
        passed = [copy(4 + j, (*chip, mc), sibling) for j, chip in enumerate(chips)]
        for j, chip in enumerate(chips):
            copy(1 + j, (*chip, mc), me).wait_recv()
            passed[j].start()
        copy(0, sibling, me).wait_recv()
        for j, chip in enumerate(chips):
            copy(4 + j, (*chip, 1 - mc), me).wait_recv()
        for cp in first + passed:
            cp.wait_send()
        mine.wait()

    return pl.pallas_call(
        body, name=name,
        in_specs=[EXCHANGE_SPEC],
        out_specs=EXCHANGE_SPEC,
        out_shape=_exchange_shape(x, True),
        scratch_shapes=list(EXCHANGE_SEMS),
        compiler_params=pltpu.CompilerParams(has_side_effects=True),
    )(x)


EXCHANGE_SPEC = pl.BlockSpec(memory_space=pl.ANY)
EXCHANGE_SEMS = (pltpu.SemaphoreType.DMA((N_DEV - 1,)), pltpu.SemaphoreType.DMA((N_DEV - 1,)), pltpu.SemaphoreType.DMA)


def _exchange_shape(x, broadcast):
    return jax.ShapeDtypeStruct((N_DEV,) + tuple(x.shape if broadcast else x.shape[1:]), x.dtype)


def _exchange_copies(x_ref, out_ref, send_sems, recv_sems, local_sem, broadcast, with_recv=True):
    mx, my, mc = lax.axis_index("x"), lax.axis_index("y"), lax.axis_index("c")
    me = 4 * mx + 2 * my + mc

    def src(idx):
        return x_ref if broadcast else x_ref.at[idx]

    local = pltpu.make_async_copy(src(me), out_ref.at[me], local_sem)
    pairs = []
    for k in range(1, N_DEV):
        px = (1 - mx) if k & 4 else mx
        py = (1 - my) if k & 2 else my
        pc = (1 - mc) if k & 1 else mc
        peer = 4 * px + 2 * py + pc
        sems = dict(send_sem=send_sems.at[k - 1], recv_sem=recv_sems.at[k - 1], device_id=(px, py, pc), device_id_type=MESH)
        recv = pltpu.make_async_remote_copy(src_ref=src(peer), dst_ref=out_ref.at[peer], **sems) if with_recv else None
        pairs.append((pltpu.make_async_remote_copy(src_ref=src(peer), dst_ref=out_ref.at[me], **sems), recv))
    return local, pairs


def _exchange_start(*refs_and_mode):
    local, pairs = _exchange_copies(*refs_and_mode, with_recv=False)
    local.start()
    for send, _ in pairs:
        send.start()


def _exchange_wait(*refs_and_mode):
    local, pairs = _exchange_copies(*refs_and_mode)
    for _, recv in pairs:
        recv.wait_recv()
    for send, _ in pairs:
        send.wait_send()
    local.wait()


def _riding(body, grid, n_in, n_out, ride):
    if ride is None:
        return body
    broadcast = ride[1]

    def wrapped(*refs):
        ins, x_ref = refs[:n_in], refs[n_in]
        outs, out_ref = refs[n_in + 1:n_in + 1 + n_out], refs[n_in + 1 + n_out]
        scratch, sems = refs[n_in + 2 + n_out:-3], refs[-3:]
        step = pl.program_id(0)
        for d in range(1, len(grid)):
            step = step * grid[d] + pl.program_id(d)
        total = 1
        for g in grid:
            total *= g

        @pl.when(step == 0)
        def _():
            _exchange_start(x_ref, out_ref, *sems, broadcast)

        body(*ins, *outs, *scratch)

        @pl.when(step == total - 1)
        def _():
            _exchange_wait(x_ref, out_ref, *sems, broadcast)

    return wrapped


def _ride_call(body, name, grid, in_specs, out_specs, out_shape, scratch_shapes, sem, operands, ride):
    if ride is None:
        return pl.pallas_call(body, name=name, grid=grid, in_specs=in_specs, out_specs=out_specs, out_shape=out_shape,
                              scratch_shapes=scratch_shapes, compiler_params=_cparams(*sem))(*operands)
    return pl.pallas_call(
        _riding(body, grid, len(in_specs), len(out_specs), ride), name=name, grid=grid,
        in_specs=list(in_specs) + [EXCHANGE_SPEC], out_specs=list(out_specs) + [EXCHANGE_SPEC],
        out_shape=list(out_shape) + [_exchange_shape(*ride)],
        scratch_shapes=list(scratch_shapes) + list(EXCHANGE_SEMS),
        compiler_params=_cparams(*(["arbitrary"] * len(grid))),
    )(*operands, ride[0])


def _adamw_math(w, g, m, v):
    m2 = ADAM_B1 * m + (1.0 - ADAM_B1) * g
    v2 = ADAM_B2 * v + (1.0 - ADAM_B2) * (g * g)
    m_hat = m2 / (1.0 - ADAM_B1 ** ADAM_STEP)
    v_hat = v2 / (1.0 - ADAM_B2 ** ADAM_STEP)
    delta = -ADAM_LR * (m_hat / (jnp.sqrt(v_hat) + ADAM_EPS) + ADAM_WD * w)
    return delta, m2, v2


def _sum_parts(parts, name, tr):
    _, rows, cols = parts.shape

    def body(p_ref, g_ref):
        g = p_ref[0].astype(F32)
        for s in range(1, N_DEV):
            g = g + p_ref[s].astype(F32)
        g_ref[...] = g

    return pl.pallas_call(
        body, name=name, grid=(rows // tr,),
        in_specs=[pl.BlockSpec((N_DEV, tr, cols), lambda i: (0, i, 0))],
        out_specs=pl.BlockSpec((tr, cols), lambda i: (i, 0)),
        out_shape=jax.ShapeDtypeStruct((rows, cols), F32),
        compiler_params=_cparams("parallel"),
    )(parts)


ADAM_SPLIT_ELEMS = 400_000


def _adamw_shard(g, w, m, v, name):
    rows, cols = w.shape
    tr = rows // 2 if rows * cols > ADAM_SPLIT_ELEMS else rows

    def body(g_ref, w_ref, m_ref, v_ref, d_ref, m2_ref, v2_ref):
        d_ref[...], m2_ref[...], v2_ref[...] = _adamw_math(w_ref[...], g_ref[...], m_ref[...], v_ref[...])

    blk = pl.BlockSpec((tr, cols), lambda i: (i, 0))
    return pl.pallas_call(
        body, name=name, grid=(rows // tr,),
        in_specs=[blk] * 4, out_specs=[blk] * 3,
        out_shape=[jax.ShapeDtypeStruct((rows, cols), F32)] * 3,
        compiler_params=_cparams("parallel"),
    )(g, w, m, v)


def _adamw_small(parts, w, m, v, name):
    names = list(SMALL_NAMES)

    def body(p_ref, *refs):
        ins, outs = refs[:3 * len(names)], refs[3 * len(names):]
        total = p_ref[0]
        for s in range(1, N_DEV):
            total = total + p_ref[s]
        for i, n in enumerate(names):
            row, off, width = SMALL_POS[n]
            g = total[row:row + 1, off:off + width]
            w_ref, m_ref, v_ref = ins[3 * i:3 * i + 3]
            g_ref, d_ref, m2_ref, v2_ref = outs[4 * i:4 * i + 4]
            g_ref[...] = g
            d_ref[...], m2_ref[...], v2_ref[...] = _adamw_math(w_ref[...], g, m_ref[...], v_ref[...])
        row, off, _ = SMALL_POS["loss"]
        outs[-1][...] = total[row:row + 1, off:off + 1]

    operands = [parts] + [t[n] for n in names for t in (w, m, v)]
    shapes = [jax.ShapeDtypeStruct(w[n].shape, F32) for n in names for _ in range(4)]
    shapes.append(jax.ShapeDtypeStruct((1, 1), F32))
    out = pl.pallas_call(body, name=name, out_shape=shapes)(*operands)
    return {n: tuple(out[4 * i:4 * i + 4]) for i, n in enumerate(names)}, out[-1]


TRANSPOSED = frozenset(("ffn1_w_gate", "ffn1_w_up", "w_in", "w_branch_fox", "w_branch_sb", "ffn2_w_gate", "ffn2_w_up",
                        "w_ple_proj"))
F_PAD_ROWS = C_SQ - FL_REAL_END


def _pack(pieces, group, dtype):
    out = []
    for name in GATHER_GROUPS[group]:
        r = pieces[name].T if name in TRANSPOSED else pieces[name]
        r = r.reshape(-1, D_MODEL).astype(dtype)
        if r.shape[0] != PACK_ROWS[name]:
            r = jnp.pad(r, ((0, PACK_ROWS[name] - r.shape[0]), (0, 0)))
        out.append(r)
    return jnp.concatenate(out, axis=0)


def _real_rows(name):
    return W_IN_ROWS if name == "w_in" else PACK_ROWS[name]


def _gathered(got, name, shape):
    off = GATHER_OFF[name]
    return got[:, off:off + _real_rows(name), :].reshape(shape)


def _w_in_device_rows(d):
    lo, hi = d * W_IN_ROWS, (d + 1) * W_IN_ROWS
    if hi <= FL_REAL_END:
        return [(lo, hi)]
    if lo >= FL_REAL_END:
        return [(lo + F_PAD_ROWS, hi + F_PAD_ROWS)]
    return [(lo, FL_REAL_END), (C_SQ, hi + F_PAD_ROWS)]


def _w_in_t_padded(got):
    t = _gathered(got, "w_in", (IN_REAL, D_MODEL))
    return jnp.concatenate([t[:FL_REAL_END], jnp.zeros((F_PAD_ROWS, D_MODEL), t.dtype), t[FL_REAL_END:]], axis=0)


def _pack_chunks(grads, group):
    out = []
    for name in SCATTER_GROUPS[group]:
        base, _, half = name.partition("#")
        g = grads[base].astype(BF16)
        if base == "w_in":
            lo, hi = W_IN_HALVES[int(half)]
            tail = jnp.zeros((PACK_ROWS[base] - W_IN_ROWS, D_MODEL), BF16)
            c = jnp.stack([jnp.concatenate([g[a:b] for a, b in _w_in_device_rows(d)] + [tail], axis=0)[lo:hi]
                           for d in range(N_DEV)])
        else:
            c = g.reshape(N_DEV, PACK_ROWS[name], D_MODEL)
        out.append(c)
    return out[0] if len(out) == 1 else jnp.concatenate(out, axis=1)


def _shard_grad(summed, name, shape):
    if name == "w_in":
        rows = jnp.concatenate([summed[f"w_in#{i}"] for i in range(len(W_IN_HALVES))], axis=0)[:W_IN_ROWS]
    else:
        rows = summed[name][SCATTER_OFF[name]:SCATTER_OFF[name] + PACK_ROWS[name], :]
    return rows.reshape(shape[1], shape[0]).T if name in TRANSPOSED else rows.reshape(shape)


WEIGHT_NAMES = ['ffn1_norm', 'ffn1_w_gate', 'ffn1_w_up', 'ffn1_w_down', 'mix_norm', 'w_in', 'forget_bias', 'q_norm',
                'k_norm', 'w_branch_fox', 'w_branch_sb', 'w_out', 'ffn2_norm', 'ffn2_w_gate', 'ffn2_w_up',
                'ffn2_w_down', 'ple_norm', 'w_ple_gate', 'w_ple_proj']
SMALL_NAMES = ('ffn1_norm', 'mix_norm', 'ffn2_norm', 'ple_norm', 'q_norm', 'k_norm', 'forget_bias')
SMALL_POS = {'ffn1_norm': (0, 0, D_MODEL), 'mix_norm': (1, 0, D_MODEL), 'ffn2_norm': (2, 0, D_MODEL),
             'ple_norm': (3, 0, D_MODEL), 'q_norm': (4, 0, HEAD_DIM), 'k_norm': (4, HEAD_DIM, HEAD_DIM),
             'forget_bias': (4, 2 * HEAD_DIM, N_HEADS), 'loss': (4, 2 * HEAD_DIM + N_HEADS, 1)}


def _pack_small(vals, loss):
    tail = [vals[n].reshape(1, -1) for n in ('q_norm', 'k_norm', 'forget_bias')] + [loss.reshape(1, 1)]
    tail.append(jnp.zeros((1, D_MODEL - sum(t.shape[1] for t in tail)), F32))
    rows = [vals[n].reshape(1, D_MODEL) for n in SMALL_NAMES[:4]] + [jnp.concatenate(tail, axis=1)]
    rows.append(jnp.zeros((SMALL_ROWS - len(rows), D_MODEL), F32))
    return jnp.concatenate(rows, axis=0)


def _step(x, p, tgt, w):
    row = lambda a: a.reshape(1, -1).astype(F32)
    g_ffn1, g_mix, g_ffn2, g_ple = (row(w[n]) for n in SMALL_NAMES[:4])
    qg = jnp.tile(row(w['q_norm']), (1, N_HEADS))
    kg = jnp.tile(row(w['k_norm']), (1, N_HEADS))
    bias = jnp.pad(row(w['forget_bias']), ((0, 0), (0, LANES - N_HEADS)))
    half = D_FF // 2
    grads = {}

    blk = lambda n: GATHER_OFF[n] // FFN_SHARD
    ffn1 = tuple(blk(n) for n in ("ffn1_w_gate", "ffn1_w_up", "ffn1_w_down"))
    ffn2 = tuple(blk(n) for n in ("ffn2_w_gate", "ffn2_w_up", "ffn2_w_down"))
    got0 = _gather_two_level(_pack(w, 0, BF16), "gather_ffn1")
    x1, g1, u1, h1, got1 = _ffn_fwd(x, g_ffn1, got0, ffn1, "ffn1_fwd", ride=(_pack(w, 1, BF16), True))
    w_in = _w_in_t_padded(got1)
    wbf = _gathered(got1, "w_branch_fox", (D_MODEL, ATT_W))
    wbs = _gathered(got1, "w_branch_sb", (D_MODEL, ATT_W))
    wo = _gathered(got1, "w_out", (D_MODEL, D_MODEL))
    (hmix, fqr, fkr, fqn, fkn, fv, logf, f_col, f_row, sq, sk, sv, gf, gs, kmax) = _mix_fwd(
        x1, g_mix, w_in, bias, qg, kg, "mix_fwd")
    y_fox, lse, got2 = _fox_fwd(fqn, fkn, fv, f_col, f_row, kmax, "fox_fwd", ride=(_pack(w, 2, BF16), True))
    wpg = _gathered(got2, "w_ple_gate", (D_MODEL, D_MODEL))
    wpp = _gathered(got2, "w_ple_proj", (D_MODEL, PLE_DIM))
    y_sb, y_sb32 = _sb_fwd(sq, sk, sv, "sb_fwd")
    x2 = _merge_fwd(x1, y_fox, y_sb, gf, gs, wbf, wbs, wo, "merge_fwd")
    x3, g2, u2, h2, = _ffn_fwd(x2, g_ffn2, got2, ffn2, "ffn2_fwd")
    dx3, n_ple, ds_ple, dpp, dg_ple, loss = _ple_loss(x3, p, tgt, g_ple, wpg, wpp, "ple_loss")

    grads['w_ple_gate'] = _wgrad(n_ple, ds_ple, "dw_ple_gate", D_MODEL, D_MODEL)
    grads['w_ple_proj'] = _wgrad(dpp, p, "dw_ple_proj", D_MODEL, PLE_DIM)
    dg2, du2, act2, dx2, dg_ffn2 = _ffn_bwd_fused(x2, dx3, g_ffn2, g2, u2, got2, ffn2, "ffn2_bwd")
    grads['ffn2_w_gate'] = _wgrad(dg2, h2, "dw_ffn2_gate", half, D_MODEL)
    grads['ffn2_w_up'] = _wgrad(du2, h2, "dw_ffn2_up", half, D_MODEL)
    grads['ffn2_w_down'] = _wgrad(act2, dx3, "dw_ffn2_down", half, D_MODEL)
    dyf, dys, dgf, dgs, dbf, dbs, merged = _merge_bwd(dx2, y_fox, y_sb, gf, gs, wbf, wbs, wo, "merge_bwd")
    grads['w_branch_fox'] = _wgrad(dbf, y_fox, "dw_branch_fox", D_MODEL, ATT_W)
    grads['w_branch_sb'] = _wgrad(dbs, y_sb, "dw_branch_sb", D_MODEL, ATT_W)
    grads['w_out'] = _wgrad(merged, dx2, "dw_out", D_MODEL, D_MODEL)
    dfqn, dfkn, dfv, dft, part_rest = _fox_bwd(fqn, fkn, fv, dyf, y_fox, lse, f_col, f_row, kmax, "fox_bwd",
                                               ride=(_pack_chunks(grads, 5), False))
    dsq, dsk, dsv = _sb_bwd(sq, sk, sv, dys, y_sb32, "sb_bwd")
    s_len = x.shape[0]
    df_col = jnp.pad(dft[:, :2, :].reshape(N_HEADS, s_len).T, ((0, 0), (0, LANES - N_HEADS)))
    dproj, dx1, dg_mix, dqg, dkg, dbias = _mix_bwd(
        x1, dx2, g_mix, w_in, fqr, fkr, dfqn, dfkn, qg, kg, dfv, df_col, logf, dsq, dsk, dsv, dgf, dgs, "mix_bwd")
    grads['w_in'] = _wgrad(dproj, hmix, "dw_in", IN_PAD // 3, D_MODEL)
    dg1, du1, act1, dx0, dg_ffn1, part_in0 = _ffn_bwd_fused(x, dx1, g_ffn1, g1, u1, got0, ffn1, "ffn1_bwd",
                                                            ride=(_pack_chunks(grads, 3), False))
    grads['ffn1_w_gate'], part_in1 = _wgrad(dg1, h1, "dw_ffn1_gate", half, D_MODEL,
                                            ride=(_pack_chunks(grads, 4), False))
    grads['ffn1_w_up'], part_gate = _wgrad(du1, h1, "dw_ffn1_up", half, D_MODEL, ride=(_pack_chunks(grads, 0), False))
    grads['ffn1_w_down'], part_up = _wgrad(act1, dx1, "dw_ffn1_down", half, D_MODEL,
                                           ride=(_pack_chunks(grads, 1), False))
    part_down = _exchange(_pack_chunks(grads, 2), "scatter_ffn1_down", False)

    fold = lambda a: a.reshape(N_HEADS, HEAD_DIM).sum(axis=0).reshape(1, HEAD_DIM)
    small_g = {'ffn1_norm': dg_ffn1, 'mix_norm': dg_mix, 'ffn2_norm': dg_ffn2, 'ple_norm': dg_ple,
               'q_norm': fold(dqg), 'k_norm': fold(dkg), 'forget_bias': dbias[:, :N_HEADS]}
    return loss[0, 0], dx0, (part_gate, part_up, part_down, part_in0, part_in1, part_rest), small_g


def kernel(x, p, ffn1_norm, ffn1_w_gate, ffn1_w_up, ffn1_w_down, mix_norm, w_in, forget_bias, q_norm, k_norm, w_branch_fox, w_branch_sb, w_out, ffn2_norm, ffn2_w_gate, ffn2_w_up, ffn2_w_down, ple_norm, w_ple_gate, w_ple_proj, loss_target, m_ffn1_norm, m_ffn1_w_gate, m_ffn1_w_up, m_ffn1_w_down, m_mix_norm, m_w_in, m_forget_bias, m_q_norm, m_k_norm, m_w_branch_fox, m_w_branch_sb, m_w_out, m_ffn2_norm, m_ffn2_w_gate, m_ffn2_w_up, m_ffn2_w_down, m_ple_norm, m_w_ple_gate, m_w_ple_proj, v_ffn1_norm, v_ffn1_w_gate, v_ffn1_w_up, v_ffn1_w_down, v_mix_norm, v_w_in, v_forget_bias, v_q_norm, v_k_norm, v_w_branch_fox, v_w_branch_sb, v_w_out, v_ffn2_norm, v_ffn2_w_gate, v_ffn2_w_up, v_ffn2_w_down, v_ple_norm, v_w_ple_gate, v_w_ple_proj):
    args = dict(locals())
    w = {n: args[n][0] for n in WEIGHT_NAMES}
    m = {n: args["m_" + n][0] for n in WEIGHT_NAMES}
    v = {n: args["v_" + n][0] for n in WEIGHT_NAMES}
    loss, dx, parts, small_g = _step(x[0], p[0, 0], loss_target[0], w)

    summed = {}
    for grp, part in enumerate(parts):
        s = _sum_parts(part, f"sum_grads_{grp}", SUM_TILE_ROWS[grp])
        summed.update({n: s for n in SCATTER_GROUPS[grp]})
    big = {}
    for n in WEIGHT_NAMES:
        if n not in SMALL_NAMES:
            g = _shard_grad(summed, n, w[n].shape)
            big[n] = (g,) + tuple(_adamw_shard(g, w[n], m[n], v[n], "adamw_" + n))
    small_parts = _exchange(_pack_small(small_g, loss), "gather_small", True)
    small, total_loss = _adamw_small(small_parts, *({n: args[pre + n] for n in SMALL_NAMES} for pre in ("", "m_", "v_")),
                                     "adamw_small")
    big.update(small)

    outs = [total_loss.reshape(()), dx.reshape(x.shape)]
    for kind in range(4):
        outs += [big[n][kind].reshape(args[n].shape) for n in WEIGHT_NAMES]
    return tuple(outs)
```

```python
import jax
import jax.numpy as jnp
from jax import lax
from jax.experimental import pallas as pl
from jax.experimental.pallas import tpu as pltpu

F32 = jnp.float32
BF16 = jnp.bfloat16

D_MODEL = 1024
D_FF = 2816
N_HEADS = 8
HEAD_DIM = 64
ATT_W = N_HEADS * HEAD_DIM
PLE_DIM = 256
EPS = 1e-6
N_DEV = 8
MESH = pl.DeviceIdType.MESH

LANES = 128
V7X_SCOPED_VMEM_BYTES = 56 * 1024 * 1024

C_FQ, C_FK, C_FV, C_FL = 0, 512, 1024, 1536
C_SQ, C_SK, C_SV, C_GF, C_GS = 1792, 2304, 2816, 3328, 4352
IN_PAD = 5376
IN_REAL = 5128
FL_REAL_END = 1544

ADAM_LR = 0.001
ADAM_B1 = 0.9
ADAM_B2 = 0.999
ADAM_EPS = 1e-08
ADAM_WD = 0.01
ADAM_STEP = 10

PACK_ROWS = {"ffn1_w_gate": 352, "ffn1_w_up": 352, "ffn1_w_down": 352, "w_in": 656, "w_branch_fox": 64,
             "w_branch_sb": 64, "w_out": 128, "ffn2_w_gate": 352, "ffn2_w_up": 352, "ffn2_w_down": 352,
             "w_ple_gate": 128, "w_ple_proj": 32}
GATHER_GROUPS = (
    ("ffn1_w_gate", "ffn1_w_up", "ffn1_w_down"),
    ("w_in", "w_branch_fox", "w_branch_sb", "w_out"),
    ("ffn2_w_gate", "ffn2_w_up", "ffn2_w_down", "w_ple_gate", "w_ple_proj"),
)
W_IN_HALVES = ((0, 336), (336, 656))
PACK_ROWS.update({f"w_in#{i}": hi - lo for i, (lo, hi) in enumerate(W_IN_HALVES)})
SCATTER_GROUPS = (
    ("ffn1_w_gate",), ("ffn1_w_up",), ("ffn1_w_down",),
    ("w_in#0",), ("w_in#1",),
    ("ffn2_w_gate", "ffn2_w_up", "ffn2_w_down", "w_ple_gate", "w_ple_proj", "w_branch_fox", "w_branch_sb", "w_out"),
)
SUM_TILE_ROWS = (352, 352, 352, 336, 320, 368)


def _offsets(groups):
    off = {}
    for grp in groups:
        o = 0
        for n in grp:
            off[n] = o
            o += PACK_ROWS[n]
    return off


GATHER_OFF = _offsets(GATHER_GROUPS)
SCATTER_OFF = _offsets(SCATTER_GROUPS)
W_IN_ROWS = 641

SMALL_ROWS = 8


def _cparams(*sem):
    return pltpu.CompilerParams(dimension_semantics=sem, vmem_limit_bytes=V7X_SCOPED_VMEM_BYTES)


def _dot(a, b):
    return jnp.dot(a, b, preferred_element_type=F32)


def _dot_nt(a, b):
    return lax.dot_general(a, b, (((1,), (1,)), ((), ())), preferred_element_type=F32)


def _dot_tn(a, b):
    return lax.dot_general(a, b, (((0,), (0,)), ((), ())), preferred_element_type=F32)


def _split(x, parts):
    out = []
    r = x
    for _ in range(parts):
        p = r.astype(BF16)
        out.append(p)
        r = r - p.astype(F32)
    return out


def _dot_split(x, m, parts):
    acc = None
    for p in _split(x, parts):
        t = _dot(p, m)
        acc = t if acc is None else acc + t
    return acc


def _dot_split_left(m, x, parts):
    acc = None
    for p in _split(x, parts):
        t = _dot(m, p)
        acc = t if acc is None else acc + t
    return acc


def _rms_rinv(xf):
    return lax.rsqrt(jnp.mean(xf * xf, axis=-1, keepdims=True) + EPS)


def _sigmoid(x):
    return 1.0 / (1.0 + jnp.exp(-x))


def _softplus_neg_abs(z):
    return jnp.log(1.0 + jnp.exp(-jnp.abs(z)))


FFN_SHARD = D_FF // N_DEV
FFN_CHUNK = 4


def _ffn_w_spec(blk, index_map):
    return pl.BlockSpec((FFN_CHUNK, FFN_SHARD, D_MODEL), lambda *g: (index_map(*g), blk, 0))


def _ffn_w(ref):
    return ref[...].reshape(FFN_CHUNK * FFN_SHARD, D_MODEL)


def _ffn_fwd(x, gain, wbuf, blks, name, ride=None):
    s_len = x.shape[0]
    ts = min(512, s_len)
    fc = FFN_CHUNK * FFN_SHARD
    nt, nc = s_len // ts, D_FF // fc

    def body(x_ref, gain_ref, wg_ref, wu_ref, wd_ref, y_ref, g_ref, u_ref, h_ref, acc_scr):
        j = pl.program_id(1)

        @pl.when(j == 0)
        def _():
            xf = x_ref[...]
            h_ref[...] = ((xf * _rms_rinv(xf)) * gain_ref[...]).astype(BF16)
            acc_scr[...] = jnp.zeros_like(acc_scr)

        h = h_ref[...]
        g = _dot_nt(h, _ffn_w(wg_ref))
        u = _dot_nt(h, _ffn_w(wu_ref))
        g_ref[...] = g.astype(BF16)
        u_ref[...] = u.astype(BF16)
        a = (g * _sigmoid(g) * u).astype(BF16)
        acc_scr[...] += _dot(a, _ffn_w(wd_ref))

        @pl.when(j == nc - 1)
        def _():
            y_ref[...] = x_ref[...] + 0.5 * acc_scr[...]

    tok = pl.BlockSpec((ts, D_MODEL), lambda i, j: (i, 0))
    hid = pl.BlockSpec((ts, fc), lambda i, j: (i, j))
    return _ride_call(
        body, name, (nt, nc),
        [tok, pl.BlockSpec((1, D_MODEL), lambda i, j: (0, 0))] + [_ffn_w_spec(b, lambda i, j: j) for b in blks],
        [tok, hid, hid, tok],
        [jax.ShapeDtypeStruct((s_len, D_MODEL), F32), jax.ShapeDtypeStruct((s_len, D_FF), BF16),
         jax.ShapeDtypeStruct((s_len, D_FF), BF16), jax.ShapeDtypeStruct((s_len, D_MODEL), BF16)],
        [pltpu.VMEM((ts, D_MODEL), F32)], ("parallel", "arbitrary"), (x, gain, wbuf, wbuf, wbuf), ride)


def _ffn_bwd_fused(x, dy, gain, g, u, wbuf, blks, name, ride=None):
    s_len = x.shape[0]
    ts = min(512, s_len)
    fc = FFN_CHUNK * FFN_SHARD
    nt = s_len // ts
    assert D_FF == 2 * fc

    def hidden(dy_ref, g_ref, u_ref, wg_ref, wu_ref, wd_ref, dg_ref, du_ref, act_ref):
        da = 0.5 * _dot_nt(dy_ref[...].astype(BF16), _ffn_w(wd_ref))
        gf = g_ref[...].astype(F32)
        uf = u_ref[...].astype(F32)
        sg = _sigmoid(gf)
        silu = gf * sg
        dg = (da * uf * (sg * (1.0 + gf * (1.0 - sg)))).astype(BF16)
        du = (da * silu).astype(BF16)
        dg_ref[...] = dg
        du_ref[...] = du
        act_ref[...] = (0.5 * silu * uf).astype(BF16)
        return _dot(dg, _ffn_w(wg_ref)) + _dot(du, _ffn_w(wu_ref))

    def first(dy_ref, g_ref, u_ref, wg_ref, wu_ref, wd_ref, dg_ref, du_ref, act_ref, dh_ref):
        dh_ref[...] = hidden(dy_ref, g_ref, u_ref, wg_ref, wu_ref, wd_ref, dg_ref, du_ref, act_ref)

    def second(x_ref, dy_ref, gain_ref, g_ref, u_ref, wg_ref, wu_ref, wd_ref, dh0_ref, dg_half, du_half, act_half,
               dg_ref, du_ref, act_ref, dx_ref, dgain_ref):
        i = pl.program_id(0)
        dh = dh0_ref[...] + hidden(dy_ref, g_ref, u_ref, wg_ref, wu_ref, wd_ref, dg_ref, du_ref, act_ref)
        xf = x_ref[...]
        r = _rms_rinv(xf)
        xhat = xf * r
        dgp = jnp.sum(dh * xhat, axis=0, keepdims=True)

        @pl.when(i == 0)
        def _():
            dgain_ref[...] = dgp

        @pl.when(i > 0)
        def _():
            dgain_ref[...] += dgp

        dn = dh * gain_ref[...]
        dx_ref[...] = dy_ref[...] + r * (dn - xhat * jnp.mean(dn * xhat, axis=-1, keepdims=True))

    tok = pl.BlockSpec((ts, D_MODEL), lambda i: (i, 0))
    row = pl.BlockSpec((1, D_MODEL), lambda i: (0, 0))
    hid = lambda c: pl.BlockSpec((ts, fc), lambda i: (i, c))
    wts = lambda c: [pl.BlockSpec((FFN_CHUNK, FFN_SHARD, D_MODEL), lambda i, b=b: (c, b, 0),
                                  pipeline_mode=pl.Buffered(1)) for b in blks]
    hidden_shapes = [jax.ShapeDtypeStruct((s_len, D_FF), BF16)] * 3
    dg, du, act, dh0, *rode = _ride_call(
        first, name + "_a", (nt,), [tok, hid(0), hid(0)] + wts(0), [hid(0)] * 3 + [tok],
        hidden_shapes + [jax.ShapeDtypeStruct((s_len, D_MODEL), F32)], [], ("parallel",),
        (dy, g, u, wbuf, wbuf, wbuf), ride)
    filled = pl.BlockSpec(memory_space=pl.ANY)
    dg, du, act, dx, dgain = pl.pallas_call(
        second, name=name + "_b", grid=(nt,),
        in_specs=[tok, tok, row, hid(1), hid(1)] + wts(1) + [tok, filled, filled, filled],
        out_specs=[hid(1)] * 3 + [tok, row],
        out_shape=hidden_shapes + [jax.ShapeDtypeStruct((s_len, D_MODEL), F32), jax.ShapeDtypeStruct((1, D_MODEL), F32)],
        input_output_aliases={9: 0, 10: 1, 11: 2},
        compiler_params=_cparams("arbitrary"),
    )(x, dy, gain, g, u, wbuf, wbuf, wbuf, dh0, dg, du, act)
    return (dg, du, act, dx, dgain, *rode)


def _wgrad(a, b, name, tk, tn, ride=None):
    s_len, k_dim = a.shape
    n_dim = b.shape[1]
    ts = min(2048, s_len)
    ns = s_len // ts

    def body(a_ref, b_ref, o_ref, acc):
        s = pl.program_id(2)
        p = _dot_tn(a_ref[...].astype(BF16), b_ref[...].astype(BF16))

        @pl.when(s == 0)
        def _():
            acc[...] = p

        @pl.when(s > 0)
        def _():
            acc[...] += p

        @pl.when(s == ns - 1)
        def _():
            o_ref[...] = acc[...].astype(BF16)

    out = _ride_call(
        body, name, (k_dim // tk, n_dim // tn, ns),
        [pl.BlockSpec((ts, tk), lambda k, n, s: (s, k)), pl.BlockSpec((ts, tn), lambda k, n, s: (s, n))],
        [pl.BlockSpec((tk, tn), lambda k, n, s: (k, n))], [jax.ShapeDtypeStruct((k_dim, n_dim), BF16)],
        [pltpu.VMEM((tk, tn), F32)], ("parallel", "parallel", "arbitrary"), (a, b), ride)
    return out[0] if ride is None else tuple(out)


HEAD_SUM_PARTS = 2


def _head_group_matrix():
    r = lax.broadcasted_iota(jnp.int32, (ATT_W, ATT_W), 0) // HEAD_DIM
    c = lax.broadcasted_iota(jnp.int32, (ATT_W, ATT_W), 1) // HEAD_DIM
    return (r == c).astype(BF16)


def _mix_fwd(x, gain, w_in, bias, qg, kg, name):
    s_len = x.shape[0]
    ts = min(512, s_len)
    nt = s_len // ts
    gmat = _head_group_matrix()

    def body(x_ref, gain_ref, w_ref, bias_ref, qg_ref, kg_ref, gm_ref,
             h_ref, fqr_ref, fkr_ref, fqn_ref, fkn_ref, fv_ref, logf_ref, f_ref, ft_ref,
             sq_ref, sk_ref, sv_ref, gf_ref, gs_ref, kmax_ref, carry):
        i = pl.program_id(0)
        xf = x_ref[...]
        h = ((xf * _rms_rinv(xf)) * gain_ref[...]).astype(BF16)
        h_ref[...] = h
        gm = gm_ref[...]

        def proj(lo, n):
            return _dot_nt(h, w_ref[lo:lo + n, :])

        def headnorm(raw, g):
            ms = _dot_split(raw * raw, gm, HEAD_SUM_PARTS) * (1.0 / HEAD_DIM)
            return ((raw * lax.rsqrt(ms + EPS)) * g).astype(BF16)

        fq = proj(C_FQ, ATT_W)
        fqr_ref[...] = fq
        fqn_ref[...] = headnorm(fq, qg_ref[...])
        fk = proj(C_FK, ATT_W)
        fkr_ref[...] = fk
        fkn = headnorm(fk, kg_ref[...])
        fkn_ref[...] = fkn
        kn2 = jnp.max(_dot_split(jnp.square(fkn.astype(F32)), gm, HEAD_SUM_PARTS), axis=0, keepdims=True)

        @pl.when(i == 0)
        def _():
            kmax_ref[...] = kn2

        @pl.when(i > 0)
        def _():
            kmax_ref[...] = jnp.maximum(kmax_ref[...], kn2)
        fv_ref[...] = proj(C_FV, ATT_W).astype(BF16)
        sq_ref[...] = proj(C_SQ, ATT_W).astype(BF16)
        sk_ref[...] = proj(C_SK, ATT_W).astype(BF16)
        sv_ref[...] = proj(C_SV, ATT_W).astype(BF16)
        gf_ref[...] = proj(C_GF, D_MODEL)
        gs_ref[...] = proj(C_GS, D_MODEL)

        fl = proj(C_FL, LANES) + bias_ref[...]
        lane = lax.broadcasted_iota(jnp.int32, fl.shape, 1)
        logf = jnp.where(lane < N_HEADS, jnp.minimum(fl, 0.0) - _softplus_neg_abs(fl), 0.0)
        logf_ref[...] = logf

        @pl.when(i == 0)
        def _():
            carry[...] = jnp.zeros_like(carry)

        r = lax.broadcasted_iota(jnp.int32, (ts, ts), 0)
        c = lax.broadcasted_iota(jnp.int32, (ts, ts), 1)
        tri = (r >= c).astype(BF16)
        f_tile = _dot_split_left(tri, logf, 3) + carry[...]
        f_ref[...] = f_tile
        ft_ref[...] = f_tile.T[:N_HEADS, :]
        carry[...] = f_tile[ts - 1:ts, :]

    tok = lambda w: pl.BlockSpec((ts, w), lambda i: (i, 0))
    full = lambda a: pl.BlockSpec(a.shape, lambda i: (0, 0), pipeline_mode=pl.Buffered(1))
    f32o = lambda w: jax.ShapeDtypeStruct((s_len, w), F32)
    b16o = lambda w: jax.ShapeDtypeStruct((s_len, w), BF16)
    return _ride_call(
        body, name, (nt,),
        [tok(D_MODEL), full(gain), full(w_in), full(bias), full(qg), full(kg), full(gmat)],
        [
            tok(D_MODEL), tok(ATT_W), tok(ATT_W), tok(ATT_W), tok(ATT_W), tok(ATT_W), tok(LANES), tok(LANES),
            pl.BlockSpec((N_HEADS, ts), lambda i: (0, i)),
            tok(ATT_W), tok(ATT_W), tok(ATT_W), tok(D_MODEL), tok(D_MODEL),
            pl.BlockSpec((1, ATT_W), lambda i: (0, 0)),
        ],
        [
            b16o(D_MODEL), f32o(ATT_W), f32o(ATT_W), b16o(ATT_W), b16o(ATT_W), b16o(ATT_W), f32o(LANES), f32o(LANES),
            jax.ShapeDtypeStruct((N_HEADS, s_len), F32),
            b16o(ATT_W), b16o(ATT_W), b16o(ATT_W), f32o(D_MODEL), f32o(D_MODEL),
            jax.ShapeDtypeStruct((1, ATT_W), F32),
        ],
        [pltpu.VMEM((1, LANES), F32)], ("arbitrary",), (x, gain, w_in, bias, qg, kg, gmat), None)


ATT_T = 256
SB_ROWS = 2
FOX_ROWS = 2
EXP_ZERO = 88.0


def _att_tiling(s_len, rows):
    t = min(ATT_T, s_len)
    nr = min(rows, s_len // t)
    return t, nr, s_len // (t * nr)


def _pair_specs(s_len, tq):
    qblk = pl.BlockSpec((tq, LANES), lambda hp, i: (i, hp))
    kvfull = pl.BlockSpec((s_len, LANES), lambda hp, i: (0, hp))
    return qblk, kvfull


def _walk_tiles(i, nr, load, sub, flush, init, more=None, trips=None):
    base = i * nr
    carries = list(init)
    for kk in range(nr - 1, -1, -1):
        rs = list(range(kk, nr))
        new, side = sub(rs, load(base + kk), [carries[r] for r in rs], [r == kk for r in rs])
        carries[kk:] = new
        flush(base + kk, side)

    done = jnp.int32(0)
    for last in range(nr - 1, -1, -1):
        rs = list(range(last + 1))

        def visit(n, cs, rs=rs):
            kb = base - 1 - n
            cs, side = sub(rs, load(kb), list(cs), [False] * len(rs))
            flush(kb, side)
            return tuple(cs)

        if trips is not None:
            todo = jnp.maximum(trips(carries, base, last) - done, 0)
            new = lax.fori_loop(0, todo, lambda it, cs, start=done, visit=visit: visit(start + it, cs),
                                tuple(carries[:last + 1]))
            done = done + todo
        else:
            def step(state, visit=visit, last=last):
                n, _, cs = state
                cs = visit(n, cs)
                return n + 1, more(cs[last]), cs

            done, _, new = lax.while_loop(lambda state: jnp.logical_and(state[0] < base, state[1] > 0), step,
                                          (done, more(carries[last]), tuple(carries[:last + 1])))
        carries[:last + 1] = list(new)
    return carries


def _stack(parts):
    return parts[0] if len(parts) == 1 else jnp.concatenate(parts, axis=0)


def _stacked_halves(x, lo):
    z = jnp.zeros_like(x)
    return jnp.concatenate([jnp.where(lo, x, z), jnp.where(lo, z, x)], axis=0)


def _fox_qk_bound(qst_r, km_ref, t):
    km = km_ref[...]
    out = []
    for j in (0, 1):
        qf = qst_r[j * t:(j + 1) * t, :].astype(F32)
        qn = jnp.sqrt(jnp.sum(qf * qf, axis=1, keepdims=True))
        out.append(qn * jnp.sqrt(km[:, j * HEAD_DIM:j * HEAD_DIM + 1]) * 1.001 + 1.0)
    return out


def _fox_trips(hp, flast_ref, qkb, fq, level):
    def trips(carries, base, r):
        gap = [jnp.max(qkb[r][j] + fq[r][j] - level(carries, r, j)) for j in (0, 1)]

        def needed(n):
            kb = jnp.maximum(base - 1 - n, 0)
            return jnp.logical_or(gap[0] - flast_ref[2 * hp, kb] > -EXP_ZERO,
                                  gap[1] - flast_ref[2 * hp + 1, kb] > -EXP_ZERO)

        return lax.while_loop(lambda n: jnp.logical_and(n < base, needed(n)), lambda n: n + 1, jnp.int32(0))
    return trips


def _fox_fwd(q, k, v, f_col, f_row, kmax, name, ride=None):
    s_len = q.shape[0]
    t, nr, nq = _att_tiling(s_len, FOX_ROWS)

    def body(q_ref, k_ref, v_ref, f_ref, ft_ref, km_ref, fl_ref, y_ref, lse_ref):
        hp = pl.program_id(0)
        i = pl.program_id(1)
        lane = lax.broadcasted_iota(jnp.int32, (t, LANES), 1)
        lo = lane < HEAD_DIM
        causal = lax.broadcasted_iota(jnp.int32, (t, t), 0) >= lax.broadcasted_iota(jnp.int32, (t, t), 1)
        rows = [pl.ds(r * t, t) for r in range(nr)]
        qst = [_stacked_halves(q_ref[rw, :] * jnp.asarray(HEAD_DIM ** -0.5, BF16), lo) for rw in rows]
        q_all = _stack(qst)
        fq = [[jnp.sum(jnp.where(lane == 2 * hp + j, f_ref[rw, :], 0.0), axis=1, keepdims=True) for j in (0, 1)]
              for rw in rows]

        def load(kb):
            k0 = pl.multiple_of(kb * t, t)
            frow = [ft_ref[pl.ds(2 * hp + j, 1), pl.ds(k0, t)] for j in (0, 1)]
            return k_ref[pl.ds(k0, t), :], v_ref[pl.ds(k0, t), :], frow

        def sub(rs, tiles, carries, masked):
            kblk, vblk, frow = tiles
            z = _dot_nt(q_all if len(rs) == nr else _stack([qst[r] for r in rs]), kblk)
            ps, stats = [], []
            for n, (r, j) in enumerate((r, j) for r in range(len(rs)) for j in (0, 1)):
                m, l, _ = carries[r]
                s = z[n * t:(n + 1) * t, :] + (fq[rs[r]][j] - frow[j])
                if masked[r]:
                    s = jnp.where(causal, s, -1e30)
                mj = jnp.maximum(m[j], jnp.max(s, axis=1, keepdims=True))
                aj = jnp.exp(m[j] - mj)
                p = jnp.exp(s - mj)
                stats.append((mj, aj, aj * l[j] + jnp.sum(p, axis=1, keepdims=True)))
                ps.append(p.astype(BF16))
            pv = _dot(_stack(ps), vblk)
            out = []
            for r in range(len(rs)):
                (m0, a0, l0), (m1, a1, l1) = stats[2 * r], stats[2 * r + 1]
                acc = carries[r][2]
                acc = (acc[0] * a0 + pv[2 * r * t:(2 * r + 1) * t, :], acc[1] * a1 + pv[(2 * r + 1) * t:(2 * r + 2) * t, :])
                out.append(((m0, m1), (l0, l1), acc))
            return out, None

        neg = jnp.full((t, 1), -1e30, F32)
        zero = jnp.zeros((t, 1), F32)
        zacc = jnp.zeros((t, LANES), F32)
        init = [((neg, neg), (zero, zero), (zacc, zacc))] * nr
        qkb = [_fox_qk_bound(qs, km_ref, t) for qs in qst]
        trips = _fox_trips(hp, fl_ref, qkb, fq, lambda carries, r, j: carries[r][0][j])
        out = _walk_tiles(i, nr, load, sub, lambda kb, side: None, init, trips=trips)
        for rw, (m, l, acc) in zip(rows, out):
            y_ref[rw, :] = jnp.where(lo, acc[0] / l[0], acc[1] / l[1]).astype(BF16)
            lse_ref[0, rw, :] = jnp.where(lo, m[0] + jnp.log(l[0]), m[1] + jnp.log(l[1]))

    qblk, kvfull = _pair_specs(s_len, t * nr)
    return _ride_call(
        body, name, (N_HEADS // 2, nq),
        [qblk, kvfull, kvfull,
         pl.BlockSpec((t * nr, LANES), lambda hp, i: (i, 0)),
         pl.BlockSpec((N_HEADS, s_len), lambda hp, i: (0, 0)),
         pl.BlockSpec((1, LANES), lambda hp, i: (0, hp)),
         pl.BlockSpec(memory_space=pltpu.SMEM)],
        [qblk, pl.BlockSpec((1, t * nr, LANES), lambda hp, i: (hp, i, 0))],
        [jax.ShapeDtypeStruct((s_len, ATT_W), BF16), jax.ShapeDtypeStruct((N_HEADS // 2, s_len, LANES), F32)],
        [], ("parallel", "parallel"), (q, k, v, f_col, f_row, kmax, f_row[:, t - 1::t]), ride)


def _fox_bwd(q, k, v, dy, y, lse, f_col, f_row, kmax, name, ride=None):
    s_len = q.shape[0]
    t, nr, nq = _att_tiling(s_len, FOX_ROWS)

    def body(q_ref, k_ref, v_ref, dy_ref, y_ref, lse_ref, f_ref, ft_ref, km_ref, fl_ref,
             dq_ref, dk_ref, dv_ref, dft_ref):
        hp = pl.program_id(0)
        i = pl.program_id(1)

        @pl.when(i == 0)
        def _():
            dk_ref[...] = jnp.zeros_like(dk_ref)
            dv_ref[...] = jnp.zeros_like(dv_ref)
            dft_ref[...] = jnp.zeros_like(dft_ref)

        lane = lax.broadcasted_iota(jnp.int32, (t, LANES), 1)
        lo = lane < HEAD_DIM
        causal = lax.broadcasted_iota(jnp.int32, (t, t), 0) >= lax.broadcasted_iota(jnp.int32, (t, t), 1)
        rows = [pl.ds(r * t, t) for r in range(nr)]
        qst, dyst, delta, lse, fq = [], [], [], [], []
        for rw in rows:
            qst.append(_stacked_halves(q_ref[rw, :] * jnp.asarray(HEAD_DIM ** -0.5, BF16), lo))
            dyb = dy_ref[rw, :]
            dyst.append(_stacked_halves(dyb, lo))
            prod = dyb.astype(F32) * y_ref[rw, :].astype(F32)
            delta.append([jnp.sum(jnp.where(lo, prod, 0.0), axis=1, keepdims=True),
                          jnp.sum(jnp.where(lo, 0.0, prod), axis=1, keepdims=True)])
            lse_b = lse_ref[0, rw, :]
            lse.append([lse_b[:, 0:1], lse_b[:, HEAD_DIM:HEAD_DIM + 1]])
            fq.append([jnp.sum(jnp.where(lane == 2 * hp + j, f_ref[rw, :], 0.0), axis=1, keepdims=True)
                       for j in (0, 1)])

        q_all, dy_all = _stack(qst), _stack(dyst)

        def load(kb):
            k0 = pl.multiple_of(kb * t, t)
            frow = [ft_ref[pl.ds(2 * hp + j, 1), pl.ds(k0, t)] for j in (0, 1)]
            return k_ref[pl.ds(k0, t), :], v_ref[pl.ds(k0, t), :], frow

        def sub(rs, tiles, carries, masked):
            kblk, vblk, frow = tiles
            qs, dys = (q_all, dy_all) if len(rs) == nr else (_stack([qst[r] for r in rs]), _stack([dyst[r] for r in rs]))
            z = _dot_nt(qs, kblk)
            dp = _dot_nt(dys, vblk)
            pb, dsb, rsum, col = [], [], [], [None, None]
            for n, (r, j) in enumerate((r, j) for r in range(len(rs)) for j in (0, 1)):
                sl = slice(n * t, (n + 1) * t)
                s = z[sl, :] + (fq[rs[r]][j] - frow[j])
                p = jnp.exp(s - lse[rs[r]][j])
                if masked[r]:
                    p = jnp.where(causal, p, 0.0)
                ds = p * (dp[sl, :] - delta[rs[r]][j])
                c = jnp.sum(ds, axis=0, keepdims=True)
                col[j] = c if col[j] is None else col[j] + c
                rsum.append(carries[r][1][j] + jnp.sum(ds, axis=1, keepdims=True))
                pb.append(p.astype(BF16))
                dsb.append(ds.astype(BF16))
            p_all, ds_all = _stack(pb), _stack(dsb)
            dqs = _dot(ds_all, kblk)
            out = []
            for r in range(len(rs)):
                dq = carries[r][0]
                dq = (dq[0] + dqs[2 * r * t:(2 * r + 1) * t, :], dq[1] + dqs[(2 * r + 1) * t:(2 * r + 2) * t, :])
                out.append((dq, (rsum[2 * r], rsum[2 * r + 1])))
            return out, (_dot_tn(ds_all, qs), _dot_tn(p_all, dys), col)

        def flush(kb, side):
            k0 = pl.multiple_of(kb * t, t)
            dk_ref[pl.ds(k0, t), :] += side[0]
            dv_ref[pl.ds(k0, t), :] += side[1]
            for j in (0, 1):
                dft_ref[0, pl.ds(j, 1), pl.ds(k0, t)] -= side[2][j]

        zero = jnp.zeros((t, 1), F32)
        zacc = jnp.zeros((t, LANES), F32)
        qkb = [_fox_qk_bound(qs, km_ref, t) for qs in qst]
        trips = _fox_trips(hp, fl_ref, qkb, fq, lambda carries, r, j: lse[r][j])
        out = _walk_tiles(i, nr, load, sub, flush, [((zacc, zacc), (zero, zero))] * nr, trips=trips)
        for r, (rw, (dq, rs)) in enumerate(zip(rows, out)):
            dq_ref[rw, :] = jnp.where(lo, dq[0], dq[1]) * (HEAD_DIM ** -0.5)
            rs_t = jnp.where(lo, rs[0], rs[1]).T
            q0 = pl.multiple_of((i * nr + r) * t, t)
            for j in (0, 1):
                dft_ref[0, pl.ds(j, 1), pl.ds(q0, t)] += rs_t[j * HEAD_DIM:j * HEAD_DIM + 1, :]

    qblk, kvfull = _pair_specs(s_len, t * nr)
    return _ride_call(
        body, name, (N_HEADS // 2, nq),
        [qblk, kvfull, kvfull, qblk, qblk,
         pl.BlockSpec((1, t * nr, LANES), lambda hp, i: (hp, i, 0)),
         pl.BlockSpec((t * nr, LANES), lambda hp, i: (i, 0)),
         pl.BlockSpec((N_HEADS, s_len), lambda hp, i: (0, 0)),
         pl.BlockSpec((1, LANES), lambda hp, i: (0, hp)),
         pl.BlockSpec(memory_space=pltpu.SMEM)],
        [qblk, kvfull, kvfull, pl.BlockSpec((1, 8, s_len), lambda hp, i: (hp, 0, 0))],
        [jax.ShapeDtypeStruct((s_len, ATT_W), F32)] * 3 + [jax.ShapeDtypeStruct((N_HEADS // 2, 8, s_len), F32)],
        [], ("arbitrary", "arbitrary"), (q, k, v, dy, y, lse, f_col, f_row, kmax, f_row[:, t - 1::t]), ride)


def _sb_more(carry):
    return (jnp.max(jnp.maximum(carry[0][0], carry[0][1])) > -EXP_ZERO).astype(jnp.int32)


def _stacked_split_dot(slabs, m, parts):
    split = [_split(x, parts) for x in slabs]
    acc = None
    for p in range(parts):
        d = _dot(_stack([s[p] for s in split]), m)
        acc = d if acc is None else acc + d
    return acc


def _sb_weights(z, c, strict, upper, t):
    logs = []
    for n in range(z.shape[0] // t):
        zn = z[n * t:(n + 1) * t, :]
        sp = _softplus_neg_abs(zn)
        l1m = jnp.minimum(-zn, 0.0) - sp
        if strict[n] is not None:
            l1m = jnp.where(strict[n], l1m, 0.0)
        logs.append((jnp.minimum(zn, 0.0) - sp, l1m))
    suf = _stacked_split_dot([l1m for _, l1m in logs], upper, 1)
    out = []
    for n, (logb, l1m) in enumerate(logs):
        after = c[n] + suf[n * t:(n + 1) * t, :]
        a = jnp.exp(logb + after)
        if strict[n] is not None:
            a = jnp.where(strict[n], a, 0.0)
        out.append((logb, a, after[:, 0:1] + l1m[:, 0:1]))
    return out


def _sb_fwd(q, k, v, name):
    s_len = q.shape[0]
    t, nr, nq = _att_tiling(s_len, SB_ROWS)

    def body(q_ref, k_ref, v_ref, y_ref, yf_ref):
        i = pl.program_id(1)
        lane = lax.broadcasted_iota(jnp.int32, (t, LANES), 1)
        lo = lane < HEAD_DIM
        ri = lax.broadcasted_iota(jnp.int32, (t, t), 0)
        ci = lax.broadcasted_iota(jnp.int32, (t, t), 1)
        strict = ci < ri
        upper = (ri > ci).astype(BF16)
        rows = [pl.ds(r * t, t) for r in range(nr)]
        qst = [_stacked_halves(q_ref[rw, :] * jnp.asarray(HEAD_DIM ** -0.5, BF16), lo) for rw in rows]
        q_all = _stack(qst)

        def load(kb):
            k0 = pl.multiple_of(kb * t, t)
            return k_ref[pl.ds(k0, t), :], v_ref[pl.ds(k0, t), :]

        def sub(rs, tiles, carries, masked):
            kblk, vblk = tiles
            z = _dot_nt(q_all if len(rs) == nr else _stack([qst[r] for r in rs]), kblk)
            c = [carries[r][0][j] for r in range(len(rs)) for j in (0, 1)]
            w = _sb_weights(z, c, [strict if masked[r] else None for r in range(len(rs)) for j in (0, 1)], upper, t)
            pv = _dot(_stack([a.astype(BF16) for _, a, _ in w]), vblk)
            out = []
            for r in range(len(rs)):
                acc = carries[r][1]
                acc = (acc[0] + pv[2 * r * t:(2 * r + 1) * t, :], acc[1] + pv[(2 * r + 1) * t:(2 * r + 2) * t, :])
                out.append(((w[2 * r][2], w[2 * r + 1][2]), acc))
            return out, None

        zero = jnp.zeros((t, 1), F32)
        zacc = jnp.zeros((t, LANES), F32)
        out = _walk_tiles(i, nr, load, sub, lambda kb, side: None, [((zero, zero), (zacc, zacc))] * nr, more=_sb_more)
        for rw, (_, acc) in zip(rows, out):
            y = jnp.where(lo, acc[0], acc[1])
            y_ref[rw, :] = y.astype(BF16)
            yf_ref[rw, :] = y

    qblk, kvfull = _pair_specs(s_len, t * nr)
    return pl.pallas_call(
        body, name=name, grid=(N_HEADS // 2, nq),
        in_specs=[qblk, kvfull, kvfull],
        out_specs=[qblk, qblk],
        out_shape=[jax.ShapeDtypeStruct((s_len, ATT_W), BF16), jax.ShapeDtypeStruct((s_len, ATT_W), F32)],
        compiler_params=_cparams("parallel", "parallel"),
    )(q, k, v)


def _sb_bwd(q, k, v, dy, yf, name):
    s_len = q.shape[0]
    t, nr, nq = _att_tiling(s_len, SB_ROWS)

    def body(q_ref, k_ref, v_ref, dy_ref, yf_ref, dq_ref, dk_ref, dv_ref):
        i = pl.program_id(1)

        @pl.when(i == 0)
        def _():
            dk_ref[...] = jnp.zeros_like(dk_ref)
            dv_ref[...] = jnp.zeros_like(dv_ref)

        lane = lax.broadcasted_iota(jnp.int32, (t, LANES), 1)
        lo = lane < HEAD_DIM
        ri = lax.broadcasted_iota(jnp.int32, (t, t), 0)
        ci = lax.broadcasted_iota(jnp.int32, (t, t), 1)
        strict = ci < ri
        upper = (ri > ci).astype(BF16)
        upper_incl = (ri >= ci).astype(BF16)
        rows = [pl.ds(r * t, t) for r in range(nr)]
        qst, dyst, delta = [], [], []
        for rw in rows:
            qst.append(_stacked_halves(q_ref[rw, :] * jnp.asarray(HEAD_DIM ** -0.5, BF16), lo))
            dyb = dy_ref[rw, :]
            dyst.append(_stacked_halves(dyb, lo))
            prod = dyb.astype(F32) * yf_ref[rw, :]
            delta.append([jnp.sum(jnp.where(lo, prod, 0.0), axis=1, keepdims=True),
                          jnp.sum(jnp.where(lo, 0.0, prod), axis=1, keepdims=True)])
        q_all, dy_all = _stack(qst), _stack(dyst)

        def load(kb):
            k0 = pl.multiple_of(kb * t, t)
            return k_ref[pl.ds(k0, t), :], v_ref[pl.ds(k0, t), :]

        def sub(rs, tiles, carries, masked):
            kblk, vblk = tiles
            qs, dys = (q_all, dy_all) if len(rs) == nr else (_stack([qst[r] for r in rs]), _stack([dyst[r] for r in rs]))
            slabs = [(r, j) for r in range(len(rs)) for j in (0, 1)]
            z = _dot_nt(qs, kblk)
            w = _sb_weights(z, [carries[r][0][j] for r, j in slabs], [strict if masked[r] else None for r, j in slabs],
                            upper, t)
            da = _dot_nt(dys, vblk)
            ab = [a.astype(BF16) for _, a, _ in w]
            dl = [ab[n].astype(F32) * da[n * t:(n + 1) * t, :] for n in range(len(slabs))]
            tail = _stacked_split_dot(dl, upper_incl, 2)
            dzb, e_new = [], []
            for n, (r, j) in enumerate(slabs):
                tl = tail[n * t:(n + 1) * t, :]
                e = carries[r][1][j]
                dl1m = (delta[rs[r]][j] - e) - tl
                e_new.append(e + tl[:, 0:1])
                dz = dl[n] - jnp.exp(w[n][0]) * (dl[n] + dl1m)
                if masked[r]:
                    dz = jnp.where(strict, dz, 0.0)
                dzb.append(dz.astype(BF16))
            a_all, dz_all = _stack(ab), _stack(dzb)
            dqs = _dot(dz_all, kblk)
            out = []
            for r in range(len(rs)):
                dq = carries[r][2]
                dq = (dq[0] + dqs[2 * r * t:(2 * r + 1) * t, :], dq[1] + dqs[(2 * r + 1) * t:(2 * r + 2) * t, :])
                out.append(((w[2 * r][2], w[2 * r + 1][2]), (e_new[2 * r], e_new[2 * r + 1]), dq))
            return out, (_dot_tn(dz_all, qs), _dot_tn(a_all, dys))

        def flush(kb, side):
            k0 = pl.multiple_of(kb * t, t)
            dk_ref[pl.ds(k0, t), :] += side[0]
            dv_ref[pl.ds(k0, t), :] += side[1]

        zero = jnp.zeros((t, 1), F32)
        zacc = jnp.zeros((t, LANES), F32)
        out = _walk_tiles(i, nr, load, sub, flush, [((zero, zero), (zero, zero), (zacc, zacc))] * nr, more=_sb_more)
        for rw, (_, _, dq) in zip(rows, out):
            dq_ref[rw, :] = jnp.where(lo, dq[0], dq[1]) * (HEAD_DIM ** -0.5)

    qblk, kvfull = _pair_specs(s_len, t * nr)
    return pl.pallas_call(
        body, name=name, grid=(N_HEADS // 2, nq),
        in_specs=[qblk, kvfull, kvfull, qblk, qblk],
        out_specs=[qblk, kvfull, kvfull],
        out_shape=[jax.ShapeDtypeStruct((s_len, ATT_W), F32)] * 3,
        compiler_params=_cparams("arbitrary", "arbitrary"),
    )(q, k, v, dy, yf)


def _merge_fwd(x, yf, ys, gf, gs, wbf, wbs, wo, name):
    s_len = x.shape[0]
    ts = min(512, s_len)

    def body(x_ref, yf_ref, ys_ref, gf_ref, gs_ref, wbf_ref, wbs_ref, wo_ref, o_ref):
        merged = (_sigmoid(gf_ref[...]) * _dot_nt(yf_ref[...], wbf_ref[...])
                  + _sigmoid(gs_ref[...]) * _dot_nt(ys_ref[...], wbs_ref[...]))
        o_ref[...] = x_ref[...] + _dot(merged.astype(BF16), wo_ref[...])

    tok = lambda w: pl.BlockSpec((ts, w), lambda i: (i, 0))
    full = lambda a: pl.BlockSpec(a.shape, lambda i: (0, 0))
    return pl.pallas_call(
        body, name=name, grid=(s_len // ts,),
        in_specs=[tok(D_MODEL), tok(ATT_W), tok(ATT_W), tok(D_MODEL), tok(D_MODEL), full(wbf), full(wbs), full(wo)],
        out_specs=tok(D_MODEL),
        out_shape=jax.ShapeDtypeStruct((s_len, D_MODEL), F32),
        compiler_params=_cparams("parallel"),
    )(x, yf, ys, gf, gs, wbf, wbs, wo)


def _merge_bwd(dx, yf, ys, gf, gs, wbf, wbs, wo, name):
    s_len = dx.shape[0]
    ts = min(512, s_len)

    def body(dx_ref, yf_ref, ys_ref, gf_ref, gs_ref, wbf_ref, wbs_ref, wo_ref,
             dyf_ref, dys_ref, dgf_ref, dgs_ref, dbf_ref, dbs_ref, mg_ref):
        bf = _dot_nt(yf_ref[...], wbf_ref[...])
        bs = _dot_nt(ys_ref[...], wbs_ref[...])
        sf = _sigmoid(gf_ref[...])
        ss = _sigmoid(gs_ref[...])
        mg_ref[...] = (sf * bf + ss * bs).astype(BF16)
        dm = _dot_nt(dx_ref[...].astype(BF16), wo_ref[...])
        dbf = (dm * sf).astype(BF16)
        dbs = (dm * ss).astype(BF16)
        dbf_ref[...] = dbf
        dbs_ref[...] = dbs
        dgf_ref[...] = (dm * bf * (sf * (1.0 - sf))).astype(BF16)
        dgs_ref[...] = (dm * bs * (ss * (1.0 - ss))).astype(BF16)
        dyf_ref[...] = _dot(dbf, wbf_ref[...]).astype(BF16)
        dys_ref[...] = _dot(dbs, wbs_ref[...]).astype(BF16)

    tok = lambda w: pl.BlockSpec((ts, w), lambda i: (i, 0))
    full = lambda a: pl.BlockSpec(a.shape, lambda i: (0, 0))
    b16o = lambda w: jax.ShapeDtypeStruct((s_len, w), BF16)
    return pl.pallas_call(
        body, name=name, grid=(s_len // ts,),
        in_specs=[tok(D_MODEL), tok(ATT_W), tok(ATT_W), tok(D_MODEL), tok(D_MODEL), full(wbf), full(wbs), full(wo)],
        out_specs=[tok(ATT_W), tok(ATT_W)] + [tok(D_MODEL)] * 5,
        out_shape=[b16o(ATT_W), b16o(ATT_W)] + [b16o(D_MODEL)] * 5,
        compiler_params=_cparams("parallel"),
    )(dx, yf, ys, gf, gs, wbf, wbs, wo)


def _mix_bwd(x, dx_in, gain, w_in, fqr, fkr, dfqn, dfkn, qg, kg, dfv, df_col, logf, dsq, dsk, dsv, dgf, dgs, name):
    s_len = x.shape[0]
    ts = min(256, s_len)
    nt = s_len // ts
    gmat = _head_group_matrix()

    def body(x_ref, dxi_ref, gain_ref, w_ref, fqr_ref, fkr_ref, dfqn_ref, dfkn_ref, qg_ref, kg_ref, gm_ref,
             dfv_ref, df_ref, logf_ref, dsq_ref, dsk_ref, dsv_ref, dgf_ref, dgs_ref,
             dp_ref, dx_ref, dgain_ref, dqg_ref, dkg_ref, dbias_ref, carry):
        i = pl.program_id(0)

        @pl.when(i == 0)
        def _():
            carry[...] = jnp.zeros_like(carry)
            dgain_ref[...] = jnp.zeros_like(dgain_ref)
            dqg_ref[...] = jnp.zeros_like(dqg_ref)
            dkg_ref[...] = jnp.zeros_like(dkg_ref)
            dbias_ref[...] = jnp.zeros_like(dbias_ref)

        gm = gm_ref[...]

        def headnorm_bwd(raw, dout, g, dg_ref):
            ms = _dot_split(raw * raw, gm, HEAD_SUM_PARTS) * (1.0 / HEAD_DIM)
            r = lax.rsqrt(ms + EPS)
            nrm = raw * r
            dg_ref[...] += jnp.sum(dout * nrm, axis=0, keepdims=True)
            dn = dout * g
            mean_h = _dot_split(dn * nrm, gm, HEAD_SUM_PARTS) * (1.0 / HEAD_DIM)
            return r * (dn - nrm * mean_h)

        dp_ref[:, C_FQ:C_FQ + ATT_W] = headnorm_bwd(fqr_ref[...], dfqn_ref[...], qg_ref[...], dqg_ref).astype(BF16)
        dp_ref[:, C_FK:C_FK + ATT_W] = headnorm_bwd(fkr_ref[...], dfkn_ref[...], kg_ref[...], dkg_ref).astype(BF16)
        dp_ref[:, C_FV:C_FV + ATT_W] = dfv_ref[...].astype(BF16)
        dp_ref[:, C_SQ:C_SQ + ATT_W] = dsq_ref[...].astype(BF16)
        dp_ref[:, C_SK:C_SK + ATT_W] = dsk_ref[...].astype(BF16)
        dp_ref[:, C_SV:C_SV + ATT_W] = dsv_ref[...].astype(BF16)
        dp_ref[:, C_GF:C_GF + D_MODEL] = dgf_ref[...]
        dp_ref[:, C_GS:C_GS + D_MODEL] = dgs_ref[...]

        r_ = lax.broadcasted_iota(jnp.int32, (ts, ts), 0)
        c_ = lax.broadcasted_iota(jnp.int32, (ts, ts), 1)
        rev = (c_ >= r_).astype(BF16)
        dlogf = _dot_split_left(rev, df_ref[...], 3) + carry[...]
        carry[...] = dlogf[0:1, :]
        lane = lax.broadcasted_iota(jnp.int32, (ts, LANES), 1)
        dfl = jnp.where(lane < N_HEADS, dlogf * (1.0 - jnp.exp(logf_ref[...])), 0.0)
        dbias_ref[...] += jnp.sum(dfl, axis=0, keepdims=True)
        dp_ref[:, C_FL:C_FL + LANES] = dfl.astype(BF16)
        dp_ref[:, C_FL + LANES:C_SQ] = jnp.zeros((ts, C_SQ - C_FL - LANES), BF16)

        dh = _dot(dp_ref[...], w_ref[...])
        xf = x_ref[...]
        r = _rms_rinv(xf)
        xhat = xf * r
        dgain_ref[...] += jnp.sum(dh * xhat, axis=0, keepdims=True)
        dn = dh * gain_ref[...]
        dx_ref[...] = dxi_ref[...] + r * (dn - xhat * jnp.mean(dn * xhat, axis=-1, keepdims=True))

    tok = lambda w: pl.BlockSpec((ts, w), lambda i: (nt - 1 - i, 0))
    full = lambda a: pl.BlockSpec(a.shape, lambda i: (0, 0))
    row = lambda w: pl.BlockSpec((1, w), lambda i: (0, 0))
    return _ride_call(
        body, name, (nt,),
        [tok(D_MODEL), tok(D_MODEL), full(gain), full(w_in), tok(ATT_W), tok(ATT_W), tok(ATT_W), tok(ATT_W),
         full(qg), full(kg), full(gmat), tok(ATT_W), tok(LANES), tok(LANES), tok(ATT_W), tok(ATT_W), tok(ATT_W),
         tok(D_MODEL), tok(D_MODEL)],
        [tok(IN_PAD), tok(D_MODEL), row(D_MODEL), row(ATT_W), row(ATT_W), row(LANES)],
        [jax.ShapeDtypeStruct((s_len, IN_PAD), BF16), jax.ShapeDtypeStruct((s_len, D_MODEL), F32),
         jax.ShapeDtypeStruct((1, D_MODEL), F32), jax.ShapeDtypeStruct((1, ATT_W), F32),
         jax.ShapeDtypeStruct((1, ATT_W), F32), jax.ShapeDtypeStruct((1, LANES), F32)],
        [pltpu.VMEM((1, LANES), F32)], ("arbitrary",),
        (x, dx_in, gain, w_in, fqr, fkr, dfqn, dfkn, qg, kg, gmat, dfv, df_col, logf, dsq, dsk, dsv, dgf, dgs), None)


def _ple_loss(x, p, tgt, gain, wpg, wpp, name):
    s_len = x.shape[0]
    ts = min(512, s_len)

    def body(x_ref, p_ref, t_ref, gain_ref, wpg_ref, wpp_ref, dx_ref, n_ref, ds_ref, dpp_ref, dgain_ref, loss_ref):
        i = pl.program_id(0)

        @pl.when(i == 0)
        def _():
            dgain_ref[...] = jnp.zeros_like(dgain_ref)
            loss_ref[...] = jnp.zeros_like(loss_ref)

        xf = x_ref[...]
        r = _rms_rinv(xf)
        n = xf * r
        hn = (n * gain_ref[...]).astype(BF16)
        n_ref[...] = hn
        sg = _sigmoid(_dot(hn, wpg_ref[...]))
        pp = _dot_nt(p_ref[...].astype(BF16), wpp_ref[...])
        err = (xf + sg * pp) - t_ref[...]
        sq = jnp.sum(jnp.sum(err * err, axis=1, keepdims=True), axis=0, keepdims=True)
        loss_ref[...] += (0.5 / D_MODEL) * sq
        dout = err * (1.0 / D_MODEL)
        dpp_ref[...] = (dout * sg).astype(BF16)
        ds = (dout * pp * (sg * (1.0 - sg))).astype(BF16)
        ds_ref[...] = ds
        dhn = _dot_nt(ds, wpg_ref[...])
        dgain_ref[...] += jnp.sum(dhn * n, axis=0, keepdims=True)
        dn = dhn * gain_ref[...]
        dx_ref[...] = dout + r * (dn - n * jnp.mean(dn * n, axis=-1, keepdims=True))

    tok = lambda w: pl.BlockSpec((ts, w), lambda i: (i, 0))
    full = lambda a: pl.BlockSpec(a.shape, lambda i: (0, 0))
    return pl.pallas_call(
        body, name=name, grid=(s_len // ts,),
        in_specs=[tok(D_MODEL), tok(PLE_DIM), tok(D_MODEL), full(gain), full(wpg), full(wpp)],
        out_specs=[tok(D_MODEL), tok(D_MODEL), tok(D_MODEL), tok(D_MODEL),
                   pl.BlockSpec((1, D_MODEL), lambda i: (0, 0)), pl.BlockSpec((8, LANES), lambda i: (0, 0))],
        out_shape=[jax.ShapeDtypeStruct((s_len, D_MODEL), F32), jax.ShapeDtypeStruct((s_len, D_MODEL), BF16),
                   jax.ShapeDtypeStruct((s_len, D_MODEL), BF16), jax.ShapeDtypeStruct((s_len, D_MODEL), BF16),
                   jax.ShapeDtypeStruct((1, D_MODEL), F32), jax.ShapeDtypeStruct((8, LANES), F32)],
        compiler_params=_cparams("arbitrary"),
    )(x, p, tgt, gain, wpg, wpp)


def _exchange(x, name, broadcast):
    def body(x_ref, out_ref, send_sems, recv_sems, local_sem):
        _exchange_start(x_ref, out_ref, send_sems, recv_sems, local_sem, broadcast)
        _exchange_wait(x_ref, out_ref, send_sems, recv_sems, local_sem, broadcast)

    return pl.pallas_call(
        body, name=name,
        in_specs=[EXCHANGE_SPEC],
        out_specs=EXCHANGE_SPEC,
        out_shape=_exchange_shape(x, broadcast),
        scratch_shapes=list(EXCHANGE_SEMS),
        compiler_params=pltpu.CompilerParams(has_side_effects=True),
    )(x)


def _gather_two_level(x, name):
    def body(x_ref, out_ref, send_sems, recv_sems, local_sem):
        mx, my, mc = lax.axis_index("x"), lax.axis_index("y"), lax.axis_index("c")
        me, sibling = (mx, my, mc), (mx, my, 1 - mc)
        chips = [(1 - mx, my), (mx, 1 - my), (1 - mx, 1 - my)]

        def slot(px, py, pc):
            return out_ref.at[4 * px + 2 * py + pc]

        def copy(k, block, to, src=None):
            return pltpu.make_async_remote_copy(
                src_ref=slot(*block) if src is None else src, dst_ref=slot(*block),
                send_sem=send_sems.at[k], recv_sem=recv_sems.at[k], device_id=to, device_id_type=MESH)

        mine = pltpu.make_async_copy(x_ref, slot(*me), local_sem)
        mine.start()
        first = [copy(0, me, sibling, src=x_ref)]
        first += [copy(1 + j, me, (*chip, mc), src=x_ref) for j, chip in enumerate(chips)]
        for cp in first:
            cp.start()
        passed = [copy(4 + j, (*chip, mc), sibling) for j, chip in enumerate(chips)]
        for j, chip in enumerate(chips):
            copy(1 + j, (*chip, mc), me).wait_recv()
            passed[j].start()
        copy(0, sibling, me).wait_recv()
        for j, chip in enumerate(chips):
            copy(4 + j, (*chip, 1 - mc), me).wait_recv()
        for cp in first + passed:
            cp.wait_send()
        mine.wait()

    return pl.pallas_call(
        body, name=name,
        in_specs=[EXCHANGE_SPEC],
        out_specs=EXCHANGE_SPEC,
        out_shape=_exchange_shape(x, True),
        scratch_shapes=list(EXCHANGE_SEMS),
        compiler_params=pltpu.CompilerParams(has_side_effects=True),
    )(x)


EXCHANGE_SPEC = pl.BlockSpec(memory_space=pl.ANY)
EXCHANGE_SEMS = (pltpu.SemaphoreType.DMA((N_DEV - 1,)), pltpu.SemaphoreType.DMA((N_DEV - 1,)), pltpu.SemaphoreType.DMA)


def _exchange_shape(x, broadcast):
    return jax.ShapeDtypeStruct((N_DEV,) + tuple(x.shape if broadcast else x.shape[1:]), x.dtype)


def _exchange_copies(x_ref, out_ref, send_sems, recv_sems, local_sem, broadcast, with_recv=True):
    mx, my, mc = lax.axis_index("x"), lax.axis_index("y"), lax.axis_index("c")
    me = 4 * mx + 2 * my + mc

    def src(idx):
        return x_ref if broadcast else x_ref.at[idx]

    local = pltpu.make_async_copy(src(me), out_ref.at[me], local_sem)
    pairs = []
    for k in range(1, N_DEV):
        px = (1 - mx) if k & 4 else mx
        py = (1 - my) if k & 2 else my
        pc = (1 - mc) if k & 1 else mc
        peer = 4 * px + 2 * py + pc
        sems = dict(send_sem=send_sems.at[k - 1], recv_sem=recv_sems.at[k - 1], device_id=(px, py, pc), device_id_type=MESH)
        recv = pltpu.make_async_remote_copy(src_ref=src(peer), dst_ref=out_ref.at[peer], **sems) if with_recv else None
        pairs.append((pltpu.make_async_remote_copy(src_ref=src(peer), dst_ref=out_ref.at[me], **sems), recv))
    return local, pairs


def _exchange_start(*refs_and_mode):
    local, pairs = _exchange_copies(*refs_and_mode, with_recv=False)
    local.start()
    for send, _ in pairs:
        send.start()


def _exchange_wait(*refs_and_mode):
    local, pairs = _exchange_copies(*refs_and_mode)
    for _, recv in pairs:
        recv.wait_recv()
    for send, _ in pairs:
        send.wait_send()
    local.wait()


def _riding(body, grid, n_in, n_out, ride):
    if ride is None:
        return body
    broadcast = ride[1]

    def wrapped(*refs):
        ins, x_ref = refs[:n_in], refs[n_in]
        outs, out_ref = refs[n_in + 1:n_in + 1 + n_out], refs[n_in + 1 + n_out]
        scratch, sems = refs[n_in + 2 + n_out:-3], refs[-3:]
        step = pl.program_id(0)
        for d in range(1, len(grid)):
            step = step * grid[d] + pl.program_id(d)
        total = 1
        for g in grid:
            total *= g

        @pl.when(step == 0)
        def _():
            _exchange_start(x_ref, out_ref, *sems, broadcast)

        body(*ins, *outs, *scratch)

        @pl.when(step == total - 1)
        def _():
            _exchange_wait(x_ref, out_ref, *sems, broadcast)

    return wrapped


def _ride_call(body, name, grid, in_specs, out_specs, out_shape, scratch_shapes, sem, operands, ride):
    if ride is None:
        return pl.pallas_call(body, name=name, grid=grid, in_specs=in_specs, out_specs=out_specs, out_shape=out_shape,
                              scratch_shapes=scratch_shapes, compiler_params=_cparams(*sem))(*operands)
    return pl.pallas_call(
        _riding(body, grid, len(in_specs), len(out_specs), ride), name=name, grid=grid,
        in_specs=list(in_specs) + [EXCHANGE_SPEC], out_specs=list(out_specs) + [EXCHANGE_SPEC],
        out_shape=list(out_shape) + [_exchange_shape(*ride)],
        scratch_shapes=list(scratch_shapes) + list(EXCHANGE_SEMS),
        compiler_params=_cparams(*(["arbitrary"] * len(grid))),
    )(*operands, ride[0])


def _adamw_math(w, g, m, v):
    m2 = ADAM_B1 * m + (1.0 - ADAM_B1) * g
    v2 = ADAM_B2 * v + (1.0 - ADAM_B2) * (g * g)
    m_hat = m2 / (1.0 - ADAM_B1 ** ADAM_STEP)
    v_hat = v2 / (1.0 - ADAM_B2 ** ADAM_STEP)
    delta = -ADAM_LR * (m_hat / (jnp.sqrt(v_hat) + ADAM_EPS) + ADAM_WD * w)
    return delta, m2, v2


def _sum_parts(parts, name, tr):
    _, rows, cols = parts.shape

    def body(p_ref, g_ref):
        g = p_ref[0].astype(F32)
        for s in range(1, N_DEV):
            g = g + p_ref[s].astype(F32)
        g_ref[...] = g

    return pl.pallas_call(
        body, name=name, grid=(rows // tr,),
        in_specs=[pl.BlockSpec((N_DEV, tr, cols), lambda i: (0, i, 0))],
        out_specs=pl.BlockSpec((tr, cols), lambda i: (i, 0)),
        out_shape=jax.ShapeDtypeStruct((rows, cols), F32),
        compiler_params=_cparams("parallel"),
    )(parts)


ADAM_SPLIT_ELEMS = 400_000


def _adamw_shard(g, w, m, v, name):
    rows, cols = w.shape
    tr = rows // 2 if rows * cols > ADAM_SPLIT_ELEMS else rows

    def body(g_ref, w_ref, m_ref, v_ref, d_ref, m2_ref, v2_ref):
        d_ref[...], m2_ref[...], v2_ref[...] = _adamw_math(w_ref[...], g_ref[...], m_ref[...], v_ref[...])

    blk = pl.BlockSpec((tr, cols), lambda i: (i, 0))
    return pl.pallas_call(
        body, name=name, grid=(rows // tr,),
        in_specs=[blk] * 4, out_specs=[blk] * 3,
        out_shape=[jax.ShapeDtypeStruct((rows, cols), F32)] * 3,
        compiler_params=_cparams("parallel"),
    )(g, w, m, v)


def _adamw_small(parts, w, m, v, name):
    names = list(SMALL_NAMES)

    def body(p_ref, *refs):
        ins, outs = refs[:3 * len(names)], refs[3 * len(names):]
        total = p_ref[0]
        for s in range(1, N_DEV):
            total = total + p_ref[s]
        for i, n in enumerate(names):
            row, off, width = SMALL_POS[n]
            g = total[row:row + 1, off:off + width]
            w_ref, m_ref, v_ref = ins[3 * i:3 * i + 3]
            g_ref, d_ref, m2_ref, v2_ref = outs[4 * i:4 * i + 4]
            g_ref[...] = g
            d_ref[...], m2_ref[...], v2_ref[...] = _adamw_math(w_ref[...], g, m_ref[...], v_ref[...])
        row, off, _ = SMALL_POS["loss"]
        outs[-1][...] = total[row:row + 1, off:off + 1]

    operands = [parts] + [t[n] for n in names for t in (w, m, v)]
    shapes = [jax.ShapeDtypeStruct(w[n].shape, F32) for n in names for _ in range(4)]
    shapes.append(jax.ShapeDtypeStruct((1, 1), F32))
    out = pl.pallas_call(body, name=name, out_shape=shapes)(*operands)
    return {n: tuple(out[4 * i:4 * i + 4]) for i, n in enumerate(names)}, out[-1]


TRANSPOSED = frozenset(("ffn1_w_gate", "ffn1_w_up", "w_in", "w_branch_fox", "w_branch_sb", "ffn2_w_gate", "ffn2_w_up",
                        "w_ple_proj"))
F_PAD_ROWS = C_SQ - FL_REAL_END


def _pack(pieces, group, dtype):
    out = []
    for name in GATHER_GROUPS[group]:
        r = pieces[name].T if name in TRANSPOSED else pieces[name]
        r = r.reshape(-1, D_MODEL).astype(dtype)
        if r.shape[0] != PACK_ROWS[name]:
            r = jnp.pad(r, ((0, PACK_ROWS[name] - r.shape[0]), (0, 0)))
        out.append(r)
    return jnp.concatenate(out, axis=0)


def _real_rows(name):
    return W_IN_ROWS if name == "w_in" else PACK_ROWS[name]


def _gathered(got, name, shape):
    off = GATHER_OFF[name]
    return got[:, off:off + _real_rows(name), :].reshape(shape)


def _w_in_device_rows(d):
    lo, hi = d * W_IN_ROWS, (d + 1) * W_IN_ROWS
    if hi <= FL_REAL_END:
        return [(lo, hi)]
    if lo >= FL_REAL_END:
        return [(lo + F_PAD_ROWS, hi + F_PAD_ROWS)]
    return [(lo, FL_REAL_END), (C_SQ, hi + F_PAD_ROWS)]


def _w_in_t_padded(got):
    t = _gathered(got, "w_in", (IN_REAL, D_MODEL))
    return jnp.concatenate([t[:FL_REAL_END], jnp.zeros((F_PAD_ROWS, D_MODEL), t.dtype), t[FL_REAL_END:]], axis=0)


def _pack_chunks(grads, group):
    out = []
    for name in SCATTER_GROUPS[group]:
        base, _, half = name.partition("#")
        g = grads[base].astype(BF16)
        if base == "w_in":
            lo, hi = W_IN_HALVES[int(half)]
            tail = jnp.zeros((PACK_ROWS[base] - W_IN_ROWS, D_MODEL), BF16)
            c = jnp.stack([jnp.concatenate([g[a:b] for a, b in _w_in_device_rows(d)] + [tail], axis=0)[lo:hi]
                           for d in range(N_DEV)])
        else:
            c = g.reshape(N_DEV, PACK_ROWS[name], D_MODEL)
        out.append(c)
    return out[0] if len(out) == 1 else jnp.concatenate(out, axis=1)


def _shard_grad(summed, name, shape):
    if name == "w_in":
        rows = jnp.concatenate([summed[f"w_in#{i}"] for i in range(len(W_IN_HALVES))], axis=0)[:W_IN_ROWS]
    else:
        rows = summed[name][SCATTER_OFF[name]:SCATTER_OFF[name] + PACK_ROWS[name], :]
    return rows.reshape(shape[1], shape[0]).T if name in TRANSPOSED else rows.reshape(shape)


WEIGHT_NAMES = ['ffn1_norm', 'ffn1_w_gate', 'ffn1_w_up', 'ffn1_w_down', 'mix_norm', 'w_in', 'forget_bias', 'q_norm',
                'k_norm', 'w_branch_fox', 'w_branch_sb', 'w_out', 'ffn2_norm', 'ffn2_w_gate', 'ffn2_w_up',
                'ffn2_w_down', 'ple_norm', 'w_ple_gate', 'w_ple_proj']
SMALL_NAMES = ('ffn1_norm', 'mix_norm', 'ffn2_norm', 'ple_norm', 'q_norm', 'k_norm', 'forget_bias')
SMALL_POS = {'ffn1_norm': (0, 0, D_MODEL), 'mix_norm': (1, 0, D_MODEL), 'ffn2_norm': (2, 0, D_MODEL),
             'ple_norm': (3, 0, D_MODEL), 'q_norm': (4, 0, HEAD_DIM), 'k_norm': (4, HEAD_DIM, HEAD_DIM),
             'forget_bias': (4, 2 * HEAD_DIM, N_HEADS), 'loss': (4, 2 * HEAD_DIM + N_HEADS, 1)}


def _pack_small(vals, loss):
    tail = [vals[n].reshape(1, -1) for n in ('q_norm', 'k_norm', 'forget_bias')] + [loss.reshape(1, 1)]
    tail.append(jnp.zeros((1, D_MODEL - sum(t.shape[1] for t in tail)), F32))
    rows = [vals[n].reshape(1, D_MODEL) for n in SMALL_NAMES[:4]] + [jnp.concatenate(tail, axis=1)]
    rows.append(jnp.zeros((SMALL_ROWS - len(rows), D_MODEL), F32))
    return jnp.concatenate(rows, axis=0)


def _step(x, p, tgt, w):
    row = lambda a: a.reshape(1, -1).astype(F32)
    g_ffn1, g_mix, g_ffn2, g_ple = (row(w[n]) for n in SMALL_NAMES[:4])
    qg = jnp.tile(row(w['q_norm']), (1, N_HEADS))
    kg = jnp.tile(row(w['k_norm']), (1, N_HEADS))
    bias = jnp.pad(row(w['forget_bias']), ((0, 0), (0, LANES - N_HEADS)))
    half = D_FF // 2
    grads = {}

    blk = lambda n: GATHER_OFF[n] // FFN_SHARD
    ffn1 = tuple(blk(n) for n in ("ffn1_w_gate", "ffn1_w_up", "ffn1_w_down"))
    ffn2 = tuple(blk(n) for n in ("ffn2_w_gate", "ffn2_w_up", "ffn2_w_down"))
    got0 = _gather_two_level(_pack(w, 0, BF16), "gather_ffn1")
    x1, g1, u1, h1, got1 = _ffn_fwd(x, g_ffn1, got0, ffn1, "ffn1_fwd", ride=(_pack(w, 1, BF16), True))
    w_in = _w_in_t_padded(got1)
    wbf = _gathered(got1, "w_branch_fox", (D_MODEL, ATT_W))
    wbs = _gathered(got1, "w_branch_sb", (D_MODEL, ATT_W))
    wo = _gathered(got1, "w_out", (D_MODEL, D_MODEL))
    (hmix, fqr, fkr, fqn, fkn, fv, logf, f_col, f_row, sq, sk, sv, gf, gs, kmax) = _mix_fwd(
        x1, g_mix, w_in, bias, qg, kg, "mix_fwd")
    y_fox, lse, got2 = _fox_fwd(fqn, fkn, fv, f_col, f_row, kmax, "fox_fwd", ride=(_pack(w, 2, BF16), True))
    wpg = _gathered(got2, "w_ple_gate", (D_MODEL, D_MODEL))
    wpp = _gathered(got2, "w_ple_proj", (D_MODEL, PLE_DIM))
    y_sb, y_sb32 = _sb_fwd(sq, sk, sv, "sb_fwd")
    x2 = _merge_fwd(x1, y_fox, y_sb, gf, gs, wbf, wbs, wo, "merge_fwd")
    x3, g2, u2, h2, = _ffn_fwd(x2, g_ffn2, got2, ffn2, "ffn2_fwd")
    dx3, n_ple, ds_ple, dpp, dg_ple, loss = _ple_loss(x3, p, tgt, g_ple, wpg, wpp, "ple_loss")

    grads['w_ple_gate'] = _wgrad(n_ple, ds_ple, "dw_ple_gate", D_MODEL, D_MODEL)
    grads['w_ple_proj'] = _wgrad(dpp, p, "dw_ple_proj", D_MODEL, PLE_DIM)
    dg2, du2, act2, dx2, dg_ffn2 = _ffn_bwd_fused(x2, dx3, g_ffn2, g2, u2, got2, ffn2, "ffn2_bwd")
    grads['ffn2_w_gate'] = _wgrad(dg2, h2, "dw_ffn2_gate", half, D_MODEL)
    grads['ffn2_w_up'] = _wgrad(du2, h2, "dw_ffn2_up", half, D_MODEL)
    grads['ffn2_w_down'] = _wgrad(act2, dx3, "dw_ffn2_down", half, D_MODEL)
    dyf, dys, dgf, dgs, dbf, dbs, merged = _merge_bwd(dx2, y_fox, y_sb, gf, gs, wbf, wbs, wo, "merge_bwd")
    grads['w_branch_fox'] = _wgrad(dbf, y_fox, "dw_branch_fox", D_MODEL, ATT_W)
    grads['w_branch_sb'] = _wgrad(dbs, y_sb, "dw_branch_sb", D_MODEL, ATT_W)
    grads['w_out'] = _wgrad(merged, dx2, "dw_out", D_MODEL, D_MODEL)
    dfqn, dfkn, dfv, dft, part_rest = _fox_bwd(fqn, fkn, fv, dyf, y_fox, lse, f_col, f_row, kmax, "fox_bwd",
                                               ride=(_pack_chunks(grads, 5), False))
    dsq, dsk, dsv = _sb_bwd(sq, sk, sv, dys, y_sb32, "sb_bwd")
    s_len = x.shape[0]
    df_col = jnp.pad(dft[:, :2, :].reshape(N_HEADS, s_len).T, ((0, 0), (0, LANES - N_HEADS)))
    dproj, dx1, dg_mix, dqg, dkg, dbias = _mix_bwd(
        x1, dx2, g_mix, w_in, fqr, fkr, dfqn, dfkn, qg, kg, dfv, df_col, logf, dsq, dsk, dsv, dgf, dgs, "mix_bwd")
    grads['w_in'] = _wgrad(dproj, hmix, "dw_in", IN_PAD // 3, D_MODEL)
    dg1, du1, act1, dx0, dg_ffn1, part_in0 = _ffn_bwd_fused(x, dx1, g_ffn1, g1, u1, got0, ffn1, "ffn1_bwd",
                                                            ride=(_pack_chunks(grads, 3), False))
    grads['ffn1_w_gate'], part_in1 = _wgrad(dg1, h1, "dw_ffn1_gate", half, D_MODEL,
                                            ride=(_pack_chunks(grads, 4), False))
    grads['ffn1_w_up'], part_gate = _wgrad(du1, h1, "dw_ffn1_up", half, D_MODEL, ride=(_pack_chunks(grads, 0), False))
    grads['ffn1_w_down'], part_up = _wgrad(act1, dx1, "dw_ffn1_down", half, D_MODEL,
                                           ride=(_pack_chunks(grads, 1), False))
    part_down = _exchange(_pack_chunks(grads, 2), "scatter_ffn1_down", False)

    fold = lambda a: a.reshape(N_HEADS, HEAD_DIM).sum(axis=0).reshape(1, HEAD_DIM)
    small_g = {'ffn1_norm': dg_ffn1, 'mix_norm': dg_mix, 'ffn2_norm': dg_ffn2, 'ple_norm': dg_ple,
               'q_norm': fold(dqg), 'k_norm': fold(dkg), 'forget_bias': dbias[:, :N_HEADS]}
    return loss[0, 0], dx0, (part_gate, part_up, part_down, part_in0, part_in1, part_rest), small_g


def kernel(x, p, ffn1_norm, ffn1_w_gate, ffn1_w_up, ffn1_w_down, mix_norm, w_in, forget_bias, q_norm, k_norm, w_branch_fox, w_branch_sb, w_out, ffn2_norm, ffn2_w_gate, ffn2_w_up, ffn2_w_down, ple_norm, w_ple_gate, w_ple_proj, loss_target, m_ffn1_norm, m_ffn1_w_gate, m_ffn1_w_up, m_ffn1_w_down, m_mix_norm, m_w_in, m_forget_bias, m_q_norm, m_k_norm, m_w_branch_fox, m_w_branch_sb, m_w_out, m_ffn2_norm, m_ffn2_w_gate, m_ffn2_w_up, m_ffn2_w_down, m_ple_norm, m_w_ple_gate, m_w_ple_proj, v_ffn1_norm, v_ffn1_w_gate, v_ffn1_w_up, v_ffn1_w_down, v_mix_norm, v_w_in, v_forget_bias, v_q_norm, v_k_norm, v_w_branch_fox, v_w_branch_sb, v_w_out, v_ffn2_norm, v_ffn2_w_gate, v_ffn2_w_up, v_ffn2_w_down, v_ple_norm, v_w_ple_gate, v_w_ple_proj):
    args = dict(locals())
    w = {n: args[n][0] for n in WEIGHT_NAMES}
    m = {n: args["m_" + n][0] for n in WEIGHT_NAMES}
    v = {n: args["v_" + n][0] for n in WEIGHT_NAMES}
    loss, dx, parts, small_g = _step(x[0], p[0, 0], loss_target[0], w)

    summed = {}
    for grp, part in enumerate(parts):
        s = _sum_parts(part, f"sum_grads_{grp}", SUM_TILE_ROWS[grp])
        summed.update({n: s for n in SCATTER_GROUPS[grp]})
    big = {}
    for n in WEIGHT_NAMES:
        if n not in SMALL_NAMES:
            g = _shard_grad(summed, n, w[n].shape)
            big[n] = (g,) + tuple(_adamw_shard(g, w[n], m[n], v[n], "adamw_" + n))
    small_parts = _exchange(_pack_small(small_g, loss), "gather_small", True)
    small, total_loss = _adamw_small(small_parts, *({n: args[pre + n] for n in SMALL_NAMES} for pre in ("", "m_", "v_")),
                                     "adamw_small")
    big.update(small)

    outs = [total_loss.reshape(()), dx.reshape(x.shape)]
    for kind in range(4):
        outs += [big[n][kind].reshape(args[n].shape) for n in WEIGHT_NAMES]
    return tuple(outs)
```

```python
import jax
import jax.numpy as jnp
from jax import lax
from jax.experimental import pallas as pl
from jax.experimental.pallas import tpu as pltpu

F32 = jnp.float32
BF16 = jnp.bfloat16

D_MODEL = 1024
D_FF = 2816
N_HEADS = 8
HEAD_DIM = 64
ATT_W = N_HEADS * HEAD_DIM
PLE_DIM = 256
EPS = 1e-6
N_DEV = 8
MESH = pl.DeviceIdType.MESH

LANES = 128
V7X_SCOPED_VMEM_BYTES = 56 * 1024 * 1024

C_FQ, C_FK, C_FV, C_FL = 0, 512, 1024, 1536
C_SQ, C_SK, C_SV, C_GF, C_GS = 1792, 2304, 2816, 3328, 4352
IN_PAD = 5376
IN_REAL = 5128
FL_REAL_END = 1544

ADAM_LR = 0.001
ADAM_B1 = 0.9
ADAM_B2 = 0.999
ADAM_EPS = 1e-08
ADAM_WD = 0.01
ADAM_STEP = 10

PACK_ROWS = {"ffn1_w_gate": 352, "ffn1_w_up": 352, "ffn1_w_down": 352, "w_in": 656, "w_branch_fox": 64,
             "w_branch_sb": 64, "w_out": 128, "ffn2_w_gate": 352, "ffn2_w_up": 352, "ffn2_w_down": 352,
             "w_ple_gate": 128, "w_ple_proj": 32}
GATHER_GROUPS = (
    ("ffn1_w_gate", "ffn1_w_up", "ffn1_w_down"),
    ("w_in", "w_branch_fox", "w_branch_sb", "w_out"),
    ("ffn2_w_gate", "ffn2_w_up", "ffn2_w_down", "w_ple_gate", "w_ple_proj"),
)
W_IN_HALVES = ((0, 336), (336, 656))
PACK_ROWS.update({f"w_in#{i}": hi - lo for i, (lo, hi) in enumerate(W_IN_HALVES)})
SCATTER_GROUPS = (
    ("ffn1_w_gate",), ("ffn1_w_up",), ("ffn1_w_down",),
    ("w_in#0",), ("w_in#1",),
    ("ffn2_w_gate", "ffn2_w_up", "ffn2_w_down", "w_ple_gate", "w_ple_proj", "w_branch_fox", "w_branch_sb", "w_out"),
)
SUM_TILE_ROWS = (352, 352, 352, 336, 320, 368)


def _offsets(groups):
    off = {}
    for grp in groups:
        o = 0
        for n in grp:
            off[n] = o
            o += PACK_ROWS[n]
    return off


GATHER_OFF = _offsets(GATHER_GROUPS)
SCATTER_OFF = _offsets(SCATTER_GROUPS)
W_IN_ROWS = 641

SMALL_ROWS = 8


def _cparams(*sem):
    return pltpu.CompilerParams(dimension_semantics=sem, vmem_limit_bytes=V7X_SCOPED_VMEM_BYTES)


def _dot(a, b):
    return jnp.dot(a, b, preferred_element_type=F32)


def _dot_nt(a, b):
    return lax.dot_general(a, b, (((1,), (1,)), ((), ())), preferred_element_type=F32)


def _dot_tn(a, b):
    return lax.dot_general(a, b, (((0,), (0,)), ((), ())), preferred_element_type=F32)


def _split(x, parts):
    out = []
    r = x
    for _ in range(parts):
        p = r.astype(BF16)
        out.append(p)
        r = r - p.astype(F32)
    return out


def _dot_split(x, m, parts):
    acc = None
    for p in _split(x, parts):
        t = _dot(p, m)
        acc = t if acc is None else acc + t
    return acc


def _dot_split_left(m, x, parts):
    acc = None
    for p in _split(x, parts):
        t = _dot(m, p)
        acc = t if acc is None else acc + t
    return acc


def _rms_rinv(xf):
    return lax.rsqrt(jnp.mean(xf * xf, axis=-1, keepdims=True) + EPS)


def _sigmoid(x):
    return 1.0 / (1.0 + jnp.exp(-x))


def _softplus_neg_abs(z):
    return jnp.log(1.0 + jnp.exp(-jnp.abs(z)))


FFN_SHARD = D_FF // N_DEV
FFN_CHUNK = 4


def _ffn_w_spec(blk, index_map):
    return pl.BlockSpec((FFN_CHUNK, FFN_SHARD, D_MODEL), lambda *g: (index_map(*g), blk, 0))


def _ffn_w(ref):
    return ref[...].reshape(FFN_CHUNK * FFN_SHARD, D_MODEL)


def _ffn_fwd(x, gain, wbuf, blks, name, ride=None):
    s_len = x.shape[0]
    ts = min(512, s_len)
    fc = FFN_CHUNK * FFN_SHARD
    nt, nc = s_len // ts, D_FF // fc

    def body(x_ref, gain_ref, wg_ref, wu_ref, wd_ref, y_ref, g_ref, u_ref, h_ref, acc_scr):
        j = pl.program_id(1)

        @pl.when(j == 0)
        def _():
            xf = x_ref[...]
            h_ref[...] = ((xf * _rms_rinv(xf)) * gain_ref[...]).astype(BF16)
            acc_scr[...] = jnp.zeros_like(acc_scr)

        h = h_ref[...]
        g = _dot_nt(h, _ffn_w(wg_ref))
        u = _dot_nt(h, _ffn_w(wu_ref))
        g_ref[...] = g.astype(BF16)
        u_ref[...] = u.astype(BF16)
        a = (g * _sigmoid(g) * u).astype(BF16)
        acc_scr[...] += _dot(a, _ffn_w(wd_ref))

        @pl.when(j == nc - 1)
        def _():
            y_ref[...] = x_ref[...] + 0.5 * acc_scr[...]

    tok = pl.BlockSpec((ts, D_MODEL), lambda i, j: (i, 0))
    hid = pl.BlockSpec((ts, fc), lambda i, j: (i, j))
    return _ride_call(
        body, name, (nt, nc),
        [tok, pl.BlockSpec((1, D_MODEL), lambda i, j: (0, 0))] + [_ffn_w_spec(b, lambda i, j: j) for b in blks],
        [tok, hid, hid, tok],
        [jax.ShapeDtypeStruct((s_len, D_MODEL), F32), jax.ShapeDtypeStruct((s_len, D_FF), BF16),
         jax.ShapeDtypeStruct((s_len, D_FF), BF16), jax.ShapeDtypeStruct((s_len, D_MODEL), BF16)],
        [pltpu.VMEM((ts, D_MODEL), F32)], ("parallel", "arbitrary"), (x, gain, wbuf, wbuf, wbuf), ride)


def _ffn_bwd_fused(x, dy, gain, g, u, wbuf, blks, name, ride=None):
    s_len = x.shape[0]
    ts = min(512, s_len)
    fc = FFN_CHUNK * FFN_SHARD
    nt = s_len // ts
    assert D_FF == 2 * fc

    def hidden(dy_ref, g_ref, u_ref, wg_ref, wu_ref, wd_ref, dg_ref, du_ref, act_ref):
        da = 0.5 * _dot_nt(dy_ref[...].astype(BF16), _ffn_w(wd_ref))
        gf = g_ref[...].astype(F32)
        uf = u_ref[...].astype(F32)
        sg = _sigmoid(gf)
        silu = gf * sg
        dg = (da * uf * (sg * (1.0 + gf * (1.0 - sg)))).astype(BF16)
        du = (da * silu).astype(BF16)
        dg_ref[...] = dg
        du_ref[...] = du
        act_ref[...] = (0.5 * silu * uf).astype(BF16)
        return _dot(dg, _ffn_w(wg_ref)) + _dot(du, _ffn_w(wu_ref))

    def first(dy_ref, g_ref, u_ref, wg_ref, wu_ref, wd_ref, dg_ref, du_ref, act_ref, dh_ref):
        dh_ref[...] = hidden(dy_ref, g_ref, u_ref, wg_ref, wu_ref, wd_ref, dg_ref, du_ref, act_ref)

    def second(x_ref, dy_ref, gain_ref, g_ref, u_ref, wg_ref, wu_ref, wd_ref, dh0_ref, dg_half, du_half, act_half,
               dg_ref, du_ref, act_ref, dx_ref, dgain_ref):
        i = pl.program_id(0)
        dh = dh0_ref[...] + hidden(dy_ref, g_ref, u_ref, wg_ref, wu_ref, wd_ref, dg_ref, du_ref, act_ref)
        xf = x_ref[...]
        r = _rms_rinv(xf)
        xhat = xf * r
        dgp = jnp.sum(dh * xhat, axis=0, keepdims=True)

        @pl.when(i == 0)
        def _():
            dgain_ref[...] = dgp

        @pl.when(i > 0)
        def _():
            dgain_ref[...] += dgp

        dn = dh * gain_ref[...]
        dx_ref[...] = dy_ref[...] + r * (dn - xhat * jnp.mean(dn * xhat, axis=-1, keepdims=True))

    tok = pl.BlockSpec((ts, D_MODEL), lambda i: (i, 0))
    row = pl.BlockSpec((1, D_MODEL), lambda i: (0, 0))
    hid = lambda c: pl.BlockSpec((ts, fc), lambda i: (i, c))
    wts = lambda c: [pl.BlockSpec((FFN_CHUNK, FFN_SHARD, D_MODEL), lambda i, b=b: (c, b, 0),
                                  pipeline_mode=pl.Buffered(1)) for b in blks]
    hidden_shapes = [jax.ShapeDtypeStruct((s_len, D_FF), BF16)] * 3
    dg, du, act, dh0, *rode = _ride_call(
        first, name + "_a", (nt,), [tok, hid(0), hid(0)] + wts(0), [hid(0)] * 3 + [tok],
        hidden_shapes + [jax.ShapeDtypeStruct((s_len, D_MODEL), F32)], [], ("parallel",),
        (dy, g, u, wbuf, wbuf, wbuf), ride)
    filled = pl.BlockSpec(memory_space=pl.ANY)
    dg, du, act, dx, dgain = pl.pallas_call(
        second, name=name + "_b", grid=(nt,),
        in_specs=[tok, tok, row, hid(1), hid(1)] + wts(1) + [tok, filled, filled, filled],
        out_specs=[hid(1)] * 3 + [tok, row],
        out_shape=hidden_shapes + [jax.ShapeDtypeStruct((s_len, D_MODEL), F32), jax.ShapeDtypeStruct((1, D_MODEL), F32)],
        input_output_aliases={9: 0, 10: 1, 11: 2},
        compiler_params=_cparams("arbitrary"),
    )(x, dy, gain, g, u, wbuf, wbuf, wbuf, dh0, dg, du, act)
    return (dg, du, act, dx, dgain, *rode)


def _wgrad(a, b, name, tk, tn, ride=None):
    s_len, k_dim = a.shape
    n_dim = b.shape[1]
    ts = min(2048, s_len)
    ns = s_len // ts

    def body(a_ref, b_ref, o_ref, acc):
        s = pl.program_id(2)
        p = _dot_tn(a_ref[...].astype(BF16), b_ref[...].astype(BF16))

        @pl.when(s == 0)
        def _():
            acc[...] = p

        @pl.when(s > 0)
        def _():
            acc[...] += p

        @pl.when(s == ns - 1)
        def _():
            o_ref[...] = acc[...].astype(BF16)

    out = _ride_call(
        body, name, (k_dim // tk, n_dim // tn, ns),
        [pl.BlockSpec((ts, tk), lambda k, n, s: (s, k)), pl.BlockSpec((ts, tn), lambda k, n, s: (s, n))],
        [pl.BlockSpec((tk, tn), lambda k, n, s: (k, n))], [jax.ShapeDtypeStruct((k_dim, n_dim), BF16)],
        [pltpu.VMEM((tk, tn), F32)], ("parallel", "parallel", "arbitrary"), (a, b), ride)
    return out[0] if ride is None else tuple(out)


HEAD_SUM_PARTS = 1


def _head_group_matrix():
    r = lax.broadcasted_iota(jnp.int32, (ATT_W, ATT_W), 0) // HEAD_DIM
    c = lax.broadcasted_iota(jnp.int32, (ATT_W, ATT_W), 1) // HEAD_DIM
    return (r == c).astype(BF16)


def _mix_fwd(x, gain, w_in, bias, qg, kg, name):
    s_len = x.shape[0]
    ts = min(512, s_len)
    nt = s_len // ts
    gmat = _head_group_matrix()

    def body(x_ref, gain_ref, w_ref, bias_ref, qg_ref, kg_ref, gm_ref,
             h_ref, fqr_ref, fkr_ref, fqn_ref, fkn_ref, fv_ref, logf_ref, f_ref, ft_ref,
             sq_ref, sk_ref, sv_ref, gf_ref, gs_ref, kmax_ref, carry):
        i = pl.program_id(0)
        xf = x_ref[...]
        h = ((xf * _rms_rinv(xf)) * gain_ref[...]).astype(BF16)
        h_ref[...] = h
        gm = gm_ref[...]

        def proj(lo, n):
            return _dot_nt(h, w_ref[lo:lo + n, :])

        def headnorm(raw, g):
            ms = _dot_split(raw * raw, gm, HEAD_SUM_PARTS) * (1.0 / HEAD_DIM)
            return ((raw * lax.rsqrt(ms + EPS)) * g).astype(BF16)

        fq = proj(C_FQ, ATT_W)
        fqr_ref[...] = fq
        fqn_ref[...] = headnorm(fq, qg_ref[...])
        fk = proj(C_FK, ATT_W)
        fkr_ref[...] = fk
        fkn = headnorm(fk, kg_ref[...])
        fkn_ref[...] = fkn
        kn2 = jnp.max(_dot_split(jnp.square(fkn.astype(F32)), gm, HEAD_SUM_PARTS), axis=0, keepdims=True)

        @pl.when(i == 0)
        def _():
            kmax_ref[...] = kn2

        @pl.when(i > 0)
        def _():
            kmax_ref[...] = jnp.maximum(kmax_ref[...], kn2)
        fv_ref[...] = proj(C_FV, ATT_W).astype(BF16)
        sq_ref[...] = proj(C_SQ, ATT_W).astype(BF16)
        sk_ref[...] = proj(C_SK, ATT_W).astype(BF16)
        sv_ref[...] = proj(C_SV, ATT_W).astype(BF16)
        gf_ref[...] = proj(C_GF, D_MODEL)
        gs_ref[...] = proj(C_GS, D_MODEL)

        fl = proj(C_FL, LANES) + bias_ref[...]
        lane = lax.broadcasted_iota(jnp.int32, fl.shape, 1)
        logf = jnp.where(lane < N_HEADS, jnp.minimum(fl, 0.0) - _softplus_neg_abs(fl), 0.0)
        logf_ref[...] = logf

        @pl.when(i == 0)
        def _():
            carry[...] = jnp.zeros_like(carry)

        r = lax.broadcasted_iota(jnp.int32, (ts, ts), 0)
        c = lax.broadcasted_iota(jnp.int32, (ts, ts), 1)
        tri = (r >= c).astype(BF16)
        f_tile = _dot_split_left(tri, logf, 3) + carry[...]
        f_ref[...] = f_tile
        ft_ref[...] = f_tile.T[:N_HEADS, :]
        carry[...] = f_tile[ts - 1:ts, :]

    tok = lambda w: pl.BlockSpec((ts, w), lambda i: (i, 0))
    full = lambda a: pl.BlockSpec(a.shape, lambda i: (0, 0), pipeline_mode=pl.Buffered(1))
    f32o = lambda w: jax.ShapeDtypeStruct((s_len, w), F32)
    b16o = lambda w: jax.ShapeDtypeStruct((s_len, w), BF16)
    return _ride_call(
        body, name, (nt,),
        [tok(D_MODEL), full(gain), full(w_in), full(bias), full(qg), full(kg), full(gmat)],
        [
            tok(D_MODEL), tok(ATT_W), tok(ATT_W), tok(ATT_W), tok(ATT_W), tok(ATT_W), tok(LANES), tok(LANES),
            pl.BlockSpec((N_HEADS, ts), lambda i: (0, i)),
            tok(ATT_W), tok(ATT_W), tok(ATT_W), tok(D_MODEL), tok(D_MODEL),
            pl.BlockSpec((1, ATT_W), lambda i: (0, 0)),
        ],
        [
            b16o(D_MODEL), f32o(ATT_W), f32o(ATT_W), b16o(ATT_W), b16o(ATT_W), b16o(ATT_W), f32o(LANES), f32o(LANES),
            jax.ShapeDtypeStruct((N_HEADS, s_len), F32),
            b16o(ATT_W), b16o(ATT_W), b16o(ATT_W), f32o(D_MODEL), f32o(D_MODEL),
            jax.ShapeDtypeStruct((1, ATT_W), F32),
        ],
        [pltpu.VMEM((1, LANES), F32)], ("arbitrary",), (x, gain, w_in, bias, qg, kg, gmat), None)


ATT_T = 256
SB_ROWS = 2
FOX_ROWS = 2
EXP_ZERO = 88.0


def _att_tiling(s_len, rows):
    t = min(ATT_T, s_len)
    nr = min(rows, s_len // t)
    return t, nr, s_len // (t * nr)


def _pair_specs(s_len, tq):
    qblk = pl.BlockSpec((tq, LANES), lambda hp, i: (i, hp))
    kvfull = pl.BlockSpec((s_len, LANES), lambda hp, i: (0, hp))
    return qblk, kvfull


def _walk_tiles(i, nr, load, sub, flush, init, more=None, trips=None):
    base = i * nr
    carries = list(init)
    for kk in range(nr - 1, -1, -1):
        rs = list(range(kk, nr))
        new, side = sub(rs, load(base + kk), [carries[r] for r in rs], [r == kk for r in rs])
        carries[kk:] = new
        flush(base + kk, side)

    done = jnp.int32(0)
    for last in range(nr - 1, -1, -1):
        rs = list(range(last + 1))

        def visit(n, cs, rs=rs):
            kb = base - 1 - n
            cs, side = sub(rs, load(kb), list(cs), [False] * len(rs))
            flush(kb, side)
            return tuple(cs)

        if trips is not None:
            todo = jnp.maximum(trips(carries, base, last) - done, 0)
            new = lax.fori_loop(0, todo, lambda it, cs, start=done, visit=visit: visit(start + it, cs),
                                tuple(carries[:last + 1]))
            done = done + todo
        else:
            def step(state, visit=visit, last=last):
                n, _, cs = state
                cs = visit(n, cs)
                return n + 1, more(cs[last]), cs

            done, _, new = lax.while_loop(lambda state: jnp.logical_and(state[0] < base, state[1] > 0), step,
                                          (done, more(carries[last]), tuple(carries[:last + 1])))
        carries[:last + 1] = list(new)
    return carries


def _stack(parts):
    return parts[0] if len(parts) == 1 else jnp.concatenate(parts, axis=0)


def _stacked_halves(x, lo):
    z = jnp.zeros_like(x)
    return jnp.concatenate([jnp.where(lo, x, z), jnp.where(lo, z, x)], axis=0)


def _fox_qk_bound(qst_r, km_ref, t):
    km = km_ref[...]
    out = []
    for j in (0, 1):
        qf = qst_r[j * t:(j + 1) * t, :].astype(F32)
        qn = jnp.sqrt(jnp.sum(qf * qf, axis=1, keepdims=True))
        out.append(qn * jnp.sqrt(km[:, j * HEAD_DIM:j * HEAD_DIM + 1]) * 1.001 + 1.0)
    return out


def _fox_trips(hp, flast_ref, qkb, fq, level):
    def trips(carries, base, r):
        gap = [jnp.max(qkb[r][j] + fq[r][j] - level(carries, r, j)) for j in (0, 1)]

        def needed(n):
            kb = jnp.maximum(base - 1 - n, 0)
            return jnp.logical_or(gap[0] - flast_ref[2 * hp, kb] > -EXP_ZERO,
                                  gap[1] - flast_ref[2 * hp + 1, kb] > -EXP_ZERO)

        return lax.while_loop(lambda n: jnp.logical_and(n < base, needed(n)), lambda n: n + 1, jnp.int32(0))
    return trips


def _fox_fwd(q, k, v, f_col, f_row, kmax, name, ride=None):
    s_len = q.shape[0]
    t, nr, nq = _att_tiling(s_len, FOX_ROWS)

    def body(q_ref, k_ref, v_ref, f_ref, ft_ref, km_ref, fl_ref, y_ref, lse_ref):
        hp = pl.program_id(0)
        i = pl.program_id(1)
        lane = lax.broadcasted_iota(jnp.int32, (t, LANES), 1)
        lo = lane < HEAD_DIM
        causal = lax.broadcasted_iota(jnp.int32, (t, t), 0) >= lax.broadcasted_iota(jnp.int32, (t, t), 1)
        rows = [pl.ds(r * t, t) for r in range(nr)]
        qst = [_stacked_halves(q_ref[rw, :] * jnp.asarray(HEAD_DIM ** -0.5, BF16), lo) for rw in rows]
        q_all = _stack(qst)
        fq = [[jnp.sum(jnp.where(lane == 2 * hp + j, f_ref[rw, :], 0.0), axis=1, keepdims=True) for j in (0, 1)]
              for rw in rows]

        def load(kb):
            k0 = pl.multiple_of(kb * t, t)
            frow = [ft_ref[pl.ds(2 * hp + j, 1), pl.ds(k0, t)] for j in (0, 1)]
            return k_ref[pl.ds(k0, t), :], v_ref[pl.ds(k0, t), :], frow

        def sub(rs, tiles, carries, masked):
            kblk, vblk, frow = tiles
            z = _dot_nt(q_all if len(rs) == nr else _stack([qst[r] for r in rs]), kblk)
            ps, stats = [], []
            for n, (r, j) in enumerate((r, j) for r in range(len(rs)) for j in (0, 1)):
                m, l, _ = carries[r]
                s = z[n * t:(n + 1) * t, :] + (fq[rs[r]][j] - frow[j])
                if masked[r]:
                    s = jnp.where(causal, s, -1e30)
                mj = jnp.maximum(m[j], jnp.max(s, axis=1, keepdims=True))
                aj = jnp.exp(m[j] - mj)
                p = jnp.exp(s - mj)
                stats.append((mj, aj, aj * l[j] + jnp.sum(p, axis=1, keepdims=True)))
                ps.append(p.astype(BF16))
            pv = _dot(_stack(ps), vblk)
            out = []
            for r in range(len(rs)):
                (m0, a0, l0), (m1, a1, l1) = stats[2 * r], stats[2 * r + 1]
                acc = carries[r][2]
                acc = (acc[0] * a0 + pv[2 * r * t:(2 * r + 1) * t, :], acc[1] * a1 + pv[(2 * r + 1) * t:(2 * r + 2) * t, :])
                out.append(((m0, m1), (l0, l1), acc))
            return out, None

        neg = jnp.full((t, 1), -1e30, F32)
        zero = jnp.zeros((t, 1), F32)
        zacc = jnp.zeros((t, LANES), F32)
        init = [((neg, neg), (zero, zero), (zacc, zacc))] * nr
        qkb = [_fox_qk_bound(qs, km_ref, t) for qs in qst]
        trips = _fox_trips(hp, fl_ref, qkb, fq, lambda carries, r, j: carries[r][0][j])
        out = _walk_tiles(i, nr, load, sub, lambda kb, side: None, init, trips=trips)
        for rw, (m, l, acc) in zip(rows, out):
            y_ref[rw, :] = jnp.where(lo, acc[0] / l[0], acc[1] / l[1]).astype(BF16)
            lse_ref[0, rw, :] = jnp.where(lo, m[0] + jnp.log(l[0]), m[1] + jnp.log(l[1]))

    qblk, kvfull = _pair_specs(s_len, t * nr)
    return _ride_call(
        body, name, (N_HEADS // 2, nq),
        [qblk, kvfull, kvfull,
         pl.BlockSpec((t * nr, LANES), lambda hp, i: (i, 0)),
         pl.BlockSpec((N_HEADS, s_len), lambda hp, i: (0, 0)),
         pl.BlockSpec((1, LANES), lambda hp, i: (0, hp)),
         pl.BlockSpec(memory_space=pltpu.SMEM)],
        [qblk, pl.BlockSpec((1, t * nr, LANES), lambda hp, i: (hp, i, 0))],
        [jax.ShapeDtypeStruct((s_len, ATT_W), BF16), jax.ShapeDtypeStruct((N_HEADS // 2, s_len, LANES), F32)],
        [], ("parallel", "parallel"), (q, k, v, f_col, f_row, kmax, f_row[:, t - 1::t]), ride)


def _fox_bwd(q, k, v, dy, y, lse, f_col, f_row, kmax, name, ride=None):
    s_len = q.shape[0]
    t, nr, nq = _att_tiling(s_len, FOX_ROWS)

    def body(q_ref, k_ref, v_ref, dy_ref, y_ref, lse_ref, f_ref, ft_ref, km_ref, fl_ref,
             dq_ref, dk_ref, dv_ref, dft_ref):
        hp = pl.program_id(0)
        i = pl.program_id(1)

        @pl.when(i == 0)
        def _():
            dk_ref[...] = jnp.zeros_like(dk_ref)
            dv_ref[...] = jnp.zeros_like(dv_ref)
            dft_ref[...] = jnp.zeros_like(dft_ref)

        lane = lax.broadcasted_iota(jnp.int32, (t, LANES), 1)
        lo = lane < HEAD_DIM
        causal = lax.broadcasted_iota(jnp.int32, (t, t), 0) >= lax.broadcasted_iota(jnp.int32, (t, t), 1)
        rows = [pl.ds(r * t, t) for r in range(nr)]
        qst, dyst, delta, lse, fq = [], [], [], [], []
        for rw in rows:
            qst.append(_stacked_halves(q_ref[rw, :] * jnp.asarray(HEAD_DIM ** -0.5, BF16), lo))
            dyb = dy_ref[rw, :]
            dyst.append(_stacked_halves(dyb, lo))
            prod = dyb.astype(F32) * y_ref[rw, :].astype(F32)
            delta.append([jnp.sum(jnp.where(lo, prod, 0.0), axis=1, keepdims=True),
                          jnp.sum(jnp.where(lo, 0.0, prod), axis=1, keepdims=True)])
            lse_b = lse_ref[0, rw, :]
            lse.append([lse_b[:, 0:1], lse_b[:, HEAD_DIM:HEAD_DIM + 1]])
            fq.append([jnp.sum(jnp.where(lane == 2 * hp + j, f_ref[rw, :], 0.0), axis=1, keepdims=True)
                       for j in (0, 1)])

        q_all, dy_all = _stack(qst), _stack(dyst)

        def load(kb):
            k0 = pl.multiple_of(kb * t, t)
            frow = [ft_ref[pl.ds(2 * hp + j, 1), pl.ds(k0, t)] for j in (0, 1)]
            return k_ref[pl.ds(k0, t), :], v_ref[pl.ds(k0, t), :], frow

        def sub(rs, tiles, carries, masked):
            kblk, vblk, frow = tiles
            qs, dys = (q_all, dy_all) if len(rs) == nr else (_stack([qst[r] for r in rs]), _stack([dyst[r] for r in rs]))
            z = _dot_nt(qs, kblk)
            dp = _dot_nt(dys, vblk)
            pb, dsb, rsum, col = [], [], [], [None, None]
            for n, (r, j) in enumerate((r, j) for r in range(len(rs)) for j in (0, 1)):
                sl = slice(n * t, (n + 1) * t)
                s = z[sl, :] + (fq[rs[r]][j] - frow[j])
                p = jnp.exp(s - lse[rs[r]][j])
                if masked[r]:
                    p = jnp.where(causal, p, 0.0)
                ds = p * (dp[sl, :] - delta[rs[r]][j])
                c = jnp.sum(ds, axis=0, keepdims=True)
                col[j] = c if col[j] is None else col[j] + c
                rsum.append(carries[r][1][j] + jnp.sum(ds, axis=1, keepdims=True))
                pb.append(p.astype(BF16))
                dsb.append(ds.astype(BF16))
            p_all, ds_all = _stack(pb), _stack(dsb)
            dqs = _dot(ds_all, kblk)
            out = []
            for r in range(len(rs)):
                dq = carries[r][0]
                dq = (dq[0] + dqs[2 * r * t:(2 * r + 1) * t, :], dq[1] + dqs[(2 * r + 1) * t:(2 * r + 2) * t, :])
                out.append((dq, (rsum[2 * r], rsum[2 * r + 1])))
            return out, (_dot_tn(ds_all, qs), _dot_tn(p_all, dys), col)

        def flush(kb, side):
            k0 = pl.multiple_of(kb * t, t)
            dk_ref[pl.ds(k0, t), :] += side[0]
            dv_ref[pl.ds(k0, t), :] += side[1]
            for j in (0, 1):
                dft_ref[0, pl.ds(j, 1), pl.ds(k0, t)] -= side[2][j]

        zero = jnp.zeros((t, 1), F32)
        zacc = jnp.zeros((t, LANES), F32)
        qkb = [_fox_qk_bound(qs, km_ref, t) for qs in qst]
        trips = _fox_trips(hp, fl_ref, qkb, fq, lambda carries, r, j: lse[r][j])
        out = _walk_tiles(i, nr, load, sub, flush, [((zacc, zacc), (zero, zero))] * nr, trips=trips)
        for r, (rw, (dq, rs)) in enumerate(zip(rows, out)):
            dq_ref[rw, :] = jnp.where(lo, dq[0], dq[1]) * (HEAD_DIM ** -0.5)
            rs_t = jnp.where(lo, rs[0], rs[1]).T
            q0 = pl.multiple_of((i * nr + r) * t, t)
            for j in (0, 1):
                dft_ref[0, pl.ds(j, 1), pl.ds(q0, t)] += rs_t[j * HEAD_DIM:j * HEAD_DIM + 1, :]

    qblk, kvfull = _pair_specs(s_len, t * nr)
    return _ride_call(
        body, name, (N_HEADS // 2, nq),
        [qblk, kvfull, kvfull, qblk, qblk,
         pl.BlockSpec((1, t * nr, LANES), lambda hp, i: (hp, i, 0)),
         pl.BlockSpec((t * nr, LANES), lambda hp, i: (i, 0)),
         pl.BlockSpec((N_HEADS, s_len), lambda hp, i: (0, 0)),
         pl.BlockSpec((1, LANES), lambda hp, i: (0, hp)),
         pl.BlockSpec(memory_space=pltpu.SMEM)],
        [qblk, kvfull, kvfull, pl.BlockSpec((1, 8, s_len), lambda hp, i: (hp, 0, 0))],
        [jax.ShapeDtypeStruct((s_len, ATT_W), F32)] * 3 + [jax.ShapeDtypeStruct((N_HEADS // 2, 8, s_len), F32)],
        [], ("arbitrary", "arbitrary"), (q, k, v, dy, y, lse, f_col, f_row, kmax, f_row[:, t - 1::t]), ride)


def _sb_more(carry):
    return (jnp.max(jnp.maximum(carry[0][0], carry[0][1])) > -EXP_ZERO).astype(jnp.int32)


def _stacked_split_dot(slabs, m, parts):
    split = [_split(x, parts) for x in slabs]
    acc = None
    for p in range(parts):
        d = _dot(_stack([s[p] for s in split]), m)
        acc = d if acc is None else acc + d
    return acc


def _sb_weights(z, c, strict, upper, t):
    logs = []
    for n in range(z.shape[0] // t):
        zn = z[n * t:(n + 1) * t, :]
        sp = _softplus_neg_abs(zn)
        l1m = jnp.minimum(-zn, 0.0) - sp
        if strict[n] is not None:
            l1m = jnp.where(strict[n], l1m, 0.0)
        logs.append((jnp.minimum(zn, 0.0) - sp, l1m))
    suf = _stacked_split_dot([l1m for _, l1m in logs], upper, 1)
    out = []
    for n, (logb, l1m) in enumerate(logs):
        after = c[n] + suf[n * t:(n + 1) * t, :]
        a = jnp.exp(logb + after)
        if strict[n] is not None:
            a = jnp.where(strict[n], a, 0.0)
        out.append((logb, a, after[:, 0:1] + l1m[:, 0:1]))
    return out


def _sb_fwd(q, k, v, name):
    s_len = q.shape[0]
    t, nr, nq = _att_tiling(s_len, SB_ROWS)

    def body(q_ref, k_ref, v_ref, y_ref, yf_ref):
        i = pl.program_id(1)
        lane = lax.broadcasted_iota(jnp.int32, (t, LANES), 1)
        lo = lane < HEAD_DIM
        ri = lax.broadcasted_iota(jnp.int32, (t, t), 0)
        ci = lax.broadcasted_iota(jnp.int32, (t, t), 1)
        strict = ci < ri
        upper = (ri > ci).astype(BF16)
        rows = [pl.ds(r * t, t) for r in range(nr)]
        qst = [_stacked_halves(q_ref[rw, :] * jnp.asarray(HEAD_DIM ** -0.5, BF16), lo) for rw in rows]
        q_all = _stack(qst)

        def load(kb):
            k0 = pl.multiple_of(kb * t, t)
            return k_ref[pl.ds(k0, t), :], v_ref[pl.ds(k0, t), :]

        def sub(rs, tiles, carries, masked):
            kblk, vblk = tiles
            z = _dot_nt(q_all if len(rs) == nr else _stack([qst[r] for r in rs]), kblk)
            c = [carries[r][0][j] for r in range(len(rs)) for j in (0, 1)]
            w = _sb_weights(z, c, [strict if masked[r] else None for r in range(len(rs)) for j in (0, 1)], upper, t)
            pv = _dot(_stack([a.astype(BF16) for _, a, _ in w]), vblk)
            out = []
            for r in range(len(rs)):
                acc = carries[r][1]
                acc = (acc[0] + pv[2 * r * t:(2 * r + 1) * t, :], acc[1] + pv[(2 * r + 1) * t:(2 * r + 2) * t, :])
                out.append(((w[2 * r][2], w[2 * r + 1][2]), acc))
            return out, None

        zero = jnp.zeros((t, 1), F32)
        zacc = jnp.zeros((t, LANES), F32)
        out = _walk_tiles(i, nr, load, sub, lambda kb, side: None, [((zero, zero), (zacc, zacc))] * nr, more=_sb_more)
        for rw, (_, acc) in zip(rows, out):
            y = jnp.where(lo, acc[0], acc[1])
            y_ref[rw, :] = y.astype(BF16)
            yf_ref[rw, :] = y

    qblk, kvfull = _pair_specs(s_len, t * nr)
    return pl.pallas_call(
        body, name=name, grid=(N_HEADS // 2, nq),
        in_specs=[qblk, kvfull, kvfull],
        out_specs=[qblk, qblk],
        out_shape=[jax.ShapeDtypeStruct((s_len, ATT_W), BF16), jax.ShapeDtypeStruct((s_len, ATT_W), F32)],
        compiler_params=_cparams("parallel", "parallel"),
    )(q, k, v)


def _sb_bwd(q, k, v, dy, yf, name):
    s_len = q.shape[0]
    t, nr, nq = _att_tiling(s_len, SB_ROWS)

    def body(q_ref, k_ref, v_ref, dy_ref, yf_ref, dq_ref, dk_ref, dv_ref):
        i = pl.program_id(1)

        @pl.when(i == 0)
        def _():
            dk_ref[...] = jnp.zeros_like(dk_ref)
            dv_ref[...] = jnp.zeros_like(dv_ref)

        lane = lax.broadcasted_iota(jnp.int32, (t, LANES), 1)
        lo = lane < HEAD_DIM
        ri = lax.broadcasted_iota(jnp.int32, (t, t), 0)
        ci = lax.broadcasted_iota(jnp.int32, (t, t), 1)
        strict = ci < ri
        upper = (ri > ci).astype(BF16)
        upper_incl = (ri >= ci).astype(BF16)
        rows = [pl.ds(r * t, t) for r in range(nr)]
        qst, dyst, delta = [], [], []
        for rw in rows:
            qst.append(_stacked_halves(q_ref[rw, :] * jnp.asarray(HEAD_DIM ** -0.5, BF16), lo))
            dyb = dy_ref[rw, :]
            dyst.append(_stacked_halves(dyb, lo))
            prod = dyb.astype(F32) * yf_ref[rw, :]
            delta.append([jnp.sum(jnp.where(lo, prod, 0.0), axis=1, keepdims=True),
                          jnp.sum(jnp.where(lo, 0.0, prod), axis=1, keepdims=True)])
        q_all, dy_all = _stack(qst), _stack(dyst)

        def load(kb):
            k0 = pl.multiple_of(kb * t, t)
            return k_ref[pl.ds(k0, t), :], v_ref[pl.ds(k0, t), :]

        def sub(rs, tiles, carries, masked):
            kblk, vblk = tiles
            qs, dys = (q_all, dy_all) if len(rs) == nr else (_stack([qst[r] for r in rs]), _stack([dyst[r] for r in rs]))
            slabs = [(r, j) for r in range(len(rs)) for j in (0, 1)]
            z = _dot_nt(qs, kblk)
            w = _sb_weights(z, [carries[r][0][j] for r, j in slabs], [strict if masked[r] else None for r, j in slabs],
                            upper, t)
            da = _dot_nt(dys, vblk)
            ab = [a.astype(BF16) for _, a, _ in w]
            dl = [ab[n].astype(F32) * da[n * t:(n + 1) * t, :] for n in range(len(slabs))]
            tail = _stacked_split_dot(dl, upper_incl, 2)
            dzb, e_new = [], []
            for n, (r, j) in enumerate(slabs):
                tl = tail[n * t:(n + 1) * t, :]
                e = carries[r][1][j]
                dl1m = (delta[rs[r]][j] - e) - tl
                e_new.append(e + tl[:, 0:1])
                dz = dl[n] - jnp.exp(w[n][0]) * (dl[n] + dl1m)
                if masked[r]:
                    dz = jnp.where(strict, dz, 0.0)
                dzb.append(dz.astype(BF16))
            a_all, dz_all = _stack(ab), _stack(dzb)
            dqs = _dot(dz_all, kblk)
            out = []
            for r in range(len(rs)):
                dq = carries[r][2]
                dq = (dq[0] + dqs[2 * r * t:(2 * r + 1) * t, :], dq[1] + dqs[(2 * r + 1) * t:(2 * r + 2) * t, :])
                out.append(((w[2 * r][2], w[2 * r + 1][2]), (e_new[2 * r], e_new[2 * r + 1]), dq))
            return out, (_dot_tn(dz_all, qs), _dot_tn(a_all, dys))

        def flush(kb, side):
            k0 = pl.multiple_of(kb * t, t)
            dk_ref[pl.ds(k0, t), :] += side[0]
            dv_ref[pl.ds(k0, t), :] += side[1]

        zero = jnp.zeros((t, 1), F32)
        zacc = jnp.zeros((t, LANES), F32)
        out = _walk_tiles(i, nr, load, sub, flush, [((zero, zero), (zero, zero), (zacc, zacc))] * nr, more=_sb_more)
        for rw, (_, _, dq) in zip(rows, out):
            dq_ref[rw, :] = jnp.where(lo, dq[0], dq[1]) * (HEAD_DIM ** -0.5)

    qblk, kvfull = _pair_specs(s_len, t * nr)
    return pl.pallas_call(
        body, name=name, grid=(N_HEADS // 2, nq),
        in_specs=[qblk, kvfull, kvfull, qblk, qblk],
        out_specs=[qblk, kvfull, kvfull],
        out_shape=[jax.ShapeDtypeStruct((s_len, ATT_W), F32)] * 3,
        compiler_params=_cparams("arbitrary", "arbitrary"),
    )(q, k, v, dy, yf)


def _merge_fwd(x, yf, ys, gf, gs, wbf, wbs, wo, name):
    s_len = x.shape[0]
    ts = min(512, s_len)

    def body(x_ref, yf_ref, ys_ref, gf_ref, gs_ref, wbf_ref, wbs_ref, wo_ref, o_ref):
        merged = (_sigmoid(gf_ref[...]) * _dot_nt(yf_ref[...], wbf_ref[...])
                  + _sigmoid(gs_ref[...]) * _dot_nt(ys_ref[...], wbs_ref[...]))
        o_ref[...] = x_ref[...] + _dot(merged.astype(BF16), wo_ref[...])

    tok = lambda w: pl.BlockSpec((ts, w), lambda i: (i, 0))
    full = lambda a: pl.BlockSpec(a.shape, lambda i: (0, 0))
    return pl.pallas_call(
        body, name=name, grid=(s_len // ts,),
        in_specs=[tok(D_MODEL), tok(ATT_W), tok(ATT_W), tok(D_MODEL), tok(D_MODEL), full(wbf), full(wbs), full(wo)],
        out_specs=tok(D_MODEL),
        out_shape=jax.ShapeDtypeStruct((s_len, D_MODEL), F32),
        compiler_params=_cparams("parallel"),
    )(x, yf, ys, gf, gs, wbf, wbs, wo)


def _merge_bwd(dx, yf, ys, gf, gs, wbf, wbs, wo, name):
    s_len = dx.shape[0]
    ts = min(512, s_len)

    def body(dx_ref, yf_ref, ys_ref, gf_ref, gs_ref, wbf_ref, wbs_ref, wo_ref,
             dyf_ref, dys_ref, dgf_ref, dgs_ref, dbf_ref, dbs_ref, mg_ref):
        bf = _dot_nt(yf_ref[...], wbf_ref[...])
        bs = _dot_nt(ys_ref[...], wbs_ref[...])
        sf = _sigmoid(gf_ref[...])
        ss = _sigmoid(gs_ref[...])
        mg_ref[...] = (sf * bf + ss * bs).astype(BF16)
        dm = _dot_nt(dx_ref[...].astype(BF16), wo_ref[...])
        dbf = (dm * sf).astype(BF16)
        dbs = (dm * ss).astype(BF16)
        dbf_ref[...] = dbf
        dbs_ref[...] = dbs
        dgf_ref[...] = (dm * bf * (sf * (1.0 - sf))).astype(BF16)
        dgs_ref[...] = (dm * bs * (ss * (1.0 - ss))).astype(BF16)
        dyf_ref[...] = _dot(dbf, wbf_ref[...]).astype(BF16)
        dys_ref[...] = _dot(dbs, wbs_ref[...]).astype(BF16)

    tok = lambda w: pl.BlockSpec((ts, w), lambda i: (i, 0))
    full = lambda a: pl.BlockSpec(a.shape, lambda i: (0, 0))
    b16o = lambda w: jax.ShapeDtypeStruct((s_len, w), BF16)
    return pl.pallas_call(
        body, name=name, grid=(s_len // ts,),
        in_specs=[tok(D_MODEL), tok(ATT_W), tok(ATT_W), tok(D_MODEL), tok(D_MODEL), full(wbf), full(wbs), full(wo)],
        out_specs=[tok(ATT_W), tok(ATT_W)] + [tok(D_MODEL)] * 5,
        out_shape=[b16o(ATT_W), b16o(ATT_W)] + [b16o(D_MODEL)] * 5,
        compiler_params=_cparams("parallel"),
    )(dx, yf, ys, gf, gs, wbf, wbs, wo)


def _mix_bwd(x, dx_in, gain, w_in, fqr, fkr, dfqn, dfkn, qg, kg, dfv, df_col, logf, dsq, dsk, dsv, dgf, dgs, name):
    s_len = x.shape[0]
    ts = min(256, s_len)
    nt = s_len // ts
    gmat = _head_group_matrix()

    def body(x_ref, dxi_ref, gain_ref, w_ref, fqr_ref, fkr_ref, dfqn_ref, dfkn_ref, qg_ref, kg_ref, gm_ref,
             dfv_ref, df_ref, logf_ref, dsq_ref, dsk_ref, dsv_ref, dgf_ref, dgs_ref,
             dp_ref, dx_ref, dgain_ref, dqg_ref, dkg_ref, dbias_ref, carry):
        i = pl.program_id(0)

        @pl.when(i == 0)
        def _():
            carry[...] = jnp.zeros_like(carry)
            dgain_ref[...] = jnp.zeros_like(dgain_ref)
            dqg_ref[...] = jnp.zeros_like(dqg_ref)
            dkg_ref[...] = jnp.zeros_like(dkg_ref)
            dbias_ref[...] = jnp.zeros_like(dbias_ref)

        gm = gm_ref[...]

        def headnorm_bwd(raw, dout, g, dg_ref):
            ms = _dot_split(raw * raw, gm, HEAD_SUM_PARTS) * (1.0 / HEAD_DIM)
            r = lax.rsqrt(ms + EPS)
            nrm = raw * r
            dg_ref[...] += jnp.sum(dout * nrm, axis=0, keepdims=True)
            dn = dout * g
            mean_h = _dot_split(dn * nrm, gm, HEAD_SUM_PARTS) * (1.0 / HEAD_DIM)
            return r * (dn - nrm * mean_h)

        dp_ref[:, C_FQ:C_FQ + ATT_W] = headnorm_bwd(fqr_ref[...], dfqn_ref[...], qg_ref[...], dqg_ref).astype(BF16)
        dp_ref[:, C_FK:C_FK + ATT_W] = headnorm_bwd(fkr_ref[...], dfkn_ref[...], kg_ref[...], dkg_ref).astype(BF16)
        dp_ref[:, C_FV:C_FV + ATT_W] = dfv_ref[...].astype(BF16)
        dp_ref[:, C_SQ:C_SQ + ATT_W] = dsq_ref[...].astype(BF16)
        dp_ref[:, C_SK:C_SK + ATT_W] = dsk_ref[...].astype(BF16)
        dp_ref[:, C_SV:C_SV + ATT_W] = dsv_ref[...].astype(BF16)
        dp_ref[:, C_GF:C_GF + D_MODEL] = dgf_ref[...]
        dp_ref[:, C_GS:C_GS + D_MODEL] = dgs_ref[...]

        r_ = lax.broadcasted_iota(jnp.int32, (ts, ts), 0)
        c_ = lax.broadcasted_iota(jnp.int32, (ts, ts), 1)
        rev = (c_ >= r_).astype(BF16)
        dlogf = _dot_split_left(rev, df_ref[...], 3) + carry[...]
        carry[...] = dlogf[0:1, :]
        lane = lax.broadcasted_iota(jnp.int32, (ts, LANES), 1)
        dfl = jnp.where(lane < N_HEADS, dlogf * (1.0 - jnp.exp(logf_ref[...])), 0.0)
        dbias_ref[...] += jnp.sum(dfl, axis=0, keepdims=True)
        dp_ref[:, C_FL:C_FL + LANES] = dfl.astype(BF16)
        dp_ref[:, C_FL + LANES:C_SQ] = jnp.zeros((ts, C_SQ - C_FL - LANES), BF16)

        dh = _dot(dp_ref[...], w_ref[...])
        xf = x_ref[...]
        r = _rms_rinv(xf)
        xhat = xf * r
        dgain_ref[...] += jnp.sum(dh * xhat, axis=0, keepdims=True)
        dn = dh * gain_ref[...]
        dx_ref[...] = dxi_ref[...] + r * (dn - xhat * jnp.mean(dn * xhat, axis=-1, keepdims=True))

    tok = lambda w: pl.BlockSpec((ts, w), lambda i: (nt - 1 - i, 0))
    full = lambda a: pl.BlockSpec(a.shape, lambda i: (0, 0))
    row = lambda w: pl.BlockSpec((1, w), lambda i: (0, 0))
    return _ride_call(
        body, name, (nt,),
        [tok(D_MODEL), tok(D_MODEL), full(gain), full(w_in), tok(ATT_W), tok(ATT_W), tok(ATT_W), tok(ATT_W),
         full(qg), full(kg), full(gmat), tok(ATT_W), tok(LANES), tok(LANES), tok(ATT_W), tok(ATT_W), tok(ATT_W),
         tok(D_MODEL), tok(D_MODEL)],
        [tok(IN_PAD), tok(D_MODEL), row(D_MODEL), row(ATT_W), row(ATT_W), row(LANES)],
        [jax.ShapeDtypeStruct((s_len, IN_PAD), BF16), jax.ShapeDtypeStruct((s_len, D_MODEL), F32),
         jax.ShapeDtypeStruct((1, D_MODEL), F32), jax.ShapeDtypeStruct((1, ATT_W), F32),
         jax.ShapeDtypeStruct((1, ATT_W), F32), jax.ShapeDtypeStruct((1, LANES), F32)],
        [pltpu.VMEM((1, LANES), F32)], ("arbitrary",),
        (x, dx_in, gain, w_in, fqr, fkr, dfqn, dfkn, qg, kg, gmat, dfv, df_col, logf, dsq, dsk, dsv, dgf, dgs), None)


def _ple_loss(x, p, tgt, gain, wpg, wpp, name):
    s_len = x.shape[0]
    ts = min(512, s_len)

    def body(x_ref, p_ref, t_ref, gain_ref, wpg_ref, wpp_ref, dx_ref, n_ref, ds_ref, dpp_ref, dgain_ref, loss_ref):
        i = pl.program_id(0)

        @pl.when(i == 0)
        def _():
            dgain_ref[...] = jnp.zeros_like(dgain_ref)
            loss_ref[...] = jnp.zeros_like(loss_ref)

        xf = x_ref[...]
        r = _rms_rinv(xf)
        n = xf * r
        hn = (n * gain_ref[...]).astype(BF16)
        n_ref[...] = hn
        sg = _sigmoid(_dot(hn, wpg_ref[...]))
        pp = _dot_nt(p_ref[...].astype(BF16), wpp_ref[...])
        err = (xf + sg * pp) - t_ref[...]
        sq = jnp.sum(jnp.sum(err * err, axis=1, keepdims=True), axis=0, keepdims=True)
        loss_ref[...] += (0.5 / D_MODEL) * sq
        dout = err * (1.0 / D_MODEL)
        dpp_ref[...] = (dout * sg).astype(BF16)
        ds = (dout * pp * (sg * (1.0 - sg))).astype(BF16)
        ds_ref[...] = ds
        dhn = _dot_nt(ds, wpg_ref[...])
        dgain_ref[...] += jnp.sum(dhn * n, axis=0, keepdims=True)
        dn = dhn * gain_ref[...]
        dx_ref[...] = dout + r * (dn - n * jnp.mean(dn * n, axis=-1, keepdims=True))

    tok = lambda w: pl.BlockSpec((ts, w), lambda i: (i, 0))
    full = lambda a: pl.BlockSpec(a.shape, lambda i: (0, 0))
    return pl.pallas_call(
        body, name=name, grid=(s_len // ts,),
        in_specs=[tok(D_MODEL), tok(PLE_DIM), tok(D_MODEL), full(gain), full(wpg), full(wpp)],
        out_specs=[tok(D_MODEL), tok(D_MODEL), tok(D_MODEL), tok(D_MODEL),
                   pl.BlockSpec((1, D_MODEL), lambda i: (0, 0)), pl.BlockSpec((8, LANES), lambda i: (0, 0))],
        out_shape=[jax.ShapeDtypeStruct((s_len, D_MODEL), F32), jax.ShapeDtypeStruct((s_len, D_MODEL), BF16),
                   jax.ShapeDtypeStruct((s_len, D_MODEL), BF16), jax.ShapeDtypeStruct((s_len, D_MODEL), BF16),
                   jax.ShapeDtypeStruct((1, D_MODEL), F32), jax.ShapeDtypeStruct((8, LANES), F32)],
        compiler_params=_cparams("arbitrary"),
    )(x, p, tgt, gain, wpg, wpp)


def _exchange(x, name, broadcast):
    def body(x_ref, out_ref, send_sems, recv_sems, local_sem):
        _exchange_start(x_ref, out_ref, send_sems, recv_sems, local_sem, broadcast)
        _exchange_wait(x_ref, out_ref, send_sems, recv_sems, local_sem, broadcast)

    return pl.pallas_call(
        body, name=name,
        in_specs=[EXCHANGE_SPEC],
        out_specs=EXCHANGE_SPEC,
        out_shape=_exchange_shape(x, broadcast),
        scratch_shapes=list(EXCHANGE_SEMS),
        compiler_params=pltpu.CompilerParams(has_side_effects=True),
    )(x)


def _gather_two_level(x, name):
    def body(x_ref, out_ref, send_sems, recv_sems, local_sem):
        mx, my, mc = lax.axis_index("x"), lax.axis_index("y"), lax.axis_index("c")
        me, sibling = (mx, my, mc), (mx, my, 1 - mc)
        chips = [(1 - mx, my), (mx, 1 - my), (1 - mx, 1 - my)]

        def slot(px, py, pc):
            return out_ref.at[4 * px + 2 * py + pc]

        def copy(k, block, to, src=None):
            return pltpu.make_async_remote_copy(
                src_ref=slot(*block) if src is None else src, dst_ref=slot(*block),
                send_sem=send_sems.at[k], recv_sem=recv_sems.at[k], device_id=to, device_id_type=MESH)

        mine = pltpu.make_async_copy(x_ref, slot(*me), local_sem)
        mine.start()
        first = [copy(0, me, sibling, src=x_ref)]
        first += [copy(1 + j, me, (*chip, mc), src=x_ref) for j, chip in enumerate(chips)]
        for cp in first:
            cp.start()
        passed = [copy(4 + j, (*chip, mc), sibling) for j, chip in enumerate(chips)]
        for j, chip in enumerate(chips):
            copy(1 + j, (*chip, mc), me).wait_recv()
            passed[j].start()
        copy(0, sibling, me).wait_recv()
        for j, chip in enumerate(chips):
            copy(4 + j, (*chip, 1 - mc), me).wait_recv()
        for cp in first + passed:
            cp.wait_send()
        mine.wait()

    return pl.pallas_call(
        body, name=name,
        in_specs=[EXCHANGE_SPEC],
        out_specs=EXCHANGE_SPEC,
        out_shape=_exchange_shape(x, True),
        scratch_shapes=list(EXCHANGE_SEMS),
        compiler_params=pltpu.CompilerParams(has_side_effects=True),
    )(x)


EXCHANGE_SPEC = pl.BlockSpec(memory_space=pl.ANY)
EXCHANGE_SEMS = (pltpu.SemaphoreType.DMA((N_DEV - 1,)), pltpu.SemaphoreType.DMA((N_DEV - 1,)), pltpu.SemaphoreType.DMA)


def _exchange_shape(x, broadcast):
    return jax.ShapeDtypeStruct((N_DEV,) + tuple(x.shape if broadcast else x.shape[1:]), x.dtype)


def _exchange_copies(x_ref, out_ref, send_sems, recv_sems, local_sem, broadcast, with_recv=True):
    mx, my, mc = lax.axis_index("x"), lax.axis_index("y"), lax.axis_index("c")
    me = 4 * mx + 2 * my + mc

    def src(idx):
        return x_ref if broadcast else x_ref.at[idx]

    local = pltpu.make_async_copy(src(me), out_ref.at[me], local_sem)
    pairs = []
    for k in range(1, N_DEV):
        px = (1 - mx) if k & 4 else mx
        py = (1 - my) if k & 2 else my
        pc = (1 - mc) if k & 1 else mc
        peer = 4 * px + 2 * py + pc
        sems = dict(send_sem=send_sems.at[k - 1], recv_sem=recv_sems.at[k - 1], device_id=(px, py, pc), device_id_type=MESH)
        recv = pltpu.make_async_remote_copy(src_ref=src(peer), dst_ref=out_ref.at[peer], **sems) if with_recv else None
        pairs.append((pltpu.make_async_remote_copy(src_ref=src(peer), dst_ref=out_ref.at[me], **sems), recv))
    return local, pairs


def _exchange_start(*refs_and_mode):
    local, pairs = _exchange_copies(*refs_and_mode, with_recv=False)
    local.start()
    for send, _ in pairs:
        send.start()


def _exchange_wait(*refs_and_mode):
    local, pairs = _exchange_copies(*refs_and_mode)
    for _, recv in pairs:
        recv.wait_recv()
    for send, _ in pairs:
        send.wait_send()
    local.wait()


def _riding(body, grid, n_in, n_out, ride):
    if ride is None:
        return body
    broadcast = ride[1]

    def wrapped(*refs):
        ins, x_ref = refs[:n_in], refs[n_in]
        outs, out_ref = refs[n_in + 1:n_in + 1 + n_out], refs[n_in + 1 + n_out]
        scratch, sems = refs[n_in + 2 + n_out:-3], refs[-3:]
        step = pl.program_id(0)
        for d in range(1, len(grid)):
            step = step * grid[d] + pl.program_id(d)
        total = 1
        for g in grid:
            total *= g

        @pl.when(step == 0)
        def _():
            _exchange_start(x_ref, out_ref, *sems, broadcast)

        body(*ins, *outs, *scratch)

        @pl.when(step == total - 1)
        def _():
            _exchange_wait(x_ref, out_ref, *sems, broadcast)

    return wrapped


def _ride_call(body, name, grid, in_specs, out_specs, out_shape, scratch_shapes, sem, operands, ride):
    if ride is None:
        return pl.pallas_call(body, name=name, grid=grid, in_specs=in_specs, out_specs=out_specs, out_shape=out_shape,
                              scratch_shapes=scratch_shapes, compiler_params=_cparams(*sem))(*operands)
    return pl.pallas_call(
        _riding(body, grid, len(in_specs), len(out_specs), ride), name=name, grid=grid,
        in_specs=list(in_specs) + [EXCHANGE_SPEC], out_specs=list(out_specs) + [EXCHANGE_SPEC],
        out_shape=list(out_shape) + [_exchange_shape(*ride)],
        scratch_shapes=list(scratch_shapes) + list(EXCHANGE_SEMS),
        compiler_params=_cparams(*(["arbitrary"] * len(grid))),
    )(*operands, ride[0])


def _adamw_math(w, g, m, v):
    m2 = ADAM_B1 * m + (1.0 - ADAM_B1) * g
    v2 = ADAM_B2 * v + (1.0 - ADAM_B2) * (g * g)
    m_hat = m2 / (1.0 - ADAM_B1 ** ADAM_STEP)
    v_hat = v2 / (1.0 - ADAM_B2 ** ADAM_STEP)
    delta = -ADAM_LR * (m_hat / (jnp.sqrt(v_hat) + ADAM_EPS) + ADAM_WD * w)
    return delta, m2, v2


def _sum_parts(parts, name, tr):
    _, rows, cols = parts.shape

    def body(p_ref, g_ref):
        g = p_ref[0].astype(F32)
        for s in range(1, N_DEV):
            g = g + p_ref[s].astype(F32)
        g_ref[...] = g

    return pl.pallas_call(
        body, name=name, grid=(rows // tr,),
        in_specs=[pl.BlockSpec((N_DEV, tr, cols), lambda i: (0, i, 0))],
        out_specs=pl.BlockSpec((tr, cols), lambda i: (i, 0)),
        out_shape=jax.ShapeDtypeStruct((rows, cols), F32),
        compiler_params=_cparams("parallel"),
    )(parts)


ADAM_SPLIT_ELEMS = 400_000


def _adamw_shard(g, w, m, v, name):
    rows, cols = w.shape
    tr = rows // 2 if rows * cols > ADAM_SPLIT_ELEMS else rows

    def body(g_ref, w_ref, m_ref, v_ref, d_ref, m2_ref, v2_ref):
        d_ref[...], m2_ref[...], v2_ref[...] = _adamw_math(w_ref[...], g_ref[...], m_ref[...], v_ref[...])

    blk = pl.BlockSpec((tr, cols), lambda i: (i, 0))
    return pl.pallas_call(
        body, name=name, grid=(rows // tr,),
        in_specs=[blk] * 4, out_specs=[blk] * 3,
        out_shape=[jax.ShapeDtypeStruct((rows, cols), F32)] * 3,
        compiler_params=_cparams("parallel"),
    )(g, w, m, v)


def _adamw_small(parts, w, m, v, name):
    names = list(SMALL_NAMES)

    def body(p_ref, *refs):
        ins, outs = refs[:3 * len(names)], refs[3 * len(names):]
        total = p_ref[0]
        for s in range(1, N_DEV):
            total = total + p_ref[s]
        for i, n in enumerate(names):
            row, off, width = SMALL_POS[n]
            g = total[row:row + 1, off:off + width]
            w_ref, m_ref, v_ref = ins[3 * i:3 * i + 3]
            g_ref, d_ref, m2_ref, v2_ref = outs[4 * i:4 * i + 4]
            g_ref[...] = g
            d_ref[...], m2_ref[...], v2_ref[...] = _adamw_math(w_ref[...], g, m_ref[...], v_ref[...])
        row, off, _ = SMALL_POS["loss"]
        outs[-1][...] = total[row:row + 1, off:off + 1]

    operands = [parts] + [t[n] for n in names for t in (w, m, v)]
    shapes = [jax.ShapeDtypeStruct(w[n].shape, F32) for n in names for _ in range(4)]
    shapes.append(jax.ShapeDtypeStruct((1, 1), F32))
    out = pl.pallas_call(body, name=name, out_shape=shapes)(*operands)
    return {n: tuple(out[4 * i:4 * i + 4]) for i, n in enumerate(names)}, out[-1]


TRANSPOSED = frozenset(("ffn1_w_gate", "ffn1_w_up", "w_in", "w_branch_fox", "w_branch_sb", "ffn2_w_gate", "ffn2_w_up",
                        "w_ple_proj"))
F_PAD_ROWS = C_SQ - FL_REAL_END


def _pack(pieces, group, dtype):
    out = []
    for name in GATHER_GROUPS[group]:
        r = pieces[name].T if name in TRANSPOSED else pieces[name]
        r = r.reshape(-1, D_MODEL).astype(dtype)
        if r.shape[0] != PACK_ROWS[name]:
            r = jnp.pad(r, ((0, PACK_ROWS[name] - r.shape[0]), (0, 0)))
        out.append(r)
    return jnp.concatenate(out, axis=0)


def _real_rows(name):
    return W_IN_ROWS if name == "w_in" else PACK_ROWS[name]


def _gathered(got, name, shape):
    off = GATHER_OFF[name]
    return got[:, off:off + _real_rows(name), :].reshape(shape)


def _w_in_device_rows(d):
    lo, hi = d * W_IN_ROWS, (d + 1) * W_IN_ROWS
    if hi <= FL_REAL_END:
        return [(lo, hi)]
    if lo >= FL_REAL_END:
        return [(lo + F_PAD_ROWS, hi + F_PAD_ROWS)]
    return [(lo, FL_REAL_END), (C_SQ, hi + F_PAD_ROWS)]


def _w_in_t_padded(got):
    t = _gathered(got, "w_in", (IN_REAL, D_MODEL))
    return jnp.concatenate([t[:FL_REAL_END], jnp.zeros((F_PAD_ROWS, D_MODEL), t.dtype), t[FL_REAL_END:]], axis=0)


def _pack_chunks(grads, group):
    out = []
    for name in SCATTER_GROUPS[group]:
        base, _, half = name.partition("#")
        g = grads[base].astype(BF16)
        if base == "w_in":
            lo, hi = W_IN_HALVES[int(half)]
            tail = jnp.zeros((PACK_ROWS[base] - W_IN_ROWS, D_MODEL), BF16)
            c = jnp.stack([jnp.concatenate([g[a:b] for a, b in _w_in_device_rows(d)] + [tail], axis=0)[lo:hi]
                           for d in range(N_DEV)])
        else:
            c = g.reshape(N_DEV, PACK_ROWS[name], D_MODEL)
        out.append(c)
    return out[0] if len(out) == 1 else jnp.concatenate(out, axis=1)


def _shard_grad(summed, name, shape):
    if name == "w_in":
        rows = jnp.concatenate([summed[f"w_in#{i}"] for i in range(len(W_IN_HALVES))], axis=0)[:W_IN_ROWS]
    else:
        rows = summed[name][SCATTER_OFF[name]:SCATTER_OFF[name] + PACK_ROWS[name], :]
    return rows.reshape(shape[1], shape[0]).T if name in TRANSPOSED else rows.reshape(shape)


WEIGHT_NAMES = ['ffn1_norm', 'ffn1_w_gate', 'ffn1_w_up', 'ffn1_w_down', 'mix_norm', 'w_in', 'forget_bias', 'q_norm',
                'k_norm', 'w_branch_fox', 'w_branch_sb', 'w_out', 'ffn2_norm', 'ffn2_w_gate', 'ffn2_w_up',
                'ffn2_w_down', 'ple_norm', 'w_ple_gate', 'w_ple_proj']
SMALL_NAMES = ('ffn1_norm', 'mix_norm', 'ffn2_norm', 'ple_norm', 'q_norm', 'k_norm', 'forget_bias')
SMALL_POS = {'ffn1_norm': (0, 0, D_MODEL), 'mix_norm': (1, 0, D_MODEL), 'ffn2_norm': (2, 0, D_MODEL),
             'ple_norm': (3, 0, D_MODEL), 'q_norm': (4, 0, HEAD_DIM), 'k_norm': (4, HEAD_DIM, HEAD_DIM),
             'forget_bias': (4, 2 * HEAD_DIM, N_HEADS), 'loss': (4, 2 * HEAD_DIM + N_HEADS, 1)}


def _pack_small(vals, loss):
    tail = [vals[n].reshape(1, -1) for n in ('q_norm', 'k_norm', 'forget_bias')] + [loss.reshape(1, 1)]
    tail.append(jnp.zeros((1, D_MODEL - sum(t.shape[1] for t in tail)), F32))
    rows = [vals[n].reshape(1, D_MODEL) for n in SMALL_NAMES[:4]] + [jnp.concatenate(tail, axis=1)]
    rows.append(jnp.zeros((SMALL_ROWS - len(rows), D_MODEL), F32))
    return jnp.concatenate(rows, axis=0)


def _step(x, p, tgt, w):
    row = lambda a: a.reshape(1, -1).astype(F32)
    g_ffn1, g_mix, g_ffn2, g_ple = (row(w[n]) for n in SMALL_NAMES[:4])
    qg = jnp.tile(row(w['q_norm']), (1, N_HEADS))
    kg = jnp.tile(row(w['k_norm']), (1, N_HEADS))
    bias = jnp.pad(row(w['forget_bias']), ((0, 0), (0, LANES - N_HEADS)))
    half = D_FF // 2
    grads = {}

    blk = lambda n: GATHER_OFF[n] // FFN_SHARD
    ffn1 = tuple(blk(n) for n in ("ffn1_w_gate", "ffn1_w_up", "ffn1_w_down"))
    ffn2 = tuple(blk(n) for n in ("ffn2_w_gate", "ffn2_w_up", "ffn2_w_down"))
    got0 = _gather_two_level(_pack(w, 0, BF16), "gather_ffn1")
    x1, g1, u1, h1, got1 = _ffn_fwd(x, g_ffn1, got0, ffn1, "ffn1_fwd", ride=(_pack(w, 1, BF16), True))
    w_in = _w_in_t_padded(got1)
    wbf = _gathered(got1, "w_branch_fox", (D_MODEL, ATT_W))
    wbs = _gathered(got1, "w_branch_sb", (D_MODEL, ATT_W))
    wo = _gathered(got1, "w_out", (D_MODEL, D_MODEL))
    (hmix, fqr, fkr, fqn, fkn, fv, logf, f_col, f_row, sq, sk, sv, gf, gs, kmax) = _mix_fwd(
        x1, g_mix, w_in, bias, qg, kg, "mix_fwd")
    y_fox, lse, got2 = _fox_fwd(fqn, fkn, fv, f_col, f_row, kmax, "fox_fwd", ride=(_pack(w, 2, BF16), True))
    wpg = _gathered(got2, "w_ple_gate", (D_MODEL, D_MODEL))
    wpp = _gathered(got2, "w_ple_proj", (D_MODEL, PLE_DIM))
    y_sb, y_sb32 = _sb_fwd(sq, sk, sv, "sb_fwd")
    x2 = _merge_fwd(x1, y_fox, y_sb, gf, gs, wbf, wbs, wo, "merge_fwd")
    x3, g2, u2, h2, = _ffn_fwd(x2, g_ffn2, got2, ffn2, "ffn2_fwd")
    dx3, n_ple, ds_ple, dpp, dg_ple, loss = _ple_loss(x3, p, tgt, g_ple, wpg, wpp, "ple_loss")

    grads['w_ple_gate'] = _wgrad(n_ple, ds_ple, "dw_ple_gate", D_MODEL, D_MODEL)
    grads['w_ple_proj'] = _wgrad(dpp, p, "dw_ple_proj", D_MODEL, PLE_DIM)
    dg2, du2, act2, dx2, dg_ffn2 = _ffn_bwd_fused(x2, dx3, g_ffn2, g2, u2, got2, ffn2, "ffn2_bwd")
    grads['ffn2_w_gate'] = _wgrad(dg2, h2, "dw_ffn2_gate", half, D_MODEL)
    grads['ffn2_w_up'] = _wgrad(du2, h2, "dw_ffn2_up", half, D_MODEL)
    grads['ffn2_w_down'] = _wgrad(act2, dx3, "dw_ffn2_down", half, D_MODEL)
    dyf, dys, dgf, dgs, dbf, dbs, merged = _merge_bwd(dx2, y_fox, y_sb, gf, gs, wbf, wbs, wo, "merge_bwd")
    grads['w_branch_fox'] = _wgrad(dbf, y_fox, "dw_branch_fox", D_MODEL, ATT_W)
    grads['w_branch_sb'] = _wgrad(dbs, y_sb, "dw_branch_sb", D_MODEL, ATT_W)
    grads['w_out'] = _wgrad(merged, dx2, "dw_out", D_MODEL, D_MODEL)
    dfqn, dfkn, dfv, dft, part_rest = _fox_bwd(fqn, fkn, fv, dyf, y_fox, lse, f_col, f_row, kmax, "fox_bwd",
                                               ride=(_pack_chunks(grads, 5), False))
    dsq, dsk, dsv = _sb_bwd(sq, sk, sv, dys, y_sb32, "sb_bwd")
    s_len = x.shape[0]
    df_col = jnp.pad(dft[:, :2, :].reshape(N_HEADS, s_len).T, ((0, 0), (0, LANES - N_HEADS)))
    dproj, dx1, dg_mix, dqg, dkg, dbias = _mix_bwd(
        x1, dx2, g_mix, w_in, fqr, fkr, dfqn, dfkn, qg, kg, dfv, df_col, logf, dsq, dsk, dsv, dgf, dgs, "mix_bwd")
    grads['w_in'] = _wgrad(dproj, hmix, "dw_in", IN_PAD // 3, D_MODEL)
    dg1, du1, act1, dx0, dg_ffn1, part_in0 = _ffn_bwd_fused(x, dx1, g_ffn1, g1, u1, got0, ffn1, "ffn1_bwd",
                                                            ride=(_pack_chunks(grads, 3), False))
    grads['ffn1_w_gate'], part_in1 = _wgrad(dg1, h1, "dw_ffn1_gate", half, D_MODEL,
                                            ride=(_pack_chunks(grads, 4), False))
    grads['ffn1_w_up'], part_gate = _wgrad(du1, h1, "dw_ffn1_up", half, D_MODEL, ride=(_pack_chunks(grads, 0), False))
    grads['ffn1_w_down'], part_up = _wgrad(act1, dx1, "dw_ffn1_down", half, D_MODEL,
                                           ride=(_pack_chunks(grads, 1), False))
    part_down = _exchange(_pack_chunks(grads, 2), "scatter_ffn1_down", False)

    fold = lambda a: a.reshape(N_HEADS, HEAD_DIM).sum(axis=0).reshape(1, HEAD_DIM)
    small_g = {'ffn1_norm': dg_ffn1, 'mix_norm': dg_mix, 'ffn2_norm': dg_ffn2, 'ple_norm': dg_ple,
               'q_norm': fold(dqg), 'k_norm': fold(dkg), 'forget_bias': dbias[:, :N_HEADS]}
    return loss[0, 0], dx0, (part_gate, part_up, part_down, part_in0, part_in1, part_rest), small_g


def kernel(x, p, ffn1_norm, ffn1_w_gate, ffn1_w_up, ffn1_w_down, mix_norm, w_in, forget_bias, q_norm, k_norm, w_branch_fox, w_branch_sb, w_out, ffn2_norm, ffn2_w_gate, ffn2_w_up, ffn2_w_down, ple_norm, w_ple_gate, w_ple_proj, loss_target, m_ffn1_norm, m_ffn1_w_gate, m_ffn1_w_up, m_ffn1_w_down, m_mix_norm, m_w_in, m_forget_bias, m_q_norm, m_k_norm, m_w_branch_fox, m_w_branch_sb, m_w_out, m_ffn2_norm, m_ffn2_w_gate, m_ffn2_w_up, m_ffn2_w_down, m_ple_norm, m_w_ple_gate, m_w_ple_proj, v_ffn1_norm, v_ffn1_w_gate, v_ffn1_w_up, v_ffn1_w_down, v_mix_norm, v_w_in, v_forget_bias, v_q_norm, v_k_norm, v_w_branch_fox, v_w_branch_sb, v_w_out, v_ffn2_norm, v_ffn2_w_gate, v_ffn2_w_up, v_ffn2_w_down, v_ple_norm, v_w_ple_gate, v_w_ple_proj):
    args = dict(locals())
    w = {n: args[n][0] for n in WEIGHT_NAMES}
    m = {n: args["m_" + n][0] for n in WEIGHT_NAMES}
    v = {n: args["v_" + n][0] for n in WEIGHT_NAMES}
    loss, dx, parts, small_g = _step(x[0], p[0, 0], loss_target[0], w)

    summed = {}
    for grp, part in enumerate(parts):
        s = _sum_parts(part, f"sum_grads_{grp}", SUM_TILE_ROWS[grp])
        summed.update({n: s for n in SCATTER_GROUPS[grp]})
    big = {}
    for n in WEIGHT_NAMES:
        if n not in SMALL_NAMES:
            g = _shard_grad(summed, n, w[n].shape)
            big[n] = (g,) + tuple(_adamw_shard(g, w[n], m[n], v[n], "adamw_" + n))
    small_parts = _exchange(_pack_small(small_g, loss), "gather_small", True)
    small, total_loss = _adamw_small(small_parts, *({n: args[pre + n] for n in SMALL_NAMES} for pre in ("", "m_", "v_")),
                                     "adamw_small")
    big.update(small)

    outs = [total_loss.reshape(()), dx.reshape(x.shape)]
    for kind in range(4):
        outs += [big[n][kind].reshape(args[n].shape) for n in WEIGHT_NAMES]
    return tuple(outs)
```

```python
import jax
import jax.numpy as jnp
from jax import lax
from jax.experimental import pallas as pl
from jax.experimental.pallas import tpu as pltpu

F32 = jnp.float32
BF16 = jnp.bfloat16

D_MODEL = 1024
D_FF = 2816
N_HEADS = 8
HEAD_DIM = 64
ATT_W = N_HEADS * HEAD_DIM
PLE_DIM = 256
EPS = 1e-6
N_DEV = 8
MESH = pl.DeviceIdType.MESH

LANES = 128
V7X_SCOPED_VMEM_BYTES = 56 * 1024 * 1024

C_FQ, C_FK, C_FV, C_FL = 0, 512, 1024, 1536
C_SQ, C_SK, C_SV, C_GF, C_GS = 1792, 2304, 2816, 3328, 4352
IN_PAD = 5376
IN_REAL = 5128
FL_REAL_END = 1544

ADAM_LR = 0.001
ADAM_B1 = 0.9
ADAM_B2 = 0.999
ADAM_EPS = 1e-08
ADAM_WD = 0.01
ADAM_STEP = 10

PACK_ROWS = {"ffn1_w_gate": 352, "ffn1_w_up": 352, "ffn1_w_down": 352, "w_in": 656, "w_branch_fox": 64,
             "w_branch_sb": 64, "w_out": 128, "ffn2_w_gate": 352, "ffn2_w_up": 352, "ffn2_w_down": 352,
             "w_ple_gate": 128, "w_ple_proj": 32}
GATHER_GROUPS = (
    ("ffn1_w_gate", "ffn1_w_up", "ffn1_w_down"),
    ("w_in", "w_branch_fox", "w_branch_sb", "w_out"),
    ("ffn2_w_gate", "ffn2_w_up", "ffn2_w_down", "w_ple_gate", "w_ple_proj"),
)
W_IN_HALVES = ((0, 336), (336, 656))
PACK_ROWS.update({f"w_in#{i}": hi - lo for i, (lo, hi) in enumerate(W_IN_HALVES)})
SCATTER_GROUPS = (
    ("ffn1_w_gate",), ("ffn1_w_up",), ("ffn1_w_down",),
    ("w_in#0",), ("w_in#1",),
    ("ffn2_w_gate", "ffn2_w_up", "ffn2_w_down", "w_ple_gate", "w_ple_proj", "w_branch_fox", "w_branch_sb", "w_out"),
)
SUM_TILE_ROWS = (352, 352, 352, 336, 320, 368)


def _offsets(groups):
    off = {}
    for grp in groups:
        o = 0
        for n in grp:
            off[n] = o
            o += PACK_ROWS[n]
    return off


GATHER_OFF = _offsets(GATHER_GROUPS)
SCATTER_OFF = _offsets(SCATTER_GROUPS)
W_IN_ROWS = 641

SMALL_ROWS = 8


def _cparams(*sem):
    return pltpu.CompilerParams(dimension_semantics=sem, vmem_limit_bytes=V7X_SCOPED_VMEM_BYTES)


def _dot(a, b):
    return jnp.dot(a, b, preferred_element_type=F32)


def _dot_nt(a, b):
    return lax.dot_general(a, b, (((1,), (1,)), ((), ())), preferred_element_type=F32)


def _dot_tn(a, b):
    return lax.dot_general(a, b, (((0,), (0,)), ((), ())), preferred_element_type=F32)


def _split(x, parts):
    out = []
    r = x
    for _ in range(parts):
        p = r.astype(BF16)
        out.append(p)
        r = r - p.astype(F32)
    return out


def _dot_split(x, m, parts):
    acc = None
    for p in _split(x, parts):
        t = _dot(p, m)
        acc = t if acc is None else acc + t
    return acc


def _dot_split_left(m, x, parts):
    acc = None
    for p in _split(x, parts):
        t = _dot(m, p)
        acc = t if acc is None else acc + t
    return acc


def _rms_rinv(xf):
    return lax.rsqrt(jnp.mean(xf * xf, axis=-1, keepdims=True) + EPS)


def _sigmoid(x):
    return 1.0 / (1.0 + jnp.exp(-x))


def _softplus_neg_abs(z):
    return jnp.log(1.0 + jnp.exp(-jnp.abs(z)))


FFN_SHARD = D_FF // N_DEV
FFN_CHUNK = 4


def _ffn_w_spec(blk, index_map):
    return pl.BlockSpec((FFN_CHUNK, FFN_SHARD, D_MODEL), lambda *g: (index_map(*g), blk, 0))


def _ffn_w(ref):
    return ref[...].reshape(FFN_CHUNK * FFN_SHARD, D_MODEL)


def _ffn_fwd(x, gain, wbuf, blks, name, ride=None):
    s_len = x.shape[0]
    ts = min(512, s_len)
    fc = FFN_CHUNK * FFN_SHARD
    nt, nc = s_len // ts, D_FF // fc

    def body(x_ref, gain_ref, wg_ref, wu_ref, wd_ref, y_ref, g_ref, u_ref, h_ref, acc_scr):
        j = pl.program_id(1)

        @pl.when(j == 0)
        def _():
            xf = x_ref[...]
            h_ref[...] = ((xf * _rms_rinv(xf)) * gain_ref[...]).astype(BF16)
            acc_scr[...] = jnp.zeros_like(acc_scr)

        h = h_ref[...]
        g = _dot_nt(h, _ffn_w(wg_ref))
        u = _dot_nt(h, _ffn_w(wu_ref))
        g_ref[...] = g.astype(BF16)
        u_ref[...] = u.astype(BF16)
        a = (g * _sigmoid(g) * u).astype(BF16)
        acc_scr[...] += _dot(a, _ffn_w(wd_ref))

        @pl.when(j == nc - 1)
        def _():
            y_ref[...] = x_ref[...] + 0.5 * acc_scr[...]

    tok = pl.BlockSpec((ts, D_MODEL), lambda i, j: (i, 0))
    hid = pl.BlockSpec((ts, fc), lambda i, j: (i, j))
    return _ride_call(
        body, name, (nt, nc),
        [tok, pl.BlockSpec((1, D_MODEL), lambda i, j: (0, 0))] + [_ffn_w_spec(b, lambda i, j: j) for b in blks],
        [tok, hid, hid, tok],
        [jax.ShapeDtypeStruct((s_len, D_MODEL), F32), jax.ShapeDtypeStruct((s_len, D_FF), BF16),
         jax.ShapeDtypeStruct((s_len, D_FF), BF16), jax.ShapeDtypeStruct((s_len, D_MODEL), BF16)],
        [pltpu.VMEM((ts, D_MODEL), F32)], ("parallel", "arbitrary"), (x, gain, wbuf, wbuf, wbuf), ride)


def _ffn_bwd_fused(x, dy, gain, g, u, wbuf, blks, name, ride=None):
    s_len = x.shape[0]
    ts = min(512, s_len)
    fc = FFN_CHUNK * FFN_SHARD
    nt = s_len // ts
    assert D_FF == 2 * fc

    def hidden(dy_ref, g_ref, u_ref, wg_ref, wu_ref, wd_ref, dg_ref, du_ref, act_ref):
        da = 0.5 * _dot_nt(dy_ref[...].astype(BF16), _ffn_w(wd_ref))
        gf = g_ref[...].astype(F32)
        uf = u_ref[...].astype(F32)
        sg = _sigmoid(gf)
        silu = gf * sg
        dg = (da * uf * (sg * (1.0 + gf * (1.0 - sg)))).astype(BF16)
        du = (da * silu).astype(BF16)
        dg_ref[...] = dg
        du_ref[...] = du
        act_ref[...] = (0.5 * silu * uf).astype(BF16)
        return _dot(dg, _ffn_w(wg_ref)) + _dot(du, _ffn_w(wu_ref))

    def first(dy_ref, g_ref, u_ref, wg_ref, wu_ref, wd_ref, dg_ref, du_ref, act_ref, dh_ref):
        dh_ref[...] = hidden(dy_ref, g_ref, u_ref, wg_ref, wu_ref, wd_ref, dg_ref, du_ref, act_ref)

    def second(x_ref, dy_ref, gain_ref, g_ref, u_ref, wg_ref, wu_ref, wd_ref, dh0_ref, dg_half, du_half, act_half,
               dg_ref, du_ref, act_ref, dx_ref, dgain_ref):
        i = pl.program_id(0)
        dh = dh0_ref[...] + hidden(dy_ref, g_ref, u_ref, wg_ref, wu_ref, wd_ref, dg_ref, du_ref, act_ref)
        xf = x_ref[...]
        r = _rms_rinv(xf)
        xhat = xf * r
        dgp = jnp.sum(dh * xhat, axis=0, keepdims=True)

        @pl.when(i == 0)
        def _():
            dgain_ref[...] = dgp

        @pl.when(i > 0)
        def _():
            dgain_ref[...] += dgp

        dn = dh * gain_ref[...]
        dx_ref[...] = dy_ref[...] + r * (dn - xhat * jnp.mean(dn * xhat, axis=-1, keepdims=True))

    tok = pl.BlockSpec((ts, D_MODEL), lambda i: (i, 0))
    row = pl.BlockSpec((1, D_MODEL), lambda i: (0, 0))
    hid = lambda c: pl.BlockSpec((ts, fc), lambda i: (i, c))
    wts = lambda c: [pl.BlockSpec((FFN_CHUNK, FFN_SHARD, D_MODEL), lambda i, b=b: (c, b, 0),
                                  pipeline_mode=pl.Buffered(1)) for b in blks]
    hidden_shapes = [jax.ShapeDtypeStruct((s_len, D_FF), BF16)] * 3
    dg, du, act, dh0, *rode = _ride_call(
        first, name + "_a", (nt,), [tok, hid(0), hid(0)] + wts(0), [hid(0)] * 3 + [tok],
        hidden_shapes + [jax.ShapeDtypeStruct((s_len, D_MODEL), F32)], [], ("parallel",),
        (dy, g, u, wbuf, wbuf, wbuf), ride)
    filled = pl.BlockSpec(memory_space=pl.ANY)
    dg, du, act, dx, dgain = pl.pallas_call(
        second, name=name + "_b", grid=(nt,),
        in_specs=[tok, tok, row, hid(1), hid(1)] + wts(1) + [tok, filled, filled, filled],
        out_specs=[hid(1)] * 3 + [tok, row],
        out_shape=hidden_shapes + [jax.ShapeDtypeStruct((s_len, D_MODEL), F32), jax.ShapeDtypeStruct((1, D_MODEL), F32)],
        input_output_aliases={9: 0, 10: 1, 11: 2},
        compiler_params=_cparams("arbitrary"),
    )(x, dy, gain, g, u, wbuf, wbuf, wbuf, dh0, dg, du, act)
    return (dg, du, act, dx, dgain, *rode)


def _wgrad(a, b, name, tk, tn, ride=None):
    s_len, k_dim = a.shape
    n_dim = b.shape[1]
    ts = min(2048, s_len)
    ns = s_len // ts

    def body(a_ref, b_ref, o_ref, acc):
        s = pl.program_id(2)
        p = _dot_tn(a_ref[...].astype(BF16), b_ref[...].astype(BF16))

        @pl.when(s == 0)
        def _():
            acc[...] = p

        @pl.when(s > 0)
        def _():
            acc[...] += p

        @pl.when(s == ns - 1)
        def _():
            o_ref[...] = acc[...].astype(BF16)

    out = _ride_call(
        body, name, (k_dim // tk, n_dim // tn, ns),
        [pl.BlockSpec((ts, tk), lambda k, n, s: (s, k)), pl.BlockSpec((ts, tn), lambda k, n, s: (s, n))],
        [pl.BlockSpec((tk, tn), lambda k, n, s: (k, n))], [jax.ShapeDtypeStruct((k_dim, n_dim), BF16)],
        [pltpu.VMEM((tk, tn), F32)], ("parallel", "parallel", "arbitrary"), (a, b), ride)
    return out[0] if ride is None else tuple(out)


HEAD_SUM_PARTS = 1


def _head_group_matrix():
    r = lax.broadcasted_iota(jnp.int32, (ATT_W, ATT_W), 0) // HEAD_DIM
    c = lax.broadcasted_iota(jnp.int32, (ATT_W, ATT_W), 1) // HEAD_DIM
    return (r == c).astype(BF16)


def _mix_fwd(x, gain, w_in, bias, qg, kg, name):
    s_len = x.shape[0]
    ts = min(512, s_len)
    nt = s_len // ts
    gmat = _head_group_matrix()

    def body(x_ref, gain_ref, w_ref, bias_ref, qg_ref, kg_ref, gm_ref,
             h_ref, fqr_ref, fkr_ref, fqn_ref, fkn_ref, fv_ref, logf_ref, f_ref, ft_ref,
             sq_ref, sk_ref, sv_ref, gf_ref, gs_ref, carry):
        i = pl.program_id(0)
        xf = x_ref[...]
        h = ((xf * _rms_rinv(xf)) * gain_ref[...]).astype(BF16)
        h_ref[...] = h
        gm = gm_ref[...]

        def proj(lo, n):
            return _dot_nt(h, w_ref[lo:lo + n, :])

        def headnorm(raw, g):
            ms = _dot_split(raw * raw, gm, HEAD_SUM_PARTS) * (1.0 / HEAD_DIM)
            return ((raw * lax.rsqrt(ms + EPS)) * g).astype(BF16)

        fq = proj(C_FQ, ATT_W)
        fqr_ref[...] = fq
        fqn_ref[...] = headnorm(fq, qg_ref[...])
        fk = proj(C_FK, ATT_W)
        fkr_ref[...] = fk
        fkn_ref[...] = headnorm(fk, kg_ref[...])
        fv_ref[...] = proj(C_FV, ATT_W).astype(BF16)
        sq_ref[...] = proj(C_SQ, ATT_W).astype(BF16)
        sk_ref[...] = proj(C_SK, ATT_W).astype(BF16)
        sv_ref[...] = proj(C_SV, ATT_W).astype(BF16)
        gf_ref[...] = proj(C_GF, D_MODEL)
        gs_ref[...] = proj(C_GS, D_MODEL)

        fl = proj(C_FL, LANES) + bias_ref[...]
        lane = lax.broadcasted_iota(jnp.int32, fl.shape, 1)
        logf = jnp.where(lane < N_HEADS, jnp.minimum(fl, 0.0) - _softplus_neg_abs(fl), 0.0)
        logf_ref[...] = logf

        @pl.when(i == 0)
        def _():
            carry[...] = jnp.zeros_like(carry)

        r = lax.broadcasted_iota(jnp.int32, (ts, ts), 0)
        c = lax.broadcasted_iota(jnp.int32, (ts, ts), 1)
        tri = (r >= c).astype(BF16)
        f_tile = _dot_split_left(tri, logf, 3) + carry[...]
        f_ref[...] = f_tile
        ft_ref[...] = f_tile.T[:N_HEADS, :]
        carry[...] = f_tile[ts - 1:ts, :]

    tok = lambda w: pl.BlockSpec((ts, w), lambda i: (i, 0))
    full = lambda a: pl.BlockSpec(a.shape, lambda i: (0, 0), pipeline_mode=pl.Buffered(1))
    f32o = lambda w: jax.ShapeDtypeStruct((s_len, w), F32)
    b16o = lambda w: jax.ShapeDtypeStruct((s_len, w), BF16)
    return _ride_call(
        body, name, (nt,),
        [tok(D_MODEL), full(gain), full(w_in), full(bias), full(qg), full(kg), full(gmat)],
        [
            tok(D_MODEL), tok(ATT_W), tok(ATT_W), tok(ATT_W), tok(ATT_W), tok(ATT_W), tok(LANES), tok(LANES),
            pl.BlockSpec((N_HEADS, ts), lambda i: (0, i)),
            tok(ATT_W), tok(ATT_W), tok(ATT_W), tok(D_MODEL), tok(D_MODEL),
        ],
        [
            b16o(D_MODEL), f32o(ATT_W), f32o(ATT_W), b16o(ATT_W), b16o(ATT_W), b16o(ATT_W), f32o(LANES), f32o(LANES),
            jax.ShapeDtypeStruct((N_HEADS, s_len), F32),
            b16o(ATT_W), b16o(ATT_W), b16o(ATT_W), f32o(D_MODEL), f32o(D_MODEL),
        ],
        [pltpu.VMEM((1, LANES), F32)], ("arbitrary",), (x, gain, w_in, bias, qg, kg, gmat), None)


ATT_T = 256
SB_ROWS = 2
FOX_ROWS = 2
EXP_ZERO = 88.0


def _att_tiling(s_len, rows):
    t = min(ATT_T, s_len)
    nr = min(rows, s_len // t)
    return t, nr, s_len // (t * nr)


def _pair_specs(s_len, tq):
    qblk = pl.BlockSpec((tq, LANES), lambda hp, i: (i, hp))
    kvfull = pl.BlockSpec((s_len, LANES), lambda hp, i: (0, hp))
    return qblk, kvfull


def _walk_tiles(i, nr, load, sub, flush, more=None, trips=None):
    base = i * nr
    for kk in range(nr - 1, -1, -1):
        rs = list(range(kk, nr))
        flush(base + kk, sub(rs, load(base + kk), [r == kk for r in rs]))

    done = jnp.int32(0)
    for last in range(nr - 1, -1, -1):
        rs = list(range(last + 1))

        def visit(n, rs=rs):
            kb = base - 1 - n
            flush(kb, sub(rs, load(kb), [False] * len(rs)))

        if trips is not None:
            todo = jnp.maximum(trips(base, last) - done, 0)

            def body(it, carry, start=done, visit=visit):
                visit(start + it)
                return carry

            lax.fori_loop(0, todo, body, jnp.int32(0))
            done = done + todo
        else:
            def step(state, visit=visit, last=last):
                visit(state[0])
                return state[0] + 1, more(last)

            done, _ = lax.while_loop(lambda state: jnp.logical_and(state[0] < base, state[1] > 0), step,
                                     (done, more(last)))


def _stack(parts):
    return parts[0] if len(parts) == 1 else jnp.concatenate(parts, axis=0)


def _stacked_halves(x, lo):
    z = jnp.zeros_like(x)
    return jnp.concatenate([jnp.where(lo, x, z), jnp.where(lo, z, x)], axis=0)


def _fox_qk_cap(q_gain, k_gain):
    cap = (HEAD_DIM ** 0.5) * jnp.max(jnp.abs(q_gain)) * jnp.max(jnp.abs(k_gain))
    return (cap * 1.01 + 1.0).reshape(1).astype(F32)


def _fox_trips(hp, flast_ref, cap_ref, fq, level):
    def trips(base, r):
        gap = [jnp.max(fq[r][j] - level(r, j)) + cap_ref[0] for j in (0, 1)]

        def needed(n):
            kb = jnp.maximum(base - 1 - n, 0)
            return jnp.logical_or(gap[0] - flast_ref[2 * hp, kb] > -EXP_ZERO,
                                  gap[1] - flast_ref[2 * hp + 1, kb] > -EXP_ZERO)

        return lax.while_loop(lambda n: jnp.logical_and(n < base, needed(n)), lambda n: n + 1, jnp.int32(0))
    return trips


def _fox_fwd(q, k, v, f_wide, f_row, cap, name, ride=None):
    s_len = q.shape[0]
    t, nr, nq = _att_tiling(s_len, FOX_ROWS)

    def body(q_ref, k_ref, v_ref, f_ref, ft_ref, cap_ref, fl_ref, y_ref, lse_ref, m_ref, l_ref, acc_ref):
        hp = pl.program_id(0)
        i = pl.program_id(1)
        lane = lax.broadcasted_iota(jnp.int32, (t, LANES), 1)
        lo = lane < HEAD_DIM
        causal = lax.broadcasted_iota(jnp.int32, (t, t), 0) >= lax.broadcasted_iota(jnp.int32, (t, t), 1)
        rows = [pl.ds(r * t, t) for r in range(nr)]
        slab = lambda r, j: pl.ds((2 * r + j) * t, t)
        qst = [_stacked_halves(q_ref[rw, :] * jnp.asarray(HEAD_DIM ** -0.5, BF16), lo) for rw in rows]
        q_all = _stack(qst)
        fq = [[f_ref[rw, j * HEAD_DIM:j * HEAD_DIM + 1] for j in (0, 1)] for rw in rows]
        m_ref[...] = jnp.full(m_ref.shape, -1e30, F32)
        l_ref[...] = jnp.zeros_like(l_ref)
        acc_ref[...] = jnp.zeros_like(acc_ref)

        def load(kb):
            k0 = pl.multiple_of(kb * t, t)
            frow = [ft_ref[pl.ds(2 * hp + j, 1), pl.ds(k0, t)] for j in (0, 1)]
            return k_ref[pl.ds(k0, t), :], v_ref[pl.ds(k0, t), :], frow

        def sub(rs, tiles, masked):
            kblk, vblk, frow = tiles
            z = _dot_nt(q_all if len(rs) == nr else _stack([qst[r] for r in rs]), kblk)
            slabs = [(r, j) for r in range(len(rs)) for j in (0, 1)]
            ps, alpha = [], []
            for n, (r, j) in enumerate(slabs):
                sl = slab(rs[r], j)
                m_old = m_ref[sl, :]
                s = z[n * t:(n + 1) * t, :] + (fq[rs[r]][j] - frow[j])
                if masked[r]:
                    s = jnp.where(causal, s, -1e30)
                mj = jnp.maximum(m_old, jnp.max(s, axis=1, keepdims=True))
                aj = jnp.exp(m_old - mj)
                p = jnp.exp(s - jnp.tile(mj, (1, t // LANES)))
                m_ref[sl, :] = mj
                l_ref[sl, :] = aj * l_ref[sl, :] + jnp.sum(p, axis=1, keepdims=True)
                alpha.append(aj)
                ps.append(p.astype(BF16))
            pv = _dot(_stack(ps), vblk)
            for n, (r, j) in enumerate(slabs):
                sl = slab(rs[r], j)
                acc_ref[sl, :] = acc_ref[sl, :] * alpha[n] + pv[n * t:(n + 1) * t, :]
            return None

        trips = _fox_trips(hp, fl_ref, cap_ref, fq, lambda r, j: m_ref[slab(r, j), :])
        _walk_tiles(i, nr, load, sub, lambda kb, side: None, trips=trips)
        for r, rw in enumerate(rows):
            l0, l1 = l_ref[slab(r, 0), :], l_ref[slab(r, 1), :]
            y_ref[rw, :] = jnp.where(lo, acc_ref[slab(r, 0), :] / l0, acc_ref[slab(r, 1), :] / l1).astype(BF16)
            lse_ref[0, rw, :] = jnp.where(lo, m_ref[slab(r, 0), :] + jnp.log(l0), m_ref[slab(r, 1), :] + jnp.log(l1))

    state = [pltpu.VMEM((2 * nr * t, LANES), F32)] * 3
    qblk, kvfull = _pair_specs(s_len, t * nr)
    return _ride_call(
        body, name, (N_HEADS // 2, nq),
        [qblk, kvfull, kvfull,
         qblk, pl.BlockSpec((N_HEADS, s_len), lambda hp, i: (0, 0)),
         pl.BlockSpec(memory_space=pltpu.SMEM), pl.BlockSpec(memory_space=pltpu.SMEM)],
        [qblk, pl.BlockSpec((1, t * nr, LANES), lambda hp, i: (hp, i, 0))],
        [jax.ShapeDtypeStruct((s_len, ATT_W), BF16), jax.ShapeDtypeStruct((N_HEADS // 2, s_len, LANES), F32)],
        state, ("parallel", "parallel"), (q, k, v, f_wide, f_row, cap, f_row[:, t - 1::t]), ride)


def _fox_bwd(q, k, v, dy, y, lse, f_wide, f_row, cap, name, ride=None):
    s_len = q.shape[0]
    t, nr, nq = _att_tiling(s_len, FOX_ROWS)

    def body(q_ref, k_ref, v_ref, dy_ref, y_ref, lse_ref, f_ref, ft_ref, cap_ref, fl_ref,
             dq_ref, dk_ref, dv_ref, dft_ref, dqs_ref, rsum_ref):
        hp = pl.program_id(0)
        i = pl.program_id(1)

        @pl.when(i == 0)
        def _():
            dk_ref[...] = jnp.zeros_like(dk_ref)
            dv_ref[...] = jnp.zeros_like(dv_ref)
            dft_ref[...] = jnp.zeros_like(dft_ref)

        dqs_ref[...] = jnp.zeros_like(dqs_ref)
        rsum_ref[...] = jnp.zeros_like(rsum_ref)
        slab = lambda r, j: pl.ds((2 * r + j) * t, t)

        lane = lax.broadcasted_iota(jnp.int32, (t, LANES), 1)
        lo = lane < HEAD_DIM
        causal = lax.broadcasted_iota(jnp.int32, (t, t), 0) >= lax.broadcasted_iota(jnp.int32, (t, t), 1)
        rows = [pl.ds(r * t, t) for r in range(nr)]
        qst, dyst, delta, lse, fq = [], [], [], [], []
        for rw in rows:
            qst.append(_stacked_halves(q_ref[rw, :] * jnp.asarray(HEAD_DIM ** -0.5, BF16), lo))
            dyb = dy_ref[rw, :]
            dyst.append(_stacked_halves(dyb, lo))
            prod = dyb.astype(F32) * y_ref[rw, :].astype(F32)
            delta.append([jnp.sum(jnp.where(lo, prod, 0.0), axis=1, keepdims=True),
                          jnp.sum(jnp.where(lo, 0.0, prod), axis=1, keepdims=True)])
            lse_b = lse_ref[0, rw, :]
            lse.append([lse_b[:, 0:1], lse_b[:, HEAD_DIM:HEAD_DIM + 1]])
            fq.append([f_ref[rw, j * HEAD_DIM:j * HEAD_DIM + 1] for j in (0, 1)])

        q_all, dy_all = _stack(qst), _stack(dyst)

        def load(kb):
            k0 = pl.multiple_of(kb * t, t)
            frow = [ft_ref[pl.ds(2 * hp + j, 1), pl.ds(k0, t)] for j in (0, 1)]
            return k_ref[pl.ds(k0, t), :], v_ref[pl.ds(k0, t), :], frow

        def sub(rs, tiles, masked):
            kblk, vblk, frow = tiles
            qs, dys = (q_all, dy_all) if len(rs) == nr else (_stack([qst[r] for r in rs]), _stack([dyst[r] for r in rs]))
            z = _dot_nt(qs, kblk)
            dp = _dot_nt(dys, vblk)
            slabs = [(r, j) for r in range(len(rs)) for j in (0, 1)]
            pb, dsb, col = [], [], [None, None]
            for n, (r, j) in enumerate(slabs):
                sl = slice(n * t, (n + 1) * t)
                s = z[sl, :] + (fq[rs[r]][j] - frow[j])
                p = jnp.exp(s - lse[rs[r]][j])
                if masked[r]:
                    p = jnp.where(causal, p, 0.0)
                ds = p * (dp[sl, :] - delta[rs[r]][j])
                c = jnp.sum(ds, axis=0, keepdims=True)
                col[j] = c if col[j] is None else col[j] + c
                rsum_ref[slab(rs[r], j), :] += jnp.sum(ds, axis=1, keepdims=True)
                pb.append(p.astype(BF16))
                dsb.append(ds.astype(BF16))
            p_all, ds_all = _stack(pb), _stack(dsb)
            dqs_ref[pl.ds(2 * rs[0] * t, len(slabs) * t), :] += _dot(ds_all, kblk)
            return _dot_tn(ds_all, qs), _dot_tn(p_all, dys), col

        def flush(kb, side):
            k0 = pl.multiple_of(kb * t, t)
            dk_ref[pl.ds(k0, t), :] += side[0]
            dv_ref[pl.ds(k0, t), :] += side[1]
            for j in (0, 1):
                dft_ref[0, pl.ds(j, 1), pl.ds(k0, t)] -= side[2][j]

        trips = _fox_trips(hp, fl_ref, cap_ref, fq, lambda r, j: lse[r][j])
        _walk_tiles(i, nr, load, sub, flush, trips=trips)
        for r, rw in enumerate(rows):
            dq_ref[rw, :] = jnp.where(lo, dqs_ref[slab(r, 0), :], dqs_ref[slab(r, 1), :]) * (HEAD_DIM ** -0.5)
            rs_t = jnp.where(lo, rsum_ref[slab(r, 0), :], rsum_ref[slab(r, 1), :]).T
            q0 = pl.multiple_of((i * nr + r) * t, t)
            for j in (0, 1):
                dft_ref[0, pl.ds(j, 1), pl.ds(q0, t)] += rs_t[j * HEAD_DIM:j * HEAD_DIM + 1, :]

    state = [pltpu.VMEM((2 * nr * t, LANES), F32), pltpu.VMEM((2 * nr * t, 1), F32)]
    qblk, kvfull = _pair_specs(s_len, t * nr)
    return _ride_call(
        body, name, (N_HEADS // 2, nq),
        [qblk, kvfull, kvfull, qblk, qblk,
         pl.BlockSpec((1, t * nr, LANES), lambda hp, i: (hp, i, 0)),
         qblk, pl.BlockSpec((N_HEADS, s_len), lambda hp, i: (0, 0)),
         pl.BlockSpec(memory_space=pltpu.SMEM), pl.BlockSpec(memory_space=pltpu.SMEM)],
        [qblk, kvfull, kvfull, pl.BlockSpec((1, 8, s_len), lambda hp, i: (hp, 0, 0))],
        [jax.ShapeDtypeStruct((s_len, ATT_W), F32)] * 3 + [jax.ShapeDtypeStruct((N_HEADS // 2, 8, s_len), F32)],
        state, ("arbitrary", "arbitrary"), (q, k, v, dy, y, lse, f_wide, f_row, cap, f_row[:, t - 1::t]), ride)


def _sb_more(c_ref, t):
    def more(r):
        return (jnp.max(c_ref[pl.ds(2 * r * t, 2 * t), :]) > -EXP_ZERO).astype(jnp.int32)
    return more


def _stacked_split_dot(slabs, m, parts):
    split = [_split(x, parts) for x in slabs]
    acc = None
    for p in range(parts):
        d = _dot(_stack([s[p] for s in split]), m)
        acc = d if acc is None else acc + d
    return acc


def _sb_weights(z, c, strict, upper, t):
    logs = []
    for n in range(z.shape[0] // t):
        zn = z[n * t:(n + 1) * t, :]
        sp = _softplus_neg_abs(zn)
        l1m = jnp.minimum(-zn, 0.0) - sp
        if strict[n] is not None:
            l1m = jnp.where(strict[n], l1m, 0.0)
        logs.append((jnp.minimum(zn, 0.0) - sp, l1m))
    suf = _stacked_split_dot([l1m for _, l1m in logs], upper, 1)
    out = []
    for n, (logb, l1m) in enumerate(logs):
        after = c[n] + suf[n * t:(n + 1) * t, :]
        a = jnp.exp(logb + after)
        if strict[n] is not None:
            a = jnp.where(strict[n], a, 0.0)
        out.append((logb, a, after[:, 0:1] + l1m[:, 0:1]))
    return out


def _sb_fwd(q, k, v, name):
    s_len = q.shape[0]
    t, nr, nq = _att_tiling(s_len, SB_ROWS)

    def body(q_ref, k_ref, v_ref, y_ref, yf_ref, c_ref, acc_ref):
        i = pl.program_id(1)
        lane = lax.broadcasted_iota(jnp.int32, (t, LANES), 1)
        lo = lane < HEAD_DIM
        ri = lax.broadcasted_iota(jnp.int32, (t, t), 0)
        ci = lax.broadcasted_iota(jnp.int32, (t, t), 1)
        strict = ci < ri
        upper = (ri > ci).astype(BF16)
        rows = [pl.ds(r * t, t) for r in range(nr)]
        slab = lambda r, j: pl.ds((2 * r + j) * t, t)
        qst = [_stacked_halves(q_ref[rw, :] * jnp.asarray(HEAD_DIM ** -0.5, BF16), lo) for rw in rows]
        q_all = _stack(qst)
        c_ref[...] = jnp.zeros_like(c_ref)
        acc_ref[...] = jnp.zeros_like(acc_ref)

        def load(kb):
            k0 = pl.multiple_of(kb * t, t)
            return k_ref[pl.ds(k0, t), :], v_ref[pl.ds(k0, t), :]

        def sub(rs, tiles, masked):
            kblk, vblk = tiles
            z = _dot_nt(q_all if len(rs) == nr else _stack([qst[r] for r in rs]), kblk)
            slabs = [(r, j) for r in range(len(rs)) for j in (0, 1)]
            w = _sb_weights(z, [c_ref[slab(rs[r], j), :] for r, j in slabs],
                            [strict if masked[r] else None for r, j in slabs], upper, t)
            for n, (r, j) in enumerate(slabs):
                c_ref[slab(rs[r], j), :] = w[n][2]
            acc_ref[pl.ds(2 * rs[0] * t, len(slabs) * t), :] += _dot(_stack([a.astype(BF16) for _, a, _ in w]), vblk)
            return None

        _walk_tiles(i, nr, load, sub, lambda kb, side: None, more=_sb_more(c_ref, t))
        for r, rw in enumerate(rows):
            y = jnp.where(lo, acc_ref[slab(r, 0), :], acc_ref[slab(r, 1), :])
            y_ref[rw, :] = y.astype(BF16)
            yf_ref[rw, :] = y

    qblk, kvfull = _pair_specs(s_len, t * nr)
    return pl.pallas_call(
        body, name=name, grid=(N_HEADS // 2, nq),
        in_specs=[qblk, kvfull, kvfull],
        out_specs=[qblk, qblk],
        out_shape=[jax.ShapeDtypeStruct((s_len, ATT_W), BF16), jax.ShapeDtypeStruct((s_len, ATT_W), F32)],
        scratch_shapes=[pltpu.VMEM((2 * nr * t, 1), F32), pltpu.VMEM((2 * nr * t, LANES), F32)],
        compiler_params=_cparams("parallel", "parallel"),
    )(q, k, v)


def _sb_bwd(q, k, v, dy, yf, name):
    s_len = q.shape[0]
    t, nr, nq = _att_tiling(s_len, SB_ROWS)

    def body(q_ref, k_ref, v_ref, dy_ref, yf_ref, dq_ref, dk_ref, dv_ref, c_ref, e_ref, dqs_ref):
        i = pl.program_id(1)

        @pl.when(i == 0)
        def _():
            dk_ref[...] = jnp.zeros_like(dk_ref)
            dv_ref[...] = jnp.zeros_like(dv_ref)

        c_ref[...] = jnp.zeros_like(c_ref)
        e_ref[...] = jnp.zeros_like(e_ref)
        dqs_ref[...] = jnp.zeros_like(dqs_ref)
        slab = lambda r, j: pl.ds((2 * r + j) * t, t)

        lane = lax.broadcasted_iota(jnp.int32, (t, LANES), 1)
        lo = lane < HEAD_DIM
        ri = lax.broadcasted_iota(jnp.int32, (t, t), 0)
        ci = lax.broadcasted_iota(jnp.int32, (t, t), 1)
        strict = ci < ri
        upper = (ri > ci).astype(BF16)
        upper_incl = (ri >= ci).astype(BF16)
        rows = [pl.ds(r * t, t) for r in range(nr)]
        qst, dyst, delta = [], [], []
        for rw in rows:
            qst.append(_stacked_halves(q_ref[rw, :] * jnp.asarray(HEAD_DIM ** -0.5, BF16), lo))
            dyb = dy_ref[rw, :]
            dyst.append(_stacked_halves(dyb, lo))
            prod = dyb.astype(F32) * yf_ref[rw, :]
            delta.append([jnp.sum(jnp.where(lo, prod, 0.0), axis=1, keepdims=True),
                          jnp.sum(jnp.where(lo, 0.0, prod), axis=1, keepdims=True)])
        q_all, dy_all = _stack(qst), _stack(dyst)

        def load(kb):
            k0 = pl.multiple_of(kb * t, t)
            return k_ref[pl.ds(k0, t), :], v_ref[pl.ds(k0, t), :]

        def sub(rs, tiles, masked):
            kblk, vblk = tiles
            qs, dys = (q_all, dy_all) if len(rs) == nr else (_stack([qst[r] for r in rs]), _stack([dyst[r] for r in rs]))
            slabs = [(r, j) for r in range(len(rs)) for j in (0, 1)]
            z = _dot_nt(qs, kblk)
            w = _sb_weights(z, [c_ref[slab(rs[r], j), :] for r, j in slabs],
                            [strict if masked[r] else None for r, j in slabs], upper, t)
            da = _dot_nt(dys, vblk)
            ab = [a.astype(BF16) for _, a, _ in w]
            dl = [ab[n].astype(F32) * da[n * t:(n + 1) * t, :] for n in range(len(slabs))]
            tail = _stacked_split_dot(dl, upper_incl, 2)
            dzb = []
            for n, (r, j) in enumerate(slabs):
                sl = slab(rs[r], j)
                tl = tail[n * t:(n + 1) * t, :]
                e = e_ref[sl, :]
                dl1m = (delta[rs[r]][j] - e) - tl
                e_ref[sl, :] = e + tl[:, 0:1]
                c_ref[sl, :] = w[n][2]
                dz = dl[n] - jnp.exp(w[n][0]) * (dl[n] + dl1m)
                if masked[r]:
                    dz = jnp.where(strict, dz, 0.0)
                dzb.append(dz.astype(BF16))
            a_all, dz_all = _stack(ab), _stack(dzb)
            dqs_ref[pl.ds(2 * rs[0] * t, len(slabs) * t), :] += _dot(dz_all, kblk)
            return _dot_tn(dz_all, qs), _dot_tn(a_all, dys)

        def flush(kb, side):
            k0 = pl.multiple_of(kb * t, t)
            dk_ref[pl.ds(k0, t), :] += side[0]
            dv_ref[pl.ds(k0, t), :] += side[1]

        _walk_tiles(i, nr, load, sub, flush, more=_sb_more(c_ref, t))
        for r, rw in enumerate(rows):
            dq_ref[rw, :] = jnp.where(lo, dqs_ref[slab(r, 0), :], dqs_ref[slab(r, 1), :]) * (HEAD_DIM ** -0.5)

    qblk, kvfull = _pair_specs(s_len, t * nr)
    return pl.pallas_call(
        body, name=name, grid=(N_HEADS // 2, nq),
        in_specs=[qblk, kvfull, kvfull, qblk, qblk],
        out_specs=[qblk, kvfull, kvfull],
        out_shape=[jax.ShapeDtypeStruct((s_len, ATT_W), F32)] * 3,
        scratch_shapes=[pltpu.VMEM((2 * nr * t, 1), F32), pltpu.VMEM((2 * nr * t, 1), F32),
                        pltpu.VMEM((2 * nr * t, LANES), F32)],
        compiler_params=_cparams("arbitrary", "arbitrary"),
    )(q, k, v, dy, yf)


def _merge_fwd(x, yf, ys, gf, gs, wbf, wbs, wo, name):
    s_len = x.shape[0]
    ts = min(512, s_len)

    def body(x_ref, yf_ref, ys_ref, gf_ref, gs_ref, wbf_ref, wbs_ref, wo_ref, o_ref):
        merged = (_sigmoid(gf_ref[...]) * _dot_nt(yf_ref[...], wbf_ref[...])
                  + _sigmoid(gs_ref[...]) * _dot_nt(ys_ref[...], wbs_ref[...]))
        o_ref[...] = x_ref[...] + _dot(merged.astype(BF16), wo_ref[...])

    tok = lambda w: pl.BlockSpec((ts, w), lambda i: (i, 0))
    full = lambda a: pl.BlockSpec(a.shape, lambda i: (0, 0))
    return pl.pallas_call(
        body, name=name, grid=(s_len // ts,),
        in_specs=[tok(D_MODEL), tok(ATT_W), tok(ATT_W), tok(D_MODEL), tok(D_MODEL), full(wbf), full(wbs), full(wo)],
        out_specs=tok(D_MODEL),
        out_shape=jax.ShapeDtypeStruct((s_len, D_MODEL), F32),
        compiler_params=_cparams("parallel"),
    )(x, yf, ys, gf, gs, wbf, wbs, wo)


def _merge_bwd(dx, yf, ys, gf, gs, wbf, wbs, wo, name):
    s_len = dx.shape[0]
    ts = min(512, s_len)

    def body(dx_ref, yf_ref, ys_ref, gf_ref, gs_ref, wbf_ref, wbs_ref, wo_ref,
             dyf_ref, dys_ref, dgf_ref, dgs_ref, dbf_ref, dbs_ref, mg_ref):
        bf = _dot_nt(yf_ref[...], wbf_ref[...])
        bs = _dot_nt(ys_ref[...], wbs_ref[...])
        sf = _sigmoid(gf_ref[...])
        ss = _sigmoid(gs_ref[...])
        mg_ref[...] = (sf * bf + ss * bs).astype(BF16)
        dm = _dot_nt(dx_ref[...].astype(BF16), wo_ref[...])
        dbf = (dm * sf).astype(BF16)
        dbs = (dm * ss).astype(BF16)
        dbf_ref[...] = dbf
        dbs_ref[...] = dbs
        dgf_ref[...] = (dm * bf * (sf * (1.0 - sf))).astype(BF16)
        dgs_ref[...] = (dm * bs * (ss * (1.0 - ss))).astype(BF16)
        dyf_ref[...] = _dot(dbf, wbf_ref[...]).astype(BF16)
        dys_ref[...] = _dot(dbs, wbs_ref[...]).astype(BF16)

    tok = lambda w: pl.BlockSpec((ts, w), lambda i: (i, 0))
    full = lambda a: pl.BlockSpec(a.shape, lambda i: (0, 0))
    b16o = lambda w: jax.ShapeDtypeStruct((s_len, w), BF16)
    return pl.pallas_call(
        body, name=name, grid=(s_len // ts,),
        in_specs=[tok(D_MODEL), tok(ATT_W), tok(ATT_W), tok(D_MODEL), tok(D_MODEL), full(wbf), full(wbs), full(wo)],
        out_specs=[tok(ATT_W), tok(ATT_W)] + [tok(D_MODEL)] * 5,
        out_shape=[b16o(ATT_W), b16o(ATT_W)] + [b16o(D_MODEL)] * 5,
        compiler_params=_cparams("parallel"),
    )(dx, yf, ys, gf, gs, wbf, wbs, wo)


def _mix_bwd(x, dx_in, gain, w_in, fqr, fkr, dfqn, dfkn, qg, kg, dfv, df_col, logf, dsq, dsk, dsv, dgf, dgs, name):
    s_len = x.shape[0]
    ts = min(256, s_len)
    nt = s_len // ts
    gmat = _head_group_matrix()

    def body(x_ref, dxi_ref, gain_ref, w_ref, fqr_ref, fkr_ref, dfqn_ref, dfkn_ref, qg_ref, kg_ref, gm_ref,
             dfv_ref, df_ref, logf_ref, dsq_ref, dsk_ref, dsv_ref, dgf_ref, dgs_ref,
             dp_ref, dx_ref, dgain_ref, dqg_ref, dkg_ref, dbias_ref, carry):
        i = pl.program_id(0)

        @pl.when(i == 0)
        def _():
            carry[...] = jnp.zeros_like(carry)
            dgain_ref[...] = jnp.zeros_like(dgain_ref)
            dqg_ref[...] = jnp.zeros_like(dqg_ref)
            dkg_ref[...] = jnp.zeros_like(dkg_ref)
            dbias_ref[...] = jnp.zeros_like(dbias_ref)

        gm = gm_ref[...]

        def headnorm_bwd(raw, dout, g, dg_ref):
            ms = _dot_split(raw * raw, gm, HEAD_SUM_PARTS) * (1.0 / HEAD_DIM)
            r = lax.rsqrt(ms + EPS)
            nrm = raw * r
            dg_ref[...] += jnp.sum(dout * nrm, axis=0, keepdims=True)
            dn = dout * g
            mean_h = _dot_split(dn * nrm, gm, HEAD_SUM_PARTS) * (1.0 / HEAD_DIM)
            return r * (dn - nrm * mean_h)

        dp_ref[:, C_FQ:C_FQ + ATT_W] = headnorm_bwd(fqr_ref[...], dfqn_ref[...], qg_ref[...], dqg_ref).astype(BF16)
        dp_ref[:, C_FK:C_FK + ATT_W] = headnorm_bwd(fkr_ref[...], dfkn_ref[...], kg_ref[...], dkg_ref).astype(BF16)
        dp_ref[:, C_FV:C_FV + ATT_W] = dfv_ref[...].astype(BF16)
        dp_ref[:, C_SQ:C_SQ + ATT_W] = dsq_ref[...].astype(BF16)
        dp_ref[:, C_SK:C_SK + ATT_W] = dsk_ref[...].astype(BF16)
        dp_ref[:, C_SV:C_SV + ATT_W] = dsv_ref[...].astype(BF16)
        dp_ref[:, C_GF:C_GF + D_MODEL] = dgf_ref[...]
        dp_ref[:, C_GS:C_GS + D_MODEL] = dgs_ref[...]

        r_ = lax.broadcasted_iota(jnp.int32, (ts, ts), 0)
        c_ = lax.broadcasted_iota(jnp.int32, (ts, ts), 1)
        rev = (c_ >= r_).astype(BF16)
        dlogf = _dot_split_left(rev, df_ref[...], 3) + carry[...]
        carry[...] = dlogf[0:1, :]
        lane = lax.broadcasted_iota(jnp.int32, (ts, LANES), 1)
        dfl = jnp.where(lane < N_HEADS, dlogf * (1.0 - jnp.exp(logf_ref[...])), 0.0)
        dbias_ref[...] += jnp.sum(dfl, axis=0, keepdims=True)
        dp_ref[:, C_FL:C_FL + LANES] = dfl.astype(BF16)
        dp_ref[:, C_FL + LANES:C_SQ] = jnp.zeros((ts, C_SQ - C_FL - LANES), BF16)

        dh = _dot(dp_ref[...], w_ref[...])
        xf = x_ref[...]
        r = _rms_rinv(xf)
        xhat = xf * r
        dgain_ref[...] += jnp.sum(dh * xhat, axis=0, keepdims=True)
        dn = dh * gain_ref[...]
        dx_ref[...] = dxi_ref[...] + r * (dn - xhat * jnp.mean(dn * xhat, axis=-1, keepdims=True))

    tok = lambda w: pl.BlockSpec((ts, w), lambda i: (nt - 1 - i, 0))
    full = lambda a: pl.BlockSpec(a.shape, lambda i: (0, 0))
    row = lambda w: pl.BlockSpec((1, w), lambda i: (0, 0))
    return _ride_call(
        body, name, (nt,),
        [tok(D_MODEL), tok(D_MODEL), full(gain), full(w_in), tok(ATT_W), tok(ATT_W), tok(ATT_W), tok(ATT_W),
         full(qg), full(kg), full(gmat), tok(ATT_W), tok(LANES), tok(LANES), tok(ATT_W), tok(ATT_W), tok(ATT_W),
         tok(D_MODEL), tok(D_MODEL)],
        [tok(IN_PAD), tok(D_MODEL), row(D_MODEL), row(ATT_W), row(ATT_W), row(LANES)],
        [jax.ShapeDtypeStruct((s_len, IN_PAD), BF16), jax.ShapeDtypeStruct((s_len, D_MODEL), F32),
         jax.ShapeDtypeStruct((1, D_MODEL), F32), jax.ShapeDtypeStruct((1, ATT_W), F32),
         jax.ShapeDtypeStruct((1, ATT_W), F32), jax.ShapeDtypeStruct((1, LANES), F32)],
        [pltpu.VMEM((1, LANES), F32)], ("arbitrary",),
        (x, dx_in, gain, w_in, fqr, fkr, dfqn, dfkn, qg, kg, gmat, dfv, df_col, logf, dsq, dsk, dsv, dgf, dgs), None)


def _ple_loss(x, p, tgt, gain, wpg, wpp, name):
    s_len = x.shape[0]
    ts = min(512, s_len)

    def body(x_ref, p_ref, t_ref, gain_ref, wpg_ref, wpp_ref, dx_ref, n_ref, ds_ref, dpp_ref, dgain_ref, loss_ref):
        i = pl.program_id(0)

        @pl.when(i == 0)
        def _():
            dgain_ref[...] = jnp.zeros_like(dgain_ref)
            loss_ref[...] = jnp.zeros_like(loss_ref)

        xf = x_ref[...]
        r = _rms_rinv(xf)
        n = xf * r
        hn = (n * gain_ref[...]).astype(BF16)
        n_ref[...] = hn
        sg = _sigmoid(_dot(hn, wpg_ref[...]))
        pp = _dot_nt(p_ref[...].astype(BF16), wpp_ref[...])
        err = (xf + sg * pp) - t_ref[...]
        sq = jnp.sum(jnp.sum(err * err, axis=1, keepdims=True), axis=0, keepdims=True)
        loss_ref[...] += (0.5 / D_MODEL) * sq
        dout = err * (1.0 / D_MODEL)
        dpp_ref[...] = (dout * sg).astype(BF16)
        ds = (dout * pp * (sg * (1.0 - sg))).astype(BF16)
        ds_ref[...] = ds
        dhn = _dot_nt(ds, wpg_ref[...])
        dgain_ref[...] += jnp.sum(dhn * n, axis=0, keepdims=True)
        dn = dhn * gain_ref[...]
        dx_ref[...] = dout + r * (dn - n * jnp.mean(dn * n, axis=-1, keepdims=True))

    tok = lambda w: pl.BlockSpec((ts, w), lambda i: (i, 0))
    full = lambda a: pl.BlockSpec(a.shape, lambda i: (0, 0))
    return pl.pallas_call(
        body, name=name, grid=(s_len // ts,),
        in_specs=[tok(D_MODEL), tok(PLE_DIM), tok(D_MODEL), full(gain), full(wpg), full(wpp)],
        out_specs=[tok(D_MODEL), tok(D_MODEL), tok(D_MODEL), tok(D_MODEL),
                   pl.BlockSpec((1, D_MODEL), lambda i: (0, 0)), pl.BlockSpec((8, LANES), lambda i: (0, 0))],
        out_shape=[jax.ShapeDtypeStruct((s_len, D_MODEL), F32), jax.ShapeDtypeStruct((s_len, D_MODEL), BF16),
                   jax.ShapeDtypeStruct((s_len, D_MODEL), BF16), jax.ShapeDtypeStruct((s_len, D_MODEL), BF16),
                   jax.ShapeDtypeStruct((1, D_MODEL), F32), jax.ShapeDtypeStruct((8, LANES), F32)],
        compiler_params=_cparams("arbitrary"),
    )(x, p, tgt, gain, wpg, wpp)


def _exchange(x, name, broadcast):
    def body(x_ref, out_ref, send_sems, recv_sems, local_sem):
        _exchange_start(x_ref, out_ref, send_sems, recv_sems, local_sem, broadcast)
        _exchange_wait(x_ref, out_ref, send_sems, recv_sems, local_sem, broadcast)

    return pl.pallas_call(
        body, name=name,
        in_specs=[EXCHANGE_SPEC],
        out_specs=EXCHANGE_SPEC,
        out_shape=_exchange_shape(x, broadcast),
        scratch_shapes=list(EXCHANGE_SEMS),
        compiler_params=pltpu.CompilerParams(has_side_effects=True),
    )(x)


def _gather_two_level(x, name):
    def body(x_ref, out_ref, send_sems, recv_sems, local_sem):
        mx, my, mc = lax.axis_index("x"), lax.axis_index("y"), lax.axis_index("c")
        me, sibling = (mx, my, mc), (mx, my, 1 - mc)
        chips = [(1 - mx, my), (mx, 1 - my), (1 - mx, 1 - my)]

        def slot(px, py, pc):
            return out_ref.at[4 * px + 2 * py + pc]

        def copy(k, block, to, src=None):
            return pltpu.make_async_remote_copy(
                src_ref=slot(*block) if src is None else src, dst_ref=slot(*block),
                send_sem=send_sems.at[k], recv_sem=recv_sems.at[k], device_id=to, device_id_type=MESH)

        mine = pltpu.make_async_copy(x_ref, slot(*me), local_sem)
        mine.start()
        first = [copy(0, me, sibling, src=x_ref)]
        first += [copy(1 + j, me, (*chip, mc), src=x_ref) for j, chip in enumerate(chips)]
        for cp in first:
            cp.start()
        passed = [copy(4 + j, (*chip, mc), sibling) for j, chip in enumerate(chips)]
        for j, chip in enumerate(chips):
            copy(1 + j, (*chip, mc), me).wait_recv()
            passed[j].start()
        copy(0, sibling, me).wait_recv()
        for j, chip in enumerate(chips):
            copy(4 + j, (*chip, 1 - mc), me).wait_recv()
        for cp in first + passed:
            cp.wait_send()
        mine.wait()

    return pl.pallas_call(
        body, name=name,
        in_specs=[EXCHANGE_SPEC],
        out_specs=EXCHANGE_SPEC,
        out_shape=_exchange_shape(x, True),
        scratch_shapes=list(EXCHANGE_SEMS),
        compiler_params=pltpu.CompilerParams(has_side_effects=True),
    )(x)


EXCHANGE_SPEC = pl.BlockSpec(memory_space=pl.ANY)
EXCHANGE_SEMS = (pltpu.SemaphoreType.DMA((N_DEV - 1,)), pltpu.SemaphoreType.DMA((N_DEV - 1,)), pltpu.SemaphoreType.DMA)


def _exchange_shape(x, broadcast):
    return jax.ShapeDtypeStruct((N_DEV,) + tuple(x.shape if broadcast else x.shape[1:]), x.dtype)


def _exchange_copies(x_ref, out_ref, send_sems, recv_sems, local_sem, broadcast, with_recv=True):
    mx, my, mc = lax.axis_index("x"), lax.axis_index("y"), lax.axis_index("c")
    me = 4 * mx + 2 * my + mc

    def src(idx):
        return x_ref if broadcast else x_ref.at[idx]

    local = pltpu.make_async_copy(src(me), out_ref.at[me], local_sem)
    pairs = []
    for k in range(1, N_DEV):
        px = (1 - mx) if k & 4 else mx
        py = (1 - my) if k & 2 else my
        pc = (1 - mc) if k & 1 else mc
        peer = 4 * px + 2 * py + pc
        sems = dict(send_sem=send_sems.at[k - 1], recv_sem=recv_sems.at[k - 1], device_id=(px, py, pc), device_id_type=MESH)
        recv = pltpu.make_async_remote_copy(src_ref=src(peer), dst_ref=out_ref.at[peer], **sems) if with_recv else None
        pairs.append((pltpu.make_async_remote_copy(src_ref=src(peer), dst_ref=out_ref.at[me], **sems), recv))
    return local, pairs


def _exchange_start(*refs_and_mode):
    local, pairs = _exchange_copies(*refs_and_mode, with_recv=False)
    local.start()
    for send, _ in pairs:
        send.start()


def _exchange_wait(*refs_and_mode):
    local, pairs = _exchange_copies(*refs_and_mode)
    for _, recv in pairs:
        recv.wait_recv()
    for send, _ in pairs:
        send.wait_send()
    local.wait()


def _riding(body, grid, n_in, n_out, ride):
    if ride is None:
        return body
    broadcast = ride[1]

    def wrapped(*refs):
        ins, x_ref = refs[:n_in], refs[n_in]
        outs, out_ref = refs[n_in + 1:n_in + 1 + n_out], refs[n_in + 1 + n_out]
        scratch, sems = refs[n_in + 2 + n_out:-3], refs[-3:]
        step = pl.program_id(0)
        for d in range(1, len(grid)):
            step = step * grid[d] + pl.program_id(d)
        total = 1
        for g in grid:
            total *= g

        @pl.when(step == 0)
        def _():
            _exchange_start(x_ref, out_ref, *sems, broadcast)

        body(*ins, *outs, *scratch)

        @pl.when(step == total - 1)
        def _():
            _exchange_wait(x_ref, out_ref, *sems, broadcast)

    return wrapped


def _ride_call(body, name, grid, in_specs, out_specs, out_shape, scratch_shapes, sem, operands, ride):
    if ride is None:
        return pl.pallas_call(body, name=name, grid=grid, in_specs=in_specs, out_specs=out_specs, out_shape=out_shape,
                              scratch_shapes=scratch_shapes, compiler_params=_cparams(*sem))(*operands)
    return pl.pallas_call(
        _riding(body, grid, len(in_specs), len(out_specs), ride), name=name, grid=grid,
        in_specs=list(in_specs) + [EXCHANGE_SPEC], out_specs=list(out_specs) + [EXCHANGE_SPEC],
        out_shape=list(out_shape) + [_exchange_shape(*ride)],
        scratch_shapes=list(scratch_shapes) + list(EXCHANGE_SEMS),
        compiler_params=_cparams(*(["arbitrary"] * len(grid))),
    )(*operands, ride[0])


def _adamw_math(w, g, m, v):
    m2 = ADAM_B1 * m + (1.0 - ADAM_B1) * g
    v2 = ADAM_B2 * v + (1.0 - ADAM_B2) * (g * g)
    m_hat = m2 / (1.0 - ADAM_B1 ** ADAM_STEP)
    v_hat = v2 / (1.0 - ADAM_B2 ** ADAM_STEP)
    delta = -ADAM_LR * (m_hat / (jnp.sqrt(v_hat) + ADAM_EPS) + ADAM_WD * w)
    return delta, m2, v2


def _sum_parts(parts, name, tr):
    _, rows, cols = parts.shape

    def body(p_ref, g_ref):
        g = p_ref[0].astype(F32)
        for s in range(1, N_DEV):
            g = g + p_ref[s].astype(F32)
        g_ref[...] = g

    return pl.pallas_call(
        body, name=name, grid=(rows // tr,),
        in_specs=[pl.BlockSpec((N_DEV, tr, cols), lambda i: (0, i, 0))],
        out_specs=pl.BlockSpec((tr, cols), lambda i: (i, 0)),
        out_shape=jax.ShapeDtypeStruct((rows, cols), F32),
        compiler_params=_cparams("parallel"),
    )(parts)


ADAM_SPLIT_ELEMS = 400_000


def _adamw_shard(g, w, m, v, name):
    rows, cols = w.shape
    tr = rows // 2 if rows * cols > ADAM_SPLIT_ELEMS else rows

    def body(g_ref, w_ref, m_ref, v_ref, d_ref, m2_ref, v2_ref):
        d_ref[...], m2_ref[...], v2_ref[...] = _adamw_math(w_ref[...], g_ref[...], m_ref[...], v_ref[...])

    blk = pl.BlockSpec((tr, cols), lambda i: (i, 0))
    return pl.pallas_call(
        body, name=name, grid=(rows // tr,),
        in_specs=[blk] * 4, out_specs=[blk] * 3,
        out_shape=[jax.ShapeDtypeStruct((rows, cols), F32)] * 3,
        compiler_params=_cparams("parallel"),
    )(g, w, m, v)


def _adamw_small(parts, w, m, v, name):
    names = list(SMALL_NAMES)

    def body(p_ref, *refs):
        ins, outs = refs[:3 * len(names)], refs[3 * len(names):]
        total = p_ref[0]
        for s in range(1, N_DEV):
            total = total + p_ref[s]
        for i, n in enumerate(names):
            row, off, width = SMALL_POS[n]
            g = total[row:row + 1, off:off + width]
            w_ref, m_ref, v_ref = ins[3 * i:3 * i + 3]
            g_ref, d_ref, m2_ref, v2_ref = outs[4 * i:4 * i + 4]
            g_ref[...] = g
            d_ref[...], m2_ref[...], v2_ref[...] = _adamw_math(w_ref[...], g, m_ref[...], v_ref[...])
        row, off, _ = SMALL_POS["loss"]
        outs[-1][...] = total[row:row + 1, off:off + 1]

    operands = [parts] + [t[n] for n in names for t in (w, m, v)]
    shapes = [jax.ShapeDtypeStruct(w[n].shape, F32) for n in names for _ in range(4)]
    shapes.append(jax.ShapeDtypeStruct((1, 1), F32))
    out = pl.pallas_call(body, name=name, out_shape=shapes)(*operands)
    return {n: tuple(out[4 * i:4 * i + 4]) for i, n in enumerate(names)}, out[-1]


TRANSPOSED = frozenset(("ffn1_w_gate", "ffn1_w_up", "w_in", "w_branch_fox", "w_branch_sb", "ffn2_w_gate", "ffn2_w_up",
                        "w_ple_proj"))
F_PAD_ROWS = C_SQ - FL_REAL_END


def _pack(pieces, group, dtype):
    out = []
    for name in GATHER_GROUPS[group]:
        r = pieces[name].T if name in TRANSPOSED else pieces[name]
        r = r.reshape(-1, D_MODEL).astype(dtype)
        if r.shape[0] != PACK_ROWS[name]:
            r = jnp.pad(r, ((0, PACK_ROWS[name] - r.shape[0]), (0, 0)))
        out.append(r)
    return jnp.concatenate(out, axis=0)


def _real_rows(name):
    return W_IN_ROWS if name == "w_in" else PACK_ROWS[name]


def _gathered(got, name, shape):
    off = GATHER_OFF[name]
    return got[:, off:off + _real_rows(name), :].reshape(shape)


def _w_in_device_rows(d):
    lo, hi = d * W_IN_ROWS, (d + 1) * W_IN_ROWS
    if hi <= FL_REAL_END:
        return [(lo, hi)]
    if lo >= FL_REAL_END:
        return [(lo + F_PAD_ROWS, hi + F_PAD_ROWS)]
    return [(lo, FL_REAL_END), (C_SQ, hi + F_PAD_ROWS)]


def _w_in_t_padded(got):
    t = _gathered(got, "w_in", (IN_REAL, D_MODEL))
    return jnp.concatenate([t[:FL_REAL_END], jnp.zeros((F_PAD_ROWS, D_MODEL), t.dtype), t[FL_REAL_END:]], axis=0)


def _pack_chunks(grads, group):
    out = []
    for name in SCATTER_GROUPS[group]:
        base, _, half = name.partition("#")
        g = grads[base].astype(BF16)
        if base == "w_in":
            lo, hi = W_IN_HALVES[int(half)]
            tail = jnp.zeros((PACK_ROWS[base] - W_IN_ROWS, D_MODEL), BF16)
            c = jnp.stack([jnp.concatenate([g[a:b] for a, b in _w_in_device_rows(d)] + [tail], axis=0)[lo:hi]
                           for d in range(N_DEV)])
        else:
            c = g.reshape(N_DEV, PACK_ROWS[name], D_MODEL)
        out.append(c)
    return out[0] if len(out) == 1 else jnp.concatenate(out, axis=1)


def _shard_grad(summed, name, shape):
    if name == "w_in":
        rows = jnp.concatenate([summed[f"w_in#{i}"] for i in range(len(W_IN_HALVES))], axis=0)[:W_IN_ROWS]
    else:
        rows = summed[name][SCATTER_OFF[name]:SCATTER_OFF[name] + PACK_ROWS[name], :]
    return rows.reshape(shape[1], shape[0]).T if name in TRANSPOSED else rows.reshape(shape)


WEIGHT_NAMES = ['ffn1_norm', 'ffn1_w_gate', 'ffn1_w_up', 'ffn1_w_down', 'mix_norm', 'w_in', 'forget_bias', 'q_norm',
                'k_norm', 'w_branch_fox', 'w_branch_sb', 'w_out', 'ffn2_norm', 'ffn2_w_gate', 'ffn2_w_up',
                'ffn2_w_down', 'ple_norm', 'w_ple_gate', 'w_ple_proj']
SMALL_NAMES = ('ffn1_norm', 'mix_norm', 'ffn2_norm', 'ple_norm', 'q_norm', 'k_norm', 'forget_bias')
SMALL_POS = {'ffn1_norm': (0, 0, D_MODEL), 'mix_norm': (1, 0, D_MODEL), 'ffn2_norm': (2, 0, D_MODEL),
             'ple_norm': (3, 0, D_MODEL), 'q_norm': (4, 0, HEAD_DIM), 'k_norm': (4, HEAD_DIM, HEAD_DIM),
             'forget_bias': (4, 2 * HEAD_DIM, N_HEADS), 'loss': (4, 2 * HEAD_DIM + N_HEADS, 1)}


def _pack_small(vals, loss):
    tail = [vals[n].reshape(1, -1) for n in ('q_norm', 'k_norm', 'forget_bias')] + [loss.reshape(1, 1)]
    tail.append(jnp.zeros((1, D_MODEL - sum(t.shape[1] for t in tail)), F32))
    rows = [vals[n].reshape(1, D_MODEL) for n in SMALL_NAMES[:4]] + [jnp.concatenate(tail, axis=1)]
    rows.append(jnp.zeros((SMALL_ROWS - len(rows), D_MODEL), F32))
    return jnp.concatenate(rows, axis=0)


def _step(x, p, tgt, w):
    row = lambda a: a.reshape(1, -1).astype(F32)
    g_ffn1, g_mix, g_ffn2, g_ple = (row(w[n]) for n in SMALL_NAMES[:4])
    qg = jnp.tile(row(w['q_norm']), (1, N_HEADS))
    kg = jnp.tile(row(w['k_norm']), (1, N_HEADS))
    bias = jnp.pad(row(w['forget_bias']), ((0, 0), (0, LANES - N_HEADS)))
    half = D_FF // 2
    grads = {}

    blk = lambda n: GATHER_OFF[n] // FFN_SHARD
    ffn1 = tuple(blk(n) for n in ("ffn1_w_gate", "ffn1_w_up", "ffn1_w_down"))
    ffn2 = tuple(blk(n) for n in ("ffn2_w_gate", "ffn2_w_up", "ffn2_w_down"))
    got0 = _gather_two_level(_pack(w, 0, BF16), "gather_ffn1")
    x1, g1, u1, h1, got1 = _ffn_fwd(x, g_ffn1, got0, ffn1, "ffn1_fwd", ride=(_pack(w, 1, BF16), True))
    w_in = _w_in_t_padded(got1)
    wbf = _gathered(got1, "w_branch_fox", (D_MODEL, ATT_W))
    wbs = _gathered(got1, "w_branch_sb", (D_MODEL, ATT_W))
    wo = _gathered(got1, "w_out", (D_MODEL, D_MODEL))
    (hmix, fqr, fkr, fqn, fkn, fv, logf, f_col, f_row, sq, sk, sv, gf, gs) = _mix_fwd(
        x1, g_mix, w_in, bias, qg, kg, "mix_fwd")
    f_wide = jnp.repeat(f_col[:, :N_HEADS], HEAD_DIM, axis=1)
    cap = _fox_qk_cap(w['q_norm'], w['k_norm'])
    y_fox, lse, got2 = _fox_fwd(fqn, fkn, fv, f_wide, f_row, cap, "fox_fwd", ride=(_pack(w, 2, BF16), True))
    wpg = _gathered(got2, "w_ple_gate", (D_MODEL, D_MODEL))
    wpp = _gathered(got2, "w_ple_proj", (D_MODEL, PLE_DIM))
    y_sb, y_sb32 = _sb_fwd(sq, sk, sv, "sb_fwd")
    x2 = _merge_fwd(x1, y_fox, y_sb, gf, gs, wbf, wbs, wo, "merge_fwd")
    x3, g2, u2, h2, = _ffn_fwd(x2, g_ffn2, got2, ffn2, "ffn2_fwd")
    dx3, n_ple, ds_ple, dpp, dg_ple, loss = _ple_loss(x3, p, tgt, g_ple, wpg, wpp, "ple_loss")

    grads['w_ple_gate'] = _wgrad(n_ple, ds_ple, "dw_ple_gate", D_MODEL, D_MODEL)
    grads['w_ple_proj'] = _wgrad(dpp, p, "dw_ple_proj", D_MODEL, PLE_DIM)
    dg2, du2, act2, dx2, dg_ffn2 = _ffn_bwd_fused(x2, dx3, g_ffn2, g2, u2, got2, ffn2, "ffn2_bwd")
    grads['ffn2_w_gate'] = _wgrad(dg2, h2, "dw_ffn2_gate", half, D_MODEL)
    grads['ffn2_w_up'] = _wgrad(du2, h2, "dw_ffn2_up", half, D_MODEL)
    grads['ffn2_w_down'] = _wgrad(act2, dx3, "dw_ffn2_down", half, D_MODEL)
    dyf, dys, dgf, dgs, dbf, dbs, merged = _merge_bwd(dx2, y_fox, y_sb, gf, gs, wbf, wbs, wo, "merge_bwd")
    grads['w_branch_fox'] = _wgrad(dbf, y_fox, "dw_branch_fox", D_MODEL, ATT_W)
    grads['w_branch_sb'] = _wgrad(dbs, y_sb, "dw_branch_sb", D_MODEL, ATT_W)
    grads['w_out'] = _wgrad(merged, dx2, "dw_out", D_MODEL, D_MODEL)
    dfqn, dfkn, dfv, dft, part_rest = _fox_bwd(fqn, fkn, fv, dyf, y_fox, lse, f_wide, f_row, cap, "fox_bwd",
                                               ride=(_pack_chunks(grads, 5), False))
    dsq, dsk, dsv = _sb_bwd(sq, sk, sv, dys, y_sb32, "sb_bwd")
    s_len = x.shape[0]
    df_col = jnp.pad(dft[:, :2, :].reshape(N_HEADS, s_len).T, ((0, 0), (0, LANES - N_HEADS)))
    dproj, dx1, dg_mix, dqg, dkg, dbias = _mix_bwd(
        x1, dx2, g_mix, w_in, fqr, fkr, dfqn, dfkn, qg, kg, dfv, df_col, logf, dsq, dsk, dsv, dgf, dgs, "mix_bwd")
    grads['w_in'] = _wgrad(dproj, hmix, "dw_in", IN_PAD // 3, D_MODEL)
    dg1, du1, act1, dx0, dg_ffn1, part_in0 = _ffn_bwd_fused(x, dx1, g_ffn1, g1, u1, got0, ffn1, "ffn1_bwd",
                                                            ride=(_pack_chunks(grads, 3), False))
    grads['ffn1_w_gate'], part_in1 = _wgrad(dg1, h1, "dw_ffn1_gate", half, D_MODEL,
                                            ride=(_pack_chunks(grads, 4), False))
    grads['ffn1_w_up'], part_gate = _wgrad(du1, h1, "dw_ffn1_up", half, D_MODEL, ride=(_pack_chunks(grads, 0), False))
    grads['ffn1_w_down'], part_up = _wgrad(act1, dx1, "dw_ffn1_down", half, D_MODEL,
                                           ride=(_pack_chunks(grads, 1), False))
    part_down = _exchange(_pack_chunks(grads, 2), "scatter_ffn1_down", False)

    fold = lambda a: a.reshape(N_HEADS, HEAD_DIM).sum(axis=0).reshape(1, HEAD_DIM)
    small_g = {'ffn1_norm': dg_ffn1, 'mix_norm': dg_mix, 'ffn2_norm': dg_ffn2, 'ple_norm': dg_ple,
               'q_norm': fold(dqg), 'k_norm': fold(dkg), 'forget_bias': dbias[:, :N_HEADS]}
    return loss[0, 0], dx0, (part_gate, part_up, part_down, part_in0, part_in1, part_rest), small_g


def kernel(x, p, ffn1_norm, ffn1_w_gate, ffn1_w_up, ffn1_w_down, mix_norm, w_in, forget_bias, q_norm, k_norm, w_branch_fox, w_branch_sb, w_out, ffn2_norm, ffn2_w_gate, ffn2_w_up, ffn2_w_down, ple_norm, w_ple_gate, w_ple_proj, loss_target, m_ffn1_norm, m_ffn1_w_gate, m_ffn1_w_up, m_ffn1_w_down, m_mix_norm, m_w_in, m_forget_bias, m_q_norm, m_k_norm, m_w_branch_fox, m_w_branch_sb, m_w_out, m_ffn2_norm, m_ffn2_w_gate, m_ffn2_w_up, m_ffn2_w_down, m_ple_norm, m_w_ple_gate, m_w_ple_proj, v_ffn1_norm, v_ffn1_w_gate, v_ffn1_w_up, v_ffn1_w_down, v_mix_norm, v_w_in, v_forget_bias, v_q_norm, v_k_norm, v_w_branch_fox, v_w_branch_sb, v_w_out, v_ffn2_norm, v_ffn2_w_gate, v_ffn2_w_up, v_ffn2_w_down, v_ple_norm, v_w_ple_gate, v_w_ple_proj):
    args = dict(locals())
    w = {n: args[n][0] for n in WEIGHT_NAMES}
    m = {n: args["m_" + n][0] for n in WEIGHT_NAMES}
    v = {n: args["v_" + n][0] for n in WEIGHT_NAMES}
    loss, dx, parts, small_g = _step(x[0], p[0, 0], loss_target[0], w)

    summed = {}
    for grp, part in enumerate(parts):
        s = _sum_parts(part, f"sum_grads_{grp}", SUM_TILE_ROWS[grp])
        summed.update({n: s for n in SCATTER_GROUPS[grp]})
    big = {}
    for n in WEIGHT_NAMES:
        if n not in SMALL_NAMES:
            g = _shard_grad(summed, n, w[n].shape)
            big[n] = (g,) + tuple(_adamw_shard(g, w[n], m[n], v[n], "adamw_" + n))
    small_parts = _exchange(_pack_small(small_g, loss), "gather_small", True)
    small, total_loss = _adamw_small(small_parts, *({n: args[pre + n] for n in SMALL_NAMES} for pre in ("", "m_", "v_")),
                                     "adamw_small")
    big.update(small)

    outs = [total_loss.reshape(()), dx.reshape(x.shape)]
    for kind in range(4):
        outs += [big[n][kind].reshape(args[n].shape) for n in WEIGHT_NAMES]
    return tuple(outs)
```

```python
import jax
import jax.numpy as jnp
from jax import lax
from jax.experimental import pallas as pl
from jax.experimental.pallas import tpu as pltpu

F32 = jnp.float32
BF16 = jnp.bfloat16

D_MODEL = 1024
D_FF = 2816
N_HEADS = 8
HEAD_DIM = 64
ATT_W = N_HEADS * HEAD_DIM
PLE_DIM = 256
EPS = 1e-6
N_DEV = 8
MESH = pl.DeviceIdType.MESH

LANES = 128
V7X_SCOPED_VMEM_BYTES = 56 * 1024 * 1024

C_FQ, C_FK, C_FV, C_FL = 0, 512, 1024, 1536
C_SQ, C_SK, C_SV, C_GF, C_GS = 1792, 2304, 2816, 3328, 4352
IN_PAD = 5376
IN_REAL = 5128
FL_REAL_END = 1544

ADAM_LR = 0.001
ADAM_B1 = 0.9
ADAM_B2 = 0.999
ADAM_EPS = 1e-08
ADAM_WD = 0.01
ADAM_STEP = 10

PACK_ROWS = {"ffn1_w_gate": 352, "ffn1_w_up": 352, "ffn1_w_down": 352, "w_in": 656, "w_branch_fox": 64,
             "w_branch_sb": 64, "w_out": 128, "ffn2_w_gate": 352, "ffn2_w_up": 352, "ffn2_w_down": 352,
             "w_ple_gate": 128, "w_ple_proj": 32}
GATHER_GROUPS = (
    ("ffn1_w_gate", "ffn1_w_up", "ffn1_w_down"),
    ("w_in", "w_branch_fox", "w_branch_sb", "w_out"),
    ("ffn2_w_gate", "ffn2_w_up"),
    ("ffn2_w_down", "w_ple_gate", "w_ple_proj"),
)
W_IN_HALVES = ((0, 336), (336, 656))
PACK_ROWS.update({f"w_in#{i}": hi - lo for i, (lo, hi) in enumerate(W_IN_HALVES)})
SCATTER_GROUPS = (
    ("ffn1_w_gate",), ("ffn1_w_up",), ("ffn1_w_down",),
    ("w_in#0",), ("w_in#1",),
    ("ffn2_w_gate", "ffn2_w_up", "ffn2_w_down", "w_ple_gate", "w_ple_proj", "w_branch_fox", "w_branch_sb", "w_out"),
)
SUM_TILE_ROWS = (352, 352, 352, 336, 320, 368)


def _offsets(groups):
    off = {}
    for grp in groups:
        o = 0
        for n in grp:
            off[n] = o
            o += PACK_ROWS[n]
    return off


GATHER_OFF = _offsets(GATHER_GROUPS)
SCATTER_OFF = _offsets(SCATTER_GROUPS)
W_IN_ROWS = 641

SMALL_ROWS = 8


def _cparams(*sem):
    return pltpu.CompilerParams(dimension_semantics=sem, vmem_limit_bytes=V7X_SCOPED_VMEM_BYTES)


def _dot(a, b):
    return jnp.dot(a, b, preferred_element_type=F32)


def _dot_nt(a, b):
    return lax.dot_general(a, b, (((1,), (1,)), ((), ())), preferred_element_type=F32)


def _dot_tn(a, b):
    return lax.dot_general(a, b, (((0,), (0,)), ((), ())), preferred_element_type=F32)


def _split(x, parts):
    out = []
    r = x
    for _ in range(parts):
        p = r.astype(BF16)
        out.append(p)
        r = r - p.astype(F32)
    return out


def _dot_split(x, m, parts):
    acc = None
    for p in _split(x, parts):
        t = _dot(p, m)
        acc = t if acc is None else acc + t
    return acc


def _dot_split_left(m, x, parts):
    acc = None
    for p in _split(x, parts):
        t = _dot(m, p)
        acc = t if acc is None else acc + t
    return acc


def _rms_rinv(xf):
    return lax.rsqrt(jnp.mean(xf * xf, axis=-1, keepdims=True) + EPS)


def _sigmoid(x):
    return 1.0 / (1.0 + jnp.exp(-x))


def _softplus_neg_abs(z):
    return jnp.log(1.0 + jnp.exp(-jnp.abs(z)))


FFN_SHARD = D_FF // N_DEV
FFN_CHUNK = 4


def _ffn_w_spec(blk, index_map):
    return pl.BlockSpec((FFN_CHUNK, FFN_SHARD, D_MODEL), lambda *g: (index_map(*g), blk, 0))


def _ffn_w(ref):
    return ref[...].reshape(FFN_CHUNK * FFN_SHARD, D_MODEL)


def _ffn_fwd(x, gain, wbufs, blks, name, ride=None):
    s_len = x.shape[0]
    ts = min(512, s_len)
    fc = FFN_CHUNK * FFN_SHARD
    nt, nc = s_len // ts, D_FF // fc

    def body(x_ref, gain_ref, wg_ref, wu_ref, wd_ref, y_ref, g_ref, u_ref, h_ref, acc_scr):
        j = pl.program_id(1)

        @pl.when(j == 0)
        def _():
            xf = x_ref[...]
            h_ref[...] = ((xf * _rms_rinv(xf)) * gain_ref[...]).astype(BF16)
            acc_scr[...] = jnp.zeros_like(acc_scr)

        h = h_ref[...]
        g = _dot_nt(h, _ffn_w(wg_ref))
        u = _dot_nt(h, _ffn_w(wu_ref))
        g_ref[...] = g.astype(BF16)
        u_ref[...] = u.astype(BF16)
        a = (g * _sigmoid(g) * u).astype(BF16)
        acc_scr[...] += _dot(a, _ffn_w(wd_ref))

        @pl.when(j == nc - 1)
        def _():
            y_ref[...] = x_ref[...] + 0.5 * acc_scr[...]

    tok = pl.BlockSpec((ts, D_MODEL), lambda i, j: (i, 0))
    hid = pl.BlockSpec((ts, fc), lambda i, j: (i, j))
    return _ride_call(
        body, name, (nt, nc),
        [tok, pl.BlockSpec((1, D_MODEL), lambda i, j: (0, 0))] + [_ffn_w_spec(b, lambda i, j: j) for b in blks],
        [tok, hid, hid, tok],
        [jax.ShapeDtypeStruct((s_len, D_MODEL), F32), jax.ShapeDtypeStruct((s_len, D_FF), BF16),
         jax.ShapeDtypeStruct((s_len, D_FF), BF16), jax.ShapeDtypeStruct((s_len, D_MODEL), BF16)],
        [pltpu.VMEM((ts, D_MODEL), F32)], ("parallel", "arbitrary"), (x, gain, *wbufs), ride)


def _ffn_bwd_fused(x, dy, gain, g, u, wbufs, blks, name, ride=None):
    s_len = x.shape[0]
    ts = min(512, s_len)
    fc = FFN_CHUNK * FFN_SHARD
    nt = s_len // ts
    assert D_FF == 2 * fc

    def hidden(dy_ref, g_ref, u_ref, wg_ref, wu_ref, wd_ref, dg_ref, du_ref, act_ref):
        da = 0.5 * _dot_nt(dy_ref[...].astype(BF16), _ffn_w(wd_ref))
        gf = g_ref[...].astype(F32)
        uf = u_ref[...].astype(F32)
        sg = _sigmoid(gf)
        silu = gf * sg
        dg = (da * uf * (sg * (1.0 + gf * (1.0 - sg)))).astype(BF16)
        du = (da * silu).astype(BF16)
        dg_ref[...] = dg
        du_ref[...] = du
        act_ref[...] = (0.5 * silu * uf).astype(BF16)
        return _dot(dg, _ffn_w(wg_ref)) + _dot(du, _ffn_w(wu_ref))

    def first(dy_ref, g_ref, u_ref, wg_ref, wu_ref, wd_ref, dg_ref, du_ref, act_ref, dh_ref):
        dh_ref[...] = hidden(dy_ref, g_ref, u_ref, wg_ref, wu_ref, wd_ref, dg_ref, du_ref, act_ref)

    def second(x_ref, dy_ref, gain_ref, g_ref, u_ref, wg_ref, wu_ref, wd_ref, dh0_ref, dg_half, du_half, act_half,
               dg_ref, du_ref, act_ref, dx_ref, dgain_ref):
        i = pl.program_id(0)
        dh = dh0_ref[...] + hidden(dy_ref, g_ref, u_ref, wg_ref, wu_ref, wd_ref, dg_ref, du_ref, act_ref)
        xf = x_ref[...]
        r = _rms_rinv(xf)
        xhat = xf * r
        dgp = jnp.sum(dh * xhat, axis=0, keepdims=True)

        @pl.when(i == 0)
        def _():
            dgain_ref[...] = dgp

        @pl.when(i > 0)
        def _():
            dgain_ref[...] += dgp

        dn = dh * gain_ref[...]
        dx_ref[...] = dy_ref[...] + r * (dn - xhat * jnp.mean(dn * xhat, axis=-1, keepdims=True))

    tok = pl.BlockSpec((ts, D_MODEL), lambda i: (i, 0))
    row = pl.BlockSpec((1, D_MODEL), lambda i: (0, 0))
    hid = lambda c: pl.BlockSpec((ts, fc), lambda i: (i, c))
    wts = lambda c: [pl.BlockSpec((FFN_CHUNK, FFN_SHARD, D_MODEL), lambda i, b=b: (c, b, 0),
                                  pipeline_mode=pl.Buffered(1)) for b in blks]
    hidden_shapes = [jax.ShapeDtypeStruct((s_len, D_FF), BF16)] * 3
    dg, du, act, dh0, *rode = _ride_call(
        first, name + "_a", (nt,), [tok, hid(0), hid(0)] + wts(0), [hid(0)] * 3 + [tok],
        hidden_shapes + [jax.ShapeDtypeStruct((s_len, D_MODEL), F32)], [], ("parallel",),
        (dy, g, u, *wbufs), ride)
    filled = pl.BlockSpec(memory_space=pl.ANY)
    dg, du, act, dx, dgain = pl.pallas_call(
        second, name=name + "_b", grid=(nt,),
        in_specs=[tok, tok, row, hid(1), hid(1)] + wts(1) + [tok, filled, filled, filled],
        out_specs=[hid(1)] * 3 + [tok, row],
        out_shape=hidden_shapes + [jax.ShapeDtypeStruct((s_len, D_MODEL), F32), jax.ShapeDtypeStruct((1, D_MODEL), F32)],
        input_output_aliases={9: 0, 10: 1, 11: 2},
        compiler_params=_cparams("arbitrary"),
    )(x, dy, gain, g, u, *wbufs, dh0, dg, du, act)
    return (dg, du, act, dx, dgain, *rode)


def _wgrad(a, b, name, tk, tn, ride=None):
    s_len, k_dim = a.shape
    n_dim = b.shape[1]
    ts = min(2048, s_len)
    ns = s_len // ts

    def body(a_ref, b_ref, o_ref, acc):
        s = pl.program_id(2)
        p = _dot_tn(a_ref[...].astype(BF16), b_ref[...].astype(BF16))

        @pl.when(s == 0)
        def _():
            acc[...] = p

        @pl.when(s > 0)
        def _():
            acc[...] += p

        @pl.when(s == ns - 1)
        def _():
            o_ref[...] = acc[...].astype(BF16)

    out = _ride_call(
        body, name, (k_dim // tk, n_dim // tn, ns),
        [pl.BlockSpec((ts, tk), lambda k, n, s: (s, k)), pl.BlockSpec((ts, tn), lambda k, n, s: (s, n))],
        [pl.BlockSpec((tk, tn), lambda k, n, s: (k, n))], [jax.ShapeDtypeStruct((k_dim, n_dim), BF16)],
        [pltpu.VMEM((tk, tn), F32)], ("parallel", "parallel", "arbitrary"), (a, b), ride)
    return out[0] if ride is None else tuple(out)


HEAD_SUM_PARTS = 1


def _head_group_matrix():
    r = lax.broadcasted_iota(jnp.int32, (ATT_W, ATT_W), 0) // HEAD_DIM
    c = lax.broadcasted_iota(jnp.int32, (ATT_W, ATT_W), 1) // HEAD_DIM
    return (r == c).astype(BF16)


def _mix_fwd(x, gain, w_in, bias, qg, kg, name):
    s_len = x.shape[0]
    ts = min(512, s_len)
    nt = s_len // ts
    gmat = _head_group_matrix()

    def body(x_ref, gain_ref, w_ref, bias_ref, qg_ref, kg_ref, gm_ref,
             h_ref, fqr_ref, fkr_ref, fqn_ref, fkn_ref, fv_ref, logf_ref, f_ref, ft_ref,
             sq_ref, sk_ref, sv_ref, gf_ref, gs_ref, carry):
        i = pl.program_id(0)
        xf = x_ref[...]
        h = ((xf * _rms_rinv(xf)) * gain_ref[...]).astype(BF16)
        h_ref[...] = h
        gm = gm_ref[...]

        def proj(lo, n):
            return _dot_nt(h, w_ref[lo:lo + n, :])

        def headnorm(raw, g):
            ms = _dot_split(raw * raw, gm, HEAD_SUM_PARTS) * (1.0 / HEAD_DIM)
            return ((raw * lax.rsqrt(ms + EPS)) * g).astype(BF16)

        fq = proj(C_FQ, ATT_W)
        fqr_ref[...] = fq
        fqn_ref[...] = headnorm(fq, qg_ref[...])
        fk = proj(C_FK, ATT_W)
        fkr_ref[...] = fk
        fkn_ref[...] = headnorm(fk, kg_ref[...])
        fv_ref[...] = proj(C_FV, ATT_W).astype(BF16)
        sq_ref[...] = proj(C_SQ, ATT_W).astype(BF16)
        sk_ref[...] = proj(C_SK, ATT_W).astype(BF16)
        sv_ref[...] = proj(C_SV, ATT_W).astype(BF16)
        gf_ref[...] = proj(C_GF, D_MODEL)
        gs_ref[...] = proj(C_GS, D_MODEL)

        fl = proj(C_FL, LANES) + bias_ref[...]
        lane = lax.broadcasted_iota(jnp.int32, fl.shape, 1)
        logf = jnp.where(lane < N_HEADS, jnp.minimum(fl, 0.0) - _softplus_neg_abs(fl), 0.0)
        logf_ref[...] = logf

        @pl.when(i == 0)
        def _():
            carry[...] = jnp.zeros_like(carry)

        r = lax.broadcasted_iota(jnp.int32, (ts, ts), 0)
        c = lax.broadcasted_iota(jnp.int32, (ts, ts), 1)
        tri = (r >= c).astype(BF16)
        f_tile = _dot_split_left(tri, logf, 3) + carry[...]
        f_ref[...] = f_tile
        ft_ref[...] = f_tile.T[:N_HEADS, :]
        carry[...] = f_tile[ts - 1:ts, :]

    tok = lambda w: pl.BlockSpec((ts, w), lambda i: (i, 0))
    full = lambda a: pl.BlockSpec(a.shape, lambda i: (0, 0), pipeline_mode=pl.Buffered(1))
    f32o = lambda w: jax.ShapeDtypeStruct((s_len, w), F32)
    b16o = lambda w: jax.ShapeDtypeStruct((s_len, w), BF16)
    return _ride_call(
        body, name, (nt,),
        [tok(D_MODEL), full(gain), full(w_in), full(bias), full(qg), full(kg), full(gmat)],
        [
            tok(D_MODEL), tok(ATT_W), tok(ATT_W), tok(ATT_W), tok(ATT_W), tok(ATT_W), tok(LANES), tok(LANES),
            pl.BlockSpec((N_HEADS, ts), lambda i: (0, i)),
            tok(ATT_W), tok(ATT_W), tok(ATT_W), tok(D_MODEL), tok(D_MODEL),
        ],
        [
            b16o(D_MODEL), f32o(ATT_W), f32o(ATT_W), b16o(ATT_W), b16o(ATT_W), b16o(ATT_W), f32o(LANES), f32o(LANES),
            jax.ShapeDtypeStruct((N_HEADS, s_len), F32),
            b16o(ATT_W), b16o(ATT_W), b16o(ATT_W), f32o(D_MODEL), f32o(D_MODEL),
        ],
        [pltpu.VMEM((1, LANES), F32)], ("arbitrary",), (x, gain, w_in, bias, qg, kg, gmat), None)


ATT_T = 256
SB_ROWS = 2
FOX_ROWS = 2
EXP_ZERO = 88.0


def _att_tiling(s_len, rows):
    t = min(ATT_T, s_len)
    nr = min(rows, s_len // t)
    return t, nr, s_len // (t * nr)


def _pair_specs(s_len, tq):
    qblk = pl.BlockSpec((tq, LANES), lambda hp, i: (i, hp))
    kvfull = pl.BlockSpec((s_len, LANES), lambda hp, i: (0, hp))
    return qblk, kvfull


def _walk_tiles(i, nr, load, sub, flush, more=None, trips=None):
    base = i * nr
    for kk in range(nr - 1, -1, -1):
        rs = list(range(kk, nr))
        flush(base + kk, sub(rs, load(base + kk), [r == kk for r in rs]))

    done = jnp.int32(0)
    for last in range(nr - 1, -1, -1):
        rs = list(range(last + 1))

        def visit(n, rs=rs):
            kb = base - 1 - n
            flush(kb, sub(rs, load(kb), [False] * len(rs)))

        if trips is not None:
            todo = jnp.maximum(trips(base, last) - done, 0)

            def body(it, carry, start=done, visit=visit):
                visit(start + it)
                return carry

            lax.fori_loop(0, todo, body, jnp.int32(0))
            done = done + todo
        else:
            def step(state, visit=visit, last=last):
                visit(state[0])
                return state[0] + 1, more(last)

            done, _ = lax.while_loop(lambda state: jnp.logical_and(state[0] < base, state[1] > 0), step,
                                     (done, more(last)))


def _stack(parts):
    return parts[0] if len(parts) == 1 else jnp.concatenate(parts, axis=0)


def _stacked_halves(x, lo):
    z = jnp.zeros_like(x)
    return jnp.concatenate([jnp.where(lo, x, z), jnp.where(lo, z, x)], axis=0)


def _fox_qk_cap(q_gain, k_gain):
    cap = (HEAD_DIM ** 0.5) * jnp.max(jnp.abs(q_gain)) * jnp.max(jnp.abs(k_gain))
    return (cap * 1.01 + 1.0).reshape(1).astype(F32)


def _fox_trips(hp, flast_ref, cap_ref, fq, level):
    def trips(base, r):
        gap = [jnp.max(fq[r][j] - level(r, j)) + cap_ref[0] for j in (0, 1)]

        def needed(n):
            kb = jnp.maximum(base - 1 - n, 0)
            return jnp.logical_or(gap[0] - flast_ref[2 * hp, kb] > -EXP_ZERO,
                                  gap[1] - flast_ref[2 * hp + 1, kb] > -EXP_ZERO)

        return lax.while_loop(lambda n: jnp.logical_and(n < base, needed(n)), lambda n: n + 1, jnp.int32(0))
    return trips


def _fox_fwd(q, k, v, f_wide, f_row, cap, name, ride=None):
    s_len = q.shape[0]
    t, nr, nq = _att_tiling(s_len, FOX_ROWS)

    def body(q_ref, k_ref, v_ref, f_ref, ft_ref, cap_ref, fl_ref, y_ref, lse_ref, m_ref, l_ref, acc_ref):
        hp = pl.program_id(0)
        i = pl.program_id(1)
        lane = lax.broadcasted_iota(jnp.int32, (t, LANES), 1)
        lo = lane < HEAD_DIM
        causal = lax.broadcasted_iota(jnp.int32, (t, t), 0) >= lax.broadcasted_iota(jnp.int32, (t, t), 1)
        rows = [pl.ds(r * t, t) for r in range(nr)]
        slab = lambda r, j: pl.ds((2 * r + j) * t, t)
        qst = [_stacked_halves(q_ref[rw, :] * jnp.asarray(HEAD_DIM ** -0.5, BF16), lo) for rw in rows]
        q_all = _stack(qst)
        fq = [[f_ref[rw, j * HEAD_DIM:j * HEAD_DIM + 1] for j in (0, 1)] for rw in rows]
        m_ref[...] = jnp.full(m_ref.shape, -1e30, F32)
        l_ref[...] = jnp.zeros_like(l_ref)
        acc_ref[...] = jnp.zeros_like(acc_ref)

        def load(kb):
            k0 = pl.multiple_of(kb * t, t)
            frow = [ft_ref[pl.ds(2 * hp + j, 1), pl.ds(k0, t)] for j in (0, 1)]
            return k_ref[pl.ds(k0, t), :], v_ref[pl.ds(k0, t), :], frow

        def sub(rs, tiles, masked):
            kblk, vblk, frow = tiles
            z = _dot_nt(q_all if len(rs) == nr else _stack([qst[r] for r in rs]), kblk)
            slabs = [(r, j) for r in range(len(rs)) for j in (0, 1)]
            ps, alpha = [], []
            for n, (r, j) in enumerate(slabs):
                sl = slab(rs[r], j)
                m_old = m_ref[sl, :]
                s = z[n * t:(n + 1) * t, :] + (fq[rs[r]][j] - frow[j])
                if masked[r]:
                    s = jnp.where(causal, s, -1e30)
                mj = jnp.maximum(m_old, jnp.max(s, axis=1, keepdims=True))
                aj = jnp.exp(m_old - mj)
                p = jnp.exp(s - jnp.tile(mj, (1, t // LANES)))
                m_ref[sl, :] = mj
                l_ref[sl, :] = aj * l_ref[sl, :] + jnp.sum(p, axis=1, keepdims=True)
                alpha.append(aj)
                ps.append(p.astype(BF16))
            pv = _dot(_stack(ps), vblk)
            for n, (r, j) in enumerate(slabs):
                sl = slab(rs[r], j)
                acc_ref[sl, :] = acc_ref[sl, :] * alpha[n] + pv[n * t:(n + 1) * t, :]
            return None

        trips = _fox_trips(hp, fl_ref, cap_ref, fq, lambda r, j: m_ref[slab(r, j), :])
        _walk_tiles(i, nr, load, sub, lambda kb, side: None, trips=trips)
        for r, rw in enumerate(rows):
            l0, l1 = l_ref[slab(r, 0), :], l_ref[slab(r, 1), :]
            y_ref[rw, :] = jnp.where(lo, acc_ref[slab(r, 0), :] / l0, acc_ref[slab(r, 1), :] / l1).astype(BF16)
            lse_ref[0, rw, :] = jnp.where(lo, m_ref[slab(r, 0), :] + jnp.log(l0), m_ref[slab(r, 1), :] + jnp.log(l1))

    state = [pltpu.VMEM((2 * nr * t, LANES), F32)] * 3
    qblk, kvfull = _pair_specs(s_len, t * nr)
    return _ride_call(
        body, name, (N_HEADS // 2, nq),
        [qblk, kvfull, kvfull,
         qblk, pl.BlockSpec((N_HEADS, s_len), lambda hp, i: (0, 0)),
         pl.BlockSpec(memory_space=pltpu.SMEM), pl.BlockSpec(memory_space=pltpu.SMEM)],
        [qblk, pl.BlockSpec((1, t * nr, LANES), lambda hp, i: (hp, i, 0))],
        [jax.ShapeDtypeStruct((s_len, ATT_W), BF16), jax.ShapeDtypeStruct((N_HEADS // 2, s_len, LANES), F32)],
        state, ("parallel", "parallel"), (q, k, v, f_wide, f_row, cap, f_row[:, t - 1::t]), ride)


def _fox_bwd(q, k, v, dy, y, lse, f_wide, f_row, cap, name, ride=None):
    s_len = q.shape[0]
    t, nr, nq = _att_tiling(s_len, FOX_ROWS)

    def body(q_ref, k_ref, v_ref, dy_ref, y_ref, lse_ref, f_ref, ft_ref, cap_ref, fl_ref,
             dq_ref, dk_ref, dv_ref, dft_ref, dqs_ref, rsum_ref):
        hp = pl.program_id(0)
        i = pl.program_id(1)

        @pl.when(i == 0)
        def _():
            dk_ref[...] = jnp.zeros_like(dk_ref)
            dv_ref[...] = jnp.zeros_like(dv_ref)
            dft_ref[...] = jnp.zeros_like(dft_ref)

        dqs_ref[...] = jnp.zeros_like(dqs_ref)
        rsum_ref[...] = jnp.zeros_like(rsum_ref)
        slab = lambda r, j: pl.ds((2 * r + j) * t, t)

        lane = lax.broadcasted_iota(jnp.int32, (t, LANES), 1)
        lo = lane < HEAD_DIM
        causal = lax.broadcasted_iota(jnp.int32, (t, t), 0) >= lax.broadcasted_iota(jnp.int32, (t, t), 1)
        rows = [pl.ds(r * t, t) for r in range(nr)]
        qst, dyst, delta, lse, fq = [], [], [], [], []
        for rw in rows:
            qst.append(_stacked_halves(q_ref[rw, :] * jnp.asarray(HEAD_DIM ** -0.5, BF16), lo))
            dyb = dy_ref[rw, :]
            dyst.append(_stacked_halves(dyb, lo))
            prod = dyb.astype(F32) * y_ref[rw, :].astype(F32)
            delta.append([jnp.sum(jnp.where(lo, prod, 0.0), axis=1, keepdims=True),
                          jnp.sum(jnp.where(lo, 0.0, prod), axis=1, keepdims=True)])
            lse_b = lse_ref[0, rw, :]
            lse.append([lse_b[:, 0:1], lse_b[:, HEAD_DIM:HEAD_DIM + 1]])
            fq.append([f_ref[rw, j * HEAD_DIM:j * HEAD_DIM + 1] for j in (0, 1)])

        q_all, dy_all = _stack(qst), _stack(dyst)

        def load(kb):
            k0 = pl.multiple_of(kb * t, t)
            frow = [ft_ref[pl.ds(2 * hp + j, 1), pl.ds(k0, t)] for j in (0, 1)]
            return k_ref[pl.ds(k0, t), :], v_ref[pl.ds(k0, t), :], frow

        def sub(rs, tiles, masked):
            kblk, vblk, frow = tiles
            qs, dys = (q_all, dy_all) if len(rs) == nr else (_stack([qst[r] for r in rs]), _stack([dyst[r] for r in rs]))
            z = _dot_nt(qs, kblk)
            dp = _dot_nt(dys, vblk)
            slabs = [(r, j) for r in range(len(rs)) for j in (0, 1)]
            pb, dsb, col = [], [], [None, None]
            for n, (r, j) in enumerate(slabs):
                sl = slice(n * t, (n + 1) * t)
                s = z[sl, :] + (fq[rs[r]][j] - frow[j])
                p = jnp.exp(s - lse[rs[r]][j])
                if masked[r]:
                    p = jnp.where(causal, p, 0.0)
                ds = p * (dp[sl, :] - delta[rs[r]][j])
                c = jnp.sum(ds, axis=0, keepdims=True)
                col[j] = c if col[j] is None else col[j] + c
                rsum_ref[slab(rs[r], j), :] += jnp.sum(ds, axis=1, keepdims=True)
                pb.append(p.astype(BF16))
                dsb.append(ds.astype(BF16))
            p_all, ds_all = _stack(pb), _stack(dsb)
            dqs_ref[pl.ds(2 * rs[0] * t, len(slabs) * t), :] += _dot(ds_all, kblk)
            return _dot_tn(ds_all, qs), _dot_tn(p_all, dys), col

        def flush(kb, side):
            k0 = pl.multiple_of(kb * t, t)
            dk_ref[pl.ds(k0, t), :] += side[0]
            dv_ref[pl.ds(k0, t), :] += side[1]
            for j in (0, 1):
                dft_ref[0, pl.ds(j, 1), pl.ds(k0, t)] -= side[2][j]

        trips = _fox_trips(hp, fl_ref, cap_ref, fq, lambda r, j: lse[r][j])
        _walk_tiles(i, nr, load, sub, flush, trips=trips)
        for r, rw in enumerate(rows):
            dq_ref[rw, :] = jnp.where(lo, dqs_ref[slab(r, 0), :], dqs_ref[slab(r, 1), :]) * (HEAD_DIM ** -0.5)
            rs_t = jnp.where(lo, rsum_ref[slab(r, 0), :], rsum_ref[slab(r, 1), :]).T
            q0 = pl.multiple_of((i * nr + r) * t, t)
            for j in (0, 1):
                dft_ref[0, pl.ds(j, 1), pl.ds(q0, t)] += rs_t[j * HEAD_DIM:j * HEAD_DIM + 1, :]

    state = [pltpu.VMEM((2 * nr * t, LANES), F32), pltpu.VMEM((2 * nr * t, 1), F32)]
    qblk, kvfull = _pair_specs(s_len, t * nr)
    return _ride_call(
        body, name, (N_HEADS // 2, nq),
        [qblk, kvfull, kvfull, qblk, qblk,
         pl.BlockSpec((1, t * nr, LANES), lambda hp, i: (hp, i, 0)),
         qblk, pl.BlockSpec((N_HEADS, s_len), lambda hp, i: (0, 0)),
         pl.BlockSpec(memory_space=pltpu.SMEM), pl.BlockSpec(memory_space=pltpu.SMEM)],
        [qblk, kvfull, kvfull, pl.BlockSpec((1, 8, s_len), lambda hp, i: (hp, 0, 0))],
        [jax.ShapeDtypeStruct((s_len, ATT_W), F32)] * 3 + [jax.ShapeDtypeStruct((N_HEADS // 2, 8, s_len), F32)],
        state, ("arbitrary", "arbitrary"), (q, k, v, dy, y, lse, f_wide, f_row, cap, f_row[:, t - 1::t]), ride)


def _sb_more(c_ref, t):
    def more(r):
        return (jnp.max(c_ref[pl.ds(2 * r * t, 2 * t), :]) > -EXP_ZERO).astype(jnp.int32)
    return more


def _stacked_split_dot(slabs, m, parts):
    split = [_split(x, parts) for x in slabs]
    acc = None
    for p in range(parts):
        d = _dot(_stack([s[p] for s in split]), m)
        acc = d if acc is None else acc + d
    return acc


def _sb_weights(z, c, strict, upper, t):
    logs = []
    for n in range(z.shape[0] // t):
        zn = z[n * t:(n + 1) * t, :]
        sp = _softplus_neg_abs(zn)
        l1m = jnp.minimum(-zn, 0.0) - sp
        if strict[n] is not None:
            l1m = jnp.where(strict[n], l1m, 0.0)
        logs.append((jnp.minimum(zn, 0.0) - sp, l1m))
    suf = _stacked_split_dot([l1m for _, l1m in logs], upper, 1)
    out = []
    for n, (logb, l1m) in enumerate(logs):
        after = c[n] + suf[n * t:(n + 1) * t, :]
        a = jnp.exp(logb + after)
        if strict[n] is not None:
            a = jnp.where(strict[n], a, 0.0)
        out.append((logb, a, after[:, 0:1] + l1m[:, 0:1]))
    return out


def _sb_fwd(q, k, v, name, ride=None):
    s_len = q.shape[0]
    t, nr, nq = _att_tiling(s_len, SB_ROWS)

    def body(q_ref, k_ref, v_ref, y_ref, yf_ref, c_ref, acc_ref):
        i = pl.program_id(1)
        lane = lax.broadcasted_iota(jnp.int32, (t, LANES), 1)
        lo = lane < HEAD_DIM
        ri = lax.broadcasted_iota(jnp.int32, (t, t), 0)
        ci = lax.broadcasted_iota(jnp.int32, (t, t), 1)
        strict = ci < ri
        upper = (ri > ci).astype(BF16)
        rows = [pl.ds(r * t, t) for r in range(nr)]
        slab = lambda r, j: pl.ds((2 * r + j) * t, t)
        qst = [_stacked_halves(q_ref[rw, :] * jnp.asarray(HEAD_DIM ** -0.5, BF16), lo) for rw in rows]
        q_all = _stack(qst)
        c_ref[...] = jnp.zeros_like(c_ref)
        acc_ref[...] = jnp.zeros_like(acc_ref)

        def load(kb):
            k0 = pl.multiple_of(kb * t, t)
            return k_ref[pl.ds(k0, t), :], v_ref[pl.ds(k0, t), :]

        def sub(rs, tiles, masked):
            kblk, vblk = tiles
            z = _dot_nt(q_all if len(rs) == nr else _stack([qst[r] for r in rs]), kblk)
            slabs = [(r, j) for r in range(len(rs)) for j in (0, 1)]
            w = _sb_weights(z, [c_ref[slab(rs[r], j), :] for r, j in slabs],
                            [strict if masked[r] else None for r, j in slabs], upper, t)
            for n, (r, j) in enumerate(slabs):
                c_ref[slab(rs[r], j), :] = w[n][2]
            acc_ref[pl.ds(2 * rs[0] * t, len(slabs) * t), :] += _dot(_stack([a.astype(BF16) for _, a, _ in w]), vblk)
            return None

        _walk_tiles(i, nr, load, sub, lambda kb, side: None, more=_sb_more(c_ref, t))
        for r, rw in enumerate(rows):
            y = jnp.where(lo, acc_ref[slab(r, 0), :], acc_ref[slab(r, 1), :])
            y_ref[rw, :] = y.astype(BF16)
            yf_ref[rw, :] = y

    qblk, kvfull = _pair_specs(s_len, t * nr)
    return _ride_call(
        body, name, (N_HEADS // 2, nq), [qblk, kvfull, kvfull], [qblk, qblk],
        [jax.ShapeDtypeStruct((s_len, ATT_W), BF16), jax.ShapeDtypeStruct((s_len, ATT_W), F32)],
        [pltpu.VMEM((2 * nr * t, 1), F32), pltpu.VMEM((2 * nr * t, LANES), F32)], ("parallel", "parallel"),
        (q, k, v), ride)


def _sb_bwd(q, k, v, dy, yf, name):
    s_len = q.shape[0]
    t, nr, nq = _att_tiling(s_len, SB_ROWS)

    def body(q_ref, k_ref, v_ref, dy_ref, yf_ref, dq_ref, dk_ref, dv_ref, c_ref, e_ref, dqs_ref):
        i = pl.program_id(1)

        @pl.when(i == 0)
        def _():
            dk_ref[...] = jnp.zeros_like(dk_ref)
            dv_ref[...] = jnp.zeros_like(dv_ref)

        c_ref[...] = jnp.zeros_like(c_ref)
        e_ref[...] = jnp.zeros_like(e_ref)
        dqs_ref[...] = jnp.zeros_like(dqs_ref)
        slab = lambda r, j: pl.ds((2 * r + j) * t, t)

        lane = lax.broadcasted_iota(jnp.int32, (t, LANES), 1)
        lo = lane < HEAD_DIM
        ri = lax.broadcasted_iota(jnp.int32, (t, t), 0)
        ci = lax.broadcasted_iota(jnp.int32, (t, t), 1)
        strict = ci < ri
        upper = (ri > ci).astype(BF16)
        upper_incl = (ri >= ci).astype(BF16)
        rows = [pl.ds(r * t, t) for r in range(nr)]
        qst, dyst, delta = [], [], []
        for rw in rows:
            qst.append(_stacked_halves(q_ref[rw, :] * jnp.asarray(HEAD_DIM ** -0.5, BF16), lo))
            dyb = dy_ref[rw, :]
            dyst.append(_stacked_halves(dyb, lo))
            prod = dyb.astype(F32) * yf_ref[rw, :]
            delta.append([jnp.sum(jnp.where(lo, prod, 0.0), axis=1, keepdims=True),
                          jnp.sum(jnp.where(lo, 0.0, prod), axis=1, keepdims=True)])
        q_all, dy_all = _stack(qst), _stack(dyst)

        def load(kb):
            k0 = pl.multiple_of(kb * t, t)
            return k_ref[pl.ds(k0, t), :], v_ref[pl.ds(k0, t), :]

        def sub(rs, tiles, masked):
            kblk, vblk = tiles
            qs, dys = (q_all, dy_all) if len(rs) == nr else (_stack([qst[r] for r in rs]), _stack([dyst[r] for r in rs]))
            slabs = [(r, j) for r in range(len(rs)) for j in (0, 1)]
            z = _dot_nt(qs, kblk)
            w = _sb_weights(z, [c_ref[slab(rs[r], j), :] for r, j in slabs],
                            [strict if masked[r] else None for r, j in slabs], upper, t)
            da = _dot_nt(dys, vblk)
            ab = [a.astype(BF16) for _, a, _ in w]
            dl = [ab[n].astype(F32) * da[n * t:(n + 1) * t, :] for n in range(len(slabs))]
            tail = _stacked_split_dot(dl, upper_incl, 2)
            dzb = []
            for n, (r, j) in enumerate(slabs):
                sl = slab(rs[r], j)
                tl = tail[n * t:(n + 1) * t, :]
                e = e_ref[sl, :]
                dl1m = (delta[rs[r]][j] - e) - tl
                e_ref[sl, :] = e + tl[:, 0:1]
                c_ref[sl, :] = w[n][2]
                dz = dl[n] - jnp.exp(w[n][0]) * (dl[n] + dl1m)
                if masked[r]:
                    dz = jnp.where(strict, dz, 0.0)
                dzb.append(dz.astype(BF16))
            a_all, dz_all = _stack(ab), _stack(dzb)
            dqs_ref[pl.ds(2 * rs[0] * t, len(slabs) * t), :] += _dot(dz_all, kblk)
            return _dot_tn(dz_all, qs), _dot_tn(a_all, dys)

        def flush(kb, side):
            k0 = pl.multiple_of(kb * t, t)
            dk_ref[pl.ds(k0, t), :] += side[0]
            dv_ref[pl.ds(k0, t), :] += side[1]

        _walk_tiles(i, nr, load, sub, flush, more=_sb_more(c_ref, t))
        for r, rw in enumerate(rows):
            dq_ref[rw, :] = jnp.where(lo, dqs_ref[slab(r, 0), :], dqs_ref[slab(r, 1), :]) * (HEAD_DIM ** -0.5)

    qblk, kvfull = _pair_specs(s_len, t * nr)
    return pl.pallas_call(
        body, name=name, grid=(N_HEADS // 2, nq),
        in_specs=[qblk, kvfull, kvfull, qblk, qblk],
        out_specs=[qblk, kvfull, kvfull],
        out_shape=[jax.ShapeDtypeStruct((s_len, ATT_W), F32)] * 3,
        scratch_shapes=[pltpu.VMEM((2 * nr * t, 1), F32), pltpu.VMEM((2 * nr * t, 1), F32),
                        pltpu.VMEM((2 * nr * t, LANES), F32)],
        compiler_params=_cparams("arbitrary", "arbitrary"),
    )(q, k, v, dy, yf)


def _merge_fwd(x, yf, ys, gf, gs, wbf, wbs, wo, name):
    s_len = x.shape[0]
    ts = min(512, s_len)

    def body(x_ref, yf_ref, ys_ref, gf_ref, gs_ref, wbf_ref, wbs_ref, wo_ref, o_ref):
        merged = (_sigmoid(gf_ref[...]) * _dot_nt(yf_ref[...], wbf_ref[...])
                  + _sigmoid(gs_ref[...]) * _dot_nt(ys_ref[...], wbs_ref[...]))
        o_ref[...] = x_ref[...] + _dot(merged.astype(BF16), wo_ref[...])

    tok = lambda w: pl.BlockSpec((ts, w), lambda i: (i, 0))
    full = lambda a: pl.BlockSpec(a.shape, lambda i: (0, 0))
    return pl.pallas_call(
        body, name=name, grid=(s_len // ts,),
        in_specs=[tok(D_MODEL), tok(ATT_W), tok(ATT_W), tok(D_MODEL), tok(D_MODEL), full(wbf), full(wbs), full(wo)],
        out_specs=tok(D_MODEL),
        out_shape=jax.ShapeDtypeStruct((s_len, D_MODEL), F32),
        compiler_params=_cparams("parallel"),
    )(x, yf, ys, gf, gs, wbf, wbs, wo)


def _merge_bwd(dx, yf, ys, gf, gs, wbf, wbs, wo, name):
    s_len = dx.shape[0]
    ts = min(512, s_len)

    def body(dx_ref, yf_ref, ys_ref, gf_ref, gs_ref, wbf_ref, wbs_ref, wo_ref,
             dyf_ref, dys_ref, dgf_ref, dgs_ref, dbf_ref, dbs_ref, mg_ref):
        bf = _dot_nt(yf_ref[...], wbf_ref[...])
        bs = _dot_nt(ys_ref[...], wbs_ref[...])
        sf = _sigmoid(gf_ref[...])
        ss = _sigmoid(gs_ref[...])
        mg_ref[...] = (sf * bf + ss * bs).astype(BF16)
        dm = _dot_nt(dx_ref[...].astype(BF16), wo_ref[...])
        dbf = (dm * sf).astype(BF16)
        dbs = (dm * ss).astype(BF16)
        dbf_ref[...] = dbf
        dbs_ref[...] = dbs
        dgf_ref[...] = (dm * bf * (sf * (1.0 - sf))).astype(BF16)
        dgs_ref[...] = (dm * bs * (ss * (1.0 - ss))).astype(BF16)
        dyf_ref[...] = _dot(dbf, wbf_ref[...]).astype(BF16)
        dys_ref[...] = _dot(dbs, wbs_ref[...]).astype(BF16)

    tok = lambda w: pl.BlockSpec((ts, w), lambda i: (i, 0))
    full = lambda a: pl.BlockSpec(a.shape, lambda i: (0, 0))
    b16o = lambda w: jax.ShapeDtypeStruct((s_len, w), BF16)
    return pl.pallas_call(
        body, name=name, grid=(s_len // ts,),
        in_specs=[tok(D_MODEL), tok(ATT_W), tok(ATT_W), tok(D_MODEL), tok(D_MODEL), full(wbf), full(wbs), full(wo)],
        out_specs=[tok(ATT_W), tok(ATT_W)] + [tok(D_MODEL)] * 5,
        out_shape=[b16o(ATT_W), b16o(ATT_W)] + [b16o(D_MODEL)] * 5,
        compiler_params=_cparams("parallel"),
    )(dx, yf, ys, gf, gs, wbf, wbs, wo)


def _mix_bwd(x, dx_in, gain, w_in, fqr, fkr, dfqn, dfkn, qg, kg, dfv, df_col, logf, dsq, dsk, dsv, dgf, dgs, name):
    s_len = x.shape[0]
    ts = min(256, s_len)
    nt = s_len // ts
    gmat = _head_group_matrix()

    def body(x_ref, dxi_ref, gain_ref, w_ref, fqr_ref, fkr_ref, dfqn_ref, dfkn_ref, qg_ref, kg_ref, gm_ref,
             dfv_ref, df_ref, logf_ref, dsq_ref, dsk_ref, dsv_ref, dgf_ref, dgs_ref,
             dp_ref, dx_ref, dgain_ref, dqg_ref, dkg_ref, dbias_ref, carry):
        i = pl.program_id(0)

        @pl.when(i == 0)
        def _():
            carry[...] = jnp.zeros_like(carry)
            dgain_ref[...] = jnp.zeros_like(dgain_ref)
            dqg_ref[...] = jnp.zeros_like(dqg_ref)
            dkg_ref[...] = jnp.zeros_like(dkg_ref)
            dbias_ref[...] = jnp.zeros_like(dbias_ref)

        gm = gm_ref[...]

        def headnorm_bwd(raw, dout, g, dg_ref):
            ms = _dot_split(raw * raw, gm, HEAD_SUM_PARTS) * (1.0 / HEAD_DIM)
            r = lax.rsqrt(ms + EPS)
            nrm = raw * r
            dg_ref[...] += jnp.sum(dout * nrm, axis=0, keepdims=True)
            dn = dout * g
            mean_h = _dot_split(dn * nrm, gm, HEAD_SUM_PARTS) * (1.0 / HEAD_DIM)
            return r * (dn - nrm * mean_h)

        dp_ref[:, C_FQ:C_FQ + ATT_W] = headnorm_bwd(fqr_ref[...], dfqn_ref[...], qg_ref[...], dqg_ref).astype(BF16)
        dp_ref[:, C_FK:C_FK + ATT_W] = headnorm_bwd(fkr_ref[...], dfkn_ref[...], kg_ref[...], dkg_ref).astype(BF16)
        dp_ref[:, C_FV:C_FV + ATT_W] = dfv_ref[...].astype(BF16)
        dp_ref[:, C_SQ:C_SQ + ATT_W] = dsq_ref[...].astype(BF16)
        dp_ref[:, C_SK:C_SK + ATT_W] = dsk_ref[...].astype(BF16)
        dp_ref[:, C_SV:C_SV + ATT_W] = dsv_ref[...].astype(BF16)
        dp_ref[:, C_GF:C_GF + D_MODEL] = dgf_ref[...]
        dp_ref[:, C_GS:C_GS + D_MODEL] = dgs_ref[...]

        r_ = lax.broadcasted_iota(jnp.int32, (ts, ts), 0)
        c_ = lax.broadcasted_iota(jnp.int32, (ts, ts), 1)
        rev = (c_ >= r_).astype(BF16)
        dlogf = _dot_split_left(rev, df_ref[...], 3) + carry[...]
        carry[...] = dlogf[0:1, :]
        lane = lax.broadcasted_iota(jnp.int32, (ts, LANES), 1)
        dfl = jnp.where(lane < N_HEADS, dlogf * (1.0 - jnp.exp(logf_ref[...])), 0.0)
        dbias_ref[...] += jnp.sum(dfl, axis=0, keepdims=True)
        dp_ref[:, C_FL:C_FL + LANES] = dfl.astype(BF16)
        dp_ref[:, C_FL + LANES:C_SQ] = jnp.zeros((ts, C_SQ - C_FL - LANES), BF16)

        dh = _dot(dp_ref[...], w_ref[...])
        xf = x_ref[...]
        r = _rms_rinv(xf)
        xhat = xf * r
        dgain_ref[...] += jnp.sum(dh * xhat, axis=0, keepdims=True)
        dn = dh * gain_ref[...]
        dx_ref[...] = dxi_ref[...] + r * (dn - xhat * jnp.mean(dn * xhat, axis=-1, keepdims=True))

    tok = lambda w: pl.BlockSpec((ts, w), lambda i: (nt - 1 - i, 0))
    full = lambda a: pl.BlockSpec(a.shape, lambda i: (0, 0))
    row = lambda w: pl.BlockSpec((1, w), lambda i: (0, 0))
    return _ride_call(
        body, name, (nt,),
        [tok(D_MODEL), tok(D_MODEL), full(gain), full(w_in), tok(ATT_W), tok(ATT_W), tok(ATT_W), tok(ATT_W),
         full(qg), full(kg), full(gmat), tok(ATT_W), tok(LANES), tok(LANES), tok(ATT_W), tok(ATT_W), tok(ATT_W),
         tok(D_MODEL), tok(D_MODEL)],
        [tok(IN_PAD), tok(D_MODEL), row(D_MODEL), row(ATT_W), row(ATT_W), row(LANES)],
        [jax.ShapeDtypeStruct((s_len, IN_PAD), BF16), jax.ShapeDtypeStruct((s_len, D_MODEL), F32),
         jax.ShapeDtypeStruct((1, D_MODEL), F32), jax.ShapeDtypeStruct((1, ATT_W), F32),
         jax.ShapeDtypeStruct((1, ATT_W), F32), jax.ShapeDtypeStruct((1, LANES), F32)],
        [pltpu.VMEM((1, LANES), F32)], ("arbitrary",),
        (x, dx_in, gain, w_in, fqr, fkr, dfqn, dfkn, qg, kg, gmat, dfv, df_col, logf, dsq, dsk, dsv, dgf, dgs), None)


def _ple_loss(x, p, tgt, gain, wpg, wpp, name):
    s_len = x.shape[0]
    ts = min(512, s_len)

    def body(x_ref, p_ref, t_ref, gain_ref, wpg_ref, wpp_ref, dx_ref, n_ref, ds_ref, dpp_ref, dgain_ref, loss_ref):
        i = pl.program_id(0)

        @pl.when(i == 0)
        def _():
            dgain_ref[...] = jnp.zeros_like(dgain_ref)
            loss_ref[...] = jnp.zeros_like(loss_ref)

        xf = x_ref[...]
        r = _rms_rinv(xf)
        n = xf * r
        hn = (n * gain_ref[...]).astype(BF16)
        n_ref[...] = hn
        sg = _sigmoid(_dot(hn, wpg_ref[...]))
        pp = _dot_nt(p_ref[...].astype(BF16), wpp_ref[...])
        err = (xf + sg * pp) - t_ref[...]
        sq = jnp.sum(jnp.sum(err * err, axis=1, keepdims=True), axis=0, keepdims=True)
        loss_ref[...] += (0.5 / D_MODEL) * sq
        dout = err * (1.0 / D_MODEL)
        dpp_ref[...] = (dout * sg).astype(BF16)
        ds = (dout * pp * (sg * (1.0 - sg))).astype(BF16)
        ds_ref[...] = ds
        dhn = _dot_nt(ds, wpg_ref[...])
        dgain_ref[...] += jnp.sum(dhn * n, axis=0, keepdims=True)
        dn = dhn * gain_ref[...]
        dx_ref[...] = dout + r * (dn - n * jnp.mean(dn * n, axis=-1, keepdims=True))

    tok = lambda w: pl.BlockSpec((ts, w), lambda i: (i, 0))
    full = lambda a: pl.BlockSpec(a.shape, lambda i: (0, 0))
    return pl.pallas_call(
        body, name=name, grid=(s_len // ts,),
        in_specs=[tok(D_MODEL), tok(PLE_DIM), tok(D_MODEL), full(gain), full(wpg), full(wpp)],
        out_specs=[tok(D_MODEL), tok(D_MODEL), tok(D_MODEL), tok(D_MODEL),
                   pl.BlockSpec((1, D_MODEL), lambda i: (0, 0)), pl.BlockSpec((8, LANES), lambda i: (0, 0))],
        out_shape=[jax.ShapeDtypeStruct((s_len, D_MODEL), F32), jax.ShapeDtypeStruct((s_len, D_MODEL), BF16),
                   jax.ShapeDtypeStruct((s_len, D_MODEL), BF16), jax.ShapeDtypeStruct((s_len, D_MODEL), BF16),
                   jax.ShapeDtypeStruct((1, D_MODEL), F32), jax.ShapeDtypeStruct((8, LANES), F32)],
        compiler_params=_cparams("arbitrary"),
    )(x, p, tgt, gain, wpg, wpp)


def _exchange(x, name, broadcast):
    def body(x_ref, out_ref, send_sems, recv_sems, local_sem):
        _exchange_start(x_ref, out_ref, send_sems, recv_sems, local_sem, broadcast)
        _exchange_wait(x_ref, out_ref, send_sems, recv_sems, local_sem, broadcast)

    return pl.pallas_call(
        body, name=name,
        in_specs=[EXCHANGE_SPEC],
        out_specs=EXCHANGE_SPEC,
        out_shape=_exchange_shape(x, broadcast),
        scratch_shapes=list(EXCHANGE_SEMS),
        compiler_params=pltpu.CompilerParams(has_side_effects=True),
    )(x)


def _gather_two_level(x, name):
    def body(x_ref, out_ref, send_sems, recv_sems, local_sem):
        mx, my, mc = lax.axis_index("x"), lax.axis_index("y"), lax.axis_index("c")
        me, sibling = (mx, my, mc), (mx, my, 1 - mc)
        chips = [(1 - mx, my), (mx, 1 - my), (1 - mx, 1 - my)]

        def slot(px, py, pc):
            return out_ref.at[4 * px + 2 * py + pc]

        def copy(k, block, to, src=None):
            return pltpu.make_async_remote_copy(
                src_ref=slot(*block) if src is None else src, dst_ref=slot(*block),
                send_sem=send_sems.at[k], recv_sem=recv_sems.at[k], device_id=to, device_id_type=MESH)

        mine = pltpu.make_async_copy(x_ref, slot(*me), local_sem)
        mine.start()
        first = [copy(0, me, sibling, src=x_ref)]
        first += [copy(1 + j, me, (*chip, mc), src=x_ref) for j, chip in enumerate(chips)]
        for cp in first:
            cp.start()
        passed = [copy(4 + j, (*chip, mc), sibling) for j, chip in enumerate(chips)]
        for j, chip in enumerate(chips):
            copy(1 + j, (*chip, mc), me).wait_recv()
            passed[j].start()
        copy(0, sibling, me).wait_recv()
        for j, chip in enumerate(chips):
            copy(4 + j, (*chip, 1 - mc), me).wait_recv()
        for cp in first + passed:
            cp.wait_send()
        mine.wait()

    return pl.pallas_call(
        body, name=name,
        in_specs=[EXCHANGE_SPEC],
        out_specs=EXCHANGE_SPEC,
        out_shape=_exchange_shape(x, True),
        scratch_shapes=list(EXCHANGE_SEMS),
        compiler_params=pltpu.CompilerParams(has_side_effects=True),
    )(x)


EXCHANGE_SPEC = pl.BlockSpec(memory_space=pl.ANY)
EXCHANGE_SEMS = (pltpu.SemaphoreType.DMA((N_DEV - 1,)), pltpu.SemaphoreType.DMA((N_DEV - 1,)), pltpu.SemaphoreType.DMA)


def _exchange_shape(x, broadcast):
    return jax.ShapeDtypeStruct((N_DEV,) + tuple(x.shape if broadcast else x.shape[1:]), x.dtype)


def _exchange_copies(x_ref, out_ref, send_sems, recv_sems, local_sem, broadcast, with_recv=True):
    mx, my, mc = lax.axis_index("x"), lax.axis_index("y"), lax.axis_index("c")
    me = 4 * mx + 2 * my + mc

    def src(idx):
        return x_ref if broadcast else x_ref.at[idx]

    local = pltpu.make_async_copy(src(me), out_ref.at[me], local_sem)
    pairs = []
    for k in range(1, N_DEV):
        px = (1 - mx) if k & 4 else mx
        py = (1 - my) if k & 2 else my
        pc = (1 - mc) if k & 1 else mc
        peer = 4 * px + 2 * py + pc
        sems = dict(send_sem=send_sems.at[k - 1], recv_sem=recv_sems.at[k - 1], device_id=(px, py, pc), device_id_type=MESH)
        recv = pltpu.make_async_remote_copy(src_ref=src(peer), dst_ref=out_ref.at[peer], **sems) if with_recv else None
        pairs.append((pltpu.make_async_remote_copy(src_ref=src(peer), dst_ref=out_ref.at[me], **sems), recv))
    return local, pairs


def _exchange_start(*refs_and_mode):
    local, pairs = _exchange_copies(*refs_and_mode, with_recv=False)
    local.start()
    for send, _ in pairs:
        send.start()


def _exchange_wait(*refs_and_mode):
    local, pairs = _exchange_copies(*refs_and_mode)
    for _, recv in pairs:
        recv.wait_recv()
    for send, _ in pairs:
        send.wait_send()
    local.wait()


def _riding(body, grid, n_in, n_out, ride):
    if ride is None:
        return body
    broadcast = ride[1]

    def wrapped(*refs):
        ins, x_ref = refs[:n_in], refs[n_in]
        outs, out_ref = refs[n_in + 1:n_in + 1 + n_out], refs[n_in + 1 + n_out]
        scratch, sems = refs[n_in + 2 + n_out:-3], refs[-3:]
        step = pl.program_id(0)
        for d in range(1, len(grid)):
            step = step * grid[d] + pl.program_id(d)
        total = 1
        for g in grid:
            total *= g

        @pl.when(step == 0)
        def _():
            _exchange_start(x_ref, out_ref, *sems, broadcast)

        body(*ins, *outs, *scratch)

        @pl.when(step == total - 1)
        def _():
            _exchange_wait(x_ref, out_ref, *sems, broadcast)

    return wrapped


def _ride_call(body, name, grid, in_specs, out_specs, out_shape, scratch_shapes, sem, operands, ride):
    if ride is None:
        return pl.pallas_call(body, name=name, grid=grid, in_specs=in_specs, out_specs=out_specs, out_shape=out_shape,
                              scratch_shapes=scratch_shapes, compiler_params=_cparams(*sem))(*operands)
    return pl.pallas_call(
        _riding(body, grid, len(in_specs), len(out_specs), ride), name=name, grid=grid,
        in_specs=list(in_specs) + [EXCHANGE_SPEC], out_specs=list(out_specs) + [EXCHANGE_SPEC],
        out_shape=list(out_shape) + [_exchange_shape(*ride)],
        scratch_shapes=list(scratch_shapes) + list(EXCHANGE_SEMS),
        compiler_params=_cparams(*(["arbitrary"] * len(grid))),
    )(*operands, ride[0])


def _adamw_math(w, g, m, v):
    m2 = ADAM_B1 * m + (1.0 - ADAM_B1) * g
    v2 = ADAM_B2 * v + (1.0 - ADAM_B2) * (g * g)
    m_hat = m2 / (1.0 - ADAM_B1 ** ADAM_STEP)
    v_hat = v2 / (1.0 - ADAM_B2 ** ADAM_STEP)
    delta = -ADAM_LR * (m_hat / (jnp.sqrt(v_hat) + ADAM_EPS) + ADAM_WD * w)
    return delta, m2, v2


def _sum_parts(parts, name, tr):
    _, rows, cols = parts.shape

    def body(p_ref, g_ref):
        g = p_ref[0].astype(F32)
        for s in range(1, N_DEV):
            g = g + p_ref[s].astype(F32)
        g_ref[...] = g

    return pl.pallas_call(
        body, name=name, grid=(rows // tr,),
        in_specs=[pl.BlockSpec((N_DEV, tr, cols), lambda i: (0, i, 0))],
        out_specs=pl.BlockSpec((tr, cols), lambda i: (i, 0)),
        out_shape=jax.ShapeDtypeStruct((rows, cols), F32),
        compiler_params=_cparams("parallel"),
    )(parts)


ADAM_SPLIT_ELEMS = 400_000


def _adamw_shard(g, w, m, v, name):
    rows, cols = w.shape
    tr = rows // 2 if rows * cols > ADAM_SPLIT_ELEMS else rows

    def body(g_ref, w_ref, m_ref, v_ref, d_ref, m2_ref, v2_ref):
        d_ref[...], m2_ref[...], v2_ref[...] = _adamw_math(w_ref[...], g_ref[...], m_ref[...], v_ref[...])

    blk = pl.BlockSpec((tr, cols), lambda i: (i, 0))
    return pl.pallas_call(
        body, name=name, grid=(rows // tr,),
        in_specs=[blk] * 4, out_specs=[blk] * 3,
        out_shape=[jax.ShapeDtypeStruct((rows, cols), F32)] * 3,
        compiler_params=_cparams("parallel"),
    )(g, w, m, v)


def _adamw_small(parts, w, m, v, name):
    names = list(SMALL_NAMES)

    def body(p_ref, *refs):
        ins, outs = refs[:3 * len(names)], refs[3 * len(names):]
        total = p_ref[0]
        for s in range(1, N_DEV):
            total = total + p_ref[s]
        for i, n in enumerate(names):
            row, off, width = SMALL_POS[n]
            g = total[row:row + 1, off:off + width]
            w_ref, m_ref, v_ref = ins[3 * i:3 * i + 3]
            g_ref, d_ref, m2_ref, v2_ref = outs[4 * i:4 * i + 4]
            g_ref[...] = g
            d_ref[...], m2_ref[...], v2_ref[...] = _adamw_math(w_ref[...], g, m_ref[...], v_ref[...])
        row, off, _ = SMALL_POS["loss"]
        outs[-1][...] = total[row:row + 1, off:off + 1]

    operands = [parts] + [t[n] for n in names for t in (w, m, v)]
    shapes = [jax.ShapeDtypeStruct(w[n].shape, F32) for n in names for _ in range(4)]
    shapes.append(jax.ShapeDtypeStruct((1, 1), F32))
    out = pl.pallas_call(body, name=name, out_shape=shapes)(*operands)
    return {n: tuple(out[4 * i:4 * i + 4]) for i, n in enumerate(names)}, out[-1]


TRANSPOSED = frozenset(("ffn1_w_gate", "ffn1_w_up", "w_in", "w_branch_fox", "w_branch_sb", "ffn2_w_gate", "ffn2_w_up",
                        "w_ple_proj"))
F_PAD_ROWS = C_SQ - FL_REAL_END


def _pack(pieces, group, dtype):
    out = []
    for name in GATHER_GROUPS[group]:
        r = pieces[name].T if name in TRANSPOSED else pieces[name]
        r = r.reshape(-1, D_MODEL).astype(dtype)
        if r.shape[0] != PACK_ROWS[name]:
            r = jnp.pad(r, ((0, PACK_ROWS[name] - r.shape[0]), (0, 0)))
        out.append(r)
    return jnp.concatenate(out, axis=0)


def _real_rows(name):
    return W_IN_ROWS if name == "w_in" else PACK_ROWS[name]


def _gathered(got, name, shape):
    off = GATHER_OFF[name]
    return got[:, off:off + _real_rows(name), :].reshape(shape)


def _w_in_device_rows(d):
    lo, hi = d * W_IN_ROWS, (d + 1) * W_IN_ROWS
    if hi <= FL_REAL_END:
        return [(lo, hi)]
    if lo >= FL_REAL_END:
        return [(lo + F_PAD_ROWS, hi + F_PAD_ROWS)]
    return [(lo, FL_REAL_END), (C_SQ, hi + F_PAD_ROWS)]


def _w_in_t_padded(got):
    t = _gathered(got, "w_in", (IN_REAL, D_MODEL))
    return jnp.concatenate([t[:FL_REAL_END], jnp.zeros((F_PAD_ROWS, D_MODEL), t.dtype), t[FL_REAL_END:]], axis=0)


def _pack_chunks(grads, group):
    out = []
    for name in SCATTER_GROUPS[group]:
        base, _, half = name.partition("#")
        g = grads[base].astype(BF16)
        if base == "w_in":
            lo, hi = W_IN_HALVES[int(half)]
            tail = jnp.zeros((PACK_ROWS[base] - W_IN_ROWS, D_MODEL), BF16)
            c = jnp.stack([jnp.concatenate([g[a:b] for a, b in _w_in_device_rows(d)] + [tail], axis=0)[lo:hi]
                           for d in range(N_DEV)])
        else:
            c = g.reshape(N_DEV, PACK_ROWS[name], D_MODEL)
        out.append(c)
    return out[0] if len(out) == 1 else jnp.concatenate(out, axis=1)


def _shard_grad(summed, name, shape):
    if name == "w_in":
        rows = jnp.concatenate([summed[f"w_in#{i}"] for i in range(len(W_IN_HALVES))], axis=0)[:W_IN_ROWS]
    else:
        rows = summed[name][SCATTER_OFF[name]:SCATTER_OFF[name] + PACK_ROWS[name], :]
    return rows.reshape(shape[1], shape[0]).T if name in TRANSPOSED else rows.reshape(shape)


WEIGHT_NAMES = ['ffn1_norm', 'ffn1_w_gate', 'ffn1_w_up', 'ffn1_w_down', 'mix_norm', 'w_in', 'forget_bias', 'q_norm',
                'k_norm', 'w_branch_fox', 'w_branch_sb', 'w_out', 'ffn2_norm', 'ffn2_w_gate', 'ffn2_w_up',
                'ffn2_w_down', 'ple_norm', 'w_ple_gate', 'w_ple_proj']
SMALL_NAMES = ('ffn1_norm', 'mix_norm', 'ffn2_norm', 'ple_norm', 'q_norm', 'k_norm', 'forget_bias')
SMALL_POS = {'ffn1_norm': (0, 0, D_MODEL), 'mix_norm': (1, 0, D_MODEL), 'ffn2_norm': (2, 0, D_MODEL),
             'ple_norm': (3, 0, D_MODEL), 'q_norm': (4, 0, HEAD_DIM), 'k_norm': (4, HEAD_DIM, HEAD_DIM),
             'forget_bias': (4, 2 * HEAD_DIM, N_HEADS), 'loss': (4, 2 * HEAD_DIM + N_HEADS, 1)}


def _pack_small(vals, loss):
    tail = [vals[n].reshape(1, -1) for n in ('q_norm', 'k_norm', 'forget_bias')] + [loss.reshape(1, 1)]
    tail.append(jnp.zeros((1, D_MODEL - sum(t.shape[1] for t in tail)), F32))
    rows = [vals[n].reshape(1, D_MODEL) for n in SMALL_NAMES[:4]] + [jnp.concatenate(tail, axis=1)]
    rows.append(jnp.zeros((SMALL_ROWS - len(rows), D_MODEL), F32))
    return jnp.concatenate(rows, axis=0)


def _step(x, p, tgt, w):
    row = lambda a: a.reshape(1, -1).astype(F32)
    g_ffn1, g_mix, g_ffn2, g_ple = (row(w[n]) for n in SMALL_NAMES[:4])
    qg = jnp.tile(row(w['q_norm']), (1, N_HEADS))
    kg = jnp.tile(row(w['k_norm']), (1, N_HEADS))
    bias = jnp.pad(row(w['forget_bias']), ((0, 0), (0, LANES - N_HEADS)))
    half = D_FF // 2
    grads = {}

    blk = lambda n: GATHER_OFF[n] // FFN_SHARD
    ffn1 = tuple(blk(n) for n in ("ffn1_w_gate", "ffn1_w_up", "ffn1_w_down"))
    ffn2 = tuple(blk(n) for n in ("ffn2_w_gate", "ffn2_w_up", "ffn2_w_down"))
    got0 = _gather_two_level(_pack(w, 0, BF16), "gather_ffn1")
    x1, g1, u1, h1, got1 = _ffn_fwd(x, g_ffn1, (got0,) * 3, ffn1, "ffn1_fwd", ride=(_pack(w, 1, BF16), True))
    w_in = _w_in_t_padded(got1)
    wbf = _gathered(got1, "w_branch_fox", (D_MODEL, ATT_W))
    wbs = _gathered(got1, "w_branch_sb", (D_MODEL, ATT_W))
    wo = _gathered(got1, "w_out", (D_MODEL, D_MODEL))
    (hmix, fqr, fkr, fqn, fkn, fv, logf, f_col, f_row, sq, sk, sv, gf, gs) = _mix_fwd(
        x1, g_mix, w_in, bias, qg, kg, "mix_fwd")
    f_wide = jnp.repeat(f_col[:, :N_HEADS], HEAD_DIM, axis=1)
    cap = _fox_qk_cap(w['q_norm'], w['k_norm'])
    y_fox, lse, got2 = _fox_fwd(fqn, fkn, fv, f_wide, f_row, cap, "fox_fwd", ride=(_pack(w, 2, BF16), True))
    y_sb, y_sb32, got3 = _sb_fwd(sq, sk, sv, "sb_fwd", ride=(_pack(w, 3, BF16), True))
    wpg = _gathered(got3, "w_ple_gate", (D_MODEL, D_MODEL))
    wpp = _gathered(got3, "w_ple_proj", (D_MODEL, PLE_DIM))
    ffn2_bufs = (got2, got2, got3)
    x2 = _merge_fwd(x1, y_fox, y_sb, gf, gs, wbf, wbs, wo, "merge_fwd")
    x3, g2, u2, h2, = _ffn_fwd(x2, g_ffn2, ffn2_bufs, ffn2, "ffn2_fwd")
    dx3, n_ple, ds_ple, dpp, dg_ple, loss = _ple_loss(x3, p, tgt, g_ple, wpg, wpp, "ple_loss")

    grads['w_ple_gate'] = _wgrad(n_ple, ds_ple, "dw_ple_gate", D_MODEL, D_MODEL)
    grads['w_ple_proj'] = _wgrad(dpp, p, "dw_ple_proj", D_MODEL, PLE_DIM)
    dg2, du2, act2, dx2, dg_ffn2 = _ffn_bwd_fused(x2, dx3, g_ffn2, g2, u2, ffn2_bufs, ffn2, "ffn2_bwd")
    grads['ffn2_w_gate'] = _wgrad(dg2, h2, "dw_ffn2_gate", half, D_MODEL)
    grads['ffn2_w_up'] = _wgrad(du2, h2, "dw_ffn2_up", half, D_MODEL)
    grads['ffn2_w_down'] = _wgrad(act2, dx3, "dw_ffn2_down", half, D_MODEL)
    dyf, dys, dgf, dgs, dbf, dbs, merged = _merge_bwd(dx2, y_fox, y_sb, gf, gs, wbf, wbs, wo, "merge_bwd")
    grads['w_branch_fox'] = _wgrad(dbf, y_fox, "dw_branch_fox", D_MODEL, ATT_W)
    grads['w_branch_sb'] = _wgrad(dbs, y_sb, "dw_branch_sb", D_MODEL, ATT_W)
    grads['w_out'] = _wgrad(merged, dx2, "dw_out", D_MODEL, D_MODEL)
    dfqn, dfkn, dfv, dft, part_rest = _fox_bwd(fqn, fkn, fv, dyf, y_fox, lse, f_wide, f_row, cap, "fox_bwd",
                                               ride=(_pack_chunks(grads, 5), False))
    dsq, dsk, dsv = _sb_bwd(sq, sk, sv, dys, y_sb32, "sb_bwd")
    s_len = x.shape[0]
    df_col = jnp.pad(dft[:, :2, :].reshape(N_HEADS, s_len).T, ((0, 0), (0, LANES - N_HEADS)))
    dproj, dx1, dg_mix, dqg, dkg, dbias = _mix_bwd(
        x1, dx2, g_mix, w_in, fqr, fkr, dfqn, dfkn, qg, kg, dfv, df_col, logf, dsq, dsk, dsv, dgf, dgs, "mix_bwd")
    grads['w_in'] = _wgrad(dproj, hmix, "dw_in", IN_PAD // 3, D_MODEL)
    dg1, du1, act1, dx0, dg_ffn1, part_in0 = _ffn_bwd_fused(x, dx1, g_ffn1, g1, u1, (got0,) * 3, ffn1, "ffn1_bwd",
                                                            ride=(_pack_chunks(grads, 3), False))
    grads['ffn1_w_gate'], part_in1 = _wgrad(dg1, h1, "dw_ffn1_gate", half, D_MODEL,
                                            ride=(_pack_chunks(grads, 4), False))
    grads['ffn1_w_up'], part_gate = _wgrad(du1, h1, "dw_ffn1_up", half, D_MODEL, ride=(_pack_chunks(grads, 0), False))
    grads['ffn1_w_down'], part_up = _wgrad(act1, dx1, "dw_ffn1_down", half, D_MODEL,
                                           ride=(_pack_chunks(grads, 1), False))
    part_down = _exchange(_pack_chunks(grads, 2), "scatter_ffn1_down", False)

    fold = lambda a: a.reshape(N_HEADS, HEAD_DIM).sum(axis=0).reshape(1, HEAD_DIM)
    small_g = {'ffn1_norm': dg_ffn1, 'mix_norm': dg_mix, 'ffn2_norm': dg_ffn2, 'ple_norm': dg_ple,
               'q_norm': fold(dqg), 'k_norm': fold(dkg), 'forget_bias': dbias[:, :N_HEADS]}
    return loss[0, 0], dx0, (part_gate, part_up, part_down, part_in0, part_in1, part_rest), small_g


def kernel(x, p, ffn1_norm, ffn1_w_gate, ffn1_w_up, ffn1_w_down, mix_norm, w_in, forget_bias, q_norm, k_norm, w_branch_fox, w_branch_sb, w_out, ffn2_norm, ffn2_w_gate, ffn2_w_up, ffn2_w_down, ple_norm, w_ple_gate, w_ple_proj, loss_target, m_ffn1_norm, m_ffn1_w_gate, m_ffn1_w_up, m_ffn1_w_down, m_mix_norm, m_w_in, m_forget_bias, m_q_norm, m_k_norm, m_w_branch_fox, m_w_branch_sb, m_w_out, m_ffn2_norm, m_ffn2_w_gate, m_ffn2_w_up, m_ffn2_w_down, m_ple_norm, m_w_ple_gate, m_w_ple_proj, v_ffn1_norm, v_ffn1_w_gate, v_ffn1_w_up, v_ffn1_w_down, v_mix_norm, v_w_in, v_forget_bias, v_q_norm, v_k_norm, v_w_branch_fox, v_w_branch_sb, v_w_out, v_ffn2_norm, v_ffn2_w_gate, v_ffn2_w_up, v_ffn2_w_down, v_ple_norm, v_w_ple_gate, v_w_ple_proj):
    args = dict(locals())
    w = {n: args[n][0] for n in WEIGHT_NAMES}
    m = {n: args["m_" + n][0] for n in WEIGHT_NAMES}
    v = {n: args["v_" + n][0] for n in WEIGHT_NAMES}
    loss, dx, parts, small_g = _step(x[0], p[0, 0], loss_target[0], w)

    summed = {}
    for grp, part in enumerate(parts):
        s = _sum_parts(part, f"sum_grads_{grp}", SUM_TILE_ROWS[grp])
        summed.update({n: s for n in SCATTER_GROUPS[grp]})
    big = {}
    for n in WEIGHT_NAMES:
        if n not in SMALL_NAMES:
            g = _shard_grad(summed, n, w[n].shape)
            big[n] = (g,) + tuple(_adamw_shard(g, w[n], m[n], v[n], "adamw_" + n))
    small_parts = _exchange(_pack_small(small_g, loss), "gather_small", True)
    small, total_loss = _adamw_small(small_parts, *({n: args[pre + n] for n in SMALL_NAMES} for pre in ("", "m_", "v_")),
                                     "adamw_small")
    big.update(small)

    outs = [total_loss.reshape(()), dx.reshape(x.shape)]
    for kind in range(4):
        outs += [big[n][kind].reshape(args[n].shape) for n in WEIGHT_NAMES]
    return tuple(outs)
```

```python
import jax
import jax.numpy as jnp
from jax import lax
from jax.experimental import pallas as pl
from jax.experimental.pallas import tpu as pltpu

F32 = jnp.float32
BF16 = jnp.bfloat16

D_MODEL = 1024
D_FF = 2816
N_HEADS = 8
HEAD_DIM = 64
ATT_W = N_HEADS * HEAD_DIM
PLE_DIM = 256
EPS = 1e-6
N_DEV = 8
MESH = pl.DeviceIdType.MESH

LANES = 128
V7X_SCOPED_VMEM_BYTES = 56 * 1024 * 1024

C_FQ, C_FK, C_FV, C_FL = 0, 512, 1024, 1536
C_SQ, C_SK, C_SV, C_GF, C_GS = 1792, 2304, 2816, 3328, 4352
IN_PAD = 5376
IN_REAL = 5128
FL_REAL_END = 1544

ADAM_LR = 0.001
ADAM_B1 = 0.9
ADAM_B2 = 0.999
ADAM_EPS = 1e-08
ADAM_WD = 0.01
ADAM_STEP = 10

PACK_ROWS = {"ffn1_w_gate": 352, "ffn1_w_up": 352, "ffn1_w_down": 352, "w_in": 656, "w_branch_fox": 64,
             "w_branch_sb": 64, "w_out": 128, "ffn2_w_gate": 352, "ffn2_w_up": 352, "ffn2_w_down": 352,
             "w_ple_gate": 128, "w_ple_proj": 32}
GATHER_GROUPS = (
    ("ffn1_w_gate", "ffn1_w_up", "ffn1_w_down"),
    ("w_in", "w_branch_fox", "w_branch_sb", "w_out"),
    ("ffn2_w_gate", "ffn2_w_up"),
    ("ffn2_w_down", "w_ple_gate", "w_ple_proj"),
)
W_IN_HALVES = ((0, 336), (336, 656))
PACK_ROWS.update({f"w_in#{i}": hi - lo for i, (lo, hi) in enumerate(W_IN_HALVES)})
SCATTER_GROUPS = (
    ("ffn1_w_gate",), ("ffn1_w_up",), ("ffn1_w_down",),
    ("w_in#0",), ("w_in#1",),
    ("ffn2_w_gate", "ffn2_w_up", "ffn2_w_down", "w_ple_gate", "w_ple_proj", "w_branch_fox", "w_branch_sb", "w_out"),
)
SUM_TILE_ROWS = (352, 352, 352, 336, 320, 368)


def _offsets(groups):
    off = {}
    for grp in groups:
        o = 0
        for n in grp:
            off[n] = o
            o += PACK_ROWS[n]
    return off


GATHER_OFF = _offsets(GATHER_GROUPS)
SCATTER_OFF = _offsets(SCATTER_GROUPS)
W_IN_ROWS = 641

SMALL_ROWS = 8


def _cparams(*sem):
    return pltpu.CompilerParams(dimension_semantics=sem, vmem_limit_bytes=V7X_SCOPED_VMEM_BYTES)


def _dot(a, b):
    return jnp.dot(a, b, preferred_element_type=F32)


def _dot_nt(a, b):
    return lax.dot_general(a, b, (((1,), (1,)), ((), ())), preferred_element_type=F32)


def _dot_tn(a, b):
    return lax.dot_general(a, b, (((0,), (0,)), ((), ())), preferred_element_type=F32)


def _split(x, parts):
    out = []
    r = x
    for _ in range(parts):
        p = r.astype(BF16)
        out.append(p)
        r = r - p.astype(F32)
    return out


def _dot_split(x, m, parts):
    acc = None
    for p in _split(x, parts):
        t = _dot(p, m)
        acc = t if acc is None else acc + t
    return acc


def _dot_split_left(m, x, parts):
    acc = None
    for p in _split(x, parts):
        t = _dot(m, p)
        acc = t if acc is None else acc + t
    return acc


def _rms_rinv(xf):
    return lax.rsqrt(jnp.mean(xf * xf, axis=-1, keepdims=True) + EPS)


def _sigmoid(x):
    return 1.0 / (1.0 + jnp.exp(-x))


def _softplus_neg_abs(z):
    return jnp.log(1.0 + jnp.exp(-jnp.abs(z)))


FFN_SHARD = D_FF // N_DEV
FFN_CHUNK = 4


def _ffn_w_spec(blk, index_map):
    return pl.BlockSpec((FFN_CHUNK, FFN_SHARD, D_MODEL), lambda *g: (index_map(*g), blk, 0))


def _ffn_w(ref):
    return ref[...].reshape(FFN_CHUNK * FFN_SHARD, D_MODEL)


def _ffn_fwd(x, gain, wbufs, blks, name, ride=None):
    s_len = x.shape[0]
    ts = min(512, s_len)
    fc = FFN_CHUNK * FFN_SHARD
    nt, nc = s_len // ts, D_FF // fc

    def body(x_ref, gain_ref, wg_ref, wu_ref, wd_ref, y_ref, g_ref, u_ref, h_ref, acc_scr):
        j = pl.program_id(1)

        @pl.when(j == 0)
        def _():
            xf = x_ref[...]
            h_ref[...] = ((xf * _rms_rinv(xf)) * gain_ref[...]).astype(BF16)
            acc_scr[...] = jnp.zeros_like(acc_scr)

        h = h_ref[...]
        g = _dot_nt(h, _ffn_w(wg_ref))
        u = _dot_nt(h, _ffn_w(wu_ref))
        g_ref[...] = g.astype(BF16)
        u_ref[...] = u.astype(BF16)
        a = (g * _sigmoid(g) * u).astype(BF16)
        acc_scr[...] += _dot(a, _ffn_w(wd_ref))

        @pl.when(j == nc - 1)
        def _():
            y_ref[...] = x_ref[...] + 0.5 * acc_scr[...]

    tok = pl.BlockSpec((ts, D_MODEL), lambda i, j: (i, 0))
    hid = pl.BlockSpec((ts, fc), lambda i, j: (i, j))
    return _ride_call(
        body, name, (nt, nc),
        [tok, pl.BlockSpec((1, D_MODEL), lambda i, j: (0, 0))] + [_ffn_w_spec(b, lambda i, j: j) for b in blks],
        [tok, hid, hid, tok],
        [jax.ShapeDtypeStruct((s_len, D_MODEL), F32), jax.ShapeDtypeStruct((s_len, D_FF), BF16),
         jax.ShapeDtypeStruct((s_len, D_FF), BF16), jax.ShapeDtypeStruct((s_len, D_MODEL), BF16)],
        [pltpu.VMEM((ts, D_MODEL), F32)], ("parallel", "arbitrary"), (x, gain, *wbufs), ride)


def _ffn_bwd_fused(x, dy, gain, g, u, wbufs, blks, name, ride=None):
    s_len = x.shape[0]
    ts = min(512, s_len)
    fc = FFN_CHUNK * FFN_SHARD
    nt = s_len // ts
    assert D_FF == 2 * fc

    def hidden(dy_ref, g_ref, u_ref, wg_ref, wu_ref, wd_ref, dg_ref, du_ref, act_ref):
        da = 0.5 * _dot_nt(dy_ref[...].astype(BF16), _ffn_w(wd_ref))
        gf = g_ref[...].astype(F32)
        uf = u_ref[...].astype(F32)
        sg = _sigmoid(gf)
        silu = gf * sg
        dg = (da * uf * (sg * (1.0 + gf * (1.0 - sg)))).astype(BF16)
        du = (da * silu).astype(BF16)
        dg_ref[...] = dg
        du_ref[...] = du
        act_ref[...] = (0.5 * silu * uf).astype(BF16)
        return _dot(dg, _ffn_w(wg_ref)) + _dot(du, _ffn_w(wu_ref))

    def first(dy_ref, g_ref, u_ref, wg_ref, wu_ref, wd_ref, dg_ref, du_ref, act_ref, dh_ref):
        dh_ref[...] = hidden(dy_ref, g_ref, u_ref, wg_ref, wu_ref, wd_ref, dg_ref, du_ref, act_ref)

    def second(x_ref, dy_ref, gain_ref, g_ref, u_ref, wg_ref, wu_ref, wd_ref, dh0_ref, dg_half, du_half, act_half,
               dg_ref, du_ref, act_ref, dx_ref, dgain_ref):
        i = pl.program_id(0)
        dh = dh0_ref[...] + hidden(dy_ref, g_ref, u_ref, wg_ref, wu_ref, wd_ref, dg_ref, du_ref, act_ref)
        xf = x_ref[...]
        r = _rms_rinv(xf)
        xhat = xf * r
        dgp = jnp.sum(dh * xhat, axis=0, keepdims=True)

        @pl.when(i == 0)
        def _():
            dgain_ref[...] = dgp

        @pl.when(i > 0)
        def _():
            dgain_ref[...] += dgp

        dn = dh * gain_ref[...]
        dx_ref[...] = dy_ref[...] + r * (dn - xhat * jnp.mean(dn * xhat, axis=-1, keepdims=True))

    tok = pl.BlockSpec((ts, D_MODEL), lambda i: (i, 0))
    row = pl.BlockSpec((1, D_MODEL), lambda i: (0, 0))
    hid = lambda c: pl.BlockSpec((ts, fc), lambda i: (i, c))
    wts = lambda c: [pl.BlockSpec((FFN_CHUNK, FFN_SHARD, D_MODEL), lambda i, b=b: (c, b, 0),
                                  pipeline_mode=pl.Buffered(1)) for b in blks]
    hidden_shapes = [jax.ShapeDtypeStruct((s_len, D_FF), BF16)] * 3
    dg, du, act, dh0, *rode = _ride_call(
        first, name + "_a", (nt,), [tok, hid(0), hid(0)] + wts(0), [hid(0)] * 3 + [tok],
        hidden_shapes + [jax.ShapeDtypeStruct((s_len, D_MODEL), F32)], [], ("parallel",),
        (dy, g, u, *wbufs), ride)
    filled = pl.BlockSpec(memory_space=pl.ANY)
    dg, du, act, dx, dgain = pl.pallas_call(
        second, name=name + "_b", grid=(nt,),
        in_specs=[tok, tok, row, hid(1), hid(1)] + wts(1) + [tok, filled, filled, filled],
        out_specs=[hid(1)] * 3 + [tok, row],
        out_shape=hidden_shapes + [jax.ShapeDtypeStruct((s_len, D_MODEL), F32), jax.ShapeDtypeStruct((1, D_MODEL), F32)],
        input_output_aliases={9: 0, 10: 1, 11: 2},
        compiler_params=_cparams("arbitrary"),
    )(x, dy, gain, g, u, *wbufs, dh0, dg, du, act)
    return (dg, du, act, dx, dgain, *rode)


def _wgrad(a, b, name, tk, tn, ride=None):
    s_len, k_dim = a.shape
    n_dim = b.shape[1]
    ts = min(2048, s_len)
    ns = s_len // ts

    def body(a_ref, b_ref, o_ref, acc):
        s = pl.program_id(2)
        p = _dot_tn(a_ref[...].astype(BF16), b_ref[...].astype(BF16))

        @pl.when(s == 0)
        def _():
            acc[...] = p

        @pl.when(s > 0)
        def _():
            acc[...] += p

        @pl.when(s == ns - 1)
        def _():
            o_ref[...] = acc[...].astype(BF16)

    out = _ride_call(
        body, name, (k_dim // tk, n_dim // tn, ns),
        [pl.BlockSpec((ts, tk), lambda k, n, s: (s, k)), pl.BlockSpec((ts, tn), lambda k, n, s: (s, n))],
        [pl.BlockSpec((tk, tn), lambda k, n, s: (k, n))], [jax.ShapeDtypeStruct((k_dim, n_dim), BF16)],
        [pltpu.VMEM((tk, tn), F32)], ("parallel", "parallel", "arbitrary"), (a, b), ride)
    return out[0] if ride is None else tuple(out)


HEAD_SUM_PARTS = 1


def _head_group_matrix():
    r = lax.broadcasted_iota(jnp.int32, (ATT_W, ATT_W), 0) // HEAD_DIM
    c = lax.broadcasted_iota(jnp.int32, (ATT_W, ATT_W), 1) // HEAD_DIM
    return (r == c).astype(BF16)


def _mix_fwd(x, gain, w_in, bias, qg, kg, name):
    s_len = x.shape[0]
    ts = min(512, s_len)
    nt = s_len // ts
    gmat = _head_group_matrix()

    def body(x_ref, gain_ref, w_ref, bias_ref, qg_ref, kg_ref, gm_ref,
             h_ref, fqr_ref, fkr_ref, fqn_ref, fkn_ref, fv_ref, logf_ref, f_ref, ft_ref,
             sq_ref, sk_ref, sv_ref, gf_ref, gs_ref, carry):
        i = pl.program_id(0)
        xf = x_ref[...]
        h = ((xf * _rms_rinv(xf)) * gain_ref[...]).astype(BF16)
        h_ref[...] = h
        gm = gm_ref[...]

        def proj(lo, n):
            return _dot_nt(h, w_ref[lo:lo + n, :])

        def headnorm(raw, g):
            ms = _dot_split(raw * raw, gm, HEAD_SUM_PARTS) * (1.0 / HEAD_DIM)
            return ((raw * lax.rsqrt(ms + EPS)) * g).astype(BF16)

        fq = proj(C_FQ, ATT_W)
        fqr_ref[...] = fq
        fqn_ref[...] = headnorm(fq, qg_ref[...])
        fk = proj(C_FK, ATT_W)
        fkr_ref[...] = fk
        fkn_ref[...] = headnorm(fk, kg_ref[...])
        fv_ref[...] = proj(C_FV, ATT_W).astype(BF16)
        sq_ref[...] = proj(C_SQ, ATT_W).astype(BF16)
        sk_ref[...] = proj(C_SK, ATT_W).astype(BF16)
        sv_ref[...] = proj(C_SV, ATT_W).astype(BF16)
        gf_ref[...] = proj(C_GF, D_MODEL)
        gs_ref[...] = proj(C_GS, D_MODEL)

        fl = proj(C_FL, LANES) + bias_ref[...]
        lane = lax.broadcasted_iota(jnp.int32, fl.shape, 1)
        logf = jnp.where(lane < N_HEADS, jnp.minimum(fl, 0.0) - _softplus_neg_abs(fl), 0.0)
        logf_ref[...] = logf

        @pl.when(i == 0)
        def _():
            carry[...] = jnp.zeros_like(carry)

        r = lax.broadcasted_iota(jnp.int32, (ts, ts), 0)
        c = lax.broadcasted_iota(jnp.int32, (ts, ts), 1)
        tri = (r >= c).astype(BF16)
        f_tile = _dot_split_left(tri, logf, 3) + carry[...]
        f_ref[...] = f_tile
        ft_ref[...] = f_tile.T[:N_HEADS, :]
        carry[...] = f_tile[ts - 1:ts, :]

    tok = lambda w: pl.BlockSpec((ts, w), lambda i: (i, 0))
    full = lambda a: pl.BlockSpec(a.shape, lambda i: (0, 0), pipeline_mode=pl.Buffered(1))
    f32o = lambda w: jax.ShapeDtypeStruct((s_len, w), F32)
    b16o = lambda w: jax.ShapeDtypeStruct((s_len, w), BF16)
    return _ride_call(
        body, name, (nt,),
        [tok(D_MODEL), full(gain), full(w_in), full(bias), full(qg), full(kg), full(gmat)],
        [
            tok(D_MODEL), tok(ATT_W), tok(ATT_W), tok(ATT_W), tok(ATT_W), tok(ATT_W), tok(LANES), tok(LANES),
            pl.BlockSpec((N_HEADS, ts), lambda i: (0, i)),
            tok(ATT_W), tok(ATT_W), tok(ATT_W), tok(D_MODEL), tok(D_MODEL),
        ],
        [
            b16o(D_MODEL), f32o(ATT_W), f32o(ATT_W), b16o(ATT_W), b16o(ATT_W), b16o(ATT_W), f32o(LANES), f32o(LANES),
            jax.ShapeDtypeStruct((N_HEADS, s_len), F32),
            b16o(ATT_W), b16o(ATT_W), b16o(ATT_W), f32o(D_MODEL), f32o(D_MODEL),
        ],
        [pltpu.VMEM((1, LANES), F32)], ("arbitrary",), (x, gain, w_in, bias, qg, kg, gmat), None)


ATT_T = 256
SB_ROWS = 2
FOX_ROWS = 2
EXP_ZERO = 88.0


def _att_tiling(s_len, rows):
    t = min(ATT_T, s_len)
    nr = min(rows, s_len // t)
    return t, nr, s_len // (t * nr)


def _pair_specs(s_len, tq):
    qblk = pl.BlockSpec((tq, LANES), lambda hp, i: (i, hp))
    kvfull = pl.BlockSpec((s_len, LANES), lambda hp, i: (0, hp))
    return qblk, kvfull


def _walk_tiles(i, nr, load, sub, flush, more=None, trips=None):
    base = i * nr
    for kk in range(nr - 1, -1, -1):
        rs = list(range(kk, nr))
        flush(base + kk, sub(rs, load(base + kk), [r == kk for r in rs]))

    done = jnp.int32(0)
    for last in range(nr - 1, -1, -1):
        rs = list(range(last + 1))

        def visit(n, rs=rs):
            kb = base - 1 - n
            flush(kb, sub(rs, load(kb), [False] * len(rs)))

        if trips is not None:
            todo = jnp.maximum(trips(base, last) - done, 0)

            def body(it, carry, start=done, visit=visit):
                visit(start + it)
                return carry

            lax.fori_loop(0, todo, body, jnp.int32(0))
            done = done + todo
        else:
            def step(state, visit=visit, last=last):
                visit(state[0])
                return state[0] + 1, more(last)

            done, _ = lax.while_loop(lambda state: jnp.logical_and(state[0] < base, state[1] > 0), step,
                                     (done, more(last)))


def _stack(parts):
    return parts[0] if len(parts) == 1 else jnp.concatenate(parts, axis=0)


def _stacked_halves(x, lo):
    z = jnp.zeros_like(x)
    return jnp.concatenate([jnp.where(lo, x, z), jnp.where(lo, z, x)], axis=0)


def _fox_qk_cap(q_gain, k_gain):
    cap = (HEAD_DIM ** 0.5) * jnp.max(jnp.abs(q_gain)) * jnp.max(jnp.abs(k_gain))
    return (cap * 1.01 + 1.0).reshape(1).astype(F32)


def _fox_trips(hp, flast_ref, cap_ref, fq, level):
    def trips(base, r):
        gap = [jnp.max(fq[r][j] - level(r, j)) + cap_ref[0] for j in (0, 1)]

        def needed(n):
            kb = jnp.maximum(base - 1 - n, 0)
            return jnp.logical_or(gap[0] - flast_ref[2 * hp, kb] > -EXP_ZERO,
                                  gap[1] - flast_ref[2 * hp + 1, kb] > -EXP_ZERO)

        return lax.while_loop(lambda n: jnp.logical_and(n < base, needed(n)), lambda n: n + 1, jnp.int32(0))
    return trips


def _fox_fwd(q, k, v, f_wide, f_row, cap, name, ride=None):
    s_len = q.shape[0]
    t, nr, nq = _att_tiling(s_len, FOX_ROWS)

    def body(q_ref, k_ref, v_ref, f_ref, ft_ref, cap_ref, fl_ref, y_ref, lse_ref, m_ref, l_ref, acc_ref):
        hp = pl.program_id(0)
        i = pl.program_id(1)
        lane = lax.broadcasted_iota(jnp.int32, (t, LANES), 1)
        lo = lane < HEAD_DIM
        causal = lax.broadcasted_iota(jnp.int32, (t, t), 0) >= lax.broadcasted_iota(jnp.int32, (t, t), 1)
        rows = [pl.ds(r * t, t) for r in range(nr)]
        slab = lambda r, j: pl.ds((2 * r + j) * t, t)
        qst = [_stacked_halves(q_ref[rw, :] * jnp.asarray(HEAD_DIM ** -0.5, BF16), lo) for rw in rows]
        q_all = _stack(qst)
        fq = [[f_ref[rw, j * HEAD_DIM:j * HEAD_DIM + 1] for j in (0, 1)] for rw in rows]
        m_ref[...] = jnp.full(m_ref.shape, -1e30, F32)
        l_ref[...] = jnp.zeros_like(l_ref)
        acc_ref[...] = jnp.zeros_like(acc_ref)

        def load(kb):
            k0 = pl.multiple_of(kb * t, t)
            frow = [ft_ref[pl.ds(2 * hp + j, 1), pl.ds(k0, t)] for j in (0, 1)]
            return k_ref[pl.ds(k0, t), :], v_ref[pl.ds(k0, t), :], frow

        def sub(rs, tiles, masked):
            kblk, vblk, frow = tiles
            z = _dot_nt(q_all if len(rs) == nr else _stack([qst[r] for r in rs]), kblk)
            slabs = [(r, j) for r in range(len(rs)) for j in (0, 1)]
            ps, alpha = [], []
            for n, (r, j) in enumerate(slabs):
                sl = slab(rs[r], j)
                m_old = m_ref[sl, :]
                s = z[n * t:(n + 1) * t, :] + (fq[rs[r]][j] - frow[j])
                if masked[r]:
                    s = jnp.where(causal, s, -1e30)
                mj = jnp.maximum(m_old, jnp.max(s, axis=1, keepdims=True))
                aj = jnp.exp(m_old - mj)
                p = jnp.exp(s - jnp.tile(mj, (1, t // LANES)))
                m_ref[sl, :] = mj
                l_ref[sl, :] = aj * l_ref[sl, :] + jnp.sum(p, axis=1, keepdims=True)
                alpha.append(aj)
                ps.append(p.astype(BF16))
            pv = _dot(_stack(ps), vblk)
            for n, (r, j) in enumerate(slabs):
                sl = slab(rs[r], j)
                acc_ref[sl, :] = acc_ref[sl, :] * alpha[n] + pv[n * t:(n + 1) * t, :]
            return None

        trips = _fox_trips(hp, fl_ref, cap_ref, fq, lambda r, j: m_ref[slab(r, j), :])
        _walk_tiles(i, nr, load, sub, lambda kb, side: None, trips=trips)
        for r, rw in enumerate(rows):
            l0, l1 = l_ref[slab(r, 0), :], l_ref[slab(r, 1), :]
            y_ref[rw, :] = jnp.where(lo, acc_ref[slab(r, 0), :] / l0, acc_ref[slab(r, 1), :] / l1).astype(BF16)
            lse_ref[0, rw, :] = jnp.where(lo, m_ref[slab(r, 0), :] + jnp.log(l0), m_ref[slab(r, 1), :] + jnp.log(l1))

    state = [pltpu.VMEM((2 * nr * t, LANES), F32)] * 3
    qblk, kvfull = _pair_specs(s_len, t * nr)
    return _ride_call(
        body, name, (N_HEADS // 2, nq),
        [qblk, kvfull, kvfull,
         qblk, pl.BlockSpec((N_HEADS, s_len), lambda hp, i: (0, 0)),
         pl.BlockSpec(memory_space=pltpu.SMEM), pl.BlockSpec(memory_space=pltpu.SMEM)],
        [qblk, pl.BlockSpec((1, t * nr, LANES), lambda hp, i: (hp, i, 0))],
        [jax.ShapeDtypeStruct((s_len, ATT_W), BF16), jax.ShapeDtypeStruct((N_HEADS // 2, s_len, LANES), F32)],
        state, ("parallel", "parallel"), (q, k, v, f_wide, f_row, cap, f_row[:, t - 1::t]), ride)


def _fox_bwd(q, k, v, dy, y, lse, f_wide, f_row, cap, name, ride=None):
    s_len = q.shape[0]
    t, nr, nq = _att_tiling(s_len, FOX_ROWS)

    def body(q_ref, k_ref, v_ref, dy_ref, y_ref, lse_ref, f_ref, ft_ref, cap_ref, fl_ref,
             dq_ref, dkt_ref, dvt_ref, dft_ref, dqs_ref, rsum_ref):
        hp = pl.program_id(0)
        i = pl.program_id(1)

        @pl.when(i == 0)
        def _():
            dkt_ref[...] = jnp.zeros_like(dkt_ref)
            dvt_ref[...] = jnp.zeros_like(dvt_ref)
            dft_ref[...] = jnp.zeros_like(dft_ref)

        dqs_ref[...] = jnp.zeros_like(dqs_ref)
        rsum_ref[...] = jnp.zeros_like(rsum_ref)
        slab = lambda r, j: pl.ds((2 * r + j) * t, t)

        lane = lax.broadcasted_iota(jnp.int32, (t, LANES), 1)
        lo = lane < HEAD_DIM
        causal = lax.broadcasted_iota(jnp.int32, (t, t), 0) >= lax.broadcasted_iota(jnp.int32, (t, t), 1)
        rows = [pl.ds(r * t, t) for r in range(nr)]
        qst, dyst, delta, lse, fq = [], [], [], [], []
        for rw in rows:
            qst.append(_stacked_halves(q_ref[rw, :] * jnp.asarray(HEAD_DIM ** -0.5, BF16), lo))
            dyb = dy_ref[rw, :]
            dyst.append(_stacked_halves(dyb, lo))
            prod = dyb.astype(F32) * y_ref[rw, :].astype(F32)
            delta.append([jnp.sum(jnp.where(lo, prod, 0.0), axis=1, keepdims=True),
                          jnp.sum(jnp.where(lo, 0.0, prod), axis=1, keepdims=True)])
            lse_b = lse_ref[0, rw, :]
            lse.append([lse_b[:, 0:1], lse_b[:, HEAD_DIM:HEAD_DIM + 1]])
            fq.append([f_ref[rw, j * HEAD_DIM:j * HEAD_DIM + 1] for j in (0, 1)])

        q_all, dy_all = _stack(qst), _stack(dyst)
        q_all_t, dy_all_t = q_all.T, dy_all.T

        def load(kb):
            k0 = pl.multiple_of(kb * t, t)
            frow = [ft_ref[pl.ds(2 * hp + j, 1), pl.ds(k0, t)] for j in (0, 1)]
            return k_ref[pl.ds(k0, t), :], v_ref[pl.ds(k0, t), :], frow

        def sub(rs, tiles, masked):
            kblk, vblk, frow = tiles
            qs, dys = (q_all, dy_all) if len(rs) == nr else (_stack([qst[r] for r in rs]), _stack([dyst[r] for r in rs]))
            cols = slice(2 * rs[0] * t, 2 * (rs[-1] + 1) * t)
            z = _dot_nt(qs, kblk)
            dp = _dot_nt(dys, vblk)
            slabs = [(r, j) for r in range(len(rs)) for j in (0, 1)]
            pb, dsb, col = [], [], [None, None]
            for n, (r, j) in enumerate(slabs):
                sl = slice(n * t, (n + 1) * t)
                s = z[sl, :] + (fq[rs[r]][j] - frow[j])
                p = jnp.exp(s - lse[rs[r]][j])
                if masked[r]:
                    p = jnp.where(causal, p, 0.0)
                ds = p * (dp[sl, :] - delta[rs[r]][j])
                c = jnp.sum(ds, axis=0, keepdims=True)
                col[j] = c if col[j] is None else col[j] + c
                rsum_ref[slab(rs[r], j), :] += jnp.sum(ds, axis=1, keepdims=True)
                pb.append(p.astype(BF16))
                dsb.append(ds.astype(BF16))
            p_all, ds_all = _stack(pb), _stack(dsb)
            dqs_ref[pl.ds(2 * rs[0] * t, len(slabs) * t), :] += _dot(ds_all, kblk)
            return _dot(q_all_t[:, cols], ds_all), _dot(dy_all_t[:, cols], p_all), col

        def flush(kb, side):
            k0 = pl.multiple_of(kb * t, t)
            dkt_ref[:, pl.ds(k0, t)] += side[0]
            dvt_ref[:, pl.ds(k0, t)] += side[1]
            for j in (0, 1):
                dft_ref[0, pl.ds(j, 1), pl.ds(k0, t)] -= side[2][j]

        trips = _fox_trips(hp, fl_ref, cap_ref, fq, lambda r, j: lse[r][j])
        _walk_tiles(i, nr, load, sub, flush, trips=trips)
        for r, rw in enumerate(rows):
            dq_ref[rw, :] = jnp.where(lo, dqs_ref[slab(r, 0), :], dqs_ref[slab(r, 1), :]) * (HEAD_DIM ** -0.5)
            rs_t = jnp.where(lo, rsum_ref[slab(r, 0), :], rsum_ref[slab(r, 1), :]).T
            q0 = pl.multiple_of((i * nr + r) * t, t)
            for j in (0, 1):
                dft_ref[0, pl.ds(j, 1), pl.ds(q0, t)] += rs_t[j * HEAD_DIM:j * HEAD_DIM + 1, :]

    state = [pltpu.VMEM((2 * nr * t, LANES), F32), pltpu.VMEM((2 * nr * t, 1), F32)]
    qblk, kvfull = _pair_specs(s_len, t * nr)
    kvfull_t = pl.BlockSpec((LANES, s_len), lambda hp, i: (hp, 0))
    return _ride_call(
        body, name, (N_HEADS // 2, nq),
        [qblk, kvfull, kvfull, qblk, qblk,
         pl.BlockSpec((1, t * nr, LANES), lambda hp, i: (hp, i, 0)),
         qblk, pl.BlockSpec((N_HEADS, s_len), lambda hp, i: (0, 0)),
         pl.BlockSpec(memory_space=pltpu.SMEM), pl.BlockSpec(memory_space=pltpu.SMEM)],
        [qblk, kvfull_t, kvfull_t, pl.BlockSpec((1, 8, s_len), lambda hp, i: (hp, 0, 0))],
        [jax.ShapeDtypeStruct((s_len, ATT_W), F32)] + [jax.ShapeDtypeStruct((ATT_W, s_len), F32)] * 2
        + [jax.ShapeDtypeStruct((N_HEADS // 2, 8, s_len), F32)],
        state, ("arbitrary", "arbitrary"), (q, k, v, dy, y, lse, f_wide, f_row, cap, f_row[:, t - 1::t]), ride)


def _sb_more(c_ref, t):
    def more(r):
        return (jnp.max(c_ref[pl.ds(2 * r * t, 2 * t), :]) > -EXP_ZERO).astype(jnp.int32)
    return more


def _stacked_split_dot(slabs, m, parts):
    split = [_split(x, parts) for x in slabs]
    acc = None
    for p in range(parts):
        d = _dot(_stack([s[p] for s in split]), m)
        acc = d if acc is None else acc + d
    return acc


def _sb_weights(z, c, strict, upper, t):
    logs = []
    for n in range(z.shape[0] // t):
        zn = z[n * t:(n + 1) * t, :]
        sp = _softplus_neg_abs(zn)
        l1m = jnp.minimum(-zn, 0.0) - sp
        if strict[n] is not None:
            l1m = jnp.where(strict[n], l1m, 0.0)
        logs.append((jnp.minimum(zn, 0.0) - sp, l1m))
    suf = _stacked_split_dot([l1m for _, l1m in logs], upper, 1)
    out = []
    for n, (logb, l1m) in enumerate(logs):
        after = c[n] + suf[n * t:(n + 1) * t, :]
        a = jnp.exp(logb + after)
        if strict[n] is not None:
            a = jnp.where(strict[n], a, 0.0)
        out.append((logb, a, after[:, 0:1] + l1m[:, 0:1]))
    return out


def _sb_fwd(q, k, v, name, ride=None):
    s_len = q.shape[0]
    t, nr, nq = _att_tiling(s_len, SB_ROWS)

    def body(q_ref, k_ref, v_ref, y_ref, yf_ref, c_ref, acc_ref):
        i = pl.program_id(1)
        lane = lax.broadcasted_iota(jnp.int32, (t, LANES), 1)
        lo = lane < HEAD_DIM
        ri = lax.broadcasted_iota(jnp.int32, (t, t), 0)
        ci = lax.broadcasted_iota(jnp.int32, (t, t), 1)
        strict = ci < ri
        upper = (ri > ci).astype(BF16)
        rows = [pl.ds(r * t, t) for r in range(nr)]
        slab = lambda r, j: pl.ds((2 * r + j) * t, t)
        qst = [_stacked_halves(q_ref[rw, :] * jnp.asarray(HEAD_DIM ** -0.5, BF16), lo) for rw in rows]
        q_all = _stack(qst)
        c_ref[...] = jnp.zeros_like(c_ref)
        acc_ref[...] = jnp.zeros_like(acc_ref)

        def load(kb):
            k0 = pl.multiple_of(kb * t, t)
            return k_ref[pl.ds(k0, t), :], v_ref[pl.ds(k0, t), :]

        def sub(rs, tiles, masked):
            kblk, vblk = tiles
            z = _dot_nt(q_all if len(rs) == nr else _stack([qst[r] for r in rs]), kblk)
            slabs = [(r, j) for r in range(len(rs)) for j in (0, 1)]
            w = _sb_weights(z, [c_ref[slab(rs[r], j), :] for r, j in slabs],
                            [strict if masked[r] else None for r, j in slabs], upper, t)
            for n, (r, j) in enumerate(slabs):
                c_ref[slab(rs[r], j), :] = w[n][2]
            acc_ref[pl.ds(2 * rs[0] * t, len(slabs) * t), :] += _dot(_stack([a.astype(BF16) for _, a, _ in w]), vblk)
            return None

        _walk_tiles(i, nr, load, sub, lambda kb, side: None, more=_sb_more(c_ref, t))
        for r, rw in enumerate(rows):
            y = jnp.where(lo, acc_ref[slab(r, 0), :], acc_ref[slab(r, 1), :])
            y_ref[rw, :] = y.astype(BF16)
            yf_ref[rw, :] = y

    qblk, kvfull = _pair_specs(s_len, t * nr)
    return _ride_call(
        body, name, (N_HEADS // 2, nq), [qblk, kvfull, kvfull], [qblk, qblk],
        [jax.ShapeDtypeStruct((s_len, ATT_W), BF16), jax.ShapeDtypeStruct((s_len, ATT_W), F32)],
        [pltpu.VMEM((2 * nr * t, 1), F32), pltpu.VMEM((2 * nr * t, LANES), F32)], ("parallel", "parallel"),
        (q, k, v), ride)


def _sb_bwd(q, k, v, dy, yf, name):
    s_len = q.shape[0]
    t, nr, nq = _att_tiling(s_len, SB_ROWS)

    def body(q_ref, k_ref, v_ref, dy_ref, yf_ref, dq_ref, dk_ref, dv_ref, c_ref, e_ref, dqs_ref):
        i = pl.program_id(1)

        @pl.when(i == 0)
        def _():
            dk_ref[...] = jnp.zeros_like(dk_ref)
            dv_ref[...] = jnp.zeros_like(dv_ref)

        c_ref[...] = jnp.zeros_like(c_ref)
        e_ref[...] = jnp.zeros_like(e_ref)
        dqs_ref[...] = jnp.zeros_like(dqs_ref)
        slab = lambda r, j: pl.ds((2 * r + j) * t, t)

        lane = lax.broadcasted_iota(jnp.int32, (t, LANES), 1)
        lo = lane < HEAD_DIM
        ri = lax.broadcasted_iota(jnp.int32, (t, t), 0)
        ci = lax.broadcasted_iota(jnp.int32, (t, t), 1)
        strict = ci < ri
        upper = (ri > ci).astype(BF16)
        upper_incl = (ri >= ci).astype(BF16)
        rows = [pl.ds(r * t, t) for r in range(nr)]
        qst, dyst, delta = [], [], []
        for rw in rows:
            qst.append(_stacked_halves(q_ref[rw, :] * jnp.asarray(HEAD_DIM ** -0.5, BF16), lo))
            dyb = dy_ref[rw, :]
            dyst.append(_stacked_halves(dyb, lo))
            prod = dyb.astype(F32) * yf_ref[rw, :]
            delta.append([jnp.sum(jnp.where(lo, prod, 0.0), axis=1, keepdims=True),
                          jnp.sum(jnp.where(lo, 0.0, prod), axis=1, keepdims=True)])
        q_all, dy_all = _stack(qst), _stack(dyst)

        def load(kb):
            k0 = pl.multiple_of(kb * t, t)
            return k_ref[pl.ds(k0, t), :], v_ref[pl.ds(k0, t), :]

        def sub(rs, tiles, masked):
            kblk, vblk = tiles
            qs, dys = (q_all, dy_all) if len(rs) == nr else (_stack([qst[r] for r in rs]), _stack([dyst[r] for r in rs]))
            slabs = [(r, j) for r in range(len(rs)) for j in (0, 1)]
            z = _dot_nt(qs, kblk)
            w = _sb_weights(z, [c_ref[slab(rs[r], j), :] for r, j in slabs],
                            [strict if masked[r] else None for r, j in slabs], upper, t)
            da = _dot_nt(dys, vblk)
            ab = [a.astype(BF16) for _, a, _ in w]
            dl = [ab[n].astype(F32) * da[n * t:(n + 1) * t, :] for n in range(len(slabs))]
            tail = _stacked_split_dot(dl, upper_incl, 2)
            dzb = []
            for n, (r, j) in enumerate(slabs):
                sl = slab(rs[r], j)
                tl = tail[n * t:(n + 1) * t, :]
                e = e_ref[sl, :]
                dl1m = (delta[rs[r]][j] - e) - tl
                e_ref[sl, :] = e + tl[:, 0:1]
                c_ref[sl, :] = w[n][2]
                dz = dl[n] - jnp.exp(w[n][0]) * (dl[n] + dl1m)
                if masked[r]:
                    dz = jnp.where(strict, dz, 0.0)
                dzb.append(dz.astype(BF16))
            a_all, dz_all = _stack(ab), _stack(dzb)
            dqs_ref[pl.ds(2 * rs[0] * t, len(slabs) * t), :] += _dot(dz_all, kblk)
            return _dot_tn(dz_all, qs), _dot_tn(a_all, dys)

        def flush(kb, side):
            k0 = pl.multiple_of(kb * t, t)
            dk_ref[pl.ds(k0, t), :] += side[0]
            dv_ref[pl.ds(k0, t), :] += side[1]

        _walk_tiles(i, nr, load, sub, flush, more=_sb_more(c_ref, t))
        for r, rw in enumerate(rows):
            dq_ref[rw, :] = jnp.where(lo, dqs_ref[slab(r, 0), :], dqs_ref[slab(r, 1), :]) * (HEAD_DIM ** -0.5)

    qblk, kvfull = _pair_specs(s_len, t * nr)
    return pl.pallas_call(
        body, name=name, grid=(N_HEADS // 2, nq),
        in_specs=[qblk, kvfull, kvfull, qblk, qblk],
        out_specs=[qblk, kvfull, kvfull],
        out_shape=[jax.ShapeDtypeStruct((s_len, ATT_W), F32)] * 3,
        scratch_shapes=[pltpu.VMEM((2 * nr * t, 1), F32), pltpu.VMEM((2 * nr * t, 1), F32),
                        pltpu.VMEM((2 * nr * t, LANES), F32)],
        compiler_params=_cparams("arbitrary", "arbitrary"),
    )(q, k, v, dy, yf)


def _merge_fwd(x, yf, ys, gf, gs, wbf, wbs, wo, name):
    s_len = x.shape[0]
    ts = min(512, s_len)

    def body(x_ref, yf_ref, ys_ref, gf_ref, gs_ref, wbf_ref, wbs_ref, wo_ref, o_ref):
        merged = (_sigmoid(gf_ref[...]) * _dot_nt(yf_ref[...], wbf_ref[...])
                  + _sigmoid(gs_ref[...]) * _dot_nt(ys_ref[...], wbs_ref[...]))
        o_ref[...] = x_ref[...] + _dot(merged.astype(BF16), wo_ref[...])

    tok = lambda w: pl.BlockSpec((ts, w), lambda i: (i, 0))
    full = lambda a: pl.BlockSpec(a.shape, lambda i: (0, 0))
    return pl.pallas_call(
        body, name=name, grid=(s_len // ts,),
        in_specs=[tok(D_MODEL), tok(ATT_W), tok(ATT_W), tok(D_MODEL), tok(D_MODEL), full(wbf), full(wbs), full(wo)],
        out_specs=tok(D_MODEL),
        out_shape=jax.ShapeDtypeStruct((s_len, D_MODEL), F32),
        compiler_params=_cparams("parallel"),
    )(x, yf, ys, gf, gs, wbf, wbs, wo)


def _merge_bwd(dx, yf, ys, gf, gs, wbf, wbs, wo, name):
    s_len = dx.shape[0]
    ts = min(512, s_len)

    def body(dx_ref, yf_ref, ys_ref, gf_ref, gs_ref, wbf_ref, wbs_ref, wo_ref,
             dyf_ref, dys_ref, dgf_ref, dgs_ref, dbf_ref, dbs_ref, mg_ref):
        bf = _dot_nt(yf_ref[...], wbf_ref[...])
        bs = _dot_nt(ys_ref[...], wbs_ref[...])
        sf = _sigmoid(gf_ref[...])
        ss = _sigmoid(gs_ref[...])
        mg_ref[...] = (sf * bf + ss * bs).astype(BF16)
        dm = _dot_nt(dx_ref[...].astype(BF16), wo_ref[...])
        dbf = (dm * sf).astype(BF16)
        dbs = (dm * ss).astype(BF16)
        dbf_ref[...] = dbf
        dbs_ref[...] = dbs
        dgf_ref[...] = (dm * bf * (sf * (1.0 - sf))).astype(BF16)
        dgs_ref[...] = (dm * bs * (ss * (1.0 - ss))).astype(BF16)
        dyf_ref[...] = _dot(dbf, wbf_ref[...]).astype(BF16)
        dys_ref[...] = _dot(dbs, wbs_ref[...]).astype(BF16)

    tok = lambda w: pl.BlockSpec((ts, w), lambda i: (i, 0))
    full = lambda a: pl.BlockSpec(a.shape, lambda i: (0, 0))
    b16o = lambda w: jax.ShapeDtypeStruct((s_len, w), BF16)
    return pl.pallas_call(
        body, name=name, grid=(s_len // ts,),
        in_specs=[tok(D_MODEL), tok(ATT_W), tok(ATT_W), tok(D_MODEL), tok(D_MODEL), full(wbf), full(wbs), full(wo)],
        out_specs=[tok(ATT_W), tok(ATT_W)] + [tok(D_MODEL)] * 5,
        out_shape=[b16o(ATT_W), b16o(ATT_W)] + [b16o(D_MODEL)] * 5,
        compiler_params=_cparams("parallel"),
    )(dx, yf, ys, gf, gs, wbf, wbs, wo)


def _mix_bwd(x, dx_in, gain, w_in, fqr, fkr, dfqn, dfkn_t, qg, kg, dfv_t, df_col, logf, dsq, dsk, dsv, dgf, dgs, name):
    s_len = x.shape[0]
    ts = min(256, s_len)
    nt = s_len // ts
    gmat = _head_group_matrix()

    def body(x_ref, dxi_ref, gain_ref, w_ref, fqr_ref, fkr_ref, dfqn_ref, dfknt_ref, qg_ref, kg_ref, gm_ref,
             dfvt_ref, df_ref, logf_ref, dsq_ref, dsk_ref, dsv_ref, dgf_ref, dgs_ref,
             dp_ref, dx_ref, dgain_ref, dqg_ref, dkg_ref, dbias_ref, carry):
        i = pl.program_id(0)

        @pl.when(i == 0)
        def _():
            carry[...] = jnp.zeros_like(carry)
            dgain_ref[...] = jnp.zeros_like(dgain_ref)
            dqg_ref[...] = jnp.zeros_like(dqg_ref)
            dkg_ref[...] = jnp.zeros_like(dkg_ref)
            dbias_ref[...] = jnp.zeros_like(dbias_ref)

        gm = gm_ref[...]

        def headnorm_bwd(raw, dout, g, dg_ref):
            ms = _dot_split(raw * raw, gm, HEAD_SUM_PARTS) * (1.0 / HEAD_DIM)
            r = lax.rsqrt(ms + EPS)
            nrm = raw * r
            dg_ref[...] += jnp.sum(dout * nrm, axis=0, keepdims=True)
            dn = dout * g
            mean_h = _dot_split(dn * nrm, gm, HEAD_SUM_PARTS) * (1.0 / HEAD_DIM)
            return r * (dn - nrm * mean_h)

        dp_ref[:, C_FQ:C_FQ + ATT_W] = headnorm_bwd(fqr_ref[...], dfqn_ref[...], qg_ref[...], dqg_ref).astype(BF16)
        dp_ref[:, C_FK:C_FK + ATT_W] = headnorm_bwd(fkr_ref[...], dfknt_ref[...].T, kg_ref[...], dkg_ref).astype(BF16)
        dp_ref[:, C_FV:C_FV + ATT_W] = dfvt_ref[...].T.astype(BF16)
        dp_ref[:, C_SQ:C_SQ + ATT_W] = dsq_ref[...].astype(BF16)
        dp_ref[:, C_SK:C_SK + ATT_W] = dsk_ref[...].astype(BF16)
        dp_ref[:, C_SV:C_SV + ATT_W] = dsv_ref[...].astype(BF16)
        dp_ref[:, C_GF:C_GF + D_MODEL] = dgf_ref[...]
        dp_ref[:, C_GS:C_GS + D_MODEL] = dgs_ref[...]

        r_ = lax.broadcasted_iota(jnp.int32, (ts, ts), 0)
        c_ = lax.broadcasted_iota(jnp.int32, (ts, ts), 1)
        rev = (c_ >= r_).astype(BF16)
        dlogf = _dot_split_left(rev, df_ref[...], 3) + carry[...]
        carry[...] = dlogf[0:1, :]
        lane = lax.broadcasted_iota(jnp.int32, (ts, LANES), 1)
        dfl = jnp.where(lane < N_HEADS, dlogf * (1.0 - jnp.exp(logf_ref[...])), 0.0)
        dbias_ref[...] += jnp.sum(dfl, axis=0, keepdims=True)
        dp_ref[:, C_FL:C_FL + LANES] = dfl.astype(BF16)
        dp_ref[:, C_FL + LANES:C_SQ] = jnp.zeros((ts, C_SQ - C_FL - LANES), BF16)

        dh = _dot(dp_ref[...], w_ref[...])
        xf = x_ref[...]
        r = _rms_rinv(xf)
        xhat = xf * r
        dgain_ref[...] += jnp.sum(dh * xhat, axis=0, keepdims=True)
        dn = dh * gain_ref[...]
        dx_ref[...] = dxi_ref[...] + r * (dn - xhat * jnp.mean(dn * xhat, axis=-1, keepdims=True))

    tok = lambda w: pl.BlockSpec((ts, w), lambda i: (nt - 1 - i, 0))
    full = lambda a: pl.BlockSpec(a.shape, lambda i: (0, 0))
    row = lambda w: pl.BlockSpec((1, w), lambda i: (0, 0))
    tok_t = pl.BlockSpec((ATT_W, ts), lambda i: (0, nt - 1 - i))
    return _ride_call(
        body, name, (nt,),
        [tok(D_MODEL), tok(D_MODEL), full(gain), full(w_in), tok(ATT_W), tok(ATT_W), tok(ATT_W), tok_t,
         full(qg), full(kg), full(gmat), tok_t, tok(LANES), tok(LANES), tok(ATT_W), tok(ATT_W), tok(ATT_W),
         tok(D_MODEL), tok(D_MODEL)],
        [tok(IN_PAD), tok(D_MODEL), row(D_MODEL), row(ATT_W), row(ATT_W), row(LANES)],
        [jax.ShapeDtypeStruct((s_len, IN_PAD), BF16), jax.ShapeDtypeStruct((s_len, D_MODEL), F32),
         jax.ShapeDtypeStruct((1, D_MODEL), F32), jax.ShapeDtypeStruct((1, ATT_W), F32),
         jax.ShapeDtypeStruct((1, ATT_W), F32), jax.ShapeDtypeStruct((1, LANES), F32)],
        [pltpu.VMEM((1, LANES), F32)], ("arbitrary",),
        (x, dx_in, gain, w_in, fqr, fkr, dfqn, dfkn_t, qg, kg, gmat, dfv_t, df_col, logf, dsq, dsk, dsv, dgf, dgs), None)


def _ple_loss(x, p, tgt, gain, wpg, wpp, name):
    s_len = x.shape[0]
    ts = min(512, s_len)

    def body(x_ref, p_ref, t_ref, gain_ref, wpg_ref, wpp_ref, dx_ref, n_ref, ds_ref, dpp_ref, dgain_ref, loss_ref):
        i = pl.program_id(0)

        @pl.when(i == 0)
        def _():
            dgain_ref[...] = jnp.zeros_like(dgain_ref)
            loss_ref[...] = jnp.zeros_like(loss_ref)

        xf = x_ref[...]
        r = _rms_rinv(xf)
        n = xf * r
        hn = (n * gain_ref[...]).astype(BF16)
        n_ref[...] = hn
        sg = _sigmoid(_dot(hn, wpg_ref[...]))
        pp = _dot_nt(p_ref[...].astype(BF16), wpp_ref[...])
        err = (xf + sg * pp) - t_ref[...]
        sq = jnp.sum(jnp.sum(err * err, axis=1, keepdims=True), axis=0, keepdims=True)
        loss_ref[...] += (0.5 / D_MODEL) * sq
        dout = err * (1.0 / D_MODEL)
        dpp_ref[...] = (dout * sg).astype(BF16)
        ds = (dout * pp * (sg * (1.0 - sg))).astype(BF16)
        ds_ref[...] = ds
        dhn = _dot_nt(ds, wpg_ref[...])
        dgain_ref[...] += jnp.sum(dhn * n, axis=0, keepdims=True)
        dn = dhn * gain_ref[...]
        dx_ref[...] = dout + r * (dn - n * jnp.mean(dn * n, axis=-1, keepdims=True))

    tok = lambda w: pl.BlockSpec((ts, w), lambda i: (i, 0))
    full = lambda a: pl.BlockSpec(a.shape, lambda i: (0, 0))
    return pl.pallas_call(
        body, name=name, grid=(s_len // ts,),
        in_specs=[tok(D_MODEL), tok(PLE_DIM), tok(D_MODEL), full(gain), full(wpg), full(wpp)],
        out_specs=[tok(D_MODEL), tok(D_MODEL), tok(D_MODEL), tok(D_MODEL),
                   pl.BlockSpec((1, D_MODEL), lambda i: (0, 0)), pl.BlockSpec((8, LANES), lambda i: (0, 0))],
        out_shape=[jax.ShapeDtypeStruct((s_len, D_MODEL), F32), jax.ShapeDtypeStruct((s_len, D_MODEL), BF16),
                   jax.ShapeDtypeStruct((s_len, D_MODEL), BF16), jax.ShapeDtypeStruct((s_len, D_MODEL), BF16),
                   jax.ShapeDtypeStruct((1, D_MODEL), F32), jax.ShapeDtypeStruct((8, LANES), F32)],
        compiler_params=_cparams("arbitrary"),
    )(x, p, tgt, gain, wpg, wpp)


def _exchange(x, name, broadcast):
    def body(x_ref, out_ref, send_sems, recv_sems, local_sem):
        _exchange_start(x_ref, out_ref, send_sems, recv_sems, local_sem, broadcast)
        _exchange_wait(x_ref, out_ref, send_sems, recv_sems, local_sem, broadcast)

    return pl.pallas_call(
        body, name=name,
        in_specs=[EXCHANGE_SPEC],
        out_specs=EXCHANGE_SPEC,
        out_shape=_exchange_shape(x, broadcast),
        scratch_shapes=list(EXCHANGE_SEMS),
        compiler_params=pltpu.CompilerParams(has_side_effects=True),
    )(x)


def _gather_two_level(x, name):
    def body(x_ref, out_ref, send_sems, recv_sems, local_sem):
        mx, my, mc = lax.axis_index("x"), lax.axis_index("y"), lax.axis_index("c")
        me, sibling = (mx, my, mc), (mx, my, 1 - mc)
        chips = [(1 - mx, my), (mx, 1 - my), (1 - mx, 1 - my)]

        def slot(px, py, pc):
            return out_ref.at[4 * px + 2 * py + pc]

        def copy(k, block, to, src=None):
            return pltpu.make_async_remote_copy(
                src_ref=slot(*block) if src is None else src, dst_ref=slot(*block),
                send_sem=send_sems.at[k], recv_sem=recv_sems.at[k], device_id=to, device_id_type=MESH)

        mine = pltpu.make_async_copy(x_ref, slot(*me), local_sem)
        mine.start()
        first = [copy(0, me, sibling, src=x_ref)]
        first += [copy(1 + j, me, (*chip, mc), src=x_ref) for j, chip in enumerate(chips)]
        for cp in first:
            cp.start()
        passed = [copy(4 + j, (*chip, mc), sibling) for j, chip in enumerate(chips)]
        for j, chip in enumerate(chips):
            copy(1 + j, (*chip, mc), me).wait_recv()
            passed[j].start()
        copy(0, sibling, me).wait_recv()
        for j, chip in enumerate(chips):
            copy(4 + j, (*chip, 1 - mc), me).wait_recv()
        for cp in first + passed:
            cp.wait_send()
        mine.wait()

    return pl.pallas_call(
        body, name=name,
        in_specs=[EXCHANGE_SPEC],
        out_specs=EXCHANGE_SPEC,
        out_shape=_exchange_shape(x, True),
        scratch_shapes=list(EXCHANGE_SEMS),
        compiler_params=pltpu.CompilerParams(has_side_effects=True),
    )(x)


EXCHANGE_SPEC = pl.BlockSpec(memory_space=pl.ANY)
EXCHANGE_SEMS = (pltpu.SemaphoreType.DMA((N_DEV - 1,)), pltpu.SemaphoreType.DMA((N_DEV - 1,)), pltpu.SemaphoreType.DMA)


def _exchange_shape(x, broadcast):
    return jax.ShapeDtypeStruct((N_DEV,) + tuple(x.shape if broadcast else x.shape[1:]), x.dtype)


def _exchange_copies(x_ref, out_ref, send_sems, recv_sems, local_sem, broadcast, with_recv=True):
    mx, my, mc = lax.axis_index("x"), lax.axis_index("y"), lax.axis_index("c")
    me = 4 * mx + 2 * my + mc

    def src(idx):
        return x_ref if broadcast else x_ref.at[idx]

    local = pltpu.make_async_copy(src(me), out_ref.at[me], local_sem)
    pairs = []
    for k in range(1, N_DEV):
        px = (1 - mx) if k & 4 else mx
        py = (1 - my) if k & 2 else my
        pc = (1 - mc) if k & 1 else mc
        peer = 4 * px + 2 * py + pc
        sems = dict(send_sem=send_sems.at[k - 1], recv_sem=recv_sems.at[k - 1], device_id=(px, py, pc), device_id_type=MESH)
        recv = pltpu.make_async_remote_copy(src_ref=src(peer), dst_ref=out_ref.at[peer], **sems) if with_recv else None
        pairs.append((pltpu.make_async_remote_copy(src_ref=src(peer), dst_ref=out_ref.at[me], **sems), recv))
    return local, pairs


def _exchange_start(*refs_and_mode):
    local, pairs = _exchange_copies(*refs_and_mode, with_recv=False)
    local.start()
    for send, _ in pairs:
        send.start()


def _exchange_wait(*refs_and_mode):
    local, pairs = _exchange_copies(*refs_and_mode)
    for _, recv in pairs:
        recv.wait_recv()
    for send, _ in pairs:
        send.wait_send()
    local.wait()


def _riding(body, grid, n_in, n_out, ride):
    if ride is None:
        return body
    broadcast = ride[1]

    def wrapped(*refs):
        ins, x_ref = refs[:n_in], refs[n_in]
        outs, out_ref = refs[n_in + 1:n_in + 1 + n_out], refs[n_in + 1 + n_out]
        scratch, sems = refs[n_in + 2 + n_out:-3], refs[-3:]
        step = pl.program_id(0)
        for d in range(1, len(grid)):
            step = step * grid[d] + pl.program_id(d)
        total = 1
        for g in grid:
            total *= g

        @pl.when(step == 0)
        def _():
            _exchange_start(x_ref, out_ref, *sems, broadcast)

        body(*ins, *outs, *scratch)

        @pl.when(step == total - 1)
        def _():
            _exchange_wait(x_ref, out_ref, *sems, broadcast)

    return wrapped


def _ride_call(body, name, grid, in_specs, out_specs, out_shape, scratch_shapes, sem, operands, ride):
    if ride is None:
        return pl.pallas_call(body, name=name, grid=grid, in_specs=in_specs, out_specs=out_specs, out_shape=out_shape,
                              scratch_shapes=scratch_shapes, compiler_params=_cparams(*sem))(*operands)
    return pl.pallas_call(
        _riding(body, grid, len(in_specs), len(out_specs), ride), name=name, grid=grid,
        in_specs=list(in_specs) + [EXCHANGE_SPEC], out_specs=list(out_specs) + [EXCHANGE_SPEC],
        out_shape=list(out_shape) + [_exchange_shape(*ride)],
        scratch_shapes=list(scratch_shapes) + list(EXCHANGE_SEMS),
        compiler_params=_cparams(*(["arbitrary"] * len(grid))),
    )(*operands, ride[0])


def _adamw_math(w, g, m, v):
    m2 = ADAM_B1 * m + (1.0 - ADAM_B1) * g
    v2 = ADAM_B2 * v + (1.0 - ADAM_B2) * (g * g)
    m_hat = m2 / (1.0 - ADAM_B1 ** ADAM_STEP)
    v_hat = v2 / (1.0 - ADAM_B2 ** ADAM_STEP)
    delta = -ADAM_LR * (m_hat / (jnp.sqrt(v_hat) + ADAM_EPS) + ADAM_WD * w)
    return delta, m2, v2


def _sum_parts(parts, name, tr):
    _, rows, cols = parts.shape

    def body(p_ref, g_ref):
        g = p_ref[0].astype(F32)
        for s in range(1, N_DEV):
            g = g + p_ref[s].astype(F32)
        g_ref[...] = g

    return pl.pallas_call(
        body, name=name, grid=(rows // tr,),
        in_specs=[pl.BlockSpec((N_DEV, tr, cols), lambda i: (0, i, 0))],
        out_specs=pl.BlockSpec((tr, cols), lambda i: (i, 0)),
        out_shape=jax.ShapeDtypeStruct((rows, cols), F32),
        compiler_params=_cparams("parallel"),
    )(parts)


ADAM_SPLIT_ELEMS = 400_000


def _adamw_shard(g, w, m, v, name):
    rows, cols = w.shape
    tr = rows // 2 if rows * cols > ADAM_SPLIT_ELEMS else rows

    def body(g_ref, w_ref, m_ref, v_ref, d_ref, m2_ref, v2_ref):
        d_ref[...], m2_ref[...], v2_ref[...] = _adamw_math(w_ref[...], g_ref[...], m_ref[...], v_ref[...])

    blk = pl.BlockSpec((tr, cols), lambda i: (i, 0))
    return pl.pallas_call(
        body, name=name, grid=(rows // tr,),
        in_specs=[blk] * 4, out_specs=[blk] * 3,
        out_shape=[jax.ShapeDtypeStruct((rows, cols), F32)] * 3,
        compiler_params=_cparams("parallel"),
    )(g, w, m, v)


def _adamw_small(parts, w, m, v, name):
    names = list(SMALL_NAMES)

    def body(p_ref, *refs):
        ins, outs = refs[:3 * len(names)], refs[3 * len(names):]
        total = p_ref[0]
        for s in range(1, N_DEV):
            total = total + p_ref[s]
        for i, n in enumerate(names):
            row, off, width = SMALL_POS[n]
            g = total[row:row + 1, off:off + width]
            w_ref, m_ref, v_ref = ins[3 * i:3 * i + 3]
            g_ref, d_ref, m2_ref, v2_ref = outs[4 * i:4 * i + 4]
            g_ref[...] = g
            d_ref[...], m2_ref[...], v2_ref[...] = _adamw_math(w_ref[...], g, m_ref[...], v_ref[...])
        row, off, _ = SMALL_POS["loss"]
        outs[-1][...] = total[row:row + 1, off:off + 1]

    operands = [parts] + [t[n] for n in names for t in (w, m, v)]
    shapes = [jax.ShapeDtypeStruct(w[n].shape, F32) for n in names for _ in range(4)]
    shapes.append(jax.ShapeDtypeStruct((1, 1), F32))
    out = pl.pallas_call(body, name=name, out_shape=shapes)(*operands)
    return {n: tuple(out[4 * i:4 * i + 4]) for i, n in enumerate(names)}, out[-1]


TRANSPOSED = frozenset(("ffn1_w_gate", "ffn1_w_up", "w_in", "w_branch_fox", "w_branch_sb", "ffn2_w_gate", "ffn2_w_up",
                        "w_ple_proj"))
F_PAD_ROWS = C_SQ - FL_REAL_END


def _pack(pieces, group, dtype):
    out = []
    for name in GATHER_GROUPS[group]:
        r = pieces[name].T if name in TRANSPOSED else pieces[name]
        r = r.reshape(-1, D_MODEL).astype(dtype)
        if r.shape[0] != PACK_ROWS[name]:
            r = jnp.pad(r, ((0, PACK_ROWS[name] - r.shape[0]), (0, 0)))
        out.append(r)
    return jnp.concatenate(out, axis=0)


def _real_rows(name):
    return W_IN_ROWS if name == "w_in" else PACK_ROWS[name]


def _gathered(got, name, shape):
    off = GATHER_OFF[name]
    return got[:, off:off + _real_rows(name), :].reshape(shape)


def _w_in_device_rows(d):
    lo, hi = d * W_IN_ROWS, (d + 1) * W_IN_ROWS
    if hi <= FL_REAL_END:
        return [(lo, hi)]
    if lo >= FL_REAL_END:
        return [(lo + F_PAD_ROWS, hi + F_PAD_ROWS)]
    return [(lo, FL_REAL_END), (C_SQ, hi + F_PAD_ROWS)]


def _w_in_t_padded(got):
    t = _gathered(got, "w_in", (IN_REAL, D_MODEL))
    return jnp.concatenate([t[:FL_REAL_END], jnp.zeros((F_PAD_ROWS, D_MODEL), t.dtype), t[FL_REAL_END:]], axis=0)


def _pack_chunks(grads, group):
    out = []
    for name in SCATTER_GROUPS[group]:
        base, _, half = name.partition("#")
        g = grads[base].astype(BF16)
        if base == "w_in":
            lo, hi = W_IN_HALVES[int(half)]
            tail = jnp.zeros((PACK_ROWS[base] - W_IN_ROWS, D_MODEL), BF16)
            c = jnp.stack([jnp.concatenate([g[a:b] for a, b in _w_in_device_rows(d)] + [tail], axis=0)[lo:hi]
                           for d in range(N_DEV)])
        else:
            c = g.reshape(N_DEV, PACK_ROWS[name], D_MODEL)
        out.append(c)
    return out[0] if len(out) == 1 else jnp.concatenate(out, axis=1)


def _shard_grad(summed, name, shape):
    if name == "w_in":
        rows = jnp.concatenate([summed[f"w_in#{i}"] for i in range(len(W_IN_HALVES))], axis=0)[:W_IN_ROWS]
    else:
        rows = summed[name][SCATTER_OFF[name]:SCATTER_OFF[name] + PACK_ROWS[name], :]
    return rows.reshape(shape[1], shape[0]).T if name in TRANSPOSED else rows.reshape(shape)


WEIGHT_NAMES = ['ffn1_norm', 'ffn1_w_gate', 'ffn1_w_up', 'ffn1_w_down', 'mix_norm', 'w_in', 'forget_bias', 'q_norm',
                'k_norm', 'w_branch_fox', 'w_branch_sb', 'w_out', 'ffn2_norm', 'ffn2_w_gate', 'ffn2_w_up',
                'ffn2_w_down', 'ple_norm', 'w_ple_gate', 'w_ple_proj']
SMALL_NAMES = ('ffn1_norm', 'mix_norm', 'ffn2_norm', 'ple_norm', 'q_norm', 'k_norm', 'forget_bias')
SMALL_POS = {'ffn1_norm': (0, 0, D_MODEL), 'mix_norm': (1, 0, D_MODEL), 'ffn2_norm': (2, 0, D_MODEL),
             'ple_norm': (3, 0, D_MODEL), 'q_norm': (4, 0, HEAD_DIM), 'k_norm': (4, HEAD_DIM, HEAD_DIM),
             'forget_bias': (4, 2 * HEAD_DIM, N_HEADS), 'loss': (4, 2 * HEAD_DIM + N_HEADS, 1)}


def _pack_small(vals, loss):
    tail = [vals[n].reshape(1, -1) for n in ('q_norm', 'k_norm', 'forget_bias')] + [loss.reshape(1, 1)]
    tail.append(jnp.zeros((1, D_MODEL - sum(t.shape[1] for t in tail)), F32))
    rows = [vals[n].reshape(1, D_MODEL) for n in SMALL_NAMES[:4]] + [jnp.concatenate(tail, axis=1)]
    rows.append(jnp.zeros((SMALL_ROWS - len(rows), D_MODEL), F32))
    return jnp.concatenate(rows, axis=0)


def _step(x, p, tgt, w):
    row = lambda a: a.reshape(1, -1).astype(F32)
    g_ffn1, g_mix, g_ffn2, g_ple = (row(w[n]) for n in SMALL_NAMES[:4])
    qg = jnp.tile(row(w['q_norm']), (1, N_HEADS))
    kg = jnp.tile(row(w['k_norm']), (1, N_HEADS))
    bias = jnp.pad(row(w['forget_bias']), ((0, 0), (0, LANES - N_HEADS)))
    half = D_FF // 2
    grads = {}

    blk = lambda n: GATHER_OFF[n] // FFN_SHARD
    ffn1 = tuple(blk(n) for n in ("ffn1_w_gate", "ffn1_w_up", "ffn1_w_down"))
    ffn2 = tuple(blk(n) for n in ("ffn2_w_gate", "ffn2_w_up", "ffn2_w_down"))
    got0 = _gather_two_level(_pack(w, 0, BF16), "gather_ffn1")
    x1, g1, u1, h1, got1 = _ffn_fwd(x, g_ffn1, (got0,) * 3, ffn1, "ffn1_fwd", ride=(_pack(w, 1, BF16), True))
    w_in = _w_in_t_padded(got1)
    wbf = _gathered(got1, "w_branch_fox", (D_MODEL, ATT_W))
    wbs = _gathered(got1, "w_branch_sb", (D_MODEL, ATT_W))
    wo = _gathered(got1, "w_out", (D_MODEL, D_MODEL))
    (hmix, fqr, fkr, fqn, fkn, fv, logf, f_col, f_row, sq, sk, sv, gf, gs) = _mix_fwd(
        x1, g_mix, w_in, bias, qg, kg, "mix_fwd")
    f_wide = jnp.repeat(f_col[:, :N_HEADS], HEAD_DIM, axis=1)
    cap = _fox_qk_cap(w['q_norm'], w['k_norm'])
    y_fox, lse, got2 = _fox_fwd(fqn, fkn, fv, f_wide, f_row, cap, "fox_fwd", ride=(_pack(w, 2, BF16), True))
    y_sb, y_sb32, got3 = _sb_fwd(sq, sk, sv, "sb_fwd", ride=(_pack(w, 3, BF16), True))
    wpg = _gathered(got3, "w_ple_gate", (D_MODEL, D_MODEL))
    wpp = _gathered(got3, "w_ple_proj", (D_MODEL, PLE_DIM))
    ffn2_bufs = (got2, got2, got3)
    x2 = _merge_fwd(x1, y_fox, y_sb, gf, gs, wbf, wbs, wo, "merge_fwd")
    x3, g2, u2, h2, = _ffn_fwd(x2, g_ffn2, ffn2_bufs, ffn2, "ffn2_fwd")
    dx3, n_ple, ds_ple, dpp, dg_ple, loss = _ple_loss(x3, p, tgt, g_ple, wpg, wpp, "ple_loss")

    grads['w_ple_gate'] = _wgrad(n_ple, ds_ple, "dw_ple_gate", D_MODEL, D_MODEL)
    grads['w_ple_proj'] = _wgrad(dpp, p, "dw_ple_proj", D_MODEL, PLE_DIM)
    dg2, du2, act2, dx2, dg_ffn2 = _ffn_bwd_fused(x2, dx3, g_ffn2, g2, u2, ffn2_bufs, ffn2, "ffn2_bwd")
    grads['ffn2_w_gate'] = _wgrad(dg2, h2, "dw_ffn2_gate", half, D_MODEL)
    grads['ffn2_w_up'] = _wgrad(du2, h2, "dw_ffn2_up", half, D_MODEL)
    grads['ffn2_w_down'] = _wgrad(act2, dx3, "dw_ffn2_down", half, D_MODEL)
    dyf, dys, dgf, dgs, dbf, dbs, merged = _merge_bwd(dx2, y_fox, y_sb, gf, gs, wbf, wbs, wo, "merge_bwd")
    grads['w_branch_fox'] = _wgrad(dbf, y_fox, "dw_branch_fox", D_MODEL, ATT_W)
    grads['w_branch_sb'] = _wgrad(dbs, y_sb, "dw_branch_sb", D_MODEL, ATT_W)
    grads['w_out'] = _wgrad(merged, dx2, "dw_out", D_MODEL, D_MODEL)
    dfqn, dfkn_t, dfv_t, dft, part_rest = _fox_bwd(fqn, fkn, fv, dyf, y_fox, lse, f_wide, f_row, cap, "fox_bwd",
                                               ride=(_pack_chunks(grads, 5), False))
    dsq, dsk, dsv = _sb_bwd(sq, sk, sv, dys, y_sb32, "sb_bwd")
    s_len = x.shape[0]
    df_col = jnp.pad(dft[:, :2, :].reshape(N_HEADS, s_len).T, ((0, 0), (0, LANES - N_HEADS)))
    dproj, dx1, dg_mix, dqg, dkg, dbias = _mix_bwd(
        x1, dx2, g_mix, w_in, fqr, fkr, dfqn, dfkn_t, qg, kg, dfv_t, df_col, logf, dsq, dsk, dsv, dgf, dgs, "mix_bwd")
    grads['w_in'] = _wgrad(dproj, hmix, "dw_in", IN_PAD // 3, D_MODEL)
    dg1, du1, act1, dx0, dg_ffn1, part_in0 = _ffn_bwd_fused(x, dx1, g_ffn1, g1, u1, (got0,) * 3, ffn1, "ffn1_bwd",
                                                            ride=(_pack_chunks(grads, 3), False))
    grads['ffn1_w_gate'], part_in1 = _wgrad(dg1, h1, "dw_ffn1_gate", half, D_MODEL,
                                            ride=(_pack_chunks(grads, 4), False))
    grads['ffn1_w_up'], part_gate = _wgrad(du1, h1, "dw_ffn1_up", half, D_MODEL, ride=(_pack_chunks(grads, 0), False))
    grads['ffn1_w_down'], part_up = _wgrad(act1, dx1, "dw_ffn1_down", half, D_MODEL,
                                           ride=(_pack_chunks(grads, 1), False))
    part_down = _exchange(_pack_chunks(grads, 2), "scatter_ffn1_down", False)

    fold = lambda a: a.reshape(N_HEADS, HEAD_DIM).sum(axis=0).reshape(1, HEAD_DIM)
    small_g = {'ffn1_norm': dg_ffn1, 'mix_norm': dg_mix, 'ffn2_norm': dg_ffn2, 'ple_norm': dg_ple,
               'q_norm': fold(dqg), 'k_norm': fold(dkg), 'forget_bias': dbias[:, :N_HEADS]}
    return loss[0, 0], dx0, (part_gate, part_up, part_down, part_in0, part_in1, part_rest), small_g


def kernel(x, p, ffn1_norm, ffn1_w_gate, ffn1_w_up, ffn1_w_down, mix_norm, w_in, forget_bias, q_norm, k_norm, w_branch_fox, w_branch_sb, w_out, ffn2_norm, ffn2_w_gate, ffn2_w_up, ffn2_w_down, ple_norm, w_ple_gate, w_ple_proj, loss_target, m_ffn1_norm, m_ffn1_w_gate, m_ffn1_w_up, m_ffn1_w_down, m_mix_norm, m_w_in, m_forget_bias, m_q_norm, m_k_norm, m_w_branch_fox, m_w_branch_sb, m_w_out, m_ffn2_norm, m_ffn2_w_gate, m_ffn2_w_up, m_ffn2_w_down, m_ple_norm, m_w_ple_gate, m_w_ple_proj, v_ffn1_norm, v_ffn1_w_gate, v_ffn1_w_up, v_ffn1_w_down, v_mix_norm, v_w_in, v_forget_bias, v_q_norm, v_k_norm, v_w_branch_fox, v_w_branch_sb, v_w_out, v_ffn2_norm, v_ffn2_w_gate, v_ffn2_w_up, v_ffn2_w_down, v_ple_norm, v_w_ple_gate, v_w_ple_proj):
    args = dict(locals())
    w = {n: args[n][0] for n in WEIGHT_NAMES}
    m = {n: args["m_" + n][0] for n in WEIGHT_NAMES}
    v = {n: args["v_" + n][0] for n in WEIGHT_NAMES}
    loss, dx, parts, small_g = _step(x[0], p[0, 0], loss_target[0], w)

    summed = {}
    for grp, part in enumerate(parts):
        s = _sum_parts(part, f"sum_grads_{grp}", SUM_TILE_ROWS[grp])
        summed.update({n: s for n in SCATTER_GROUPS[grp]})
    big = {}
    for n in WEIGHT_NAMES:
        if n not in SMALL_NAMES:
            g = _shard_grad(summed, n, w[n].shape)
            big[n] = (g,) + tuple(_adamw_shard(g, w[n], m[n], v[n], "adamw_" + n))
    small_parts = _exchange(_pack_small(small_g, loss), "gather_small", True)
    small, total_loss = _adamw_small(small_parts, *({n: args[pre + n] for n in SMALL_NAMES} for pre in ("", "m_", "v_")),
                                     "adamw_small")
    big.update(small)

    outs = [total_loss.reshape(()), dx.reshape(x.shape)]
    for kind in range(4):
        outs += [big[n][kind].reshape(args[n].shape) for n in WEIGHT_NAMES]
    return tuple(outs)
```

```python
import jax
import jax.numpy as jnp
from jax import lax
from jax.experimental import pallas as pl
from jax.experimental.pallas import tpu as pltpu

F32 = jnp.float32
BF16 = jnp.bfloat16

D_MODEL = 1024
D_FF = 2816
N_HEADS = 8
HEAD_DIM = 64
ATT_W = N_HEADS * HEAD_DIM
PLE_DIM = 256
EPS = 1e-6
N_DEV = 8
MESH = pl.DeviceIdType.MESH

LANES = 128
V7X_SCOPED_VMEM_BYTES = 56 * 1024 * 1024

C_FQ, C_FK, C_FV, C_FL = 0, 512, 1024, 1536
C_SQ, C_SK, C_SV, C_GF, C_GS = 1792, 2304, 2816, 3328, 4352
IN_PAD = 5376
IN_REAL = 5128
FL_REAL_END = 1544

ADAM_LR = 0.001
ADAM_B1 = 0.9
ADAM_B2 = 0.999
ADAM_EPS = 1e-08
ADAM_WD = 0.01
ADAM_STEP = 10

PACK_ROWS = {"ffn1_w_gate": 352, "ffn1_w_up": 352, "ffn1_w_down": 352, "w_in": 656, "w_branch_fox": 64,
             "w_branch_sb": 64, "w_out": 128, "ffn2_w_gate": 352, "ffn2_w_up": 352, "ffn2_w_down": 352,
             "w_ple_gate": 128, "w_ple_proj": 32}
GATHER_GROUPS = (
    ("ffn1_w_gate", "ffn1_w_up", "ffn1_w_down"),
    ("w_in", "w_branch_fox", "w_branch_sb", "w_out"),
    ("ffn2_w_gate", "ffn2_w_up"),
    ("ffn2_w_down", "w_ple_gate", "w_ple_proj"),
)
W_IN_HALVES = ((0, 336), (336, 656))
PACK_ROWS.update({f"w_in#{i}": hi - lo for i, (lo, hi) in enumerate(W_IN_HALVES)})
SCATTER_GROUPS = (
    ("ffn1_w_gate",), ("ffn1_w_up",), ("ffn1_w_down",),
    ("w_in#0",), ("w_in#1",),
    ("ffn2_w_gate", "ffn2_w_up", "ffn2_w_down", "w_ple_gate", "w_ple_proj", "w_branch_fox", "w_branch_sb", "w_out"),
)
SUM_TILE_ROWS = (352, 352, 352, 336, 320, 368)


def _offsets(groups):
    off = {}
    for grp in groups:
        o = 0
        for n in grp:
            off[n] = o
            o += PACK_ROWS[n]
    return off


GATHER_OFF = _offsets(GATHER_GROUPS)
SCATTER_OFF = _offsets(SCATTER_GROUPS)
W_IN_ROWS = 641

SMALL_ROWS = 8


def _cparams(*sem):
    return pltpu.CompilerParams(dimension_semantics=sem, vmem_limit_bytes=V7X_SCOPED_VMEM_BYTES)


def _dot(a, b):
    return jnp.dot(a, b, preferred_element_type=F32)


def _dot_nt(a, b):
    return lax.dot_general(a, b, (((1,), (1,)), ((), ())), preferred_element_type=F32)


def _dot_tn(a, b):
    return lax.dot_general(a, b, (((0,), (0,)), ((), ())), preferred_element_type=F32)


def _split(x, parts):
    out = []
    r = x
    for _ in range(parts):
        p = r.astype(BF16)
        out.append(p)
        r = r - p.astype(F32)
    return out


def _dot_split(x, m, parts):
    acc = None
    for p in _split(x, parts):
        t = _dot(p, m)
        acc = t if acc is None else acc + t
    return acc


def _dot_split_left(m, x, parts):
    acc = None
    for p in _split(x, parts):
        t = _dot(m, p)
        acc = t if acc is None else acc + t
    return acc


def _rms_rinv(xf):
    return lax.rsqrt(jnp.mean(xf * xf, axis=-1, keepdims=True) + EPS)


def _sigmoid(x):
    return 1.0 / (1.0 + jnp.exp(-x))


def _softplus_neg_abs(z):
    return jnp.log(1.0 + jnp.exp(-jnp.abs(z)))


FFN_SHARD = D_FF // N_DEV
FFN_CHUNK = 4


def _ffn_w_spec(blk, index_map):
    return pl.BlockSpec((FFN_CHUNK, FFN_SHARD, D_MODEL), lambda *g: (index_map(*g), blk, 0))


def _ffn_w(ref):
    return ref[...].reshape(FFN_CHUNK * FFN_SHARD, D_MODEL)


def _ffn_fwd(x, gain, wbufs, blks, name, ride=None):
    s_len = x.shape[0]
    ts = min(512, s_len)
    fc = FFN_CHUNK * FFN_SHARD
    nt, nc = s_len // ts, D_FF // fc

    def body(x_ref, gain_ref, wg_ref, wu_ref, wd_ref, y_ref, g_ref, u_ref, h_ref, acc_scr):
        j = pl.program_id(1)

        @pl.when(j == 0)
        def _():
            xf = x_ref[...]
            h_ref[...] = ((xf * _rms_rinv(xf)) * gain_ref[...]).astype(BF16)
            acc_scr[...] = jnp.zeros_like(acc_scr)

        h = h_ref[...]
        g = _dot_nt(h, _ffn_w(wg_ref))
        u = _dot_nt(h, _ffn_w(wu_ref))
        g_ref[...] = g.astype(BF16)
        u_ref[...] = u.astype(BF16)
        a = (g * _sigmoid(g) * u).astype(BF16)
        acc_scr[...] += _dot(a, _ffn_w(wd_ref))

        @pl.when(j == nc - 1)
        def _():
            y_ref[...] = x_ref[...] + 0.5 * acc_scr[...]

    tok = pl.BlockSpec((ts, D_MODEL), lambda i, j: (i, 0))
    hid = pl.BlockSpec((ts, fc), lambda i, j: (i, j))
    return _ride_call(
        body, name, (nt, nc),
        [tok, pl.BlockSpec((1, D_MODEL), lambda i, j: (0, 0))] + [_ffn_w_spec(b, lambda i, j: j) for b in blks],
        [tok, hid, hid, tok],
        [jax.ShapeDtypeStruct((s_len, D_MODEL), F32), jax.ShapeDtypeStruct((s_len, D_FF), BF16),
         jax.ShapeDtypeStruct((s_len, D_FF), BF16), jax.ShapeDtypeStruct((s_len, D_MODEL), BF16)],
        [pltpu.VMEM((ts, D_MODEL), F32)], ("parallel", "arbitrary"), (x, gain, *wbufs), ride)


def _ffn_bwd_fused(x, dy, gain, g, u, wbufs, blks, name, ride=None):
    s_len = x.shape[0]
    ts = min(512, s_len)
    fc = FFN_CHUNK * FFN_SHARD
    nt = s_len // ts
    assert D_FF == 2 * fc

    def hidden(dy_ref, g_ref, u_ref, wg_ref, wu_ref, wd_ref, dg_ref, du_ref, act_ref):
        da = 0.5 * _dot_nt(dy_ref[...].astype(BF16), _ffn_w(wd_ref))
        gf = g_ref[...].astype(F32)
        uf = u_ref[...].astype(F32)
        sg = _sigmoid(gf)
        silu = gf * sg
        dg = (da * uf * (sg * (1.0 + gf * (1.0 - sg)))).astype(BF16)
        du = (da * silu).astype(BF16)
        dg_ref[...] = dg
        du_ref[...] = du
        act_ref[...] = (0.5 * silu * uf).astype(BF16)
        return _dot(dg, _ffn_w(wg_ref)) + _dot(du, _ffn_w(wu_ref))

    def first(dy_ref, g_ref, u_ref, wg_ref, wu_ref, wd_ref, dg_ref, du_ref, act_ref, dh_ref):
        dh_ref[...] = hidden(dy_ref, g_ref, u_ref, wg_ref, wu_ref, wd_ref, dg_ref, du_ref, act_ref)

    def second(x_ref, dy_ref, gain_ref, g_ref, u_ref, wg_ref, wu_ref, wd_ref, dh0_ref, dg_half, du_half, act_half,
               dg_ref, du_ref, act_ref, dx_ref, dgain_ref):
        i = pl.program_id(0)
        dh = dh0_ref[...] + hidden(dy_ref, g_ref, u_ref, wg_ref, wu_ref, wd_ref, dg_ref, du_ref, act_ref)
        xf = x_ref[...]
        r = _rms_rinv(xf)
        xhat = xf * r
        dgp = jnp.sum(dh * xhat, axis=0, keepdims=True)

        @pl.when(i == 0)
        def _():
            dgain_ref[...] = dgp

        @pl.when(i > 0)
        def _():
            dgain_ref[...] += dgp

        dn = dh * gain_ref[...]
        dx_ref[...] = dy_ref[...] + r * (dn - xhat * jnp.mean(dn * xhat, axis=-1, keepdims=True))

    tok = pl.BlockSpec((ts, D_MODEL), lambda i: (i, 0))
    row = pl.BlockSpec((1, D_MODEL), lambda i: (0, 0))
    hid = lambda c: pl.BlockSpec((ts, fc), lambda i: (i, c))
    wts = lambda c: [pl.BlockSpec((FFN_CHUNK, FFN_SHARD, D_MODEL), lambda i, b=b: (c, b, 0),
                                  pipeline_mode=pl.Buffered(1)) for b in blks]
    hidden_shapes = [jax.ShapeDtypeStruct((s_len, D_FF), BF16)] * 3
    dg, du, act, dh0, *rode = _ride_call(
        first, name + "_a", (nt,), [tok, hid(0), hid(0)] + wts(0), [hid(0)] * 3 + [tok],
        hidden_shapes + [jax.ShapeDtypeStruct((s_len, D_MODEL), F32)], [], ("parallel",),
        (dy, g, u, *wbufs), ride)
    filled = pl.BlockSpec(memory_space=pl.ANY)
    dg, du, act, dx, dgain = pl.pallas_call(
        second, name=name + "_b", grid=(nt,),
        in_specs=[tok, tok, row, hid(1), hid(1)] + wts(1) + [tok, filled, filled, filled],
        out_specs=[hid(1)] * 3 + [tok, row],
        out_shape=hidden_shapes + [jax.ShapeDtypeStruct((s_len, D_MODEL), F32), jax.ShapeDtypeStruct((1, D_MODEL), F32)],
        input_output_aliases={9: 0, 10: 1, 11: 2},
        compiler_params=_cparams("arbitrary"),
    )(x, dy, gain, g, u, *wbufs, dh0, dg, du, act)
    return (dg, du, act, dx, dgain, *rode)


def _wgrad(a, b, name, tk, tn, ride=None):
    s_len, k_dim = a.shape
    n_dim = b.shape[1]
    ts = min(2048, s_len)
    ns = s_len // ts

    def body(a_ref, b_ref, o_ref, acc):
        s = pl.program_id(2)
        p = _dot_tn(a_ref[...].astype(BF16), b_ref[...].astype(BF16))

        @pl.when(s == 0)
        def _():
            acc[...] = p

        @pl.when(s > 0)
        def _():
            acc[...] += p

        @pl.when(s == ns - 1)
        def _():
            o_ref[...] = acc[...].astype(BF16)

    out = _ride_call(
        body, name, (k_dim // tk, n_dim // tn, ns),
        [pl.BlockSpec((ts, tk), lambda k, n, s: (s, k)), pl.BlockSpec((ts, tn), lambda k, n, s: (s, n))],
        [pl.BlockSpec((tk, tn), lambda k, n, s: (k, n))], [jax.ShapeDtypeStruct((k_dim, n_dim), BF16)],
        [pltpu.VMEM((tk, tn), F32)], ("parallel", "parallel", "arbitrary"), (a, b), ride)
    return out[0] if ride is None else tuple(out)


HEAD_SUM_PARTS = 1


def _head_group_matrix():
    r = lax.broadcasted_iota(jnp.int32, (ATT_W, ATT_W), 0) // HEAD_DIM
    c = lax.broadcasted_iota(jnp.int32, (ATT_W, ATT_W), 1) // HEAD_DIM
    return (r == c).astype(BF16)


def _mix_fwd(x, gain, w_in, bias, qg, kg, name):
    s_len = x.shape[0]
    ts = min(512, s_len)
    nt = s_len // ts
    gmat = _head_group_matrix()

    def body(x_ref, gain_ref, w_ref, bias_ref, qg_ref, kg_ref, gm_ref,
             h_ref, fqr_ref, fkr_ref, fqn_ref, fkn_ref, fv_ref, logf_ref, f_ref, ft_ref,
             sq_ref, sk_ref, sv_ref, gf_ref, gs_ref, carry):
        i = pl.program_id(0)
        xf = x_ref[...]
        h = ((xf * _rms_rinv(xf)) * gain_ref[...]).astype(BF16)
        h_ref[...] = h
        gm = gm_ref[...]

        def proj(lo, n):
            return _dot_nt(h, w_ref[lo:lo + n, :])

        def headnorm(raw, g):
            ms = _dot_split(raw * raw, gm, HEAD_SUM_PARTS) * (1.0 / HEAD_DIM)
            return ((raw * lax.rsqrt(ms + EPS)) * g).astype(BF16)

        fq = proj(C_FQ, ATT_W)
        fqr_ref[...] = fq
        fqn_ref[...] = headnorm(fq, qg_ref[...])
        fk = proj(C_FK, ATT_W)
        fkr_ref[...] = fk
        fkn_ref[...] = headnorm(fk, kg_ref[...])
        fv_ref[...] = proj(C_FV, ATT_W).astype(BF16)
        sq_ref[...] = proj(C_SQ, ATT_W).astype(BF16)
        sk_ref[...] = proj(C_SK, ATT_W).astype(BF16)
        sv_ref[...] = proj(C_SV, ATT_W).astype(BF16)
        gf_ref[...] = proj(C_GF, D_MODEL)
        gs_ref[...] = proj(C_GS, D_MODEL)

        fl = proj(C_FL, LANES) + bias_ref[...]
        lane = lax.broadcasted_iota(jnp.int32, fl.shape, 1)
        logf = jnp.where(lane < N_HEADS, jnp.minimum(fl, 0.0) - _softplus_neg_abs(fl), 0.0)
        logf_ref[...] = logf

        @pl.when(i == 0)
        def _():
            carry[...] = jnp.zeros_like(carry)

        r = lax.broadcasted_iota(jnp.int32, (ts, ts), 0)
        c = lax.broadcasted_iota(jnp.int32, (ts, ts), 1)
        tri = (r >= c).astype(BF16)
        f_tile = _dot_split_left(tri, logf, 3) + carry[...]
        f_ref[...] = f_tile
        ft_ref[...] = f_tile.T[:N_HEADS, :]
        carry[...] = f_tile[ts - 1:ts, :]

    tok = lambda w: pl.BlockSpec((ts, w), lambda i: (i, 0))
    full = lambda a: pl.BlockSpec(a.shape, lambda i: (0, 0), pipeline_mode=pl.Buffered(1))
    f32o = lambda w: jax.ShapeDtypeStruct((s_len, w), F32)
    b16o = lambda w: jax.ShapeDtypeStruct((s_len, w), BF16)
    return _ride_call(
        body, name, (nt,),
        [tok(D_MODEL), full(gain), full(w_in), full(bias), full(qg), full(kg), full(gmat)],
        [
            tok(D_MODEL), tok(ATT_W), tok(ATT_W), tok(ATT_W), tok(ATT_W), tok(ATT_W), tok(LANES), tok(LANES),
            pl.BlockSpec((N_HEADS, ts), lambda i: (0, i)),
            tok(ATT_W), tok(ATT_W), tok(ATT_W), tok(D_MODEL), tok(D_MODEL),
        ],
        [
            b16o(D_MODEL), f32o(ATT_W), f32o(ATT_W), b16o(ATT_W), b16o(ATT_W), b16o(ATT_W), f32o(LANES), f32o(LANES),
            jax.ShapeDtypeStruct((N_HEADS, s_len), F32),
            b16o(ATT_W), b16o(ATT_W), b16o(ATT_W), f32o(D_MODEL), f32o(D_MODEL),
        ],
        [pltpu.VMEM((1, LANES), F32)], ("arbitrary",), (x, gain, w_in, bias, qg, kg, gmat), None)


ATT_T = 256
SB_ROWS = 2
FOX_ROWS = 2
EXP_ZERO = 88.0


def _att_tiling(s_len, rows):
    t = min(ATT_T, s_len)
    nr = min(rows, s_len // t)
    return t, nr, s_len // (t * nr)


def _pair_specs(s_len, tq):
    qblk = pl.BlockSpec((tq, LANES), lambda hp, i: (i, hp))
    kvfull = pl.BlockSpec((s_len, LANES), lambda hp, i: (0, hp))
    return qblk, kvfull


def _walk_tiles(i, nr, load, sub, flush, more=None, trips=None):
    base = i * nr
    for kk in range(nr - 1, -1, -1):
        rs = list(range(kk, nr))
        flush(base + kk, sub(rs, load(base + kk), [r == kk for r in rs]))

    done = jnp.int32(0)
    for last in range(nr - 1, -1, -1):
        rs = list(range(last + 1))

        def visit(n, rs=rs):
            kb = base - 1 - n
            flush(kb, sub(rs, load(kb), [False] * len(rs)))

        if trips is not None:
            todo = jnp.maximum(trips(base, last) - done, 0)

            def body(it, carry, start=done, visit=visit):
                visit(start + it)
                return carry

            lax.fori_loop(0, todo, body, jnp.int32(0))
            done = done + todo
        else:
            def step(state, visit=visit, last=last):
                visit(state[0])
                return state[0] + 1, more(last)

            done, _ = lax.while_loop(lambda state: jnp.logical_and(state[0] < base, state[1] > 0), step,
                                     (done, more(last)))


def _stack(parts):
    return parts[0] if len(parts) == 1 else jnp.concatenate(parts, axis=0)


def _stacked_halves(x, lo):
    z = jnp.zeros_like(x)
    return jnp.concatenate([jnp.where(lo, x, z), jnp.where(lo, z, x)], axis=0)


def _fox_qk_cap(q_gain, k_gain):
    cap = (HEAD_DIM ** 0.5) * jnp.max(jnp.abs(q_gain)) * jnp.max(jnp.abs(k_gain))
    return (cap * 1.01 + 1.0).reshape(1).astype(F32)


def _fox_trips(hp, flast_ref, cap_ref, fq, level):
    def trips(base, r):
        gap = [jnp.max(fq[r][j] - level(r, j)) + cap_ref[0] for j in (0, 1)]

        def needed(n):
            kb = jnp.maximum(base - 1 - n, 0)
            return jnp.logical_or(gap[0] - flast_ref[2 * hp, kb] > -EXP_ZERO,
                                  gap[1] - flast_ref[2 * hp + 1, kb] > -EXP_ZERO)

        return lax.while_loop(lambda n: jnp.logical_and(n < base, needed(n)), lambda n: n + 1, jnp.int32(0))
    return trips


def _fox_fwd(q, k, v, f_wide, f_row, cap, name, ride=None):
    s_len = q.shape[0]
    t, nr, nq = _att_tiling(s_len, FOX_ROWS)

    def body(q_ref, k_ref, v_ref, f_ref, ft_ref, cap_ref, fl_ref, y_ref, lse_ref, m_ref, l_ref, acc_ref):
        hp = pl.program_id(0)
        i = pl.program_id(1)
        lane = lax.broadcasted_iota(jnp.int32, (t, LANES), 1)
        lo = lane < HEAD_DIM
        causal = lax.broadcasted_iota(jnp.int32, (t, t), 0) >= lax.broadcasted_iota(jnp.int32, (t, t), 1)
        rows = [pl.ds(r * t, t) for r in range(nr)]
        slab = lambda r, j: pl.ds((2 * r + j) * t, t)
        qst = [_stacked_halves(q_ref[rw, :] * jnp.asarray(HEAD_DIM ** -0.5, BF16), lo) for rw in rows]
        q_all = _stack(qst)
        fq = [[f_ref[rw, j * HEAD_DIM:j * HEAD_DIM + 1] for j in (0, 1)] for rw in rows]
        m_ref[...] = jnp.full(m_ref.shape, -1e30, F32)
        l_ref[...] = jnp.zeros_like(l_ref)
        acc_ref[...] = jnp.zeros_like(acc_ref)

        def load(kb):
            k0 = pl.multiple_of(kb * t, t)
            frow = [ft_ref[pl.ds(2 * hp + j, 1), pl.ds(k0, t)] for j in (0, 1)]
            return k_ref[pl.ds(k0, t), :], v_ref[pl.ds(k0, t), :], frow

        def sub(rs, tiles, masked):
            kblk, vblk, frow = tiles
            z = _dot_nt(q_all if len(rs) == nr else _stack([qst[r] for r in rs]), kblk)
            slabs = [(r, j) for r in range(len(rs)) for j in (0, 1)]
            ps, alpha = [], []
            for n, (r, j) in enumerate(slabs):
                sl = slab(rs[r], j)
                m_old = m_ref[sl, :]
                s = z[n * t:(n + 1) * t, :] + (fq[rs[r]][j] - frow[j])
                if masked[r]:
                    s = jnp.where(causal, s, -1e30)
                mj = jnp.maximum(m_old, jnp.max(s, axis=1, keepdims=True))
                aj = jnp.exp(m_old - mj)
                p = jnp.exp(s - jnp.tile(mj, (1, t // LANES)))
                m_ref[sl, :] = mj
                l_ref[sl, :] = aj * l_ref[sl, :] + jnp.sum(p, axis=1, keepdims=True)
                alpha.append(aj)
                ps.append(p.astype(BF16))
            pv = _dot(_stack(ps), vblk)
            for n, (r, j) in enumerate(slabs):
                sl = slab(rs[r], j)
                acc_ref[sl, :] = acc_ref[sl, :] * alpha[n] + pv[n * t:(n + 1) * t, :]
            return None

        trips = _fox_trips(hp, fl_ref, cap_ref, fq, lambda r, j: m_ref[slab(r, j), :])
        _walk_tiles(i, nr, load, sub, lambda kb, side: None, trips=trips)
        for r, rw in enumerate(rows):
            l0, l1 = l_ref[slab(r, 0), :], l_ref[slab(r, 1), :]
            y_ref[rw, :] = jnp.where(lo, acc_ref[slab(r, 0), :] / l0, acc_ref[slab(r, 1), :] / l1).astype(BF16)
            lse_ref[0, rw, :] = jnp.where(lo, m_ref[slab(r, 0), :] + jnp.log(l0), m_ref[slab(r, 1), :] + jnp.log(l1))

    state = [pltpu.VMEM((2 * nr * t, LANES), F32)] * 3
    qblk, kvfull = _pair_specs(s_len, t * nr)
    return _ride_call(
        body, name, (N_HEADS // 2, nq),
        [qblk, kvfull, kvfull,
         qblk, pl.BlockSpec((N_HEADS, s_len), lambda hp, i: (0, 0)),
         pl.BlockSpec(memory_space=pltpu.SMEM), pl.BlockSpec(memory_space=pltpu.SMEM)],
        [qblk, pl.BlockSpec((1, t * nr, LANES), lambda hp, i: (hp, i, 0))],
        [jax.ShapeDtypeStruct((s_len, ATT_W), BF16), jax.ShapeDtypeStruct((N_HEADS // 2, s_len, LANES), F32)],
        state, ("parallel", "parallel"), (q, k, v, f_wide, f_row, cap, f_row[:, t - 1::t]), ride)


def _fox_bwd(q, k, v, dy, y, lse, f_wide, f_row, cap, name, ride=None):
    s_len = q.shape[0]
    t, nr, nq = _att_tiling(s_len, FOX_ROWS)

    def body(q_ref, k_ref, v_ref, dy_ref, y_ref, lse_ref, f_ref, ft_ref, cap_ref, fl_ref,
             dq_ref, dkt_ref, dvt_ref, dft_ref, dqs_ref, rsum_ref):
        hp = pl.program_id(0)
        i = pl.program_id(1)

        @pl.when(i == 0)
        def _():
            dkt_ref[...] = jnp.zeros_like(dkt_ref)
            dvt_ref[...] = jnp.zeros_like(dvt_ref)
            dft_ref[...] = jnp.zeros_like(dft_ref)

        dqs_ref[...] = jnp.zeros_like(dqs_ref)
        rsum_ref[...] = jnp.zeros_like(rsum_ref)
        slab = lambda r, j: pl.ds((2 * r + j) * t, t)

        lane = lax.broadcasted_iota(jnp.int32, (t, LANES), 1)
        lo = lane < HEAD_DIM
        causal = lax.broadcasted_iota(jnp.int32, (t, t), 0) >= lax.broadcasted_iota(jnp.int32, (t, t), 1)
        rows = [pl.ds(r * t, t) for r in range(nr)]
        qst, dyst, delta, lse, fq = [], [], [], [], []
        for rw in rows:
            qst.append(_stacked_halves(q_ref[rw, :] * jnp.asarray(HEAD_DIM ** -0.5, BF16), lo))
            dyb = dy_ref[rw, :]
            dyst.append(_stacked_halves(dyb, lo))
            prod = dyb.astype(F32) * y_ref[rw, :].astype(F32)
            delta.append([jnp.sum(jnp.where(lo, prod, 0.0), axis=1, keepdims=True),
                          jnp.sum(jnp.where(lo, 0.0, prod), axis=1, keepdims=True)])
            lse_b = lse_ref[0, rw, :]
            lse.append([lse_b[:, 0:1], lse_b[:, HEAD_DIM:HEAD_DIM + 1]])
            fq.append([f_ref[rw, j * HEAD_DIM:j * HEAD_DIM + 1] for j in (0, 1)])

        q_all, dy_all = _stack(qst), _stack(dyst)
        q_all_t, dy_all_t = q_all.T, dy_all.T

        def load(kb):
            k0 = pl.multiple_of(kb * t, t)
            frow = [ft_ref[pl.ds(2 * hp + j, 1), pl.ds(k0, t)] for j in (0, 1)]
            return k_ref[pl.ds(k0, t), :], v_ref[pl.ds(k0, t), :], frow

        def sub(rs, tiles, masked):
            kblk, vblk, frow = tiles
            qs, dys = (q_all, dy_all) if len(rs) == nr else (_stack([qst[r] for r in rs]), _stack([dyst[r] for r in rs]))
            cols = slice(2 * rs[0] * t, 2 * (rs[-1] + 1) * t)
            z = _dot_nt(qs, kblk)
            dp = _dot_nt(dys, vblk)
            slabs = [(r, j) for r in range(len(rs)) for j in (0, 1)]
            pb, dsb, col = [], [], [None, None]
            for n, (r, j) in enumerate(slabs):
                sl = slice(n * t, (n + 1) * t)
                s = z[sl, :] + (fq[rs[r]][j] - frow[j])
                p = jnp.exp(s - lse[rs[r]][j])
                if masked[r]:
                    p = jnp.where(causal, p, 0.0)
                ds = p * (dp[sl, :] - delta[rs[r]][j])
                c = jnp.sum(ds, axis=0, keepdims=True)
                col[j] = c if col[j] is None else col[j] + c
                rsum_ref[slab(rs[r], j), :] += jnp.sum(ds, axis=1, keepdims=True)
                pb.append(p.astype(BF16))
                dsb.append(ds.astype(BF16))
            p_all, ds_all = _stack(pb), _stack(dsb)
            dqs_ref[pl.ds(2 * rs[0] * t, len(slabs) * t), :] += _dot(ds_all, kblk)
            return _dot(q_all_t[:, cols], ds_all), _dot(dy_all_t[:, cols], p_all), col

        def flush(kb, side):
            k0 = pl.multiple_of(kb * t, t)
            dkt_ref[:, pl.ds(k0, t)] += side[0]
            dvt_ref[:, pl.ds(k0, t)] += side[1]
            for j in (0, 1):
                dft_ref[0, pl.ds(j, 1), pl.ds(k0, t)] -= side[2][j]

        trips = _fox_trips(hp, fl_ref, cap_ref, fq, lambda r, j: lse[r][j])
        _walk_tiles(i, nr, load, sub, flush, trips=trips)
        for r, rw in enumerate(rows):
            dq_ref[rw, :] = jnp.where(lo, dqs_ref[slab(r, 0), :], dqs_ref[slab(r, 1), :]) * (HEAD_DIM ** -0.5)
            rs_t = jnp.where(lo, rsum_ref[slab(r, 0), :], rsum_ref[slab(r, 1), :]).T
            q0 = pl.multiple_of((i * nr + r) * t, t)
            for j in (0, 1):
                dft_ref[0, pl.ds(j, 1), pl.ds(q0, t)] += rs_t[j * HEAD_DIM:j * HEAD_DIM + 1, :]

    state = [pltpu.VMEM((2 * nr * t, LANES), F32), pltpu.VMEM((2 * nr * t, 1), F32)]
    qblk, kvfull = _pair_specs(s_len, t * nr)
    kvfull_t = pl.BlockSpec((LANES, s_len), lambda hp, i: (hp, 0))
    return _ride_call(
        body, name, (N_HEADS // 2, nq),
        [qblk, kvfull, kvfull, qblk, qblk,
         pl.BlockSpec((1, t * nr, LANES), lambda hp, i: (hp, i, 0)),
         qblk, pl.BlockSpec((N_HEADS, s_len), lambda hp, i: (0, 0)),
         pl.BlockSpec(memory_space=pltpu.SMEM), pl.BlockSpec(memory_space=pltpu.SMEM)],
        [qblk, kvfull_t, kvfull_t, pl.BlockSpec((1, 8, s_len), lambda hp, i: (hp, 0, 0))],
        [jax.ShapeDtypeStruct((s_len, ATT_W), F32)] + [jax.ShapeDtypeStruct((ATT_W, s_len), F32)] * 2
        + [jax.ShapeDtypeStruct((N_HEADS // 2, 8, s_len), F32)],
        state, ("arbitrary", "arbitrary"), (q, k, v, dy, y, lse, f_wide, f_row, cap, f_row[:, t - 1::t]), ride)


def _sb_more(c_ref, t):
    def more(r):
        return (jnp.max(c_ref[pl.ds(2 * r * t, 2 * t), :]) > -EXP_ZERO).astype(jnp.int32)
    return more


def _stacked_split_dot(slabs, m, parts):
    split = [_split(x, parts) for x in slabs]
    acc = None
    for p in range(parts):
        d = _dot(_stack([s[p] for s in split]), m)
        acc = d if acc is None else acc + d
    return acc


def _sb_weights(z, c, strict, upper, t):
    logs = []
    for n in range(z.shape[0] // t):
        zn = z[n * t:(n + 1) * t, :]
        sp = _softplus_neg_abs(zn)
        l1m = jnp.minimum(-zn, 0.0) - sp
        if strict[n] is not None:
            l1m = jnp.where(strict[n], l1m, 0.0)
        logs.append((jnp.minimum(zn, 0.0) - sp, l1m))
    suf = _stacked_split_dot([l1m for _, l1m in logs], upper, 1)
    out = []
    for n, (logb, l1m) in enumerate(logs):
        after = c[n] + suf[n * t:(n + 1) * t, :]
        a = jnp.exp(logb + after)
        if strict[n] is not None:
            a = jnp.where(strict[n], a, 0.0)
        out.append((logb, a, after[:, 0:1] + l1m[:, 0:1]))
    return out


def _sb_fwd(q, k, v, name, ride=None):
    s_len = q.shape[0]
    t, nr, nq = _att_tiling(s_len, SB_ROWS)

    def body(q_ref, k_ref, v_ref, y_ref, yf_ref, c_ref, acc_ref):
        i = pl.program_id(1)
        lane = lax.broadcasted_iota(jnp.int32, (t, LANES), 1)
        lo = lane < HEAD_DIM
        ri = lax.broadcasted_iota(jnp.int32, (t, t), 0)
        ci = lax.broadcasted_iota(jnp.int32, (t, t), 1)
        strict = ci < ri
        upper = (ri > ci).astype(BF16)
        rows = [pl.ds(r * t, t) for r in range(nr)]
        slab = lambda r, j: pl.ds((2 * r + j) * t, t)
        qst = [_stacked_halves(q_ref[rw, :] * jnp.asarray(HEAD_DIM ** -0.5, BF16), lo) for rw in rows]
        q_all = _stack(qst)
        c_ref[...] = jnp.zeros_like(c_ref)
        acc_ref[...] = jnp.zeros_like(acc_ref)

        def load(kb):
            k0 = pl.multiple_of(kb * t, t)
            return k_ref[pl.ds(k0, t), :], v_ref[pl.ds(k0, t), :]

        def sub(rs, tiles, masked):
            kblk, vblk = tiles
            z = _dot_nt(q_all if len(rs) == nr else _stack([qst[r] for r in rs]), kblk)
            slabs = [(r, j) for r in range(len(rs)) for j in (0, 1)]
            w = _sb_weights(z, [c_ref[slab(rs[r], j), :] for r, j in slabs],
                            [strict if masked[r] else None for r, j in slabs], upper, t)
            for n, (r, j) in enumerate(slabs):
                c_ref[slab(rs[r], j), :] = w[n][2]
            acc_ref[pl.ds(2 * rs[0] * t, len(slabs) * t), :] += _dot(_stack([a.astype(BF16) for _, a, _ in w]), vblk)
            return None

        _walk_tiles(i, nr, load, sub, lambda kb, side: None, more=_sb_more(c_ref, t))
        for r, rw in enumerate(rows):
            y = jnp.where(lo, acc_ref[slab(r, 0), :], acc_ref[slab(r, 1), :])
            y_ref[rw, :] = y.astype(BF16)
            yf_ref[rw, :] = y

    qblk, kvfull = _pair_specs(s_len, t * nr)
    return _ride_call(
        body, name, (N_HEADS // 2, nq), [qblk, kvfull, kvfull], [qblk, qblk],
        [jax.ShapeDtypeStruct((s_len, ATT_W), BF16), jax.ShapeDtypeStruct((s_len, ATT_W), F32)],
        [pltpu.VMEM((2 * nr * t, 1), F32), pltpu.VMEM((2 * nr * t, LANES), F32)], ("parallel", "parallel"),
        (q, k, v), ride)


def _sb_bwd(q, k, v, dy, yf, name):
    s_len = q.shape[0]
    t, nr, nq = _att_tiling(s_len, SB_ROWS)

    def body(q_ref, k_ref, v_ref, dy_ref, yf_ref, dq_ref, dkt_ref, dvt_ref, c_ref, e_ref, dqs_ref):
        i = pl.program_id(1)

        @pl.when(i == 0)
        def _():
            dkt_ref[...] = jnp.zeros_like(dkt_ref)
            dvt_ref[...] = jnp.zeros_like(dvt_ref)

        c_ref[...] = jnp.zeros_like(c_ref)
        e_ref[...] = jnp.zeros_like(e_ref)
        dqs_ref[...] = jnp.zeros_like(dqs_ref)
        slab = lambda r, j: pl.ds((2 * r + j) * t, t)

        lane = lax.broadcasted_iota(jnp.int32, (t, LANES), 1)
        lo = lane < HEAD_DIM
        ri = lax.broadcasted_iota(jnp.int32, (t, t), 0)
        ci = lax.broadcasted_iota(jnp.int32, (t, t), 1)
        strict = ci < ri
        upper = (ri > ci).astype(BF16)
        upper_incl = (ri >= ci).astype(BF16)
        rows = [pl.ds(r * t, t) for r in range(nr)]
        qst, dyst, delta = [], [], []
        for rw in rows:
            qst.append(_stacked_halves(q_ref[rw, :] * jnp.asarray(HEAD_DIM ** -0.5, BF16), lo))
            dyb = dy_ref[rw, :]
            dyst.append(_stacked_halves(dyb, lo))
            prod = dyb.astype(F32) * yf_ref[rw, :]
            delta.append([jnp.sum(jnp.where(lo, prod, 0.0), axis=1, keepdims=True),
                          jnp.sum(jnp.where(lo, 0.0, prod), axis=1, keepdims=True)])
        q_all, dy_all = _stack(qst), _stack(dyst)
        q_all_t, dy_all_t = q_all.T, dy_all.T

        def load(kb):
            k0 = pl.multiple_of(kb * t, t)
            return k_ref[pl.ds(k0, t), :], v_ref[pl.ds(k0, t), :]

        def sub(rs, tiles, masked):
            kblk, vblk = tiles
            qs, dys = (q_all, dy_all) if len(rs) == nr else (_stack([qst[r] for r in rs]), _stack([dyst[r] for r in rs]))
            cols = slice(2 * rs[0] * t, 2 * (rs[-1] + 1) * t)
            slabs = [(r, j) for r in range(len(rs)) for j in (0, 1)]
            z = _dot_nt(qs, kblk)
            w = _sb_weights(z, [c_ref[slab(rs[r], j), :] for r, j in slabs],
                            [strict if masked[r] else None for r, j in slabs], upper, t)
            da = _dot_nt(dys, vblk)
            ab = [a.astype(BF16) for _, a, _ in w]
            dl = [ab[n].astype(F32) * da[n * t:(n + 1) * t, :] for n in range(len(slabs))]
            tail = _stacked_split_dot(dl, upper_incl, 2)
            dzb = []
            for n, (r, j) in enumerate(slabs):
                sl = slab(rs[r], j)
                tl = tail[n * t:(n + 1) * t, :]
                e = e_ref[sl, :]
                dl1m = (delta[rs[r]][j] - e) - tl
                e_ref[sl, :] = e + tl[:, 0:1]
                c_ref[sl, :] = w[n][2]
                dz = dl[n] - jnp.exp(w[n][0]) * (dl[n] + dl1m)
                if masked[r]:
                    dz = jnp.where(strict, dz, 0.0)
                dzb.append(dz.astype(BF16))
            a_all, dz_all = _stack(ab), _stack(dzb)
            dqs_ref[pl.ds(2 * rs[0] * t, len(slabs) * t), :] += _dot(dz_all, kblk)
            return _dot(q_all_t[:, cols], dz_all), _dot(dy_all_t[:, cols], a_all)

        def flush(kb, side):
            k0 = pl.multiple_of(kb * t, t)
            dkt_ref[:, pl.ds(k0, t)] += side[0]
            dvt_ref[:, pl.ds(k0, t)] += side[1]

        _walk_tiles(i, nr, load, sub, flush, more=_sb_more(c_ref, t))
        for r, rw in enumerate(rows):
            dq_ref[rw, :] = jnp.where(lo, dqs_ref[slab(r, 0), :], dqs_ref[slab(r, 1), :]) * (HEAD_DIM ** -0.5)

    qblk, kvfull = _pair_specs(s_len, t * nr)
    kvfull_t = pl.BlockSpec((LANES, s_len), lambda hp, i: (hp, 0))
    return pl.pallas_call(
        body, name=name, grid=(N_HEADS // 2, nq),
        in_specs=[qblk, kvfull, kvfull, qblk, qblk],
        out_specs=[qblk, kvfull_t, kvfull_t],
        out_shape=[jax.ShapeDtypeStruct((s_len, ATT_W), F32)] + [jax.ShapeDtypeStruct((ATT_W, s_len), F32)] * 2,
        scratch_shapes=[pltpu.VMEM((2 * nr * t, 1), F32), pltpu.VMEM((2 * nr * t, 1), F32),
                        pltpu.VMEM((2 * nr * t, LANES), F32)],
        compiler_params=_cparams("arbitrary", "arbitrary"),
    )(q, k, v, dy, yf)


def _merge_fwd(x, yf, ys, gf, gs, wbf, wbs, wo, name):
    s_len = x.shape[0]
    ts = min(512, s_len)

    def body(x_ref, yf_ref, ys_ref, gf_ref, gs_ref, wbf_ref, wbs_ref, wo_ref, o_ref):
        merged = (_sigmoid(gf_ref[...]) * _dot_nt(yf_ref[...], wbf_ref[...])
                  + _sigmoid(gs_ref[...]) * _dot_nt(ys_ref[...], wbs_ref[...]))
        o_ref[...] = x_ref[...] + _dot(merged.astype(BF16), wo_ref[...])

    tok = lambda w: pl.BlockSpec((ts, w), lambda i: (i, 0))
    full = lambda a: pl.BlockSpec(a.shape, lambda i: (0, 0))
    return pl.pallas_call(
        body, name=name, grid=(s_len // ts,),
        in_specs=[tok(D_MODEL), tok(ATT_W), tok(ATT_W), tok(D_MODEL), tok(D_MODEL), full(wbf), full(wbs), full(wo)],
        out_specs=tok(D_MODEL),
        out_shape=jax.ShapeDtypeStruct((s_len, D_MODEL), F32),
        compiler_params=_cparams("parallel"),
    )(x, yf, ys, gf, gs, wbf, wbs, wo)


def _merge_bwd(dx, yf, ys, gf, gs, wbf, wbs, wo, name):
    s_len = dx.shape[0]
    ts = min(512, s_len)

    def body(dx_ref, yf_ref, ys_ref, gf_ref, gs_ref, wbf_ref, wbs_ref, wo_ref,
             dyf_ref, dys_ref, dgf_ref, dgs_ref, dbf_ref, dbs_ref, mg_ref):
        bf = _dot_nt(yf_ref[...], wbf_ref[...])
        bs = _dot_nt(ys_ref[...], wbs_ref[...])
        sf = _sigmoid(gf_ref[...])
        ss = _sigmoid(gs_ref[...])
        mg_ref[...] = (sf * bf + ss * bs).astype(BF16)
        dm = _dot_nt(dx_ref[...].astype(BF16), wo_ref[...])
        dbf = (dm * sf).astype(BF16)
        dbs = (dm * ss).astype(BF16)
        dbf_ref[...] = dbf
        dbs_ref[...] = dbs
        dgf_ref[...] = (dm * bf * (sf * (1.0 - sf))).astype(BF16)
        dgs_ref[...] = (dm * bs * (ss * (1.0 - ss))).astype(BF16)
        dyf_ref[...] = _dot(dbf, wbf_ref[...]).astype(BF16)
        dys_ref[...] = _dot(dbs, wbs_ref[...]).astype(BF16)

    tok = lambda w: pl.BlockSpec((ts, w), lambda i: (i, 0))
    full = lambda a: pl.BlockSpec(a.shape, lambda i: (0, 0))
    b16o = lambda w: jax.ShapeDtypeStruct((s_len, w), BF16)
    return pl.pallas_call(
        body, name=name, grid=(s_len // ts,),
        in_specs=[tok(D_MODEL), tok(ATT_W), tok(ATT_W), tok(D_MODEL), tok(D_MODEL), full(wbf), full(wbs), full(wo)],
        out_specs=[tok(ATT_W), tok(ATT_W)] + [tok(D_MODEL)] * 5,
        out_shape=[b16o(ATT_W), b16o(ATT_W)] + [b16o(D_MODEL)] * 5,
        compiler_params=_cparams("parallel"),
    )(dx, yf, ys, gf, gs, wbf, wbs, wo)


def _mix_bwd(x, dx_in, gain, w_in, fqr, fkr, dfqn, dfkn_t, qg, kg, dfv_t, df_col, logf, dsq, dsk_t, dsv_t, dgf, dgs,
             name):
    s_len = x.shape[0]
    ts = min(256, s_len)
    nt = s_len // ts
    gmat = _head_group_matrix()

    def body(x_ref, dxi_ref, gain_ref, w_ref, fqr_ref, fkr_ref, dfqn_ref, dfknt_ref, qg_ref, kg_ref, gm_ref,
             dfvt_ref, df_ref, logf_ref, dsq_ref, dskt_ref, dsvt_ref, dgf_ref, dgs_ref,
             dp_ref, dx_ref, dgain_ref, dqg_ref, dkg_ref, dbias_ref, carry):
        i = pl.program_id(0)

        @pl.when(i == 0)
        def _():
            carry[...] = jnp.zeros_like(carry)
            dgain_ref[...] = jnp.zeros_like(dgain_ref)
            dqg_ref[...] = jnp.zeros_like(dqg_ref)
            dkg_ref[...] = jnp.zeros_like(dkg_ref)
            dbias_ref[...] = jnp.zeros_like(dbias_ref)

        gm = gm_ref[...]

        def headnorm_bwd(raw, dout, g, dg_ref):
            ms = _dot_split(raw * raw, gm, HEAD_SUM_PARTS) * (1.0 / HEAD_DIM)
            r = lax.rsqrt(ms + EPS)
            nrm = raw * r
            dg_ref[...] += jnp.sum(dout * nrm, axis=0, keepdims=True)
            dn = dout * g
            mean_h = _dot_split(dn * nrm, gm, HEAD_SUM_PARTS) * (1.0 / HEAD_DIM)
            return r * (dn - nrm * mean_h)

        dp_ref[:, C_FQ:C_FQ + ATT_W] = headnorm_bwd(fqr_ref[...], dfqn_ref[...], qg_ref[...], dqg_ref).astype(BF16)
        dp_ref[:, C_FK:C_FK + ATT_W] = headnorm_bwd(fkr_ref[...], dfknt_ref[...].T, kg_ref[...], dkg_ref).astype(BF16)
        dp_ref[:, C_FV:C_FV + ATT_W] = dfvt_ref[...].T.astype(BF16)
        dp_ref[:, C_SQ:C_SQ + ATT_W] = dsq_ref[...].astype(BF16)
        dp_ref[:, C_SK:C_SK + ATT_W] = dskt_ref[...].T.astype(BF16)
        dp_ref[:, C_SV:C_SV + ATT_W] = dsvt_ref[...].T.astype(BF16)
        dp_ref[:, C_GF:C_GF + D_MODEL] = dgf_ref[...]
        dp_ref[:, C_GS:C_GS + D_MODEL] = dgs_ref[...]

        r_ = lax.broadcasted_iota(jnp.int32, (ts, ts), 0)
        c_ = lax.broadcasted_iota(jnp.int32, (ts, ts), 1)
        rev = (c_ >= r_).astype(BF16)
        dlogf = _dot_split_left(rev, df_ref[...], 3) + carry[...]
        carry[...] = dlogf[0:1, :]
        lane = lax.broadcasted_iota(jnp.int32, (ts, LANES), 1)
        dfl = jnp.where(lane < N_HEADS, dlogf * (1.0 - jnp.exp(logf_ref[...])), 0.0)
        dbias_ref[...] += jnp.sum(dfl, axis=0, keepdims=True)
        dp_ref[:, C_FL:C_FL + LANES] = dfl.astype(BF16)
        dp_ref[:, C_FL + LANES:C_SQ] = jnp.zeros((ts, C_SQ - C_FL - LANES), BF16)

        dh = _dot(dp_ref[...], w_ref[...])
        xf = x_ref[...]
        r = _rms_rinv(xf)
        xhat = xf * r
        dgain_ref[...] += jnp.sum(dh * xhat, axis=0, keepdims=True)
        dn = dh * gain_ref[...]
        dx_ref[...] = dxi_ref[...] + r * (dn - xhat * jnp.mean(dn * xhat, axis=-1, keepdims=True))

    tok = lambda w: pl.BlockSpec((ts, w), lambda i: (nt - 1 - i, 0))
    full = lambda a: pl.BlockSpec(a.shape, lambda i: (0, 0))
    row = lambda w: pl.BlockSpec((1, w), lambda i: (0, 0))
    tok_t = pl.BlockSpec((ATT_W, ts), lambda i: (0, nt - 1 - i))
    return _ride_call(
        body, name, (nt,),
        [tok(D_MODEL), tok(D_MODEL), full(gain), full(w_in), tok(ATT_W), tok(ATT_W), tok(ATT_W), tok_t,
         full(qg), full(kg), full(gmat), tok_t, tok(LANES), tok(LANES), tok(ATT_W), tok_t, tok_t,
         tok(D_MODEL), tok(D_MODEL)],
        [tok(IN_PAD), tok(D_MODEL), row(D_MODEL), row(ATT_W), row(ATT_W), row(LANES)],
        [jax.ShapeDtypeStruct((s_len, IN_PAD), BF16), jax.ShapeDtypeStruct((s_len, D_MODEL), F32),
         jax.ShapeDtypeStruct((1, D_MODEL), F32), jax.ShapeDtypeStruct((1, ATT_W), F32),
         jax.ShapeDtypeStruct((1, ATT_W), F32), jax.ShapeDtypeStruct((1, LANES), F32)],
        [pltpu.VMEM((1, LANES), F32)], ("arbitrary",),
        (x, dx_in, gain, w_in, fqr, fkr, dfqn, dfkn_t, qg, kg, gmat, dfv_t, df_col, logf, dsq, dsk_t, dsv_t, dgf, dgs),
        None)


def _ple_loss(x, p, tgt, gain, wpg, wpp, name):
    s_len = x.shape[0]
    ts = min(512, s_len)

    def body(x_ref, p_ref, t_ref, gain_ref, wpg_ref, wpp_ref, dx_ref, n_ref, ds_ref, dpp_ref, dgain_ref, loss_ref):
        i = pl.program_id(0)

        @pl.when(i == 0)
        def _():
            dgain_ref[...] = jnp.zeros_like(dgain_ref)
            loss_ref[...] = jnp.zeros_like(loss_ref)

        xf = x_ref[...]
        r = _rms_rinv(xf)
        n = xf * r
        hn = (n * gain_ref[...]).astype(BF16)
        n_ref[...] = hn
        sg = _sigmoid(_dot(hn, wpg_ref[...]))
        pp = _dot_nt(p_ref[...].astype(BF16), wpp_ref[...])
        err = (xf + sg * pp) - t_ref[...]
        sq = jnp.sum(jnp.sum(err * err, axis=1, keepdims=True), axis=0, keepdims=True)
        loss_ref[...] += (0.5 / D_MODEL) * sq
        dout = err * (1.0 / D_MODEL)
        dpp_ref[...] = (dout * sg).astype(BF16)
        ds = (dout * pp * (sg * (1.0 - sg))).astype(BF16)
        ds_ref[...] = ds
        dhn = _dot_nt(ds, wpg_ref[...])
        dgain_ref[...] += jnp.sum(dhn * n, axis=0, keepdims=True)
        dn = dhn * gain_ref[...]
        dx_ref[...] = dout + r * (dn - n * jnp.mean(dn * n, axis=-1, keepdims=True))

    tok = lambda w: pl.BlockSpec((ts, w), lambda i: (i, 0))
    full = lambda a: pl.BlockSpec(a.shape, lambda i: (0, 0))
    return pl.pallas_call(
        body, name=name, grid=(s_len // ts,),
        in_specs=[tok(D_MODEL), tok(PLE_DIM), tok(D_MODEL), full(gain), full(wpg), full(wpp)],
        out_specs=[tok(D_MODEL), tok(D_MODEL), tok(D_MODEL), tok(D_MODEL),
                   pl.BlockSpec((1, D_MODEL), lambda i: (0, 0)), pl.BlockSpec((8, LANES), lambda i: (0, 0))],
        out_shape=[jax.ShapeDtypeStruct((s_len, D_MODEL), F32), jax.ShapeDtypeStruct((s_len, D_MODEL), BF16),
                   jax.ShapeDtypeStruct((s_len, D_MODEL), BF16), jax.ShapeDtypeStruct((s_len, D_MODEL), BF16),
                   jax.ShapeDtypeStruct((1, D_MODEL), F32), jax.ShapeDtypeStruct((8, LANES), F32)],
        compiler_params=_cparams("arbitrary"),
    )(x, p, tgt, gain, wpg, wpp)


def _exchange(x, name, broadcast):
    def body(x_ref, out_ref, send_sems, recv_sems, local_sem):
        _exchange_start(x_ref, out_ref, send_sems, recv_sems, local_sem, broadcast)
        _exchange_wait(x_ref, out_ref, send_sems, recv_sems, local_sem, broadcast)

    return pl.pallas_call(
        body, name=name,
        in_specs=[EXCHANGE_SPEC],
        out_specs=EXCHANGE_SPEC,
        out_shape=_exchange_shape(x, broadcast),
        scratch_shapes=list(EXCHANGE_SEMS),
        compiler_params=pltpu.CompilerParams(has_side_effects=True),
    )(x)


def _gather_two_level(x, name):
    def body(x_ref, out_ref, send_sems, recv_sems, local_sem):
        mx, my, mc = lax.axis_index("x"), lax.axis_index("y"), lax.axis_index("c")
        me, sibling = (mx, my, mc), (mx, my, 1 - mc)
        chips = [(1 - mx, my), (mx, 1 - my), (1 - mx, 1 - my)]

        def slot(px, py, pc):
            return out_ref.at[4 * px + 2 * py + pc]

        def copy(k, block, to, src=None):
            return pltpu.make_async_remote_copy(
                src_ref=slot(*block) if src is None else src, dst_ref=slot(*block),
                send_sem=send_sems.at[k], recv_sem=recv_sems.at[k], device_id=to, device_id_type=MESH)

        mine = pltpu.make_async_copy(x_ref, slot(*me), local_sem)
        mine.start()
        first = [copy(0, me, sibling, src=x_ref)]
        first += [copy(1 + j, me, (*chip, mc), src=x_ref) for j, chip in enumerate(chips)]
        for cp in first:
            cp.start()
        passed = [copy(4 + j, (*chip, mc), sibling) for j, chip in enumerate(chips)]
        for j, chip in enumerate(chips):
            copy(1 + j, (*chip, mc), me).wait_recv()
            passed[j].start()
        copy(0, sibling, me).wait_recv()
        for j, chip in enumerate(chips):
            copy(4 + j, (*chip, 1 - mc), me).wait_recv()
        for cp in first + passed:
            cp.wait_send()
        mine.wait()

    return pl.pallas_call(
        body, name=name,
        in_specs=[EXCHANGE_SPEC],
        out_specs=EXCHANGE_SPEC,
        out_shape=_exchange_shape(x, True),
        scratch_shapes=list(EXCHANGE_SEMS),
        compiler_params=pltpu.CompilerParams(has_side_effects=True),
    )(x)


EXCHANGE_SPEC = pl.BlockSpec(memory_space=pl.ANY)
EXCHANGE_SEMS = (pltpu.SemaphoreType.DMA((N_DEV - 1,)), pltpu.SemaphoreType.DMA((N_DEV - 1,)), pltpu.SemaphoreType.DMA)


def _exchange_shape(x, broadcast):
    return jax.ShapeDtypeStruct((N_DEV,) + tuple(x.shape if broadcast else x.shape[1:]), x.dtype)


def _exchange_copies(x_ref, out_ref, send_sems, recv_sems, local_sem, broadcast, with_recv=True):
    mx, my, mc = lax.axis_index("x"), lax.axis_index("y"), lax.axis_index("c")
    me = 4 * mx + 2 * my + mc

    def src(idx):
        return x_ref if broadcast else x_ref.at[idx]

    local = pltpu.make_async_copy(src(me), out_ref.at[me], local_sem)
    pairs = []
    for k in range(1, N_DEV):
        px = (1 - mx) if k & 4 else mx
        py = (1 - my) if k & 2 else my
        pc = (1 - mc) if k & 1 else mc
        peer = 4 * px + 2 * py + pc
        sems = dict(send_sem=send_sems.at[k - 1], recv_sem=recv_sems.at[k - 1], device_id=(px, py, pc), device_id_type=MESH)
        recv = pltpu.make_async_remote_copy(src_ref=src(peer), dst_ref=out_ref.at[peer], **sems) if with_recv else None
        pairs.append((pltpu.make_async_remote_copy(src_ref=src(peer), dst_ref=out_ref.at[me], **sems), recv))
    return local, pairs


def _exchange_start(*refs_and_mode):
    local, pairs = _exchange_copies(*refs_and_mode, with_recv=False)
    local.start()
    for send, _ in pairs:
        send.start()


def _exchange_wait(*refs_and_mode):
    local, pairs = _exchange_copies(*refs_and_mode)
    for _, recv in pairs:
        recv.wait_recv()
    for send, _ in pairs:
        send.wait_send()
    local.wait()


def _riding(body, grid, n_in, n_out, ride):
    if ride is None:
        return body
    broadcast = ride[1]

    def wrapped(*refs):
        ins, x_ref = refs[:n_in], refs[n_in]
        outs, out_ref = refs[n_in + 1:n_in + 1 + n_out], refs[n_in + 1 + n_out]
        scratch, sems = refs[n_in + 2 + n_out:-3], refs[-3:]
        step = pl.program_id(0)
        for d in range(1, len(grid)):
            step = step * grid[d] + pl.program_id(d)
        total = 1
        for g in grid:
            total *= g

        @pl.when(step == 0)
        def _():
            _exchange_start(x_ref, out_ref, *sems, broadcast)

        body(*ins, *outs, *scratch)

        @pl.when(step == total - 1)
        def _():
            _exchange_wait(x_ref, out_ref, *sems, broadcast)

    return wrapped


def _ride_call(body, name, grid, in_specs, out_specs, out_shape, scratch_shapes, sem, operands, ride):
    if ride is None:
        return pl.pallas_call(body, name=name, grid=grid, in_specs=in_specs, out_specs=out_specs, out_shape=out_shape,
                              scratch_shapes=scratch_shapes, compiler_params=_cparams(*sem))(*operands)
    return pl.pallas_call(
        _riding(body, grid, len(in_specs), len(out_specs), ride), name=name, grid=grid,
        in_specs=list(in_specs) + [EXCHANGE_SPEC], out_specs=list(out_specs) + [EXCHANGE_SPEC],
        out_shape=list(out_shape) + [_exchange_shape(*ride)],
        scratch_shapes=list(scratch_shapes) + list(EXCHANGE_SEMS),
        compiler_params=_cparams(*(["arbitrary"] * len(grid))),
    )(*operands, ride[0])


def _adamw_math(w, g, m, v):
    m2 = ADAM_B1 * m + (1.0 - ADAM_B1) * g
    v2 = ADAM_B2 * v + (1.0 - ADAM_B2) * (g * g)
    m_hat = m2 / (1.0 - ADAM_B1 ** ADAM_STEP)
    v_hat = v2 / (1.0 - ADAM_B2 ** ADAM_STEP)
    delta = -ADAM_LR * (m_hat / (jnp.sqrt(v_hat) + ADAM_EPS) + ADAM_WD * w)
    return delta, m2, v2


def _sum_parts(parts, name, tr):
    _, rows, cols = parts.shape

    def body(p_ref, g_ref):
        g = p_ref[0].astype(F32)
        for s in range(1, N_DEV):
            g = g + p_ref[s].astype(F32)
        g_ref[...] = g

    return pl.pallas_call(
        body, name=name, grid=(rows // tr,),
        in_specs=[pl.BlockSpec((N_DEV, tr, cols), lambda i: (0, i, 0))],
        out_specs=pl.BlockSpec((tr, cols), lambda i: (i, 0)),
        out_shape=jax.ShapeDtypeStruct((rows, cols), F32),
        compiler_params=_cparams("parallel"),
    )(parts)


ADAM_SPLIT_ELEMS = 400_000


def _adamw_shard(g, w, m, v, name):
    rows, cols = w.shape
    tr = rows // 2 if rows * cols > ADAM_SPLIT_ELEMS else rows

    def body(g_ref, w_ref, m_ref, v_ref, d_ref, m2_ref, v2_ref):
        d_ref[...], m2_ref[...], v2_ref[...] = _adamw_math(w_ref[...], g_ref[...], m_ref[...], v_ref[...])

    blk = pl.BlockSpec((tr, cols), lambda i: (i, 0))
    return pl.pallas_call(
        body, name=name, grid=(rows // tr,),
        in_specs=[blk] * 4, out_specs=[blk] * 3,
        out_shape=[jax.ShapeDtypeStruct((rows, cols), F32)] * 3,
        compiler_params=_cparams("parallel"),
    )(g, w, m, v)


def _adamw_small(parts, w, m, v, name):
    names = list(SMALL_NAMES)

    def body(p_ref, *refs):
        ins, outs = refs[:3 * len(names)], refs[3 * len(names):]
        total = p_ref[0]
        for s in range(1, N_DEV):
            total = total + p_ref[s]
        for i, n in enumerate(names):
            row, off, width = SMALL_POS[n]
            g = total[row:row + 1, off:off + width]
            w_ref, m_ref, v_ref = ins[3 * i:3 * i + 3]
            g_ref, d_ref, m2_ref, v2_ref = outs[4 * i:4 * i + 4]
            g_ref[...] = g
            d_ref[...], m2_ref[...], v2_ref[...] = _adamw_math(w_ref[...], g, m_ref[...], v_ref[...])
        row, off, _ = SMALL_POS["loss"]
        outs[-1][...] = total[row:row + 1, off:off + 1]

    operands = [parts] + [t[n] for n in names for t in (w, m, v)]
    shapes = [jax.ShapeDtypeStruct(w[n].shape, F32) for n in names for _ in range(4)]
    shapes.append(jax.ShapeDtypeStruct((1, 1), F32))
    out = pl.pallas_call(body, name=name, out_shape=shapes)(*operands)
    return {n: tuple(out[4 * i:4 * i + 4]) for i, n in enumerate(names)}, out[-1]


TRANSPOSED = frozenset(("ffn1_w_gate", "ffn1_w_up", "w_in", "w_branch_fox", "w_branch_sb", "ffn2_w_gate", "ffn2_w_up",
                        "w_ple_proj"))
F_PAD_ROWS = C_SQ - FL_REAL_END


def _pack(pieces, group, dtype):
    out = []
    for name in GATHER_GROUPS[group]:
        r = pieces[name].T if name in TRANSPOSED else pieces[name]
        r = r.reshape(-1, D_MODEL).astype(dtype)
        if r.shape[0] != PACK_ROWS[name]:
            r = jnp.pad(r, ((0, PACK_ROWS[name] - r.shape[0]), (0, 0)))
        out.append(r)
    return jnp.concatenate(out, axis=0)


def _real_rows(name):
    return W_IN_ROWS if name == "w_in" else PACK_ROWS[name]


def _gathered(got, name, shape):
    off = GATHER_OFF[name]
    return got[:, off:off + _real_rows(name), :].reshape(shape)


def _w_in_device_rows(d):
    lo, hi = d * W_IN_ROWS, (d + 1) * W_IN_ROWS
    if hi <= FL_REAL_END:
        return [(lo, hi)]
    if lo >= FL_REAL_END:
        return [(lo + F_PAD_ROWS, hi + F_PAD_ROWS)]
    return [(lo, FL_REAL_END), (C_SQ, hi + F_PAD_ROWS)]


def _w_in_t_padded(got):
    t = _gathered(got, "w_in", (IN_REAL, D_MODEL))
    return jnp.concatenate([t[:FL_REAL_END], jnp.zeros((F_PAD_ROWS, D_MODEL), t.dtype), t[FL_REAL_END:]], axis=0)


def _pack_chunks(grads, group):
    out = []
    for name in SCATTER_GROUPS[group]:
        base, _, half = name.partition("#")
        g = grads[base].astype(BF16)
        if base == "w_in":
            lo, hi = W_IN_HALVES[int(half)]
            tail = jnp.zeros((PACK_ROWS[base] - W_IN_ROWS, D_MODEL), BF16)
            c = jnp.stack([jnp.concatenate([g[a:b] for a, b in _w_in_device_rows(d)] + [tail], axis=0)[lo:hi]
                           for d in range(N_DEV)])
        else:
            c = g.reshape(N_DEV, PACK_ROWS[name], D_MODEL)
        out.append(c)
    return out[0] if len(out) == 1 else jnp.concatenate(out, axis=1)


def _shard_grad(summed, name, shape):
    if name == "w_in":
        rows = jnp.concatenate([summed[f"w_in#{i}"] for i in range(len(W_IN_HALVES))], axis=0)[:W_IN_ROWS]
    else:
        rows = summed[name][SCATTER_OFF[name]:SCATTER_OFF[name] + PACK_ROWS[name], :]
    return rows.reshape(shape[1], shape[0]).T if name in TRANSPOSED else rows.reshape(shape)


WEIGHT_NAMES = ['ffn1_norm', 'ffn1_w_gate', 'ffn1_w_up', 'ffn1_w_down', 'mix_norm', 'w_in', 'forget_bias', 'q_norm',
                'k_norm', 'w_branch_fox', 'w_branch_sb', 'w_out', 'ffn2_norm', 'ffn2_w_gate', 'ffn2_w_up',
                'ffn2_w_down', 'ple_norm', 'w_ple_gate', 'w_ple_proj']
SMALL_NAMES = ('ffn1_norm', 'mix_norm', 'ffn2_norm', 'ple_norm', 'q_norm', 'k_norm', 'forget_bias')
SMALL_POS = {'ffn1_norm': (0, 0, D_MODEL), 'mix_norm': (1, 0, D_MODEL), 'ffn2_norm': (2, 0, D_MODEL),
             'ple_norm': (3, 0, D_MODEL), 'q_norm': (4, 0, HEAD_DIM), 'k_norm': (4, HEAD_DIM, HEAD_DIM),
             'forget_bias': (4, 2 * HEAD_DIM, N_HEADS), 'loss': (4, 2 * HEAD_DIM + N_HEADS, 1)}


def _pack_small(vals, loss):
    tail = [vals[n].reshape(1, -1) for n in ('q_norm', 'k_norm', 'forget_bias')] + [loss.reshape(1, 1)]
    tail.append(jnp.zeros((1, D_MODEL - sum(t.shape[1] for t in tail)), F32))
    rows = [vals[n].reshape(1, D_MODEL) for n in SMALL_NAMES[:4]] + [jnp.concatenate(tail, axis=1)]
    rows.append(jnp.zeros((SMALL_ROWS - len(rows), D_MODEL), F32))
    return jnp.concatenate(rows, axis=0)


def _step(x, p, tgt, w):
    row = lambda a: a.reshape(1, -1).astype(F32)
    g_ffn1, g_mix, g_ffn2, g_ple = (row(w[n]) for n in SMALL_NAMES[:4])
    qg = jnp.tile(row(w['q_norm']), (1, N_HEADS))
    kg = jnp.tile(row(w['k_norm']), (1, N_HEADS))
    bias = jnp.pad(row(w['forget_bias']), ((0, 0), (0, LANES - N_HEADS)))
    half = D_FF // 2
    grads = {}

    blk = lambda n: GATHER_OFF[n] // FFN_SHARD
    ffn1 = tuple(blk(n) for n in ("ffn1_w_gate", "ffn1_w_up", "ffn1_w_down"))
    ffn2 = tuple(blk(n) for n in ("ffn2_w_gate", "ffn2_w_up", "ffn2_w_down"))
    got0 = _gather_two_level(_pack(w, 0, BF16), "gather_ffn1")
    x1, g1, u1, h1, got1 = _ffn_fwd(x, g_ffn1, (got0,) * 3, ffn1, "ffn1_fwd", ride=(_pack(w, 1, BF16), True))
    w_in = _w_in_t_padded(got1)
    wbf = _gathered(got1, "w_branch_fox", (D_MODEL, ATT_W))
    wbs = _gathered(got1, "w_branch_sb", (D_MODEL, ATT_W))
    wo = _gathered(got1, "w_out", (D_MODEL, D_MODEL))
    (hmix, fqr, fkr, fqn, fkn, fv, logf, f_col, f_row, sq, sk, sv, gf, gs) = _mix_fwd(
        x1, g_mix, w_in, bias, qg, kg, "mix_fwd")
    f_wide = jnp.repeat(f_col[:, :N_HEADS], HEAD_DIM, axis=1)
    cap = _fox_qk_cap(w['q_norm'], w['k_norm'])
    y_fox, lse, got2 = _fox_fwd(fqn, fkn, fv, f_wide, f_row, cap, "fox_fwd", ride=(_pack(w, 2, BF16), True))
    y_sb, y_sb32, got3 = _sb_fwd(sq, sk, sv, "sb_fwd", ride=(_pack(w, 3, BF16), True))
    wpg = _gathered(got3, "w_ple_gate", (D_MODEL, D_MODEL))
    wpp = _gathered(got3, "w_ple_proj", (D_MODEL, PLE_DIM))
    ffn2_bufs = (got2, got2, got3)
    x2 = _merge_fwd(x1, y_fox, y_sb, gf, gs, wbf, wbs, wo, "merge_fwd")
    x3, g2, u2, h2, = _ffn_fwd(x2, g_ffn2, ffn2_bufs, ffn2, "ffn2_fwd")
    dx3, n_ple, ds_ple, dpp, dg_ple, loss = _ple_loss(x3, p, tgt, g_ple, wpg, wpp, "ple_loss")

    grads['w_ple_gate'] = _wgrad(n_ple, ds_ple, "dw_ple_gate", D_MODEL, D_MODEL)
    grads['w_ple_proj'] = _wgrad(dpp, p, "dw_ple_proj", D_MODEL, PLE_DIM)
    dg2, du2, act2, dx2, dg_ffn2 = _ffn_bwd_fused(x2, dx3, g_ffn2, g2, u2, ffn2_bufs, ffn2, "ffn2_bwd")
    grads['ffn2_w_gate'] = _wgrad(dg2, h2, "dw_ffn2_gate", half, D_MODEL)
    grads['ffn2_w_up'] = _wgrad(du2, h2, "dw_ffn2_up", half, D_MODEL)
    grads['ffn2_w_down'] = _wgrad(act2, dx3, "dw_ffn2_down", half, D_MODEL)
    dyf, dys, dgf, dgs, dbf, dbs, merged = _merge_bwd(dx2, y_fox, y_sb, gf, gs, wbf, wbs, wo, "merge_bwd")
    grads['w_branch_fox'] = _wgrad(dbf, y_fox, "dw_branch_fox", D_MODEL, ATT_W)
    grads['w_branch_sb'] = _wgrad(dbs, y_sb, "dw_branch_sb", D_MODEL, ATT_W)
    grads['w_out'] = _wgrad(merged, dx2, "dw_out", D_MODEL, D_MODEL)
    dfqn, dfkn_t, dfv_t, dft, part_rest = _fox_bwd(fqn, fkn, fv, dyf, y_fox, lse, f_wide, f_row, cap, "fox_bwd",
                                               ride=(_pack_chunks(grads, 5), False))
    dsq, dsk_t, dsv_t = _sb_bwd(sq, sk, sv, dys, y_sb32, "sb_bwd")
    s_len = x.shape[0]
    df_col = jnp.pad(dft[:, :2, :].reshape(N_HEADS, s_len).T, ((0, 0), (0, LANES - N_HEADS)))
    dproj, dx1, dg_mix, dqg, dkg, dbias = _mix_bwd(
        x1, dx2, g_mix, w_in, fqr, fkr, dfqn, dfkn_t, qg, kg, dfv_t, df_col, logf, dsq, dsk_t, dsv_t, dgf, dgs,
        "mix_bwd")
    grads['w_in'] = _wgrad(dproj, hmix, "dw_in", IN_PAD // 3, D_MODEL)
    dg1, du1, act1, dx0, dg_ffn1, part_in0 = _ffn_bwd_fused(x, dx1, g_ffn1, g1, u1, (got0,) * 3, ffn1, "ffn1_bwd",
                                                            ride=(_pack_chunks(grads, 3), False))
    grads['ffn1_w_gate'], part_in1 = _wgrad(dg1, h1, "dw_ffn1_gate", half, D_MODEL,
                                            ride=(_pack_chunks(grads, 4), False))
    grads['ffn1_w_up'], part_gate = _wgrad(du1, h1, "dw_ffn1_up", half, D_MODEL, ride=(_pack_chunks(grads, 0), False))
    grads['ffn1_w_down'], part_up = _wgrad(act1, dx1, "dw_ffn1_down", half, D_MODEL,
                                           ride=(_pack_chunks(grads, 1), False))
    part_down = _exchange(_pack_chunks(grads, 2), "scatter_ffn1_down", False)

    fold = lambda a: a.reshape(N_HEADS, HEAD_DIM).sum(axis=0).reshape(1, HEAD_DIM)
    small_g = {'ffn1_norm': dg_ffn1, 'mix_norm': dg_mix, 'ffn2_norm': dg_ffn2, 'ple_norm': dg_ple,
               'q_norm': fold(dqg), 'k_norm': fold(dkg), 'forget_bias': dbias[:, :N_HEADS]}
    return loss[0, 0], dx0, (part_gate, part_up, part_down, part_in0, part_in1, part_rest), small_g


def kernel(x, p, ffn1_norm, ffn1_w_gate, ffn1_w_up, ffn1_w_down, mix_norm, w_in, forget_bias, q_norm, k_norm, w_branch_fox, w_branch_sb, w_out, ffn2_norm, ffn2_w_gate, ffn2_w_up, ffn2_w_down, ple_norm, w_ple_gate, w_ple_proj, loss_target, m_ffn1_norm, m_ffn1_w_gate, m_ffn1_w_up, m_ffn1_w_down, m_mix_norm, m_w_in, m_forget_bias, m_q_norm, m_k_norm, m_w_branch_fox, m_w_branch_sb, m_w_out, m_ffn2_norm, m_ffn2_w_gate, m_ffn2_w_up, m_ffn2_w_down, m_ple_norm, m_w_ple_gate, m_w_ple_proj, v_ffn1_norm, v_ffn1_w_gate, v_ffn1_w_up, v_ffn1_w_down, v_mix_norm, v_w_in, v_forget_bias, v_q_norm, v_k_norm, v_w_branch_fox, v_w_branch_sb, v_w_out, v_ffn2_norm, v_ffn2_w_gate, v_ffn2_w_up, v_ffn2_w_down, v_ple_norm, v_w_ple_gate, v_w_ple_proj):
    args = dict(locals())
    w = {n: args[n][0] for n in WEIGHT_NAMES}
    m = {n: args["m_" + n][0] for n in WEIGHT_NAMES}
    v = {n: args["v_" + n][0] for n in WEIGHT_NAMES}
    loss, dx, parts, small_g = _step(x[0], p[0, 0], loss_target[0], w)

    summed = {}
    for grp, part in enumerate(parts):
        s = _sum_parts(part, f"sum_grads_{grp}", SUM_TILE_ROWS[grp])
        summed.update({n: s for n in SCATTER_GROUPS[grp]})
    big = {}
    for n in WEIGHT_NAMES:
        if n not in SMALL_NAMES:
            g = _shard_grad(summed, n, w[n].shape)
            big[n] = (g,) + tuple(_adamw_shard(g, w[n], m[n], v[n], "adamw_" + n))
    small_parts = _exchange(_pack_small(small_g, loss), "gather_small", True)
    small, total_loss = _adamw_small(small_parts, *({n: args[pre + n] for n in SMALL_NAMES} for pre in ("", "m_", "v_")),
                                     "adamw_small")
    big.update(small)

    outs = [total_loss.reshape(()), dx.reshape(x.shape)]
    for kind in range(4):
        outs += [big[n][kind].reshape(args[n].shape) for n in WEIGHT_NAMES]
    return tuple(outs)
```

```python
import jax
import jax.numpy as jnp
from jax import lax
from jax.experimental import pallas as pl
from jax.experimental.pallas import tpu as pltpu

F32 = jnp.float32
BF16 = jnp.bfloat16

D_MODEL = 1024
D_FF = 2816
N_HEADS = 8
HEAD_DIM = 64
ATT_W = N_HEADS * HEAD_DIM
PLE_DIM = 256
EPS = 1e-6
N_DEV = 8
MESH = pl.DeviceIdType.MESH

LANES = 128
V7X_SCOPED_VMEM_BYTES = 56 * 1024 * 1024

C_FQ, C_FK, C_FV, C_FL = 0, 512, 1024, 1536
C_SQ, C_SK, C_SV, C_GF, C_GS = 1792, 2304, 2816, 3328, 4352
IN_PAD = 5376
IN_REAL = 5128
FL_REAL_END = 1544

ADAM_LR = 0.001
ADAM_B1 = 0.9
ADAM_B2 = 0.999
ADAM_EPS = 1e-08
ADAM_WD = 0.01
ADAM_STEP = 10

PACK_ROWS = {"ffn1_w_gate": 352, "ffn1_w_up": 352, "ffn1_w_down": 352, "w_in": 656, "w_branch_fox": 64,
             "w_branch_sb": 64, "w_out": 128, "ffn2_w_gate": 352, "ffn2_w_up": 352, "ffn2_w_down": 352,
             "w_ple_gate": 128, "w_ple_proj": 32}
GATHER_GROUPS = (
    ("ffn1_w_gate", "ffn1_w_up", "ffn1_w_down"),
    ("w_in", "w_branch_fox", "w_branch_sb", "w_out"),
    ("ffn2_w_gate", "ffn2_w_up"),
    ("ffn2_w_down", "w_ple_gate", "w_ple_proj"),
)
W_IN_HALVES = ((0, 336), (336, 656))
PACK_ROWS.update({f"w_in#{i}": hi - lo for i, (lo, hi) in enumerate(W_IN_HALVES)})
SCATTER_GROUPS = (
    ("ffn1_w_gate",), ("ffn1_w_up",), ("ffn1_w_down",),
    ("w_in#0",), ("w_in#1",),
    ("ffn2_w_gate", "ffn2_w_up", "ffn2_w_down", "w_ple_gate", "w_ple_proj", "w_branch_fox", "w_branch_sb", "w_out"),
)
SUM_TILE_ROWS = (352, 352, 352, 336, 320, 368)


def _offsets(groups):
    off = {}
    for grp in groups:
        o = 0
        for n in grp:
            off[n] = o
            o += PACK_ROWS[n]
    return off


GATHER_OFF = _offsets(GATHER_GROUPS)
SCATTER_OFF = _offsets(SCATTER_GROUPS)
W_IN_ROWS = 641

SMALL_ROWS = 8


def _cparams(*sem):
    return pltpu.CompilerParams(dimension_semantics=sem, vmem_limit_bytes=V7X_SCOPED_VMEM_BYTES)


def _dot(a, b):
    return jnp.dot(a, b, preferred_element_type=F32)


def _dot_nt(a, b):
    return lax.dot_general(a, b, (((1,), (1,)), ((), ())), preferred_element_type=F32)


def _dot_tn(a, b):
    return lax.dot_general(a, b, (((0,), (0,)), ((), ())), preferred_element_type=F32)


def _split(x, parts):
    out = []
    r = x
    for _ in range(parts):
        p = r.astype(BF16)
        out.append(p)
        r = r - p.astype(F32)
    return out


def _dot_split(x, m, parts):
    acc = None
    for p in _split(x, parts):
        t = _dot(p, m)
        acc = t if acc is None else acc + t
    return acc


def _dot_split_left(m, x, parts):
    acc = None
    for p in _split(x, parts):
        t = _dot(m, p)
        acc = t if acc is None else acc + t
    return acc


def _rms_rinv(xf):
    return lax.rsqrt(jnp.mean(xf * xf, axis=-1, keepdims=True) + EPS)


def _sigmoid(x):
    return 1.0 / (1.0 + jnp.exp(-x))


def _softplus_neg_abs(z):
    return jnp.log(1.0 + jnp.exp(-jnp.abs(z)))


FFN_SHARD = D_FF // N_DEV
FFN_CHUNK = 4


def _ffn_w_spec(blk, index_map):
    return pl.BlockSpec((FFN_CHUNK, FFN_SHARD, D_MODEL), lambda *g: (index_map(*g), blk, 0))


def _ffn_w(ref):
    return ref[...].reshape(FFN_CHUNK * FFN_SHARD, D_MODEL)


def _ffn_fwd(x, gain, wbufs, blks, name, ride=None):
    s_len = x.shape[0]
    ts = min(512, s_len)
    fc = FFN_CHUNK * FFN_SHARD
    nt, nc = s_len // ts, D_FF // fc

    def body(x_ref, gain_ref, wg_ref, wu_ref, wd_ref, y_ref, g_ref, u_ref, h_ref, acc_scr):
        j = pl.program_id(1)

        @pl.when(j == 0)
        def _():
            xf = x_ref[...]
            h_ref[...] = ((xf * _rms_rinv(xf)) * gain_ref[...]).astype(BF16)
            acc_scr[...] = jnp.zeros_like(acc_scr)

        h = h_ref[...]
        g = _dot_nt(h, _ffn_w(wg_ref))
        u = _dot_nt(h, _ffn_w(wu_ref))
        g_ref[...] = g.astype(BF16)
        u_ref[...] = u.astype(BF16)
        a = (g * _sigmoid(g) * u).astype(BF16)
        acc_scr[...] += _dot(a, _ffn_w(wd_ref))

        @pl.when(j == nc - 1)
        def _():
            y_ref[...] = x_ref[...] + 0.5 * acc_scr[...]

    tok = pl.BlockSpec((ts, D_MODEL), lambda i, j: (i, 0))
    hid = pl.BlockSpec((ts, fc), lambda i, j: (i, j))
    return _ride_call(
        body, name, (nt, nc),
        [tok, pl.BlockSpec((1, D_MODEL), lambda i, j: (0, 0))] + [_ffn_w_spec(b, lambda i, j: j) for b in blks],
        [tok, hid, hid, tok],
        [jax.ShapeDtypeStruct((s_len, D_MODEL), F32), jax.ShapeDtypeStruct((s_len, D_FF), BF16),
         jax.ShapeDtypeStruct((s_len, D_FF), BF16), jax.ShapeDtypeStruct((s_len, D_MODEL), BF16)],
        [pltpu.VMEM((ts, D_MODEL), F32)], ("parallel", "arbitrary"), (x, gain, *wbufs), ride)


def _ffn_bwd_fused(x, dy, gain, g, u, wbufs, blks, name, ride=None):
    s_len = x.shape[0]
    ts = min(512, s_len)
    fc = FFN_CHUNK * FFN_SHARD
    nt = s_len // ts
    assert D_FF == 2 * fc

    def hidden(dy_ref, g_ref, u_ref, wg_ref, wu_ref, wd_ref, dg_ref, du_ref, act_ref):
        da = 0.5 * _dot_nt(dy_ref[...].astype(BF16), _ffn_w(wd_ref))
        gf = g_ref[...].astype(F32)
        uf = u_ref[...].astype(F32)
        sg = _sigmoid(gf)
        silu = gf * sg
        dg = (da * uf * (sg * (1.0 + gf * (1.0 - sg)))).astype(BF16)
        du = (da * silu).astype(BF16)
        dg_ref[...] = dg
        du_ref[...] = du
        act_ref[...] = (0.5 * silu * uf).astype(BF16)
        return _dot(dg, _ffn_w(wg_ref)) + _dot(du, _ffn_w(wu_ref))

    def first(dy_ref, g_ref, u_ref, wg_ref, wu_ref, wd_ref, dg_ref, du_ref, act_ref, dh_ref):
        dh_ref[...] = hidden(dy_ref, g_ref, u_ref, wg_ref, wu_ref, wd_ref, dg_ref, du_ref, act_ref)

    def second(x_ref, dy_ref, gain_ref, g_ref, u_ref, wg_ref, wu_ref, wd_ref, dh0_ref, dg_half, du_half, act_half,
               dg_ref, du_ref, act_ref, dx_ref, dgain_ref):
        i = pl.program_id(0)
        dh = dh0_ref[...] + hidden(dy_ref, g_ref, u_ref, wg_ref, wu_ref, wd_ref, dg_ref, du_ref, act_ref)
        xf = x_ref[...]
        r = _rms_rinv(xf)
        xhat = xf * r
        dgp = jnp.sum(dh * xhat, axis=0, keepdims=True)

        @pl.when(i == 0)
        def _():
            dgain_ref[...] = dgp

        @pl.when(i > 0)
        def _():
            dgain_ref[...] += dgp

        dn = dh * gain_ref[...]
        dx_ref[...] = dy_ref[...] + r * (dn - xhat * jnp.mean(dn * xhat, axis=-1, keepdims=True))

    tok = pl.BlockSpec((ts, D_MODEL), lambda i: (i, 0))
    row = pl.BlockSpec((1, D_MODEL), lambda i: (0, 0))
    hid = lambda c: pl.BlockSpec((ts, fc), lambda i: (i, c))
    wts = lambda c: [pl.BlockSpec((FFN_CHUNK, FFN_SHARD, D_MODEL), lambda i, b=b: (c, b, 0),
                                  pipeline_mode=pl.Buffered(1)) for b in blks]
    hidden_shapes = [jax.ShapeDtypeStruct((s_len, D_FF), BF16)] * 3
    dg, du, act, dh0, *rode = _ride_call(
        first, name + "_a", (nt,), [tok, hid(0), hid(0)] + wts(0), [hid(0)] * 3 + [tok],
        hidden_shapes + [jax.ShapeDtypeStruct((s_len, D_MODEL), F32)], [], ("parallel",),
        (dy, g, u, *wbufs), ride)
    filled = pl.BlockSpec(memory_space=pl.ANY)
    dg, du, act, dx, dgain = pl.pallas_call(
        second, name=name + "_b", grid=(nt,),
        in_specs=[tok, tok, row, hid(1), hid(1)] + wts(1) + [tok, filled, filled, filled],
        out_specs=[hid(1)] * 3 + [tok, row],
        out_shape=hidden_shapes + [jax.ShapeDtypeStruct((s_len, D_MODEL), F32), jax.ShapeDtypeStruct((1, D_MODEL), F32)],
        input_output_aliases={9: 0, 10: 1, 11: 2},
        compiler_params=_cparams("arbitrary"),
    )(x, dy, gain, g, u, *wbufs, dh0, dg, du, act)
    return (dg, du, act, dx, dgain, *rode)


def _wgrad(a, b, name, tk, tn, ride=None):
    s_len, k_dim = a.shape
    n_dim = b.shape[1]
    ts = min(2048, s_len)
    ns = s_len // ts

    def body(a_ref, b_ref, o_ref, acc):
        s = pl.program_id(2)

        @pl.when(s == 0)
        def _():
            acc[...] = jnp.zeros_like(acc)

        acc[...] += _dot_tn(a_ref[...].astype(BF16), b_ref[...].astype(BF16))

        @pl.when(s == ns - 1)
        def _():
            o_ref[...] = acc[...].astype(BF16)

    out = _ride_call(
        body, name, (k_dim // tk, n_dim // tn, ns),
        [pl.BlockSpec((ts, tk), lambda k, n, s: (s, k)), pl.BlockSpec((ts, tn), lambda k, n, s: (s, n))],
        [pl.BlockSpec((tk, tn), lambda k, n, s: (k, n))], [jax.ShapeDtypeStruct((k_dim, n_dim), BF16)],
        [pltpu.VMEM((tk, tn), F32)], ("parallel", "parallel", "arbitrary"), (a, b), ride)
    return out[0] if ride is None else tuple(out)


HEAD_SUM_PARTS = 1


def _head_group_matrix():
    r = lax.broadcasted_iota(jnp.int32, (ATT_W, ATT_W), 0) // HEAD_DIM
    c = lax.broadcasted_iota(jnp.int32, (ATT_W, ATT_W), 1) // HEAD_DIM
    return (r == c).astype(BF16)


def _mix_fwd(x, gain, w_in, bias, qg, kg, name):
    s_len = x.shape[0]
    ts = min(512, s_len)
    nt = s_len // ts
    gmat = _head_group_matrix()

    def body(x_ref, gain_ref, w_ref, bias_ref, qg_ref, kg_ref, gm_ref,
             h_ref, fqr_ref, fkr_ref, fqn_ref, fkn_ref, fv_ref, logf_ref, f_ref, ft_ref,
             sq_ref, sk_ref, sv_ref, gf_ref, gs_ref, carry):
        i = pl.program_id(0)
        xf = x_ref[...]
        h = ((xf * _rms_rinv(xf)) * gain_ref[...]).astype(BF16)
        h_ref[...] = h
        gm = gm_ref[...]

        def proj(lo, n):
            return _dot_nt(h, w_ref[lo:lo + n, :])

        def headnorm(raw, g):
            ms = _dot_split(raw * raw, gm, HEAD_SUM_PARTS) * (1.0 / HEAD_DIM)
            return ((raw * lax.rsqrt(ms + EPS)) * g).astype(BF16)

        fq = proj(C_FQ, ATT_W)
        fqr_ref[...] = fq
        fqn_ref[...] = headnorm(fq, qg_ref[...])
        fk = proj(C_FK, ATT_W)
        fkr_ref[...] = fk
        fkn_ref[...] = headnorm(fk, kg_ref[...])
        fv_ref[...] = proj(C_FV, ATT_W).astype(BF16)
        sq_ref[...] = proj(C_SQ, ATT_W).astype(BF16)
        sk_ref[...] = proj(C_SK, ATT_W).astype(BF16)
        sv_ref[...] = proj(C_SV, ATT_W).astype(BF16)
        gf_ref[...] = proj(C_GF, D_MODEL)
        gs_ref[...] = proj(C_GS, D_MODEL)

        fl = proj(C_FL, LANES) + bias_ref[...]
        lane = lax.broadcasted_iota(jnp.int32, fl.shape, 1)
        logf = jnp.where(lane < N_HEADS, jnp.minimum(fl, 0.0) - _softplus_neg_abs(fl), 0.0)
        logf_ref[...] = logf

        @pl.when(i == 0)
        def _():
            carry[...] = jnp.zeros_like(carry)

        r = lax.broadcasted_iota(jnp.int32, (ts, ts), 0)
        c = lax.broadcasted_iota(jnp.int32, (ts, ts), 1)
        tri = (r >= c).astype(BF16)
        f_tile = _dot_split_left(tri, logf, 3) + carry[...]
        f_ref[...] = f_tile
        ft_ref[...] = f_tile.T[:N_HEADS, :]
        carry[...] = f_tile[ts - 1:ts, :]

    tok = lambda w: pl.BlockSpec((ts, w), lambda i: (i, 0))
    full = lambda a: pl.BlockSpec(a.shape, lambda i: (0, 0), pipeline_mode=pl.Buffered(1))
    f32o = lambda w: jax.ShapeDtypeStruct((s_len, w), F32)
    b16o = lambda w: jax.ShapeDtypeStruct((s_len, w), BF16)
    return _ride_call(
        body, name, (nt,),
        [tok(D_MODEL), full(gain), full(w_in), full(bias), full(qg), full(kg), full(gmat)],
        [
            tok(D_MODEL), tok(ATT_W), tok(ATT_W), tok(ATT_W), tok(ATT_W), tok(ATT_W), tok(LANES), tok(LANES),
            pl.BlockSpec((N_HEADS, ts), lambda i: (0, i)),
            tok(ATT_W), tok(ATT_W), tok(ATT_W), tok(D_MODEL), tok(D_MODEL),
        ],
        [
            b16o(D_MODEL), f32o(ATT_W), f32o(ATT_W), b16o(ATT_W), b16o(ATT_W), b16o(ATT_W), f32o(LANES), f32o(LANES),
            jax.ShapeDtypeStruct((N_HEADS, s_len), F32),
            b16o(ATT_W), b16o(ATT_W), b16o(ATT_W), f32o(D_MODEL), f32o(D_MODEL),
        ],
        [pltpu.VMEM((1, LANES), F32)], ("arbitrary",), (x, gain, w_in, bias, qg, kg, gmat), None)


ATT_T = 256
SB_ROWS = 2
FOX_ROWS = 2
EXP_ZERO = 88.0


def _att_tiling(s_len, rows):
    t = min(ATT_T, s_len)
    nr = min(rows, s_len // t)
    return t, nr, s_len // (t * nr)


def _pair_specs(s_len, tq):
    qblk = pl.BlockSpec((tq, LANES), lambda hp, i: (i, hp))
    kvfull = pl.BlockSpec((s_len, LANES), lambda hp, i: (0, hp))
    return qblk, kvfull


def _walk_tiles(i, nr, load, sub, flush, more=None, trips=None):
    base = i * nr
    for kk in range(nr - 1, -1, -1):
        rs = list(range(kk, nr))
        flush(base + kk, sub(rs, load(base + kk), [r == kk for r in rs]))

    done = jnp.int32(0)
    for last in range(nr - 1, -1, -1):
        rs = list(range(last + 1))

        def visit(n, rs=rs):
            kb = base - 1 - n
            flush(kb, sub(rs, load(kb), [False] * len(rs)))

        if trips is not None:
            todo = jnp.maximum(trips(base, last) - done, 0)

            def body(it, carry, start=done, visit=visit):
                visit(start + it)
                return carry

            lax.fori_loop(0, todo, body, jnp.int32(0))
            done = done + todo
        else:
            def step(state, visit=visit, last=last):
                visit(state[0])
                return state[0] + 1, more(last)

            done, _ = lax.while_loop(lambda state: jnp.logical_and(state[0] < base, state[1] > 0), step,
                                     (done, more(last)))


def _stack(parts):
    return parts[0] if len(parts) == 1 else jnp.concatenate(parts, axis=0)


def _stacked_halves(x, lo):
    z = jnp.zeros_like(x)
    return jnp.concatenate([jnp.where(lo, x, z), jnp.where(lo, z, x)], axis=0)


def _fox_qk_cap(q_gain, k_gain):
    cap = (HEAD_DIM ** 0.5) * jnp.max(jnp.abs(q_gain)) * jnp.max(jnp.abs(k_gain))
    return (cap * 1.01 + 1.0).reshape(1).astype(F32)


def _fox_trips(hp, flast_ref, cap_ref, fq, level):
    def trips(base, r):
        gap = [jnp.max(fq[r][j] - level(r, j)) + cap_ref[0] for j in (0, 1)]

        def needed(n):
            kb = jnp.maximum(base - 1 - n, 0)
            return jnp.logical_or(gap[0] - flast_ref[2 * hp, kb] > -EXP_ZERO,
                                  gap[1] - flast_ref[2 * hp + 1, kb] > -EXP_ZERO)

        return lax.while_loop(lambda n: jnp.logical_and(n < base, needed(n)), lambda n: n + 1, jnp.int32(0))
    return trips


def _fox_fwd(q, k, v, f_wide, f_row, cap, name, ride=None):
    s_len = q.shape[0]
    t, nr, nq = _att_tiling(s_len, FOX_ROWS)

    def body(q_ref, k_ref, v_ref, f_ref, ft_ref, cap_ref, fl_ref, y_ref, lse_ref, m_ref, l_ref, acc_ref):
        hp = pl.program_id(0)
        i = pl.program_id(1)
        lane = lax.broadcasted_iota(jnp.int32, (t, LANES), 1)
        lo = lane < HEAD_DIM
        causal = lax.broadcasted_iota(jnp.int32, (t, t), 0) >= lax.broadcasted_iota(jnp.int32, (t, t), 1)
        rows = [pl.ds(r * t, t) for r in range(nr)]
        slab = lambda r, j: pl.ds((2 * r + j) * t, t)
        qst = [_stacked_halves(q_ref[rw, :] * jnp.asarray(HEAD_DIM ** -0.5, BF16), lo) for rw in rows]
        q_all = _stack(qst)
        fq = [[f_ref[rw, j * HEAD_DIM:j * HEAD_DIM + 1] for j in (0, 1)] for rw in rows]
        m_ref[...] = jnp.full(m_ref.shape, -1e30, F32)
        l_ref[...] = jnp.zeros_like(l_ref)
        acc_ref[...] = jnp.zeros_like(acc_ref)

        def load(kb):
            k0 = pl.multiple_of(kb * t, t)
            frow = [ft_ref[pl.ds(2 * hp + j, 1), pl.ds(k0, t)] for j in (0, 1)]
            return k_ref[pl.ds(k0, t), :], v_ref[pl.ds(k0, t), :], frow

        def sub(rs, tiles, masked):
            kblk, vblk, frow = tiles
            z = _dot_nt(q_all if len(rs) == nr else _stack([qst[r] for r in rs]), kblk)
            slabs = [(r, j) for r in range(len(rs)) for j in (0, 1)]
            ps, alpha = [], []
            for n, (r, j) in enumerate(slabs):
                sl = slab(rs[r], j)
                m_old = m_ref[sl, :]
                s = z[n * t:(n + 1) * t, :] + (fq[rs[r]][j] - frow[j])
                if masked[r]:
                    s = jnp.where(causal, s, -1e30)
                mj = jnp.maximum(m_old, jnp.max(s, axis=1, keepdims=True))
                aj = jnp.exp(m_old - mj)
                p = jnp.exp(s - jnp.tile(mj, (1, t // LANES)))
                m_ref[sl, :] = mj
                l_ref[sl, :] = aj * l_ref[sl, :] + jnp.sum(p, axis=1, keepdims=True)
                alpha.append(aj)
                ps.append(p.astype(BF16))
            pv = _dot(_stack(ps), vblk)
            for n, (r, j) in enumerate(slabs):
                sl = slab(rs[r], j)
                acc_ref[sl, :] = acc_ref[sl, :] * alpha[n] + pv[n * t:(n + 1) * t, :]
            return None

        trips = _fox_trips(hp, fl_ref, cap_ref, fq, lambda r, j: m_ref[slab(r, j), :])
        _walk_tiles(i, nr, load, sub, lambda kb, side: None, trips=trips)
        for r, rw in enumerate(rows):
            l0, l1 = l_ref[slab(r, 0), :], l_ref[slab(r, 1), :]
            y_ref[rw, :] = jnp.where(lo, acc_ref[slab(r, 0), :] / l0, acc_ref[slab(r, 1), :] / l1).astype(BF16)
            lse_ref[0, rw, :] = jnp.where(lo, m_ref[slab(r, 0), :] + jnp.log(l0), m_ref[slab(r, 1), :] + jnp.log(l1))

    state = [pltpu.VMEM((2 * nr * t, LANES), F32)] * 3
    qblk, kvfull = _pair_specs(s_len, t * nr)
    return _ride_call(
        body, name, (N_HEADS // 2, nq),
        [qblk, kvfull, kvfull,
         qblk, pl.BlockSpec((N_HEADS, s_len), lambda hp, i: (0, 0)),
         pl.BlockSpec(memory_space=pltpu.SMEM), pl.BlockSpec(memory_space=pltpu.SMEM)],
        [qblk, pl.BlockSpec((1, t * nr, LANES), lambda hp, i: (hp, i, 0))],
        [jax.ShapeDtypeStruct((s_len, ATT_W), BF16), jax.ShapeDtypeStruct((N_HEADS // 2, s_len, LANES), F32)],
        state, ("parallel", "parallel"), (q, k, v, f_wide, f_row, cap, f_row[:, t - 1::t]), ride)


def _fox_bwd(q, k, v, dy, y, lse, f_wide, f_row, cap, name, ride=None):
    s_len = q.shape[0]
    t, nr, nq = _att_tiling(s_len, FOX_ROWS)

    def body(q_ref, k_ref, v_ref, dy_ref, y_ref, lse_ref, f_ref, ft_ref, cap_ref, fl_ref,
             dq_ref, dkt_ref, dvt_ref, dft_ref, dqs_ref, rsum_ref):
        hp = pl.program_id(0)
        i = pl.program_id(1)

        @pl.when(i == 0)
        def _():
            dkt_ref[...] = jnp.zeros_like(dkt_ref)
            dvt_ref[...] = jnp.zeros_like(dvt_ref)
            dft_ref[...] = jnp.zeros_like(dft_ref)

        dqs_ref[...] = jnp.zeros_like(dqs_ref)
        rsum_ref[...] = jnp.zeros_like(rsum_ref)
        slab = lambda r, j: pl.ds((2 * r + j) * t, t)

        lane = lax.broadcasted_iota(jnp.int32, (t, LANES), 1)
        lo = lane < HEAD_DIM
        causal = lax.broadcasted_iota(jnp.int32, (t, t), 0) >= lax.broadcasted_iota(jnp.int32, (t, t), 1)
        rows = [pl.ds(r * t, t) for r in range(nr)]
        qst, dyst, delta, lse, fq = [], [], [], [], []
        for rw in rows:
            qst.append(_stacked_halves(q_ref[rw, :] * jnp.asarray(HEAD_DIM ** -0.5, BF16), lo))
            dyb = dy_ref[rw, :]
            dyst.append(_stacked_halves(dyb, lo))
            prod = dyb.astype(F32) * y_ref[rw, :].astype(F32)
            delta.append([jnp.sum(jnp.where(lo, prod, 0.0), axis=1, keepdims=True),
                          jnp.sum(jnp.where(lo, 0.0, prod), axis=1, keepdims=True)])
            lse_b = lse_ref[0, rw, :]
            lse.append([lse_b[:, 0:1], lse_b[:, HEAD_DIM:HEAD_DIM + 1]])
            fq.append([f_ref[rw, j * HEAD_DIM:j * HEAD_DIM + 1] for j in (0, 1)])

        q_all, dy_all = _stack(qst), _stack(dyst)
        q_all_t, dy_all_t = q_all.T, dy_all.T

        def load(kb):
            k0 = pl.multiple_of(kb * t, t)
            frow = [ft_ref[pl.ds(2 * hp + j, 1), pl.ds(k0, t)] for j in (0, 1)]
            return k_ref[pl.ds(k0, t), :], v_ref[pl.ds(k0, t), :], frow

        def sub(rs, tiles, masked):
            kblk, vblk, frow = tiles
            qs, dys = (q_all, dy_all) if len(rs) == nr else (_stack([qst[r] for r in rs]), _stack([dyst[r] for r in rs]))
            cols = slice(2 * rs[0] * t, 2 * (rs[-1] + 1) * t)
            z = _dot_nt(qs, kblk)
            dp = _dot_nt(dys, vblk)
            slabs = [(r, j) for r in range(len(rs)) for j in (0, 1)]
            pb, dsb, col = [], [], [None, None]
            for n, (r, j) in enumerate(slabs):
                sl = slice(n * t, (n + 1) * t)
                s = z[sl, :] + (fq[rs[r]][j] - frow[j])
                p = jnp.exp(s - lse[rs[r]][j])
                if masked[r]:
                    p = jnp.where(causal, p, 0.0)
                ds = p * (dp[sl, :] - delta[rs[r]][j])
                c = jnp.sum(ds, axis=0, keepdims=True)
                col[j] = c if col[j] is None else col[j] + c
                rsum_ref[slab(rs[r], j), :] += jnp.sum(ds, axis=1, keepdims=True)
                pb.append(p.astype(BF16))
                dsb.append(ds.astype(BF16))
            p_all, ds_all = _stack(pb), _stack(dsb)
            dqs_ref[pl.ds(2 * rs[0] * t, len(slabs) * t), :] += _dot(ds_all, kblk)
            return _dot(q_all_t[:, cols], ds_all), _dot(dy_all_t[:, cols], p_all), col

        def flush(kb, side):
            k0 = pl.multiple_of(kb * t, t)
            dkt_ref[:, pl.ds(k0, t)] += side[0]
            dvt_ref[:, pl.ds(k0, t)] += side[1]
            for j in (0, 1):
                dft_ref[0, pl.ds(j, 1), pl.ds(k0, t)] -= side[2][j]

        trips = _fox_trips(hp, fl_ref, cap_ref, fq, lambda r, j: lse[r][j])
        _walk_tiles(i, nr, load, sub, flush, trips=trips)
        for r, rw in enumerate(rows):
            dq_ref[rw, :] = jnp.where(lo, dqs_ref[slab(r, 0), :], dqs_ref[slab(r, 1), :]) * (HEAD_DIM ** -0.5)
            rs_t = jnp.where(lo, rsum_ref[slab(r, 0), :], rsum_ref[slab(r, 1), :]).T
            q0 = pl.multiple_of((i * nr + r) * t, t)
            for j in (0, 1):
                dft_ref[0, pl.ds(j, 1), pl.ds(q0, t)] += rs_t[j * HEAD_DIM:j * HEAD_DIM + 1, :]

    state = [pltpu.VMEM((2 * nr * t, LANES), F32), pltpu.VMEM((2 * nr * t, 1), F32)]
    qblk, kvfull = _pair_specs(s_len, t * nr)
    kvfull_t = pl.BlockSpec((LANES, s_len), lambda hp, i: (hp, 0))
    return _ride_call(
        body, name, (N_HEADS // 2, nq),
        [qblk, kvfull, kvfull, qblk, qblk,
         pl.BlockSpec((1, t * nr, LANES), lambda hp, i: (hp, i, 0)),
         qblk, pl.BlockSpec((N_HEADS, s_len), lambda hp, i: (0, 0)),
         pl.BlockSpec(memory_space=pltpu.SMEM), pl.BlockSpec(memory_space=pltpu.SMEM)],
        [qblk, kvfull_t, kvfull_t, pl.BlockSpec((1, 8, s_len), lambda hp, i: (hp, 0, 0))],
        [jax.ShapeDtypeStruct((s_len, ATT_W), F32)] + [jax.ShapeDtypeStruct((ATT_W, s_len), F32)] * 2
        + [jax.ShapeDtypeStruct((N_HEADS // 2, 8, s_len), F32)],
        state, ("arbitrary", "arbitrary"), (q, k, v, dy, y, lse, f_wide, f_row, cap, f_row[:, t - 1::t]), ride)


def _sb_more(c_ref, t):
    def more(r):
        return (jnp.max(c_ref[pl.ds(2 * r * t, 2 * t), :]) > -EXP_ZERO).astype(jnp.int32)
    return more


def _stacked_split_dot(slabs, m, parts):
    split = [_split(x, parts) for x in slabs]
    acc = None
    for p in range(parts):
        d = _dot(_stack([s[p] for s in split]), m)
        acc = d if acc is None else acc + d
    return acc


def _sb_weights(z, c, strict, upper, t):
    logs = []
    for n in range(z.shape[0] // t):
        zn = z[n * t:(n + 1) * t, :]
        sp = _softplus_neg_abs(zn)
        l1m = jnp.minimum(-zn, 0.0) - sp
        if strict[n] is not None:
            l1m = jnp.where(strict[n], l1m, 0.0)
        logs.append((jnp.minimum(zn, 0.0) - sp, l1m))
    suf = _stacked_split_dot([l1m for _, l1m in logs], upper, 1)
    out = []
    for n, (logb, l1m) in enumerate(logs):
        after = c[n] + suf[n * t:(n + 1) * t, :]
        a = jnp.exp(logb + after)
        if strict[n] is not None:
            a = jnp.where(strict[n], a, 0.0)
        out.append((logb, a, after[:, 0:1] + l1m[:, 0:1]))
    return out


def _sb_fwd(q, k, v, name, ride=None):
    s_len = q.shape[0]
    t, nr, nq = _att_tiling(s_len, SB_ROWS)

    def body(q_ref, k_ref, v_ref, y_ref, yf_ref, c_ref, acc_ref):
        i = pl.program_id(1)
        lane = lax.broadcasted_iota(jnp.int32, (t, LANES), 1)
        lo = lane < HEAD_DIM
        ri = lax.broadcasted_iota(jnp.int32, (t, t), 0)
        ci = lax.broadcasted_iota(jnp.int32, (t, t), 1)
        strict = ci < ri
        upper = (ri > ci).astype(BF16)
        rows = [pl.ds(r * t, t) for r in range(nr)]
        slab = lambda r, j: pl.ds((2 * r + j) * t, t)
        qst = [_stacked_halves(q_ref[rw, :] * jnp.asarray(HEAD_DIM ** -0.5, BF16), lo) for rw in rows]
        q_all = _stack(qst)
        c_ref[...] = jnp.zeros_like(c_ref)
        acc_ref[...] = jnp.zeros_like(acc_ref)

        def load(kb):
            k0 = pl.multiple_of(kb * t, t)
            return k_ref[pl.ds(k0, t), :], v_ref[pl.ds(k0, t), :]

        def sub(rs, tiles, masked):
            kblk, vblk = tiles
            z = _dot_nt(q_all if len(rs) == nr else _stack([qst[r] for r in rs]), kblk)
            slabs = [(r, j) for r in range(len(rs)) for j in (0, 1)]
            w = _sb_weights(z, [c_ref[slab(rs[r], j), :] for r, j in slabs],
                            [strict if masked[r] else None for r, j in slabs], upper, t)
            for n, (r, j) in enumerate(slabs):
                c_ref[slab(rs[r], j), :] = w[n][2]
            acc_ref[pl.ds(2 * rs[0] * t, len(slabs) * t), :] += _dot(_stack([a.astype(BF16) for _, a, _ in w]), vblk)
            return None

        _walk_tiles(i, nr, load, sub, lambda kb, side: None, more=_sb_more(c_ref, t))
        for r, rw in enumerate(rows):
            y = jnp.where(lo, acc_ref[slab(r, 0), :], acc_ref[slab(r, 1), :])
            y_ref[rw, :] = y.astype(BF16)
            yf_ref[rw, :] = y

    qblk, kvfull = _pair_specs(s_len, t * nr)
    return _ride_call(
        body, name, (N_HEADS // 2, nq), [qblk, kvfull, kvfull], [qblk, qblk],
        [jax.ShapeDtypeStruct((s_len, ATT_W), BF16), jax.ShapeDtypeStruct((s_len, ATT_W), F32)],
        [pltpu.VMEM((2 * nr * t, 1), F32), pltpu.VMEM((2 * nr * t, LANES), F32)], ("parallel", "parallel"),
        (q, k, v), ride)


def _sb_bwd(q, k, v, dy, yf, name):
    s_len = q.shape[0]
    t, nr, nq = _att_tiling(s_len, SB_ROWS)

    def body(q_ref, k_ref, v_ref, dy_ref, yf_ref, dq_ref, dkt_ref, dvt_ref, c_ref, e_ref, dqs_ref):
        i = pl.program_id(1)

        @pl.when(i == 0)
        def _():
            dkt_ref[...] = jnp.zeros_like(dkt_ref)
            dvt_ref[...] = jnp.zeros_like(dvt_ref)

        c_ref[...] = jnp.zeros_like(c_ref)
        e_ref[...] = jnp.zeros_like(e_ref)
        dqs_ref[...] = jnp.zeros_like(dqs_ref)
        slab = lambda r, j: pl.ds((2 * r + j) * t, t)

        lane = lax.broadcasted_iota(jnp.int32, (t, LANES), 1)
        lo = lane < HEAD_DIM
        ri = lax.broadcasted_iota(jnp.int32, (t, t), 0)
        ci = lax.broadcasted_iota(jnp.int32, (t, t), 1)
        strict = ci < ri
        upper = (ri > ci).astype(BF16)
        upper_incl = (ri >= ci).astype(BF16)
        rows = [pl.ds(r * t, t) for r in range(nr)]
        qst, dyst, delta = [], [], []
        for rw in rows:
            qst.append(_stacked_halves(q_ref[rw, :] * jnp.asarray(HEAD_DIM ** -0.5, BF16), lo))
            dyb = dy_ref[rw, :]
            dyst.append(_stacked_halves(dyb, lo))
            prod = dyb.astype(F32) * yf_ref[rw, :]
            delta.append([jnp.sum(jnp.where(lo, prod, 0.0), axis=1, keepdims=True),
                          jnp.sum(jnp.where(lo, 0.0, prod), axis=1, keepdims=True)])
        q_all, dy_all = _stack(qst), _stack(dyst)
        q_all_t, dy_all_t = q_all.T, dy_all.T

        def load(kb):
            k0 = pl.multiple_of(kb * t, t)
            return k_ref[pl.ds(k0, t), :], v_ref[pl.ds(k0, t), :]

        def sub(rs, tiles, masked):
            kblk, vblk = tiles
            qs, dys = (q_all, dy_all) if len(rs) == nr else (_stack([qst[r] for r in rs]), _stack([dyst[r] for r in rs]))
            cols = slice(2 * rs[0] * t, 2 * (rs[-1] + 1) * t)
            slabs = [(r, j) for r in range(len(rs)) for j in (0, 1)]
            z = _dot_nt(qs, kblk)
            w = _sb_weights(z, [c_ref[slab(rs[r], j), :] for r, j in slabs],
                            [strict if masked[r] else None for r, j in slabs], upper, t)
            da = _dot_nt(dys, vblk)
            ab = [a.astype(BF16) for _, a, _ in w]
            dl = [ab[n].astype(F32) * da[n * t:(n + 1) * t, :] for n in range(len(slabs))]
            tail = _stacked_split_dot(dl, upper_incl, 2)
            dzb = []
            for n, (r, j) in enumerate(slabs):
                sl = slab(rs[r], j)
                tl = tail[n * t:(n + 1) * t, :]
                e = e_ref[sl, :]
                dl1m = (delta[rs[r]][j] - e) - tl
                e_ref[sl, :] = e + tl[:, 0:1]
                c_ref[sl, :] = w[n][2]
                dz = dl[n] - jnp.exp(w[n][0]) * (dl[n] + dl1m)
                if masked[r]:
                    dz = jnp.where(strict, dz, 0.0)
                dzb.append(dz.astype(BF16))
            a_all, dz_all = _stack(ab), _stack(dzb)
            dqs_ref[pl.ds(2 * rs[0] * t, len(slabs) * t), :] += _dot(dz_all, kblk)
            return _dot(q_all_t[:, cols], dz_all), _dot(dy_all_t[:, cols], a_all)

        def flush(kb, side):
            k0 = pl.multiple_of(kb * t, t)
            dkt_ref[:, pl.ds(k0, t)] += side[0]
            dvt_ref[:, pl.ds(k0, t)] += side[1]

        _walk_tiles(i, nr, load, sub, flush, more=_sb_more(c_ref, t))
        for r, rw in enumerate(rows):
            dq_ref[rw, :] = jnp.where(lo, dqs_ref[slab(r, 0), :], dqs_ref[slab(r, 1), :]) * (HEAD_DIM ** -0.5)

    qblk, kvfull = _pair_specs(s_len, t * nr)
    kvfull_t = pl.BlockSpec((LANES, s_len), lambda hp, i: (hp, 0))
    return pl.pallas_call(
        body, name=name, grid=(N_HEADS // 2, nq),
        in_specs=[qblk, kvfull, kvfull, qblk, qblk],
        out_specs=[qblk, kvfull_t, kvfull_t],
        out_shape=[jax.ShapeDtypeStruct((s_len, ATT_W), F32)] + [jax.ShapeDtypeStruct((ATT_W, s_len), F32)] * 2,
        scratch_shapes=[pltpu.VMEM((2 * nr * t, 1), F32), pltpu.VMEM((2 * nr * t, 1), F32),
                        pltpu.VMEM((2 * nr * t, LANES), F32)],
        compiler_params=_cparams("arbitrary", "arbitrary"),
    )(q, k, v, dy, yf)


def _merge_fwd(x, yf, ys, gf, gs, wbf, wbs, wo, name):
    s_len = x.shape[0]
    ts = min(512, s_len)

    def body(x_ref, yf_ref, ys_ref, gf_ref, gs_ref, wbf_ref, wbs_ref, wo_ref, o_ref):
        merged = (_sigmoid(gf_ref[...]) * _dot_nt(yf_ref[...], wbf_ref[...])
                  + _sigmoid(gs_ref[...]) * _dot_nt(ys_ref[...], wbs_ref[...]))
        o_ref[...] = x_ref[...] + _dot(merged.astype(BF16), wo_ref[...])

    tok = lambda w: pl.BlockSpec((ts, w), lambda i: (i, 0))
    full = lambda a: pl.BlockSpec(a.shape, lambda i: (0, 0))
    return pl.pallas_call(
        body, name=name, grid=(s_len // ts,),
        in_specs=[tok(D_MODEL), tok(ATT_W), tok(ATT_W), tok(D_MODEL), tok(D_MODEL), full(wbf), full(wbs), full(wo)],
        out_specs=tok(D_MODEL),
        out_shape=jax.ShapeDtypeStruct((s_len, D_MODEL), F32),
        compiler_params=_cparams("parallel"),
    )(x, yf, ys, gf, gs, wbf, wbs, wo)


def _merge_bwd(dx, yf, ys, gf, gs, wbf, wbs, wo, name):
    s_len = dx.shape[0]
    ts = min(512, s_len)

    def body(dx_ref, yf_ref, ys_ref, gf_ref, gs_ref, wbf_ref, wbs_ref, wo_ref,
             dyf_ref, dys_ref, dgf_ref, dgs_ref, dbf_ref, dbs_ref, mg_ref):
        bf = _dot_nt(yf_ref[...], wbf_ref[...])
        bs = _dot_nt(ys_ref[...], wbs_ref[...])
        sf = _sigmoid(gf_ref[...])
        ss = _sigmoid(gs_ref[...])
        mg_ref[...] = (sf * bf + ss * bs).astype(BF16)
        dm = _dot_nt(dx_ref[...].astype(BF16), wo_ref[...])
        dbf = (dm * sf).astype(BF16)
        dbs = (dm * ss).astype(BF16)
        dbf_ref[...] = dbf
        dbs_ref[...] = dbs
        dgf_ref[...] = (dm * bf * (sf * (1.0 - sf))).astype(BF16)
        dgs_ref[...] = (dm * bs * (ss * (1.0 - ss))).astype(BF16)
        dyf_ref[...] = _dot(dbf, wbf_ref[...]).astype(BF16)
        dys_ref[...] = _dot(dbs, wbs_ref[...]).astype(BF16)

    tok = lambda w: pl.BlockSpec((ts, w), lambda i: (i, 0))
    full = lambda a: pl.BlockSpec(a.shape, lambda i: (0, 0))
    b16o = lambda w: jax.ShapeDtypeStruct((s_len, w), BF16)
    return pl.pallas_call(
        body, name=name, grid=(s_len // ts,),
        in_specs=[tok(D_MODEL), tok(ATT_W), tok(ATT_W), tok(D_MODEL), tok(D_MODEL), full(wbf), full(wbs), full(wo)],
        out_specs=[tok(ATT_W), tok(ATT_W)] + [tok(D_MODEL)] * 5,
        out_shape=[b16o(ATT_W), b16o(ATT_W)] + [b16o(D_MODEL)] * 5,
        compiler_params=_cparams("parallel"),
    )(dx, yf, ys, gf, gs, wbf, wbs, wo)


def _mix_bwd(x, dx_in, gain, w_in, fqr, fkr, dfqn, dfkn_t, qg, kg, dfv_t, df_col, logf, dsq, dsk_t, dsv_t, dgf, dgs,
             name):
    s_len = x.shape[0]
    ts = min(256, s_len)
    nt = s_len // ts
    gmat = _head_group_matrix()

    def body(x_ref, dxi_ref, gain_ref, w_ref, fqr_ref, fkr_ref, dfqn_ref, dfknt_ref, qg_ref, kg_ref, gm_ref,
             dfvt_ref, df_ref, logf_ref, dsq_ref, dskt_ref, dsvt_ref, dgf_ref, dgs_ref,
             dp_ref, dx_ref, dgain_ref, dqg_ref, dkg_ref, dbias_ref, carry):
        i = pl.program_id(0)

        @pl.when(i == 0)
        def _():
            carry[...] = jnp.zeros_like(carry)
            dgain_ref[...] = jnp.zeros_like(dgain_ref)
            dqg_ref[...] = jnp.zeros_like(dqg_ref)
            dkg_ref[...] = jnp.zeros_like(dkg_ref)
            dbias_ref[...] = jnp.zeros_like(dbias_ref)

        gm = gm_ref[...]

        def headnorm_bwd(raw, dout, g, dg_ref):
            ms = _dot_split(raw * raw, gm, HEAD_SUM_PARTS) * (1.0 / HEAD_DIM)
            r = lax.rsqrt(ms + EPS)
            nrm = raw * r
            dg_ref[...] += jnp.sum(dout * nrm, axis=0, keepdims=True)
            dn = dout * g
            mean_h = _dot_split(dn * nrm, gm, HEAD_SUM_PARTS) * (1.0 / HEAD_DIM)
            return r * (dn - nrm * mean_h)

        dp_ref[:, C_FQ:C_FQ + ATT_W] = headnorm_bwd(fqr_ref[...], dfqn_ref[...], qg_ref[...], dqg_ref).astype(BF16)
        dp_ref[:, C_FK:C_FK + ATT_W] = headnorm_bwd(fkr_ref[...], dfknt_ref[...].T, kg_ref[...], dkg_ref).astype(BF16)
        dp_ref[:, C_FV:C_FV + ATT_W] = dfvt_ref[...].T.astype(BF16)
        dp_ref[:, C_SQ:C_SQ + ATT_W] = dsq_ref[...].astype(BF16)
        dp_ref[:, C_SK:C_SK + ATT_W] = dskt_ref[...].T.astype(BF16)
        dp_ref[:, C_SV:C_SV + ATT_W] = dsvt_ref[...].T.astype(BF16)
        dp_ref[:, C_GF:C_GF + D_MODEL] = dgf_ref[...]
        dp_ref[:, C_GS:C_GS + D_MODEL] = dgs_ref[...]

        r_ = lax.broadcasted_iota(jnp.int32, (ts, ts), 0)
        c_ = lax.broadcasted_iota(jnp.int32, (ts, ts), 1)
        rev = (c_ >= r_).astype(BF16)
        dlogf = _dot_split_left(rev, df_ref[...], 3) + carry[...]
        carry[...] = dlogf[0:1, :]
        lane = lax.broadcasted_iota(jnp.int32, (ts, LANES), 1)
        dfl = jnp.where(lane < N_HEADS, dlogf * (1.0 - jnp.exp(logf_ref[...])), 0.0)
        dbias_ref[...] += jnp.sum(dfl, axis=0, keepdims=True)
        dp_ref[:, C_FL:C_FL + LANES] = dfl.astype(BF16)
        dp_ref[:, C_FL + LANES:C_SQ] = jnp.zeros((ts, C_SQ - C_FL - LANES), BF16)

        dh = _dot(dp_ref[...], w_ref[...])
        xf = x_ref[...]
        r = _rms_rinv(xf)
        xhat = xf * r
        dgain_ref[...] += jnp.sum(dh * xhat, axis=0, keepdims=True)
        dn = dh * gain_ref[...]
        dx_ref[...] = dxi_ref[...] + r * (dn - xhat * jnp.mean(dn * xhat, axis=-1, keepdims=True))

    tok = lambda w: pl.BlockSpec((ts, w), lambda i: (nt - 1 - i, 0))
    full = lambda a: pl.BlockSpec(a.shape, lambda i: (0, 0))
    row = lambda w: pl.BlockSpec((1, w), lambda i: (0, 0))
    tok_t = pl.BlockSpec((ATT_W, ts), lambda i: (0, nt - 1 - i))
    return _ride_call(
        body, name, (nt,),
        [tok(D_MODEL), tok(D_MODEL), full(gain), full(w_in), tok(ATT_W), tok(ATT_W), tok(ATT_W), tok_t,
         full(qg), full(kg), full(gmat), tok_t, tok(LANES), tok(LANES), tok(ATT_W), tok_t, tok_t,
         tok(D_MODEL), tok(D_MODEL)],
        [tok(IN_PAD), tok(D_MODEL), row(D_MODEL), row(ATT_W), row(ATT_W), row(LANES)],
        [jax.ShapeDtypeStruct((s_len, IN_PAD), BF16), jax.ShapeDtypeStruct((s_len, D_MODEL), F32),
         jax.ShapeDtypeStruct((1, D_MODEL), F32), jax.ShapeDtypeStruct((1, ATT_W), F32),
         jax.ShapeDtypeStruct((1, ATT_W), F32), jax.ShapeDtypeStruct((1, LANES), F32)],
        [pltpu.VMEM((1, LANES), F32)], ("arbitrary",),
        (x, dx_in, gain, w_in, fqr, fkr, dfqn, dfkn_t, qg, kg, gmat, dfv_t, df_col, logf, dsq, dsk_t, dsv_t, dgf, dgs),
        None)


def _ple_loss(x, p, tgt, gain, wpg, wpp, name):
    s_len = x.shape[0]
    ts = min(512, s_len)

    def body(x_ref, p_ref, t_ref, gain_ref, wpg_ref, wpp_ref, dx_ref, n_ref, ds_ref, dpp_ref, dgain_ref, loss_ref):
        i = pl.program_id(0)

        @pl.when(i == 0)
        def _():
            dgain_ref[...] = jnp.zeros_like(dgain_ref)
            loss_ref[...] = jnp.zeros_like(loss_ref)

        xf = x_ref[...]
        r = _rms_rinv(xf)
        n = xf * r
        hn = (n * gain_ref[...]).astype(BF16)
        n_ref[...] = hn
        sg = _sigmoid(_dot(hn, wpg_ref[...]))
        pp = _dot_nt(p_ref[...].astype(BF16), wpp_ref[...])
        err = (xf + sg * pp) - t_ref[...]
        sq = jnp.sum(jnp.sum(err * err, axis=1, keepdims=True), axis=0, keepdims=True)
        loss_ref[...] += (0.5 / D_MODEL) * sq
        dout = err * (1.0 / D_MODEL)
        dpp_ref[...] = (dout * sg).astype(BF16)
        ds = (dout * pp * (sg * (1.0 - sg))).astype(BF16)
        ds_ref[...] = ds
        dhn = _dot_nt(ds, wpg_ref[...])
        dgain_ref[...] += jnp.sum(dhn * n, axis=0, keepdims=True)
        dn = dhn * gain_ref[...]
        dx_ref[...] = dout + r * (dn - n * jnp.mean(dn * n, axis=-1, keepdims=True))

    tok = lambda w: pl.BlockSpec((ts, w), lambda i: (i, 0))
    full = lambda a: pl.BlockSpec(a.shape, lambda i: (0, 0))
    return pl.pallas_call(
        body, name=name, grid=(s_len // ts,),
        in_specs=[tok(D_MODEL), tok(PLE_DIM), tok(D_MODEL), full(gain), full(wpg), full(wpp)],
        out_specs=[tok(D_MODEL), tok(D_MODEL), tok(D_MODEL), tok(D_MODEL),
                   pl.BlockSpec((1, D_MODEL), lambda i: (0, 0)), pl.BlockSpec((8, LANES), lambda i: (0, 0))],
        out_shape=[jax.ShapeDtypeStruct((s_len, D_MODEL), F32), jax.ShapeDtypeStruct((s_len, D_MODEL), BF16),
                   jax.ShapeDtypeStruct((s_len, D_MODEL), BF16), jax.ShapeDtypeStruct((s_len, D_MODEL), BF16),
                   jax.ShapeDtypeStruct((1, D_MODEL), F32), jax.ShapeDtypeStruct((8, LANES), F32)],
        compiler_params=_cparams("arbitrary"),
    )(x, p, tgt, gain, wpg, wpp)


def _exchange(x, name, broadcast):
    def body(x_ref, out_ref, send_sems, recv_sems, local_sem):
        _exchange_start(x_ref, out_ref, send_sems, recv_sems, local_sem, broadcast)
        _exchange_wait(x_ref, out_ref, send_sems, recv_sems, local_sem, broadcast)

    return pl.pallas_call(
        body, name=name,
        in_specs=[EXCHANGE_SPEC],
        out_specs=EXCHANGE_SPEC,
        out_shape=_exchange_shape(x, broadcast),
        scratch_shapes=list(EXCHANGE_SEMS),
        compiler_params=pltpu.CompilerParams(has_side_effects=True),
    )(x)


def _gather_two_level(x, name):
    def body(x_ref, out_ref, send_sems, recv_sems, local_sem):
        mx, my, mc = lax.axis_index("x"), lax.axis_index("y"), lax.axis_index("c")
        me, sibling = (mx, my, mc), (mx, my, 1 - mc)
        chips = [(1 - mx, my), (mx, 1 - my), (1 - mx, 1 - my)]

        def slot(px, py, pc):
            return out_ref.at[4 * px + 2 * py + pc]

        def copy(k, block, to, src=None):
            return pltpu.make_async_remote_copy(
                src_ref=slot(*block) if src is None else src, dst_ref=slot(*block),
                send_sem=send_sems.at[k], recv_sem=recv_sems.at[k], device_id=to, device_id_type=MESH)

        mine = pltpu.make_async_copy(x_ref, slot(*me), local_sem)
        mine.start()
        first = [copy(0, me, sibling, src=x_ref)]
        first += [copy(1 + j, me, (*chip, mc), src=x_ref) for j, chip in enumerate(chips)]
        for cp in first:
            cp.start()
        passed = [copy(4 + j, (*chip, mc), sibling) for j, chip in enumerate(chips)]
        for j, chip in enumerate(chips):
            copy(1 + j, (*chip, mc), me).wait_recv()
            passed[j].start()
        copy(0, sibling, me).wait_recv()
        for j, chip in enumerate(chips):
            copy(4 + j, (*chip, 1 - mc), me).wait_recv()
        for cp in first + passed:
            cp.wait_send()
        mine.wait()

    return pl.pallas_call(
        body, name=name,
        in_specs=[EXCHANGE_SPEC],
        out_specs=EXCHANGE_SPEC,
        out_shape=_exchange_shape(x, True),
        scratch_shapes=list(EXCHANGE_SEMS),
        compiler_params=pltpu.CompilerParams(has_side_effects=True),
    )(x)


EXCHANGE_SPEC = pl.BlockSpec(memory_space=pl.ANY)
EXCHANGE_SEMS = (pltpu.SemaphoreType.DMA((N_DEV - 1,)), pltpu.SemaphoreType.DMA((N_DEV - 1,)), pltpu.SemaphoreType.DMA)


def _exchange_shape(x, broadcast):
    return jax.ShapeDtypeStruct((N_DEV,) + tuple(x.shape if broadcast else x.shape[1:]), x.dtype)


def _exchange_copies(x_ref, out_ref, send_sems, recv_sems, local_sem, broadcast, with_recv=True):
    mx, my, mc = lax.axis_index("x"), lax.axis_index("y"), lax.axis_index("c")
    me = 4 * mx + 2 * my + mc

    def src(idx):
        return x_ref if broadcast else x_ref.at[idx]

    local = pltpu.make_async_copy(src(me), out_ref.at[me], local_sem)
    pairs = []
    for k in range(1, N_DEV):
        px = (1 - mx) if k & 4 else mx
        py = (1 - my) if k & 2 else my
        pc = (1 - mc) if k & 1 else mc
        peer = 4 * px + 2 * py + pc
        sems = dict(send_sem=send_sems.at[k - 1], recv_sem=recv_sems.at[k - 1], device_id=(px, py, pc), device_id_type=MESH)
        recv = pltpu.make_async_remote_copy(src_ref=src(peer), dst_ref=out_ref.at[peer], **sems) if with_recv else None
        pairs.append((pltpu.make_async_remote_copy(src_ref=src(peer), dst_ref=out_ref.at[me], **sems), recv))
    return local, pairs


def _exchange_start(*refs_and_mode):
    local, pairs = _exchange_copies(*refs_and_mode, with_recv=False)
    local.start()
    for send, _ in pairs:
        send.start()


def _exchange_wait(*refs_and_mode):
    local, pairs = _exchange_copies(*refs_and_mode)
    for _, recv in pairs:
        recv.wait_recv()
    for send, _ in pairs:
        send.wait_send()
    local.wait()


def _riding(body, grid, n_in, n_out, ride):
    if ride is None:
        return body
    broadcast = ride[1]

    def wrapped(*refs):
        ins, x_ref = refs[:n_in], refs[n_in]
        outs, out_ref = refs[n_in + 1:n_in + 1 + n_out], refs[n_in + 1 + n_out]
        scratch, sems = refs[n_in + 2 + n_out:-3], refs[-3:]
        step = pl.program_id(0)
        for d in range(1, len(grid)):
            step = step * grid[d] + pl.program_id(d)
        total = 1
        for g in grid:
            total *= g

        @pl.when(step == 0)
        def _():
            _exchange_start(x_ref, out_ref, *sems, broadcast)

        body(*ins, *outs, *scratch)

        @pl.when(step == total - 1)
        def _():
            _exchange_wait(x_ref, out_ref, *sems, broadcast)

    return wrapped


def _ride_call(body, name, grid, in_specs, out_specs, out_shape, scratch_shapes, sem, operands, ride):
    if ride is None:
        return pl.pallas_call(body, name=name, grid=grid, in_specs=in_specs, out_specs=out_specs, out_shape=out_shape,
                              scratch_shapes=scratch_shapes, compiler_params=_cparams(*sem))(*operands)
    return pl.pallas_call(
        _riding(body, grid, len(in_specs), len(out_specs), ride), name=name, grid=grid,
        in_specs=list(in_specs) + [EXCHANGE_SPEC], out_specs=list(out_specs) + [EXCHANGE_SPEC],
        out_shape=list(out_shape) + [_exchange_shape(*ride)],
        scratch_shapes=list(scratch_shapes) + list(EXCHANGE_SEMS),
        compiler_params=_cparams(*(["arbitrary"] * len(grid))),
    )(*operands, ride[0])


def _adamw_math(w, g, m, v):
    m2 = ADAM_B1 * m + (1.0 - ADAM_B1) * g
    v2 = ADAM_B2 * v + (1.0 - ADAM_B2) * (g * g)
    m_hat = m2 / (1.0 - ADAM_B1 ** ADAM_STEP)
    v_hat = v2 / (1.0 - ADAM_B2 ** ADAM_STEP)
    delta = -ADAM_LR * (m_hat / (jnp.sqrt(v_hat) + ADAM_EPS) + ADAM_WD * w)
    return delta, m2, v2


def _sum_parts(parts, name, tr):
    _, rows, cols = parts.shape

    def body(p_ref, g_ref):
        g = p_ref[0].astype(F32)
        for s in range(1, N_DEV):
            g = g + p_ref[s].astype(F32)
        g_ref[...] = g

    return pl.pallas_call(
        body, name=name, grid=(rows // tr,),
        in_specs=[pl.BlockSpec((N_DEV, tr, cols), lambda i: (0, i, 0))],
        out_specs=pl.BlockSpec((tr, cols), lambda i: (i, 0)),
        out_shape=jax.ShapeDtypeStruct((rows, cols), F32),
        compiler_params=_cparams("parallel"),
    )(parts)


ADAM_SPLIT_ELEMS = 400_000


def _adamw_shard(g, w, m, v, name):
    rows, cols = w.shape
    tr = rows // 2 if rows * cols > ADAM_SPLIT_ELEMS else rows

    def body(g_ref, w_ref, m_ref, v_ref, d_ref, m2_ref, v2_ref):
        d_ref[...], m2_ref[...], v2_ref[...] = _adamw_math(w_ref[...], g_ref[...], m_ref[...], v_ref[...])

    blk = pl.BlockSpec((tr, cols), lambda i: (i, 0))
    return pl.pallas_call(
        body, name=name, grid=(rows // tr,),
        in_specs=[blk] * 4, out_specs=[blk] * 3,
        out_shape=[jax.ShapeDtypeStruct((rows, cols), F32)] * 3,
        compiler_params=_cparams("parallel"),
    )(g, w, m, v)


def _adamw_small(parts, w, m, v, name):
    names = list(SMALL_NAMES)

    def body(p_ref, *refs):
        ins, outs = refs[:3 * len(names)], refs[3 * len(names):]
        total = p_ref[0]
        for s in range(1, N_DEV):
            total = total + p_ref[s]
        for i, n in enumerate(names):
            row, off, width = SMALL_POS[n]
            g = total[row:row + 1, off:off + width]
            w_ref, m_ref, v_ref = ins[3 * i:3 * i + 3]
            g_ref, d_ref, m2_ref, v2_ref = outs[4 * i:4 * i + 4]
            g_ref[...] = g
            d_ref[...], m2_ref[...], v2_ref[...] = _adamw_math(w_ref[...], g, m_ref[...], v_ref[...])
        row, off, _ = SMALL_POS["loss"]
        outs[-1][...] = total[row:row + 1, off:off + 1]

    operands = [parts] + [t[n] for n in names for t in (w, m, v)]
    shapes = [jax.ShapeDtypeStruct(w[n].shape, F32) for n in names for _ in range(4)]
    shapes.append(jax.ShapeDtypeStruct((1, 1), F32))
    out = pl.pallas_call(body, name=name, out_shape=shapes)(*operands)
    return {n: tuple(out[4 * i:4 * i + 4]) for i, n in enumerate(names)}, out[-1]


TRANSPOSED = frozenset(("ffn1_w_gate", "ffn1_w_up", "w_in", "w_branch_fox", "w_branch_sb", "ffn2_w_gate", "ffn2_w_up",
                        "w_ple_proj"))
F_PAD_ROWS = C_SQ - FL_REAL_END


def _pack(pieces, group, dtype):
    out = []
    for name in GATHER_GROUPS[group]:
        r = pieces[name].T if name in TRANSPOSED else pieces[name]
        r = r.reshape(-1, D_MODEL).astype(dtype)
        if r.shape[0] != PACK_ROWS[name]:
            r = jnp.pad(r, ((0, PACK_ROWS[name] - r.shape[0]), (0, 0)))
        out.append(r)
    return jnp.concatenate(out, axis=0)


def _real_rows(name):
    return W_IN_ROWS if name == "w_in" else PACK_ROWS[name]


def _gathered(got, name, shape):
    off = GATHER_OFF[name]
    return got[:, off:off + _real_rows(name), :].reshape(shape)


def _w_in_device_rows(d):
    lo, hi = d * W_IN_ROWS, (d + 1) * W_IN_ROWS
    if hi <= FL_REAL_END:
        return [(lo, hi)]
    if lo >= FL_REAL_END:
        return [(lo + F_PAD_ROWS, hi + F_PAD_ROWS)]
    return [(lo, FL_REAL_END), (C_SQ, hi + F_PAD_ROWS)]


def _w_in_t_padded(got):
    t = _gathered(got, "w_in", (IN_REAL, D_MODEL))
    return jnp.concatenate([t[:FL_REAL_END], jnp.zeros((F_PAD_ROWS, D_MODEL), t.dtype), t[FL_REAL_END:]], axis=0)


def _pack_chunks(grads, group):
    out = []
    for name in SCATTER_GROUPS[group]:
        base, _, half = name.partition("#")
        g = grads[base].astype(BF16)
        if base == "w_in":
            lo, hi = W_IN_HALVES[int(half)]
            tail = jnp.zeros((PACK_ROWS[base] - W_IN_ROWS, D_MODEL), BF16)
            c = jnp.stack([jnp.concatenate([g[a:b] for a, b in _w_in_device_rows(d)] + [tail], axis=0)[lo:hi]
                           for d in range(N_DEV)])
        else:
            c = g.reshape(N_DEV, PACK_ROWS[name], D_MODEL)
        out.append(c)
    return out[0] if len(out) == 1 else jnp.concatenate(out, axis=1)


def _shard_grad(summed, name, shape):
    if name == "w_in":
        rows = jnp.concatenate([summed[f"w_in#{i}"] for i in range(len(W_IN_HALVES))], axis=0)[:W_IN_ROWS]
    else:
        rows = summed[name][SCATTER_OFF[name]:SCATTER_OFF[name] + PACK_ROWS[name], :]
    return rows.reshape(shape[1], shape[0]).T if name in TRANSPOSED else rows.reshape(shape)


WEIGHT_NAMES = ['ffn1_norm', 'ffn1_w_gate', 'ffn1_w_up', 'ffn1_w_down', 'mix_norm', 'w_in', 'forget_bias', 'q_norm',
                'k_norm', 'w_branch_fox', 'w_branch_sb', 'w_out', 'ffn2_norm', 'ffn2_w_gate', 'ffn2_w_up',
                'ffn2_w_down', 'ple_norm', 'w_ple_gate', 'w_ple_proj']
SMALL_NAMES = ('ffn1_norm', 'mix_norm', 'ffn2_norm', 'ple_norm', 'q_norm', 'k_norm', 'forget_bias')
SMALL_POS = {'ffn1_norm': (0, 0, D_MODEL), 'mix_norm': (1, 0, D_MODEL), 'ffn2_norm': (2, 0, D_MODEL),
             'ple_norm': (3, 0, D_MODEL), 'q_norm': (4, 0, HEAD_DIM), 'k_norm': (4, HEAD_DIM, HEAD_DIM),
             'forget_bias': (4, 2 * HEAD_DIM, N_HEADS), 'loss': (4, 2 * HEAD_DIM + N_HEADS, 1)}


def _pack_small(vals, loss):
    tail = [vals[n].reshape(1, -1) for n in ('q_norm', 'k_norm', 'forget_bias')] + [loss.reshape(1, 1)]
    tail.append(jnp.zeros((1, D_MODEL - sum(t.shape[1] for t in tail)), F32))
    rows = [vals[n].reshape(1, D_MODEL) for n in SMALL_NAMES[:4]] + [jnp.concatenate(tail, axis=1)]
    rows.append(jnp.zeros((SMALL_ROWS - len(rows), D_MODEL), F32))
    return jnp.concatenate(rows, axis=0)


def _step(x, p, tgt, w):
    row = lambda a: a.reshape(1, -1).astype(F32)
    g_ffn1, g_mix, g_ffn2, g_ple = (row(w[n]) for n in SMALL_NAMES[:4])
    qg = jnp.tile(row(w['q_norm']), (1, N_HEADS))
    kg = jnp.tile(row(w['k_norm']), (1, N_HEADS))
    bias = jnp.pad(row(w['forget_bias']), ((0, 0), (0, LANES - N_HEADS)))
    half = D_FF // 2
    grads = {}

    blk = lambda n: GATHER_OFF[n] // FFN_SHARD
    ffn1 = tuple(blk(n) for n in ("ffn1_w_gate", "ffn1_w_up", "ffn1_w_down"))
    ffn2 = tuple(blk(n) for n in ("ffn2_w_gate", "ffn2_w_up", "ffn2_w_down"))
    got0 = _gather_two_level(_pack(w, 0, BF16), "gather_ffn1")
    x1, g1, u1, h1, got1 = _ffn_fwd(x, g_ffn1, (got0,) * 3, ffn1, "ffn1_fwd", ride=(_pack(w, 1, BF16), True))
    w_in = _w_in_t_padded(got1)
    wbf = _gathered(got1, "w_branch_fox", (D_MODEL, ATT_W))
    wbs = _gathered(got1, "w_branch_sb", (D_MODEL, ATT_W))
    wo = _gathered(got1, "w_out", (D_MODEL, D_MODEL))
    (hmix, fqr, fkr, fqn, fkn, fv, logf, f_col, f_row, sq, sk, sv, gf, gs) = _mix_fwd(
        x1, g_mix, w_in, bias, qg, kg, "mix_fwd")
    f_wide = jnp.repeat(f_col[:, :N_HEADS], HEAD_DIM, axis=1)
    cap = _fox_qk_cap(w['q_norm'], w['k_norm'])
    y_fox, lse, got2 = _fox_fwd(fqn, fkn, fv, f_wide, f_row, cap, "fox_fwd", ride=(_pack(w, 2, BF16), True))
    y_sb, y_sb32, got3 = _sb_fwd(sq, sk, sv, "sb_fwd", ride=(_pack(w, 3, BF16), True))
    wpg = _gathered(got3, "w_ple_gate", (D_MODEL, D_MODEL))
    wpp = _gathered(got3, "w_ple_proj", (D_MODEL, PLE_DIM))
    ffn2_bufs = (got2, got2, got3)
    x2 = _merge_fwd(x1, y_fox, y_sb, gf, gs, wbf, wbs, wo, "merge_fwd")
    x3, g2, u2, h2, = _ffn_fwd(x2, g_ffn2, ffn2_bufs, ffn2, "ffn2_fwd")
    dx3, n_ple, ds_ple, dpp, dg_ple, loss = _ple_loss(x3, p, tgt, g_ple, wpg, wpp, "ple_loss")

    grads['w_ple_gate'] = _wgrad(n_ple, ds_ple, "dw_ple_gate", D_MODEL, D_MODEL)
    grads['w_ple_proj'] = _wgrad(dpp, p, "dw_ple_proj", D_MODEL, PLE_DIM)
    dg2, du2, act2, dx2, dg_ffn2 = _ffn_bwd_fused(x2, dx3, g_ffn2, g2, u2, ffn2_bufs, ffn2, "ffn2_bwd")
    grads['ffn2_w_gate'] = _wgrad(dg2, h2, "dw_ffn2_gate", half, D_MODEL)
    grads['ffn2_w_up'] = _wgrad(du2, h2, "dw_ffn2_up", half, D_MODEL)
    grads['ffn2_w_down'] = _wgrad(act2, dx3, "dw_ffn2_down", half, D_MODEL)
    dyf, dys, dgf, dgs, dbf, dbs, merged = _merge_bwd(dx2, y_fox, y_sb, gf, gs, wbf, wbs, wo, "merge_bwd")
    grads['w_branch_fox'] = _wgrad(dbf, y_fox, "dw_branch_fox", D_MODEL, ATT_W)
    grads['w_branch_sb'] = _wgrad(dbs, y_sb, "dw_branch_sb", D_MODEL, ATT_W)
    grads['w_out'] = _wgrad(merged, dx2, "dw_out", D_MODEL, D_MODEL)
    dfqn, dfkn_t, dfv_t, dft, part_rest = _fox_bwd(fqn, fkn, fv, dyf, y_fox, lse, f_wide, f_row, cap, "fox_bwd",
                                               ride=(_pack_chunks(grads, 5), False))
    dsq, dsk_t, dsv_t = _sb_bwd(sq, sk, sv, dys, y_sb32, "sb_bwd")
    s_len = x.shape[0]
    df_col = jnp.pad(dft[:, :2, :].reshape(N_HEADS, s_len).T, ((0, 0), (0, LANES - N_HEADS)))
    dproj, dx1, dg_mix, dqg, dkg, dbias = _mix_bwd(
        x1, dx2, g_mix, w_in, fqr, fkr, dfqn, dfkn_t, qg, kg, dfv_t, df_col, logf, dsq, dsk_t, dsv_t, dgf, dgs,
        "mix_bwd")
    grads['w_in'] = _wgrad(dproj, hmix, "dw_in", IN_PAD // 3, D_MODEL)
    dg1, du1, act1, dx0, dg_ffn1, part_in0 = _ffn_bwd_fused(x, dx1, g_ffn1, g1, u1, (got0,) * 3, ffn1, "ffn1_bwd",
                                                            ride=(_pack_chunks(grads, 3), False))
    grads['ffn1_w_gate'], part_in1 = _wgrad(dg1, h1, "dw_ffn1_gate", half, D_MODEL,
                                            ride=(_pack_chunks(grads, 4), False))
    grads['ffn1_w_up'], part_gate = _wgrad(du1, h1, "dw_ffn1_up", half, D_MODEL, ride=(_pack_chunks(grads, 0), False))
    grads['ffn1_w_down'], part_up = _wgrad(act1, dx1, "dw_ffn1_down", half, D_MODEL,
                                           ride=(_pack_chunks(grads, 1), False))
    part_down = _exchange(_pack_chunks(grads, 2), "scatter_ffn1_down", False)

    fold = lambda a: a.reshape(N_HEADS, HEAD_DIM).sum(axis=0).reshape(1, HEAD_DIM)
    small_g = {'ffn1_norm': dg_ffn1, 'mix_norm': dg_mix, 'ffn2_norm': dg_ffn2, 'ple_norm': dg_ple,
               'q_norm': fold(dqg), 'k_norm': fold(dkg), 'forget_bias': dbias[:, :N_HEADS]}
    return loss[0, 0], dx0, (part_gate, part_up, part_down, part_in0, part_in1, part_rest), small_g


def kernel(x, p, ffn1_norm, ffn1_w_gate, ffn1_w_up, ffn1_w_down, mix_norm, w_in, forget_bias, q_norm, k_norm, w_branch_fox, w_branch_sb, w_out, ffn2_norm, ffn2_w_gate, ffn2_w_up, ffn2_w_down, ple_norm, w_ple_gate, w_ple_proj, loss_target, m_ffn1_norm, m_ffn1_w_gate, m_ffn1_w_up, m_ffn1_w_down, m_mix_norm, m_w_in, m_forget_bias, m_q_norm, m_k_norm, m_w_branch_fox, m_w_branch_sb, m_w_out, m_ffn2_norm, m_ffn2_w_gate, m_ffn2_w_up, m_ffn2_w_down, m_ple_norm, m_w_ple_gate, m_w_ple_proj, v_ffn1_norm, v_ffn1_w_gate, v_ffn1_w_up, v_ffn1_w_down, v_mix_norm, v_w_in, v_forget_bias, v_q_norm, v_k_norm, v_w_branch_fox, v_w_branch_sb, v_w_out, v_ffn2_norm, v_ffn2_w_gate, v_ffn2_w_up, v_ffn2_w_down, v_ple_norm, v_w_ple_gate, v_w_ple_proj):
    args = dict(locals())
    w = {n: args[n][0] for n in WEIGHT_NAMES}
    m = {n: args["m_" + n][0] for n in WEIGHT_NAMES}
    v = {n: args["v_" + n][0] for n in WEIGHT_NAMES}
    loss, dx, parts, small_g = _step(x[0], p[0, 0], loss_target[0], w)

    summed = {}
    for grp, part in enumerate(parts):
        s = _sum_parts(part, f"sum_grads_{grp}", SUM_TILE_ROWS[grp])
        summed.update({n: s for n in SCATTER_GROUPS[grp]})
    big = {}
    for n in WEIGHT_NAMES:
        if n not in SMALL_NAMES:
            g = _shard_grad(summed, n, w[n].shape)
            big[n] = (g,) + tuple(_adamw_shard(g, w[n], m[n], v[n], "adamw_" + n))
    small_parts = _exchange(_pack_small(small_g, loss), "gather_small", True)
    small, total_loss = _adamw_small(small_parts, *({n: args[pre + n] for n in SMALL_NAMES} for pre in ("", "m_", "v_")),
                                     "adamw_small")
    big.update(small)

    outs = [total_loss.reshape(()), dx.reshape(x.shape)]
    for kind in range(4):
        outs += [big[n][kind].reshape(args[n].shape) for n in WEIGHT_NAMES]
    return tuple(outs)
```

```python
import jax
import jax.numpy as jnp
from jax import lax
from jax.experimental import pallas as pl
from jax.experimental.pallas import tpu as pltpu

F32 = jnp.float32
BF16 = jnp.bfloat16

D_MODEL = 1024
D_FF = 2816
N_HEADS = 8
HEAD_DIM = 64
ATT_W = N_HEADS * HEAD_DIM
PLE_DIM = 256
EPS = 1e-6
N_DEV = 8
MESH = pl.DeviceIdType.MESH

LANES = 128
V7X_SCOPED_VMEM_BYTES = 56 * 1024 * 1024

C_FQ, C_FK, C_FV, C_FL = 0, 512, 1024, 1536
C_SQ, C_SK, C_SV, C_GF, C_GS = 1792, 2304, 2816, 3328, 4352
IN_PAD = 5376
IN_REAL = 5128
FL_REAL_END = 1544

ADAM_LR = 0.001
ADAM_B1 = 0.9
ADAM_B2 = 0.999
ADAM_EPS = 1e-08
ADAM_WD = 0.01
ADAM_STEP = 10

PACK_ROWS = {"ffn1_w_gate": 352, "ffn1_w_up": 352, "ffn1_w_down": 352, "w_in": 656, "w_branch_fox": 64,
             "w_branch_sb": 64, "w_out": 128, "ffn2_w_gate": 352, "ffn2_w_up": 352, "ffn2_w_down": 352,
             "w_ple_gate": 128, "w_ple_proj": 32}
GATHER_GROUPS = (
    ("ffn1_w_gate", "ffn1_w_up", "ffn1_w_down"),
    ("w_in", "w_branch_fox", "w_branch_sb", "w_out"),
    ("ffn2_w_gate", "ffn2_w_up"),
    ("ffn2_w_down", "w_ple_gate", "w_ple_proj"),
)
W_IN_HALVES = ((0, 336), (336, 656))
PACK_ROWS.update({f"w_in#{i}": hi - lo for i, (lo, hi) in enumerate(W_IN_HALVES)})
SCATTER_GROUPS = (
    ("ffn1_w_gate",), ("ffn1_w_up",), ("ffn1_w_down",),
    ("w_in#0",), ("w_in#1",),
    ("ffn2_w_gate", "ffn2_w_up", "ffn2_w_down", "w_ple_gate", "w_ple_proj", "w_branch_fox", "w_branch_sb", "w_out"),
)
SUM_TILE_ROWS = (352, 352, 352, 336, 320, 368)


def _offsets(groups):
    off = {}
    for grp in groups:
        o = 0
        for n in grp:
            off[n] = o
            o += PACK_ROWS[n]
    return off


GATHER_OFF = _offsets(GATHER_GROUPS)
SCATTER_OFF = _offsets(SCATTER_GROUPS)
W_IN_ROWS = 641

SMALL_ROWS = 8


def _cparams(*sem):
    return pltpu.CompilerParams(dimension_semantics=sem, vmem_limit_bytes=V7X_SCOPED_VMEM_BYTES)


def _dot(a, b):
    return jnp.dot(a, b, preferred_element_type=F32)


def _dot_nt(a, b):
    return lax.dot_general(a, b, (((1,), (1,)), ((), ())), preferred_element_type=F32)


def _dot_tn(a, b):
    return lax.dot_general(a, b, (((0,), (0,)), ((), ())), preferred_element_type=F32)


def _split(x, parts):
    out = []
    r = x
    for _ in range(parts):
        p = r.astype(BF16)
        out.append(p)
        r = r - p.astype(F32)
    return out


def _dot_split(x, m, parts):
    acc = None
    for p in _split(x, parts):
        t = _dot(p, m)
        acc = t if acc is None else acc + t
    return acc


def _dot_split_left(m, x, parts):
    acc = None
    for p in _split(x, parts):
        t = _dot(m, p)
        acc = t if acc is None else acc + t
    return acc


def _rms_rinv(xf):
    return lax.rsqrt(jnp.mean(xf * xf, axis=-1, keepdims=True) + EPS)


def _sigmoid(x):
    return 1.0 / (1.0 + jnp.exp(-x))


def _softplus_neg_abs(z):
    return jnp.log(1.0 + jnp.exp(-jnp.abs(z)))


FFN_SHARD = D_FF // N_DEV
FFN_CHUNK = 4


def _ffn_w_spec(blk, index_map):
    return pl.BlockSpec((FFN_CHUNK, FFN_SHARD, D_MODEL), lambda *g: (index_map(*g), blk, 0))


def _ffn_w(ref):
    return ref[...].reshape(FFN_CHUNK * FFN_SHARD, D_MODEL)


def _ffn_fwd(x, gain, wbufs, blks, name, ride=None):
    s_len = x.shape[0]
    ts = min(512, s_len)
    fc = FFN_CHUNK * FFN_SHARD
    nt, nc = s_len // ts, D_FF // fc

    def body(x_ref, gain_ref, wg_ref, wu_ref, wd_ref, y_ref, g_ref, u_ref, h_ref, acc_scr):
        j = pl.program_id(1)

        @pl.when(j == 0)
        def _():
            xf = x_ref[...]
            h_ref[...] = ((xf * _rms_rinv(xf)) * gain_ref[...]).astype(BF16)
            acc_scr[...] = jnp.zeros_like(acc_scr)

        h = h_ref[...]
        g = _dot_nt(h, _ffn_w(wg_ref))
        u = _dot_nt(h, _ffn_w(wu_ref))
        g_ref[...] = g.astype(BF16)
        u_ref[...] = u.astype(BF16)
        a = (g * _sigmoid(g) * u).astype(BF16)
        acc_scr[...] += _dot(a, _ffn_w(wd_ref))

        @pl.when(j == nc - 1)
        def _():
            y_ref[...] = x_ref[...] + 0.5 * acc_scr[...]

    tok = pl.BlockSpec((ts, D_MODEL), lambda i, j: (i, 0))
    hid = pl.BlockSpec((ts, fc), lambda i, j: (i, j))
    return _ride_call(
        body, name, (nt, nc),
        [tok, pl.BlockSpec((1, D_MODEL), lambda i, j: (0, 0))] + [_ffn_w_spec(b, lambda i, j: j) for b in blks],
        [tok, hid, hid, tok],
        [jax.ShapeDtypeStruct((s_len, D_MODEL), F32), jax.ShapeDtypeStruct((s_len, D_FF), BF16),
         jax.ShapeDtypeStruct((s_len, D_FF), BF16), jax.ShapeDtypeStruct((s_len, D_MODEL), BF16)],
        [pltpu.VMEM((ts, D_MODEL), F32)], ("parallel", "arbitrary"), (x, gain, *wbufs), ride)


def _ffn_bwd_fused(x, dy, gain, g, u, wbufs, blks, name, ride=None):
    s_len = x.shape[0]
    ts = min(512, s_len)
    fc = FFN_CHUNK * FFN_SHARD
    nt = s_len // ts
    assert D_FF == 2 * fc

    def hidden(dy_ref, g_ref, u_ref, wg_ref, wu_ref, wd_ref, dg_ref, du_ref, act_ref):
        da = 0.5 * _dot_nt(dy_ref[...].astype(BF16), _ffn_w(wd_ref))
        gf = g_ref[...].astype(F32)
        uf = u_ref[...].astype(F32)
        sg = _sigmoid(gf)
        silu = gf * sg
        dg = (da * uf * (sg * (1.0 + gf * (1.0 - sg)))).astype(BF16)
        du = (da * silu).astype(BF16)
        dg_ref[...] = dg
        du_ref[...] = du
        act_ref[...] = (0.5 * silu * uf).astype(BF16)
        return _dot(dg, _ffn_w(wg_ref)) + _dot(du, _ffn_w(wu_ref))

    def first(dy_ref, g_ref, u_ref, wg_ref, wu_ref, wd_ref, dg_ref, du_ref, act_ref, dh_ref):
        dh_ref[...] = hidden(dy_ref, g_ref, u_ref, wg_ref, wu_ref, wd_ref, dg_ref, du_ref, act_ref)

    def second(x_ref, dy_ref, gain_ref, g_ref, u_ref, wg_ref, wu_ref, wd_ref, dh0_ref, dg_half, du_half, act_half,
               dg_ref, du_ref, act_ref, dx_ref, dgain_ref):
        i = pl.program_id(0)
        dh = dh0_ref[...] + hidden(dy_ref, g_ref, u_ref, wg_ref, wu_ref, wd_ref, dg_ref, du_ref, act_ref)
        xf = x_ref[...]
        r = _rms_rinv(xf)
        xhat = xf * r
        dgp = jnp.sum(dh * xhat, axis=0, keepdims=True)

        @pl.when(i == 0)
        def _():
            dgain_ref[...] = dgp

        @pl.when(i > 0)
        def _():
            dgain_ref[...] += dgp

        dn = dh * gain_ref[...]
        dx_ref[...] = dy_ref[...] + r * (dn - xhat * jnp.mean(dn * xhat, axis=-1, keepdims=True))

    tok = pl.BlockSpec((ts, D_MODEL), lambda i: (i, 0))
    row = pl.BlockSpec((1, D_MODEL), lambda i: (0, 0))
    hid = lambda c: pl.BlockSpec((ts, fc), lambda i: (i, c))
    wts = lambda c: [pl.BlockSpec((FFN_CHUNK, FFN_SHARD, D_MODEL), lambda i, b=b: (c, b, 0),
                                  pipeline_mode=pl.Buffered(1)) for b in blks]
    hidden_shapes = [jax.ShapeDtypeStruct((s_len, D_FF), BF16)] * 3
    dg, du, act, dh0, *rode = _ride_call(
        first, name + "_a", (nt,), [tok, hid(0), hid(0)] + wts(0), [hid(0)] * 3 + [tok],
        hidden_shapes + [jax.ShapeDtypeStruct((s_len, D_MODEL), F32)], [], ("parallel",),
        (dy, g, u, *wbufs), ride)
    filled = pl.BlockSpec(memory_space=pl.ANY)
    dg, du, act, dx, dgain = pl.pallas_call(
        second, name=name + "_b", grid=(nt,),
        in_specs=[tok, tok, row, hid(1), hid(1)] + wts(1) + [tok, filled, filled, filled],
        out_specs=[hid(1)] * 3 + [tok, row],
        out_shape=hidden_shapes + [jax.ShapeDtypeStruct((s_len, D_MODEL), F32), jax.ShapeDtypeStruct((1, D_MODEL), F32)],
        input_output_aliases={9: 0, 10: 1, 11: 2},
        compiler_params=_cparams("arbitrary"),
    )(x, dy, gain, g, u, *wbufs, dh0, dg, du, act)
    return (dg, du, act, dx, dgain, *rode)


def _wgrad(a, b, name, tk, tn, ride=None):
    s_len, k_dim = a.shape
    n_dim = b.shape[1]
    ts = min(2048, s_len)
    ns = s_len // ts

    def body(a_ref, b_ref, o_ref, acc):
        s = pl.program_id(2)

        @pl.when(s == 0)
        def _():
            acc[...] = jnp.zeros_like(acc)

        acc[...] += _dot_tn(a_ref[...].astype(BF16), b_ref[...].astype(BF16))

        @pl.when(s == ns - 1)
        def _():
            o_ref[...] = acc[...].astype(BF16)

    out = _ride_call(
        body, name, (k_dim // tk, n_dim // tn, ns),
        [pl.BlockSpec((ts, tk), lambda k, n, s: (s, k)), pl.BlockSpec((ts, tn), lambda k, n, s: (s, n))],
        [pl.BlockSpec((tk, tn), lambda k, n, s: (k, n))], [jax.ShapeDtypeStruct((k_dim, n_dim), BF16)],
        [pltpu.VMEM((tk, tn), F32)], ("parallel", "parallel", "arbitrary"), (a, b), ride)
    return out[0] if ride is None else tuple(out)


HEAD_SUM_PARTS = 1


def _head_group_matrix():
    r = lax.broadcasted_iota(jnp.int32, (ATT_W, ATT_W), 0) // HEAD_DIM
    c = lax.broadcasted_iota(jnp.int32, (ATT_W, ATT_W), 1) // HEAD_DIM
    return (r == c).astype(BF16)


def _mix_fwd(x, gain, w_in, bias, qg, kg, name):
    s_len = x.shape[0]
    ts = min(512, s_len)
    nt = s_len // ts
    gmat = _head_group_matrix()

    def body(x_ref, gain_ref, w_ref, bias_ref, qg_ref, kg_ref, gm_ref,
             h_ref, fqr_ref, fkr_ref, fqn_ref, fkn_ref, fv_ref, logf_ref, f_ref, ft_ref,
             sq_ref, sk_ref, sv_ref, gf_ref, gs_ref, carry):
        i = pl.program_id(0)
        xf = x_ref[...]
        h = ((xf * _rms_rinv(xf)) * gain_ref[...]).astype(BF16)
        h_ref[...] = h
        gm = gm_ref[...]

        def proj(lo, n):
            return _dot_nt(h, w_ref[lo:lo + n, :])

        def headnorm(raw, g):
            ms = _dot_split(raw * raw, gm, HEAD_SUM_PARTS) * (1.0 / HEAD_DIM)
            return ((raw * lax.rsqrt(ms + EPS)) * g).astype(BF16)

        fq = proj(C_FQ, ATT_W)
        fqr_ref[...] = fq
        fqn_ref[...] = headnorm(fq, qg_ref[...])
        fk = proj(C_FK, ATT_W)
        fkr_ref[...] = fk
        fkn_ref[...] = headnorm(fk, kg_ref[...])
        fv_ref[...] = proj(C_FV, ATT_W).astype(BF16)
        sq_ref[...] = proj(C_SQ, ATT_W).astype(BF16)
        sk_ref[...] = proj(C_SK, ATT_W).astype(BF16)
        sv_ref[...] = proj(C_SV, ATT_W).astype(BF16)
        gf_ref[...] = proj(C_GF, D_MODEL)
        gs_ref[...] = proj(C_GS, D_MODEL)

        fl = proj(C_FL, LANES) + bias_ref[...]
        lane = lax.broadcasted_iota(jnp.int32, fl.shape, 1)
        logf = jnp.where(lane < N_HEADS, jnp.minimum(fl, 0.0) - _softplus_neg_abs(fl), 0.0)
        logf_ref[...] = logf

        @pl.when(i == 0)
        def _():
            carry[...] = jnp.zeros_like(carry)

        r = lax.broadcasted_iota(jnp.int32, (ts, ts), 0)
        c = lax.broadcasted_iota(jnp.int32, (ts, ts), 1)
        tri = (r >= c).astype(BF16)
        f_tile = _dot_split_left(tri, logf, 3) + carry[...]
        f_ref[...] = f_tile
        ft_ref[...] = f_tile.T[:N_HEADS, :]
        carry[...] = f_tile[ts - 1:ts, :]

    tok = lambda w: pl.BlockSpec((ts, w), lambda i: (i, 0))
    full = lambda a: pl.BlockSpec(a.shape, lambda i: (0, 0), pipeline_mode=pl.Buffered(1))
    f32o = lambda w: jax.ShapeDtypeStruct((s_len, w), F32)
    b16o = lambda w: jax.ShapeDtypeStruct((s_len, w), BF16)
    return _ride_call(
        body, name, (nt,),
        [tok(D_MODEL), full(gain), full(w_in), full(bias), full(qg), full(kg), full(gmat)],
        [
            tok(D_MODEL), tok(ATT_W), tok(ATT_W), tok(ATT_W), tok(ATT_W), tok(ATT_W), tok(LANES), tok(LANES),
            pl.BlockSpec((N_HEADS, ts), lambda i: (0, i)),
            tok(ATT_W), tok(ATT_W), tok(ATT_W), tok(D_MODEL), tok(D_MODEL),
        ],
        [
            b16o(D_MODEL), f32o(ATT_W), f32o(ATT_W), b16o(ATT_W), b16o(ATT_W), b16o(ATT_W), f32o(LANES), f32o(LANES),
            jax.ShapeDtypeStruct((N_HEADS, s_len), F32),
            b16o(ATT_W), b16o(ATT_W), b16o(ATT_W), f32o(D_MODEL), f32o(D_MODEL),
        ],
        [pltpu.VMEM((1, LANES), F32)], ("arbitrary",), (x, gain, w_in, bias, qg, kg, gmat), None)


ATT_T = 256
SB_ROWS = 2
FOX_ROWS = 2
EXP_ZERO = 88.0


def _att_tiling(s_len, rows):
    t = min(ATT_T, s_len)
    nr = min(rows, s_len // t)
    return t, nr, s_len // (t * nr)


def _pair_specs(s_len, tq):
    qblk = pl.BlockSpec((tq, LANES), lambda hp, i: (i, hp))
    kvfull = pl.BlockSpec((s_len, LANES), lambda hp, i: (0, hp))
    return qblk, kvfull


def _walk_tiles(i, nr, load, sub, flush, more=None, trips=None):
    base = i * nr
    for kk in range(nr - 1, -1, -1):
        rs = list(range(kk, nr))
        flush(base + kk, sub(rs, load(base + kk), [r == kk for r in rs]))

    done = jnp.int32(0)
    for last in range(nr - 1, -1, -1):
        rs = list(range(last + 1))

        def visit(n, rs=rs):
            kb = base - 1 - n
            flush(kb, sub(rs, load(kb), [False] * len(rs)))

        if trips is not None:
            todo = jnp.maximum(trips(base, last) - done, 0)

            def body(it, carry, start=done, visit=visit):
                visit(start + it)
                return carry

            lax.fori_loop(0, todo, body, jnp.int32(0))
            done = done + todo
        else:
            def step(state, visit=visit, last=last):
                visit(state[0])
                return state[0] + 1, more(last)

            done, _ = lax.while_loop(lambda state: jnp.logical_and(state[0] < base, state[1] > 0), step,
                                     (done, more(last)))


def _stack(parts):
    return parts[0] if len(parts) == 1 else jnp.concatenate(parts, axis=0)


def _stacked_halves(x, lo):
    z = jnp.zeros_like(x)
    return jnp.concatenate([jnp.where(lo, x, z), jnp.where(lo, z, x)], axis=0)


def _fox_qk_cap(q_gain, k_gain):
    cap = (HEAD_DIM ** 0.5) * jnp.max(jnp.abs(q_gain)) * jnp.max(jnp.abs(k_gain))
    return (cap * 1.01 + 1.0).reshape(1).astype(F32)


def _fox_trips(hp, flast_ref, cap_ref, fq, level):
    def trips(base, r):
        gap = [jnp.max(fq[r][j] - level(r, j)) + cap_ref[0] for j in (0, 1)]

        def needed(n):
            kb = jnp.maximum(base - 1 - n, 0)
            return jnp.logical_or(gap[0] - flast_ref[2 * hp, kb] > -EXP_ZERO,
                                  gap[1] - flast_ref[2 * hp + 1, kb] > -EXP_ZERO)

        return lax.while_loop(lambda n: jnp.logical_and(n < base, needed(n)), lambda n: n + 1, jnp.int32(0))
    return trips


def _fox_fwd(q, k, v, f_wide, f_row, cap, name, ride=None):
    s_len = q.shape[0]
    t, nr, nq = _att_tiling(s_len, FOX_ROWS)

    def body(q_ref, k_ref, v_ref, f_ref, ft_ref, cap_ref, fl_ref, y_ref, lse_ref, m_ref, l_ref, acc_ref):
        hp = pl.program_id(0)
        i = pl.program_id(1)
        lane = lax.broadcasted_iota(jnp.int32, (t, LANES), 1)
        lo = lane < HEAD_DIM
        causal = lax.broadcasted_iota(jnp.int32, (t, t), 0) >= lax.broadcasted_iota(jnp.int32, (t, t), 1)
        rows = [pl.ds(r * t, t) for r in range(nr)]
        slab = lambda r, j: pl.ds((2 * r + j) * t, t)
        qst = [_stacked_halves(q_ref[rw, :] * jnp.asarray(HEAD_DIM ** -0.5, BF16), lo) for rw in rows]
        q_all = _stack(qst)
        fq = [[f_ref[rw, j * HEAD_DIM:j * HEAD_DIM + 1] for j in (0, 1)] for rw in rows]
        m_ref[...] = jnp.full(m_ref.shape, -1e30, F32)
        l_ref[...] = jnp.zeros_like(l_ref)
        acc_ref[...] = jnp.zeros_like(acc_ref)

        def load(kb):
            k0 = pl.multiple_of(kb * t, t)
            frow = [ft_ref[pl.ds(2 * hp + j, 1), pl.ds(k0, t)] for j in (0, 1)]
            return k_ref[pl.ds(k0, t), :], v_ref[pl.ds(k0, t), :], frow

        def sub(rs, tiles, masked):
            kblk, vblk, frow = tiles
            z = _dot_nt(q_all if len(rs) == nr else _stack([qst[r] for r in rs]), kblk)
            slabs = [(r, j) for r in range(len(rs)) for j in (0, 1)]
            ps, alpha = [], []
            for n, (r, j) in enumerate(slabs):
                sl = slab(rs[r], j)
                m_old = m_ref[sl, :]
                s = z[n * t:(n + 1) * t, :] + (fq[rs[r]][j] - frow[j])
                if masked[r]:
                    s = jnp.where(causal, s, -1e30)
                mj = jnp.maximum(m_old, jnp.max(s, axis=1, keepdims=True))
                aj = jnp.exp(m_old - mj)
                p = jnp.exp(s - jnp.tile(mj, (1, t // LANES)))
                m_ref[sl, :] = mj
                l_ref[sl, :] = aj * l_ref[sl, :] + jnp.sum(p, axis=1, keepdims=True)
                alpha.append(aj)
                ps.append(p.astype(BF16))
            pv = _dot(_stack(ps), vblk)
            for n, (r, j) in enumerate(slabs):
                sl = slab(rs[r], j)
                acc_ref[sl, :] = acc_ref[sl, :] * alpha[n] + pv[n * t:(n + 1) * t, :]
            return None

        trips = _fox_trips(hp, fl_ref, cap_ref, fq, lambda r, j: m_ref[slab(r, j), :])
        _walk_tiles(i, nr, load, sub, lambda kb, side: None, trips=trips)
        for r, rw in enumerate(rows):
            l0, l1 = l_ref[slab(r, 0), :], l_ref[slab(r, 1), :]
            y_ref[rw, :] = jnp.where(lo, acc_ref[slab(r, 0), :] / l0, acc_ref[slab(r, 1), :] / l1).astype(BF16)
            lse_ref[0, rw, :] = jnp.where(lo, m_ref[slab(r, 0), :] + jnp.log(l0), m_ref[slab(r, 1), :] + jnp.log(l1))

    state = [pltpu.VMEM((2 * nr * t, LANES), F32)] * 3
    qblk, kvfull = _pair_specs(s_len, t * nr)
    return _ride_call(
        body, name, (N_HEADS // 2, nq),
        [qblk, kvfull, kvfull,
         qblk, pl.BlockSpec((N_HEADS, s_len), lambda hp, i: (0, 0)),
         pl.BlockSpec(memory_space=pltpu.SMEM), pl.BlockSpec(memory_space=pltpu.SMEM)],
        [qblk, pl.BlockSpec((1, t * nr, LANES), lambda hp, i: (hp, i, 0))],
        [jax.ShapeDtypeStruct((s_len, ATT_W), BF16), jax.ShapeDtypeStruct((N_HEADS // 2, s_len, LANES), F32)],
        state, ("parallel", "parallel"), (q, k, v, f_wide, f_row, cap, f_row[:, t - 1::t]), ride)


def _fox_bwd(q, k, v, dy, y, lse, f_wide, f_row, cap, name, ride=None):
    s_len = q.shape[0]
    t, nr, nq = _att_tiling(s_len, FOX_ROWS)

    def body(q_ref, k_ref, v_ref, dy_ref, y_ref, lse_ref, f_ref, ft_ref, cap_ref, fl_ref,
             dq_ref, dkt_ref, dvt_ref, dft_ref, dqs_ref, rsum_ref):
        hp = pl.program_id(0)
        i = pl.program_id(1)

        @pl.when(i == 0)
        def _():
            dkt_ref[...] = jnp.zeros_like(dkt_ref)
            dvt_ref[...] = jnp.zeros_like(dvt_ref)
            dft_ref[...] = jnp.zeros_like(dft_ref)

        dqs_ref[...] = jnp.zeros_like(dqs_ref)
        rsum_ref[...] = jnp.zeros_like(rsum_ref)
        slab = lambda r, j: pl.ds((2 * r + j) * t, t)

        lane = lax.broadcasted_iota(jnp.int32, (t, LANES), 1)
        lo = lane < HEAD_DIM
        causal = lax.broadcasted_iota(jnp.int32, (t, t), 0) >= lax.broadcasted_iota(jnp.int32, (t, t), 1)
        rows = [pl.ds(r * t, t) for r in range(nr)]
        qst, dyst, delta, lse, fq = [], [], [], [], []
        for rw in rows:
            qst.append(_stacked_halves(q_ref[rw, :] * jnp.asarray(HEAD_DIM ** -0.5, BF16), lo))
            dyb = dy_ref[rw, :]
            dyst.append(_stacked_halves(dyb, lo))
            prod = dyb.astype(F32) * y_ref[rw, :].astype(F32)
            delta.append([jnp.sum(jnp.where(lo, prod, 0.0), axis=1, keepdims=True),
                          jnp.sum(jnp.where(lo, 0.0, prod), axis=1, keepdims=True)])
            lse_b = lse_ref[0, rw, :]
            lse.append([lse_b[:, 0:1], lse_b[:, HEAD_DIM:HEAD_DIM + 1]])
            fq.append([f_ref[rw, j * HEAD_DIM:j * HEAD_DIM + 1] for j in (0, 1)])

        q_all, dy_all = _stack(qst), _stack(dyst)
        q_all_t, dy_all_t = q_all.T, dy_all.T

        def load(kb):
            k0 = pl.multiple_of(kb * t, t)
            frow = [ft_ref[pl.ds(2 * hp + j, 1), pl.ds(k0, t)] for j in (0, 1)]
            return k_ref[pl.ds(k0, t), :], v_ref[pl.ds(k0, t), :], frow

        def sub(rs, tiles, masked):
            kblk, vblk, frow = tiles
            qs, dys = (q_all, dy_all) if len(rs) == nr else (_stack([qst[r] for r in rs]), _stack([dyst[r] for r in rs]))
            cols = slice(2 * rs[0] * t, 2 * (rs[-1] + 1) * t)
            z = _dot_nt(qs, kblk)
            dp = _dot_nt(dys, vblk)
            slabs = [(r, j) for r in range(len(rs)) for j in (0, 1)]
            pb, dsb, col = [], [], [None, None]
            for n, (r, j) in enumerate(slabs):
                sl = slice(n * t, (n + 1) * t)
                s = z[sl, :] + (fq[rs[r]][j] - frow[j])
                p = jnp.exp(s - lse[rs[r]][j])
                if masked[r]:
                    p = jnp.where(causal, p, 0.0)
                ds = p * (dp[sl, :] - delta[rs[r]][j])
                c = jnp.sum(ds, axis=0, keepdims=True)
                col[j] = c if col[j] is None else col[j] + c
                rsum_ref[slab(rs[r], j), :] += jnp.sum(ds, axis=1, keepdims=True)
                pb.append(p.astype(BF16))
                dsb.append(ds.astype(BF16))
            p_all, ds_all = _stack(pb), _stack(dsb)
            dqs_ref[pl.ds(2 * rs[0] * t, len(slabs) * t), :] += _dot(ds_all, kblk)
            return _dot(q_all_t[:, cols], ds_all), _dot(dy_all_t[:, cols], p_all), col

        def flush(kb, side):
            k0 = pl.multiple_of(kb * t, t)
            dkt_ref[:, pl.ds(k0, t)] += side[0]
            dvt_ref[:, pl.ds(k0, t)] += side[1]
            for j in (0, 1):
                dft_ref[0, pl.ds(j, 1), pl.ds(k0, t)] -= side[2][j]

        trips = _fox_trips(hp, fl_ref, cap_ref, fq, lambda r, j: lse[r][j])
        _walk_tiles(i, nr, load, sub, flush, trips=trips)
        for r, rw in enumerate(rows):
            dq_ref[rw, :] = jnp.where(lo, dqs_ref[slab(r, 0), :], dqs_ref[slab(r, 1), :]) * (HEAD_DIM ** -0.5)
            rs_t = jnp.where(lo, rsum_ref[slab(r, 0), :], rsum_ref[slab(r, 1), :]).T
            q0 = pl.multiple_of((i * nr + r) * t, t)
            for j in (0, 1):
                dft_ref[0, pl.ds(j, 1), pl.ds(q0, t)] += rs_t[j * HEAD_DIM:j * HEAD_DIM + 1, :]

    state = [pltpu.VMEM((2 * nr * t, LANES), F32), pltpu.VMEM((2 * nr * t, 1), F32)]
    qblk, kvfull = _pair_specs(s_len, t * nr)
    kvfull_t = pl.BlockSpec((LANES, s_len), lambda hp, i: (hp, 0))
    return _ride_call(
        body, name, (N_HEADS // 2, nq),
        [qblk, kvfull, kvfull, qblk, qblk,
         pl.BlockSpec((1, t * nr, LANES), lambda hp, i: (hp, i, 0)),
         qblk, pl.BlockSpec((N_HEADS, s_len), lambda hp, i: (0, 0)),
         pl.BlockSpec(memory_space=pltpu.SMEM), pl.BlockSpec(memory_space=pltpu.SMEM)],
        [qblk, kvfull_t, kvfull_t, pl.BlockSpec((1, 8, s_len), lambda hp, i: (hp, 0, 0))],
        [jax.ShapeDtypeStruct((s_len, ATT_W), F32)] + [jax.ShapeDtypeStruct((ATT_W, s_len), F32)] * 2
        + [jax.ShapeDtypeStruct((N_HEADS // 2, 8, s_len), F32)],
        state, ("arbitrary", "arbitrary"), (q, k, v, dy, y, lse, f_wide, f_row, cap, f_row[:, t - 1::t]), ride)


def _sb_more(c_ref, t):
    def more(r):
        return (jnp.max(c_ref[pl.ds(2 * r * t, 2 * t), :]) > -EXP_ZERO).astype(jnp.int32)
    return more


def _stacked_split_dot(slabs, m, parts):
    split = [_split(x, parts) for x in slabs]
    acc = None
    for p in range(parts):
        d = _dot(_stack([s[p] for s in split]), m)
        acc = d if acc is None else acc + d
    return acc


def _sb_weights(z, c, strict, upper, t):
    logs = []
    for n in range(z.shape[0] // t):
        zn = z[n * t:(n + 1) * t, :]
        sp = _softplus_neg_abs(zn)
        l1m = jnp.minimum(-zn, 0.0) - sp
        if strict[n] is not None:
            l1m = jnp.where(strict[n], l1m, 0.0)
        logs.append((jnp.minimum(zn, 0.0) - sp, l1m))
    suf = _stacked_split_dot([l1m for _, l1m in logs], upper, 1)
    out = []
    for n, (logb, l1m) in enumerate(logs):
        after = c[n] + suf[n * t:(n + 1) * t, :]
        a = jnp.exp(logb + after)
        if strict[n] is not None:
            a = jnp.where(strict[n], a, 0.0)
        out.append((logb, a, after[:, 0:1] + l1m[:, 0:1]))
    return out


def _sb_fwd(q, k, v, name, ride=None):
    s_len = q.shape[0]
    t, nr, nq = _att_tiling(s_len, SB_ROWS)

    def body(q_ref, k_ref, v_ref, y_ref, yf_ref, c_ref, acc_ref):
        i = pl.program_id(1)
        lane = lax.broadcasted_iota(jnp.int32, (t, LANES), 1)
        lo = lane < HEAD_DIM
        ri = lax.broadcasted_iota(jnp.int32, (t, t), 0)
        ci = lax.broadcasted_iota(jnp.int32, (t, t), 1)
        strict = ci < ri
        upper = (ri > ci).astype(BF16)
        rows = [pl.ds(r * t, t) for r in range(nr)]
        slab = lambda r, j: pl.ds((2 * r + j) * t, t)
        qst = [_stacked_halves(q_ref[rw, :] * jnp.asarray(HEAD_DIM ** -0.5, BF16), lo) for rw in rows]
        q_all = _stack(qst)
        c_ref[...] = jnp.zeros_like(c_ref)
        acc_ref[...] = jnp.zeros_like(acc_ref)

        def load(kb):
            k0 = pl.multiple_of(kb * t, t)
            return k_ref[pl.ds(k0, t), :], v_ref[pl.ds(k0, t), :]

        def sub(rs, tiles, masked):
            kblk, vblk = tiles
            z = _dot_nt(q_all if len(rs) == nr else _stack([qst[r] for r in rs]), kblk)
            slabs = [(r, j) for r in range(len(rs)) for j in (0, 1)]
            w = _sb_weights(z, [c_ref[slab(rs[r], j), :] for r, j in slabs],
                            [strict if masked[r] else None for r, j in slabs], upper, t)
            for n, (r, j) in enumerate(slabs):
                c_ref[slab(rs[r], j), :] = w[n][2]
            acc_ref[pl.ds(2 * rs[0] * t, len(slabs) * t), :] += _dot(_stack([a.astype(BF16) for _, a, _ in w]), vblk)
            return None

        _walk_tiles(i, nr, load, sub, lambda kb, side: None, more=_sb_more(c_ref, t))
        for r, rw in enumerate(rows):
            y = jnp.where(lo, acc_ref[slab(r, 0), :], acc_ref[slab(r, 1), :])
            y_ref[rw, :] = y.astype(BF16)
            yf_ref[rw, :] = y

    qblk, kvfull = _pair_specs(s_len, t * nr)
    return _ride_call(
        body, name, (N_HEADS // 2, nq), [qblk, kvfull, kvfull], [qblk, qblk],
        [jax.ShapeDtypeStruct((s_len, ATT_W), BF16), jax.ShapeDtypeStruct((s_len, ATT_W), F32)],
        [pltpu.VMEM((2 * nr * t, 1), F32), pltpu.VMEM((2 * nr * t, LANES), F32)], ("parallel", "parallel"),
        (q, k, v), ride)


def _sb_bwd(q, k, v, dy, yf, name):
    s_len = q.shape[0]
    t, nr, nq = _att_tiling(s_len, SB_ROWS)

    def body(q_ref, k_ref, v_ref, dy_ref, yf_ref, dq_ref, dkt_ref, dvt_ref, c_ref, e_ref, dqs_ref):
        i = pl.program_id(1)

        @pl.when(i == 0)
        def _():
            dkt_ref[...] = jnp.zeros_like(dkt_ref)
            dvt_ref[...] = jnp.zeros_like(dvt_ref)

        c_ref[...] = jnp.zeros_like(c_ref)
        e_ref[...] = jnp.zeros_like(e_ref)
        dqs_ref[...] = jnp.zeros_like(dqs_ref)
        slab = lambda r, j: pl.ds((2 * r + j) * t, t)

        lane = lax.broadcasted_iota(jnp.int32, (t, LANES), 1)
        lo = lane < HEAD_DIM
        ri = lax.broadcasted_iota(jnp.int32, (t, t), 0)
        ci = lax.broadcasted_iota(jnp.int32, (t, t), 1)
        strict = ci < ri
        upper = (ri > ci).astype(BF16)
        upper_incl = (ri >= ci).astype(BF16)
        rows = [pl.ds(r * t, t) for r in range(nr)]
        qst, dyst, delta = [], [], []
        for rw in rows:
            qst.append(_stacked_halves(q_ref[rw, :] * jnp.asarray(HEAD_DIM ** -0.5, BF16), lo))
            dyb = dy_ref[rw, :]
            dyst.append(_stacked_halves(dyb, lo))
            prod = dyb.astype(F32) * yf_ref[rw, :]
            delta.append([jnp.sum(jnp.where(lo, prod, 0.0), axis=1, keepdims=True),
                          jnp.sum(jnp.where(lo, 0.0, prod), axis=1, keepdims=True)])
        q_all, dy_all = _stack(qst), _stack(dyst)
        q_all_t, dy_all_t = q_all.T, dy_all.T

        def load(kb):
            k0 = pl.multiple_of(kb * t, t)
            return k_ref[pl.ds(k0, t), :], v_ref[pl.ds(k0, t), :]

        def sub(rs, tiles, masked):
            kblk, vblk = tiles
            qs, dys = (q_all, dy_all) if len(rs) == nr else (_stack([qst[r] for r in rs]), _stack([dyst[r] for r in rs]))
            cols = slice(2 * rs[0] * t, 2 * (rs[-1] + 1) * t)
            slabs = [(r, j) for r in range(len(rs)) for j in (0, 1)]
            z = _dot_nt(qs, kblk)
            w = _sb_weights(z, [c_ref[slab(rs[r], j), :] for r, j in slabs],
                            [strict if masked[r] else None for r, j in slabs], upper, t)
            da = _dot_nt(dys, vblk)
            ab = [a.astype(BF16) for _, a, _ in w]
            dl = [ab[n].astype(F32) * da[n * t:(n + 1) * t, :] for n in range(len(slabs))]
            tail = _stacked_split_dot(dl, upper_incl, 2)
            dzb = []
            for n, (r, j) in enumerate(slabs):
                sl = slab(rs[r], j)
                tl = tail[n * t:(n + 1) * t, :]
                e = e_ref[sl, :]
                dl1m = (delta[rs[r]][j] - e) - tl
                e_ref[sl, :] = e + tl[:, 0:1]
                c_ref[sl, :] = w[n][2]
                dz = dl[n] - jnp.exp(w[n][0]) * (dl[n] + dl1m)
                if masked[r]:
                    dz = jnp.where(strict, dz, 0.0)
                dzb.append(dz.astype(BF16))
            a_all, dz_all = _stack(ab), _stack(dzb)
            dqs_ref[pl.ds(2 * rs[0] * t, len(slabs) * t), :] += _dot(dz_all, kblk)
            return _dot(q_all_t[:, cols], dz_all), _dot(dy_all_t[:, cols], a_all)

        def flush(kb, side):
            k0 = pl.multiple_of(kb * t, t)
            dkt_ref[:, pl.ds(k0, t)] += side[0]
            dvt_ref[:, pl.ds(k0, t)] += side[1]

        _walk_tiles(i, nr, load, sub, flush, more=_sb_more(c_ref, t))
        for r, rw in enumerate(rows):
            dq_ref[rw, :] = jnp.where(lo, dqs_ref[slab(r, 0), :], dqs_ref[slab(r, 1), :]) * (HEAD_DIM ** -0.5)

    qblk, kvfull = _pair_specs(s_len, t * nr)
    kvfull_t = pl.BlockSpec((LANES, s_len), lambda hp, i: (hp, 0))
    return pl.pallas_call(
        body, name=name, grid=(N_HEADS // 2, nq),
        in_specs=[qblk, kvfull, kvfull, qblk, qblk],
        out_specs=[qblk, kvfull_t, kvfull_t],
        out_shape=[jax.ShapeDtypeStruct((s_len, ATT_W), F32)] + [jax.ShapeDtypeStruct((ATT_W, s_len), F32)] * 2,
        scratch_shapes=[pltpu.VMEM((2 * nr * t, 1), F32), pltpu.VMEM((2 * nr * t, 1), F32),
                        pltpu.VMEM((2 * nr * t, LANES), F32)],
        compiler_params=_cparams("arbitrary", "arbitrary"),
    )(q, k, v, dy, yf)


def _merge_fwd(x, yf, ys, gf, gs, wbf, wbs, wo, name):
    s_len = x.shape[0]
    ts = min(512, s_len)

    def body(x_ref, yf_ref, ys_ref, gf_ref, gs_ref, wbf_ref, wbs_ref, wo_ref, o_ref):
        merged = (_sigmoid(gf_ref[...]) * _dot_nt(yf_ref[...], wbf_ref[...])
                  + _sigmoid(gs_ref[...]) * _dot_nt(ys_ref[...], wbs_ref[...]))
        o_ref[...] = x_ref[...] + _dot(merged.astype(BF16), wo_ref[...])

    tok = lambda w: pl.BlockSpec((ts, w), lambda i: (i, 0))
    full = lambda a: pl.BlockSpec(a.shape, lambda i: (0, 0))
    return pl.pallas_call(
        body, name=name, grid=(s_len // ts,),
        in_specs=[tok(D_MODEL), tok(ATT_W), tok(ATT_W), tok(D_MODEL), tok(D_MODEL), full(wbf), full(wbs), full(wo)],
        out_specs=tok(D_MODEL),
        out_shape=jax.ShapeDtypeStruct((s_len, D_MODEL), F32),
        compiler_params=_cparams("parallel"),
    )(x, yf, ys, gf, gs, wbf, wbs, wo)


def _merge_bwd(dx, yf, ys, gf, gs, wbf, wbs, wo, name):
    s_len = dx.shape[0]
    ts = min(512, s_len)

    def body(dx_ref, yf_ref, ys_ref, gf_ref, gs_ref, wbf_ref, wbs_ref, wo_ref,
             dyf_ref, dys_ref, dgf_ref, dgs_ref, dbf_ref, dbs_ref, mg_ref):
        bf = _dot_nt(yf_ref[...], wbf_ref[...])
        bs = _dot_nt(ys_ref[...], wbs_ref[...])
        sf = _sigmoid(gf_ref[...])
        ss = _sigmoid(gs_ref[...])
        mg_ref[...] = (sf * bf + ss * bs).astype(BF16)
        dm = _dot_nt(dx_ref[...].astype(BF16), wo_ref[...])
        dbf = (dm * sf).astype(BF16)
        dbs = (dm * ss).astype(BF16)
        dbf_ref[...] = dbf
        dbs_ref[...] = dbs
        dgf_ref[...] = (dm * bf * (sf * (1.0 - sf))).astype(BF16)
        dgs_ref[...] = (dm * bs * (ss * (1.0 - ss))).astype(BF16)
        dyf_ref[...] = _dot(dbf, wbf_ref[...]).astype(BF16)
        dys_ref[...] = _dot(dbs, wbs_ref[...]).astype(BF16)

    tok = lambda w: pl.BlockSpec((ts, w), lambda i: (i, 0))
    full = lambda a: pl.BlockSpec(a.shape, lambda i: (0, 0))
    b16o = lambda w: jax.ShapeDtypeStruct((s_len, w), BF16)
    return pl.pallas_call(
        body, name=name, grid=(s_len // ts,),
        in_specs=[tok(D_MODEL), tok(ATT_W), tok(ATT_W), tok(D_MODEL), tok(D_MODEL), full(wbf), full(wbs), full(wo)],
        out_specs=[tok(ATT_W), tok(ATT_W)] + [tok(D_MODEL)] * 5,
        out_shape=[b16o(ATT_W), b16o(ATT_W)] + [b16o(D_MODEL)] * 5,
        compiler_params=_cparams("parallel"),
    )(dx, yf, ys, gf, gs, wbf, wbs, wo)


def _mix_bwd(x, dx_in, gain, w_in, fqr, fkr, dfqn, dfkn_t, qg, kg, dfv_t, df_col, logf, dsq, dsk_t, dsv_t, dgf, dgs,
             name):
    s_len = x.shape[0]
    ts = min(256, s_len)
    nt = s_len // ts
    gmat = _head_group_matrix()

    def body(x_ref, dxi_ref, gain_ref, w_ref, fqr_ref, fkr_ref, dfqn_ref, dfknt_ref, qg_ref, kg_ref, gm_ref,
             dfvt_ref, df_ref, logf_ref, dsq_ref, dskt_ref, dsvt_ref, dgf_ref, dgs_ref,
             dp_ref, dx_ref, dgain_ref, dqg_ref, dkg_ref, dbias_ref, carry):
        i = pl.program_id(0)

        @pl.when(i == 0)
        def _():
            carry[...] = jnp.zeros_like(carry)
            dgain_ref[...] = jnp.zeros_like(dgain_ref)
            dqg_ref[...] = jnp.zeros_like(dqg_ref)
            dkg_ref[...] = jnp.zeros_like(dkg_ref)
            dbias_ref[...] = jnp.zeros_like(dbias_ref)

        gm = gm_ref[...]

        def headnorm_bwd(raw, dout, g, dg_ref):
            ms = _dot_split(raw * raw, gm, HEAD_SUM_PARTS) * (1.0 / HEAD_DIM)
            r = lax.rsqrt(ms + EPS)
            nrm = raw * r
            dg_ref[...] += jnp.sum(dout * nrm, axis=0, keepdims=True)
            dn = dout * g
            mean_h = _dot_split(dn * nrm, gm, HEAD_SUM_PARTS) * (1.0 / HEAD_DIM)
            return r * (dn - nrm * mean_h)

        dp_ref[:, C_FQ:C_FQ + ATT_W] = headnorm_bwd(fqr_ref[...], dfqn_ref[...], qg_ref[...], dqg_ref).astype(BF16)
        dp_ref[:, C_FK:C_FK + ATT_W] = headnorm_bwd(fkr_ref[...], dfknt_ref[...].T, kg_ref[...], dkg_ref).astype(BF16)
        dp_ref[:, C_FV:C_FV + ATT_W] = dfvt_ref[...].T.astype(BF16)
        dp_ref[:, C_SQ:C_SQ + ATT_W] = dsq_ref[...].astype(BF16)
        dp_ref[:, C_SK:C_SK + ATT_W] = dskt_ref[...].T.astype(BF16)
        dp_ref[:, C_SV:C_SV + ATT_W] = dsvt_ref[...].T.astype(BF16)
        dp_ref[:, C_GF:C_GF + D_MODEL] = dgf_ref[...]
        dp_ref[:, C_GS:C_GS + D_MODEL] = dgs_ref[...]

        r_ = lax.broadcasted_iota(jnp.int32, (ts, ts), 0)
        c_ = lax.broadcasted_iota(jnp.int32, (ts, ts), 1)
        rev = (c_ >= r_).astype(BF16)
        dlogf = _dot_split_left(rev, df_ref[...], 3) + carry[...]
        carry[...] = dlogf[0:1, :]
        lane = lax.broadcasted_iota(jnp.int32, (ts, LANES), 1)
        dfl = jnp.where(lane < N_HEADS, dlogf * (1.0 - jnp.exp(logf_ref[...])), 0.0)
        dbias_ref[...] += jnp.sum(dfl, axis=0, keepdims=True)
        dp_ref[:, C_FL:C_FL + LANES] = dfl.astype(BF16)
        dp_ref[:, C_FL + LANES:C_SQ] = jnp.zeros((ts, C_SQ - C_FL - LANES), BF16)

        dh = _dot(dp_ref[...], w_ref[...])
        xf = x_ref[...]
        r = _rms_rinv(xf)
        xhat = xf * r
        dgain_ref[...] += jnp.sum(dh * xhat, axis=0, keepdims=True)
        dn = dh * gain_ref[...]
        dx_ref[...] = dxi_ref[...] + r * (dn - xhat * jnp.mean(dn * xhat, axis=-1, keepdims=True))

    tok = lambda w: pl.BlockSpec((ts, w), lambda i: (nt - 1 - i, 0))
    full = lambda a: pl.BlockSpec(a.shape, lambda i: (0, 0))
    row = lambda w: pl.BlockSpec((1, w), lambda i: (0, 0))
    tok_t = pl.BlockSpec((ATT_W, ts), lambda i: (0, nt - 1 - i))
    return _ride_call(
        body, name, (nt,),
        [tok(D_MODEL), tok(D_MODEL), full(gain), full(w_in), tok(ATT_W), tok(ATT_W), tok(ATT_W), tok_t,
         full(qg), full(kg), full(gmat), tok_t, tok(LANES), tok(LANES), tok(ATT_W), tok_t, tok_t,
         tok(D_MODEL), tok(D_MODEL)],
        [tok(IN_PAD), tok(D_MODEL), row(D_MODEL), row(ATT_W), row(ATT_W), row(LANES)],
        [jax.ShapeDtypeStruct((s_len, IN_PAD), BF16), jax.ShapeDtypeStruct((s_len, D_MODEL), F32),
         jax.ShapeDtypeStruct((1, D_MODEL), F32), jax.ShapeDtypeStruct((1, ATT_W), F32),
         jax.ShapeDtypeStruct((1, ATT_W), F32), jax.ShapeDtypeStruct((1, LANES), F32)],
        [pltpu.VMEM((1, LANES), F32)], ("arbitrary",),
        (x, dx_in, gain, w_in, fqr, fkr, dfqn, dfkn_t, qg, kg, gmat, dfv_t, df_col, logf, dsq, dsk_t, dsv_t, dgf, dgs),
        None)


def _ple_loss(x, p, tgt, gain, wpg, wpp, name):
    s_len = x.shape[0]
    ts = min(512, s_len)

    def body(x_ref, p_ref, t_ref, gain_ref, wpg_ref, wpp_ref, dx_ref, n_ref, ds_ref, dpp_ref, dgain_ref, loss_ref):
        i = pl.program_id(0)

        @pl.when(i == 0)
        def _():
            dgain_ref[...] = jnp.zeros_like(dgain_ref)
            loss_ref[...] = jnp.zeros_like(loss_ref)

        xf = x_ref[...]
        r = _rms_rinv(xf)
        n = xf * r
        hn = (n * gain_ref[...]).astype(BF16)
        n_ref[...] = hn
        sg = _sigmoid(_dot(hn, wpg_ref[...]))
        pp = _dot_nt(p_ref[...].astype(BF16), wpp_ref[...])
        err = (xf + sg * pp) - t_ref[...]
        sq = jnp.sum(jnp.sum(err * err, axis=1, keepdims=True), axis=0, keepdims=True)
        loss_ref[...] += (0.5 / D_MODEL) * sq
        dout = err * (1.0 / D_MODEL)
        dpp_ref[...] = (dout * sg).astype(BF16)
        ds = (dout * pp * (sg * (1.0 - sg))).astype(BF16)
        ds_ref[...] = ds
        dhn = _dot_nt(ds, wpg_ref[...])
        dgain_ref[...] += jnp.sum(dhn * n, axis=0, keepdims=True)
        dn = dhn * gain_ref[...]
        dx_ref[...] = dout + r * (dn - n * jnp.mean(dn * n, axis=-1, keepdims=True))

    tok = lambda w: pl.BlockSpec((ts, w), lambda i: (i, 0))
    full = lambda a: pl.BlockSpec(a.shape, lambda i: (0, 0))
    return pl.pallas_call(
        body, name=name, grid=(s_len // ts,),
        in_specs=[tok(D_MODEL), tok(PLE_DIM), tok(D_MODEL), full(gain), full(wpg), full(wpp)],
        out_specs=[tok(D_MODEL), tok(D_MODEL), tok(D_MODEL), tok(D_MODEL),
                   pl.BlockSpec((1, D_MODEL), lambda i: (0, 0)), pl.BlockSpec((8, LANES), lambda i: (0, 0))],
        out_shape=[jax.ShapeDtypeStruct((s_len, D_MODEL), F32), jax.ShapeDtypeStruct((s_len, D_MODEL), BF16),
                   jax.ShapeDtypeStruct((s_len, D_MODEL), BF16), jax.ShapeDtypeStruct((s_len, D_MODEL), BF16),
                   jax.ShapeDtypeStruct((1, D_MODEL), F32), jax.ShapeDtypeStruct((8, LANES), F32)],
        compiler_params=_cparams("arbitrary"),
    )(x, p, tgt, gain, wpg, wpp)


def _exchange(x, name, broadcast):
    def body(x_ref, out_ref, send_sems, recv_sems, local_sem):
        _exchange_start(x_ref, out_ref, send_sems, recv_sems, local_sem, broadcast)
        _exchange_wait(x_ref, out_ref, send_sems, recv_sems, local_sem, broadcast)

    return pl.pallas_call(
        body, name=name,
        in_specs=[EXCHANGE_SPEC],
        out_specs=EXCHANGE_SPEC,
        out_shape=_exchange_shape(x, broadcast),
        scratch_shapes=list(EXCHANGE_SEMS),
        compiler_params=pltpu.CompilerParams(has_side_effects=True),
    )(x)


def _gather_two_level(x, name):
    def body(x_ref, out_ref, send_sems, recv_sems, local_sem):
        mx, my, mc = lax.axis_index("x"), lax.axis_index("y"), lax.axis_index("c")
        me, sibling = (mx, my, mc), (mx, my, 1 - mc)
        chips = [(1 - mx, my), (mx, 1 - my), (1 - mx, 1 - my)]

        def slot(px, py, pc):
            return out_ref.at[4 * px + 2 * py + pc]

        def copy(k, block, to, src=None):
            return pltpu.make_async_remote_copy(
                src_ref=slot(*block) if src is None else src, dst_ref=slot(*block),
                send_sem=send_sems.at[k], recv_sem=recv_sems.at[k], device_id=to, device_id_type=MESH)

        mine = pltpu.make_async_copy(x_ref, slot(*me), local_sem)
        mine.start()
        first = [copy(0, me, sibling, src=x_ref)]
        first += [copy(1 + j, me, (*chip, mc), src=x_ref) for j, chip in enumerate(chips)]
        for cp in first:
            cp.start()
        passed = [copy(4 + j, (*chip, mc), sibling) for j, chip in enumerate(chips)]
        for j, chip in enumerate(chips):
            copy(1 + j, (*chip, mc), me).wait_recv()
            passed[j].start()
        copy(0, sibling, me).wait_recv()
        for j, chip in enumerate(chips):
            copy(4 + j, (*chip, 1 - mc), me).wait_recv()
        for cp in first + passed:
            cp.wait_send()
        mine.wait()

    return pl.pallas_call(
        body, name=name,
        in_specs=[EXCHANGE_SPEC],
        out_specs=EXCHANGE_SPEC,
        out_shape=_exchange_shape(x, True),
        scratch_shapes=list(EXCHANGE_SEMS),
        compiler_params=pltpu.CompilerParams(has_side_effects=True),
    )(x)


EXCHANGE_SPEC = pl.BlockSpec(memory_space=pl.ANY)
EXCHANGE_SEMS = (pltpu.SemaphoreType.DMA((N_DEV - 1,)), pltpu.SemaphoreType.DMA((N_DEV - 1,)), pltpu.SemaphoreType.DMA)


def _exchange_shape(x, broadcast):
    return jax.ShapeDtypeStruct((N_DEV,) + tuple(x.shape if broadcast else x.shape[1:]), x.dtype)


def _exchange_copies(x_ref, out_ref, send_sems, recv_sems, local_sem, broadcast, with_recv=True):
    mx, my, mc = lax.axis_index("x"), lax.axis_index("y"), lax.axis_index("c")
    me = 4 * mx + 2 * my + mc

    def src(idx):
        return x_ref if broadcast else x_ref.at[idx]

    local = pltpu.make_async_copy(src(me), out_ref.at[me], local_sem)
    pairs = []
    for k in range(1, N_DEV):
        px = (1 - mx) if k & 4 else mx
        py = (1 - my) if k & 2 else my
        pc = (1 - mc) if k & 1 else mc
        peer = 4 * px + 2 * py + pc
        sems = dict(send_sem=send_sems.at[k - 1], recv_sem=recv_sems.at[k - 1], device_id=(px, py, pc), device_id_type=MESH)
        recv = pltpu.make_async_remote_copy(src_ref=src(peer), dst_ref=out_ref.at[peer], **sems) if with_recv else None
        pairs.append((pltpu.make_async_remote_copy(src_ref=src(peer), dst_ref=out_ref.at[me], **sems), recv))
    return local, pairs


def _exchange_start(*refs_and_mode):
    local, pairs = _exchange_copies(*refs_and_mode, with_recv=False)
    local.start()
    for send, _ in pairs:
        send.start()


def _exchange_wait(*refs_and_mode):
    local, pairs = _exchange_copies(*refs_and_mode)
    for _, recv in pairs:
        recv.wait_recv()
    for send, _ in pairs:
        send.wait_send()
    local.wait()


def _riding(body, grid, n_in, n_out, ride):
    if ride is None:
        return body
    broadcast = ride[1]

    def wrapped(*refs):
        ins, x_ref = refs[:n_in], refs[n_in]
        outs, out_ref = refs[n_in + 1:n_in + 1 + n_out], refs[n_in + 1 + n_out]
        scratch, sems = refs[n_in + 2 + n_out:-3], refs[-3:]
        step = pl.program_id(0)
        for d in range(1, len(grid)):
            step = step * grid[d] + pl.program_id(d)
        total = 1
        for g in grid:
            total *= g

        @pl.when(step == 0)
        def _():
            _exchange_start(x_ref, out_ref, *sems, broadcast)

        body(*ins, *outs, *scratch)

        @pl.when(step == total - 1)
        def _():
            _exchange_wait(x_ref, out_ref, *sems, broadcast)

    return wrapped


def _ride_call(body, name, grid, in_specs, out_specs, out_shape, scratch_shapes, sem, operands, ride):
    if ride is None:
        return pl.pallas_call(body, name=name, grid=grid, in_specs=in_specs, out_specs=out_specs, out_shape=out_shape,
                              scratch_shapes=scratch_shapes, compiler_params=_cparams(*sem))(*operands)
    return pl.pallas_call(
        _riding(body, grid, len(in_specs), len(out_specs), ride), name=name, grid=grid,
        in_specs=list(in_specs) + [EXCHANGE_SPEC], out_specs=list(out_specs) + [EXCHANGE_SPEC],
        out_shape=list(out_shape) + [_exchange_shape(*ride)],
        scratch_shapes=list(scratch_shapes) + list(EXCHANGE_SEMS),
        compiler_params=_cparams(*(["arbitrary"] * len(grid))),
    )(*operands, ride[0])


def _adamw_math(w, g, m, v):
    m2 = ADAM_B1 * m + (1.0 - ADAM_B1) * g
    v2 = ADAM_B2 * v + (1.0 - ADAM_B2) * (g * g)
    m_hat = m2 / (1.0 - ADAM_B1 ** ADAM_STEP)
    v_hat = v2 / (1.0 - ADAM_B2 ** ADAM_STEP)
    delta = -ADAM_LR * (m_hat / (jnp.sqrt(v_hat) + ADAM_EPS) + ADAM_WD * w)
    return delta, m2, v2


def _sum_parts(parts, name, tr):
    _, rows, cols = parts.shape

    def body(p_ref, g_ref):
        g = p_ref[0].astype(F32)
        for s in range(1, N_DEV):
            g = g + p_ref[s].astype(F32)
        g_ref[...] = g

    return pl.pallas_call(
        body, name=name, grid=(rows // tr,),
        in_specs=[pl.BlockSpec((N_DEV, tr, cols), lambda i: (0, i, 0))],
        out_specs=pl.BlockSpec((tr, cols), lambda i: (i, 0)),
        out_shape=jax.ShapeDtypeStruct((rows, cols), F32),
        compiler_params=_cparams("parallel"),
    )(parts)


ADAM_SPLIT_ELEMS = 400_000


def _adamw_shard(g, w, m, v, name):
    rows, cols = w.shape
    tr = rows // 2 if rows * cols > ADAM_SPLIT_ELEMS else rows

    def body(g_ref, w_ref, m_ref, v_ref, d_ref, m2_ref, v2_ref):
        d_ref[...], m2_ref[...], v2_ref[...] = _adamw_math(w_ref[...], g_ref[...], m_ref[...], v_ref[...])

    blk = pl.BlockSpec((tr, cols), lambda i: (i, 0))
    return pl.pallas_call(
        body, name=name, grid=(rows // tr,),
        in_specs=[blk] * 4, out_specs=[blk] * 3,
        out_shape=[jax.ShapeDtypeStruct((rows, cols), F32)] * 3,
        compiler_params=_cparams("parallel"),
    )(g, w, m, v)


def _adamw_from_parts(parts, off, w, m, v, name, transposed):
    n_rows = w.size // D_MODEL
    assert off % n_rows == 0 and w.shape == ((D_MODEL, n_rows) if transposed else (n_rows, D_MODEL))

    def body(p_ref, w_ref, m_ref, v_ref, g_ref, d_ref, m2_ref, v2_ref):
        g = p_ref[0].astype(F32)
        for s in range(1, N_DEV):
            g = g + p_ref[s].astype(F32)
        g = g.T if transposed else g
        g_ref[...] = g
        d_ref[...], m2_ref[...], v2_ref[...] = _adamw_math(w_ref[...], g, m_ref[...], v_ref[...])

    blk = pl.BlockSpec(w.shape, lambda i: (0, 0))
    return pl.pallas_call(
        body, name=name, grid=(1,),
        in_specs=[pl.BlockSpec((N_DEV, n_rows, D_MODEL), lambda i: (0, off // n_rows, 0))] + [blk] * 3,
        out_specs=[blk] * 4,
        out_shape=[jax.ShapeDtypeStruct(w.shape, F32)] * 4,
        compiler_params=_cparams("arbitrary"),
    )(parts, w, m, v)


def _adamw_small(parts, w, m, v, name):
    names = list(SMALL_NAMES)

    def body(p_ref, *refs):
        ins, outs = refs[:3 * len(names)], refs[3 * len(names):]
        total = p_ref[0]
        for s in range(1, N_DEV):
            total = total + p_ref[s]
        for i, n in enumerate(names):
            row, off, width = SMALL_POS[n]
            g = total[row:row + 1, off:off + width]
            w_ref, m_ref, v_ref = ins[3 * i:3 * i + 3]
            g_ref, d_ref, m2_ref, v2_ref = outs[4 * i:4 * i + 4]
            g_ref[...] = g
            d_ref[...], m2_ref[...], v2_ref[...] = _adamw_math(w_ref[...], g, m_ref[...], v_ref[...])
        row, off, _ = SMALL_POS["loss"]
        outs[-1][...] = total[row:row + 1, off:off + 1]

    operands = [parts] + [t[n] for n in names for t in (w, m, v)]
    shapes = [jax.ShapeDtypeStruct(w[n].shape, F32) for n in names for _ in range(4)]
    shapes.append(jax.ShapeDtypeStruct((1, 1), F32))
    out = pl.pallas_call(body, name=name, out_shape=shapes)(*operands)
    return {n: tuple(out[4 * i:4 * i + 4]) for i, n in enumerate(names)}, out[-1]


TRANSPOSED = frozenset(("ffn1_w_gate", "ffn1_w_up", "w_in", "w_branch_fox", "w_branch_sb", "ffn2_w_gate", "ffn2_w_up",
                        "w_ple_proj"))
F_PAD_ROWS = C_SQ - FL_REAL_END


def _pack(pieces, group, dtype):
    out = []
    for name in GATHER_GROUPS[group]:
        r = pieces[name].T if name in TRANSPOSED else pieces[name]
        r = r.reshape(-1, D_MODEL).astype(dtype)
        if r.shape[0] != PACK_ROWS[name]:
            r = jnp.pad(r, ((0, PACK_ROWS[name] - r.shape[0]), (0, 0)))
        out.append(r)
    return jnp.concatenate(out, axis=0)


def _real_rows(name):
    return W_IN_ROWS if name == "w_in" else PACK_ROWS[name]


def _gathered(got, name, shape):
    off = GATHER_OFF[name]
    return got[:, off:off + _real_rows(name), :].reshape(shape)


def _w_in_device_rows(d):
    lo, hi = d * W_IN_ROWS, (d + 1) * W_IN_ROWS
    if hi <= FL_REAL_END:
        return [(lo, hi)]
    if lo >= FL_REAL_END:
        return [(lo + F_PAD_ROWS, hi + F_PAD_ROWS)]
    return [(lo, FL_REAL_END), (C_SQ, hi + F_PAD_ROWS)]


def _w_in_t_padded(got):
    t = _gathered(got, "w_in", (IN_REAL, D_MODEL))
    return jnp.concatenate([t[:FL_REAL_END], jnp.zeros((F_PAD_ROWS, D_MODEL), t.dtype), t[FL_REAL_END:]], axis=0)


def _pack_chunks(grads, group):
    out = []
    for name in SCATTER_GROUPS[group]:
        base, _, half = name.partition("#")
        g = grads[base].astype(BF16)
        if base == "w_in":
            lo, hi = W_IN_HALVES[int(half)]
            tail = jnp.zeros((PACK_ROWS[base] - W_IN_ROWS, D_MODEL), BF16)
            c = jnp.stack([jnp.concatenate([g[a:b] for a, b in _w_in_device_rows(d)] + [tail], axis=0)[lo:hi]
                           for d in range(N_DEV)])
        else:
            c = g.reshape(N_DEV, PACK_ROWS[name], D_MODEL)
        out.append(c)
    return out[0] if len(out) == 1 else jnp.concatenate(out, axis=1)


def _shard_grad(summed, name, shape):
    if name == "w_in":
        rows = jnp.concatenate([summed[f"w_in#{i}"] for i in range(len(W_IN_HALVES))], axis=0)[:W_IN_ROWS]
    else:
        rows = summed[name][SCATTER_OFF[name]:SCATTER_OFF[name] + PACK_ROWS[name], :]
    return rows.reshape(shape[1], shape[0]).T if name in TRANSPOSED else rows.reshape(shape)


WEIGHT_NAMES = ['ffn1_norm', 'ffn1_w_gate', 'ffn1_w_up', 'ffn1_w_down', 'mix_norm', 'w_in', 'forget_bias', 'q_norm',
                'k_norm', 'w_branch_fox', 'w_branch_sb', 'w_out', 'ffn2_norm', 'ffn2_w_gate', 'ffn2_w_up',
                'ffn2_w_down', 'ple_norm', 'w_ple_gate', 'w_ple_proj']
SMALL_NAMES = ('ffn1_norm', 'mix_norm', 'ffn2_norm', 'ple_norm', 'q_norm', 'k_norm', 'forget_bias')
SMALL_POS = {'ffn1_norm': (0, 0, D_MODEL), 'mix_norm': (1, 0, D_MODEL), 'ffn2_norm': (2, 0, D_MODEL),
             'ple_norm': (3, 0, D_MODEL), 'q_norm': (4, 0, HEAD_DIM), 'k_norm': (4, HEAD_DIM, HEAD_DIM),
             'forget_bias': (4, 2 * HEAD_DIM, N_HEADS), 'loss': (4, 2 * HEAD_DIM + N_HEADS, 1)}


def _pack_small(vals, loss):
    tail = [vals[n].reshape(1, -1) for n in ('q_norm', 'k_norm', 'forget_bias')] + [loss.reshape(1, 1)]
    tail.append(jnp.zeros((1, D_MODEL - sum(t.shape[1] for t in tail)), F32))
    rows = [vals[n].reshape(1, D_MODEL) for n in SMALL_NAMES[:4]] + [jnp.concatenate(tail, axis=1)]
    rows.append(jnp.zeros((SMALL_ROWS - len(rows), D_MODEL), F32))
    return jnp.concatenate(rows, axis=0)


def _step(x, p, tgt, w):
    row = lambda a: a.reshape(1, -1).astype(F32)
    g_ffn1, g_mix, g_ffn2, g_ple = (row(w[n]) for n in SMALL_NAMES[:4])
    qg = jnp.tile(row(w['q_norm']), (1, N_HEADS))
    kg = jnp.tile(row(w['k_norm']), (1, N_HEADS))
    bias = jnp.pad(row(w['forget_bias']), ((0, 0), (0, LANES - N_HEADS)))
    half = D_FF // 2
    grads = {}

    blk = lambda n: GATHER_OFF[n] // FFN_SHARD
    ffn1 = tuple(blk(n) for n in ("ffn1_w_gate", "ffn1_w_up", "ffn1_w_down"))
    ffn2 = tuple(blk(n) for n in ("ffn2_w_gate", "ffn2_w_up", "ffn2_w_down"))
    got0 = _gather_two_level(_pack(w, 0, BF16), "gather_ffn1")
    x1, g1, u1, h1, got1 = _ffn_fwd(x, g_ffn1, (got0,) * 3, ffn1, "ffn1_fwd", ride=(_pack(w, 1, BF16), True))
    w_in = _w_in_t_padded(got1)
    wbf = _gathered(got1, "w_branch_fox", (D_MODEL, ATT_W))
    wbs = _gathered(got1, "w_branch_sb", (D_MODEL, ATT_W))
    wo = _gathered(got1, "w_out", (D_MODEL, D_MODEL))
    (hmix, fqr, fkr, fqn, fkn, fv, logf, f_col, f_row, sq, sk, sv, gf, gs) = _mix_fwd(
        x1, g_mix, w_in, bias, qg, kg, "mix_fwd")
    f_wide = jnp.repeat(f_col[:, :N_HEADS], HEAD_DIM, axis=1)
    cap = _fox_qk_cap(w['q_norm'], w['k_norm'])
    y_fox, lse, got2 = _fox_fwd(fqn, fkn, fv, f_wide, f_row, cap, "fox_fwd", ride=(_pack(w, 2, BF16), True))
    y_sb, y_sb32, got3 = _sb_fwd(sq, sk, sv, "sb_fwd", ride=(_pack(w, 3, BF16), True))
    wpg = _gathered(got3, "w_ple_gate", (D_MODEL, D_MODEL))
    wpp = _gathered(got3, "w_ple_proj", (D_MODEL, PLE_DIM))
    ffn2_bufs = (got2, got2, got3)
    x2 = _merge_fwd(x1, y_fox, y_sb, gf, gs, wbf, wbs, wo, "merge_fwd")
    x3, g2, u2, h2, = _ffn_fwd(x2, g_ffn2, ffn2_bufs, ffn2, "ffn2_fwd")
    dx3, n_ple, ds_ple, dpp, dg_ple, loss = _ple_loss(x3, p, tgt, g_ple, wpg, wpp, "ple_loss")

    grads['w_ple_gate'] = _wgrad(n_ple, ds_ple, "dw_ple_gate", D_MODEL, D_MODEL)
    grads['w_ple_proj'] = _wgrad(dpp, p, "dw_ple_proj", D_MODEL, PLE_DIM)
    dg2, du2, act2, dx2, dg_ffn2 = _ffn_bwd_fused(x2, dx3, g_ffn2, g2, u2, ffn2_bufs, ffn2, "ffn2_bwd")
    grads['ffn2_w_gate'] = _wgrad(dg2, h2, "dw_ffn2_gate", half, D_MODEL)
    grads['ffn2_w_up'] = _wgrad(du2, h2, "dw_ffn2_up", half, D_MODEL)
    grads['ffn2_w_down'] = _wgrad(act2, dx3, "dw_ffn2_down", half, D_MODEL)
    dyf, dys, dgf, dgs, dbf, dbs, merged = _merge_bwd(dx2, y_fox, y_sb, gf, gs, wbf, wbs, wo, "merge_bwd")
    grads['w_branch_fox'] = _wgrad(dbf, y_fox, "dw_branch_fox", D_MODEL, ATT_W)
    grads['w_branch_sb'] = _wgrad(dbs, y_sb, "dw_branch_sb", D_MODEL, ATT_W)
    grads['w_out'] = _wgrad(merged, dx2, "dw_out", D_MODEL, D_MODEL)
    dfqn, dfkn_t, dfv_t, dft, part_rest = _fox_bwd(fqn, fkn, fv, dyf, y_fox, lse, f_wide, f_row, cap, "fox_bwd",
                                               ride=(_pack_chunks(grads, 5), False))
    dsq, dsk_t, dsv_t = _sb_bwd(sq, sk, sv, dys, y_sb32, "sb_bwd")
    s_len = x.shape[0]
    df_col = jnp.pad(dft[:, :2, :].reshape(N_HEADS, s_len).T, ((0, 0), (0, LANES - N_HEADS)))
    dproj, dx1, dg_mix, dqg, dkg, dbias = _mix_bwd(
        x1, dx2, g_mix, w_in, fqr, fkr, dfqn, dfkn_t, qg, kg, dfv_t, df_col, logf, dsq, dsk_t, dsv_t, dgf, dgs,
        "mix_bwd")
    grads['w_in'] = _wgrad(dproj, hmix, "dw_in", IN_PAD // 3, D_MODEL)
    dg1, du1, act1, dx0, dg_ffn1, part_in0 = _ffn_bwd_fused(x, dx1, g_ffn1, g1, u1, (got0,) * 3, ffn1, "ffn1_bwd",
                                                            ride=(_pack_chunks(grads, 3), False))
    grads['ffn1_w_gate'], part_in1 = _wgrad(dg1, h1, "dw_ffn1_gate", half, D_MODEL,
                                            ride=(_pack_chunks(grads, 4), False))
    grads['ffn1_w_up'], part_gate = _wgrad(du1, h1, "dw_ffn1_up", half, D_MODEL, ride=(_pack_chunks(grads, 0), False))
    grads['ffn1_w_down'], part_up = _wgrad(act1, dx1, "dw_ffn1_down", half, D_MODEL,
                                           ride=(_pack_chunks(grads, 1), False))
    part_down = _exchange(_pack_chunks(grads, 2), "scatter_ffn1_down", False)

    fold = lambda a: a.reshape(N_HEADS, HEAD_DIM).sum(axis=0).reshape(1, HEAD_DIM)
    small_g = {'ffn1_norm': dg_ffn1, 'mix_norm': dg_mix, 'ffn2_norm': dg_ffn2, 'ple_norm': dg_ple,
               'q_norm': fold(dqg), 'k_norm': fold(dkg), 'forget_bias': dbias[:, :N_HEADS]}
    return loss[0, 0], dx0, (part_gate, part_up, part_down, part_in0, part_in1, part_rest), small_g


def kernel(x, p, ffn1_norm, ffn1_w_gate, ffn1_w_up, ffn1_w_down, mix_norm, w_in, forget_bias, q_norm, k_norm, w_branch_fox, w_branch_sb, w_out, ffn2_norm, ffn2_w_gate, ffn2_w_up, ffn2_w_down, ple_norm, w_ple_gate, w_ple_proj, loss_target, m_ffn1_norm, m_ffn1_w_gate, m_ffn1_w_up, m_ffn1_w_down, m_mix_norm, m_w_in, m_forget_bias, m_q_norm, m_k_norm, m_w_branch_fox, m_w_branch_sb, m_w_out, m_ffn2_norm, m_ffn2_w_gate, m_ffn2_w_up, m_ffn2_w_down, m_ple_norm, m_w_ple_gate, m_w_ple_proj, v_ffn1_norm, v_ffn1_w_gate, v_ffn1_w_up, v_ffn1_w_down, v_mix_norm, v_w_in, v_forget_bias, v_q_norm, v_k_norm, v_w_branch_fox, v_w_branch_sb, v_w_out, v_ffn2_norm, v_ffn2_w_gate, v_ffn2_w_up, v_ffn2_w_down, v_ple_norm, v_w_ple_gate, v_w_ple_proj):
    args = dict(locals())
    w = {n: args[n][0] for n in WEIGHT_NAMES}
    m = {n: args["m_" + n][0] for n in WEIGHT_NAMES}
    v = {n: args["v_" + n][0] for n in WEIGHT_NAMES}
    loss, dx, parts, small_g = _step(x[0], p[0, 0], loss_target[0], w)

    group_of = {n: grp for grp, members in enumerate(SCATTER_GROUPS) for n in members}
    direct = [n for n in WEIGHT_NAMES if n.startswith("ffn") and n not in SMALL_NAMES]
    summed = {}
    for grp, part in enumerate(parts):
        if not all(n in direct for n in SCATTER_GROUPS[grp]):
            s = _sum_parts(part, f"sum_grads_{grp}", SUM_TILE_ROWS[grp])
            summed.update({n: s for n in SCATTER_GROUPS[grp]})
    big = {}
    for n in WEIGHT_NAMES:
        if n in direct:
            big[n] = tuple(_adamw_from_parts(parts[group_of[n]], SCATTER_OFF[n], w[n], m[n], v[n], "adamw_" + n,
                                             n in TRANSPOSED))
        elif n not in SMALL_NAMES:
            g = _shard_grad(summed, n, w[n].shape)
            big[n] = (g,) + tuple(_adamw_shard(g, w[n], m[n], v[n], "adamw_" + n))
    small_parts = _exchange(_pack_small(small_g, loss), "gather_small", True)
    small, total_loss = _adamw_small(small_parts, *({n: args[pre + n] for n in SMALL_NAMES} for pre in ("", "m_", "v_")),
                                     "adamw_small")
    big.update(small)

    outs = [total_loss.reshape(()), dx.reshape(x.shape)]
    for kind in range(4):
        outs += [big[n][kind].reshape(args[n].shape) for n in WEIGHT_NAMES]
    return tuple(outs)
```

```python
import jax
import jax.numpy as jnp
from jax import lax
from jax.experimental import pallas as pl
from jax.experimental.pallas import tpu as pltpu

F32 = jnp.float32
BF16 = jnp.bfloat16

D_MODEL = 1024
D_FF = 2816
N_HEADS = 8
HEAD_DIM = 64
ATT_W = N_HEADS * HEAD_DIM
PLE_DIM = 256
EPS = 1e-6
N_DEV = 8
MESH = pl.DeviceIdType.MESH

LANES = 128
V7X_SCOPED_VMEM_BYTES = 56 * 1024 * 1024

C_FQ, C_FK, C_FV, C_FL = 0, 512, 1024, 1536
C_SQ, C_SK, C_SV, C_GF, C_GS = 1792, 2304, 2816, 3328, 4352
IN_PAD = 5376
IN_REAL = 5128
FL_REAL_END = 1544

ADAM_LR = 0.001
ADAM_B1 = 0.9
ADAM_B2 = 0.999
ADAM_EPS = 1e-08
ADAM_WD = 0.01
ADAM_STEP = 10

PACK_ROWS = {"ffn1_w_gate": 352, "ffn1_w_up": 352, "ffn1_w_down": 352, "w_in": 656, "w_branch_fox": 64,
             "w_branch_sb": 64, "w_out": 128, "ffn2_w_gate": 352, "ffn2_w_up": 352, "ffn2_w_down": 352,
             "w_ple_gate": 128, "w_ple_proj": 32}
GATHER_GROUPS = (
    ("ffn1_w_gate", "ffn1_w_up", "ffn1_w_down"),
    ("w_in", "w_branch_fox", "w_branch_sb", "w_out"),
    ("ffn2_w_gate", "ffn2_w_up"),
    ("ffn2_w_down", "w_ple_gate", "w_ple_proj"),
)
W_IN_HALVES = ((0, 336), (336, 656))
PACK_ROWS.update({f"w_in#{i}": hi - lo for i, (lo, hi) in enumerate(W_IN_HALVES)})
SCATTER_GROUPS = (
    ("ffn1_w_gate",), ("ffn1_w_up",), ("ffn1_w_down",),
    ("w_in#0",), ("w_in#1",),
    ("ffn2_w_gate", "ffn2_w_up", "ffn2_w_down", "w_ple_gate", "w_ple_proj", "w_branch_fox", "w_branch_sb", "w_out"),
)
SUM_TILE_ROWS = (352, 352, 352, 336, 320, 368)


def _offsets(groups):
    off = {}
    for grp in groups:
        o = 0
        for n in grp:
            off[n] = o
            o += PACK_ROWS[n]
    return off


GATHER_OFF = _offsets(GATHER_GROUPS)
SCATTER_OFF = _offsets(SCATTER_GROUPS)
W_IN_ROWS = 641

SMALL_ROWS = 8


def _cparams(*sem):
    return pltpu.CompilerParams(dimension_semantics=sem, vmem_limit_bytes=V7X_SCOPED_VMEM_BYTES)


def _dot(a, b):
    return jnp.dot(a, b, preferred_element_type=F32)


def _dot_nt(a, b):
    return lax.dot_general(a, b, (((1,), (1,)), ((), ())), preferred_element_type=F32)


def _dot_tn(a, b):
    return lax.dot_general(a, b, (((0,), (0,)), ((), ())), preferred_element_type=F32)


def _split(x, parts):
    out = []
    r = x
    for _ in range(parts):
        p = r.astype(BF16)
        out.append(p)
        r = r - p.astype(F32)
    return out


def _dot_split(x, m, parts):
    acc = None
    for p in _split(x, parts):
        t = _dot(p, m)
        acc = t if acc is None else acc + t
    return acc


def _dot_split_left(m, x, parts):
    acc = None
    for p in _split(x, parts):
        t = _dot(m, p)
        acc = t if acc is None else acc + t
    return acc


def _rms_rinv(xf):
    return lax.rsqrt(jnp.mean(xf * xf, axis=-1, keepdims=True) + EPS)


def _sigmoid(x):
    return 1.0 / (1.0 + jnp.exp(-x))


def _softplus_neg_abs(z):
    return jnp.log(1.0 + jnp.exp(-jnp.abs(z)))


FFN_SHARD = D_FF // N_DEV
FFN_CHUNK = 4


def _ffn_w_spec(blk, index_map):
    return pl.BlockSpec((FFN_CHUNK, FFN_SHARD, D_MODEL), lambda *g: (index_map(*g), blk, 0))


def _ffn_w(ref):
    return ref[...].reshape(FFN_CHUNK * FFN_SHARD, D_MODEL)


def _ffn_fwd(x, gain, wbufs, blks, name, ride=None):
    s_len = x.shape[0]
    ts = min(512, s_len)
    fc = FFN_CHUNK * FFN_SHARD
    nt, nc = s_len // ts, D_FF // fc

    def body(x_ref, gain_ref, wg_ref, wu_ref, wd_ref, y_ref, g_ref, u_ref, h_ref, acc_scr):
        j = pl.program_id(1)

        @pl.when(j == 0)
        def _():
            xf = x_ref[...]
            h_ref[...] = ((xf * _rms_rinv(xf)) * gain_ref[...]).astype(BF16)
            acc_scr[...] = jnp.zeros_like(acc_scr)

        h = h_ref[...]
        g = _dot_nt(h, _ffn_w(wg_ref))
        u = _dot_nt(h, _ffn_w(wu_ref))
        g_ref[...] = g.astype(BF16)
        u_ref[...] = u.astype(BF16)
        a = (g * _sigmoid(g) * u).astype(BF16)
        acc_scr[...] += _dot(a, _ffn_w(wd_ref))

        @pl.when(j == nc - 1)
        def _():
            y_ref[...] = x_ref[...] + 0.5 * acc_scr[...]

    tok = pl.BlockSpec((ts, D_MODEL), lambda i, j: (i, 0))
    hid = pl.BlockSpec((ts, fc), lambda i, j: (i, j))
    return _ride_call(
        body, name, (nt, nc),
        [tok, pl.BlockSpec((1, D_MODEL), lambda i, j: (0, 0))] + [_ffn_w_spec(b, lambda i, j: j) for b in blks],
        [tok, hid, hid, tok],
        [jax.ShapeDtypeStruct((s_len, D_MODEL), F32), jax.ShapeDtypeStruct((s_len, D_FF), BF16),
         jax.ShapeDtypeStruct((s_len, D_FF), BF16), jax.ShapeDtypeStruct((s_len, D_MODEL), BF16)],
        [pltpu.VMEM((ts, D_MODEL), F32)], ("parallel", "arbitrary"), (x, gain, *wbufs), ride)


def _ffn_bwd_fused(x, dy, gain, g, u, wbufs, blks, name, ride=None):
    s_len = x.shape[0]
    ts = min(512, s_len)
    fc = FFN_CHUNK * FFN_SHARD
    nt = s_len // ts
    assert D_FF == 2 * fc

    def hidden(dy_ref, g_ref, u_ref, wg_ref, wu_ref, wd_ref, dg_ref, du_ref, act_ref):
        da = 0.5 * _dot_nt(dy_ref[...].astype(BF16), _ffn_w(wd_ref))
        gf = g_ref[...].astype(F32)
        uf = u_ref[...].astype(F32)
        sg = _sigmoid(gf)
        silu = gf * sg
        dg = (da * uf * (sg * (1.0 + gf * (1.0 - sg)))).astype(BF16)
        du = (da * silu).astype(BF16)
        dg_ref[...] = dg
        du_ref[...] = du
        act_ref[...] = (0.5 * silu * uf).astype(BF16)
        return _dot(dg, _ffn_w(wg_ref)) + _dot(du, _ffn_w(wu_ref))

    def first(dy_ref, g_ref, u_ref, wg_ref, wu_ref, wd_ref, dg_ref, du_ref, act_ref, dh_ref):
        dh_ref[...] = hidden(dy_ref, g_ref, u_ref, wg_ref, wu_ref, wd_ref, dg_ref, du_ref, act_ref)

    def second(x_ref, dy_ref, gain_ref, g_ref, u_ref, wg_ref, wu_ref, wd_ref, dh0_ref, dg_half, du_half, act_half,
               dg_ref, du_ref, act_ref, dx_ref, dgain_ref):
        i = pl.program_id(0)
        dh = dh0_ref[...] + hidden(dy_ref, g_ref, u_ref, wg_ref, wu_ref, wd_ref, dg_ref, du_ref, act_ref)
        xf = x_ref[...]
        r = _rms_rinv(xf)
        xhat = xf * r
        dgp = jnp.sum(dh * xhat, axis=0, keepdims=True)

        @pl.when(i == 0)
        def _():
            dgain_ref[...] = dgp

        @pl.when(i > 0)
        def _():
            dgain_ref[...] += dgp

        dn = dh * gain_ref[...]
        dx_ref[...] = dy_ref[...] + r * (dn - xhat * jnp.mean(dn * xhat, axis=-1, keepdims=True))

    tok = pl.BlockSpec((ts, D_MODEL), lambda i: (i, 0))
    row = pl.BlockSpec((1, D_MODEL), lambda i: (0, 0))
    hid = lambda c: pl.BlockSpec((ts, fc), lambda i: (i, c))
    wts = lambda c: [pl.BlockSpec((FFN_CHUNK, FFN_SHARD, D_MODEL), lambda i, b=b: (c, b, 0),
                                  pipeline_mode=pl.Buffered(1)) for b in blks]
    hidden_shapes = [jax.ShapeDtypeStruct((s_len, D_FF), BF16)] * 3
    dg, du, act, dh0, *rode = _ride_call(
        first, name + "_a", (nt,), [tok, hid(0), hid(0)] + wts(0), [hid(0)] * 3 + [tok],
        hidden_shapes + [jax.ShapeDtypeStruct((s_len, D_MODEL), F32)], [], ("parallel",),
        (dy, g, u, *wbufs), ride)
    filled = pl.BlockSpec(memory_space=pl.ANY)
    dg, du, act, dx, dgain = pl.pallas_call(
        second, name=name + "_b", grid=(nt,),
        in_specs=[tok, tok, row, hid(1), hid(1)] + wts(1) + [tok, filled, filled, filled],
        out_specs=[hid(1)] * 3 + [tok, row],
        out_shape=hidden_shapes + [jax.ShapeDtypeStruct((s_len, D_MODEL), F32), jax.ShapeDtypeStruct((1, D_MODEL), F32)],
        input_output_aliases={9: 0, 10: 1, 11: 2},
        compiler_params=_cparams("arbitrary"),
    )(x, dy, gain, g, u, *wbufs, dh0, dg, du, act)
    return (dg, du, act, dx, dgain, *rode)


def _wgrad(a, b, name, tk, tn, ride=None):
    s_len, k_dim = a.shape
    n_dim = b.shape[1]
    ts = min(2048, s_len)
    ns = s_len // ts

    def body(a_ref, b_ref, o_ref, acc):
        s = pl.program_id(2)

        @pl.when(s == 0)
        def _():
            acc[...] = jnp.zeros_like(acc)

        acc[...] += _dot_tn(a_ref[...].astype(BF16), b_ref[...].astype(BF16))

        @pl.when(s == ns - 1)
        def _():
            o_ref[...] = acc[...].astype(BF16)

    out = _ride_call(
        body, name, (k_dim // tk, n_dim // tn, ns),
        [pl.BlockSpec((ts, tk), lambda k, n, s: (s, k)), pl.BlockSpec((ts, tn), lambda k, n, s: (s, n))],
        [pl.BlockSpec((tk, tn), lambda k, n, s: (k, n))], [jax.ShapeDtypeStruct((k_dim, n_dim), BF16)],
        [pltpu.VMEM((tk, tn), F32)], ("parallel", "parallel", "arbitrary"), (a, b), ride)
    return out[0] if ride is None else tuple(out)


HEAD_SUM_PARTS = 1


def _head_group_matrix():
    r = lax.broadcasted_iota(jnp.int32, (ATT_W, ATT_W), 0) // HEAD_DIM
    c = lax.broadcasted_iota(jnp.int32, (ATT_W, ATT_W), 1) // HEAD_DIM
    return (r == c).astype(BF16)


def _mix_fwd(x, gain, w_in, bias, qg, kg, name):
    s_len = x.shape[0]
    ts = min(512, s_len)
    nt = s_len // ts
    gmat = _head_group_matrix()

    def body(x_ref, gain_ref, w_ref, bias_ref, qg_ref, kg_ref, gm_ref,
             h_ref, fqr_ref, fkr_ref, fqn_ref, fkn_ref, fv_ref, logf_ref, f_ref, ft_ref,
             sq_ref, sk_ref, sv_ref, gf_ref, gs_ref, carry):
        i = pl.program_id(0)
        xf = x_ref[...]
        h = ((xf * _rms_rinv(xf)) * gain_ref[...]).astype(BF16)
        h_ref[...] = h
        gm = gm_ref[...]

        def proj(lo, n):
            return _dot_nt(h, w_ref[lo:lo + n, :])

        def headnorm(raw, g):
            ms = _dot_split(raw * raw, gm, HEAD_SUM_PARTS) * (1.0 / HEAD_DIM)
            return ((raw * lax.rsqrt(ms + EPS)) * g).astype(BF16)

        fq = proj(C_FQ, ATT_W)
        fqr_ref[...] = fq
        fqn_ref[...] = headnorm(fq, qg_ref[...])
        fk = proj(C_FK, ATT_W)
        fkr_ref[...] = fk
        fkn_ref[...] = headnorm(fk, kg_ref[...])
        fv_ref[...] = proj(C_FV, ATT_W).astype(BF16)
        sq_ref[...] = proj(C_SQ, ATT_W).astype(BF16)
        sk_ref[...] = proj(C_SK, ATT_W).astype(BF16)
        sv_ref[...] = proj(C_SV, ATT_W).astype(BF16)
        gf_ref[...] = proj(C_GF, D_MODEL)
        gs_ref[...] = proj(C_GS, D_MODEL)

        fl = proj(C_FL, LANES) + bias_ref[...]
        lane = lax.broadcasted_iota(jnp.int32, fl.shape, 1)
        logf = jnp.where(lane < N_HEADS, jnp.minimum(fl, 0.0) - _softplus_neg_abs(fl), 0.0)
        logf_ref[...] = logf

        @pl.when(i == 0)
        def _():
            carry[...] = jnp.zeros_like(carry)

        r = lax.broadcasted_iota(jnp.int32, (ts, ts), 0)
        c = lax.broadcasted_iota(jnp.int32, (ts, ts), 1)
        tri = (r >= c).astype(BF16)
        f_tile = _dot_split_left(tri, logf, 3) + carry[...]
        f_ref[...] = f_tile
        ft_ref[...] = f_tile.T[:N_HEADS, :]
        carry[...] = f_tile[ts - 1:ts, :]

    tok = lambda w: pl.BlockSpec((ts, w), lambda i: (i, 0))
    full = lambda a: pl.BlockSpec(a.shape, lambda i: (0, 0), pipeline_mode=pl.Buffered(1))
    f32o = lambda w: jax.ShapeDtypeStruct((s_len, w), F32)
    b16o = lambda w: jax.ShapeDtypeStruct((s_len, w), BF16)
    return _ride_call(
        body, name, (nt,),
        [tok(D_MODEL), full(gain), full(w_in), full(bias), full(qg), full(kg), full(gmat)],
        [
            tok(D_MODEL), tok(ATT_W), tok(ATT_W), tok(ATT_W), tok(ATT_W), tok(ATT_W), tok(LANES), tok(LANES),
            pl.BlockSpec((N_HEADS, ts), lambda i: (0, i)),
            tok(ATT_W), tok(ATT_W), tok(ATT_W), tok(D_MODEL), tok(D_MODEL),
        ],
        [
            b16o(D_MODEL), f32o(ATT_W), f32o(ATT_W), b16o(ATT_W), b16o(ATT_W), b16o(ATT_W), f32o(LANES), f32o(LANES),
            jax.ShapeDtypeStruct((N_HEADS, s_len), F32),
            b16o(ATT_W), b16o(ATT_W), b16o(ATT_W), f32o(D_MODEL), f32o(D_MODEL),
        ],
        [pltpu.VMEM((1, LANES), F32)], ("arbitrary",), (x, gain, w_in, bias, qg, kg, gmat), None)


ATT_T = 256
SB_ROWS = 2
FOX_ROWS = 2
EXP_ZERO = 88.0


def _att_tiling(s_len, rows):
    t = min(ATT_T, s_len)
    nr = min(rows, s_len // t)
    return t, nr, s_len // (t * nr)


def _pair_specs(s_len, tq):
    qblk = pl.BlockSpec((tq, LANES), lambda hp, i: (i, hp))
    kvfull = pl.BlockSpec((s_len, LANES), lambda hp, i: (0, hp))
    return qblk, kvfull


def _walk_tiles(i, nr, load, sub, flush, more=None, trips=None):
    base = i * nr
    for kk in range(nr - 1, -1, -1):
        rs = list(range(kk, nr))
        flush(base + kk, sub(rs, load(base + kk), [r == kk for r in rs]))

    done = jnp.int32(0)
    for last in range(nr - 1, -1, -1):
        rs = list(range(last + 1))

        def visit(n, rs=rs):
            kb = base - 1 - n
            flush(kb, sub(rs, load(kb), [False] * len(rs)))

        if trips is not None:
            todo = jnp.maximum(trips(base, last) - done, 0)

            def body(it, carry, start=done, visit=visit):
                visit(start + it)
                return carry

            lax.fori_loop(0, todo, body, jnp.int32(0))
            done = done + todo
        else:
            def step(state, visit=visit, last=last):
                visit(state[0])
                return state[0] + 1, more(last)

            done, _ = lax.while_loop(lambda state: jnp.logical_and(state[0] < base, state[1] > 0), step,
                                     (done, more(last)))


def _stack(parts):
    return parts[0] if len(parts) == 1 else jnp.concatenate(parts, axis=0)


def _stacked_halves(x, lo):
    z = jnp.zeros_like(x)
    return jnp.concatenate([jnp.where(lo, x, z), jnp.where(lo, z, x)], axis=0)


def _fox_qk_cap(q_gain, k_gain):
    cap = (HEAD_DIM ** 0.5) * jnp.max(jnp.abs(q_gain)) * jnp.max(jnp.abs(k_gain))
    return (cap * 1.01 + 1.0).reshape(1).astype(F32)


def _fox_trips(hp, flast_ref, cap_ref, fq, level):
    def trips(base, r):
        gap = [jnp.max(fq[r][j] - level(r, j)) + cap_ref[0] for j in (0, 1)]

        def needed(n):
            kb = jnp.maximum(base - 1 - n, 0)
            return jnp.logical_or(gap[0] - flast_ref[2 * hp, kb] > -EXP_ZERO,
                                  gap[1] - flast_ref[2 * hp + 1, kb] > -EXP_ZERO)

        return lax.while_loop(lambda n: jnp.logical_and(n < base, needed(n)), lambda n: n + 1, jnp.int32(0))
    return trips


def _fox_fwd(q, k, v, f_wide, f_row, cap, name, ride=None):
    s_len = q.shape[0]
    t, nr, nq = _att_tiling(s_len, FOX_ROWS)

    def body(q_ref, k_ref, v_ref, f_ref, ft_ref, cap_ref, fl_ref, y_ref, lse_ref, m_ref, l_ref, acc_ref):
        hp = pl.program_id(0)
        i = pl.program_id(1)
        lane = lax.broadcasted_iota(jnp.int32, (t, LANES), 1)
        lo = lane < HEAD_DIM
        causal = lax.broadcasted_iota(jnp.int32, (t, t), 0) >= lax.broadcasted_iota(jnp.int32, (t, t), 1)
        rows = [pl.ds(r * t, t) for r in range(nr)]
        slab = lambda r, j: pl.ds((2 * r + j) * t, t)
        qst = [_stacked_halves(q_ref[rw, :] * jnp.asarray(HEAD_DIM ** -0.5, BF16), lo) for rw in rows]
        q_all = _stack(qst)
        fq = [[f_ref[rw, j * HEAD_DIM:j * HEAD_DIM + 1] for j in (0, 1)] for rw in rows]
        m_ref[...] = jnp.full(m_ref.shape, -1e30, F32)
        l_ref[...] = jnp.zeros_like(l_ref)
        acc_ref[...] = jnp.zeros_like(acc_ref)

        def load(kb):
            k0 = pl.multiple_of(kb * t, t)
            frow = [ft_ref[pl.ds(2 * hp + j, 1), pl.ds(k0, t)] for j in (0, 1)]
            return k_ref[pl.ds(k0, t), :], v_ref[pl.ds(k0, t), :], frow

        def sub(rs, tiles, masked):
            kblk, vblk, frow = tiles
            z = _dot_nt(q_all if len(rs) == nr else _stack([qst[r] for r in rs]), kblk)
            slabs = [(r, j) for r in range(len(rs)) for j in (0, 1)]
            ps, alpha = [], []
            for n, (r, j) in enumerate(slabs):
                sl = slab(rs[r], j)
                m_old = m_ref[sl, :]
                s = z[n * t:(n + 1) * t, :] + (fq[rs[r]][j] - frow[j])
                if masked[r]:
                    s = jnp.where(causal, s, -1e30)
                mj = jnp.maximum(m_old, jnp.max(s, axis=1, keepdims=True))
                aj = jnp.exp(m_old - mj)
                p = jnp.exp(s - jnp.tile(mj, (1, t // LANES)))
                m_ref[sl, :] = mj
                l_ref[sl, :] = aj * l_ref[sl, :] + jnp.sum(p, axis=1, keepdims=True)
                alpha.append(aj)
                ps.append(p.astype(BF16))
            pv = _dot(_stack(ps), vblk)
            for n, (r, j) in enumerate(slabs):
                sl = slab(rs[r], j)
                acc_ref[sl, :] = acc_ref[sl, :] * alpha[n] + pv[n * t:(n + 1) * t, :]
            return None

        trips = _fox_trips(hp, fl_ref, cap_ref, fq, lambda r, j: m_ref[slab(r, j), :])
        _walk_tiles(i, nr, load, sub, lambda kb, side: None, trips=trips)
        for r, rw in enumerate(rows):
            l0, l1 = l_ref[slab(r, 0), :], l_ref[slab(r, 1), :]
            y_ref[rw, :] = jnp.where(lo, acc_ref[slab(r, 0), :] / l0, acc_ref[slab(r, 1), :] / l1).astype(BF16)
            lse_ref[0, rw, :] = jnp.where(lo, m_ref[slab(r, 0), :] + jnp.log(l0), m_ref[slab(r, 1), :] + jnp.log(l1))

    state = [pltpu.VMEM((2 * nr * t, LANES), F32)] * 3
    qblk, kvfull = _pair_specs(s_len, t * nr)
    return _ride_call(
        body, name, (N_HEADS // 2, nq),
        [qblk, kvfull, kvfull,
         qblk, pl.BlockSpec((N_HEADS, s_len), lambda hp, i: (0, 0)),
         pl.BlockSpec(memory_space=pltpu.SMEM), pl.BlockSpec(memory_space=pltpu.SMEM)],
        [qblk, pl.BlockSpec((1, t * nr, LANES), lambda hp, i: (hp, i, 0))],
        [jax.ShapeDtypeStruct((s_len, ATT_W), BF16), jax.ShapeDtypeStruct((N_HEADS // 2, s_len, LANES), F32)],
        state, ("parallel", "parallel"), (q, k, v, f_wide, f_row, cap, f_row[:, t - 1::t]), ride)


def _fox_bwd(q, k, v, dy, y, lse, f_wide, f_row, cap, name, ride=None):
    s_len = q.shape[0]
    t, nr, nq = _att_tiling(s_len, FOX_ROWS)

    def body(q_ref, k_ref, v_ref, dy_ref, y_ref, lse_ref, f_ref, ft_ref, cap_ref, fl_ref,
             dq_ref, dkt_ref, dvt_ref, dft_ref, dqs_ref, rsum_ref):
        hp = pl.program_id(0)
        i = pl.program_id(1)

        @pl.when(i == 0)
        def _():
            dkt_ref[...] = jnp.zeros_like(dkt_ref)
            dvt_ref[...] = jnp.zeros_like(dvt_ref)
            dft_ref[...] = jnp.zeros_like(dft_ref)

        dqs_ref[...] = jnp.zeros_like(dqs_ref)
        rsum_ref[...] = jnp.zeros_like(rsum_ref)
        slab = lambda r, j: pl.ds((2 * r + j) * t, t)

        lane = lax.broadcasted_iota(jnp.int32, (t, LANES), 1)
        lo = lane < HEAD_DIM
        causal = lax.broadcasted_iota(jnp.int32, (t, t), 0) >= lax.broadcasted_iota(jnp.int32, (t, t), 1)
        rows = [pl.ds(r * t, t) for r in range(nr)]
        qst, dyst, delta, lse, fq = [], [], [], [], []
        for rw in rows:
            qst.append(_stacked_halves(q_ref[rw, :] * jnp.asarray(HEAD_DIM ** -0.5, BF16), lo))
            dyb = dy_ref[rw, :]
            dyst.append(_stacked_halves(dyb, lo))
            prod = dyb.astype(F32) * y_ref[rw, :].astype(F32)
            delta.append([jnp.sum(jnp.where(lo, prod, 0.0), axis=1, keepdims=True),
                          jnp.sum(jnp.where(lo, 0.0, prod), axis=1, keepdims=True)])
            lse_b = lse_ref[0, rw, :]
            lse.append([lse_b[:, 0:1], lse_b[:, HEAD_DIM:HEAD_DIM + 1]])
            fq.append([f_ref[rw, j * HEAD_DIM:j * HEAD_DIM + 1] for j in (0, 1)])

        q_all, dy_all = _stack(qst), _stack(dyst)
        q_all_t, dy_all_t = q_all.T, dy_all.T

        def load(kb):
            k0 = pl.multiple_of(kb * t, t)
            frow = [ft_ref[pl.ds(2 * hp + j, 1), pl.ds(k0, t)] for j in (0, 1)]
            return k_ref[pl.ds(k0, t), :], v_ref[pl.ds(k0, t), :], frow

        def sub(rs, tiles, masked):
            kblk, vblk, frow = tiles
            qs, dys = (q_all, dy_all) if len(rs) == nr else (_stack([qst[r] for r in rs]), _stack([dyst[r] for r in rs]))
            cols = slice(2 * rs[0] * t, 2 * (rs[-1] + 1) * t)
            z = _dot_nt(qs, kblk)
            dp = _dot_nt(dys, vblk)
            slabs = [(r, j) for r in range(len(rs)) for j in (0, 1)]
            pb, dsb, col = [], [], [None, None]
            for n, (r, j) in enumerate(slabs):
                sl = slice(n * t, (n + 1) * t)
                s = z[sl, :] + (fq[rs[r]][j] - frow[j])
                p = jnp.exp(s - lse[rs[r]][j])
                if masked[r]:
                    p = jnp.where(causal, p, 0.0)
                ds = p * (dp[sl, :] - delta[rs[r]][j])
                c = jnp.sum(ds, axis=0, keepdims=True)
                col[j] = c if col[j] is None else col[j] + c
                rsum_ref[slab(rs[r], j), :] += jnp.sum(ds, axis=1, keepdims=True)
                pb.append(p.astype(BF16))
                dsb.append(ds.astype(BF16))
            p_all, ds_all = _stack(pb), _stack(dsb)
            dqs_ref[pl.ds(2 * rs[0] * t, len(slabs) * t), :] += _dot(ds_all, kblk)
            return _dot(q_all_t[:, cols], ds_all), _dot(dy_all_t[:, cols], p_all), col

        def flush(kb, side):
            k0 = pl.multiple_of(kb * t, t)
            dkt_ref[:, pl.ds(k0, t)] += side[0]
            dvt_ref[:, pl.ds(k0, t)] += side[1]
            for j in (0, 1):
                dft_ref[0, pl.ds(j, 1), pl.ds(k0, t)] -= side[2][j]

        trips = _fox_trips(hp, fl_ref, cap_ref, fq, lambda r, j: lse[r][j])
        _walk_tiles(i, nr, load, sub, flush, trips=trips)
        for r, rw in enumerate(rows):
            dq_ref[rw, :] = jnp.where(lo, dqs_ref[slab(r, 0), :], dqs_ref[slab(r, 1), :]) * (HEAD_DIM ** -0.5)
            rs_t = jnp.where(lo, rsum_ref[slab(r, 0), :], rsum_ref[slab(r, 1), :]).T
            q0 = pl.multiple_of((i * nr + r) * t, t)
            for j in (0, 1):
                dft_ref[0, pl.ds(j, 1), pl.ds(q0, t)] += rs_t[j * HEAD_DIM:j * HEAD_DIM + 1, :]

    state = [pltpu.VMEM((2 * nr * t, LANES), F32), pltpu.VMEM((2 * nr * t, 1), F32)]
    qblk, kvfull = _pair_specs(s_len, t * nr)
    kvfull_t = pl.BlockSpec((LANES, s_len), lambda hp, i: (hp, 0))
    return _ride_call(
        body, name, (N_HEADS // 2, nq),
        [qblk, kvfull, kvfull, qblk, qblk,
         pl.BlockSpec((1, t * nr, LANES), lambda hp, i: (hp, i, 0)),
         qblk, pl.BlockSpec((N_HEADS, s_len), lambda hp, i: (0, 0)),
         pl.BlockSpec(memory_space=pltpu.SMEM), pl.BlockSpec(memory_space=pltpu.SMEM)],
        [qblk, kvfull_t, kvfull_t, pl.BlockSpec((1, 8, s_len), lambda hp, i: (hp, 0, 0))],
        [jax.ShapeDtypeStruct((s_len, ATT_W), F32)] + [jax.ShapeDtypeStruct((ATT_W, s_len), F32)] * 2
        + [jax.ShapeDtypeStruct((N_HEADS // 2, 8, s_len), F32)],
        state, ("arbitrary", "arbitrary"), (q, k, v, dy, y, lse, f_wide, f_row, cap, f_row[:, t - 1::t]), ride)


def _sb_more(c_ref, t):
    def more(r):
        return (jnp.max(c_ref[pl.ds(2 * r * t, 2 * t), :]) > -EXP_ZERO).astype(jnp.int32)
    return more


def _stacked_split_dot(slabs, m, parts):
    split = [_split(x, parts) for x in slabs]
    acc = None
    for p in range(parts):
        d = _dot(_stack([s[p] for s in split]), m)
        acc = d if acc is None else acc + d
    return acc


def _sb_weights(z, c, strict, upper, t):
    logs = []
    for n in range(z.shape[0] // t):
        zn = z[n * t:(n + 1) * t, :]
        sp = _softplus_neg_abs(zn)
        l1m = jnp.minimum(-zn, 0.0) - sp
        if strict[n] is not None:
            l1m = jnp.where(strict[n], l1m, 0.0)
        logs.append((jnp.minimum(zn, 0.0) - sp, l1m))
    suf = _stacked_split_dot([l1m for _, l1m in logs], upper, 1)
    out = []
    for n, (logb, l1m) in enumerate(logs):
        after = c[n] + suf[n * t:(n + 1) * t, :]
        a = jnp.exp(logb + after)
        if strict[n] is not None:
            a = jnp.where(strict[n], a, 0.0)
        out.append((logb, a, after[:, 0:1] + l1m[:, 0:1]))
    return out


def _sb_fwd(q, k, v, name, ride=None):
    s_len = q.shape[0]
    t, nr, nq = _att_tiling(s_len, SB_ROWS)

    def body(q_ref, k_ref, v_ref, y_ref, yf_ref, c_ref, acc_ref):
        i = pl.program_id(1)
        lane = lax.broadcasted_iota(jnp.int32, (t, LANES), 1)
        lo = lane < HEAD_DIM
        ri = lax.broadcasted_iota(jnp.int32, (t, t), 0)
        ci = lax.broadcasted_iota(jnp.int32, (t, t), 1)
        strict = ci < ri
        upper = (ri > ci).astype(BF16)
        rows = [pl.ds(r * t, t) for r in range(nr)]
        slab = lambda r, j: pl.ds((2 * r + j) * t, t)
        qst = [_stacked_halves(q_ref[rw, :] * jnp.asarray(HEAD_DIM ** -0.5, BF16), lo) for rw in rows]
        q_all = _stack(qst)
        c_ref[...] = jnp.zeros_like(c_ref)
        acc_ref[...] = jnp.zeros_like(acc_ref)

        def load(kb):
            k0 = pl.multiple_of(kb * t, t)
            return k_ref[pl.ds(k0, t), :], v_ref[pl.ds(k0, t), :]

        def sub(rs, tiles, masked):
            kblk, vblk = tiles
            z = _dot_nt(q_all if len(rs) == nr else _stack([qst[r] for r in rs]), kblk)
            slabs = [(r, j) for r in range(len(rs)) for j in (0, 1)]
            w = _sb_weights(z, [c_ref[slab(rs[r], j), :] for r, j in slabs],
                            [strict if masked[r] else None for r, j in slabs], upper, t)
            for n, (r, j) in enumerate(slabs):
                c_ref[slab(rs[r], j), :] = w[n][2]
            acc_ref[pl.ds(2 * rs[0] * t, len(slabs) * t), :] += _dot(_stack([a.astype(BF16) for _, a, _ in w]), vblk)
            return None

        _walk_tiles(i, nr, load, sub, lambda kb, side: None, more=_sb_more(c_ref, t))
        for r, rw in enumerate(rows):
            y = jnp.where(lo, acc_ref[slab(r, 0), :], acc_ref[slab(r, 1), :])
            y_ref[rw, :] = y.astype(BF16)
            yf_ref[rw, :] = y

    qblk, kvfull = _pair_specs(s_len, t * nr)
    return _ride_call(
        body, name, (N_HEADS // 2, nq), [qblk, kvfull, kvfull], [qblk, qblk],
        [jax.ShapeDtypeStruct((s_len, ATT_W), BF16), jax.ShapeDtypeStruct((s_len, ATT_W), F32)],
        [pltpu.VMEM((2 * nr * t, 1), F32), pltpu.VMEM((2 * nr * t, LANES), F32)], ("parallel", "parallel"),
        (q, k, v), ride)


def _sb_bwd(q, k, v, dy, yf, name):
    s_len = q.shape[0]
    t, nr, nq = _att_tiling(s_len, SB_ROWS)

    def body(q_ref, k_ref, v_ref, dy_ref, yf_ref, dq_ref, dkt_ref, dvt_ref, c_ref, e_ref, dqs_ref):
        i = pl.program_id(1)

        @pl.when(i == 0)
        def _():
            dkt_ref[...] = jnp.zeros_like(dkt_ref)
            dvt_ref[...] = jnp.zeros_like(dvt_ref)

        c_ref[...] = jnp.zeros_like(c_ref)
        e_ref[...] = jnp.zeros_like(e_ref)
        dqs_ref[...] = jnp.zeros_like(dqs_ref)
        slab = lambda r, j: pl.ds((2 * r + j) * t, t)

        lane = lax.broadcasted_iota(jnp.int32, (t, LANES), 1)
        lo = lane < HEAD_DIM
        ri = lax.broadcasted_iota(jnp.int32, (t, t), 0)
        ci = lax.broadcasted_iota(jnp.int32, (t, t), 1)
        strict = ci < ri
        upper = (ri > ci).astype(BF16)
        upper_incl = (ri >= ci).astype(BF16)
        rows = [pl.ds(r * t, t) for r in range(nr)]
        qst, dyst, delta = [], [], []
        for rw in rows:
            qst.append(_stacked_halves(q_ref[rw, :] * jnp.asarray(HEAD_DIM ** -0.5, BF16), lo))
            dyb = dy_ref[rw, :]
            dyst.append(_stacked_halves(dyb, lo))
            prod = dyb.astype(F32) * yf_ref[rw, :]
            delta.append([jnp.sum(jnp.where(lo, prod, 0.0), axis=1, keepdims=True),
                          jnp.sum(jnp.where(lo, 0.0, prod), axis=1, keepdims=True)])
        q_all, dy_all = _stack(qst), _stack(dyst)
        q_all_t, dy_all_t = q_all.T, dy_all.T

        def load(kb):
            k0 = pl.multiple_of(kb * t, t)
            return k_ref[pl.ds(k0, t), :], v_ref[pl.ds(k0, t), :]

        def sub(rs, tiles, masked):
            kblk, vblk = tiles
            qs, dys = (q_all, dy_all) if len(rs) == nr else (_stack([qst[r] for r in rs]), _stack([dyst[r] for r in rs]))
            cols = slice(2 * rs[0] * t, 2 * (rs[-1] + 1) * t)
            slabs = [(r, j) for r in range(len(rs)) for j in (0, 1)]
            z = _dot_nt(qs, kblk)
            w = _sb_weights(z, [c_ref[slab(rs[r], j), :] for r, j in slabs],
                            [strict if masked[r] else None for r, j in slabs], upper, t)
            da = _dot_nt(dys, vblk)
            ab = [a.astype(BF16) for _, a, _ in w]
            dl = [ab[n].astype(F32) * da[n * t:(n + 1) * t, :] for n in range(len(slabs))]
            tail = _stacked_split_dot(dl, upper_incl, 2)
            dzb = []
            for n, (r, j) in enumerate(slabs):
                sl = slab(rs[r], j)
                tl = tail[n * t:(n + 1) * t, :]
                e = e_ref[sl, :]
                dl1m = (delta[rs[r]][j] - e) - tl
                e_ref[sl, :] = e + tl[:, 0:1]
                c_ref[sl, :] = w[n][2]
                dz = dl[n] - jnp.exp(w[n][0]) * (dl[n] + dl1m)
                if masked[r]:
                    dz = jnp.where(strict, dz, 0.0)
                dzb.append(dz.astype(BF16))
            a_all, dz_all = _stack(ab), _stack(dzb)
            dqs_ref[pl.ds(2 * rs[0] * t, len(slabs) * t), :] += _dot(dz_all, kblk)
            return _dot(q_all_t[:, cols], dz_all), _dot(dy_all_t[:, cols], a_all)

        def flush(kb, side):
            k0 = pl.multiple_of(kb * t, t)
            dkt_ref[:, pl.ds(k0, t)] += side[0]
            dvt_ref[:, pl.ds(k0, t)] += side[1]

        _walk_tiles(i, nr, load, sub, flush, more=_sb_more(c_ref, t))
        for r, rw in enumerate(rows):
            dq_ref[rw, :] = jnp.where(lo, dqs_ref[slab(r, 0), :], dqs_ref[slab(r, 1), :]) * (HEAD_DIM ** -0.5)

    qblk, kvfull = _pair_specs(s_len, t * nr)
    kvfull_t = pl.BlockSpec((LANES, s_len), lambda hp, i: (hp, 0))
    return pl.pallas_call(
        body, name=name, grid=(N_HEADS // 2, nq),
        in_specs=[qblk, kvfull, kvfull, qblk, qblk],
        out_specs=[qblk, kvfull_t, kvfull_t],
        out_shape=[jax.ShapeDtypeStruct((s_len, ATT_W), F32)] + [jax.ShapeDtypeStruct((ATT_W, s_len), F32)] * 2,
        scratch_shapes=[pltpu.VMEM((2 * nr * t, 1), F32), pltpu.VMEM((2 * nr * t, 1), F32),
                        pltpu.VMEM((2 * nr * t, LANES), F32)],
        compiler_params=_cparams("arbitrary", "arbitrary"),
    )(q, k, v, dy, yf)


def _merge_fwd(x, yf, ys, gf, gs, wbf, wbs, wo, name):
    s_len = x.shape[0]
    ts = min(512, s_len)

    def body(x_ref, yf_ref, ys_ref, gf_ref, gs_ref, wbf_ref, wbs_ref, wo_ref, o_ref):
        merged = (_sigmoid(gf_ref[...]) * _dot_nt(yf_ref[...], wbf_ref[...])
                  + _sigmoid(gs_ref[...]) * _dot_nt(ys_ref[...], wbs_ref[...]))
        o_ref[...] = x_ref[...] + _dot(merged.astype(BF16), wo_ref[...])

    tok = lambda w: pl.BlockSpec((ts, w), lambda i: (i, 0))
    full = lambda a: pl.BlockSpec(a.shape, lambda i: (0, 0))
    return pl.pallas_call(
        body, name=name, grid=(s_len // ts,),
        in_specs=[tok(D_MODEL), tok(ATT_W), tok(ATT_W), tok(D_MODEL), tok(D_MODEL), full(wbf), full(wbs), full(wo)],
        out_specs=tok(D_MODEL),
        out_shape=jax.ShapeDtypeStruct((s_len, D_MODEL), F32),
        compiler_params=_cparams("parallel"),
    )(x, yf, ys, gf, gs, wbf, wbs, wo)


def _merge_bwd(dx, yf, ys, gf, gs, wbf, wbs, wo, name):
    s_len = dx.shape[0]
    ts = min(512, s_len)

    def body(dx_ref, yf_ref, ys_ref, gf_ref, gs_ref, wbf_ref, wbs_ref, wo_ref,
             dyf_ref, dys_ref, dgf_ref, dgs_ref, dbf_ref, dbs_ref, mg_ref):
        bf = _dot_nt(yf_ref[...], wbf_ref[...])
        bs = _dot_nt(ys_ref[...], wbs_ref[...])
        sf = _sigmoid(gf_ref[...])
        ss = _sigmoid(gs_ref[...])
        mg_ref[...] = (sf * bf + ss * bs).astype(BF16)
        dm = _dot_nt(dx_ref[...].astype(BF16), wo_ref[...])
        dbf = (dm * sf).astype(BF16)
        dbs = (dm * ss).astype(BF16)
        dbf_ref[...] = dbf
        dbs_ref[...] = dbs
        dgf_ref[...] = (dm * bf * (sf * (1.0 - sf))).astype(BF16)
        dgs_ref[...] = (dm * bs * (ss * (1.0 - ss))).astype(BF16)
        dyf_ref[...] = _dot(dbf, wbf_ref[...]).astype(BF16)
        dys_ref[...] = _dot(dbs, wbs_ref[...]).astype(BF16)

    tok = lambda w: pl.BlockSpec((ts, w), lambda i: (i, 0))
    full = lambda a: pl.BlockSpec(a.shape, lambda i: (0, 0))
    b16o = lambda w: jax.ShapeDtypeStruct((s_len, w), BF16)
    return pl.pallas_call(
        body, name=name, grid=(s_len // ts,),
        in_specs=[tok(D_MODEL), tok(ATT_W), tok(ATT_W), tok(D_MODEL), tok(D_MODEL), full(wbf), full(wbs), full(wo)],
        out_specs=[tok(ATT_W), tok(ATT_W)] + [tok(D_MODEL)] * 5,
        out_shape=[b16o(ATT_W), b16o(ATT_W)] + [b16o(D_MODEL)] * 5,
        compiler_params=_cparams("parallel"),
    )(dx, yf, ys, gf, gs, wbf, wbs, wo)


def _mix_bwd(x, dx_in, gain, w_in, fqr, fkr, dfqn, dfkn_t, qg, kg, dfv_t, df_col, logf, dsq, dsk_t, dsv_t, dgf, dgs,
             name):
    s_len = x.shape[0]
    ts = min(256, s_len)
    nt = s_len // ts
    gmat = _head_group_matrix()

    def body(x_ref, dxi_ref, gain_ref, w_ref, fqr_ref, fkr_ref, dfqn_ref, dfknt_ref, qg_ref, kg_ref, gm_ref,
             dfvt_ref, df_ref, logf_ref, dsq_ref, dskt_ref, dsvt_ref, dgf_ref, dgs_ref,
             dp_ref, dx_ref, dgain_ref, dqg_ref, dkg_ref, dbias_ref, carry):
        i = pl.program_id(0)

        @pl.when(i == 0)
        def _():
            carry[...] = jnp.zeros_like(carry)
            dgain_ref[...] = jnp.zeros_like(dgain_ref)
            dqg_ref[...] = jnp.zeros_like(dqg_ref)
            dkg_ref[...] = jnp.zeros_like(dkg_ref)
            dbias_ref[...] = jnp.zeros_like(dbias_ref)

        gm = gm_ref[...]

        def headnorm_bwd(raw, dout, g, dg_ref):
            ms = _dot_split(raw * raw, gm, HEAD_SUM_PARTS) * (1.0 / HEAD_DIM)
            r = lax.rsqrt(ms + EPS)
            nrm = raw * r
            dg_ref[...] += jnp.sum(dout * nrm, axis=0, keepdims=True)
            dn = dout * g
            mean_h = _dot_split(dn * nrm, gm, HEAD_SUM_PARTS) * (1.0 / HEAD_DIM)
            return r * (dn - nrm * mean_h)

        dp_ref[:, C_FQ:C_FQ + ATT_W] = headnorm_bwd(fqr_ref[...], dfqn_ref[...], qg_ref[...], dqg_ref).astype(BF16)
        dp_ref[:, C_FK:C_FK + ATT_W] = headnorm_bwd(fkr_ref[...], dfknt_ref[...].T, kg_ref[...], dkg_ref).astype(BF16)
        dp_ref[:, C_FV:C_FV + ATT_W] = dfvt_ref[...].T.astype(BF16)
        dp_ref[:, C_SQ:C_SQ + ATT_W] = dsq_ref[...].astype(BF16)
        dp_ref[:, C_SK:C_SK + ATT_W] = dskt_ref[...].T.astype(BF16)
        dp_ref[:, C_SV:C_SV + ATT_W] = dsvt_ref[...].T.astype(BF16)
        dp_ref[:, C_GF:C_GF + D_MODEL] = dgf_ref[...]
        dp_ref[:, C_GS:C_GS + D_MODEL] = dgs_ref[...]

        r_ = lax.broadcasted_iota(jnp.int32, (ts, ts), 0)
        c_ = lax.broadcasted_iota(jnp.int32, (ts, ts), 1)
        rev = (c_ >= r_).astype(BF16)
        dlogf = _dot_split_left(rev, df_ref[...], 3) + carry[...]
        carry[...] = dlogf[0:1, :]
        lane = lax.broadcasted_iota(jnp.int32, (ts, LANES), 1)
        dfl = jnp.where(lane < N_HEADS, dlogf * (1.0 - jnp.exp(logf_ref[...])), 0.0)
        dbias_ref[...] += jnp.sum(dfl, axis=0, keepdims=True)
        dp_ref[:, C_FL:C_FL + LANES] = dfl.astype(BF16)
        dp_ref[:, C_FL + LANES:C_SQ] = jnp.zeros((ts, C_SQ - C_FL - LANES), BF16)

        dh = _dot(dp_ref[...], w_ref[...])
        xf = x_ref[...]
        r = _rms_rinv(xf)
        xhat = xf * r
        dgain_ref[...] += jnp.sum(dh * xhat, axis=0, keepdims=True)
        dn = dh * gain_ref[...]
        dx_ref[...] = dxi_ref[...] + r * (dn - xhat * jnp.mean(dn * xhat, axis=-1, keepdims=True))

    tok = lambda w: pl.BlockSpec((ts, w), lambda i: (nt - 1 - i, 0))
    full = lambda a: pl.BlockSpec(a.shape, lambda i: (0, 0))
    row = lambda w: pl.BlockSpec((1, w), lambda i: (0, 0))
    tok_t = pl.BlockSpec((ATT_W, ts), lambda i: (0, nt - 1 - i))
    return _ride_call(
        body, name, (nt,),
        [tok(D_MODEL), tok(D_MODEL), full(gain), full(w_in), tok(ATT_W), tok(ATT_W), tok(ATT_W), tok_t,
         full(qg), full(kg), full(gmat), tok_t, tok(LANES), tok(LANES), tok(ATT_W), tok_t, tok_t,
         tok(D_MODEL), tok(D_MODEL)],
        [tok(IN_PAD), tok(D_MODEL), row(D_MODEL), row(ATT_W), row(ATT_W), row(LANES)],
        [jax.ShapeDtypeStruct((s_len, IN_PAD), BF16), jax.ShapeDtypeStruct((s_len, D_MODEL), F32),
         jax.ShapeDtypeStruct((1, D_MODEL), F32), jax.ShapeDtypeStruct((1, ATT_W), F32),
         jax.ShapeDtypeStruct((1, ATT_W), F32), jax.ShapeDtypeStruct((1, LANES), F32)],
        [pltpu.VMEM((1, LANES), F32)], ("arbitrary",),
        (x, dx_in, gain, w_in, fqr, fkr, dfqn, dfkn_t, qg, kg, gmat, dfv_t, df_col, logf, dsq, dsk_t, dsv_t, dgf, dgs),
        None)


def _ple_loss(x, p, tgt, gain, wpg, wpp, name):
    s_len = x.shape[0]
    ts = min(512, s_len)

    def body(x_ref, p_ref, t_ref, gain_ref, wpg_ref, wpp_ref, dx_ref, n_ref, ds_ref, dpp_ref, dgain_ref, loss_ref):
        i = pl.program_id(0)

        @pl.when(i == 0)
        def _():
            dgain_ref[...] = jnp.zeros_like(dgain_ref)
            loss_ref[...] = jnp.zeros_like(loss_ref)

        xf = x_ref[...]
        r = _rms_rinv(xf)
        n = xf * r
        hn = (n * gain_ref[...]).astype(BF16)
        n_ref[...] = hn
        sg = _sigmoid(_dot(hn, wpg_ref[...]))
        pp = _dot_nt(p_ref[...].astype(BF16), wpp_ref[...])
        err = (xf + sg * pp) - t_ref[...]
        sq = jnp.sum(jnp.sum(err * err, axis=1, keepdims=True), axis=0, keepdims=True)
        loss_ref[...] += (0.5 / D_MODEL) * sq
        dout = err * (1.0 / D_MODEL)
        dpp_ref[...] = (dout * sg).astype(BF16)
        ds = (dout * pp * (sg * (1.0 - sg))).astype(BF16)
        ds_ref[...] = ds
        dhn = _dot_nt(ds, wpg_ref[...])
        dgain_ref[...] += jnp.sum(dhn * n, axis=0, keepdims=True)
        dn = dhn * gain_ref[...]
        dx_ref[...] = dout + r * (dn - n * jnp.mean(dn * n, axis=-1, keepdims=True))

    tok = lambda w: pl.BlockSpec((ts, w), lambda i: (i, 0))
    full = lambda a: pl.BlockSpec(a.shape, lambda i: (0, 0))
    return pl.pallas_call(
        body, name=name, grid=(s_len // ts,),
        in_specs=[tok(D_MODEL), tok(PLE_DIM), tok(D_MODEL), full(gain), full(wpg), full(wpp)],
        out_specs=[tok(D_MODEL), tok(D_MODEL), tok(D_MODEL), tok(D_MODEL),
                   pl.BlockSpec((1, D_MODEL), lambda i: (0, 0)), pl.BlockSpec((8, LANES), lambda i: (0, 0))],
        out_shape=[jax.ShapeDtypeStruct((s_len, D_MODEL), F32), jax.ShapeDtypeStruct((s_len, D_MODEL), BF16),
                   jax.ShapeDtypeStruct((s_len, D_MODEL), BF16), jax.ShapeDtypeStruct((s_len, D_MODEL), BF16),
                   jax.ShapeDtypeStruct((1, D_MODEL), F32), jax.ShapeDtypeStruct((8, LANES), F32)],
        compiler_params=_cparams("arbitrary"),
    )(x, p, tgt, gain, wpg, wpp)


def _exchange(x, name, broadcast):
    def body(x_ref, out_ref, send_sems, recv_sems, local_sem):
        _exchange_start(x_ref, out_ref, send_sems, recv_sems, local_sem, broadcast)
        _exchange_wait(x_ref, out_ref, send_sems, recv_sems, local_sem, broadcast)

    return pl.pallas_call(
        body, name=name,
        in_specs=[EXCHANGE_SPEC],
        out_specs=EXCHANGE_SPEC,
        out_shape=_exchange_shape(x, broadcast),
        scratch_shapes=list(EXCHANGE_SEMS),
        compiler_params=pltpu.CompilerParams(has_side_effects=True),
    )(x)


def _gather_two_level(x, name):
    def body(x_ref, out_ref, send_sems, recv_sems, local_sem):
        mx, my, mc = lax.axis_index("x"), lax.axis_index("y"), lax.axis_index("c")
        me, sibling = (mx, my, mc), (mx, my, 1 - mc)
        chips = [(1 - mx, my), (mx, 1 - my), (1 - mx, 1 - my)]

        def slot(px, py, pc):
            return out_ref.at[4 * px + 2 * py + pc]

        def copy(k, block, to, src=None):
            return pltpu.make_async_remote_copy(
                src_ref=slot(*block) if src is None else src, dst_ref=slot(*block),
                send_sem=send_sems.at[k], recv_sem=recv_sems.at[k], device_id=to, device_id_type=MESH)

        mine = pltpu.make_async_copy(x_ref, slot(*me), local_sem)
        mine.start()
        first = [copy(0, me, sibling, src=x_ref)]
        first += [copy(1 + j, me, (*chip, mc), src=x_ref) for j, chip in enumerate(chips)]
        for cp in first:
            cp.start()
        passed = [copy(4 + j, (*chip, mc), sibling) for j, chip in enumerate(chips)]
        for j, chip in enumerate(chips):
            copy(1 + j, (*chip, mc), me).wait_recv()
            passed[j].start()
        copy(0, sibling, me).wait_recv()
        for j, chip in enumerate(chips):
            copy(4 + j, (*chip, 1 - mc), me).wait_recv()
        for cp in first + passed:
            cp.wait_send()
        mine.wait()

    return pl.pallas_call(
        body, name=name,
        in_specs=[EXCHANGE_SPEC],
        out_specs=EXCHANGE_SPEC,
        out_shape=_exchange_shape(x, True),
        scratch_shapes=list(EXCHANGE_SEMS),
        compiler_params=pltpu.CompilerParams(has_side_effects=True),
    )(x)


EXCHANGE_SPEC = pl.BlockSpec(memory_space=pl.ANY)
EXCHANGE_SEMS = (pltpu.SemaphoreType.DMA((N_DEV - 1,)), pltpu.SemaphoreType.DMA((N_DEV - 1,)), pltpu.SemaphoreType.DMA)


def _exchange_shape(x, broadcast):
    return jax.ShapeDtypeStruct((N_DEV,) + tuple(x.shape if broadcast else x.shape[1:]), x.dtype)


def _exchange_copies(x_ref, out_ref, send_sems, recv_sems, local_sem, broadcast, with_recv=True):
    mx, my, mc = lax.axis_index("x"), lax.axis_index("y"), lax.axis_index("c")
    me = 4 * mx + 2 * my + mc

    def src(idx):
        return x_ref if broadcast else x_ref.at[idx]

    local = pltpu.make_async_copy(src(me), out_ref.at[me], local_sem)
    pairs = []
    for k in range(1, N_DEV):
        px = (1 - mx) if k & 4 else mx
        py = (1 - my) if k & 2 else my
        pc = (1 - mc) if k & 1 else mc
        peer = 4 * px + 2 * py + pc
        sems = dict(send_sem=send_sems.at[k - 1], recv_sem=recv_sems.at[k - 1], device_id=(px, py, pc), device_id_type=MESH)
        recv = pltpu.make_async_remote_copy(src_ref=src(peer), dst_ref=out_ref.at[peer], **sems) if with_recv else None
        pairs.append((pltpu.make_async_remote_copy(src_ref=src(peer), dst_ref=out_ref.at[me], **sems), recv))
    return local, pairs


def _exchange_start(*refs_and_mode):
    local, pairs = _exchange_copies(*refs_and_mode, with_recv=False)
    local.start()
    for send, _ in pairs:
        send.start()


def _exchange_wait(*refs_and_mode):
    local, pairs = _exchange_copies(*refs_and_mode)
    for _, recv in pairs:
        recv.wait_recv()
    for send, _ in pairs:
        send.wait_send()
    local.wait()


def _riding(body, grid, n_in, n_out, ride):
    if ride is None:
        return body
    broadcast = ride[1]

    def wrapped(*refs):
        ins, x_ref = refs[:n_in], refs[n_in]
        outs, out_ref = refs[n_in + 1:n_in + 1 + n_out], refs[n_in + 1 + n_out]
        scratch, sems = refs[n_in + 2 + n_out:-3], refs[-3:]
        step = pl.program_id(0)
        for d in range(1, len(grid)):
            step = step * grid[d] + pl.program_id(d)
        total = 1
        for g in grid:
            total *= g

        @pl.when(step == 0)
        def _():
            _exchange_start(x_ref, out_ref, *sems, broadcast)

        body(*ins, *outs, *scratch)

        @pl.when(step == total - 1)
        def _():
            _exchange_wait(x_ref, out_ref, *sems, broadcast)

    return wrapped


def _ride_call(body, name, grid, in_specs, out_specs, out_shape, scratch_shapes, sem, operands, ride):
    if ride is None:
        return pl.pallas_call(body, name=name, grid=grid, in_specs=in_specs, out_specs=out_specs, out_shape=out_shape,
                              scratch_shapes=scratch_shapes, compiler_params=_cparams(*sem))(*operands)
    return pl.pallas_call(
        _riding(body, grid, len(in_specs), len(out_specs), ride), name=name, grid=grid,
        in_specs=list(in_specs) + [EXCHANGE_SPEC], out_specs=list(out_specs) + [EXCHANGE_SPEC],
        out_shape=list(out_shape) + [_exchange_shape(*ride)],
        scratch_shapes=list(scratch_shapes) + list(EXCHANGE_SEMS),
        compiler_params=_cparams(*(["arbitrary"] * len(grid))),
    )(*operands, ride[0])


def _adamw_math(w, g, m, v):
    m2 = ADAM_B1 * m + (1.0 - ADAM_B1) * g
    v2 = ADAM_B2 * v + (1.0 - ADAM_B2) * (g * g)
    m_hat = m2 / (1.0 - ADAM_B1 ** ADAM_STEP)
    v_hat = v2 / (1.0 - ADAM_B2 ** ADAM_STEP)
    delta = -ADAM_LR * (m_hat / (jnp.sqrt(v_hat) + ADAM_EPS) + ADAM_WD * w)
    return delta, m2, v2


def _sum_parts(parts, name, tr):
    _, rows, cols = parts.shape

    def body(p_ref, g_ref):
        g = p_ref[0].astype(F32)
        for s in range(1, N_DEV):
            g = g + p_ref[s].astype(F32)
        g_ref[...] = g

    return pl.pallas_call(
        body, name=name, grid=(rows // tr,),
        in_specs=[pl.BlockSpec((N_DEV, tr, cols), lambda i: (0, i, 0))],
        out_specs=pl.BlockSpec((tr, cols), lambda i: (i, 0)),
        out_shape=jax.ShapeDtypeStruct((rows, cols), F32),
        compiler_params=_cparams("parallel"),
    )(parts)


ADAM_SPLIT_ELEMS = 400_000


def _adamw_shard(g, w, m, v, name):
    rows, cols = w.shape
    tr = rows // 2 if rows * cols > ADAM_SPLIT_ELEMS else rows

    def body(g_ref, w_ref, m_ref, v_ref, d_ref, m2_ref, v2_ref):
        d_ref[...], m2_ref[...], v2_ref[...] = _adamw_math(w_ref[...], g_ref[...], m_ref[...], v_ref[...])

    blk = pl.BlockSpec((tr, cols), lambda i: (i, 0))
    return pl.pallas_call(
        body, name=name, grid=(rows // tr,),
        in_specs=[blk] * 4, out_specs=[blk] * 3,
        out_shape=[jax.ShapeDtypeStruct((rows, cols), F32)] * 3,
        compiler_params=_cparams("parallel"),
    )(g, w, m, v)


def _adamw_from_parts(parts, off, w, m, v, name, transposed):
    n_rows = w.size // D_MODEL
    assert off % n_rows == 0 and w.shape == ((1, D_MODEL, n_rows) if transposed else (1, n_rows, D_MODEL))

    def body(p_ref, w_ref, m_ref, v_ref, g_ref, d_ref, m2_ref, v2_ref):
        g = p_ref[0].astype(F32)
        for s in range(1, N_DEV):
            g = g + p_ref[s].astype(F32)
        g = g.T if transposed else g
        g_ref[0] = g
        d_ref[0], m2_ref[0], v2_ref[0] = _adamw_math(w_ref[0], g, m_ref[0], v_ref[0])

    blk = pl.BlockSpec(w.shape, lambda i: (0, 0, 0))
    return pl.pallas_call(
        body, name=name, grid=(1,),
        in_specs=[pl.BlockSpec((N_DEV, n_rows, D_MODEL), lambda i: (0, off // n_rows, 0))] + [blk] * 3,
        out_specs=[blk] * 4,
        out_shape=[jax.ShapeDtypeStruct(w.shape, F32)] * 4,
        compiler_params=_cparams("arbitrary"),
    )(parts, w, m, v)


def _adamw_small(parts, w, m, v, name):
    names = list(SMALL_NAMES)

    def body(p_ref, *refs):
        ins, outs = refs[:3 * len(names)], refs[3 * len(names):]
        total = p_ref[0]
        for s in range(1, N_DEV):
            total = total + p_ref[s]
        for i, n in enumerate(names):
            row, off, width = SMALL_POS[n]
            g = total[row:row + 1, off:off + width]
            w_ref, m_ref, v_ref = ins[3 * i:3 * i + 3]
            g_ref, d_ref, m2_ref, v2_ref = outs[4 * i:4 * i + 4]
            g_ref[...] = g
            d_ref[...], m2_ref[...], v2_ref[...] = _adamw_math(w_ref[...], g, m_ref[...], v_ref[...])
        row, off, _ = SMALL_POS["loss"]
        outs[-1][...] = total[row:row + 1, off:off + 1]

    operands = [parts] + [t[n] for n in names for t in (w, m, v)]
    shapes = [jax.ShapeDtypeStruct(w[n].shape, F32) for n in names for _ in range(4)]
    shapes.append(jax.ShapeDtypeStruct((1, 1), F32))
    out = pl.pallas_call(body, name=name, out_shape=shapes)(*operands)
    return {n: tuple(out[4 * i:4 * i + 4]) for i, n in enumerate(names)}, out[-1]


TRANSPOSED = frozenset(("ffn1_w_gate", "ffn1_w_up", "w_in", "w_branch_fox", "w_branch_sb", "ffn2_w_gate", "ffn2_w_up",
                        "w_ple_proj"))
F_PAD_ROWS = C_SQ - FL_REAL_END


def _pack(pieces, group, dtype):
    out = []
    for name in GATHER_GROUPS[group]:
        r = pieces[name].T if name in TRANSPOSED else pieces[name]
        r = r.reshape(-1, D_MODEL).astype(dtype)
        if r.shape[0] != PACK_ROWS[name]:
            r = jnp.pad(r, ((0, PACK_ROWS[name] - r.shape[0]), (0, 0)))
        out.append(r)
    return jnp.concatenate(out, axis=0)


def _real_rows(name):
    return W_IN_ROWS if name == "w_in" else PACK_ROWS[name]


def _gathered(got, name, shape):
    off = GATHER_OFF[name]
    return got[:, off:off + _real_rows(name), :].reshape(shape)


def _w_in_device_rows(d):
    lo, hi = d * W_IN_ROWS, (d + 1) * W_IN_ROWS
    if hi <= FL_REAL_END:
        return [(lo, hi)]
    if lo >= FL_REAL_END:
        return [(lo + F_PAD_ROWS, hi + F_PAD_ROWS)]
    return [(lo, FL_REAL_END), (C_SQ, hi + F_PAD_ROWS)]


def _w_in_t_padded(got):
    t = _gathered(got, "w_in", (IN_REAL, D_MODEL))
    return jnp.concatenate([t[:FL_REAL_END], jnp.zeros((F_PAD_ROWS, D_MODEL), t.dtype), t[FL_REAL_END:]], axis=0)


def _pack_chunks(grads, group):
    out = []
    for name in SCATTER_GROUPS[group]:
        base, _, half = name.partition("#")
        g = grads[base].astype(BF16)
        if base == "w_in":
            lo, hi = W_IN_HALVES[int(half)]
            tail = jnp.zeros((PACK_ROWS[base] - W_IN_ROWS, D_MODEL), BF16)
            c = jnp.stack([jnp.concatenate([g[a:b] for a, b in _w_in_device_rows(d)] + [tail], axis=0)[lo:hi]
                           for d in range(N_DEV)])
        else:
            c = g.reshape(N_DEV, PACK_ROWS[name], D_MODEL)
        out.append(c)
    return out[0] if len(out) == 1 else jnp.concatenate(out, axis=1)


def _shard_grad(summed, name, shape):
    if name == "w_in":
        rows = jnp.concatenate([summed[f"w_in#{i}"] for i in range(len(W_IN_HALVES))], axis=0)[:W_IN_ROWS]
    else:
        rows = summed[name][SCATTER_OFF[name]:SCATTER_OFF[name] + PACK_ROWS[name], :]
    return rows.reshape(shape[1], shape[0]).T if name in TRANSPOSED else rows.reshape(shape)


WEIGHT_NAMES = ['ffn1_norm', 'ffn1_w_gate', 'ffn1_w_up', 'ffn1_w_down', 'mix_norm', 'w_in', 'forget_bias', 'q_norm',
                'k_norm', 'w_branch_fox', 'w_branch_sb', 'w_out', 'ffn2_norm', 'ffn2_w_gate', 'ffn2_w_up',
                'ffn2_w_down', 'ple_norm', 'w_ple_gate', 'w_ple_proj']
SMALL_NAMES = ('ffn1_norm', 'mix_norm', 'ffn2_norm', 'ple_norm', 'q_norm', 'k_norm', 'forget_bias')
SMALL_POS = {'ffn1_norm': (0, 0, D_MODEL), 'mix_norm': (1, 0, D_MODEL), 'ffn2_norm': (2, 0, D_MODEL),
             'ple_norm': (3, 0, D_MODEL), 'q_norm': (4, 0, HEAD_DIM), 'k_norm': (4, HEAD_DIM, HEAD_DIM),
             'forget_bias': (4, 2 * HEAD_DIM, N_HEADS), 'loss': (4, 2 * HEAD_DIM + N_HEADS, 1)}


def _pack_small(vals, loss):
    tail = [vals[n].reshape(1, -1) for n in ('q_norm', 'k_norm', 'forget_bias')] + [loss.reshape(1, 1)]
    tail.append(jnp.zeros((1, D_MODEL - sum(t.shape[1] for t in tail)), F32))
    rows = [vals[n].reshape(1, D_MODEL) for n in SMALL_NAMES[:4]] + [jnp.concatenate(tail, axis=1)]
    rows.append(jnp.zeros((SMALL_ROWS - len(rows), D_MODEL), F32))
    return jnp.concatenate(rows, axis=0)


def _step(x, p, tgt, w):
    row = lambda a: a.reshape(1, -1).astype(F32)
    g_ffn1, g_mix, g_ffn2, g_ple = (row(w[n]) for n in SMALL_NAMES[:4])
    qg = jnp.tile(row(w['q_norm']), (1, N_HEADS))
    kg = jnp.tile(row(w['k_norm']), (1, N_HEADS))
    bias = jnp.pad(row(w['forget_bias']), ((0, 0), (0, LANES - N_HEADS)))
    half = D_FF // 2
    grads = {}

    blk = lambda n: GATHER_OFF[n] // FFN_SHARD
    ffn1 = tuple(blk(n) for n in ("ffn1_w_gate", "ffn1_w_up", "ffn1_w_down"))
    ffn2 = tuple(blk(n) for n in ("ffn2_w_gate", "ffn2_w_up", "ffn2_w_down"))
    got0 = _gather_two_level(_pack(w, 0, BF16), "gather_ffn1")
    x1, g1, u1, h1, got1 = _ffn_fwd(x, g_ffn1, (got0,) * 3, ffn1, "ffn1_fwd", ride=(_pack(w, 1, BF16), True))
    w_in = _w_in_t_padded(got1)
    wbf = _gathered(got1, "w_branch_fox", (D_MODEL, ATT_W))
    wbs = _gathered(got1, "w_branch_sb", (D_MODEL, ATT_W))
    wo = _gathered(got1, "w_out", (D_MODEL, D_MODEL))
    (hmix, fqr, fkr, fqn, fkn, fv, logf, f_col, f_row, sq, sk, sv, gf, gs) = _mix_fwd(
        x1, g_mix, w_in, bias, qg, kg, "mix_fwd")
    f_wide = jnp.repeat(f_col[:, :N_HEADS], HEAD_DIM, axis=1)
    cap = _fox_qk_cap(w['q_norm'], w['k_norm'])
    y_fox, lse, got2 = _fox_fwd(fqn, fkn, fv, f_wide, f_row, cap, "fox_fwd", ride=(_pack(w, 2, BF16), True))
    y_sb, y_sb32, got3 = _sb_fwd(sq, sk, sv, "sb_fwd", ride=(_pack(w, 3, BF16), True))
    wpg = _gathered(got3, "w_ple_gate", (D_MODEL, D_MODEL))
    wpp = _gathered(got3, "w_ple_proj", (D_MODEL, PLE_DIM))
    ffn2_bufs = (got2, got2, got3)
    x2 = _merge_fwd(x1, y_fox, y_sb, gf, gs, wbf, wbs, wo, "merge_fwd")
    x3, g2, u2, h2, = _ffn_fwd(x2, g_ffn2, ffn2_bufs, ffn2, "ffn2_fwd")
    dx3, n_ple, ds_ple, dpp, dg_ple, loss = _ple_loss(x3, p, tgt, g_ple, wpg, wpp, "ple_loss")

    grads['w_ple_gate'] = _wgrad(n_ple, ds_ple, "dw_ple_gate", D_MODEL, D_MODEL)
    grads['w_ple_proj'] = _wgrad(dpp, p, "dw_ple_proj", D_MODEL, PLE_DIM)
    dg2, du2, act2, dx2, dg_ffn2 = _ffn_bwd_fused(x2, dx3, g_ffn2, g2, u2, ffn2_bufs, ffn2, "ffn2_bwd")
    grads['ffn2_w_gate'] = _wgrad(dg2, h2, "dw_ffn2_gate", half, D_MODEL)
    grads['ffn2_w_up'] = _wgrad(du2, h2, "dw_ffn2_up", half, D_MODEL)
    grads['ffn2_w_down'] = _wgrad(act2, dx3, "dw_ffn2_down", half, D_MODEL)
    dyf, dys, dgf, dgs, dbf, dbs, merged = _merge_bwd(dx2, y_fox, y_sb, gf, gs, wbf, wbs, wo, "merge_bwd")
    grads['w_branch_fox'] = _wgrad(dbf, y_fox, "dw_branch_fox", D_MODEL, ATT_W)
    grads['w_branch_sb'] = _wgrad(dbs, y_sb, "dw_branch_sb", D_MODEL, ATT_W)
    grads['w_out'] = _wgrad(merged, dx2, "dw_out", D_MODEL, D_MODEL)
    dfqn, dfkn_t, dfv_t, dft, part_rest = _fox_bwd(fqn, fkn, fv, dyf, y_fox, lse, f_wide, f_row, cap, "fox_bwd",
                                               ride=(_pack_chunks(grads, 5), False))
    dsq, dsk_t, dsv_t = _sb_bwd(sq, sk, sv, dys, y_sb32, "sb_bwd")
    s_len = x.shape[0]
    df_col = jnp.pad(dft[:, :2, :].reshape(N_HEADS, s_len).T, ((0, 0), (0, LANES - N_HEADS)))
    dproj, dx1, dg_mix, dqg, dkg, dbias = _mix_bwd(
        x1, dx2, g_mix, w_in, fqr, fkr, dfqn, dfkn_t, qg, kg, dfv_t, df_col, logf, dsq, dsk_t, dsv_t, dgf, dgs,
        "mix_bwd")
    grads['w_in'] = _wgrad(dproj, hmix, "dw_in", IN_PAD // 3, D_MODEL)
    dg1, du1, act1, dx0, dg_ffn1, part_in0 = _ffn_bwd_fused(x, dx1, g_ffn1, g1, u1, (got0,) * 3, ffn1, "ffn1_bwd",
                                                            ride=(_pack_chunks(grads, 3), False))
    grads['ffn1_w_gate'], part_in1 = _wgrad(dg1, h1, "dw_ffn1_gate", half, D_MODEL,
                                            ride=(_pack_chunks(grads, 4), False))
    grads['ffn1_w_up'], part_gate = _wgrad(du1, h1, "dw_ffn1_up", half, D_MODEL, ride=(_pack_chunks(grads, 0), False))
    grads['ffn1_w_down'], part_up = _wgrad(act1, dx1, "dw_ffn1_down", half, D_MODEL,
                                           ride=(_pack_chunks(grads, 1), False))
    part_down = _exchange(_pack_chunks(grads, 2), "scatter_ffn1_down", False)

    fold = lambda a: a.reshape(N_HEADS, HEAD_DIM).sum(axis=0).reshape(1, HEAD_DIM)
    small_g = {'ffn1_norm': dg_ffn1, 'mix_norm': dg_mix, 'ffn2_norm': dg_ffn2, 'ple_norm': dg_ple,
               'q_norm': fold(dqg), 'k_norm': fold(dkg), 'forget_bias': dbias[:, :N_HEADS]}
    return loss[0, 0], dx0, (part_gate, part_up, part_down, part_in0, part_in1, part_rest), small_g


def kernel(x, p, ffn1_norm, ffn1_w_gate, ffn1_w_up, ffn1_w_down, mix_norm, w_in, forget_bias, q_norm, k_norm, w_branch_fox, w_branch_sb, w_out, ffn2_norm, ffn2_w_gate, ffn2_w_up, ffn2_w_down, ple_norm, w_ple_gate, w_ple_proj, loss_target, m_ffn1_norm, m_ffn1_w_gate, m_ffn1_w_up, m_ffn1_w_down, m_mix_norm, m_w_in, m_forget_bias, m_q_norm, m_k_norm, m_w_branch_fox, m_w_branch_sb, m_w_out, m_ffn2_norm, m_ffn2_w_gate, m_ffn2_w_up, m_ffn2_w_down, m_ple_norm, m_w_ple_gate, m_w_ple_proj, v_ffn1_norm, v_ffn1_w_gate, v_ffn1_w_up, v_ffn1_w_down, v_mix_norm, v_w_in, v_forget_bias, v_q_norm, v_k_norm, v_w_branch_fox, v_w_branch_sb, v_w_out, v_ffn2_norm, v_ffn2_w_gate, v_ffn2_w_up, v_ffn2_w_down, v_ple_norm, v_w_ple_gate, v_w_ple_proj):
    args = dict(locals())
    w = {n: args[n][0] for n in WEIGHT_NAMES}
    m = {n: args["m_" + n][0] for n in WEIGHT_NAMES}
    v = {n: args["v_" + n][0] for n in WEIGHT_NAMES}
    loss, dx, parts, small_g = _step(x[0], p[0, 0], loss_target[0], w)

    group_of = {n: grp for grp, members in enumerate(SCATTER_GROUPS) for n in members}
    direct = [n for n in WEIGHT_NAMES if n.startswith("ffn") and n not in SMALL_NAMES]
    summed = {}
    for grp, part in enumerate(parts):
        if not all(n in direct for n in SCATTER_GROUPS[grp]):
            s = _sum_parts(part, f"sum_grads_{grp}", SUM_TILE_ROWS[grp])
            summed.update({n: s for n in SCATTER_GROUPS[grp]})
    big = {}
    for n in WEIGHT_NAMES:
        if n in direct:
            big[n] = tuple(_adamw_from_parts(parts[group_of[n]], SCATTER_OFF[n], args[n], args["m_" + n],
                                             args["v_" + n], "adamw_" + n, n in TRANSPOSED))
        elif n not in SMALL_NAMES:
            g = _shard_grad(summed, n, w[n].shape)
            big[n] = (g,) + tuple(_adamw_shard(g, w[n], m[n], v[n], "adamw_" + n))
    small_parts = _exchange(_pack_small(small_g, loss), "gather_small", True)
    small, total_loss = _adamw_small(small_parts, *({n: args[pre + n] for n in SMALL_NAMES} for pre in ("", "m_", "v_")),
                                     "adamw_small")
    big.update(small)

    outs = [total_loss.reshape(()), dx.reshape(x.shape)]
    for kind in range(4):
        outs += [big[n][kind].reshape(args[n].shape) for n in WEIGHT_NAMES]
    return tuple(outs)
```

```python
import jax
import jax.numpy as jnp
from jax import lax
from jax.experimental import pallas as pl
from jax.experimental.pallas import tpu as pltpu

F32 = jnp.float32
BF16 = jnp.bfloat16

D_MODEL = 1024
D_FF = 2816
N_HEADS = 8
HEAD_DIM = 64
ATT_W = N_HEADS * HEAD_DIM
PLE_DIM = 256
EPS = 1e-6
N_DEV = 8
MESH = pl.DeviceIdType.MESH

LANES = 128
V7X_SCOPED_VMEM_BYTES = 56 * 1024 * 1024

C_FQ, C_FK, C_FV, C_FL = 0, 512, 1024, 1536
C_SQ, C_SK, C_SV, C_GF, C_GS = 1792, 2304, 2816, 3328, 4352
IN_PAD = 5376
IN_REAL = 5128
FL_REAL_END = 1544

ADAM_LR = 0.001
ADAM_B1 = 0.9
ADAM_B2 = 0.999
ADAM_EPS = 1e-08
ADAM_WD = 0.01
ADAM_STEP = 10

PACK_ROWS = {"ffn1_w_gate": 352, "ffn1_w_up": 352, "ffn1_w_down": 352, "w_in": 656, "w_branch_fox": 64,
             "w_branch_sb": 64, "w_out": 128, "ffn2_w_gate": 352, "ffn2_w_up": 352, "ffn2_w_down": 352,
             "w_ple_gate": 128, "w_ple_proj": 32}
GATHER_GROUPS = (
    ("ffn1_w_gate", "ffn1_w_up", "ffn1_w_down"),
    ("w_in", "w_branch_fox", "w_branch_sb", "w_out"),
    ("ffn2_w_gate", "ffn2_w_up"),
    ("ffn2_w_down", "w_ple_gate", "w_ple_proj"),
)
W_IN_HALVES = ((0, 336), (336, 656))
PACK_ROWS.update({f"w_in#{i}": hi - lo for i, (lo, hi) in enumerate(W_IN_HALVES)})
SCATTER_GROUPS = (
    ("ffn1_w_gate",), ("ffn1_w_up",), ("ffn1_w_down",),
    ("w_in#0",), ("w_in#1",),
    ("ffn2_w_gate", "ffn2_w_up", "ffn2_w_down", "w_ple_gate", "w_ple_proj", "w_branch_fox", "w_branch_sb", "w_out"),
)
SUM_TILE_ROWS = (352, 352, 352, 336, 320, 368)


def _offsets(groups):
    off = {}
    for grp in groups:
        o = 0
        for n in grp:
            off[n] = o
            o += PACK_ROWS[n]
    return off


GATHER_OFF = _offsets(GATHER_GROUPS)
SCATTER_OFF = _offsets(SCATTER_GROUPS)
W_IN_ROWS = 641

SMALL_ROWS = 8


def _cparams(*sem):
    return pltpu.CompilerParams(dimension_semantics=sem, vmem_limit_bytes=V7X_SCOPED_VMEM_BYTES)


def _dot(a, b):
    return jnp.dot(a, b, preferred_element_type=F32)


def _dot_nt(a, b):
    return lax.dot_general(a, b, (((1,), (1,)), ((), ())), preferred_element_type=F32)


def _dot_tn(a, b):
    return lax.dot_general(a, b, (((0,), (0,)), ((), ())), preferred_element_type=F32)


def _split(x, parts):
    out = []
    r = x
    for _ in range(parts):
        p = r.astype(BF16)
        out.append(p)
        r = r - p.astype(F32)
    return out


def _dot_split(x, m, parts):
    acc = None
    for p in _split(x, parts):
        t = _dot(p, m)
        acc = t if acc is None else acc + t
    return acc


def _dot_split_left(m, x, parts):
    acc = None
    for p in _split(x, parts):
        t = _dot(m, p)
        acc = t if acc is None else acc + t
    return acc


def _rms_rinv(xf):
    return lax.rsqrt(jnp.mean(xf * xf, axis=-1, keepdims=True) + EPS)


def _sigmoid(x):
    return 1.0 / (1.0 + jnp.exp(-x))


def _softplus_neg_abs(z):
    return jnp.log(1.0 + jnp.exp(-jnp.abs(z)))


FFN_SHARD = D_FF // N_DEV
FFN_CHUNK = 4


def _ffn_w_spec(blk, index_map):
    return pl.BlockSpec((FFN_CHUNK, FFN_SHARD, D_MODEL), lambda *g: (index_map(*g), blk, 0))


def _ffn_w(ref):
    return ref[...].reshape(FFN_CHUNK * FFN_SHARD, D_MODEL)


def _ffn_fwd(x, gain, wbufs, blks, name, ride=None):
    s_len = x.shape[0]
    ts = min(512, s_len)
    fc = FFN_CHUNK * FFN_SHARD
    nt, nc = s_len // ts, D_FF // fc

    def body(x_ref, gain_ref, wg_ref, wu_ref, wd_ref, y_ref, g_ref, u_ref, h_ref, acc_scr):
        j = pl.program_id(1)

        @pl.when(j == 0)
        def _():
            xf = x_ref[...]
            h_ref[...] = ((xf * _rms_rinv(xf)) * gain_ref[...]).astype(BF16)
            acc_scr[...] = jnp.zeros_like(acc_scr)

        h = h_ref[...]
        g = _dot_nt(h, _ffn_w(wg_ref))
        u = _dot_nt(h, _ffn_w(wu_ref))
        g_ref[...] = g.astype(BF16)
        u_ref[...] = u.astype(BF16)
        a = (g * _sigmoid(g) * u).astype(BF16)
        acc_scr[...] += _dot(a, _ffn_w(wd_ref))

        @pl.when(j == nc - 1)
        def _():
            y_ref[...] = x_ref[...] + 0.5 * acc_scr[...]

    tok = pl.BlockSpec((ts, D_MODEL), lambda i, j: (i, 0))
    hid = pl.BlockSpec((ts, fc), lambda i, j: (i, j))
    return _ride_call(
        body, name, (nt, nc),
        [tok, pl.BlockSpec((1, D_MODEL), lambda i, j: (0, 0))] + [_ffn_w_spec(b, lambda i, j: j) for b in blks],
        [tok, hid, hid, tok],
        [jax.ShapeDtypeStruct((s_len, D_MODEL), F32), jax.ShapeDtypeStruct((s_len, D_FF), BF16),
         jax.ShapeDtypeStruct((s_len, D_FF), BF16), jax.ShapeDtypeStruct((s_len, D_MODEL), BF16)],
        [pltpu.VMEM((ts, D_MODEL), F32)], ("parallel", "arbitrary"), (x, gain, *wbufs), ride)


def _ffn_bwd_fused(x, dy, gain, g, u, wbufs, blks, name, ride=None):
    s_len = x.shape[0]
    ts = min(512, s_len)
    fc = FFN_CHUNK * FFN_SHARD
    nt = s_len // ts
    assert D_FF == 2 * fc

    def hidden(dy_ref, g_ref, u_ref, wg_ref, wu_ref, wd_ref, dg_ref, du_ref, act_ref):
        da = 0.5 * _dot_nt(dy_ref[...].astype(BF16), _ffn_w(wd_ref))
        gf = g_ref[...].astype(F32)
        uf = u_ref[...].astype(F32)
        sg = _sigmoid(gf)
        silu = gf * sg
        dg = (da * uf * (sg * (1.0 + gf * (1.0 - sg)))).astype(BF16)
        du = (da * silu).astype(BF16)
        dg_ref[...] = dg
        du_ref[...] = du
        act_ref[...] = (0.5 * silu * uf).astype(BF16)
        return _dot(dg, _ffn_w(wg_ref)) + _dot(du, _ffn_w(wu_ref))

    def first(dy_ref, g_ref, u_ref, wg_ref, wu_ref, wd_ref, dg_ref, du_ref, act_ref, dh_ref):
        dh_ref[...] = hidden(dy_ref, g_ref, u_ref, wg_ref, wu_ref, wd_ref, dg_ref, du_ref, act_ref)

    def second(x_ref, dy_ref, gain_ref, g_ref, u_ref, wg_ref, wu_ref, wd_ref, dh0_ref, dg_half, du_half, act_half,
               dg_ref, du_ref, act_ref, dx_ref, dgain_ref):
        i = pl.program_id(0)
        dh = dh0_ref[...] + hidden(dy_ref, g_ref, u_ref, wg_ref, wu_ref, wd_ref, dg_ref, du_ref, act_ref)
        xf = x_ref[...]
        r = _rms_rinv(xf)
        xhat = xf * r
        dgp = jnp.sum(dh * xhat, axis=0, keepdims=True)

        @pl.when(i == 0)
        def _():
            dgain_ref[...] = dgp

        @pl.when(i > 0)
        def _():
            dgain_ref[...] += dgp

        dn = dh * gain_ref[...]
        dx_ref[...] = dy_ref[...] + r * (dn - xhat * jnp.mean(dn * xhat, axis=-1, keepdims=True))

    tok = pl.BlockSpec((ts, D_MODEL), lambda i: (i, 0))
    row = pl.BlockSpec((1, D_MODEL), lambda i: (0, 0))
    hid = lambda c: pl.BlockSpec((ts, fc), lambda i: (i, c))
    wts = lambda c: [pl.BlockSpec((FFN_CHUNK, FFN_SHARD, D_MODEL), lambda i, b=b: (c, b, 0),
                                  pipeline_mode=pl.Buffered(1)) for b in blks]
    hidden_shapes = [jax.ShapeDtypeStruct((s_len, D_FF), BF16)] * 3
    dg, du, act, dh0, *rode = _ride_call(
        first, name + "_a", (nt,), [tok, hid(0), hid(0)] + wts(0), [hid(0)] * 3 + [tok],
        hidden_shapes + [jax.ShapeDtypeStruct((s_len, D_MODEL), F32)], [], ("parallel",),
        (dy, g, u, *wbufs), ride)
    filled = pl.BlockSpec(memory_space=pl.ANY)
    dg, du, act, dx, dgain = pl.pallas_call(
        second, name=name + "_b", grid=(nt,),
        in_specs=[tok, tok, row, hid(1), hid(1)] + wts(1) + [tok, filled, filled, filled],
        out_specs=[hid(1)] * 3 + [tok, row],
        out_shape=hidden_shapes + [jax.ShapeDtypeStruct((s_len, D_MODEL), F32), jax.ShapeDtypeStruct((1, D_MODEL), F32)],
        input_output_aliases={9: 0, 10: 1, 11: 2},
        compiler_params=_cparams("arbitrary"),
    )(x, dy, gain, g, u, *wbufs, dh0, dg, du, act)
    return (dg, du, act, dx, dgain, *rode)


def _wgrad(a, b, name, tk, tn, ride=None):
    s_len, k_dim = a.shape
    n_dim = b.shape[1]
    ts = min(2048, s_len)
    ns = s_len // ts

    def body(a_ref, b_ref, o_ref, acc):
        s = pl.program_id(2)

        @pl.when(s == 0)
        def _():
            acc[...] = jnp.zeros_like(acc)

        acc[...] += _dot_tn(a_ref[...].astype(BF16), b_ref[...].astype(BF16))

        @pl.when(s == ns - 1)
        def _():
            o_ref[...] = acc[...].astype(BF16)

    out = _ride_call(
        body, name, (k_dim // tk, n_dim // tn, ns),
        [pl.BlockSpec((ts, tk), lambda k, n, s: (s, k)), pl.BlockSpec((ts, tn), lambda k, n, s: (s, n))],
        [pl.BlockSpec((tk, tn), lambda k, n, s: (k, n))], [jax.ShapeDtypeStruct((k_dim, n_dim), BF16)],
        [pltpu.VMEM((tk, tn), F32)], ("parallel", "parallel", "arbitrary"), (a, b), ride)
    return out[0] if ride is None else tuple(out)


HEAD_SUM_PARTS = 1


def _head_group_matrix():
    r = lax.broadcasted_iota(jnp.int32, (ATT_W, ATT_W), 0) // HEAD_DIM
    c = lax.broadcasted_iota(jnp.int32, (ATT_W, ATT_W), 1) // HEAD_DIM
    return (r == c).astype(BF16)


def _mix_fwd(x, gain, w_in, bias, qg, kg, name):
    s_len = x.shape[0]
    ts = min(512, s_len)
    nt = s_len // ts
    gmat = _head_group_matrix()

    def body(x_ref, gain_ref, w_ref, bias_ref, qg_ref, kg_ref, gm_ref,
             h_ref, fqr_ref, fkr_ref, fqn_ref, fkn_ref, fv_ref, logf_ref, f_ref, ft_ref,
             sq_ref, sk_ref, sv_ref, gf_ref, gs_ref, carry):
        i = pl.program_id(0)
        xf = x_ref[...]
        h = ((xf * _rms_rinv(xf)) * gain_ref[...]).astype(BF16)
        h_ref[...] = h
        gm = gm_ref[...]

        def proj(lo, n):
            return _dot_nt(h, w_ref[lo:lo + n, :])

        def headnorm(raw, g):
            ms = _dot_split(raw * raw, gm, HEAD_SUM_PARTS) * (1.0 / HEAD_DIM)
            return ((raw * lax.rsqrt(ms + EPS)) * g).astype(BF16)

        fq = proj(C_FQ, ATT_W)
        fqr_ref[...] = fq
        fqn_ref[...] = headnorm(fq, qg_ref[...])
        fk = proj(C_FK, ATT_W)
        fkr_ref[...] = fk
        fkn_ref[...] = headnorm(fk, kg_ref[...])
        fv_ref[...] = proj(C_FV, ATT_W).astype(BF16)
        sq_ref[...] = proj(C_SQ, ATT_W).astype(BF16)
        sk_ref[...] = proj(C_SK, ATT_W).astype(BF16)
        sv_ref[...] = proj(C_SV, ATT_W).astype(BF16)
        gf_ref[...] = proj(C_GF, D_MODEL)
        gs_ref[...] = proj(C_GS, D_MODEL)

        fl = proj(C_FL, LANES) + bias_ref[...]
        lane = lax.broadcasted_iota(jnp.int32, fl.shape, 1)
        logf = jnp.where(lane < N_HEADS, jnp.minimum(fl, 0.0) - _softplus_neg_abs(fl), 0.0)
        logf_ref[...] = logf

        @pl.when(i == 0)
        def _():
            carry[...] = jnp.zeros_like(carry)

        r = lax.broadcasted_iota(jnp.int32, (ts, ts), 0)
        c = lax.broadcasted_iota(jnp.int32, (ts, ts), 1)
        tri = (r >= c).astype(BF16)
        f_tile = _dot_split_left(tri, logf, 3) + carry[...]
        f_ref[...] = f_tile
        ft_ref[...] = f_tile.T[:N_HEADS, :]
        carry[...] = f_tile[ts - 1:ts, :]

    tok = lambda w: pl.BlockSpec((ts, w), lambda i: (i, 0))
    full = lambda a: pl.BlockSpec(a.shape, lambda i: (0, 0), pipeline_mode=pl.Buffered(1))
    f32o = lambda w: jax.ShapeDtypeStruct((s_len, w), F32)
    b16o = lambda w: jax.ShapeDtypeStruct((s_len, w), BF16)
    return _ride_call(
        body, name, (nt,),
        [tok(D_MODEL), full(gain), full(w_in), full(bias), full(qg), full(kg), full(gmat)],
        [
            tok(D_MODEL), tok(ATT_W), tok(ATT_W), tok(ATT_W), tok(ATT_W), tok(ATT_W), tok(LANES), tok(LANES),
            pl.BlockSpec((N_HEADS, ts), lambda i: (0, i)),
            tok(ATT_W), tok(ATT_W), tok(ATT_W), tok(D_MODEL), tok(D_MODEL),
        ],
        [
            b16o(D_MODEL), f32o(ATT_W), f32o(ATT_W), b16o(ATT_W), b16o(ATT_W), b16o(ATT_W), f32o(LANES), f32o(LANES),
            jax.ShapeDtypeStruct((N_HEADS, s_len), F32),
            b16o(ATT_W), b16o(ATT_W), b16o(ATT_W), f32o(D_MODEL), f32o(D_MODEL),
        ],
        [pltpu.VMEM((1, LANES), F32)], ("arbitrary",), (x, gain, w_in, bias, qg, kg, gmat), None)


ATT_T = 256
SB_ROWS = 2
FOX_ROWS = 2
EXP_ZERO = 88.0


def _att_tiling(s_len, rows):
    t = min(ATT_T, s_len)
    nr = min(rows, s_len // t)
    return t, nr, s_len // (t * nr)


def _pair_specs(s_len, tq):
    qblk = pl.BlockSpec((tq, LANES), lambda hp, i: (i, hp))
    kvfull = pl.BlockSpec((s_len, LANES), lambda hp, i: (0, hp))
    return qblk, kvfull


def _walk_tiles(i, nr, load, sub, flush, more=None, trips=None):
    base = i * nr
    for kk in range(nr - 1, -1, -1):
        rs = list(range(kk, nr))
        flush(base + kk, sub(rs, load(base + kk), [r == kk for r in rs]))

    done = jnp.int32(0)
    for last in range(nr - 1, -1, -1):
        rs = list(range(last + 1))

        def visit(n, rs=rs):
            kb = base - 1 - n
            flush(kb, sub(rs, load(kb), [False] * len(rs)))

        if trips is not None:
            todo = jnp.maximum(trips(base, last) - done, 0)

            def body(it, carry, start=done, visit=visit):
                visit(start + it)
                return carry

            lax.fori_loop(0, todo, body, jnp.int32(0))
            done = done + todo
        else:
            def step(state, visit=visit, last=last):
                visit(state[0])
                return state[0] + 1, more(last)

            done, _ = lax.while_loop(lambda state: jnp.logical_and(state[0] < base, state[1] > 0), step,
                                     (done, more(last)))


def _stack(parts):
    return parts[0] if len(parts) == 1 else jnp.concatenate(parts, axis=0)


def _stacked_halves(x, lo):
    z = jnp.zeros_like(x)
    return jnp.concatenate([jnp.where(lo, x, z), jnp.where(lo, z, x)], axis=0)


def _fox_qk_cap(q_gain, k_gain):
    cap = (HEAD_DIM ** 0.5) * jnp.max(jnp.abs(q_gain)) * jnp.max(jnp.abs(k_gain))
    return (cap * 1.01 + 1.0).reshape(1).astype(F32)


def _fox_trips(hp, flast_ref, cap_ref, fq, level):
    def trips(base, r):
        gap = [jnp.max(fq[r][j] - level(r, j)) + cap_ref[0] for j in (0, 1)]

        def needed(n):
            kb = jnp.maximum(base - 1 - n, 0)
            return jnp.logical_or(gap[0] - flast_ref[2 * hp, kb] > -EXP_ZERO,
                                  gap[1] - flast_ref[2 * hp + 1, kb] > -EXP_ZERO)

        return lax.while_loop(lambda n: jnp.logical_and(n < base, needed(n)), lambda n: n + 1, jnp.int32(0))
    return trips


def _fox_fwd(q, k, v, f_wide, f_row, cap, name, ride=None):
    s_len = q.shape[0]
    t, nr, nq = _att_tiling(s_len, FOX_ROWS)

    def body(q_ref, k_ref, v_ref, f_ref, ft_ref, cap_ref, fl_ref, y_ref, lse_ref, m_ref, l_ref, acc_ref):
        hp = pl.program_id(0)
        i = pl.program_id(1)
        lane = lax.broadcasted_iota(jnp.int32, (t, LANES), 1)
        lo = lane < HEAD_DIM
        causal = lax.broadcasted_iota(jnp.int32, (t, t), 0) >= lax.broadcasted_iota(jnp.int32, (t, t), 1)
        rows = [pl.ds(r * t, t) for r in range(nr)]
        slab = lambda r, j: pl.ds((2 * r + j) * t, t)
        qst = [_stacked_halves(q_ref[rw, :] * jnp.asarray(HEAD_DIM ** -0.5, BF16), lo) for rw in rows]
        q_all = _stack(qst)
        fq = [[f_ref[rw, j * HEAD_DIM:j * HEAD_DIM + 1] for j in (0, 1)] for rw in rows]
        m_ref[...] = jnp.full(m_ref.shape, -1e30, F32)
        l_ref[...] = jnp.zeros_like(l_ref)
        acc_ref[...] = jnp.zeros_like(acc_ref)

        def load(kb):
            k0 = pl.multiple_of(kb * t, t)
            frow = [ft_ref[pl.ds(2 * hp + j, 1), pl.ds(k0, t)] for j in (0, 1)]
            return k_ref[pl.ds(k0, t), :], v_ref[pl.ds(k0, t), :], frow

        def sub(rs, tiles, masked):
            kblk, vblk, frow = tiles
            z = _dot_nt(q_all if len(rs) == nr else _stack([qst[r] for r in rs]), kblk)
            slabs = [(r, j) for r in range(len(rs)) for j in (0, 1)]
            ps, alpha = [], []
            for n, (r, j) in enumerate(slabs):
                sl = slab(rs[r], j)
                m_old = m_ref[sl, :]
                s = z[n * t:(n + 1) * t, :] + (fq[rs[r]][j] - frow[j])
                if masked[r]:
                    s = jnp.where(causal, s, -1e30)
                mj = jnp.maximum(m_old, jnp.max(s, axis=1, keepdims=True))
                aj = jnp.exp(m_old - mj)
                p = jnp.exp(s - jnp.tile(mj, (1, t // LANES)))
                m_ref[sl, :] = mj
                l_ref[sl, :] = aj * l_ref[sl, :] + jnp.sum(p, axis=1, keepdims=True)
                alpha.append(aj)
                ps.append(p.astype(BF16))
            pv = _dot(_stack(ps), vblk)
            for n, (r, j) in enumerate(slabs):
                sl = slab(rs[r], j)
                acc_ref[sl, :] = acc_ref[sl, :] * alpha[n] + pv[n * t:(n + 1) * t, :]
            return None

        trips = _fox_trips(hp, fl_ref, cap_ref, fq, lambda r, j: m_ref[slab(r, j), :])
        _walk_tiles(i, nr, load, sub, lambda kb, side: None, trips=trips)
        for r, rw in enumerate(rows):
            l0, l1 = l_ref[slab(r, 0), :], l_ref[slab(r, 1), :]
            y_ref[rw, :] = jnp.where(lo, acc_ref[slab(r, 0), :] / l0, acc_ref[slab(r, 1), :] / l1).astype(BF16)
            lse_ref[0, rw, :] = jnp.where(lo, m_ref[slab(r, 0), :] + jnp.log(l0), m_ref[slab(r, 1), :] + jnp.log(l1))

    state = [pltpu.VMEM((2 * nr * t, LANES), F32)] * 3
    qblk, kvfull = _pair_specs(s_len, t * nr)
    return _ride_call(
        body, name, (N_HEADS // 2, nq),
        [qblk, kvfull, kvfull,
         qblk, pl.BlockSpec((N_HEADS, s_len), lambda hp, i: (0, 0)),
         pl.BlockSpec(memory_space=pltpu.SMEM), pl.BlockSpec(memory_space=pltpu.SMEM)],
        [qblk, pl.BlockSpec((1, t * nr, LANES), lambda hp, i: (hp, i, 0))],
        [jax.ShapeDtypeStruct((s_len, ATT_W), BF16), jax.ShapeDtypeStruct((N_HEADS // 2, s_len, LANES), F32)],
        state, ("parallel", "parallel"), (q, k, v, f_wide, f_row, cap, f_row[:, t - 1::t]), ride)


def _fox_bwd(q, k, v, dy, y, lse, f_wide, f_row, cap, name, ride=None):
    s_len = q.shape[0]
    t, nr, nq = _att_tiling(s_len, FOX_ROWS)

    def body(q_ref, k_ref, v_ref, dy_ref, y_ref, lse_ref, f_ref, ft_ref, cap_ref, fl_ref,
             dq_ref, dkt_ref, dvt_ref, dft_ref, dqs_ref, rsum_ref):
        hp = pl.program_id(0)
        i = pl.program_id(1)

        @pl.when(i == 0)
        def _():
            dkt_ref[...] = jnp.zeros_like(dkt_ref)
            dvt_ref[...] = jnp.zeros_like(dvt_ref)
            dft_ref[...] = jnp.zeros_like(dft_ref)

        dqs_ref[...] = jnp.zeros_like(dqs_ref)
        rsum_ref[...] = jnp.zeros_like(rsum_ref)
        slab = lambda r, j: pl.ds((2 * r + j) * t, t)

        lane = lax.broadcasted_iota(jnp.int32, (t, LANES), 1)
        lo = lane < HEAD_DIM
        causal = lax.broadcasted_iota(jnp.int32, (t, t), 0) >= lax.broadcasted_iota(jnp.int32, (t, t), 1)
        rows = [pl.ds(r * t, t) for r in range(nr)]
        qst, dyst, delta, lse, fq = [], [], [], [], []
        for rw in rows:
            qst.append(_stacked_halves(q_ref[rw, :] * jnp.asarray(HEAD_DIM ** -0.5, BF16), lo))
            dyb = dy_ref[rw, :]
            dyst.append(_stacked_halves(dyb, lo))
            prod = dyb.astype(F32) * y_ref[rw, :].astype(F32)
            delta.append([jnp.sum(jnp.where(lo, prod, 0.0), axis=1, keepdims=True),
                          jnp.sum(jnp.where(lo, 0.0, prod), axis=1, keepdims=True)])
            lse_b = lse_ref[0, rw, :]
            lse.append([lse_b[:, 0:1], lse_b[:, HEAD_DIM:HEAD_DIM + 1]])
            fq.append([f_ref[rw, j * HEAD_DIM:j * HEAD_DIM + 1] for j in (0, 1)])

        q_all, dy_all = _stack(qst), _stack(dyst)
        q_all_t, dy_all_t = q_all.T, dy_all.T

        def load(kb):
            k0 = pl.multiple_of(kb * t, t)
            frow = [ft_ref[pl.ds(2 * hp + j, 1), pl.ds(k0, t)] for j in (0, 1)]
            return k_ref[pl.ds(k0, t), :], v_ref[pl.ds(k0, t), :], frow

        def sub(rs, tiles, masked):
            kblk, vblk, frow = tiles
            qs, dys = (q_all, dy_all) if len(rs) == nr else (_stack([qst[r] for r in rs]), _stack([dyst[r] for r in rs]))
            cols = slice(2 * rs[0] * t, 2 * (rs[-1] + 1) * t)
            z = _dot_nt(qs, kblk)
            dp = _dot_nt(dys, vblk)
            slabs = [(r, j) for r in range(len(rs)) for j in (0, 1)]
            pb, dsb, col = [], [], [None, None]
            for n, (r, j) in enumerate(slabs):
                sl = slice(n * t, (n + 1) * t)
                s = z[sl, :] + (fq[rs[r]][j] - frow[j])
                p = jnp.exp(s - lse[rs[r]][j])
                if masked[r]:
                    p = jnp.where(causal, p, 0.0)
                ds = p * (dp[sl, :] - delta[rs[r]][j])
                c = jnp.sum(ds, axis=0, keepdims=True)
                col[j] = c if col[j] is None else col[j] + c
                rsum_ref[slab(rs[r], j), :] += jnp.sum(ds, axis=1, keepdims=True)
                pb.append(p.astype(BF16))
                dsb.append(ds.astype(BF16))
            p_all, ds_all = _stack(pb), _stack(dsb)
            dqs_ref[pl.ds(2 * rs[0] * t, len(slabs) * t), :] += _dot(ds_all, kblk)
            return _dot(q_all_t[:, cols], ds_all), _dot(dy_all_t[:, cols], p_all), col

        def flush(kb, side):
            k0 = pl.multiple_of(kb * t, t)
            dkt_ref[:, pl.ds(k0, t)] += side[0]
            dvt_ref[:, pl.ds(k0, t)] += side[1]
            for j in (0, 1):
                dft_ref[0, pl.ds(j, 1), pl.ds(k0, t)] -= side[2][j]

        trips = _fox_trips(hp, fl_ref, cap_ref, fq, lambda r, j: lse[r][j])
        _walk_tiles(i, nr, load, sub, flush, trips=trips)
        for r, rw in enumerate(rows):
            dq_ref[rw, :] = jnp.where(lo, dqs_ref[slab(r, 0), :], dqs_ref[slab(r, 1), :]) * (HEAD_DIM ** -0.5)
            rs_t = jnp.where(lo, rsum_ref[slab(r, 0), :], rsum_ref[slab(r, 1), :]).T
            q0 = pl.multiple_of((i * nr + r) * t, t)
            for j in (0, 1):
                dft_ref[0, pl.ds(j, 1), pl.ds(q0, t)] += rs_t[j * HEAD_DIM:j * HEAD_DIM + 1, :]

    state = [pltpu.VMEM((2 * nr * t, LANES), F32), pltpu.VMEM((2 * nr * t, 1), F32)]
    qblk, kvfull = _pair_specs(s_len, t * nr)
    kvfull_t = pl.BlockSpec((LANES, s_len), lambda hp, i: (hp, 0))
    return _ride_call(
        body, name, (N_HEADS // 2, nq),
        [qblk, kvfull, kvfull, qblk, qblk,
         pl.BlockSpec((1, t * nr, LANES), lambda hp, i: (hp, i, 0)),
         qblk, pl.BlockSpec((N_HEADS, s_len), lambda hp, i: (0, 0)),
         pl.BlockSpec(memory_space=pltpu.SMEM), pl.BlockSpec(memory_space=pltpu.SMEM)],
        [qblk, kvfull_t, kvfull_t, pl.BlockSpec((1, 8, s_len), lambda hp, i: (hp, 0, 0))],
        [jax.ShapeDtypeStruct((s_len, ATT_W), F32)] + [jax.ShapeDtypeStruct((ATT_W, s_len), F32)] * 2
        + [jax.ShapeDtypeStruct((N_HEADS // 2, 8, s_len), F32)],
        state, ("arbitrary", "arbitrary"), (q, k, v, dy, y, lse, f_wide, f_row, cap, f_row[:, t - 1::t]), ride)


def _sb_more(c_ref, t):
    def more(r):
        return (jnp.max(c_ref[pl.ds(2 * r * t, 2 * t), :]) > -EXP_ZERO).astype(jnp.int32)
    return more


def _stacked_split_dot(slabs, m, parts):
    split = [_split(x, parts) for x in slabs]
    acc = None
    for p in range(parts):
        d = _dot(_stack([s[p] for s in split]), m)
        acc = d if acc is None else acc + d
    return acc


def _sb_weights(z, c, strict, upper, t):
    logs = []
    for n in range(z.shape[0] // t):
        zn = z[n * t:(n + 1) * t, :]
        sp = _softplus_neg_abs(zn)
        l1m = jnp.minimum(-zn, 0.0) - sp
        if strict[n] is not None:
            l1m = jnp.where(strict[n], l1m, 0.0)
        logs.append((jnp.minimum(zn, 0.0) - sp, l1m))
    suf = _stacked_split_dot([l1m for _, l1m in logs], upper, 1)
    out = []
    for n, (logb, l1m) in enumerate(logs):
        after = c[n] + suf[n * t:(n + 1) * t, :]
        a = jnp.exp(logb + after)
        if strict[n] is not None:
            a = jnp.where(strict[n], a, 0.0)
        out.append((logb, a, after[:, 0:1] + l1m[:, 0:1]))
    return out


def _sb_fwd(q, k, v, name, ride=None):
    s_len = q.shape[0]
    t, nr, nq = _att_tiling(s_len, SB_ROWS)

    def body(q_ref, k_ref, v_ref, y_ref, yf_ref, c_ref, acc_ref):
        i = pl.program_id(1)
        lane = lax.broadcasted_iota(jnp.int32, (t, LANES), 1)
        lo = lane < HEAD_DIM
        ri = lax.broadcasted_iota(jnp.int32, (t, t), 0)
        ci = lax.broadcasted_iota(jnp.int32, (t, t), 1)
        strict = ci < ri
        upper = (ri > ci).astype(BF16)
        rows = [pl.ds(r * t, t) for r in range(nr)]
        slab = lambda r, j: pl.ds((2 * r + j) * t, t)
        qst = [_stacked_halves(q_ref[rw, :] * jnp.asarray(HEAD_DIM ** -0.5, BF16), lo) for rw in rows]
        q_all = _stack(qst)
        c_ref[...] = jnp.zeros_like(c_ref)
        acc_ref[...] = jnp.zeros_like(acc_ref)

        def load(kb):
            k0 = pl.multiple_of(kb * t, t)
            return k_ref[pl.ds(k0, t), :], v_ref[pl.ds(k0, t), :]

        def sub(rs, tiles, masked):
            kblk, vblk = tiles
            z = _dot_nt(q_all if len(rs) == nr else _stack([qst[r] for r in rs]), kblk)
            slabs = [(r, j) for r in range(len(rs)) for j in (0, 1)]
            w = _sb_weights(z, [c_ref[slab(rs[r], j), :] for r, j in slabs],
                            [strict if masked[r] else None for r, j in slabs], upper, t)
            for n, (r, j) in enumerate(slabs):
                c_ref[slab(rs[r], j), :] = w[n][2]
            acc_ref[pl.ds(2 * rs[0] * t, len(slabs) * t), :] += _dot(_stack([a.astype(BF16) for _, a, _ in w]), vblk)
            return None

        _walk_tiles(i, nr, load, sub, lambda kb, side: None, more=_sb_more(c_ref, t))
        for r, rw in enumerate(rows):
            y = jnp.where(lo, acc_ref[slab(r, 0), :], acc_ref[slab(r, 1), :])
            y_ref[rw, :] = y.astype(BF16)
            yf_ref[rw, :] = y

    qblk, kvfull = _pair_specs(s_len, t * nr)
    return _ride_call(
        body, name, (N_HEADS // 2, nq), [qblk, kvfull, kvfull], [qblk, qblk],
        [jax.ShapeDtypeStruct((s_len, ATT_W), BF16), jax.ShapeDtypeStruct((s_len, ATT_W), F32)],
        [pltpu.VMEM((2 * nr * t, 1), F32), pltpu.VMEM((2 * nr * t, LANES), F32)], ("parallel", "parallel"),
        (q, k, v), ride)


def _sb_bwd(q, k, v, dy, yf, name):
    s_len = q.shape[0]
    t, nr, nq = _att_tiling(s_len, SB_ROWS)

    def body(q_ref, k_ref, v_ref, dy_ref, yf_ref, dq_ref, dkt_ref, dvt_ref, c_ref, e_ref, dqs_ref):
        i = pl.program_id(1)

        @pl.when(i == 0)
        def _():
            dkt_ref[...] = jnp.zeros_like(dkt_ref)
            dvt_ref[...] = jnp.zeros_like(dvt_ref)

        c_ref[...] = jnp.zeros_like(c_ref)
        e_ref[...] = jnp.zeros_like(e_ref)
        dqs_ref[...] = jnp.zeros_like(dqs_ref)
        slab = lambda r, j: pl.ds((2 * r + j) * t, t)

        lane = lax.broadcasted_iota(jnp.int32, (t, LANES), 1)
        lo = lane < HEAD_DIM
        ri = lax.broadcasted_iota(jnp.int32, (t, t), 0)
        ci = lax.broadcasted_iota(jnp.int32, (t, t), 1)
        strict = ci < ri
        upper = (ri > ci).astype(BF16)
        upper_incl = (ri >= ci).astype(BF16)
        rows = [pl.ds(r * t, t) for r in range(nr)]
        qst, dyst, delta = [], [], []
        for rw in rows:
            qst.append(_stacked_halves(q_ref[rw, :] * jnp.asarray(HEAD_DIM ** -0.5, BF16), lo))
            dyb = dy_ref[rw, :]
            dyst.append(_stacked_halves(dyb, lo))
            prod = dyb.astype(F32) * yf_ref[rw, :]
            delta.append([jnp.sum(jnp.where(lo, prod, 0.0), axis=1, keepdims=True),
                          jnp.sum(jnp.where(lo, 0.0, prod), axis=1, keepdims=True)])
        q_all, dy_all = _stack(qst), _stack(dyst)
        q_all_t, dy_all_t = q_all.T, dy_all.T

        def load(kb):
            k0 = pl.multiple_of(kb * t, t)
            return k_ref[pl.ds(k0, t), :], v_ref[pl.ds(k0, t), :]

        def sub(rs, tiles, masked):
            kblk, vblk = tiles
            qs, dys = (q_all, dy_all) if len(rs) == nr else (_stack([qst[r] for r in rs]), _stack([dyst[r] for r in rs]))
            cols = slice(2 * rs[0] * t, 2 * (rs[-1] + 1) * t)
            slabs = [(r, j) for r in range(len(rs)) for j in (0, 1)]
            z = _dot_nt(qs, kblk)
            w = _sb_weights(z, [c_ref[slab(rs[r], j), :] for r, j in slabs],
                            [strict if masked[r] else None for r, j in slabs], upper, t)
            da = _dot_nt(dys, vblk)
            ab = [a.astype(BF16) for _, a, _ in w]
            dl = [ab[n].astype(F32) * da[n * t:(n + 1) * t, :] for n in range(len(slabs))]
            tail = _stacked_split_dot(dl, upper_incl, 2)
            dzb = []
            for n, (r, j) in enumerate(slabs):
                sl = slab(rs[r], j)
                tl = tail[n * t:(n + 1) * t, :]
                e = e_ref[sl, :]
                dl1m = (delta[rs[r]][j] - e) - tl
                e_ref[sl, :] = e + tl[:, 0:1]
                c_ref[sl, :] = w[n][2]
                dz = dl[n] - jnp.exp(w[n][0]) * (dl[n] + dl1m)
                if masked[r]:
                    dz = jnp.where(strict, dz, 0.0)
                dzb.append(dz.astype(BF16))
            a_all, dz_all = _stack(ab), _stack(dzb)
            dqs_ref[pl.ds(2 * rs[0] * t, len(slabs) * t), :] += _dot(dz_all, kblk)
            return _dot(q_all_t[:, cols], dz_all), _dot(dy_all_t[:, cols], a_all)

        def flush(kb, side):
            k0 = pl.multiple_of(kb * t, t)
            dkt_ref[:, pl.ds(k0, t)] += side[0]
            dvt_ref[:, pl.ds(k0, t)] += side[1]

        _walk_tiles(i, nr, load, sub, flush, more=_sb_more(c_ref, t))
        for r, rw in enumerate(rows):
            dq_ref[rw, :] = jnp.where(lo, dqs_ref[slab(r, 0), :], dqs_ref[slab(r, 1), :]) * (HEAD_DIM ** -0.5)

    qblk, kvfull = _pair_specs(s_len, t * nr)
    kvfull_t = pl.BlockSpec((LANES, s_len), lambda hp, i: (hp, 0))
    return pl.pallas_call(
        body, name=name, grid=(N_HEADS // 2, nq),
        in_specs=[qblk, kvfull, kvfull, qblk, qblk],
        out_specs=[qblk, kvfull_t, kvfull_t],
        out_shape=[jax.ShapeDtypeStruct((s_len, ATT_W), F32)] + [jax.ShapeDtypeStruct((ATT_W, s_len), F32)] * 2,
        scratch_shapes=[pltpu.VMEM((2 * nr * t, 1), F32), pltpu.VMEM((2 * nr * t, 1), F32),
                        pltpu.VMEM((2 * nr * t, LANES), F32)],
        compiler_params=_cparams("arbitrary", "arbitrary"),
    )(q, k, v, dy, yf)


def _merge_fwd(x, yf, ys, gf, gs, wbf, wbs, wo, name):
    s_len = x.shape[0]
    ts = min(512, s_len)

    def body(x_ref, yf_ref, ys_ref, gf_ref, gs_ref, wbf_ref, wbs_ref, wo_ref, o_ref):
        merged = (_sigmoid(gf_ref[...]) * _dot_nt(yf_ref[...], wbf_ref[...])
                  + _sigmoid(gs_ref[...]) * _dot_nt(ys_ref[...], wbs_ref[...]))
        o_ref[...] = x_ref[...] + _dot(merged.astype(BF16), wo_ref[...])

    tok = lambda w: pl.BlockSpec((ts, w), lambda i: (i, 0))
    full = lambda a: pl.BlockSpec(a.shape, lambda i: (0, 0))
    return pl.pallas_call(
        body, name=name, grid=(s_len // ts,),
        in_specs=[tok(D_MODEL), tok(ATT_W), tok(ATT_W), tok(D_MODEL), tok(D_MODEL), full(wbf), full(wbs), full(wo)],
        out_specs=tok(D_MODEL),
        out_shape=jax.ShapeDtypeStruct((s_len, D_MODEL), F32),
        compiler_params=_cparams("parallel"),
    )(x, yf, ys, gf, gs, wbf, wbs, wo)


def _merge_bwd(dx, yf, ys, gf, gs, wbf, wbs, wo, name):
    s_len = dx.shape[0]
    ts = min(512, s_len)

    def body(dx_ref, yf_ref, ys_ref, gf_ref, gs_ref, wbf_ref, wbs_ref, wo_ref,
             dyf_ref, dys_ref, dgf_ref, dgs_ref, dbf_ref, dbs_ref, mg_ref):
        bf = _dot_nt(yf_ref[...], wbf_ref[...])
        bs = _dot_nt(ys_ref[...], wbs_ref[...])
        sf = _sigmoid(gf_ref[...])
        ss = _sigmoid(gs_ref[...])
        mg_ref[...] = (sf * bf + ss * bs).astype(BF16)
        dm = _dot_nt(dx_ref[...].astype(BF16), wo_ref[...])
        dbf = (dm * sf).astype(BF16)
        dbs = (dm * ss).astype(BF16)
        dbf_ref[...] = dbf
        dbs_ref[...] = dbs
        dgf_ref[...] = (dm * bf * (sf * (1.0 - sf))).astype(BF16)
        dgs_ref[...] = (dm * bs * (ss * (1.0 - ss))).astype(BF16)
        dyf_ref[...] = _dot(dbf, wbf_ref[...]).astype(BF16)
        dys_ref[...] = _dot(dbs, wbs_ref[...]).astype(BF16)

    tok = lambda w: pl.BlockSpec((ts, w), lambda i: (i, 0))
    full = lambda a: pl.BlockSpec(a.shape, lambda i: (0, 0))
    b16o = lambda w: jax.ShapeDtypeStruct((s_len, w), BF16)
    return pl.pallas_call(
        body, name=name, grid=(s_len // ts,),
        in_specs=[tok(D_MODEL), tok(ATT_W), tok(ATT_W), tok(D_MODEL), tok(D_MODEL), full(wbf), full(wbs), full(wo)],
        out_specs=[tok(ATT_W), tok(ATT_W)] + [tok(D_MODEL)] * 5,
        out_shape=[b16o(ATT_W), b16o(ATT_W)] + [b16o(D_MODEL)] * 5,
        compiler_params=_cparams("parallel"),
    )(dx, yf, ys, gf, gs, wbf, wbs, wo)


def _mix_bwd(x, dx_in, gain, w_in, fqr, fkr, dfqn, dfkn_t, qg, kg, dfv_t, df_col, logf, dsq, dsk_t, dsv_t, dgf, dgs,
             name):
    s_len = x.shape[0]
    ts = min(256, s_len)
    nt = s_len // ts
    gmat = _head_group_matrix()

    def body(x_ref, dxi_ref, gain_ref, w_ref, fqr_ref, fkr_ref, dfqn_ref, dfknt_ref, qg_ref, kg_ref, gm_ref,
             dfvt_ref, df_ref, logf_ref, dsq_ref, dskt_ref, dsvt_ref, dgf_ref, dgs_ref,
             dp_ref, dx_ref, dgain_ref, dqg_ref, dkg_ref, dbias_ref, carry):
        i = pl.program_id(0)

        @pl.when(i == 0)
        def _():
            carry[...] = jnp.zeros_like(carry)
            dgain_ref[...] = jnp.zeros_like(dgain_ref)
            dqg_ref[...] = jnp.zeros_like(dqg_ref)
            dkg_ref[...] = jnp.zeros_like(dkg_ref)
            dbias_ref[...] = jnp.zeros_like(dbias_ref)

        gm = gm_ref[...]

        def headnorm_bwd(raw, dout, g, dg_ref):
            ms = _dot_split(raw * raw, gm, HEAD_SUM_PARTS) * (1.0 / HEAD_DIM)
            r = lax.rsqrt(ms + EPS)
            nrm = raw * r
            dg_ref[...] += jnp.sum(dout * nrm, axis=0, keepdims=True)
            dn = dout * g
            mean_h = _dot_split(dn * nrm, gm, HEAD_SUM_PARTS) * (1.0 / HEAD_DIM)
            return r * (dn - nrm * mean_h)

        dp_ref[:, C_FQ:C_FQ + ATT_W] = headnorm_bwd(fqr_ref[...], dfqn_ref[...], qg_ref[...], dqg_ref).astype(BF16)
        dp_ref[:, C_FK:C_FK + ATT_W] = headnorm_bwd(fkr_ref[...], dfknt_ref[...].T, kg_ref[...], dkg_ref).astype(BF16)
        dp_ref[:, C_FV:C_FV + ATT_W] = dfvt_ref[...].T.astype(BF16)
        dp_ref[:, C_SQ:C_SQ + ATT_W] = dsq_ref[...].astype(BF16)
        dp_ref[:, C_SK:C_SK + ATT_W] = dskt_ref[...].T.astype(BF16)
        dp_ref[:, C_SV:C_SV + ATT_W] = dsvt_ref[...].T.astype(BF16)
        dp_ref[:, C_GF:C_GF + D_MODEL] = dgf_ref[...]
        dp_ref[:, C_GS:C_GS + D_MODEL] = dgs_ref[...]

        r_ = lax.broadcasted_iota(jnp.int32, (ts, ts), 0)
        c_ = lax.broadcasted_iota(jnp.int32, (ts, ts), 1)
        rev = (c_ >= r_).astype(BF16)
        dlogf = _dot_split_left(rev, df_ref[...], 3) + carry[...]
        carry[...] = dlogf[0:1, :]
        lane = lax.broadcasted_iota(jnp.int32, (ts, LANES), 1)
        dfl = jnp.where(lane < N_HEADS, dlogf * (1.0 - jnp.exp(logf_ref[...])), 0.0)
        dbias_ref[...] += jnp.sum(dfl, axis=0, keepdims=True)
        dp_ref[:, C_FL:C_FL + LANES] = dfl.astype(BF16)
        dp_ref[:, C_FL + LANES:C_SQ] = jnp.zeros((ts, C_SQ - C_FL - LANES), BF16)

        dh = _dot(dp_ref[...], w_ref[...])
        xf = x_ref[...]
        r = _rms_rinv(xf)
        xhat = xf * r
        dgain_ref[...] += jnp.sum(dh * xhat, axis=0, keepdims=True)
        dn = dh * gain_ref[...]
        dx_ref[...] = dxi_ref[...] + r * (dn - xhat * jnp.mean(dn * xhat, axis=-1, keepdims=True))

    tok = lambda w: pl.BlockSpec((ts, w), lambda i: (nt - 1 - i, 0))
    full = lambda a: pl.BlockSpec(a.shape, lambda i: (0, 0))
    row = lambda w: pl.BlockSpec((1, w), lambda i: (0, 0))
    tok_t = pl.BlockSpec((ATT_W, ts), lambda i: (0, nt - 1 - i))
    return _ride_call(
        body, name, (nt,),
        [tok(D_MODEL), tok(D_MODEL), full(gain), full(w_in), tok(ATT_W), tok(ATT_W), tok(ATT_W), tok_t,
         full(qg), full(kg), full(gmat), tok_t, tok(LANES), tok(LANES), tok(ATT_W), tok_t, tok_t,
         tok(D_MODEL), tok(D_MODEL)],
        [tok(IN_PAD), tok(D_MODEL), row(D_MODEL), row(ATT_W), row(ATT_W), row(LANES)],
        [jax.ShapeDtypeStruct((s_len, IN_PAD), BF16), jax.ShapeDtypeStruct((s_len, D_MODEL), F32),
         jax.ShapeDtypeStruct((1, D_MODEL), F32), jax.ShapeDtypeStruct((1, ATT_W), F32),
         jax.ShapeDtypeStruct((1, ATT_W), F32), jax.ShapeDtypeStruct((1, LANES), F32)],
        [pltpu.VMEM((1, LANES), F32)], ("arbitrary",),
        (x, dx_in, gain, w_in, fqr, fkr, dfqn, dfkn_t, qg, kg, gmat, dfv_t, df_col, logf, dsq, dsk_t, dsv_t, dgf, dgs),
        None)


def _ple_loss(x, p, tgt, gain, wpg, wpp, name):
    s_len = x.shape[0]
    ts = min(512, s_len)

    def body(x_ref, p_ref, t_ref, gain_ref, wpg_ref, wpp_ref, dx_ref, n_ref, ds_ref, dpp_ref, dgain_ref, loss_ref):
        i = pl.program_id(0)

        @pl.when(i == 0)
        def _():
            dgain_ref[...] = jnp.zeros_like(dgain_ref)
            loss_ref[...] = jnp.zeros_like(loss_ref)

        xf = x_ref[...]
        r = _rms_rinv(xf)
        n = xf * r
        hn = (n * gain_ref[...]).astype(BF16)
        n_ref[...] = hn
        sg = _sigmoid(_dot(hn, wpg_ref[...]))
        pp = _dot_nt(p_ref[...].astype(BF16), wpp_ref[...])
        err = (xf + sg * pp) - t_ref[...]
        sq = jnp.sum(jnp.sum(err * err, axis=1, keepdims=True), axis=0, keepdims=True)
        loss_ref[...] += (0.5 / D_MODEL) * sq
        dout = err * (1.0 / D_MODEL)
        dpp_ref[...] = (dout * sg).astype(BF16)
        ds = (dout * pp * (sg * (1.0 - sg))).astype(BF16)
        ds_ref[...] = ds
        dhn = _dot_nt(ds, wpg_ref[...])
        dgain_ref[...] += jnp.sum(dhn * n, axis=0, keepdims=True)
        dn = dhn * gain_ref[...]
        dx_ref[...] = dout + r * (dn - n * jnp.mean(dn * n, axis=-1, keepdims=True))

    tok = lambda w: pl.BlockSpec((ts, w), lambda i: (i, 0))
    full = lambda a: pl.BlockSpec(a.shape, lambda i: (0, 0))
    return pl.pallas_call(
        body, name=name, grid=(s_len // ts,),
        in_specs=[tok(D_MODEL), tok(PLE_DIM), tok(D_MODEL), full(gain), full(wpg), full(wpp)],
        out_specs=[tok(D_MODEL), tok(D_MODEL), tok(D_MODEL), tok(D_MODEL),
                   pl.BlockSpec((1, D_MODEL), lambda i: (0, 0)), pl.BlockSpec((8, LANES), lambda i: (0, 0))],
        out_shape=[jax.ShapeDtypeStruct((s_len, D_MODEL), F32), jax.ShapeDtypeStruct((s_len, D_MODEL), BF16),
                   jax.ShapeDtypeStruct((s_len, D_MODEL), BF16), jax.ShapeDtypeStruct((s_len, D_MODEL), BF16),
                   jax.ShapeDtypeStruct((1, D_MODEL), F32), jax.ShapeDtypeStruct((8, LANES), F32)],
        compiler_params=_cparams("arbitrary"),
    )(x, p, tgt, gain, wpg, wpp)


def _exchange(x, name, broadcast):
    def body(x_ref, out_ref, send_sems, recv_sems, local_sem):
        _exchange_start(x_ref, out_ref, send_sems, recv_sems, local_sem, broadcast)
        _exchange_wait(x_ref, out_ref, send_sems, recv_sems, local_sem, broadcast)

    return pl.pallas_call(
        body, name=name,
        in_specs=[EXCHANGE_SPEC],
        out_specs=EXCHANGE_SPEC,
        out_shape=_exchange_shape(x, broadcast),
        scratch_shapes=list(EXCHANGE_SEMS),
        compiler_params=pltpu.CompilerParams(has_side_effects=True),
    )(x)


def _gather_two_level(x, name):
    def body(x_ref, out_ref, send_sems, recv_sems, local_sem):
        mx, my, mc = lax.axis_index("x"), lax.axis_index("y"), lax.axis_index("c")
        me, sibling = (mx, my, mc), (mx, my, 1 - mc)
        chips = [(1 - mx, my), (mx, 1 - my), (1 - mx, 1 - my)]

        def slot(px, py, pc):
            return out_ref.at[4 * px + 2 * py + pc]

        def copy(k, block, to, src=None):
            return pltpu.make_async_remote_copy(
                src_ref=slot(*block) if src is None else src, dst_ref=slot(*block),
                send_sem=send_sems.at[k], recv_sem=recv_sems.at[k], device_id=to, device_id_type=MESH)

        mine = pltpu.make_async_copy(x_ref, slot(*me), local_sem)
        mine.start()
        first = [copy(0, me, sibling, src=x_ref)]
        first += [copy(1 + j, me, (*chip, mc), src=x_ref) for j, chip in enumerate(chips)]
        for cp in first:
            cp.start()
        passed = [copy(4 + j, (*chip, mc), sibling) for j, chip in enumerate(chips)]
        for j, chip in enumerate(chips):
            copy(1 + j, (*chip, mc), me).wait_recv()
            passed[j].start()
        copy(0, sibling, me).wait_recv()
        for j, chip in enumerate(chips):
            copy(4 + j, (*chip, 1 - mc), me).wait_recv()
        for cp in first + passed:
            cp.wait_send()
        mine.wait()

    return pl.pallas_call(
        body, name=name,
        in_specs=[EXCHANGE_SPEC],
        out_specs=EXCHANGE_SPEC,
        out_shape=_exchange_shape(x, True),
        scratch_shapes=list(EXCHANGE_SEMS),
        compiler_params=pltpu.CompilerParams(has_side_effects=True),
    )(x)


EXCHANGE_SPEC = pl.BlockSpec(memory_space=pl.ANY)
EXCHANGE_SEMS = (pltpu.SemaphoreType.DMA((N_DEV - 1,)), pltpu.SemaphoreType.DMA((N_DEV - 1,)), pltpu.SemaphoreType.DMA)


def _exchange_shape(x, broadcast):
    return jax.ShapeDtypeStruct((N_DEV,) + tuple(x.shape if broadcast else x.shape[1:]), x.dtype)


def _exchange_copies(x_ref, out_ref, send_sems, recv_sems, local_sem, broadcast, with_recv=True):
    mx, my, mc = lax.axis_index("x"), lax.axis_index("y"), lax.axis_index("c")
    me = 4 * mx + 2 * my + mc

    def src(idx):
        return x_ref if broadcast else x_ref.at[idx]

    local = pltpu.make_async_copy(src(me), out_ref.at[me], local_sem)
    pairs = []
    for k in range(1, N_DEV):
        px = (1 - mx) if k & 4 else mx
        py = (1 - my) if k & 2 else my
        pc = (1 - mc) if k & 1 else mc
        peer = 4 * px + 2 * py + pc
        sems = dict(send_sem=send_sems.at[k - 1], recv_sem=recv_sems.at[k - 1], device_id=(px, py, pc), device_id_type=MESH)
        recv = pltpu.make_async_remote_copy(src_ref=src(peer), dst_ref=out_ref.at[peer], **sems) if with_recv else None
        pairs.append((pltpu.make_async_remote_copy(src_ref=src(peer), dst_ref=out_ref.at[me], **sems), recv))
    return local, pairs


def _exchange_start(*refs_and_mode):
    local, pairs = _exchange_copies(*refs_and_mode, with_recv=False)
    local.start()
    for send, _ in pairs:
        send.start()


def _exchange_wait(*refs_and_mode):
    local, pairs = _exchange_copies(*refs_and_mode)
    for _, recv in pairs:
        recv.wait_recv()
    for send, _ in pairs:
        send.wait_send()
    local.wait()


def _riding(body, grid, n_in, n_out, ride):
    if ride is None:
        return body
    broadcast = ride[1]

    def wrapped(*refs):
        ins, x_ref = refs[:n_in], refs[n_in]
        outs, out_ref = refs[n_in + 1:n_in + 1 + n_out], refs[n_in + 1 + n_out]
        scratch, sems = refs[n_in + 2 + n_out:-3], refs[-3:]
        step = pl.program_id(0)
        for d in range(1, len(grid)):
            step = step * grid[d] + pl.program_id(d)
        total = 1
        for g in grid:
            total *= g

        @pl.when(step == 0)
        def _():
            _exchange_start(x_ref, out_ref, *sems, broadcast)

        body(*ins, *outs, *scratch)

        @pl.when(step == total - 1)
        def _():
            _exchange_wait(x_ref, out_ref, *sems, broadcast)

    return wrapped


def _ride_call(body, name, grid, in_specs, out_specs, out_shape, scratch_shapes, sem, operands, ride):
    if ride is None:
        return pl.pallas_call(body, name=name, grid=grid, in_specs=in_specs, out_specs=out_specs, out_shape=out_shape,
                              scratch_shapes=scratch_shapes, compiler_params=_cparams(*sem))(*operands)
    return pl.pallas_call(
        _riding(body, grid, len(in_specs), len(out_specs), ride), name=name, grid=grid,
        in_specs=list(in_specs) + [EXCHANGE_SPEC], out_specs=list(out_specs) + [EXCHANGE_SPEC],
        out_shape=list(out_shape) + [_exchange_shape(*ride)],
        scratch_shapes=list(scratch_shapes) + list(EXCHANGE_SEMS),
        compiler_params=_cparams(*(["arbitrary"] * len(grid))),
    )(*operands, ride[0])


def _adamw_math(w, g, m, v):
    m2 = ADAM_B1 * m + (1.0 - ADAM_B1) * g
    v2 = ADAM_B2 * v + (1.0 - ADAM_B2) * (g * g)
    m_hat = m2 / (1.0 - ADAM_B1 ** ADAM_STEP)
    v_hat = v2 / (1.0 - ADAM_B2 ** ADAM_STEP)
    delta = -ADAM_LR * (m_hat / (jnp.sqrt(v_hat) + ADAM_EPS) + ADAM_WD * w)
    return delta, m2, v2


def _sum_parts(parts, name, tr):
    _, rows, cols = parts.shape

    def body(p_ref, g_ref):
        g = p_ref[0].astype(F32)
        for s in range(1, N_DEV):
            g = g + p_ref[s].astype(F32)
        g_ref[...] = g

    return pl.pallas_call(
        body, name=name, grid=(rows // tr,),
        in_specs=[pl.BlockSpec((N_DEV, tr, cols), lambda i: (0, i, 0))],
        out_specs=pl.BlockSpec((tr, cols), lambda i: (i, 0)),
        out_shape=jax.ShapeDtypeStruct((rows, cols), F32),
        compiler_params=_cparams("parallel"),
    )(parts)


ADAM_SPLIT_ELEMS = 400_000


def _adamw_shard(g, w, m, v, name):
    rows, cols = w.shape
    tr = rows // 2 if rows * cols > ADAM_SPLIT_ELEMS else rows

    def body(g_ref, w_ref, m_ref, v_ref, d_ref, m2_ref, v2_ref):
        d_ref[...], m2_ref[...], v2_ref[...] = _adamw_math(w_ref[...], g_ref[...], m_ref[...], v_ref[...])

    blk = pl.BlockSpec((tr, cols), lambda i: (i, 0))
    return pl.pallas_call(
        body, name=name, grid=(rows // tr,),
        in_specs=[blk] * 4, out_specs=[blk] * 3,
        out_shape=[jax.ShapeDtypeStruct((rows, cols), F32)] * 3,
        compiler_params=_cparams("parallel"),
    )(g, w, m, v)


UPDATE_STEPS = 2


def _adamw_from_parts(parts, off, w, m, v, name, transposed):
    n_rows = w.size // D_MODEL
    assert off % n_rows == 0 and w.shape == ((1, D_MODEL, n_rows) if transposed else (1, n_rows, D_MODEL))

    def body(p_ref, w_ref, m_ref, v_ref, g_ref, d_ref, m2_ref, v2_ref):
        g = p_ref[0].astype(F32)
        for s in range(1, N_DEV):
            g = g + p_ref[s].astype(F32)
        g = g.T if transposed else g
        g_ref[0] = g
        d_ref[0], m2_ref[0], v2_ref[0] = _adamw_math(w_ref[0], g, m_ref[0], v_ref[0])

    ns = UPDATE_STEPS
    if transposed:
        part = pl.BlockSpec((N_DEV, n_rows, D_MODEL // ns), lambda i: (0, off // n_rows, i))
        blk = pl.BlockSpec((1, D_MODEL // ns, n_rows), lambda i: (0, i, 0))
    else:
        part = pl.BlockSpec((N_DEV, n_rows // ns, D_MODEL), lambda i: (0, off // (n_rows // ns) + i, 0))
        blk = pl.BlockSpec((1, n_rows // ns, D_MODEL), lambda i: (0, i, 0))
    return pl.pallas_call(
        body, name=name, grid=(ns,),
        in_specs=[part] + [blk] * 3,
        out_specs=[blk] * 4,
        out_shape=[jax.ShapeDtypeStruct(w.shape, F32)] * 4,
        compiler_params=_cparams("parallel"),
    )(parts, w, m, v)


def _adamw_small(parts, w, m, v, name):
    names = list(SMALL_NAMES)

    def body(p_ref, *refs):
        ins, outs = refs[:3 * len(names)], refs[3 * len(names):]
        total = p_ref[0]
        for s in range(1, N_DEV):
            total = total + p_ref[s]
        for i, n in enumerate(names):
            row, off, width = SMALL_POS[n]
            g = total[row:row + 1, off:off + width]
            w_ref, m_ref, v_ref = ins[3 * i:3 * i + 3]
            g_ref, d_ref, m2_ref, v2_ref = outs[4 * i:4 * i + 4]
            g_ref[...] = g
            d_ref[...], m2_ref[...], v2_ref[...] = _adamw_math(w_ref[...], g, m_ref[...], v_ref[...])
        row, off, _ = SMALL_POS["loss"]
        outs[-1][...] = total[row:row + 1, off:off + 1]

    operands = [parts] + [t[n] for n in names for t in (w, m, v)]
    shapes = [jax.ShapeDtypeStruct(w[n].shape, F32) for n in names for _ in range(4)]
    shapes.append(jax.ShapeDtypeStruct((1, 1), F32))
    out = pl.pallas_call(body, name=name, out_shape=shapes)(*operands)
    return {n: tuple(out[4 * i:4 * i + 4]) for i, n in enumerate(names)}, out[-1]


TRANSPOSED = frozenset(("ffn1_w_gate", "ffn1_w_up", "w_in", "w_branch_fox", "w_branch_sb", "ffn2_w_gate", "ffn2_w_up",
                        "w_ple_proj"))
F_PAD_ROWS = C_SQ - FL_REAL_END


def _pack(pieces, group, dtype):
    out = []
    for name in GATHER_GROUPS[group]:
        r = pieces[name].T if name in TRANSPOSED else pieces[name]
        r = r.reshape(-1, D_MODEL).astype(dtype)
        if r.shape[0] != PACK_ROWS[name]:
            r = jnp.pad(r, ((0, PACK_ROWS[name] - r.shape[0]), (0, 0)))
        out.append(r)
    return jnp.concatenate(out, axis=0)


def _real_rows(name):
    return W_IN_ROWS if name == "w_in" else PACK_ROWS[name]


def _gathered(got, name, shape):
    off = GATHER_OFF[name]
    return got[:, off:off + _real_rows(name), :].reshape(shape)


def _w_in_device_rows(d):
    lo, hi = d * W_IN_ROWS, (d + 1) * W_IN_ROWS
    if hi <= FL_REAL_END:
        return [(lo, hi)]
    if lo >= FL_REAL_END:
        return [(lo + F_PAD_ROWS, hi + F_PAD_ROWS)]
    return [(lo, FL_REAL_END), (C_SQ, hi + F_PAD_ROWS)]


def _w_in_t_padded(got):
    t = _gathered(got, "w_in", (IN_REAL, D_MODEL))
    return jnp.concatenate([t[:FL_REAL_END], jnp.zeros((F_PAD_ROWS, D_MODEL), t.dtype), t[FL_REAL_END:]], axis=0)


def _pack_chunks(grads, group):
    out = []
    for name in SCATTER_GROUPS[group]:
        base, _, half = name.partition("#")
        g = grads[base].astype(BF16)
        if base == "w_in":
            lo, hi = W_IN_HALVES[int(half)]
            tail = jnp.zeros((PACK_ROWS[base] - W_IN_ROWS, D_MODEL), BF16)
            c = jnp.stack([jnp.concatenate([g[a:b] for a, b in _w_in_device_rows(d)] + [tail], axis=0)[lo:hi]
                           for d in range(N_DEV)])
        else:
            c = g.reshape(N_DEV, PACK_ROWS[name], D_MODEL)
        out.append(c)
    return out[0] if len(out) == 1 else jnp.concatenate(out, axis=1)


def _shard_grad(summed, name, shape):
    if name == "w_in":
        rows = jnp.concatenate([summed[f"w_in#{i}"] for i in range(len(W_IN_HALVES))], axis=0)[:W_IN_ROWS]
    else:
        rows = summed[name][SCATTER_OFF[name]:SCATTER_OFF[name] + PACK_ROWS[name], :]
    return rows.reshape(shape[1], shape[0]).T if name in TRANSPOSED else rows.reshape(shape)


WEIGHT_NAMES = ['ffn1_norm', 'ffn1_w_gate', 'ffn1_w_up', 'ffn1_w_down', 'mix_norm', 'w_in', 'forget_bias', 'q_norm',
                'k_norm', 'w_branch_fox', 'w_branch_sb', 'w_out', 'ffn2_norm', 'ffn2_w_gate', 'ffn2_w_up',
                'ffn2_w_down', 'ple_norm', 'w_ple_gate', 'w_ple_proj']
SMALL_NAMES = ('ffn1_norm', 'mix_norm', 'ffn2_norm', 'ple_norm', 'q_norm', 'k_norm', 'forget_bias')
SMALL_POS = {'ffn1_norm': (0, 0, D_MODEL), 'mix_norm': (1, 0, D_MODEL), 'ffn2_norm': (2, 0, D_MODEL),
             'ple_norm': (3, 0, D_MODEL), 'q_norm': (4, 0, HEAD_DIM), 'k_norm': (4, HEAD_DIM, HEAD_DIM),
             'forget_bias': (4, 2 * HEAD_DIM, N_HEADS), 'loss': (4, 2 * HEAD_DIM + N_HEADS, 1)}


def _pack_small(vals, loss):
    tail = [vals[n].reshape(1, -1) for n in ('q_norm', 'k_norm', 'forget_bias')] + [loss.reshape(1, 1)]
    tail.append(jnp.zeros((1, D_MODEL - sum(t.shape[1] for t in tail)), F32))
    rows = [vals[n].reshape(1, D_MODEL) for n in SMALL_NAMES[:4]] + [jnp.concatenate(tail, axis=1)]
    rows.append(jnp.zeros((SMALL_ROWS - len(rows), D_MODEL), F32))
    return jnp.concatenate(rows, axis=0)


def _step(x, p, tgt, w):
    row = lambda a: a.reshape(1, -1).astype(F32)
    g_ffn1, g_mix, g_ffn2, g_ple = (row(w[n]) for n in SMALL_NAMES[:4])
    qg = jnp.tile(row(w['q_norm']), (1, N_HEADS))
    kg = jnp.tile(row(w['k_norm']), (1, N_HEADS))
    bias = jnp.pad(row(w['forget_bias']), ((0, 0), (0, LANES - N_HEADS)))
    half = D_FF // 2
    grads = {}

    blk = lambda n: GATHER_OFF[n] // FFN_SHARD
    ffn1 = tuple(blk(n) for n in ("ffn1_w_gate", "ffn1_w_up", "ffn1_w_down"))
    ffn2 = tuple(blk(n) for n in ("ffn2_w_gate", "ffn2_w_up", "ffn2_w_down"))
    got0 = _gather_two_level(_pack(w, 0, BF16), "gather_ffn1")
    x1, g1, u1, h1, got1 = _ffn_fwd(x, g_ffn1, (got0,) * 3, ffn1, "ffn1_fwd", ride=(_pack(w, 1, BF16), True))
    w_in = _w_in_t_padded(got1)
    wbf = _gathered(got1, "w_branch_fox", (D_MODEL, ATT_W))
    wbs = _gathered(got1, "w_branch_sb", (D_MODEL, ATT_W))
    wo = _gathered(got1, "w_out", (D_MODEL, D_MODEL))
    (hmix, fqr, fkr, fqn, fkn, fv, logf, f_col, f_row, sq, sk, sv, gf, gs) = _mix_fwd(
        x1, g_mix, w_in, bias, qg, kg, "mix_fwd")
    f_wide = jnp.repeat(f_col[:, :N_HEADS], HEAD_DIM, axis=1)
    cap = _fox_qk_cap(w['q_norm'], w['k_norm'])
    y_fox, lse, got2 = _fox_fwd(fqn, fkn, fv, f_wide, f_row, cap, "fox_fwd", ride=(_pack(w, 2, BF16), True))
    y_sb, y_sb32, got3 = _sb_fwd(sq, sk, sv, "sb_fwd", ride=(_pack(w, 3, BF16), True))
    wpg = _gathered(got3, "w_ple_gate", (D_MODEL, D_MODEL))
    wpp = _gathered(got3, "w_ple_proj", (D_MODEL, PLE_DIM))
    ffn2_bufs = (got2, got2, got3)
    x2 = _merge_fwd(x1, y_fox, y_sb, gf, gs, wbf, wbs, wo, "merge_fwd")
    x3, g2, u2, h2, = _ffn_fwd(x2, g_ffn2, ffn2_bufs, ffn2, "ffn2_fwd")
    dx3, n_ple, ds_ple, dpp, dg_ple, loss = _ple_loss(x3, p, tgt, g_ple, wpg, wpp, "ple_loss")

    grads['w_ple_gate'] = _wgrad(n_ple, ds_ple, "dw_ple_gate", D_MODEL, D_MODEL)
    grads['w_ple_proj'] = _wgrad(dpp, p, "dw_ple_proj", D_MODEL, PLE_DIM)
    dg2, du2, act2, dx2, dg_ffn2 = _ffn_bwd_fused(x2, dx3, g_ffn2, g2, u2, ffn2_bufs, ffn2, "ffn2_bwd")
    grads['ffn2_w_gate'] = _wgrad(dg2, h2, "dw_ffn2_gate", half, D_MODEL)
    grads['ffn2_w_up'] = _wgrad(du2, h2, "dw_ffn2_up", half, D_MODEL)
    grads['ffn2_w_down'] = _wgrad(act2, dx3, "dw_ffn2_down", half, D_MODEL)
    dyf, dys, dgf, dgs, dbf, dbs, merged = _merge_bwd(dx2, y_fox, y_sb, gf, gs, wbf, wbs, wo, "merge_bwd")
    grads['w_branch_fox'] = _wgrad(dbf, y_fox, "dw_branch_fox", D_MODEL, ATT_W)
    grads['w_branch_sb'] = _wgrad(dbs, y_sb, "dw_branch_sb", D_MODEL, ATT_W)
    grads['w_out'] = _wgrad(merged, dx2, "dw_out", D_MODEL, D_MODEL)
    dfqn, dfkn_t, dfv_t, dft, part_rest = _fox_bwd(fqn, fkn, fv, dyf, y_fox, lse, f_wide, f_row, cap, "fox_bwd",
                                               ride=(_pack_chunks(grads, 5), False))
    dsq, dsk_t, dsv_t = _sb_bwd(sq, sk, sv, dys, y_sb32, "sb_bwd")
    s_len = x.shape[0]
    df_col = jnp.pad(dft[:, :2, :].reshape(N_HEADS, s_len).T, ((0, 0), (0, LANES - N_HEADS)))
    dproj, dx1, dg_mix, dqg, dkg, dbias = _mix_bwd(
        x1, dx2, g_mix, w_in, fqr, fkr, dfqn, dfkn_t, qg, kg, dfv_t, df_col, logf, dsq, dsk_t, dsv_t, dgf, dgs,
        "mix_bwd")
    grads['w_in'] = _wgrad(dproj, hmix, "dw_in", IN_PAD // 3, D_MODEL)
    dg1, du1, act1, dx0, dg_ffn1, part_in0 = _ffn_bwd_fused(x, dx1, g_ffn1, g1, u1, (got0,) * 3, ffn1, "ffn1_bwd",
                                                            ride=(_pack_chunks(grads, 3), False))
    grads['ffn1_w_gate'], part_in1 = _wgrad(dg1, h1, "dw_ffn1_gate", half, D_MODEL,
                                            ride=(_pack_chunks(grads, 4), False))
    grads['ffn1_w_up'], part_gate = _wgrad(du1, h1, "dw_ffn1_up", half, D_MODEL, ride=(_pack_chunks(grads, 0), False))
    grads['ffn1_w_down'], part_up = _wgrad(act1, dx1, "dw_ffn1_down", half, D_MODEL,
                                           ride=(_pack_chunks(grads, 1), False))
    part_down = _exchange(_pack_chunks(grads, 2), "scatter_ffn1_down", False)

    fold = lambda a: a.reshape(N_HEADS, HEAD_DIM).sum(axis=0).reshape(1, HEAD_DIM)
    small_g = {'ffn1_norm': dg_ffn1, 'mix_norm': dg_mix, 'ffn2_norm': dg_ffn2, 'ple_norm': dg_ple,
               'q_norm': fold(dqg), 'k_norm': fold(dkg), 'forget_bias': dbias[:, :N_HEADS]}
    return loss[0, 0], dx0, (part_gate, part_up, part_down, part_in0, part_in1, part_rest), small_g


def kernel(x, p, ffn1_norm, ffn1_w_gate, ffn1_w_up, ffn1_w_down, mix_norm, w_in, forget_bias, q_norm, k_norm, w_branch_fox, w_branch_sb, w_out, ffn2_norm, ffn2_w_gate, ffn2_w_up, ffn2_w_down, ple_norm, w_ple_gate, w_ple_proj, loss_target, m_ffn1_norm, m_ffn1_w_gate, m_ffn1_w_up, m_ffn1_w_down, m_mix_norm, m_w_in, m_forget_bias, m_q_norm, m_k_norm, m_w_branch_fox, m_w_branch_sb, m_w_out, m_ffn2_norm, m_ffn2_w_gate, m_ffn2_w_up, m_ffn2_w_down, m_ple_norm, m_w_ple_gate, m_w_ple_proj, v_ffn1_norm, v_ffn1_w_gate, v_ffn1_w_up, v_ffn1_w_down, v_mix_norm, v_w_in, v_forget_bias, v_q_norm, v_k_norm, v_w_branch_fox, v_w_branch_sb, v_w_out, v_ffn2_norm, v_ffn2_w_gate, v_ffn2_w_up, v_ffn2_w_down, v_ple_norm, v_w_ple_gate, v_w_ple_proj):
    args = dict(locals())
    w = {n: args[n][0] for n in WEIGHT_NAMES}
    m = {n: args["m_" + n][0] for n in WEIGHT_NAMES}
    v = {n: args["v_" + n][0] for n in WEIGHT_NAMES}
    loss, dx, parts, small_g = _step(x[0], p[0, 0], loss_target[0], w)

    group_of = {n: grp for grp, members in enumerate(SCATTER_GROUPS) for n in members}
    direct = [n for n in WEIGHT_NAMES if n.startswith("ffn") and n not in SMALL_NAMES]
    summed = {}
    for grp, part in enumerate(parts):
        if not all(n in direct for n in SCATTER_GROUPS[grp]):
            s = _sum_parts(part, f"sum_grads_{grp}", SUM_TILE_ROWS[grp])
            summed.update({n: s for n in SCATTER_GROUPS[grp]})
    big = {}
    for n in WEIGHT_NAMES:
        if n in direct:
            big[n] = tuple(_adamw_from_parts(parts[group_of[n]], SCATTER_OFF[n], args[n], args["m_" + n],
                                             args["v_" + n], "adamw_" + n, n in TRANSPOSED))
        elif n not in SMALL_NAMES:
            g = _shard_grad(summed, n, w[n].shape)
            big[n] = (g,) + tuple(_adamw_shard(g, w[n], m[n], v[n], "adamw_" + n))
    small_parts = _exchange(_pack_small(small_g, loss), "gather_small", True)
    small, total_loss = _adamw_small(small_parts, *({n: args[pre + n] for n in SMALL_NAMES} for pre in ("", "m_", "v_")),
                                     "adamw_small")
    big.update(small)

    outs = [total_loss.reshape(()), dx.reshape(x.shape)]
    for kind in range(4):
        outs += [big[n][kind].reshape(args[n].shape) for n in WEIGHT_NAMES]
    return tuple(outs)
```
